```python
import jax, jax.numpy as jnp
from jax import lax
import numpy as np

D_MODEL = 1024
BATCH = 16
SEQ = 2048
DEPTH = 1

RWKV_HEADS = 8
RWKV_HEAD_DIM = 64
D_RWKV = RWKV_HEADS * RWKV_HEAD_DIM
DECAY_LORA = 64
ICLR_LORA = 64
GATE_LORA = 128
GN_EPS = 64e-5
L2_EPS = 1e-12
MLA_HEADS = 8
MLA_NOPE_DIM = 64
MLA_ROPE_DIM = 32
MLA_V_DIM = 64
D_MLA = MLA_HEADS * MLA_V_DIM
Q_LORA_RANK = 768
KV_LORA_RANK = 256
ROPE_THETA = 10000.0
MLA_SCALE = (MLA_NOPE_DIM + MLA_ROPE_DIM) ** -0.5
Q_BLOCK = 128
D_MIX = D_RWKV + D_MLA
RWKV_SPLITS = (D_RWKV, D_RWKV, D_RWKV, DECAY_LORA, DECAY_LORA, ICLR_LORA, ICLR_LORA, GATE_LORA)
MLA_SPLITS = (Q_LORA_RANK, KV_LORA_RANK, MLA_ROPE_DIM)
RWKV_COLS = 3 * D_RWKV + 2 * DECAY_LORA + 2 * ICLR_LORA + GATE_LORA
MLA_COLS = Q_LORA_RANK + KV_LORA_RANK + MLA_ROPE_DIM
D_IN_PROJ = RWKV_COLS + MLA_COLS
D_FF = 2816
CONV_WIDTH = 3
NORM_EPS = 1e-6

kernel_name = 'hymba_rwkv7_mla_convffn_bidir'


def split_cols(z, sizes):
    bounds = np.cumsum(sizes)[:-1].tolist()
    return jnp.split(z, bounds, axis=-1)


def rms_norm(x, g):
    xf = x.astype(jnp.float32)
    y = xf * lax.rsqrt(jnp.mean(xf * xf, axis=-1, keepdims=True) + NORM_EPS)
    return (y * g.astype(jnp.float32)).astype(x.dtype)


def centred_token_shift(z, mu_prev, mu_next):
    z_prev = jnp.pad(z[:, :-1], ((0, 0), (1, 0), (0, 0)))
    z_next = jnp.pad(z[:, 1:], ((0, 0), (0, 1), (0, 0)))
    return z + mu_prev * (z_prev - z) + mu_next * (z_next - z)


def wkv7_scan(r, w, k, v, kk, kka, reverse):
    B, T, H, N = r.shape

    def step(S, inp):
        r_t, w_t, k_t, v_t, kk_t, kka_t = inp
        s_kk = jnp.einsum('bhij,bhj->bhi', S, kk_t)
        S = (S * w_t[:, :, None, :]
             - s_kk[..., None] * kka_t[:, :, None, :]
             + v_t[..., None] * k_t[:, :, None, :])
        return S, jnp.einsum('bhij,bhj->bhi', S, r_t)

    xs = tuple(jnp.swapaxes(t, 0, 1) for t in (r, w, k, v, kk, kka))
    S0 = jnp.zeros((B, H, N, N), jnp.float32)
    _, ys = lax.scan(step, S0, xs, reverse=reverse)
    return jnp.swapaxes(ys, 0, 1)


def rwkv7_decay(wd, w0, w2):
    logit = w0 + jnp.tanh(wd) @ w2
    return jnp.exp(-jnp.exp(-jax.nn.softplus(-logit) - 0.5))


def rwkv7_bidir_mixer(z, mu_prev, mu_next, w0f, w2f, w0b, w2b, a0f, a2f, a0b, a2b,
                      g2, k_k, k_a, r_k, lnx_g, lnx_b):
    B, T, _ = z.shape
    H, N = RWKV_HEADS, RWKV_HEAD_DIM
    f32 = lambda t: t.astype(jnp.float32)
    heads = lambda t: t.reshape(B, T, H, N)
    z = centred_token_shift(f32(z), f32(mu_prev), f32(mu_next))
    r, k, v, wdf, wdb, adf, adb, gd = split_cols(z, RWKV_SPLITS)
    w_f = rwkv7_decay(wdf, f32(w0f), f32(w2f))
    w_b = rwkv7_decay(wdb, f32(w0b), f32(w2b))
    a_f = jax.nn.sigmoid(f32(a0f) + adf @ f32(a2f))
    a_b = jax.nn.sigmoid(f32(a0b) + adb @ f32(a2b))
    g = jax.nn.sigmoid(gd) @ f32(g2)
    kk = heads(k * f32(k_k))
    kk = kk / jnp.maximum(jnp.linalg.norm(kk, axis=-1, keepdims=True), L2_EPS)
    k_f = k * (1.0 + (a_f - 1.0) * f32(k_a))
    k_b = k * (1.0 + (a_b - 1.0) * f32(k_a))
    r_h, v_h = heads(r), heads(v)
    y = (wkv7_scan(r_h, heads(w_f), heads(k_f), v_h, kk, kk * heads(a_f), False)
         + wkv7_scan(r_h, heads(w_b), heads(k_b), v_h, kk, kk * heads(a_b), True))
    mu = jnp.mean(y, axis=-1, keepdims=True)
    var = jnp.mean(jnp.square(y - mu), axis=-1, keepdims=True)
    y = ((y - mu) * lax.rsqrt(var + GN_EPS)).reshape(B, T, D_RWKV) * f32(lnx_g) + f32(lnx_b)
    bonus = jnp.sum(r_h * heads(k_f + k_b) * f32(r_k), axis=-1, keepdims=True) * v_h
    return (y + bonus.reshape(B, T, D_RWKV)) * g


def rope_tables(T):
    inv_freq = jnp.power(ROPE_THETA, -jnp.arange(0, MLA_ROPE_DIM, 2, dtype=jnp.float32) / MLA_ROPE_DIM)
    ang = jnp.arange(T, dtype=jnp.float32)[:, None] * inv_freq[None, :]
    ang = jnp.concatenate([ang, ang], axis=-1)
    return jnp.cos(ang), jnp.sin(ang)


def apply_rope(x, cos, sin):
    x1, x2 = jnp.split(x, 2, axis=-1)
    rot = jnp.concatenate([-x2, x1], axis=-1)
    return x * cos.astype(x.dtype) + rot * sin.astype(x.dtype)


def bidir_block_attention(q_nope, q_rope, k_nope, k_rope, v):
    B, T, H, _ = q_nope.shape
    nb = T // Q_BLOCK
    to_blocks = lambda t: jnp.moveaxis(t.reshape(B, nb, Q_BLOCK, H, t.shape[-1]), 1, 0)

    def one_block(qs):
        qn, qr = qs
        s = (jnp.einsum('bqhd,bkhd->bhqk', qn, k_nope)
             + jnp.einsum('bqhr,bkr->bhqk', qr, k_rope))
        p = jax.nn.softmax(s.astype(jnp.float32) * MLA_SCALE, axis=-1).astype(v.dtype)
        return jnp.einsum('bhqk,bkhd->bqhd', p, v)

    o = lax.map(one_block, (to_blocks(q_nope), to_blocks(q_rope)))
    return jnp.moveaxis(o, 0, 1).reshape(B, T, H, MLA_V_DIM)


def mla_mixer(z, q_norm_g, w_uq, kv_norm_g, w_ukv, out_g, cos, sin):
    B, T, _ = z.shape
    H = MLA_HEADS
    c_q, c_kv, k_rope = split_cols(z, MLA_SPLITS)
    q = (rms_norm(c_q, q_norm_g) @ w_uq).reshape(B, T, H, MLA_NOPE_DIM + MLA_ROPE_DIM)
    q_nope = q[..., :MLA_NOPE_DIM]
    q_rope = apply_rope(q[..., MLA_NOPE_DIM:], cos[:, None, :], sin[:, None, :])
    kv = (rms_norm(c_kv, kv_norm_g) @ w_ukv).reshape(B, T, H, MLA_NOPE_DIM + MLA_V_DIM)
    k_nope, v = kv[..., :MLA_NOPE_DIM], kv[..., MLA_NOPE_DIM:]
    k_rope = apply_rope(k_rope, cos, sin)
    o = bidir_block_attention(q_nope, q_rope, k_nope, k_rope, v)
    return rms_norm(o.reshape(B, T, D_MLA), out_g)


def conv_ffn(n, w_up, conv_w, conv_b, w_down):
    u = n @ w_up
    pad = CONV_WIDTH // 2
    u = lax.conv_general_dilated(u, conv_w[:, None, :].astype(u.dtype), window_strides=(1,),
                                 padding=((pad, pad),), dimension_numbers=('NWC', 'WIO', 'NWC'),
                                 feature_group_count=2 * D_FF) + conv_b
    gate, val = jnp.split(u, 2, axis=-1)
    return (jax.nn.silu(gate) * val) @ w_down


def _fwd_setup_inputs(seed: int = 0) -> dict:
    key = jax.random.key(seed)
    k = jax.random.split(key, 31)
    L = DEPTH
    nrm = lambda kk, shape, scale: scale * jax.random.normal(kk, shape, jnp.float32)
    gain = lambda kk, shape: 1.0 + 0.02 * jax.random.normal(kk, shape, jnp.float32)
    unif = lambda kk, shape, lo, hi: jax.random.uniform(kk, shape, jnp.float32, lo, hi)
    return {
        'x': nrm(k[0], (BATCH, SEQ, D_MODEL), 1.0),
        'ln_mix_g': gain(k[1], (L, D_MODEL)),
        'w_in': nrm(k[2], (L, D_MODEL, D_IN_PROJ), D_MODEL ** -0.5),
        'shift_mu_prev': unif(k[3], (L, RWKV_COLS), 0.0, 0.5),
        'shift_mu_next': unif(k[4], (L, RWKV_COLS), 0.0, 0.5),
        'decay_w0_fwd': unif(k[5], (L, D_RWKV), -6.5, -1.5),
        'decay_w2_fwd': nrm(k[6], (L, DECAY_LORA, D_RWKV), 0.5 * DECAY_LORA ** -0.5),
        'decay_w0_bwd': unif(k[7], (L, D_RWKV), -6.5, -1.5),
        'decay_w2_bwd': nrm(k[8], (L, DECAY_LORA, D_RWKV), 0.5 * DECAY_LORA ** -0.5),
        'iclr_a0_fwd': nrm(k[9], (L, D_RWKV), 0.5),
        'iclr_a2_fwd': nrm(k[10], (L, ICLR_LORA, D_RWKV), 0.5 * ICLR_LORA ** -0.5),
        'iclr_a0_bwd': nrm(k[11], (L, D_RWKV), 0.5),
        'iclr_a2_bwd': nrm(k[12], (L, ICLR_LORA, D_RWKV), 0.5 * ICLR_LORA ** -0.5),
        'gate_g2': nrm(k[13], (L, GATE_LORA, D_RWKV), GATE_LORA ** -0.5),
        'k_k': 0.85 + nrm(k[14], (L, D_RWKV), 0.02),
        'k_a': gain(k[15], (L, D_RWKV)),
        'r_k': nrm(k[16], (L, RWKV_HEADS, RWKV_HEAD_DIM), 0.1),
        'ln_x_g': gain(k[17], (L, D_RWKV)),
        'ln_x_b': nrm(k[18], (L, D_RWKV), 0.02),
        'q_norm_g': gain(k[19], (L, Q_LORA_RANK)),
        'w_uq': nrm(k[20], (L, Q_LORA_RANK, MLA_HEADS * (MLA_NOPE_DIM + MLA_ROPE_DIM)), Q_LORA_RANK ** -0.5),
        'kv_norm_g': gain(k[21], (L, KV_LORA_RANK)),
        'w_ukv': nrm(k[22], (L, KV_LORA_RANK, MLA_HEADS * (MLA_NOPE_DIM + MLA_V_DIM)), KV_LORA_RANK ** -0.5),
        'mla_out_g': gain(k[23], (L, D_MLA)),
        'w_out': nrm(k[24], (L, D_MIX, D_MODEL), D_MIX ** -0.5),
        'ln_ffn_g': gain(k[25], (L, D_MODEL)),
        'w_ffn_up': nrm(k[26], (L, D_MODEL, 2 * D_FF), D_MODEL ** -0.5),
        'ffn_conv_w': nrm(k[27], (L, CONV_WIDTH, 2 * D_FF), CONV_WIDTH ** -0.5),
        'ffn_conv_b': nrm(k[28], (L, 2 * D_FF), 0.02),
        'w_ffn_down': nrm(k[29], (L, D_FF, D_MODEL), D_FF ** -0.5),
        'ln_final_g': gain(k[30], (D_MODEL,)),
    }


def _fwd_reference(x, ln_mix_g, w_in, shift_mu_prev, shift_mu_next, decay_w0_fwd, decay_w2_fwd,
              decay_w0_bwd, decay_w2_bwd, iclr_a0_fwd, iclr_a2_fwd, iclr_a0_bwd, iclr_a2_bwd,
              gate_g2, k_k, k_a, r_k, ln_x_g, ln_x_b, q_norm_g, w_uq, kv_norm_g, w_ukv,
              mla_out_g, w_out, ln_ffn_g, w_ffn_up, ffn_conv_w, ffn_conv_b, w_ffn_down, ln_final_g):
    T = x.shape[1]
    cos, sin = rope_tables(T)
    h = x
    for l in range(DEPTH):
        n = rms_norm(h, ln_mix_g[l])
        z = n @ w_in[l]
        z_rwkv, z_mla = z[..., :RWKV_COLS], z[..., RWKV_COLS:]
        y_rwkv = rwkv7_bidir_mixer(z_rwkv, shift_mu_prev[l], shift_mu_next[l],
                                   decay_w0_fwd[l], decay_w2_fwd[l], decay_w0_bwd[l], decay_w2_bwd[l],
                                   iclr_a0_fwd[l], iclr_a2_fwd[l], iclr_a0_bwd[l], iclr_a2_bwd[l],
                                   gate_g2[l], k_k[l], k_a[l], r_k[l], ln_x_g[l], ln_x_b[l]).astype(h.dtype)
        y_mla = mla_mixer(z_mla, q_norm_g[l], w_uq[l], kv_norm_g[l], w_ukv[l], mla_out_g[l], cos, sin)
        h = h + jnp.concatenate([y_rwkv, y_mla], axis=-1) @ w_out[l]
        h = h + conv_ffn(rms_norm(h, ln_ffn_g[l]), w_ffn_up[l], ffn_conv_w[l], ffn_conv_b[l], w_ffn_down[l])
    return rms_norm(h, ln_final_g)


import jax as _jax
import jax.numpy as _jnp

TWIN_FORMAT = 'train_step'
FWD_PARAMS = ['x', 'ln_mix_g', 'w_in', 'shift_mu_prev', 'shift_mu_next', 'decay_w0_fwd', 'decay_w2_fwd', 'decay_w0_bwd', 'decay_w2_bwd', 'iclr_a0_fwd', 'iclr_a2_fwd', 'iclr_a0_bwd', 'iclr_a2_bwd', 'gate_g2', 'k_k', 'k_a', 'r_k', 'ln_x_g', 'ln_x_b', 'q_norm_g', 'w_uq', 'kv_norm_g', 'w_ukv', 'mla_out_g', 'w_out', 'ln_ffn_g', 'w_ffn_up', 'ffn_conv_w', 'ffn_conv_b', 'w_ffn_down', 'ln_final_g']
TWIN_WEIGHTS = ['ln_mix_g', 'w_in', 'shift_mu_prev', 'shift_mu_next', 'decay_w0_fwd', 'decay_w2_fwd', 'decay_w0_bwd', 'decay_w2_bwd', 'iclr_a0_fwd', 'iclr_a2_fwd', 'iclr_a0_bwd', 'iclr_a2_bwd', 'gate_g2', 'k_k', 'k_a', 'r_k', 'ln_x_g', 'ln_x_b', 'q_norm_g', 'w_uq', 'kv_norm_g', 'w_ukv', 'mla_out_g', 'w_out', 'ln_ffn_g', 'w_ffn_up', 'ffn_conv_w', 'ffn_conv_b', 'w_ffn_down', 'ln_final_g']
TWIN_DIFF_INPUT = 'x'
TWIN_INPUTS = ['x', 'ln_mix_g', 'w_in', 'shift_mu_prev', 'shift_mu_next', 'decay_w0_fwd', 'decay_w2_fwd', 'decay_w0_bwd', 'decay_w2_bwd', 'iclr_a0_fwd', 'iclr_a2_fwd', 'iclr_a0_bwd', 'iclr_a2_bwd', 'gate_g2', 'k_k', 'k_a', 'r_k', 'ln_x_g', 'ln_x_b', 'q_norm_g', 'w_uq', 'kv_norm_g', 'w_ukv', 'mla_out_g', 'w_out', 'ln_ffn_g', 'w_ffn_up', 'ffn_conv_w', 'ffn_conv_b', 'w_ffn_down', 'ln_final_g', 'loss_target', 'm_ln_mix_g', 'm_w_in', 'm_shift_mu_prev', 'm_shift_mu_next', 'm_decay_w0_fwd', 'm_decay_w2_fwd', 'm_decay_w0_bwd', 'm_decay_w2_bwd', 'm_iclr_a0_fwd', 'm_iclr_a2_fwd', 'm_iclr_a0_bwd', 'm_iclr_a2_bwd', 'm_gate_g2', 'm_k_k', 'm_k_a', 'm_r_k', 'm_ln_x_g', 'm_ln_x_b', 'm_q_norm_g', 'm_w_uq', 'm_kv_norm_g', 'm_w_ukv', 'm_mla_out_g', 'm_w_out', 'm_ln_ffn_g', 'm_w_ffn_up', 'm_ffn_conv_w', 'm_ffn_conv_b', 'm_w_ffn_down', 'm_ln_final_g', 'v_ln_mix_g', 'v_w_in', 'v_shift_mu_prev', 'v_shift_mu_next', 'v_decay_w0_fwd', 'v_decay_w2_fwd', 'v_decay_w0_bwd', 'v_decay_w2_bwd', 'v_iclr_a0_fwd', 'v_iclr_a2_fwd', 'v_iclr_a0_bwd', 'v_iclr_a2_bwd', 'v_gate_g2', 'v_k_k', 'v_k_a', 'v_r_k', 'v_ln_x_g', 'v_ln_x_b', 'v_q_norm_g', 'v_w_uq', 'v_kv_norm_g', 'v_w_ukv', 'v_mla_out_g', 'v_w_out', 'v_ln_ffn_g', 'v_w_ffn_up', 'v_ffn_conv_w', 'v_ffn_conv_b', 'v_w_ffn_down', 'v_ln_final_g']
TWIN_OUTPUTS = ['loss', 'grad_x', 'grad_ln_mix_g', 'grad_w_in', 'grad_shift_mu_prev', 'grad_shift_mu_next', 'grad_decay_w0_fwd', 'grad_decay_w2_fwd', 'grad_decay_w0_bwd', 'grad_decay_w2_bwd', 'grad_iclr_a0_fwd', 'grad_iclr_a2_fwd', 'grad_iclr_a0_bwd', 'grad_iclr_a2_bwd', 'grad_gate_g2', 'grad_k_k', 'grad_k_a', 'grad_r_k', 'grad_ln_x_g', 'grad_ln_x_b', 'grad_q_norm_g', 'grad_w_uq', 'grad_kv_norm_g', 'grad_w_ukv', 'grad_mla_out_g', 'grad_w_out', 'grad_ln_ffn_g', 'grad_w_ffn_up', 'grad_ffn_conv_w', 'grad_ffn_conv_b', 'grad_w_ffn_down', 'grad_ln_final_g', 'delta_ln_mix_g', 'delta_w_in', 'delta_shift_mu_prev', 'delta_shift_mu_next', 'delta_decay_w0_fwd', 'delta_decay_w2_fwd', 'delta_decay_w0_bwd', 'delta_decay_w2_bwd', 'delta_iclr_a0_fwd', 'delta_iclr_a2_fwd', 'delta_iclr_a0_bwd', 'delta_iclr_a2_bwd', 'delta_gate_g2', 'delta_k_k', 'delta_k_a', 'delta_r_k', 'delta_ln_x_g', 'delta_ln_x_b', 'delta_q_norm_g', 'delta_w_uq', 'delta_kv_norm_g', 'delta_w_ukv', 'delta_mla_out_g', 'delta_w_out', 'delta_ln_ffn_g', 'delta_w_ffn_up', 'delta_ffn_conv_w', 'delta_ffn_conv_b', 'delta_w_ffn_down', 'delta_ln_final_g', 'new_m_ln_mix_g', 'new_m_w_in', 'new_m_shift_mu_prev', 'new_m_shift_mu_next', 'new_m_decay_w0_fwd', 'new_m_decay_w2_fwd', 'new_m_decay_w0_bwd', 'new_m_decay_w2_bwd', 'new_m_iclr_a0_fwd', 'new_m_iclr_a2_fwd', 'new_m_iclr_a0_bwd', 'new_m_iclr_a2_bwd', 'new_m_gate_g2', 'new_m_k_k', 'new_m_k_a', 'new_m_r_k', 'new_m_ln_x_g', 'new_m_ln_x_b', 'new_m_q_norm_g', 'new_m_w_uq', 'new_m_kv_norm_g', 'new_m_w_ukv', 'new_m_mla_out_g', 'new_m_w_out', 'new_m_ln_ffn_g', 'new_m_w_ffn_up', 'new_m_ffn_conv_w', 'new_m_ffn_conv_b', 'new_m_w_ffn_down', 'new_m_ln_final_g', 'new_v_ln_mix_g', 'new_v_w_in', 'new_v_shift_mu_prev', 'new_v_shift_mu_next', 'new_v_decay_w0_fwd', 'new_v_decay_w2_fwd', 'new_v_decay_w0_bwd', 'new_v_decay_w2_bwd', 'new_v_iclr_a0_fwd', 'new_v_iclr_a2_fwd', 'new_v_iclr_a0_bwd', 'new_v_iclr_a2_bwd', 'new_v_gate_g2', 'new_v_k_k', 'new_v_k_a', 'new_v_r_k', 'new_v_ln_x_g', 'new_v_ln_x_b', 'new_v_q_norm_g', 'new_v_w_uq', 'new_v_kv_norm_g', 'new_v_w_ukv', 'new_v_mla_out_g', 'new_v_w_out', 'new_v_ln_ffn_g', 'new_v_w_ffn_up', 'new_v_ffn_conv_w', 'new_v_ffn_conv_b', 'new_v_w_ffn_down', 'new_v_ln_final_g']
TWIN_LEAF_KINDS = {'loss': 'loss', 'grad_x': 'grad_x', 'grad_ln_mix_g': 'grad_w', 'grad_w_in': 'grad_w', 'grad_shift_mu_prev': 'grad_w', 'grad_shift_mu_next': 'grad_w', 'grad_decay_w0_fwd': 'grad_w', 'grad_decay_w2_fwd': 'grad_w', 'grad_decay_w0_bwd': 'grad_w', 'grad_decay_w2_bwd': 'grad_w', 'grad_iclr_a0_fwd': 'grad_w', 'grad_iclr_a2_fwd': 'grad_w', 'grad_iclr_a0_bwd': 'grad_w', 'grad_iclr_a2_bwd': 'grad_w', 'grad_gate_g2': 'grad_w', 'grad_k_k': 'grad_w', 'grad_k_a': 'grad_w', 'grad_r_k': 'grad_w', 'grad_ln_x_g': 'grad_w', 'grad_ln_x_b': 'grad_w', 'grad_q_norm_g': 'grad_w', 'grad_w_uq': 'grad_w', 'grad_kv_norm_g': 'grad_w', 'grad_w_ukv': 'grad_w', 'grad_mla_out_g': 'grad_w', 'grad_w_out': 'grad_w', 'grad_ln_ffn_g': 'grad_w', 'grad_w_ffn_up': 'grad_w', 'grad_ffn_conv_w': 'grad_w', 'grad_ffn_conv_b': 'grad_w', 'grad_w_ffn_down': 'grad_w', 'grad_ln_final_g': 'grad_w', 'delta_ln_mix_g': 'delta_w', 'delta_w_in': 'delta_w', 'delta_shift_mu_prev': 'delta_w', 'delta_shift_mu_next': 'delta_w', 'delta_decay_w0_fwd': 'delta_w', 'delta_decay_w2_fwd': 'delta_w', 'delta_decay_w0_bwd': 'delta_w', 'delta_decay_w2_bwd': 'delta_w', 'delta_iclr_a0_fwd': 'delta_w', 'delta_iclr_a2_fwd': 'delta_w', 'delta_iclr_a0_bwd': 'delta_w', 'delta_iclr_a2_bwd': 'delta_w', 'delta_gate_g2': 'delta_w', 'delta_k_k': 'delta_w', 'delta_k_a': 'delta_w', 'delta_r_k': 'delta_w', 'delta_ln_x_g': 'delta_w', 'delta_ln_x_b': 'delta_w', 'delta_q_norm_g': 'delta_w', 'delta_w_uq': 'delta_w', 'delta_kv_norm_g': 'delta_w', 'delta_w_ukv': 'delta_w', 'delta_mla_out_g': 'delta_w', 'delta_w_out': 'delta_w', 'delta_ln_ffn_g': 'delta_w', 'delta_w_ffn_up': 'delta_w', 'delta_ffn_conv_w': 'delta_w', 'delta_ffn_conv_b': 'delta_w', 'delta_w_ffn_down': 'delta_w', 'delta_ln_final_g': 'delta_w', 'new_m_ln_mix_g': 'new_m', 'new_m_w_in': 'new_m', 'new_m_shift_mu_prev': 'new_m', 'new_m_shift_mu_next': 'new_m', 'new_m_decay_w0_fwd': 'new_m', 'new_m_decay_w2_fwd': 'new_m', 'new_m_decay_w0_bwd': 'new_m', 'new_m_decay_w2_bwd': 'new_m', 'new_m_iclr_a0_fwd': 'new_m', 'new_m_iclr_a2_fwd': 'new_m', 'new_m_iclr_a0_bwd': 'new_m', 'new_m_iclr_a2_bwd': 'new_m', 'new_m_gate_g2': 'new_m', 'new_m_k_k': 'new_m', 'new_m_k_a': 'new_m', 'new_m_r_k': 'new_m', 'new_m_ln_x_g': 'new_m', 'new_m_ln_x_b': 'new_m', 'new_m_q_norm_g': 'new_m', 'new_m_w_uq': 'new_m', 'new_m_kv_norm_g': 'new_m', 'new_m_w_ukv': 'new_m', 'new_m_mla_out_g': 'new_m', 'new_m_w_out': 'new_m', 'new_m_ln_ffn_g': 'new_m', 'new_m_w_ffn_up': 'new_m', 'new_m_ffn_conv_w': 'new_m', 'new_m_ffn_conv_b': 'new_m', 'new_m_w_ffn_down': 'new_m', 'new_m_ln_final_g': 'new_m', 'new_v_ln_mix_g': 'new_v', 'new_v_w_in': 'new_v', 'new_v_shift_mu_prev': 'new_v', 'new_v_shift_mu_next': 'new_v', 'new_v_decay_w0_fwd': 'new_v', 'new_v_decay_w2_fwd': 'new_v', 'new_v_decay_w0_bwd': 'new_v', 'new_v_decay_w2_bwd': 'new_v', 'new_v_iclr_a0_fwd': 'new_v', 'new_v_iclr_a2_fwd': 'new_v', 'new_v_iclr_a0_bwd': 'new_v', 'new_v_iclr_a2_bwd': 'new_v', 'new_v_gate_g2': 'new_v', 'new_v_k_k': 'new_v', 'new_v_k_a': 'new_v', 'new_v_r_k': 'new_v', 'new_v_ln_x_g': 'new_v', 'new_v_ln_x_b': 'new_v', 'new_v_q_norm_g': 'new_v', 'new_v_w_uq': 'new_v', 'new_v_kv_norm_g': 'new_v', 'new_v_w_ukv': 'new_v', 'new_v_mla_out_g': 'new_v', 'new_v_w_out': 'new_v', 'new_v_ln_ffn_g': 'new_v', 'new_v_w_ffn_up': 'new_v', 'new_v_ffn_conv_w': 'new_v', 'new_v_ffn_conv_b': 'new_v', 'new_v_w_ffn_down': 'new_v', 'new_v_ln_final_g': 'new_v'}


def _forward(args):
    return _fwd_reference(*[args[k] for k in FWD_PARAMS])


def _output_shape():
    out = _jax.eval_shape(lambda: _forward(_fwd_setup_inputs(0)))
    return out.shape, out.dtype

N_MICROBATCH = 1
ADAM_LR = 0.001
ADAM_B1 = 0.9
ADAM_B2 = 0.999
ADAM_EPS = 1e-08
ADAM_WD = 0.01
ADAM_STEP = 10
PER_EXAMPLE_BATCH_AXIS = {'x': 0, 'loss_target': 0}
SHARED_INPUTS = []
_WEIGHT_DTYPES = {'ln_mix_g': _jnp.float32, 'w_in': _jnp.float32, 'shift_mu_prev': _jnp.float32, 'shift_mu_next': _jnp.float32, 'decay_w0_fwd': _jnp.float32, 'decay_w2_fwd': _jnp.float32, 'decay_w0_bwd': _jnp.float32, 'decay_w2_bwd': _jnp.float32, 'iclr_a0_fwd': _jnp.float32, 'iclr_a2_fwd': _jnp.float32, 'iclr_a0_bwd': _jnp.float32, 'iclr_a2_bwd': _jnp.float32, 'gate_g2': _jnp.float32, 'k_k': _jnp.float32, 'k_a': _jnp.float32, 'r_k': _jnp.float32, 'ln_x_g': _jnp.float32, 'ln_x_b': _jnp.float32, 'q_norm_g': _jnp.float32, 'w_uq': _jnp.float32, 'kv_norm_g': _jnp.float32, 'w_ukv': _jnp.float32, 'mla_out_g': _jnp.float32, 'w_out': _jnp.float32, 'ln_ffn_g': _jnp.float32, 'w_ffn_up': _jnp.float32, 'ffn_conv_w': _jnp.float32, 'ffn_conv_b': _jnp.float32, 'w_ffn_down': _jnp.float32, 'ln_final_g': _jnp.float32}
MOMENT_SCALE = {'ln_mix_g': 2.206157e-01, 'w_in': 1.187990e-01, 'shift_mu_prev': 1.404889e-01, 'shift_mu_next': 1.444456e-01, 'decay_w0_fwd': 2.322670e-02, 'decay_w2_fwd': 2.404075e-03, 'decay_w0_bwd': 2.170217e-02, 'decay_w2_bwd': 2.336586e-03, 'iclr_a0_fwd': 2.404488e-02, 'iclr_a2_fwd': 1.712589e-02, 'iclr_a0_bwd': 2.226162e-02, 'iclr_a2_bwd': 1.736656e-02, 'gate_g2': 8.203845e-02, 'k_k': 1.185846e-01, 'k_a': 1.196255e-01, 'r_k': 2.123638e-01, 'ln_x_g': 7.964345e-02, 'ln_x_b': 8.751902e-02, 'q_norm_g': 1.266331e-01, 'w_uq': 1.257498e-01, 'kv_norm_g': 5.103196e-01, 'w_ukv': 1.527512e-01, 'mla_out_g': 1.512220e-01, 'w_out': 1.236725e-01, 'ln_ffn_g': 1.116611e-01, 'w_ffn_up': 4.702513e-02, 'ffn_conv_w': 4.704835e-02, 'ffn_conv_b': 4.589394e-02, 'w_ffn_down': 7.702018e-02, 'ln_final_g': 3.189438e+01}


def _to_microbatches(a, axis):
    t = _jnp.moveaxis(a, axis, 0)
    t = t.reshape((N_MICROBATCH, t.shape[0] // N_MICROBATCH) + t.shape[1:])
    return _jnp.moveaxis(t, 1, axis + 1)


def setup_inputs(seed: int = 0) -> dict:
    inp = _fwd_setup_inputs(seed)
    key = _jax.random.fold_in(_jax.random.key(seed), 7919)
    shape, _ = _output_shape()
    out = dict(inp)
    out["loss_target"] = _jax.random.normal(_jax.random.fold_in(key, 0), shape, _jnp.float32)
    for i, name in enumerate(TWIN_WEIGHTS):
        w = inp[name].astype(_jnp.float32)
        if MOMENT_SCALE is None:
            s = _jnp.sqrt(_jnp.mean(_jnp.square(w)) + 1e-30)
        else:
            s = MOMENT_SCALE[name]
        km, kv = _jax.random.split(_jax.random.fold_in(key, i + 1))
        out[name] = w
        out["m_" + name] = s * _jax.random.normal(km, w.shape, _jnp.float32)
        out["v_" + name] = (s * s) * _jax.random.uniform(kv, w.shape, _jnp.float32, 0.5, 1.5)
    if N_MICROBATCH > 1:
        for name, axis in PER_EXAMPLE_BATCH_AXIS.items():
            out[name] = _to_microbatches(out[name], axis)
    return {'x': out['x'], 'ln_mix_g': out['ln_mix_g'], 'w_in': out['w_in'], 'shift_mu_prev': out['shift_mu_prev'], 'shift_mu_next': out['shift_mu_next'], 'decay_w0_fwd': out['decay_w0_fwd'], 'decay_w2_fwd': out['decay_w2_fwd'], 'decay_w0_bwd': out['decay_w0_bwd'], 'decay_w2_bwd': out['decay_w2_bwd'], 'iclr_a0_fwd': out['iclr_a0_fwd'], 'iclr_a2_fwd': out['iclr_a2_fwd'], 'iclr_a0_bwd': out['iclr_a0_bwd'], 'iclr_a2_bwd': out['iclr_a2_bwd'], 'gate_g2': out['gate_g2'], 'k_k': out['k_k'], 'k_a': out['k_a'], 'r_k': out['r_k'], 'ln_x_g': out['ln_x_g'], 'ln_x_b': out['ln_x_b'], 'q_norm_g': out['q_norm_g'], 'w_uq': out['w_uq'], 'kv_norm_g': out['kv_norm_g'], 'w_ukv': out['w_ukv'], 'mla_out_g': out['mla_out_g'], 'w_out': out['w_out'], 'ln_ffn_g': out['ln_ffn_g'], 'w_ffn_up': out['w_ffn_up'], 'ffn_conv_w': out['ffn_conv_w'], 'ffn_conv_b': out['ffn_conv_b'], 'w_ffn_down': out['w_ffn_down'], 'ln_final_g': out['ln_final_g'], 'loss_target': out['loss_target'], 'm_ln_mix_g': out['m_ln_mix_g'], 'm_w_in': out['m_w_in'], 'm_shift_mu_prev': out['m_shift_mu_prev'], 'm_shift_mu_next': out['m_shift_mu_next'], 'm_decay_w0_fwd': out['m_decay_w0_fwd'], 'm_decay_w2_fwd': out['m_decay_w2_fwd'], 'm_decay_w0_bwd': out['m_decay_w0_bwd'], 'm_decay_w2_bwd': out['m_decay_w2_bwd'], 'm_iclr_a0_fwd': out['m_iclr_a0_fwd'], 'm_iclr_a2_fwd': out['m_iclr_a2_fwd'], 'm_iclr_a0_bwd': out['m_iclr_a0_bwd'], 'm_iclr_a2_bwd': out['m_iclr_a2_bwd'], 'm_gate_g2': out['m_gate_g2'], 'm_k_k': out['m_k_k'], 'm_k_a': out['m_k_a'], 'm_r_k': out['m_r_k'], 'm_ln_x_g': out['m_ln_x_g'], 'm_ln_x_b': out['m_ln_x_b'], 'm_q_norm_g': out['m_q_norm_g'], 'm_w_uq': out['m_w_uq'], 'm_kv_norm_g': out['m_kv_norm_g'], 'm_w_ukv': out['m_w_ukv'], 'm_mla_out_g': out['m_mla_out_g'], 'm_w_out': out['m_w_out'], 'm_ln_ffn_g': out['m_ln_ffn_g'], 'm_w_ffn_up': out['m_w_ffn_up'], 'm_ffn_conv_w': out['m_ffn_conv_w'], 'm_ffn_conv_b': out['m_ffn_conv_b'], 'm_w_ffn_down': out['m_w_ffn_down'], 'm_ln_final_g': out['m_ln_final_g'], 'v_ln_mix_g': out['v_ln_mix_g'], 'v_w_in': out['v_w_in'], 'v_shift_mu_prev': out['v_shift_mu_prev'], 'v_shift_mu_next': out['v_shift_mu_next'], 'v_decay_w0_fwd': out['v_decay_w0_fwd'], 'v_decay_w2_fwd': out['v_decay_w2_fwd'], 'v_decay_w0_bwd': out['v_decay_w0_bwd'], 'v_decay_w2_bwd': out['v_decay_w2_bwd'], 'v_iclr_a0_fwd': out['v_iclr_a0_fwd'], 'v_iclr_a2_fwd': out['v_iclr_a2_fwd'], 'v_iclr_a0_bwd': out['v_iclr_a0_bwd'], 'v_iclr_a2_bwd': out['v_iclr_a2_bwd'], 'v_gate_g2': out['v_gate_g2'], 'v_k_k': out['v_k_k'], 'v_k_a': out['v_k_a'], 'v_r_k': out['v_r_k'], 'v_ln_x_g': out['v_ln_x_g'], 'v_ln_x_b': out['v_ln_x_b'], 'v_q_norm_g': out['v_q_norm_g'], 'v_w_uq': out['v_w_uq'], 'v_kv_norm_g': out['v_kv_norm_g'], 'v_w_ukv': out['v_w_ukv'], 'v_mla_out_g': out['v_mla_out_g'], 'v_w_out': out['v_w_out'], 'v_ln_ffn_g': out['v_ln_ffn_g'], 'v_w_ffn_up': out['v_w_ffn_up'], 'v_ffn_conv_w': out['v_ffn_conv_w'], 'v_ffn_conv_b': out['v_ffn_conv_b'], 'v_w_ffn_down': out['v_w_ffn_down'], 'v_ln_final_g': out['v_ln_final_g']}


def _loss(weights, diff, rest, loss_target):
    with _jax.named_scope("forward"):
        args = {**rest, TWIN_DIFF_INPUT: diff, **{k: w.astype(_WEIGHT_DTYPES[k]) for k, w in weights.items()}}
        y = _forward(args)
    with _jax.named_scope("loss_head"):
        err = _jnp.square(y.astype(_jnp.float32) - loss_target)
        return 0.5 * _jnp.sum(_jnp.mean(err, axis=-1)) if err.ndim else 0.5 * err


def _adamw(w, g, m, v):
    m = ADAM_B1 * m + (1.0 - ADAM_B1) * g
    v = ADAM_B2 * v + (1.0 - ADAM_B2) * _jnp.square(g)
    m_hat = m / (1.0 - ADAM_B1 ** ADAM_STEP)
    v_hat = v / (1.0 - ADAM_B2 ** ADAM_STEP)
    delta = -ADAM_LR * (m_hat / (_jnp.sqrt(v_hat) + ADAM_EPS) + ADAM_WD * w)
    return delta, m, v


def reference(x, ln_mix_g, w_in, shift_mu_prev, shift_mu_next, decay_w0_fwd, decay_w2_fwd, decay_w0_bwd, decay_w2_bwd, iclr_a0_fwd, iclr_a2_fwd, iclr_a0_bwd, iclr_a2_bwd, gate_g2, k_k, k_a, r_k, ln_x_g, ln_x_b, q_norm_g, w_uq, kv_norm_g, w_ukv, mla_out_g, w_out, ln_ffn_g, w_ffn_up, ffn_conv_w, ffn_conv_b, w_ffn_down, ln_final_g, loss_target, m_ln_mix_g, m_w_in, m_shift_mu_prev, m_shift_mu_next, m_decay_w0_fwd, m_decay_w2_fwd, m_decay_w0_bwd, m_decay_w2_bwd, m_iclr_a0_fwd, m_iclr_a2_fwd, m_iclr_a0_bwd, m_iclr_a2_bwd, m_gate_g2, m_k_k, m_k_a, m_r_k, m_ln_x_g, m_ln_x_b, m_q_norm_g, m_w_uq, m_kv_norm_g, m_w_ukv, m_mla_out_g, m_w_out, m_ln_ffn_g, m_w_ffn_up, m_ffn_conv_w, m_ffn_conv_b, m_w_ffn_down, m_ln_final_g, v_ln_mix_g, v_w_in, v_shift_mu_prev, v_shift_mu_next, v_decay_w0_fwd, v_decay_w2_fwd, v_decay_w0_bwd, v_decay_w2_bwd, v_iclr_a0_fwd, v_iclr_a2_fwd, v_iclr_a0_bwd, v_iclr_a2_bwd, v_gate_g2, v_k_k, v_k_a, v_r_k, v_ln_x_g, v_ln_x_b, v_q_norm_g, v_w_uq, v_kv_norm_g, v_w_ukv, v_mla_out_g, v_w_out, v_ln_ffn_g, v_w_ffn_up, v_ffn_conv_w, v_ffn_conv_b, v_w_ffn_down, v_ln_final_g):
    given = dict(x=x, ln_mix_g=ln_mix_g, w_in=w_in, shift_mu_prev=shift_mu_prev, shift_mu_next=shift_mu_next, decay_w0_fwd=decay_w0_fwd, decay_w2_fwd=decay_w2_fwd, decay_w0_bwd=decay_w0_bwd, decay_w2_bwd=decay_w2_bwd, iclr_a0_fwd=iclr_a0_fwd, iclr_a2_fwd=iclr_a2_fwd, iclr_a0_bwd=iclr_a0_bwd, iclr_a2_bwd=iclr_a2_bwd, gate_g2=gate_g2, k_k=k_k, k_a=k_a, r_k=r_k, ln_x_g=ln_x_g, ln_x_b=ln_x_b, q_norm_g=q_norm_g, w_uq=w_uq, kv_norm_g=kv_norm_g, w_ukv=w_ukv, mla_out_g=mla_out_g, w_out=w_out, ln_ffn_g=ln_ffn_g, w_ffn_up=w_ffn_up, ffn_conv_w=ffn_conv_w, ffn_conv_b=ffn_conv_b, w_ffn_down=w_ffn_down, ln_final_g=ln_final_g, loss_target=loss_target, m_ln_mix_g=m_ln_mix_g, m_w_in=m_w_in, m_shift_mu_prev=m_shift_mu_prev, m_shift_mu_next=m_shift_mu_next, m_decay_w0_fwd=m_decay_w0_fwd, m_decay_w2_fwd=m_decay_w2_fwd, m_decay_w0_bwd=m_decay_w0_bwd, m_decay_w2_bwd=m_decay_w2_bwd, m_iclr_a0_fwd=m_iclr_a0_fwd, m_iclr_a2_fwd=m_iclr_a2_fwd, m_iclr_a0_bwd=m_iclr_a0_bwd, m_iclr_a2_bwd=m_iclr_a2_bwd, m_gate_g2=m_gate_g2, m_k_k=m_k_k, m_k_a=m_k_a, m_r_k=m_r_k, m_ln_x_g=m_ln_x_g, m_ln_x_b=m_ln_x_b, m_q_norm_g=m_q_norm_g, m_w_uq=m_w_uq, m_kv_norm_g=m_kv_norm_g, m_w_ukv=m_w_ukv, m_mla_out_g=m_mla_out_g, m_w_out=m_w_out, m_ln_ffn_g=m_ln_ffn_g, m_w_ffn_up=m_w_ffn_up, m_ffn_conv_w=m_ffn_conv_w, m_ffn_conv_b=m_ffn_conv_b, m_w_ffn_down=m_w_ffn_down, m_ln_final_g=m_ln_final_g, v_ln_mix_g=v_ln_mix_g, v_w_in=v_w_in, v_shift_mu_prev=v_shift_mu_prev, v_shift_mu_next=v_shift_mu_next, v_decay_w0_fwd=v_decay_w0_fwd, v_decay_w2_fwd=v_decay_w2_fwd, v_decay_w0_bwd=v_decay_w0_bwd, v_decay_w2_bwd=v_decay_w2_bwd, v_iclr_a0_fwd=v_iclr_a0_fwd, v_iclr_a2_fwd=v_iclr_a2_fwd, v_iclr_a0_bwd=v_iclr_a0_bwd, v_iclr_a2_bwd=v_iclr_a2_bwd, v_gate_g2=v_gate_g2, v_k_k=v_k_k, v_k_a=v_k_a, v_r_k=v_r_k, v_ln_x_g=v_ln_x_g, v_ln_x_b=v_ln_x_b, v_q_norm_g=v_q_norm_g, v_w_uq=v_w_uq, v_kv_norm_g=v_kv_norm_g, v_w_ukv=v_w_ukv, v_mla_out_g=v_mla_out_g, v_w_out=v_w_out, v_ln_ffn_g=v_ln_ffn_g, v_w_ffn_up=v_w_ffn_up, v_ffn_conv_w=v_ffn_conv_w, v_ffn_conv_b=v_ffn_conv_b, v_w_ffn_down=v_w_ffn_down, v_ln_final_g=v_ln_final_g)
    weights = {n: given[n] for n in TWIN_WEIGHTS}
    shared = {n: given[n] for n in SHARED_INPUTS}
    per_example = {n: given[n] for n in ['x']}
    grad_fn = _jax.value_and_grad(_loss, argnums=(0, 1))

    def one_microbatch(ex, loss_target):
        ex = dict(ex)
        diff = ex.pop(TWIN_DIFF_INPUT)
        return grad_fn(weights, diff, {**shared, **ex}, loss_target)

    if N_MICROBATCH == 1:
        loss, (grad_w, grad_x) = one_microbatch(per_example, given["loss_target"])
    else:
        def body(carry, xs):
            loss_sum, grad_sum = carry
            l_k, (gw_k, gx_k) = one_microbatch(xs[0], xs[1])
            with _jax.named_scope("update"):
                return (loss_sum + l_k, _jax.tree.map(_jnp.add, grad_sum, gw_k)), gx_k

        init = (_jnp.zeros((), _jnp.float32), _jax.tree.map(_jnp.zeros_like, weights))
        (loss, grad_w), grad_x = _jax.lax.scan(body, init, (per_example, given["loss_target"]))
    with _jax.named_scope("update"):
        delta_w, new_m, new_v = {}, {}, {}
        for n in TWIN_WEIGHTS:
            delta_w[n], new_m[n], new_v[n] = _adamw(weights[n], grad_w[n], given["m_" + n], given["v_" + n])
    return (loss, grad_x, *[grad_w[n] for n in TWIN_WEIGHTS], *[delta_w[n] for n in TWIN_WEIGHTS],
            *[new_m[n] for n in TWIN_WEIGHTS], *[new_v[n] for n in TWIN_WEIGHTS])
```

```python
import functools
import math

import jax
import jax.numpy as jnp
from jax import lax
from jax.experimental import pallas as pl
from jax.experimental.pallas import tpu as pltpu

F32, BF16 = jnp.float32, jnp.bfloat16
HIGHEST = lax.Precision.HIGHEST
MESH = pl.DeviceIdType.MESH
ANY = pl.BlockSpec(memory_space=pl.ANY)
VMEM = pl.BlockSpec(memory_space=pltpu.VMEM)
BS = pl.BlockSpec
SDS = jax.ShapeDtypeStruct

NORM_EPS = 1e-6
GN_EPS = 64e-5
L2_EPS = 1e-12
HEADS = 8
HEAD_DIM = 64
D_RWKV = HEADS * HEAD_DIM
ROPE_DIM = 32
ROPE_THETA = 10000.0
MLA_SCALE = (64 + ROPE_DIM) ** -0.5
Q_RANK, KV_RANK = 768, 256
RWKV_COLS = 1920
MLA_PAD_COLS = Q_RANK + KV_RANK + 256
D_FF = 2816
ADAM_LR, ADAM_B1, ADAM_B2, ADAM_EPS, ADAM_WD, ADAM_STEP = 0.001, 0.9, 0.999, 1e-08, 0.01, 10

V7X_LANES = 128
V7X_VMEM_LIMIT = 48 * 1024 * 1024
SCAN_CHUNK = 32
N_CHIPS = 4


def _cp(*sem):
    return pltpu.CompilerParams(dimension_semantics=sem, vmem_limit_bytes=V7X_VMEM_LIMIT)


def _tile(n, cands=(512, 640, 384, 256, 128)):
    for c in cands:
        if n % c == 0:
            return c
    return n


def _row_tile(n, cap=256):
    best = n
    for t in range(8, cap + 1, 8):
        if n % t == 0:
            best = t
    return best if best <= cap or n <= cap else n


def _rms(x, g):
    ms = jnp.mean(x * x, axis=-1, keepdims=True)
    return x * lax.rsqrt(ms + NORM_EPS) * g


@jax.custom_vjp
def _bdot(x, w):
    return jnp.dot(x.astype(BF16), w.astype(BF16), preferred_element_type=F32)


def _bdot_fwd(x, w):
    return _bdot(x, w), (x, w)


def _bdot_bwd(res, ct):
    x, w = res
    c = ct.astype(BF16)
    dx = lax.dot_general(c, w.astype(BF16), (((1,), (1,)), ((), ())), preferred_element_type=F32)
    dw = lax.dot_general(x.astype(BF16), c, (((0,), (0,)), ((), ())), preferred_element_type=F32)
    return dx.astype(x.dtype), dw.astype(w.dtype)


_bdot.defvjp(_bdot_fwd, _bdot_bwd)


def _headsum(x, ones_bd):
    return jnp.dot(x, ones_bd, precision=HIGHEST, preferred_element_type=F32)


def _prep_fn(zs, w0, w2, a0, a2, g2, k_k, k_a, ones_bd):
    k = zs[:, 512:1024]
    wd = zs[:, 1536:1664]
    ad = zs[:, 1664:1792]
    gd = zs[:, 1792:1920]
    logit = w0 + _bdot(jnp.tanh(wd), w2)
    w = jnp.exp(-math.exp(-0.5) * jax.nn.sigmoid(logit))
    a = jax.nn.sigmoid(a0 + _bdot(ad, a2))
    g = _bdot(jax.nn.sigmoid(gd), g2)
    kkr = k * k_k
    nrm = jnp.sqrt(_headsum(kkr * kkr, ones_bd))
    kk = kkr / jnp.maximum(nrm, L2_EPS)
    a_f, a_b = a[:, :512], a[:, 512:]
    kf = k * (1.0 + (a_f - 1.0) * k_a)
    kb = k * (1.0 + (a_b - 1.0) * k_a)
    return w[:, :512], w[:, 512:], kf, kb, kk, kk * a_f, kk * a_b, g


def _post_fn(y_t, r, kf, kb, v, g, r_k, ln_g, ln_b, ones_bd):
    y = y_t.T
    mu = _headsum(y, ones_bd) * (1.0 / HEAD_DIM)
    yc = y - mu
    var = _headsum(yc * yc, ones_bd) * (1.0 / HEAD_DIM)
    yn = yc * lax.rsqrt(var + GN_EPS) * ln_g + ln_b
    bonus = _headsum(r * (kf + kb) * r_k, ones_bd) * v
    return (yn + bonus) * g


def _cat8(x):
    return jnp.concatenate([x] * HEADS, axis=1)


def _mla_fn(zm, cs, sn, gq, gkv, wq, wqr, wk, wv):
    cq = zm[:, :Q_RANK]
    ckv = zm[:, Q_RANK:Q_RANK + KV_RANK]
    kr = zm[:, Q_RANK + KV_RANK:Q_RANK + KV_RANK + 128]
    krr = zm[:, Q_RANK + KV_RANK + 128:]
    cqn = _rms(cq, gq)
    ckvn = _rms(ckv, gkv)
    q = _bdot(cqn, wq) * _cat8(cs) + _bdot(cqn, wqr) * _cat8(sn)
    kro = kr * cs + krr * sn
    kfull = _bdot(ckvn, wk) + _cat8(kro)
    v = _bdot(ckvn, wv)
    return q, kfull, v


def _adamw_math(w, g, m, v):
    m2 = ADAM_B1 * m + (1.0 - ADAM_B1) * g
    v2 = ADAM_B2 * v + (1.0 - ADAM_B2) * (g * g)
    m_hat = m2 / (1.0 - ADAM_B1 ** ADAM_STEP)
    v_hat = v2 / (1.0 - ADAM_B2 ** ADAM_STEP)
    delta = -ADAM_LR * (m_hat / (jnp.sqrt(v_hat) + ADAM_EPS) + ADAM_WD * w)
    return delta, m2, v2


_DIMS = {"nn": (((1,), (0,)), ((), ())), "nt": (((1,), (1,)), ((), ())), "tn": (((0,), (0,)), ((), ()))}


def _mm(a, b, mode, name, out_dtype=F32, add=None):
    if mode == "nn":
        (m, k), (_, n) = a.shape, b.shape
    elif mode == "nt":
        (m, k), (n, _) = a.shape, b.shape
    else:
        (k, m), (_, n) = a.shape, b.shape
    tm, tn, tk = _tile(m), _tile(n), _tile(k)
    nk = k // tk

    def body(a_ref, b_ref, *rest):
        if add is None:
            o_ref, acc_ref = rest
        else:
            add_ref, o_ref, acc_ref = rest
        kk = pl.program_id(2)

        @pl.when(kk == 0)
        def _():
            acc_ref[...] = jnp.zeros_like(acc_ref)

        acc_ref[...] += lax.dot_general(
            a_ref[...].astype(BF16), b_ref[...].astype(BF16), _DIMS[mode], preferred_element_type=F32
        )

        @pl.when(kk == nk - 1)
        def _():
            r = acc_ref[...]
            if add is not None:
                r = r + add_ref[...]
            o_ref[...] = r.astype(out_dtype)

    a_spec = BS((tk, tm), lambda i, j, kk: (kk, i)) if mode == "tn" else BS((tm, tk), lambda i, j, kk: (i, kk))
    b_spec = BS((tn, tk), lambda i, j, kk: (j, kk)) if mode == "nt" else BS((tk, tn), lambda i, j, kk: (kk, j))
    o_spec = BS((tm, tn), lambda i, j, kk: (i, j))
    ins, specs = [a, b], [a_spec, b_spec]
    if add is not None:
        ins.append(add)
        specs.append(o_spec)
    return pl.pallas_call(
        body, grid=(m // tm, n // tn, nk), in_specs=specs, out_specs=o_spec, out_shape=SDS((m, n), out_dtype),
        scratch_shapes=[pltpu.VMEM((tm, tn), F32)], compiler_params=_cp("parallel", "parallel", "arbitrary"), name=name,
    )(*ins)


def _rms_fwd(x, g, name):
    m, d = x.shape
    tr = _tile(m)

    def body(x_ref, g_ref, o_ref):
        o_ref[...] = _rms(x_ref[...], g_ref[...]).astype(BF16)

    return pl.pallas_call(
        body, grid=(m // tr,), in_specs=[BS((tr, d), lambda i: (i, 0)), BS((1, d), lambda i: (0, 0))],
        out_specs=BS((tr, d), lambda i: (i, 0)), out_shape=SDS((m, d), BF16), compiler_params=_cp("parallel"), name=name,
    )(x, g)


def _rms_bwd(x, g, dy, name, dres=None, dy_block=0):
    m, d = x.shape
    tr = _row_tile(m)

    def body(x_ref, g_ref, dy_ref, *rest):
        if dres is None:
            dx_ref, dg_ref = rest
        else:
            dres_ref, dx_ref, dg_ref = rest
        _, vjp = jax.vjp(_rms, x_ref[...], g_ref[...])
        dx, dg = vjp(dy_ref[...])
        if dres is not None:
            dx = dx + dres_ref[...]
        dx_ref[...] = dx

        @pl.when(pl.program_id(0) == 0)
        def _():
            dg_ref[...] = jnp.zeros_like(dg_ref)

        dg_ref[...] += dg

    row = BS((tr, d), lambda i: (i, 0))
    vec = BS((1, d), lambda i: (0, 0))
    ins, specs = [x, g, dy], [row, vec, BS((tr, d), lambda i: (i, dy_block))]
    if dres is not None:
        ins.append(dres)
        specs.append(row)
    return pl.pallas_call(
        body, grid=(m // tr,), in_specs=specs, out_specs=[row, vec], out_shape=[SDS((m, d), F32), SDS((1, d), F32)],
        compiler_params=_cp("arbitrary"), name=name,
    )(*ins)


def _final(h, g, tgt):
    m, d = h.shape
    tr = _row_tile(m)

    def loss_fn(hh, gg, tt):
        e = _rms(hh, gg) - tt
        return 0.5 * jnp.sum(e * e) * (1.0 / d)

    def body(h_ref, g_ref, t_ref, l_ref, dh_ref, dg_ref):
        val, (dh, dg) = jax.value_and_grad(loss_fn, argnums=(0, 1))(h_ref[...], g_ref[...], t_ref[...])
        dh_ref[...] = dh

        @pl.when(pl.program_id(0) == 0)
        def _():
            dg_ref[...] = jnp.zeros_like(dg_ref)
            l_ref[...] = jnp.zeros_like(l_ref)

        dg_ref[...] += dg
        l_ref[...] += jnp.full(l_ref.shape, val, F32)

    row = BS((tr, d), lambda i: (i, 0))
    vec = BS((1, d), lambda i: (0, 0))
    return pl.pallas_call(
        body, grid=(m // tr,), in_specs=[row, vec, row], out_specs=[BS((8, 128), lambda i: (0, 0)), row, vec],
        out_shape=[SDS((8, 128), F32), SDS((m, d), F32), SDS((1, d), F32)], compiler_params=_cp("arbitrary"), name="final_loss",
    )(h, g, tgt)


def _prev_next(z, t):
    row = lax.broadcasted_iota(jnp.int32, z.shape, 0)
    zp = jnp.where(row == 0, 0.0, pltpu.roll(z, 1, axis=0))
    zn = jnp.where(row == t - 1, 0.0, pltpu.roll(z, t - 1, axis=0))
    return zp, zn


def _shift_fwd(z3, mu_p, mu_n):
    b, t, c = z3.shape
    nc = c // 128

    def body(z_ref, mp_ref, mn_ref, o_ref):
        z = z_ref[0]
        zp, zn = _prev_next(z, t)
        o_ref[0] = z + mp_ref[...] * (zp - z) + mn_ref[...] * (zn - z)

    blk = BS((1, t, 128), lambda i, j: (i, 0, j))
    vec = BS((1, 128), lambda i, j: (0, j))
    return pl.pallas_call(
        body, grid=(b, nc), in_specs=[blk, vec, vec], out_specs=blk, out_shape=SDS((b, t, c), F32),
        compiler_params=_cp("parallel", "parallel"), name="shift_fwd",
    )(z3, mu_p, mu_n)


def _shift_bwd(dzs3, z3, mu_p, mu_n):
    b, t, c = z3.shape
    nc = c // 128

    def body(d_ref, z_ref, mp_ref, mn_ref, dz_ref, dmp_ref, dmn_ref):
        d, z = d_ref[0], z_ref[0]
        mp, mn = mp_ref[...], mn_ref[...]
        zp, zn = _prev_next(z, t)
        _, dp_next = _prev_next(d * mp, t)
        dn_prev, _ = _prev_next(d * mn, t)
        dz_ref[0] = d * (1.0 - mp - mn) + dp_next + dn_prev

        @pl.when(pl.program_id(1) == 0)
        def _():
            dmp_ref[...] = jnp.zeros_like(dmp_ref)
            dmn_ref[...] = jnp.zeros_like(dmn_ref)

        dmp_ref[...] += jnp.sum(d * (zp - z), axis=0, keepdims=True)
        dmn_ref[...] += jnp.sum(d * (zn - z), axis=0, keepdims=True)

    blk = BS((1, t, 128), lambda j, i: (i, 0, j))
    vec = BS((1, 128), lambda j, i: (0, j))
    return pl.pallas_call(
        body, grid=(nc, b), in_specs=[blk, blk, vec, vec], out_specs=[blk, vec, vec],
        out_shape=[SDS((b, t, c), F32), SDS((1, c), F32), SDS((1, c), F32)],
        compiler_params=_cp("parallel", "arbitrary"), name="shift_bwd",
    )(dzs3, z3, mu_p, mu_n)


def _const(shape):
    nd = len(shape)
    return BS(shape, lambda i: (0,) * nd)


def _prep_fwd(zs, p, b, t):
    m = zs.shape[0]
    tr = 256
    per = t // tr
    params = [p["w0"], p["w2"], p["a0"], p["a2"], p["g2"], p["k_k"], p["k_a"], p["ones_bd"]]

    def body(zs_ref, w0, w2, a0, a2, g2, kk_, ka_, bd, wf, wb, kf, kb, kk, kaf, kab, g, vt):
        z = zs_ref[...]
        outs = _prep_fn(z, w0[...], w2[...], a0[...], a2[...], g2[...], kk_[...], ka_[...], bd[...])
        for ref, val in zip((wf, wb, kf, kb, kk, kaf, kab, g), outs):
            ref[...] = val
        vt[0] = z[:, 1024:1536].T

    row = BS((tr, 512), lambda i: (i, 0))
    return pl.pallas_call(
        body, grid=(m // tr,), in_specs=[BS((tr, RWKV_COLS), lambda i: (i, 0))] + [_const(q.shape) for q in params],
        out_specs=[row] * 8 + [BS((1, 512, tr), lambda i: (i // per, 0, i % per))],
        out_shape=[SDS((m, 512), F32)] * 8 + [SDS((b, 512, t), F32)], compiler_params=_cp("parallel"), name="rwkv_prep_fwd",
    )(zs, *params)


def _prep_bwd(zs, p, b, t, ct_rows, dvt_f, dvt_b):
    m = zs.shape[0]
    tr = 128
    per = t // tr
    params = [p["w0"], p["w2"], p["a0"], p["a2"], p["g2"], p["k_k"], p["k_a"]]
    names = ["dwf", "dwb", "dkf", "dkb", "dkk_f", "dkk_b", "dkaf", "dkab", "dr_f", "dr_b", "dr_p", "dk_p", "dv_p", "dg"]
    rows = [ct_rows[n] for n in names]

    def body(zs_ref, w0, w2, a0, a2, g2, kk_, ka_, bd, *rest):
        c = {n: r[...] for n, r in zip(names, rest[:len(names)])}
        vtf, vtb = rest[len(names)], rest[len(names) + 1]
        outs = rest[len(names) + 2:]
        dzs_ref, grads = outs[0], outs[1:]
        ones_bd = bd[...]
        _, vjp = jax.vjp(
            lambda *q: _prep_fn(*q, ones_bd), zs_ref[...], w0[...], w2[...], a0[...], a2[...], g2[...], kk_[...], ka_[...]
        )
        cts = (c["dwf"], c["dwb"], c["dkf"] + c["dk_p"], c["dkb"] + c["dk_p"], c["dkk_f"] + c["dkk_b"], c["dkaf"], c["dkab"], c["dg"])
        dzs, *dparams = vjp(cts)
        dr = c["dr_f"] + c["dr_b"] + c["dr_p"]
        dv = (vtf[0] + vtb[0]).T + c["dv_p"]
        dzs_ref[:, 0:512] = dzs[:, 0:512] + dr
        dzs_ref[:, 512:1024] = dzs[:, 512:1024]
        dzs_ref[:, 1024:1536] = dzs[:, 1024:1536] + dv
        dzs_ref[:, 1536:1920] = dzs[:, 1536:1920]

        @pl.when(pl.program_id(0) == 0)
        def _():
            for gr in grads:
                gr[...] = jnp.zeros_like(gr)

        for gr, val in zip(grads, dparams):
            gr[...] += val

    row = BS((tr, 512), lambda i: (i, 0))
    vt_spec = BS((1, 512, tr), lambda i: (i // per, 0, i % per))
    return pl.pallas_call(
        body, grid=(m // tr,),
        in_specs=[BS((tr, RWKV_COLS), lambda i: (i, 0))] + [_const(q.shape) for q in params] + [_const(p["ones_bd"].shape)]
        + [row] * len(names) + [vt_spec, vt_spec],
        out_specs=[BS((tr, RWKV_COLS), lambda i: (i, 0))] + [_const(q.shape) for q in params],
        out_shape=[SDS((m, RWKV_COLS), F32)] + [SDS(q.shape, F32) for q in params],
        compiler_params=_cp("arbitrary"), name="rwkv_prep_bwd",
    )(zs, *params, p["ones_bd"], *rows, dvt_f, dvt_b)


def _post_specs(tr, per):
    yt = BS((1, 512, tr), lambda i: (i // per, 0, i % per))
    r = BS((tr, 512), lambda i: (i, 0))
    v = BS((tr, 512), lambda i: (i, 2))
    row = BS((tr, 512), lambda i: (i, 0))
    return yt, r, v, row


def _post_fwd(yt_f, yt_b, zs, kf, kb, g, p, t):
    m = zs.shape[0]
    tr = 256
    yt, r, v, row = _post_specs(tr, t // tr)
    vecs = [p["r_k"], p["ln_x_g"], p["ln_x_b"], p["ones_bd"]]

    def body(ytf, ytb, r_ref, v_ref, kf_ref, kb_ref, g_ref, rk, lg, lb, bd, o_ref):
        o_ref[...] = _post_fn(
            ytf[0] + ytb[0], r_ref[...], kf_ref[...], kb_ref[...], v_ref[...], g_ref[...], rk[...], lg[...], lb[...], bd[...]
        ).astype(BF16)

    return pl.pallas_call(
        body, grid=(m // tr,), in_specs=[yt, yt, r, v, row, row, row] + [_const(q.shape) for q in vecs],
        out_specs=row, out_shape=SDS((m, 512), BF16), compiler_params=_cp("parallel"), name="rwkv_post_fwd",
    )(yt_f, yt_b, zs, zs, kf, kb, g, *vecs)


def _post_bwd(yt_f, yt_b, zs, kf, kb, g, p, b, t, dymix):
    m = zs.shape[0]
    tr = 128
    yt, r, v, row = _post_specs(tr, t // tr)
    vecs = [p["r_k"], p["ln_x_g"], p["ln_x_b"]]

    def body(ytf, ytb, r_ref, v_ref, kf_ref, kb_ref, g_ref, rk, lg, lb, bd, dy_ref, dyt, dr, dk, dv, dg, drk, dlg, dlb):
        ones_bd = bd[...]
        _, vjp = jax.vjp(
            lambda *q: _post_fn(*q, ones_bd),
            ytf[0] + ytb[0], r_ref[...], kf_ref[...], kb_ref[...], v_ref[...], g_ref[...], rk[...], lg[...], lb[...],
        )
        c_yt, c_r, c_kf, _, c_v, c_g, c_rk, c_lg, c_lb = vjp(dy_ref[...])
        dyt[0] = c_yt
        dr[...] = c_r
        dk[...] = c_kf
        dv[...] = c_v
        dg[...] = c_g

        @pl.when(pl.program_id(0) == 0)
        def _():
            for ref in (drk, dlg, dlb):
                ref[...] = jnp.zeros_like(ref)

        drk[...] += c_rk
        dlg[...] += c_lg
        dlb[...] += c_lb

    vec = _const((1, 512))
    return pl.pallas_call(
        body, grid=(m // tr,),
        in_specs=[yt, yt, r, v, row, row, row] + [_const(q.shape) for q in vecs] + [_const(p["ones_bd"].shape), row],
        out_specs=[yt, row, row, row, row, vec, vec, vec],
        out_shape=[SDS((b, 512, t), F32)] + [SDS((m, 512), F32)] * 4 + [SDS((1, 512), F32)] * 3,
        compiler_params=_cp("arbitrary"), name="rwkv_post_bwd",
    )(yt_f, yt_b, zs, zs, kf, kb, g, *vecs, p["ones_bd"], dymix)


def _seg_lane_sum(x, lo):
    s_lo = jnp.sum(jnp.where(lo, x, 0.0), axis=1, keepdims=True)
    s_hi = jnp.sum(jnp.where(lo, 0.0, x), axis=1, keepdims=True)
    return jnp.where(lo, s_lo, s_hi), s_lo, s_hi


def _column(tile_ref, b, p, sel, lo):
    c_lo = jnp.sum(jnp.where(sel, tile_ref[b, 128 * p:128 * p + 64, :], 0.0), axis=1, keepdims=True)
    c_hi = jnp.sum(jnp.where(sel, tile_ref[b, 128 * p + 64:128 * p + 128, :], 0.0), axis=1, keepdims=True)
    return jnp.where(lo, c_lo, c_hi)


def _scan_specs(b, t, down):
    nc = t // SCAN_CHUNK
    per = 128 // SCAN_CHUNK
    ci = (lambda c: nc - 1 - c) if down else (lambda c: c)
    rows = BS((b, SCAN_CHUNK, 1, 512), lambda c: (0, ci(c), 0, 0))
    tile = BS((b, 512, 128), lambda c: (0, 0, ci(c) // per))
    hist = BS((SCAN_CHUNK, b * 4, 64, 128), lambda c: (ci(c), 0, 0, 0))
    return nc, per, ci, rows, tile, hist


def _scan_fwd(r, w, k, kk, kka, vt, rev, name):
    b, t = r.shape[:2]
    nc, per, ci, rows, tile, hist = _scan_specs(b, t, rev)

    def body(r_ref, w_ref, k_ref, kk_ref, ka_ref, vt_ref, yt_ref, h_ref, s_ref):
        c = pl.program_id(0)
        sub = ci(c) % per

        @pl.when(c == 0)
        def _():
            s_ref[...] = jnp.zeros_like(s_ref)

        @pl.when(sub == (per - 1 if rev else 0))
        def _():
            yt_ref[...] = jnp.zeros_like(yt_ref)

        lane = lax.broadcasted_iota(jnp.int32, (64, 128), 1)
        lo = lane < 64

        def step(tt, carry):
            tl = SCAN_CHUNK - 1 - tt if rev else tt
            sel = lane == sub * SCAN_CHUNK + tl
            for bi in range(b):
                for p in range(4):
                    pr = bi * 4 + p
                    cols = slice(128 * p, 128 * p + 128)
                    at = lambda ref: ref[bi, tl, :, cols]
                    s_prev = s_ref[pr]
                    h_ref[tl, pr] = s_prev
                    vcol = _column(vt_ref, bi, p, sel, lo)
                    skk, _, _ = _seg_lane_sum(s_prev * at(kk_ref), lo)
                    s_new = s_prev * at(w_ref) - skk * at(ka_ref) + vcol * at(k_ref)
                    s_ref[pr] = s_new
                    _, y_lo, y_hi = _seg_lane_sum(s_new * at(r_ref), lo)
                    ra = slice(128 * p, 128 * p + 64)
                    rb = slice(128 * p + 64, 128 * p + 128)
                    yt_ref[bi, ra, :] = jnp.where(sel, y_lo, yt_ref[bi, ra, :])
                    yt_ref[bi, rb, :] = jnp.where(sel, y_hi, yt_ref[bi, rb, :])
            return carry

        lax.fori_loop(0, SCAN_CHUNK, step, 0)

    return pl.pallas_call(
        body, grid=(nc,), in_specs=[rows] * 5 + [tile], out_specs=[tile, hist],
        out_shape=[SDS((b, 512, t), F32), SDS((t, b * 4, 64, 128), F32)],
        scratch_shapes=[pltpu.VMEM((b * 4, 64, 128), F32)], compiler_params=_cp("arbitrary"), name=name,
    )(r, w, k, kk, kka, vt)


def _scan_bwd(r, w, k, kk, kka, vt, dyt, hist_arr, rev, name):
    b, t = r.shape[:2]
    down = not rev
    nc, per, ci, rows, tile, hist = _scan_specs(b, t, down)

    def body(r_ref, w_ref, k_ref, kk_ref, ka_ref, vt_ref, dyt_ref, h_ref, dr_ref, dw_ref, dk_ref, dkk_ref, dka_ref, dvt_ref, ds_ref):
        c = pl.program_id(0)
        sub = ci(c) % per

        @pl.when(c == 0)
        def _():
            ds_ref[...] = jnp.zeros_like(ds_ref)

        @pl.when(sub == (per - 1 if down else 0))
        def _():
            dvt_ref[...] = jnp.zeros_like(dvt_ref)

        lane = lax.broadcasted_iota(jnp.int32, (64, 128), 1)
        lo = lane < 64
        colsum = lambda x: jnp.sum(x, axis=0, keepdims=True)

        def step(tt, carry):
            tl = SCAN_CHUNK - 1 - tt if down else tt
            sel = lane == sub * SCAN_CHUNK + tl
            for bi in range(b):
                for p in range(4):
                    pr = bi * 4 + p
                    cols = slice(128 * p, 128 * p + 128)
                    at = lambda ref: ref[bi, tl, :, cols]
                    kk_r, w_r, ka_r, k_r, r_r = at(kk_ref), at(w_ref), at(ka_ref), at(k_ref), at(r_ref)
                    s_prev = h_ref[tl, pr]
                    vcol = _column(vt_ref, bi, p, sel, lo)
                    dycol = _column(dyt_ref, bi, p, sel, lo)
                    skk, _, _ = _seg_lane_sum(s_prev * kk_r, lo)
                    s_new = s_prev * w_r - skk * ka_r + vcol * k_r
                    ds = ds_ref[pr] + dycol * r_r
                    dr_ref[bi, tl, :, cols] = colsum(s_new * dycol)
                    _, dv_lo, dv_hi = _seg_lane_sum(ds * k_r, lo)
                    ra = slice(128 * p, 128 * p + 64)
                    rb = slice(128 * p + 64, 128 * p + 128)
                    dvt_ref[bi, ra, :] = jnp.where(sel, dv_lo, dvt_ref[bi, ra, :])
                    dvt_ref[bi, rb, :] = jnp.where(sel, dv_hi, dvt_ref[bi, rb, :])
                    dk_ref[bi, tl, :, cols] = colsum(ds * vcol)
                    dka_ref[bi, tl, :, cols] = -colsum(ds * skk)
                    dskk, _, _ = _seg_lane_sum(ds * ka_r, lo)
                    dskk = -dskk
                    dw_ref[bi, tl, :, cols] = colsum(ds * s_prev)
                    dkk_ref[bi, tl, :, cols] = colsum(s_prev * dskk)
                    ds_ref[pr] = ds * w_r + dskk * kk_r
            return carry

        lax.fori_loop(0, SCAN_CHUNK, step, 0)

    return pl.pallas_call(
        body, grid=(nc,), in_specs=[rows] * 5 + [tile, tile, hist], out_specs=[rows] * 5 + [tile],
        out_shape=[SDS((b, t, 1, 512), F32)] * 5 + [SDS((b, 512, t), F32)],
        scratch_shapes=[pltpu.VMEM((b * 4, 64, 128), F32)], compiler_params=_cp("arbitrary"), name=name,
    )(r, w, k, kk, kka, vt, dyt, hist_arr)


def _mla_fwd(zm, cs, sn, p, t):
    m = zm.shape[0]
    tr = 256
    per = t // tr
    params = [p["q_norm_g"], p["kv_norm_g"], p["wq"], p["wqr"], p["wk"], p["wv"]]

    def body(z_ref, cs_ref, sn_ref, gq, gkv, wq, wqr, wk, wv, q_ref, k_ref, v_ref):
        q, kf, v = _mla_fn(z_ref[...], cs_ref[...], sn_ref[...], gq[...], gkv[...], wq[...], wqr[...], wk[...], wv[...])
        q_ref[...] = q.astype(BF16)
        k_ref[...] = kf.astype(BF16)
        v_ref[...] = v.astype(BF16)

    tab = BS((tr, 128), lambda i: (i % per, 0))
    return pl.pallas_call(
        body, grid=(m // tr,), in_specs=[BS((tr, MLA_PAD_COLS), lambda i: (i, 0)), tab, tab] + [_const(q.shape) for q in params],
        out_specs=[BS((tr, 1024), lambda i: (i, 0)), BS((tr, 1024), lambda i: (i, 0)), BS((tr, 512), lambda i: (i, 0))],
        out_shape=[SDS((m, 1024), BF16), SDS((m, 1024), BF16), SDS((m, 512), BF16)], compiler_params=_cp("parallel"), name="mla_prep_fwd",
    )(zm, cs, sn, *params)


def _mla_bwd(zm, cs, sn, p, t, dq, dk, dv):
    m = zm.shape[0]
    tr = 128
    per = t // tr
    params = [p["q_norm_g"], p["kv_norm_g"], p["wq"], p["wqr"], p["wk"], p["wv"]]

    def body(z_ref, cs_ref, sn_ref, gq, gkv, wq, wqr, wk, wv, dq_ref, dk_ref, dv_ref, dz_ref, *grads):
        cs_v, sn_v = cs_ref[...], sn_ref[...]
        _, vjp = jax.vjp(
            lambda *q: _mla_fn(q[0], cs_v, sn_v, *q[1:]), z_ref[...], gq[...], gkv[...], wq[...], wqr[...], wk[...], wv[...]
        )
        dz, *dparams = vjp((dq_ref[...], dk_ref[...], dv_ref[...]))
        dz_ref[...] = dz

        @pl.when(pl.program_id(0) == 0)
        def _():
            for gr in grads:
                gr[...] = jnp.zeros_like(gr)

        for gr, val in zip(grads, dparams):
            gr[...] += val

    tab = BS((tr, 128), lambda i: (i % per, 0))
    wide = BS((tr, 1024), lambda i: (i, 0))
    return pl.pallas_call(
        body, grid=(m // tr,),
        in_specs=[BS((tr, MLA_PAD_COLS), lambda i: (i, 0)), tab, tab] + [_const(q.shape) for q in params]
        + [wide, wide, BS((tr, 512), lambda i: (i, 0))],
        out_specs=[BS((tr, MLA_PAD_COLS), lambda i: (i, 0))] + [_const(q.shape) for q in params],
        out_shape=[SDS((m, MLA_PAD_COLS), F32)] + [SDS(q.shape, F32) for q in params],
        compiler_params=_cp("arbitrary"), name="mla_prep_bwd",
    )(zm, cs, sn, *params, dq, dk, dv)


_NT = (((1,), (1,)), ((), ()))
_TN = (((0,), (0,)), ((), ()))


def _attn_fwd(q, kf, v, b, t):
    m = q.shape[0]
    tq = 256
    nq = t // tq

    def body(q_ref, k_ref, v_ref, o_ref, l_ref):
        lo = lax.broadcasted_iota(jnp.int32, (1, 128), 1) < 64
        v_all = v_ref[...]
        o = jnp.zeros((tq, 128), F32)
        lse = []
        for h in range(2):
            hs = slice(128 * h, 128 * h + 128)
            s = lax.dot_general(q_ref[:, hs], k_ref[:, hs], _NT, preferred_element_type=F32) * MLA_SCALE
            mx = jnp.max(s, axis=1, keepdims=True)
            e = jnp.exp(s - mx)
            den = jnp.sum(e, axis=1, keepdims=True)
            vh = jnp.where(lo if h == 0 else jnp.logical_not(lo), v_all, jnp.zeros_like(v_all))
            o = o + jnp.dot(e.astype(BF16), vh, preferred_element_type=F32) / den
            lse.append(mx + jnp.log(den))
        o_ref[...] = o
        l_ref[...] = jnp.where(lo, lse[0], lse[1])

    return pl.pallas_call(
        body, grid=(b, 4, nq),
        in_specs=[BS((tq, 256), lambda bi, hp, i: (bi * nq + i, hp)), BS((t, 256), lambda bi, hp, i: (bi, hp)),
                  BS((t, 128), lambda bi, hp, i: (bi, hp))],
        out_specs=[BS((tq, 128), lambda bi, hp, i: (bi * nq + i, hp))] * 2,
        out_shape=[SDS((m, 512), F32), SDS((m, 512), F32)], compiler_params=_cp("parallel", "parallel", "arbitrary"), name="attn_fwd",
    )(q, kf, v)


def _attn_bwd(q, kf, v, o, lse, do, b, t):
    m = q.shape[0]
    tq = 256
    nq = t // tq

    def body(q_ref, k_ref, v_ref, o_ref, l_ref, do_ref, dq_ref, dk_ref, dv_ref):
        lo = lax.broadcasted_iota(jnp.int32, (1, 128), 1) < 64

        @pl.when(pl.program_id(2) == 0)
        def _():
            dk_ref[...] = jnp.zeros_like(dk_ref)
            dv_ref[...] = jnp.zeros_like(dv_ref)

        v_all, o_all, l_all, do_all = v_ref[...], o_ref[...], l_ref[...], do_ref[...]
        dv_acc = jnp.zeros((t, 128), F32)
        for h in range(2):
            hs = slice(128 * h, 128 * h + 128)
            mask = lo if h == 0 else jnp.logical_not(lo)
            qh, kh = q_ref[:, hs], k_ref[:, hs]
            s = lax.dot_general(qh, kh, _NT, preferred_element_type=F32) * MLA_SCALE
            lse_h = jnp.max(jnp.where(mask, l_all, -jnp.inf), axis=1, keepdims=True)
            pr = jnp.exp(s - lse_h)
            do_h = jnp.where(mask, do_all, 0.0)
            dp = lax.dot_general(do_h.astype(BF16), v_all, _NT, preferred_element_type=F32)
            dsum = jnp.sum(do_h * o_all, axis=1, keepdims=True)
            ds = (pr * (dp - dsum) * MLA_SCALE).astype(BF16)
            dq_ref[:, hs] = jnp.dot(ds, kh, preferred_element_type=F32)
            dk_ref[:, hs] += lax.dot_general(ds, qh, _TN, preferred_element_type=F32)
            dv_acc = dv_acc + lax.dot_general(pr.astype(BF16), do_h.astype(BF16), _TN, preferred_element_type=F32)
        dv_ref[...] += dv_acc

    qspec = BS((tq, 256), lambda bi, hp, i: (bi * nq + i, hp))
    kspec = BS((t, 256), lambda bi, hp, i: (bi, hp))
    vspec = BS((t, 128), lambda bi, hp, i: (bi, hp))
    ospec = BS((tq, 128), lambda bi, hp, i: (bi * nq + i, hp))
    return pl.pallas_call(
        body, grid=(b, 4, nq), in_specs=[qspec, kspec, vspec, ospec, ospec, ospec], out_specs=[qspec, kspec, vspec],
        out_shape=[SDS((m, 1024), F32), SDS((m, 1024), F32), SDS((m, 512), F32)],
        compiler_params=_cp("parallel", "parallel", "arbitrary"), name="attn_bwd",
    )(q, kf, v, o, lse, do)


def _conv3(u, w_ref, b_ref, t):
    up, un = _prev_next(u, t)
    return w_ref[0:1, :] * up + w_ref[1:2, :] * u + w_ref[2:3, :] * un + b_ref[...], up, un


def _ffn_mid_fwd(ug3, uv3, cw, cb):
    b, t, f = ug3.shape
    nc = f // 256

    def body(ug_ref, uv_ref, wg_ref, wv_ref, bg_ref, bv_ref, a_ref):
        gc, _, _ = _conv3(ug_ref[0], wg_ref, bg_ref, t)
        vc, _, _ = _conv3(uv_ref[0], wv_ref, bv_ref, t)
        a_ref[0] = (gc * jax.nn.sigmoid(gc) * vc).astype(BF16)

    blk = BS((1, t, 256), lambda i, j: (i, 0, j))
    return pl.pallas_call(
        body, grid=(b, nc),
        in_specs=[blk, blk, BS((3, 256), lambda i, j: (0, j)), BS((3, 256), lambda i, j: (0, j + nc)),
                  BS((1, 256), lambda i, j: (0, j)), BS((1, 256), lambda i, j: (0, j + nc))],
        out_specs=blk, out_shape=SDS((b, t, f), BF16), compiler_params=_cp("parallel", "parallel"), name="ffn_mid_fwd",
    )(ug3, uv3, cw, cw, cb, cb)


def _ffn_mid_bwd(ug3, uv3, cw, cb, da3):
    b, t, f = ug3.shape
    nc = f // 256

    def half(u, up, un, dc, w_ref):
        dprev, dnext = _prev_next(dc, t)
        du = w_ref[1:2, :] * dc + w_ref[0:1, :] * dnext + w_ref[2:3, :] * dprev
        sums = [jnp.sum(dc * q, axis=0, keepdims=True) for q in (up, u, un)] + [jnp.sum(dc, axis=0, keepdims=True)]
        row = lax.broadcasted_iota(jnp.int32, (8, 256), 0)
        tab = jnp.zeros((8, 256), F32)
        for i, s in enumerate(sums):
            tab = jnp.where(row == i, s, tab)
        return du, tab

    def body(ug_ref, uv_ref, wg_ref, wv_ref, bg_ref, bv_ref, da_ref, dug_ref, duv_ref, tg_ref, tv_ref):
        ug, uv, da = ug_ref[0], uv_ref[0], da_ref[0]
        gc, gp, gn = _conv3(ug, wg_ref, bg_ref, t)
        vc, vp, vn = _conv3(uv, wv_ref, bv_ref, t)
        sg = jax.nn.sigmoid(gc)
        d_gc = da * vc * (sg * (1.0 + gc * (1.0 - sg)))
        d_vc = da * (gc * sg)
        dug, tg = half(ug, gp, gn, d_gc, wg_ref)
        duv, tv = half(uv, vp, vn, d_vc, wv_ref)
        dug_ref[0] = dug
        duv_ref[0] = duv

        @pl.when(pl.program_id(1) == 0)
        def _():
            tg_ref[...] = jnp.zeros_like(tg_ref)
            tv_ref[...] = jnp.zeros_like(tv_ref)

        tg_ref[...] += tg
        tv_ref[...] += tv

    blk = BS((1, t, 256), lambda j, i: (i, 0, j))
    tab = BS((8, 256), lambda j, i: (0, j))
    return pl.pallas_call(
        body, grid=(nc, b),
        in_specs=[blk, blk, BS((3, 256), lambda j, i: (0, j)), BS((3, 256), lambda j, i: (0, j + nc)),
                  BS((1, 256), lambda j, i: (0, j)), BS((1, 256), lambda j, i: (0, j + nc)), blk],
        out_specs=[blk, blk, tab, tab],
        out_shape=[SDS((b, t, f), F32), SDS((b, t, f), F32), SDS((8, f), F32), SDS((8, f), F32)],
        compiler_params=_cp("parallel", "arbitrary"), name="ffn_mid_bwd",
    )(ug3, uv3, cw, cw, cb, cb, da3)


def _add_rows(parts, name):
    r = parts[0].shape[0]
    tr = _row_tile(r, 1024)
    n = len(parts)

    def body(*refs):
        acc = refs[0][...]
        for q in refs[1:n]:
            acc = acc + q[...]
        refs[n][...] = acc

    row = BS((tr, 128), lambda i: (i, 0))
    return pl.pallas_call(
        body, grid=(r // tr,), in_specs=[row] * n, out_specs=row, out_shape=SDS((r, 128), F32),
        compiler_params=_cp("parallel"), name=name,
    )(*parts)


def _adamw(w, g, m, v, name):
    r, c = w.shape
    tr = _row_tile(r)

    def body(w_ref, g_ref, m_ref, v_ref, d_ref, m2_ref, v2_ref):
        d, m2, v2 = _adamw_math(w_ref[...], g_ref[...], m_ref[...], v_ref[...])
        d_ref[...] = d
        m2_ref[...] = m2
        v2_ref[...] = v2

    blk = BS((tr, c), lambda i: (i, 0))
    return pl.pallas_call(
        body, grid=(r // tr,), in_specs=[blk] * 4, out_specs=[blk] * 3, out_shape=[SDS((r, c), F32)] * 3,
        compiler_params=_cp("parallel"), name=name,
    )(w, g, m, v)


def _place():
    return lax.axis_index("x"), lax.axis_index("y"), lax.axis_index("c")


def _flip(v, bit):
    return 1 - v if bit else v


def _allgather_weights(shard):
    r = shard.shape[0]
    rh = r // 2

    def body(x_ref, out_ref, send_sems, recv_sems, local_sem):
        x, y, c = _place()
        me, sibling = (x, y, c), (x, y, 1 - c)
        chips = [(1 - x, y), (x, 1 - y), (1 - x, 1 - y)]
        mine_src = x_ref.at[pl.ds(c * rh, rh), :]

        def rows(px, py, pc):
            return out_ref.at[pl.ds((4 * px + 2 * py + pc) * rh, rh), :]

        def copy(k, block, to, src=None):
            return pltpu.make_async_remote_copy(
                src_ref=rows(*block) if src is None else src, dst_ref=rows(*block), send_sem=send_sems.at[k],
                recv_sem=recv_sems.at[k], device_id=to, device_id_type=MESH,
            )

        mine = pltpu.make_async_copy(mine_src, rows(*me), local_sem)
        mine.start()
        first = [copy(0, me, sibling, src=mine_src)]
        first += [copy(1 + j, me, (*chip, c), src=mine_src) for j, chip in enumerate(chips)]
        for cp in first:
            cp.start()
        passed = [copy(4 + j, (*chip, c), sibling) for j, chip in enumerate(chips)]
        for j, chip in enumerate(chips):
            copy(1 + j, (*chip, c), me).wait_recv()
            passed[j].start()
        copy(0, sibling, me).wait_recv()
        for j, chip in enumerate(chips):
            copy(4 + j, (*chip, 1 - c), me).wait_recv()
        for cp in first + passed:
            cp.wait_send()
        mine.wait()

    return pl.pallas_call(
        body, out_shape=SDS((8 * rh, 128), shard.dtype), in_specs=[ANY], out_specs=ANY,
        scratch_shapes=[pltpu.SemaphoreType.DMA((7,)), pltpu.SemaphoreType.DMA((7,)), pltpu.SemaphoreType.DMA],
        name="allgather_weights",
    )(shard)


def _scatter_to_chips(g):
    def body(g_ref, recv_ref, send_sems, recv_sems, local_sem):
        x, y, c = _place()
        copies = []
        for j, (fx, fy) in enumerate(((1, 0), (0, 1), (1, 1))):
            px, py = _flip(x, fx), _flip(y, fy)
            cp = pltpu.make_async_remote_copy(
                src_ref=g_ref.at[2 * px + py], dst_ref=recv_ref.at[j], send_sem=send_sems.at[j], recv_sem=recv_sems.at[j],
                device_id=(px, py, c), device_id_type=MESH,
            )
            cp.start()
            copies.append(cp)
        own = pltpu.make_async_copy(g_ref.at[2 * x + y], recv_ref.at[3], local_sem)
        own.start()
        for cp in copies:
            cp.wait_recv()
        for cp in copies:
            cp.wait_send()
        own.wait()

    return pl.pallas_call(
        body, out_shape=SDS(g.shape, g.dtype), in_specs=[ANY], out_specs=ANY,
        scratch_shapes=[pltpu.SemaphoreType.DMA((3,)), pltpu.SemaphoreType.DMA((3,)), pltpu.SemaphoreType.DMA],
        name="scatter_grads",
    )(g)


def _swap_with_sibling(a):
    def body(a_ref, b_ref, send_sem, recv_sem):
        x, y, c = _place()
        cp = pltpu.make_async_remote_copy(
            src_ref=a_ref, dst_ref=b_ref, send_sem=send_sem, recv_sem=recv_sem, device_id=(x, y, 1 - c), device_id_type=MESH
        )
        cp.start()
        cp.wait()

    return pl.pallas_call(
        body, out_shape=SDS(a.shape, a.dtype), in_specs=[ANY], out_specs=ANY,
        scratch_shapes=[pltpu.SemaphoreType.DMA, pltpu.SemaphoreType.DMA], name="swap_sibling",
    )(a)


def _allreduce_small(v):
    r = v.shape[0]

    def body(v_ref, out_ref, buf_ref, send_sems, recv_sems):
        x, y, c = _place()
        buf_ref[0] = v_ref[...]
        copies = []
        for k in range(1, 8):
            peer = (_flip(x, k >> 2 & 1), _flip(y, k >> 1 & 1), _flip(c, k & 1))
            cp = pltpu.make_async_remote_copy(
                src_ref=v_ref, dst_ref=buf_ref.at[k], send_sem=send_sems.at[k - 1], recv_sem=recv_sems.at[k - 1],
                device_id=peer, device_id_type=MESH,
            )
            cp.start()
            copies.append(cp)
        for cp in copies:
            cp.wait_recv()
        acc = None
        for d in range(8):
            slot = 4 * _flip(x, d >> 2 & 1) + 2 * _flip(y, d >> 1 & 1) + _flip(c, d & 1)
            term = buf_ref[slot]
            acc = term if acc is None else acc + term
        out_ref[...] = acc
        for cp in copies:
            cp.wait_send()

    return pl.pallas_call(
        body, out_shape=SDS(v.shape, F32), in_specs=[VMEM], out_specs=VMEM,
        scratch_shapes=[pltpu.VMEM((8, r, 128), F32), pltpu.SemaphoreType.DMA((7,)), pltpu.SemaphoreType.DMA((7,))],
        name="allreduce_small",
    )(v)


_BIG = (
    ("w_in", 1, False), ("decay_w2_fwd", 1, False), ("decay_w2_bwd", 1, False), ("iclr_a2_fwd", 1, False),
    ("iclr_a2_bwd", 1, False), ("gate_g2", 1, False), ("w_uq", 0, False), ("w_ukv", 1, False), ("w_out", 0, False),
    ("w_ffn_up", 1, False), ("ffn_conv_w", 1, True), ("w_ffn_down", 0, False),
)
_SMALL = (
    "ln_mix_g", "shift_mu_prev", "shift_mu_next", "decay_w0_fwd", "decay_w0_bwd", "iclr_a0_fwd", "iclr_a0_bwd", "k_k",
    "k_a", "r_k", "ln_x_g", "ln_x_b", "q_norm_g", "kv_norm_g", "mla_out_g", "ln_ffn_g", "ffn_conv_b", "ln_final_g",
)
_WEIGHTS = (
    "ln_mix_g", "w_in", "shift_mu_prev", "shift_mu_next", "decay_w0_fwd", "decay_w2_fwd", "decay_w0_bwd", "decay_w2_bwd",
    "iclr_a0_fwd", "iclr_a2_fwd", "iclr_a0_bwd", "iclr_a2_bwd", "gate_g2", "k_k", "k_a", "r_k", "ln_x_g", "ln_x_b",
    "q_norm_g", "w_uq", "kv_norm_g", "w_ukv", "mla_out_g", "w_out", "ln_ffn_g", "w_ffn_up", "ffn_conv_w", "ffn_conv_b",
    "w_ffn_down", "ln_final_g",
)


def _pad_rows(flat, rows):
    return jnp.pad(flat, (0, rows * 128 - flat.shape[0])).reshape(rows, 128)


def _rows_for(n, mult):
    rows = -(-n // 128)
    return -(-rows // mult) * mult


def _pack_shards_bf16(arrs):
    parts = []
    for name, _, raw in _BIG:
        w = arrs[name][0]
        flat = lax.bitcast_convert_type(w, BF16).reshape(-1) if raw else w.astype(BF16).reshape(-1)
        parts.append(_pad_rows(flat, _rows_for(flat.shape[0], 32)))
    return jnp.concatenate(parts, axis=0)


def _unpack_gathered(g4, arrs):
    out, off = {}, 0
    for name, axis, raw in _BIG:
        a, b = arrs[name].shape[1:]
        n = a * b * (2 if raw else 1)
        rows = _rows_for(n, 32)
        seg = g4[:, off:off + rows].reshape(4, rows * 128)[:, :n]
        off += rows
        if raw:
            seg = lax.bitcast_convert_type(seg.reshape(4, a * b, 2), F32)
        seg = seg.reshape(4, a, b)
        out[name] = jnp.concatenate([seg[s] for s in range(4)], axis=1) if axis == 1 else seg.reshape(4 * a, b)
    return out


def _pack_grads(full, arrs):
    parts = []
    for name, axis, _ in _BIG:
        a, b = arrs[name].shape[1:]
        g = full[name]
        sh = g.reshape(a, 4, b).transpose(1, 0, 2) if axis == 1 else g.reshape(4, a, b)
        rows = _rows_for(a * b, 8)
        parts.append(jnp.pad(sh.reshape(4, a * b), ((0, 0), (0, rows * 128 - a * b))).reshape(4, rows, 128))
    packed = jnp.concatenate(parts, axis=1)
    total = packed.shape[1]
    return jnp.pad(packed, ((0, 0), (0, -(-total // 512) * 512 - total), (0, 0)))


def _unpack_grads(g, arrs):
    out, off = {}, 0
    for name, _, _ in _BIG:
        a, b = arrs[name].shape[1:]
        rows = _rows_for(a * b, 8)
        out[name] = g[off:off + rows].reshape(-1)[:a * b].reshape(a, b)
        off += rows
    return out


def _pack_small(vals):
    flat = jnp.concatenate([vals[n].reshape(-1).astype(F32) for n in _SMALL] + [vals["_loss"].reshape(-1)])
    return _pad_rows(flat, _rows_for(flat.shape[0], 8))


def _unpack_small(buf, arrs):
    flat, out, off = buf.reshape(-1), {}, 0
    for n in _SMALL:
        size = arrs[n].size
        out[n] = flat[off:off + size].reshape(arrs[n].shape)
        off += size
    out["_loss"] = flat[off]
    return out


def _rot_cols(w):
    return jnp.concatenate([-w[..., 16:], w[..., :16]], axis=-1)


def _rot_cols_t(g):
    return jnp.concatenate([g[..., 16:], -g[..., :16]], axis=-1)


def _rope_tables(t):
    inv = jnp.power(ROPE_THETA, -jnp.arange(0, ROPE_DIM, 2, dtype=F32) / ROPE_DIM)
    ang = jnp.arange(t, dtype=F32)[:, None] * inv[None, :]
    one, zero = jnp.ones((t, 64), F32), jnp.zeros((t, 64), F32)
    cs = jnp.concatenate([one, jnp.cos(ang), jnp.cos(ang), zero[:, :32]], axis=1)
    sn = jnp.concatenate([zero, jnp.sin(ang), jnp.sin(ang), zero[:, :32]], axis=1)
    return cs, sn


def _block_diag(a, b):
    za = jnp.zeros_like(a)
    return jnp.concatenate([jnp.concatenate([a, za], axis=1), jnp.concatenate([za, b], axis=1)], axis=0)


def kernel(x, ln_mix_g, w_in, shift_mu_prev, shift_mu_next, decay_w0_fwd, decay_w2_fwd, decay_w0_bwd, decay_w2_bwd, iclr_a0_fwd, iclr_a2_fwd, iclr_a0_bwd, iclr_a2_bwd, gate_g2, k_k, k_a, r_k, ln_x_g, ln_x_b, q_norm_g, w_uq, kv_norm_g, w_ukv, mla_out_g, w_out, ln_ffn_g, w_ffn_up, ffn_conv_w, ffn_conv_b, w_ffn_down, ln_final_g, loss_target, m_ln_mix_g, m_w_in, m_shift_mu_prev, m_shift_mu_next, m_decay_w0_fwd, m_decay_w2_fwd, m_decay_w0_bwd, m_decay_w2_bwd, m_iclr_a0_fwd, m_iclr_a2_fwd, m_iclr_a0_bwd, m_iclr_a2_bwd, m_gate_g2, m_k_k, m_k_a, m_r_k, m_ln_x_g, m_ln_x_b, m_q_norm_g, m_w_uq, m_kv_norm_g, m_w_ukv, m_mla_out_g, m_w_out, m_ln_ffn_g, m_w_ffn_up, m_ffn_conv_w, m_ffn_conv_b, m_w_ffn_down, m_ln_final_g, v_ln_mix_g, v_w_in, v_shift_mu_prev, v_shift_mu_next, v_decay_w0_fwd, v_decay_w2_fwd, v_decay_w0_bwd, v_decay_w2_bwd, v_iclr_a0_fwd, v_iclr_a2_fwd, v_iclr_a0_bwd, v_iclr_a2_bwd, v_gate_g2, v_k_k, v_k_a, v_r_k, v_ln_x_g, v_ln_x_b, v_q_norm_g, v_w_uq, v_kv_norm_g, v_w_ukv, v_mla_out_g, v_w_out, v_ln_ffn_g, v_w_ffn_up, v_ffn_conv_w, v_ffn_conv_b, v_w_ffn_down, v_ln_final_g):
    arrs = dict(locals())
    b, t, d = x.shape
    m = b * t
    x2 = x.reshape(m, d)
    tgt = loss_target.reshape(m, d)
    vec = lambda n: arrs[n].reshape(1, -1)

    gathered = _allgather_weights(_pack_shards_bf16(arrs))
    fw = _unpack_gathered(gathered.reshape(N_CHIPS, -1, 128), arrs)
    win = fw["w_in"]
    zc = jnp.zeros((d, 64), BF16)
    w_kr = win[:, 2944:2976]
    win_m = jnp.concatenate([win[:, 1920:2944], zc, w_kr, zc[:, :32], zc, _rot_cols(w_kr), zc[:, :32]], axis=1)
    win_r = win[:, :RWKV_COLS]
    uq = fw["w_uq"].astype(F32).reshape(Q_RANK, HEADS, 96)
    z32 = jnp.zeros((Q_RANK, HEADS, 32), F32)
    wq = jnp.concatenate([uq[..., :64], uq[..., 64:], z32], axis=-1).reshape(Q_RANK, 1024)
    wqr = jnp.concatenate([z32, z32, _rot_cols(uq[..., 64:]), z32], axis=-1).reshape(Q_RANK, 1024)
    ukv = fw["w_ukv"].astype(F32).reshape(KV_RANK, HEADS, 128)
    wk = jnp.concatenate([ukv[..., :64], jnp.zeros_like(ukv[..., :64])], axis=-1).reshape(KV_RANK, 1024)
    wv = ukv[..., 64:].reshape(KV_RANK, 512)
    head = jnp.arange(512) // HEAD_DIM
    rw = dict(
        w0=jnp.concatenate([vec("decay_w0_fwd"), vec("decay_w0_bwd")], axis=1),
        w2=_block_diag(fw["decay_w2_fwd"], fw["decay_w2_bwd"]).astype(F32),
        a0=jnp.concatenate([vec("iclr_a0_fwd"), vec("iclr_a0_bwd")], axis=1),
        a2=_block_diag(fw["iclr_a2_fwd"], fw["iclr_a2_bwd"]).astype(F32),
        g2=fw["gate_g2"].astype(F32), k_k=vec("k_k"), k_a=vec("k_a"), r_k=vec("r_k"), ln_x_g=vec("ln_x_g"), ln_x_b=vec("ln_x_b"),
        ones_bd=(head[:, None] == head[None, :]).astype(F32),
    )
    mp = dict(q_norm_g=vec("q_norm_g"), kv_norm_g=vec("kv_norm_g"), wq=wq, wqr=wqr, wk=wk, wv=wv)
    cs, sn = _rope_tables(t)
    w_up_g, w_up_v = fw["w_ffn_up"][:, :D_FF], fw["w_ffn_up"][:, D_FF:]
    cw, cb = fw["ffn_conv_w"], vec("ffn_conv_b")

    n1 = _rms_fwd(x2, vec("ln_mix_g"), "rms_mix")
    zm = _mm(n1, win_m, "nn", "proj_in_mla")
    zr = _mm(n1, win_r, "nn", "proj_in_rwkv")
    zs = _shift_fwd(zr.reshape(b, t, RWKV_COLS), vec("shift_mu_prev"), vec("shift_mu_next"))
    zs2 = zs.reshape(m, RWKV_COLS)
    wf, wb, kf, kb, kk, kaf, kab, gate, vt = _prep_fwd(zs2, rw, b, t)
    r4 = lambda a: a.reshape(b, t, 1, 512)
    rr, kk4 = r4(zs[:, :, :512]), r4(kk)
    ops_f = (rr, r4(wf), r4(kf), kk4, r4(kaf), vt)
    ops_b = (rr, r4(wb), r4(kb), kk4, r4(kab), vt)
    yt_f, hist_f = _scan_fwd(*ops_f, False, "wkv_fwd_scan")
    yt_b, hist_b = _scan_fwd(*ops_b, True, "wkv_bwd_scan")
    y_rwkv = _post_fwd(yt_f, yt_b, zs2, kf, kb, gate, rw, t)
    q, kfull, v = _mla_fwd(zm, cs, sn, mp, t)
    o, lse = _attn_fwd(q, kfull, v, b, t)
    y_mla = _rms_fwd(o, vec("mla_out_g"), "rms_mla_out")
    ymix = jnp.concatenate([y_rwkv, y_mla], axis=1)
    h1 = _mm(ymix, fw["w_out"], "nn", "proj_out", add=x2)
    n2 = _rms_fwd(h1, vec("ln_ffn_g"), "rms_ffn")
    ug = _mm(n2, w_up_g, "nn", "ffn_up_gate")
    uv = _mm(n2, w_up_v, "nn", "ffn_up_val")
    r3f = lambda a: a.reshape(b, t, D_FF)
    act = _ffn_mid_fwd(r3f(ug), r3f(uv), cw, cb).reshape(m, D_FF)
    h2 = _mm(act, fw["w_ffn_down"], "nn", "ffn_down", add=h1)
    loss_tab, dh2, g_ln_final = _final(h2, vec("ln_final_g"), tgt)

    gfull = {}
    dact = _mm(dh2, fw["w_ffn_down"], "nt", "d_ffn_act")
    gfull["w_ffn_down"] = _mm(act, dh2, "tn", "g_ffn_down")
    dug, duv, tab_g, tab_v = _ffn_mid_bwd(r3f(ug), r3f(uv), cw, cb, r3f(dact))
    dug, duv = dug.reshape(m, D_FF), duv.reshape(m, D_FF)
    gfull["ffn_conv_w"] = jnp.concatenate([tab_g[0:3], tab_v[0:3]], axis=1)
    g_conv_b = jnp.concatenate([tab_g[3:4], tab_v[3:4]], axis=1)
    dn2 = _mm(duv, w_up_v, "nt", "d_ffn_in_val", add=_mm(dug, w_up_g, "nt", "d_ffn_in_gate"))
    gfull["w_ffn_up"] = jnp.concatenate([_mm(n2, dug, "tn", "g_ffn_up_gate"), _mm(n2, duv, "tn", "g_ffn_up_val")], axis=1)
    dh1, g_ln_ffn = _rms_bwd(h1, vec("ln_ffn_g"), dn2, "rms_ffn_bwd", dres=dh2)
    dymix = _mm(dh1, fw["w_out"], "nt", "d_mix")
    gfull["w_out"] = _mm(ymix, dh1, "tn", "g_w_out")
    do, g_mla_out = _rms_bwd(o, vec("mla_out_g"), dymix, "rms_mla_out_bwd", dy_block=1)
    dq, dk, dv = _attn_bwd(q, kfull, v, o, lse, do, b, t)
    dzm, g_qn, g_kvn, g_wq, g_wqr, g_wk, g_wv = _mla_bwd(zm, cs, sn, mp, t, dq, dk, dv)
    gq3, gqr3 = g_wq.reshape(Q_RANK, HEADS, 128), g_wqr.reshape(Q_RANK, HEADS, 128)
    gfull["w_uq"] = jnp.concatenate(
        [gq3[..., :64], gq3[..., 64:96] + _rot_cols_t(gqr3[..., 64:96])], axis=-1
    ).reshape(Q_RANK, HEADS * 96)
    gfull["w_ukv"] = jnp.concatenate(
        [g_wk.reshape(KV_RANK, HEADS, 128)[..., :64], g_wv.reshape(KV_RANK, HEADS, 64)], axis=-1
    ).reshape(KV_RANK, 1024)
    dyt, dr_p, dk_p, dv_p, dgate, g_rk, g_lnx_g, g_lnx_b = _post_bwd(yt_f, yt_b, zs2, kf, kb, gate, rw, b, t, dymix)
    dr_f, dwf, dkf, dkk_f, dkaf, dvt_f = _scan_bwd(*ops_f, dyt, hist_f, False, "wkv_fwd_scan_bwd")
    dr_b, dwb, dkb, dkk_b, dkab, dvt_b = _scan_bwd(*ops_b, dyt, hist_b, True, "wkv_bwd_scan_bwd")
    f2 = lambda a: a.reshape(m, 512)
    cts = dict(dwf=f2(dwf), dwb=f2(dwb), dkf=f2(dkf), dkb=f2(dkb), dkk_f=f2(dkk_f), dkk_b=f2(dkk_b), dkaf=f2(dkaf), dkab=f2(dkab),
               dr_f=f2(dr_f), dr_b=f2(dr_b), dr_p=dr_p, dk_p=dk_p, dv_p=dv_p, dg=dgate)
    dzs, g_w0, g_w2, g_a0, g_a2, g_g2, g_kk, g_ka = _prep_bwd(zs2, rw, b, t, cts, dvt_f, dvt_b)
    dzr, g_mu_p, g_mu_n = _shift_bwd(dzs.reshape(b, t, RWKV_COLS), zr.reshape(b, t, RWKV_COLS), vec("shift_mu_prev"), vec("shift_mu_next"))
    dzr = dzr.reshape(m, RWKV_COLS)
    gfull["decay_w2_fwd"], gfull["decay_w2_bwd"] = g_w2[:64, :512], g_w2[64:, 512:]
    gfull["iclr_a2_fwd"], gfull["iclr_a2_bwd"] = g_a2[:64, :512], g_a2[64:, 512:]
    gfull["gate_g2"] = g_g2
    dn1 = _mm(dzr, win_r, "nt", "d_proj_in_rwkv", add=_mm(dzm, win_m, "nt", "d_proj_in_mla"))
    g_m = _mm(n1, dzm, "tn", "g_w_in_mla")
    g_r = _mm(n1, dzr, "tn", "g_w_in_rwkv")
    g_kr = g_m[:, 1088:1120] + _rot_cols_t(g_m[:, 1216:1248])
    gfull["w_in"] = jnp.concatenate([g_r, g_m[:, :1024], g_kr], axis=1)
    dx, g_ln_mix = _rms_bwd(x2, vec("ln_mix_g"), dn1, "rms_mix_bwd", dres=dh1)

    recv = _scatter_to_chips(_pack_grads(gfull, arrs))
    mine = _add_rows([recv[3], recv[0], recv[1], recv[2]], "sum_chips")
    g_big = _unpack_grads(_add_rows([mine, _swap_with_sibling(mine)], "sum_cores"), arrs)
    small = {
        "ln_mix_g": g_ln_mix, "shift_mu_prev": g_mu_p, "shift_mu_next": g_mu_n, "decay_w0_fwd": g_w0[:, :512],
        "decay_w0_bwd": g_w0[:, 512:], "iclr_a0_fwd": g_a0[:, :512], "iclr_a0_bwd": g_a0[:, 512:], "k_k": g_kk, "k_a": g_ka,
        "r_k": g_rk, "ln_x_g": g_lnx_g, "ln_x_b": g_lnx_b, "q_norm_g": g_qn, "kv_norm_g": g_kvn, "mla_out_g": g_mla_out,
        "ln_ffn_g": g_ln_ffn, "ffn_conv_b": g_conv_b, "ln_final_g": g_ln_final,
        "_loss": jnp.pad(loss_tab[0, 0:1], (0, 127)),
    }
    g_small_buf = _allreduce_small(_pack_small(small))
    g_small = _unpack_small(g_small_buf, arrs)

    grads, deltas, new_m, new_v = {}, {}, {}, {}
    for name, _, _ in _BIG:
        shape = arrs[name].shape
        two = lambda a: a.reshape(shape[1:])
        dlt, m2, v2 = _adamw(two(arrs[name]), g_big[name], two(arrs["m_" + name]), two(arrs["v_" + name]), "adamw_" + name)
        grads[name] = g_big[name].reshape(shape)
        deltas[name], new_m[name], new_v[name] = dlt.reshape(shape), m2.reshape(shape), v2.reshape(shape)
    pk = lambda pre: _pack_small({**{n: arrs[pre + n] for n in _SMALL}, "_loss": jnp.zeros((128,), F32)})
    sd, sm, sv = _adamw(pk(""), g_small_buf, pk("m_"), pk("v_"), "adamw_small")
    sd, sm, sv = _unpack_small(sd, arrs), _unpack_small(sm, arrs), _unpack_small(sv, arrs)
    for n in _SMALL:
        grads[n], deltas[n], new_m[n], new_v[n] = g_small[n], sd[n], sm[n], sv[n]

    return (g_small["_loss"], dx.reshape(b, t, d), *[grads[n] for n in _WEIGHTS], *[deltas[n] for n in _WEIGHTS],
            *[new_m[n] for n in _WEIGHTS], *[new_v[n] for n in _WEIGHTS])
```

```python
import functools
import math

import jax
import jax.numpy as jnp
from jax import lax
from jax.experimental import pallas as pl
from jax.experimental.pallas import tpu as pltpu

F32, BF16 = jnp.float32, jnp.bfloat16
HIGHEST = lax.Precision.HIGHEST
MESH = pl.DeviceIdType.MESH
ANY = pl.BlockSpec(memory_space=pl.ANY)
VMEM = pl.BlockSpec(memory_space=pltpu.VMEM)
BS = pl.BlockSpec
SDS = jax.ShapeDtypeStruct

NORM_EPS = 1e-6
GN_EPS = 64e-5
L2_EPS = 1e-12
HEADS = 8
HEAD_DIM = 64
D_RWKV = HEADS * HEAD_DIM
ROPE_DIM = 32
ROPE_THETA = 10000.0
MLA_SCALE = (64 + ROPE_DIM) ** -0.5
Q_RANK, KV_RANK = 768, 256
RWKV_COLS = 1920
MLA_PAD_COLS = Q_RANK + KV_RANK + 256
D_FF = 2816
ADAM_LR, ADAM_B1, ADAM_B2, ADAM_EPS, ADAM_WD, ADAM_STEP = 0.001, 0.9, 0.999, 1e-08, 0.01, 10

V7X_LANES = 128
V7X_VMEM_LIMIT = 48 * 1024 * 1024
SCAN_CHUNK = 16
N_CHIPS = 4


def _cp(*sem):
    return pltpu.CompilerParams(dimension_semantics=sem, vmem_limit_bytes=V7X_VMEM_LIMIT)


def _tile(n, cands=(512, 640, 384, 256, 128)):
    for c in cands:
        if n % c == 0:
            return c
    return n


def _row_tile(n, cap=256):
    best = n
    for t in range(8, cap + 1, 8):
        if n % t == 0:
            best = t
    return best if best <= cap or n <= cap else n


def _rms(x, g):
    ms = jnp.mean(x * x, axis=-1, keepdims=True)
    return x * lax.rsqrt(ms + NORM_EPS) * g


@jax.custom_vjp
def _bdot(x, w):
    return jnp.dot(x.astype(BF16), w.astype(BF16), preferred_element_type=F32)


def _bdot_fwd(x, w):
    return _bdot(x, w), (x, w)


def _bdot_bwd(res, ct):
    x, w = res
    c = ct.astype(BF16)
    dx = lax.dot_general(c, w.astype(BF16), (((1,), (1,)), ((), ())), preferred_element_type=F32)
    dw = lax.dot_general(x.astype(BF16), c, (((0,), (0,)), ((), ())), preferred_element_type=F32)
    return dx.astype(x.dtype), dw.astype(w.dtype)


_bdot.defvjp(_bdot_fwd, _bdot_bwd)


def _headsum(x, ones_bd):
    return jnp.dot(x, ones_bd, precision=HIGHEST, preferred_element_type=F32)


def _prep_fn(zs, w0, w2, a0, a2, g2, k_k, k_a, ones_bd):
    k = zs[:, 512:1024]
    wd = zs[:, 1536:1664]
    ad = zs[:, 1664:1792]
    gd = zs[:, 1792:1920]
    logit = w0 + _bdot(jnp.tanh(wd), w2)
    w = jnp.exp(-math.exp(-0.5) * jax.nn.sigmoid(logit))
    a = jax.nn.sigmoid(a0 + _bdot(ad, a2))
    g = _bdot(jax.nn.sigmoid(gd), g2)
    kkr = k * k_k
    nrm = jnp.sqrt(_headsum(kkr * kkr, ones_bd))
    kk = kkr / jnp.maximum(nrm, L2_EPS)
    a_f, a_b = a[:, :512], a[:, 512:]
    kf = k * (1.0 + (a_f - 1.0) * k_a)
    kb = k * (1.0 + (a_b - 1.0) * k_a)
    return w[:, :512], w[:, 512:], kf, kb, kk, kk * a_f, kk * a_b, g


def _post_fn(y, r, kf, kb, v, g, r_k, ln_g, ln_b, ones_bd):
    mu =_headsum(y, ones_bd) * (1.0 / HEAD_DIM)
    yc = y - mu
    var = _headsum(yc * yc, ones_bd) * (1.0 / HEAD_DIM)
    yn = yc * lax.rsqrt(var + GN_EPS) * ln_g + ln_b
    bonus = _headsum(r * (kf + kb) * r_k, ones_bd) * v
    return (yn + bonus) * g


def _cat8(x):
    return jnp.concatenate([x] * HEADS, axis=1)


def _mla_fn(zm, cs, sn, gq, gkv, wq, wqr, wk, wv):
    cq = zm[:, :Q_RANK]
    ckv = zm[:, Q_RANK:Q_RANK + KV_RANK]
    kr = zm[:, Q_RANK + KV_RANK:Q_RANK + KV_RANK + 128]
    krr = zm[:, Q_RANK + KV_RANK + 128:]
    cqn = _rms(cq, gq)
    ckvn = _rms(ckv, gkv)
    q = _bdot(cqn, wq) * _cat8(cs) + _bdot(cqn, wqr) * _cat8(sn)
    kro = kr * cs + krr * sn
    kfull = _bdot(ckvn, wk) + _cat8(kro)
    v = _bdot(ckvn, wv)
    return q, kfull, v


def _adamw_math(w, g, m, v):
    m2 = ADAM_B1 * m + (1.0 - ADAM_B1) * g
    v2 = ADAM_B2 * v + (1.0 - ADAM_B2) * (g * g)
    m_hat = m2 / (1.0 - ADAM_B1 ** ADAM_STEP)
    v_hat = v2 / (1.0 - ADAM_B2 ** ADAM_STEP)
    delta = -ADAM_LR * (m_hat / (jnp.sqrt(v_hat) + ADAM_EPS) + ADAM_WD * w)
    return delta, m2, v2


_DIMS = {"nn": (((1,), (0,)), ((), ())), "nt": (((1,), (1,)), ((), ())), "tn": (((0,), (0,)), ((), ()))}


def _mm(a, b, mode, name, out_dtype=F32, add=None):
    if mode == "nn":
        (m, k), (_, n) = a.shape, b.shape
    elif mode == "nt":
        (m, k), (n, _) = a.shape, b.shape
    else:
        (k, m), (_, n) = a.shape, b.shape
    tm, tn, tk = _tile(m), _tile(n), _tile(k)
    nk = k // tk

    def body(a_ref, b_ref, *rest):
        if add is None:
            o_ref, acc_ref = rest
        else:
            add_ref, o_ref, acc_ref = rest
        kk = pl.program_id(2)

        @pl.when(kk == 0)
        def _():
            acc_ref[...] = jnp.zeros_like(acc_ref)

        acc_ref[...] += lax.dot_general(
            a_ref[...].astype(BF16), b_ref[...].astype(BF16), _DIMS[mode], preferred_element_type=F32
        )

        @pl.when(kk == nk - 1)
        def _():
            r = acc_ref[...]
            if add is not None:
                r = r + add_ref[...]
            o_ref[...] = r.astype(out_dtype)

    a_spec = BS((tk, tm), lambda i, j, kk: (kk, i)) if mode == "tn" else BS((tm, tk), lambda i, j, kk: (i, kk))
    b_spec = BS((tn, tk), lambda i, j, kk: (j, kk)) if mode == "nt" else BS((tk, tn), lambda i, j, kk: (kk, j))
    o_spec = BS((tm, tn), lambda i, j, kk: (i, j))
    ins, specs = [a, b], [a_spec, b_spec]
    if add is not None:
        ins.append(add)
        specs.append(o_spec)
    return pl.pallas_call(
        body, grid=(m // tm, n // tn, nk), in_specs=specs, out_specs=o_spec, out_shape=SDS((m, n), out_dtype),
        scratch_shapes=[pltpu.VMEM((tm, tn), F32)], compiler_params=_cp("parallel", "parallel", "arbitrary"), name=name,
    )(*ins)


def _rms_fwd(x, g, name):
    m, d = x.shape
    tr = _tile(m)

    def body(x_ref, g_ref, o_ref):
        o_ref[...] = _rms(x_ref[...], g_ref[...]).astype(BF16)

    return pl.pallas_call(
        body, grid=(m // tr,), in_specs=[BS((tr, d), lambda i: (i, 0)), BS((1, d), lambda i: (0, 0))],
        out_specs=BS((tr, d), lambda i: (i, 0)), out_shape=SDS((m, d), BF16), compiler_params=_cp("parallel"), name=name,
    )(x, g)


def _rms_bwd(x, g, dy, name, dres=None, dy_block=0):
    m, d = x.shape
    tr = _row_tile(m)

    def body(x_ref, g_ref, dy_ref, *rest):
        if dres is None:
            dx_ref, dg_ref = rest
        else:
            dres_ref, dx_ref, dg_ref = rest
        _, vjp = jax.vjp(_rms, x_ref[...], g_ref[...])
        dx, dg = vjp(dy_ref[...])
        if dres is not None:
            dx = dx + dres_ref[...]
        dx_ref[...] = dx

        @pl.when(pl.program_id(0) == 0)
        def _():
            dg_ref[...] = jnp.zeros_like(dg_ref)

        dg_ref[...] += dg

    row = BS((tr, d), lambda i: (i, 0))
    vec = BS((1, d), lambda i: (0, 0))
    ins, specs = [x, g, dy], [row, vec, BS((tr, d), lambda i: (i, dy_block))]
    if dres is not None:
        ins.append(dres)
        specs.append(row)
    return pl.pallas_call(
        body, grid=(m // tr,), in_specs=specs, out_specs=[row, vec], out_shape=[SDS((m, d), F32), SDS((1, d), F32)],
        compiler_params=_cp("arbitrary"), name=name,
    )(*ins)


def _final(h, g, tgt):
    m, d = h.shape
    tr = _row_tile(m)

    def loss_fn(hh, gg, tt):
        e = _rms(hh, gg) - tt
        return 0.5 * jnp.sum(e * e) * (1.0 / d)

    def body(h_ref, g_ref, t_ref, l_ref, dh_ref, dg_ref):
        val, (dh, dg) = jax.value_and_grad(loss_fn, argnums=(0, 1))(h_ref[...], g_ref[...], t_ref[...])
        dh_ref[...] = dh

        @pl.when(pl.program_id(0) == 0)
        def _():
            dg_ref[...] = jnp.zeros_like(dg_ref)
            l_ref[...] = jnp.zeros_like(l_ref)

        dg_ref[...] += dg
        l_ref[...] += jnp.full(l_ref.shape, val, F32)

    row = BS((tr, d), lambda i: (i, 0))
    vec = BS((1, d), lambda i: (0, 0))
    return pl.pallas_call(
        body, grid=(m // tr,), in_specs=[row, vec, row], out_specs=[BS((8, 128), lambda i: (0, 0)), row, vec],
        out_shape=[SDS((8, 128), F32), SDS((m, d), F32), SDS((1, d), F32)], compiler_params=_cp("arbitrary"), name="final_loss",
    )(h, g, tgt)


def _prev_next(z, t):
    row = lax.broadcasted_iota(jnp.int32, z.shape, 0)
    zp = jnp.where(row == 0, 0.0, pltpu.roll(z, 1, axis=0))
    zn = jnp.where(row == t - 1, 0.0, pltpu.roll(z, t - 1, axis=0))
    return zp, zn


def _shift_fwd(z3, mu_p, mu_n):
    b, t, c = z3.shape
    nc = c // 128

    def body(z_ref, mp_ref, mn_ref, o_ref):
        z = z_ref[0]
        zp, zn = _prev_next(z, t)
        o_ref[0] = z + mp_ref[...] * (zp - z) + mn_ref[...] * (zn - z)

    blk = BS((1, t, 128), lambda i, j: (i, 0, j))
    vec = BS((1, 128), lambda i, j: (0, j))
    return pl.pallas_call(
        body, grid=(b, nc), in_specs=[blk, vec, vec], out_specs=blk, out_shape=SDS((b, t, c), F32),
        compiler_params=_cp("parallel", "parallel"), name="shift_fwd",
    )(z3, mu_p, mu_n)


def _shift_bwd(dzs3, z3, mu_p, mu_n):
    b, t, c = z3.shape
    nc = c // 128

    def body(d_ref, z_ref, mp_ref, mn_ref, dz_ref, dmp_ref, dmn_ref):
        d, z = d_ref[0], z_ref[0]
        mp, mn = mp_ref[...], mn_ref[...]
        zp, zn = _prev_next(z, t)
        _, dp_next = _prev_next(d * mp, t)
        dn_prev, _ = _prev_next(d * mn, t)
        dz_ref[0] = d * (1.0 - mp - mn) + dp_next + dn_prev

        @pl.when(pl.program_id(1) == 0)
        def _():
            dmp_ref[...] = jnp.zeros_like(dmp_ref)
            dmn_ref[...] = jnp.zeros_like(dmn_ref)

        dmp_ref[...] += jnp.sum(d * (zp - z), axis=0, keepdims=True)
        dmn_ref[...] += jnp.sum(d * (zn - z), axis=0, keepdims=True)

    blk = BS((1, t, 128), lambda j, i: (i, 0, j))
    vec = BS((1, 128), lambda j, i: (0, j))
    return pl.pallas_call(
        body, grid=(nc, b), in_specs=[blk, blk, vec, vec], out_specs=[blk, vec, vec],
        out_shape=[SDS((b, t, c), F32), SDS((1, c), F32), SDS((1, c), F32)],
        compiler_params=_cp("parallel", "arbitrary"), name="shift_bwd",
    )(dzs3, z3, mu_p, mu_n)


def _const(shape):
    nd = len(shape)
    return BS(shape, lambda i: (0,) * nd)


def _prep_fwd(zs, p):
    m = zs.shape[0]
    tr = 256
    params = [p["w0"], p["w2"], p["a0"], p["a2"], p["g2"], p["k_k"], p["k_a"], p["ones_bd"]]

    def body(zs_ref, w0, w2, a0, a2, g2, kk_, ka_, bd, wf, wb, kf, kb, kk, kaf, kab, g):
        outs = _prep_fn(zs_ref[...], w0[...], w2[...], a0[...], a2[...], g2[...], kk_[...], ka_[...], bd[...])
        for ref, val in zip((wf, wb, kf, kb, kk, kaf, kab, g), outs):
            ref[...] = val

    row = BS((tr, 512), lambda i: (i, 0))
    return pl.pallas_call(
        body, grid=(m // tr,), in_specs=[BS((tr, RWKV_COLS), lambda i: (i, 0))] + [_const(q.shape) for q in params],
        out_specs=[row] * 8, out_shape=[SDS((m, 512), F32)] * 8, compiler_params=_cp("parallel"), name="rwkv_prep_fwd",
    )(zs, *params)


def _prep_bwd(zs, p, ct_rows):
    m = zs.shape[0]
    tr = 128
    params = [p["w0"], p["w2"], p["a0"], p["a2"], p["g2"], p["k_k"], p["k_a"]]
    names = ["dwf", "dwb", "dkf", "dkb", "dkk_f", "dkk_b", "dkaf", "dkab", "dr_f", "dr_b", "dr_p", "dk_p", "dv_p", "dg",
             "dv_f", "dv_b"]
    rows = [ct_rows[n] for n in names]

    def body(zs_ref, w0, w2, a0, a2, g2, kk_, ka_, bd, *rest):
        c = {n: r[...] for n, r in zip(names, rest[:len(names)])}
        outs = rest[len(names):]
        dzs_ref, grads = outs[0], outs[1:]
        ones_bd = bd[...]
        _, vjp = jax.vjp(
            lambda *q: _prep_fn(*q, ones_bd), zs_ref[...], w0[...], w2[...], a0[...], a2[...], g2[...], kk_[...], ka_[...]
        )
        cts = (c["dwf"], c["dwb"], c["dkf"] + c["dk_p"], c["dkb"] + c["dk_p"], c["dkk_f"] + c["dkk_b"], c["dkaf"], c["dkab"], c["dg"])
        dzs, *dparams = vjp(cts)
        dr = c["dr_f"] + c["dr_b"] + c["dr_p"]
        dv = c["dv_f"] + c["dv_b"] + c["dv_p"]
        dzs_ref[:, 0:512] = dzs[:, 0:512] + dr
        dzs_ref[:, 512:1024] = dzs[:, 512:1024]
        dzs_ref[:, 1024:1536] = dzs[:, 1024:1536] + dv
        dzs_ref[:, 1536:1920] = dzs[:, 1536:1920]

        @pl.when(pl.program_id(0) == 0)
        def _():
            for gr in grads:
                gr[...] = jnp.zeros_like(gr)

        for gr, val in zip(grads, dparams):
            gr[...] += val

    row = BS((tr, 512), lambda i: (i, 0))
    return pl.pallas_call(
        body, grid=(m // tr,),
        in_specs=[BS((tr, RWKV_COLS), lambda i: (i, 0))] + [_const(q.shape) for q in params] + [_const(p["ones_bd"].shape)]
        + [row] * len(names),
        out_specs=[BS((tr, RWKV_COLS), lambda i: (i, 0))] + [_const(q.shape) for q in params],
        out_shape=[SDS((m, RWKV_COLS), F32)] + [SDS(q.shape, F32) for q in params],
        compiler_params=_cp("arbitrary"), name="rwkv_prep_bwd",
    )(zs, *params, p["ones_bd"], *rows)


def _post_specs(tr):
    r = BS((tr, 512), lambda i: (i, 0))
    v = BS((tr, 512), lambda i: (i, 2))
    row = BS((tr, 512), lambda i: (i, 0))
    return r, v, row


def _post_fwd(y_f, y_b, zs, kf, kb, g, p):
    m = zs.shape[0]
    tr = 256
    r, v, row = _post_specs(tr)
    vecs = [p["r_k"], p["ln_x_g"], p["ln_x_b"], p["ones_bd"]]

    def body(yf, yb, r_ref, v_ref, kf_ref, kb_ref, g_ref, rk, lg, lb, bd, o_ref):
        o_ref[...] = _post_fn(
            yf[...] + yb[...], r_ref[...], kf_ref[...], kb_ref[...], v_ref[...], g_ref[...], rk[...], lg[...], lb[...], bd[...]
        ).astype(BF16)

    return pl.pallas_call(
        body, grid=(m // tr,), in_specs=[row, row, r, v, row, row, row] + [_const(q.shape) for q in vecs],
        out_specs=row, out_shape=SDS((m, 512), BF16), compiler_params=_cp("parallel"), name="rwkv_post_fwd",
    )(y_f, y_b, zs, zs, kf, kb, g, *vecs)


def _post_bwd(y_f, y_b, zs, kf, kb, g, p, dymix):
    m = zs.shape[0]
    tr = 128
    r, v, row = _post_specs(tr)
    vecs = [p["r_k"], p["ln_x_g"], p["ln_x_b"]]

    def body(yf, yb, r_ref, v_ref, kf_ref, kb_ref, g_ref, rk, lg, lb, bd, dy_ref, dyo, dr, dk, dv, dg, drk, dlg, dlb):
        ones_bd = bd[...]
        _, vjp = jax.vjp(
            lambda *q: _post_fn(*q, ones_bd),
            yf[...] + yb[...], r_ref[...], kf_ref[...], kb_ref[...], v_ref[...], g_ref[...], rk[...], lg[...], lb[...],
        )
        c_y, c_r, c_kf, _, c_v, c_g, c_rk, c_lg, c_lb = vjp(dy_ref[...])
        dyo[...] = c_y
        dr[...] = c_r
        dk[...] = c_kf
        dv[...] = c_v
        dg[...] = c_g

        @pl.when(pl.program_id(0) == 0)
        def _():
            for ref in (drk, dlg, dlb):
                ref[...] = jnp.zeros_like(ref)

        drk[...] += c_rk
        dlg[...] += c_lg
        dlb[...] += c_lb

    vec = _const((1, 512))
    return pl.pallas_call(
        body, grid=(m // tr,),
        in_specs=[row, row, r, v, row, row, row] + [_const(q.shape) for q in vecs] + [_const(p["ones_bd"].shape), row],
        out_specs=[row, row, row, row, row, vec, vec, vec],
        out_shape=[SDS((m, 512), F32)] * 5 + [SDS((1, 512), F32)] * 3,
        compiler_params=_cp("arbitrary"), name="rwkv_post_bwd",
    )(y_f, y_b, zs, zs, kf, kb, g, *vecs, p["ones_bd"], dymix)


def _half_ones():
    ri = lax.broadcasted_iota(jnp.int32, (256, 128), 0) & 127
    ci = lax.broadcasted_iota(jnp.int32, (256, 128), 1)
    return jnp.where((ri < 64) == (ci < 64), 1.0, 0.0).astype(BF16)


def _half_sums(xs, ones):
    x = jnp.concatenate(xs, axis=0)
    hi = x.astype(BF16)
    mid = (x - hi.astype(F32)).astype(BF16)
    res = jnp.dot(jnp.concatenate([hi, mid], axis=1), ones, preferred_element_type=F32)
    return [res[64 * i:64 * i + 64] for i in range(len(xs))]


def _scan_specs(b, t):
    nc = t // SCAN_CHUNK
    up, down = (lambda c: c), (lambda c: nc - 1 - c)
    rows = [BS((b, SCAN_CHUNK, 1, 512), lambda c, ci=ci: (0, ci(c), 0, 0)) for ci in (up, down)]
    hist = [BS((SCAN_CHUNK, b * 4, 64, 128), lambda c, ci=ci: (ci(c), 0, 0, 0)) for ci in (up, down)]
    return nc, rows, hist


def _pairs(b):
    return [(bi * 4 + p, bi, slice(128 * p, 128 * p + 128)) for bi in range(b) for p in range(4)]


def _colsum(x):
    return jnp.sum(x, axis=0, keepdims=True)


def _eye_mask():
    return (lax.broadcasted_iota(jnp.int32, (64, 128), 1) & 63) == lax.broadcasted_iota(jnp.int32, (64, 128), 0)


def _scan_fwd(r, kk, v, ops_f, ops_b):
    b, t = r.shape[:2]
    nc, rows, hist = _scan_specs(b, t)
    npair = b * 4

    def body(*refs):
        ins, outs, s_ref = refs[:12], refs[12:16], refs[16]
        dirs = [dict(zip(("r", "kk", "v", "w", "k", "ka", "y", "h"), (*ins[6 * d:6 * d + 6], *outs[2 * d:2 * d + 2])))
                for d in (0, 1)]

        @pl.when(pl.program_id(0) == 0)
        def _():
            s_ref[...] = jnp.zeros_like(s_ref)

        ones, eye = _half_ones(), _eye_mask()
        chains = [(d, pr, bi, cols) for d in (0, 1) for pr, bi, cols in _pairs(b)]

        def step(tt, carry):
            tls = (tt, SCAN_CHUNK - 1 - tt)
            s_prev, xa = [], []
            for d, pr, bi, cols in chains:
                q, tl = dirs[d], tls[d]
                s = s_ref[d * npair + pr]
                q["h"][tl, pr] = s
                s_prev.append(s)
                xa += [s * q["kk"][bi, tl, :, cols], jnp.where(eye, q["v"][bi, tl, :, cols], 0.0)]
            ra = _half_sums(xa, ones)
            xb = []
            for i, (d, pr, bi, cols) in enumerate(chains):
                q, tl = dirs[d], tls[d]
                s_new = s_prev[i] * q["w"][bi, tl, :, cols] - ra[2 * i] * q["ka"][bi, tl, :, cols] \
                    + ra[2 * i + 1] * q["k"][bi, tl, :, cols]
                s_ref[d * npair + pr] = s_new
                xb.append(s_new * q["r"][bi, tl, :, cols])
            rb = _half_sums(xb, ones)
            for i, (d, pr, bi, cols) in enumerate(chains):
                dirs[d]["y"][bi, tls[d], :, cols] = _colsum(jnp.where(eye, rb[i], 0.0))
            return carry

        lax.fori_loop(0, SCAN_CHUNK, step, 0)

    row_shape, hist_shape = SDS((b, t, 1, 512), F32), SDS((t, npair, 64, 128), F32)
    return pl.pallas_call(
        body, grid=(nc,), in_specs=[rows[0]] * 6 + [rows[1]] * 6, out_specs=[rows[0], hist[0], rows[1], hist[1]],
        out_shape=[row_shape, hist_shape, row_shape, hist_shape],
        scratch_shapes=[pltpu.VMEM((2 * npair, 64, 128), F32)], compiler_params=_cp("arbitrary"), name="wkv_scan",
    )(r, kk, v, *ops_f, r, kk, v, *ops_b)


def _scan_bwd(r, kk, v, dy, ops_f, hist_f, ops_b, hist_b):
    b, t = r.shape[:2]
    nc, rows, hist = _scan_specs(b, t)
    npair = b * 4
    names_in = ("r", "kk", "v", "dy", "w", "k", "ka", "h")
    names_out = ("dr", "dw", "dk", "dkk", "dka", "dv")

    def body(*refs):
        ins, outs, ds_ref = refs[:16], refs[16:28], refs[28]
        dirs = [dict(zip(names_in + names_out, (*ins[8 * d:8 * d + 8], *outs[6 * d:6 * d + 6]))) for d in (0, 1)]

        @pl.when(pl.program_id(0) == 0)
        def _():
            ds_ref[...] = jnp.zeros_like(ds_ref)

        ones, eye = _half_ones(), _eye_mask()
        chains = [(d, pr, bi, cols) for d in (0, 1) for pr, bi, cols in _pairs(b)]

        def step(tt, carry):
            tls = (SCAN_CHUNK - 1 - tt, tt)
            s_prev, xa = [], []
            for d, pr, bi, cols in chains:
                q, tl = dirs[d], tls[d]
                s = q["h"][tl, pr]
                s_prev.append(s)
                xa += [s * q["kk"][bi, tl, :, cols], jnp.where(eye, q["v"][bi, tl, :, cols], 0.0),
                       jnp.where(eye, q["dy"][bi, tl, :, cols], 0.0)]
            ra = _half_sums(xa, ones)
            ds_now, xb = [], []
            for i, (d, pr, bi, cols) in enumerate(chains):
                q, tl = dirs[d], tls[d]
                skk, vcol, dycol = ra[3 * i], ra[3 * i + 1], ra[3 * i + 2]
                ka_r, k_r = q["ka"][bi, tl, :, cols], q["k"][bi, tl, :, cols]
                s_new = s_prev[i] * q["w"][bi, tl, :, cols] - skk * ka_r + vcol * k_r
                ds = ds_ref[d * npair + pr] + dycol * q["r"][bi, tl, :, cols]
                q["dr"][bi, tl, :, cols] = _colsum(s_new * dycol)
                q["dk"][bi, tl, :, cols] = _colsum(ds * vcol)
                q["dka"][bi, tl, :, cols] = -_colsum(ds * skk)
                q["dw"][bi, tl, :, cols] = _colsum(ds * s_prev[i])
                ds_now.append(ds)
                xb += [ds * k_r, ds * ka_r]
            rb = _half_sums(xb, ones)
            for i, (d, pr, bi, cols) in enumerate(chains):
                q, tl = dirs[d], tls[d]
                dskk = -rb[2 * i + 1]
                q["dv"][bi, tl, :, cols] = _colsum(jnp.where(eye, rb[2 * i], 0.0))
                q["dkk"][bi, tl, :, cols] = _colsum(s_prev[i] * dskk)
                ds_ref[d * npair + pr] = ds_now[i] * q["w"][bi, tl, :, cols] + dskk * q["kk"][bi, tl, :, cols]
            return carry

        lax.fori_loop(0, SCAN_CHUNK, step, 0)

    row_shape = SDS((b, t, 1, 512), F32)
    return pl.pallas_call(
        body, grid=(nc,), in_specs=[rows[1]] * 7 + [hist[1]] + [rows[0]] * 7 + [hist[0]],
        out_specs=[rows[1]] * 6 + [rows[0]] * 6, out_shape=[row_shape] * 12,
        scratch_shapes=[pltpu.VMEM((2 * npair, 64, 128), F32)], compiler_params=_cp("arbitrary"), name="wkv_scan_bwd",
    )(r, kk, v, dy, *ops_f, hist_f, r, kk, v, dy, *ops_b, hist_b)


def _mla_fwd(zm, cs, sn, p, t):
    m = zm.shape[0]
    tr = 256
    per = t // tr
    params = [p["q_norm_g"], p["kv_norm_g"], p["wq"], p["wqr"], p["wk"], p["wv"]]

    def body(z_ref, cs_ref, sn_ref, gq, gkv, wq, wqr, wk, wv, q_ref, k_ref, v_ref):
        q, kf, v = _mla_fn(z_ref[...], cs_ref[...], sn_ref[...], gq[...], gkv[...], wq[...], wqr[...], wk[...], wv[...])
        q_ref[...] = q.astype(BF16)
        k_ref[...] = kf.astype(BF16)
        v_ref[...] = v.astype(BF16)

    tab = BS((tr, 128), lambda i: (i % per, 0))
    return pl.pallas_call(
        body, grid=(m // tr,), in_specs=[BS((tr, MLA_PAD_COLS), lambda i: (i, 0)), tab, tab] + [_const(q.shape) for q in params],
        out_specs=[BS((tr, 1024), lambda i: (i, 0)), BS((tr, 1024), lambda i: (i, 0)), BS((tr, 512), lambda i: (i, 0))],
        out_shape=[SDS((m, 1024), BF16), SDS((m, 1024), BF16), SDS((m, 512), BF16)], compiler_params=_cp("parallel"), name="mla_prep_fwd",
    )(zm, cs, sn, *params)


def _mla_bwd(zm, cs, sn, p, t, dq, dk, dv):
    m = zm.shape[0]
    tr = 128
    per = t // tr
    params = [p["q_norm_g"], p["kv_norm_g"], p["wq"], p["wqr"], p["wk"], p["wv"]]

    def body(z_ref, cs_ref, sn_ref, gq, gkv, wq, wqr, wk, wv, dq_ref, dk_ref, dv_ref, dz_ref, *grads):
        cs_v, sn_v = cs_ref[...], sn_ref[...]
        _, vjp = jax.vjp(
            lambda *q: _mla_fn(q[0], cs_v, sn_v, *q[1:]), z_ref[...], gq[...], gkv[...], wq[...], wqr[...], wk[...], wv[...]
        )
        dz, *dparams = vjp((dq_ref[...], dk_ref[...], dv_ref[...]))
        dz_ref[...] = dz

        @pl.when(pl.program_id(0) == 0)
        def _():
            for gr in grads:
                gr[...] = jnp.zeros_like(gr)

        for gr, val in zip(grads, dparams):
            gr[...] += val

    tab = BS((tr, 128), lambda i: (i % per, 0))
    wide = BS((tr, 1024), lambda i: (i, 0))
    return pl.pallas_call(
        body, grid=(m // tr,),
        in_specs=[BS((tr, MLA_PAD_COLS), lambda i: (i, 0)), tab, tab] + [_const(q.shape) for q in params]
        + [wide, wide, BS((tr, 512), lambda i: (i, 0))],
        out_specs=[BS((tr, MLA_PAD_COLS), lambda i: (i, 0))] + [_const(q.shape) for q in params],
        out_shape=[SDS((m, MLA_PAD_COLS), F32)] + [SDS(q.shape, F32) for q in params],
        compiler_params=_cp("arbitrary"), name="mla_prep_bwd",
    )(zm, cs, sn, *params, dq, dk, dv)


_NT = (((1,), (1,)), ((), ()))
_TN = (((0,), (0,)), ((), ()))


def _attn_fwd(q, kf, v, b, t):
    m = q.shape[0]
    tq = 256
    nq = t // tq

    def body(q_ref, k_ref, v_ref, o_ref, l_ref):
        lo = lax.broadcasted_iota(jnp.int32, (1, 128), 1) < 64
        v_all = v_ref[...]
        o = jnp.zeros((tq, 128), F32)
        lse = []
        for h in range(2):
            hs = slice(128 * h, 128 * h + 128)
            s = lax.dot_general(q_ref[:, hs], k_ref[:, hs], _NT, preferred_element_type=F32) * MLA_SCALE
            mx = jnp.max(s, axis=1, keepdims=True)
            e = jnp.exp(s - mx)
            den = jnp.sum(e, axis=1, keepdims=True)
            vh = jnp.where(lo if h == 0 else jnp.logical_not(lo), v_all, jnp.zeros_like(v_all))
            o = o + jnp.dot(e.astype(BF16), vh, preferred_element_type=F32) / den
            lse.append(mx + jnp.log(den))
        o_ref[...] = o
        l_ref[...] = jnp.where(lo, lse[0], lse[1])

    return pl.pallas_call(
        body, grid=(b, 4, nq),
        in_specs=[BS((tq, 256), lambda bi, hp, i: (bi * nq + i, hp)), BS((t, 256), lambda bi, hp, i: (bi, hp)),
                  BS((t, 128), lambda bi, hp, i: (bi, hp))],
        out_specs=[BS((tq, 128), lambda bi, hp, i: (bi * nq + i, hp))] * 2,
        out_shape=[SDS((m, 512), F32), SDS((m, 512), F32)], compiler_params=_cp("parallel", "parallel", "arbitrary"), name="attn_fwd",
    )(q, kf, v)


def _attn_bwd(q, kf, v, o, lse, do, b, t):
    m = q.shape[0]
    tq = 256
    nq = t // tq

    def body(q_ref, k_ref, v_ref, o_ref, l_ref, do_ref, dq_ref, dk_ref, dv_ref):
        lo = lax.broadcasted_iota(jnp.int32, (1, 128), 1) < 64

        @pl.when(pl.program_id(2) == 0)
        def _():
            dk_ref[...] = jnp.zeros_like(dk_ref)
            dv_ref[...] = jnp.zeros_like(dv_ref)

        v_all, o_all, l_all, do_all = v_ref[...], o_ref[...], l_ref[...], do_ref[...]
        dv_acc = jnp.zeros((t, 128), F32)
        for h in range(2):
            hs = slice(128 * h, 128 * h + 128)
            mask = lo if h == 0 else jnp.logical_not(lo)
            qh, kh = q_ref[:, hs], k_ref[:, hs]
            s = lax.dot_general(qh, kh, _NT, preferred_element_type=F32) * MLA_SCALE
            lse_h = jnp.max(jnp.where(mask, l_all, -jnp.inf), axis=1, keepdims=True)
            pr = jnp.exp(s - lse_h)
            do_h = jnp.where(mask, do_all, 0.0)
            dp = lax.dot_general(do_h.astype(BF16), v_all, _NT, preferred_element_type=F32)
            dsum = jnp.sum(do_h * o_all, axis=1, keepdims=True)
            ds = (pr * (dp - dsum) * MLA_SCALE).astype(BF16)
            dq_ref[:, hs] = jnp.dot(ds, kh, preferred_element_type=F32)
            dk_ref[:, hs] += lax.dot_general(ds, qh, _TN, preferred_element_type=F32)
            dv_acc = dv_acc + lax.dot_general(pr.astype(BF16), do_h.astype(BF16), _TN, preferred_element_type=F32)
        dv_ref[...] += dv_acc

    qspec = BS((tq, 256), lambda bi, hp, i: (bi * nq + i, hp))
    kspec = BS((t, 256), lambda bi, hp, i: (bi, hp))
    vspec = BS((t, 128), lambda bi, hp, i: (bi, hp))
    ospec = BS((tq, 128), lambda bi, hp, i: (bi * nq + i, hp))
    return pl.pallas_call(
        body, grid=(b, 4, nq), in_specs=[qspec, kspec, vspec, ospec, ospec, ospec], out_specs=[qspec, kspec, vspec],
        out_shape=[SDS((m, 1024), F32), SDS((m, 1024), F32), SDS((m, 512), F32)],
        compiler_params=_cp("parallel", "parallel", "arbitrary"), name="attn_bwd",
    )(q, kf, v, o, lse, do)


def _conv3(u, w_ref, b_ref, t):
    up, un = _prev_next(u, t)
    return w_ref[0:1, :] * up + w_ref[1:2, :] * u + w_ref[2:3, :] * un + b_ref[...], up, un


def _ffn_mid_fwd(ug3, uv3, cw, cb):
    b, t, f = ug3.shape
    nc = f // 256

    def body(ug_ref, uv_ref, wg_ref, wv_ref, bg_ref, bv_ref, a_ref):
        gc, _, _ = _conv3(ug_ref[0], wg_ref, bg_ref, t)
        vc, _, _ = _conv3(uv_ref[0], wv_ref, bv_ref, t)
        a_ref[0] = (gc * jax.nn.sigmoid(gc) * vc).astype(BF16)

    blk = BS((1, t, 256), lambda i, j: (i, 0, j))
    return pl.pallas_call(
        body, grid=(b, nc),
        in_specs=[blk, blk, BS((3, 256), lambda i, j: (0, j)), BS((3, 256), lambda i, j: (0, j + nc)),
                  BS((1, 256), lambda i, j: (0, j)), BS((1, 256), lambda i, j: (0, j + nc))],
        out_specs=blk, out_shape=SDS((b, t, f), BF16), compiler_params=_cp("parallel", "parallel"), name="ffn_mid_fwd",
    )(ug3, uv3, cw, cw, cb, cb)


def _ffn_mid_bwd(ug3, uv3, cw, cb, da3):
    b, t, f = ug3.shape
    nc = f // 256

    def half(u, up, un, dc, w_ref):
        dprev, dnext = _prev_next(dc, t)
        du = w_ref[1:2, :] * dc + w_ref[0:1, :] * dnext + w_ref[2:3, :] * dprev
        sums = [jnp.sum(dc * q, axis=0, keepdims=True) for q in (up, u, un)] + [jnp.sum(dc, axis=0, keepdims=True)]
        row = lax.broadcasted_iota(jnp.int32, (8, 256), 0)
        tab = jnp.zeros((8, 256), F32)
        for i, s in enumerate(sums):
            tab = jnp.where(row == i, s, tab)
        return du, tab

    def body(ug_ref, uv_ref, wg_ref, wv_ref, bg_ref, bv_ref, da_ref, dug_ref, duv_ref, tg_ref, tv_ref):
        ug, uv, da = ug_ref[0], uv_ref[0], da_ref[0]
        gc, gp, gn = _conv3(ug, wg_ref, bg_ref, t)
        vc, vp, vn = _conv3(uv, wv_ref, bv_ref, t)
        sg = jax.nn.sigmoid(gc)
        d_gc = da * vc * (sg * (1.0 + gc * (1.0 - sg)))
        d_vc = da * (gc * sg)
        dug, tg = half(ug, gp, gn, d_gc, wg_ref)
        duv, tv = half(uv, vp, vn, d_vc, wv_ref)
        dug_ref[0] = dug
        duv_ref[0] = duv

        @pl.when(pl.program_id(1) == 0)
        def _():
            tg_ref[...] = jnp.zeros_like(tg_ref)
            tv_ref[...] = jnp.zeros_like(tv_ref)

        tg_ref[...] += tg
        tv_ref[...] += tv

    blk = BS((1, t, 256), lambda j, i: (i, 0, j))
    tab = BS((8, 256), lambda j, i: (0, j))
    return pl.pallas_call(
        body, grid=(nc, b),
        in_specs=[blk, blk, BS((3, 256), lambda j, i: (0, j)), BS((3, 256), lambda j, i: (0, j + nc)),
                  BS((1, 256), lambda j, i: (0, j)), BS((1, 256), lambda j, i: (0, j + nc)), blk],
        out_specs=[blk, blk, tab, tab],
        out_shape=[SDS((b, t, f), F32), SDS((b, t, f), F32), SDS((8, f), F32), SDS((8, f), F32)],
        compiler_params=_cp("parallel", "arbitrary"), name="ffn_mid_bwd",
    )(ug3, uv3, cw, cw, cb, cb, da3)


def _add_rows(parts, name):
    r = parts[0].shape[0]
    tr = _row_tile(r, 1024)
    n = len(parts)

    def body(*refs):
        acc = refs[0][...]
        for q in refs[1:n]:
            acc = acc + q[...]
        refs[n][...] = acc

    row = BS((tr, 128), lambda i: (i, 0))
    return pl.pallas_call(
        body, grid=(r // tr,), in_specs=[row] * n, out_specs=row, out_shape=SDS((r, 128), F32),
        compiler_params=_cp("parallel"), name=name,
    )(*parts)


def _adamw(w, g, m, v, name):
    r, c = w.shape
    tr = _row_tile(r)

    def body(w_ref, g_ref, m_ref, v_ref, d_ref, m2_ref, v2_ref):
        d, m2, v2 = _adamw_math(w_ref[...], g_ref[...], m_ref[...], v_ref[...])
        d_ref[...] = d
        m2_ref[...] = m2
        v2_ref[...] = v2

    blk = BS((tr, c), lambda i: (i, 0))
    return pl.pallas_call(
        body, grid=(r // tr,), in_specs=[blk] * 4, out_specs=[blk] * 3, out_shape=[SDS((r, c), F32)] * 3,
        compiler_params=_cp("parallel"), name=name,
    )(w, g, m, v)


def _place():
    return lax.axis_index("x"), lax.axis_index("y"), lax.axis_index("c")


def _flip(v, bit):
    return 1 - v if bit else v


def _allgather_weights(shard):
    r = shard.shape[0]
    rh = r // 2

    def body(x_ref, out_ref, send_sems, recv_sems, local_sem):
        x, y, c = _place()
        me, sibling = (x, y, c), (x, y, 1 - c)
        chips = [(1 - x, y), (x, 1 - y), (1 - x, 1 - y)]
        mine_src = x_ref.at[pl.ds(c * rh, rh), :]

        def rows(px, py, pc):
            return out_ref.at[pl.ds((4 * px + 2 * py + pc) * rh, rh), :]

        def copy(k, block, to, src=None):
            return pltpu.make_async_remote_copy(
                src_ref=rows(*block) if src is None else src, dst_ref=rows(*block), send_sem=send_sems.at[k],
                recv_sem=recv_sems.at[k], device_id=to, device_id_type=MESH,
            )

        mine = pltpu.make_async_copy(mine_src, rows(*me), local_sem)
        mine.start()
        first = [copy(0, me, sibling, src=mine_src)]
        first += [copy(1 + j, me, (*chip, c), src=mine_src) for j, chip in enumerate(chips)]
        for cp in first:
            cp.start()
        passed = [copy(4 + j, (*chip, c), sibling) for j, chip in enumerate(chips)]
        for j, chip in enumerate(chips):
            copy(1 + j, (*chip, c), me).wait_recv()
            passed[j].start()
        copy(0, sibling, me).wait_recv()
        for j, chip in enumerate(chips):
            copy(4 + j, (*chip, 1 - c), me).wait_recv()
        for cp in first + passed:
            cp.wait_send()
        mine.wait()

    return pl.pallas_call(
        body, out_shape=SDS((8 * rh, 128), shard.dtype), in_specs=[ANY], out_specs=ANY,
        scratch_shapes=[pltpu.SemaphoreType.DMA((7,)), pltpu.SemaphoreType.DMA((7,)), pltpu.SemaphoreType.DMA],
        name="allgather_weights",
    )(shard)


def _scatter_to_chips(g):
    def body(g_ref, recv_ref, send_sems, recv_sems, local_sem):
        x, y, c = _place()
        copies = []
        for j, (fx, fy) in enumerate(((1, 0), (0, 1), (1, 1))):
            px, py = _flip(x, fx), _flip(y, fy)
            cp = pltpu.make_async_remote_copy(
                src_ref=g_ref.at[2 * px + py], dst_ref=recv_ref.at[j], send_sem=send_sems.at[j], recv_sem=recv_sems.at[j],
                device_id=(px, py, c), device_id_type=MESH,
            )
            cp.start()
            copies.append(cp)
        own = pltpu.make_async_copy(g_ref.at[2 * x + y], recv_ref.at[3], local_sem)
        own.start()
        for cp in copies:
            cp.wait_recv()
        for cp in copies:
            cp.wait_send()
        own.wait()

    return pl.pallas_call(
        body, out_shape=SDS(g.shape, g.dtype), in_specs=[ANY], out_specs=ANY,
        scratch_shapes=[pltpu.SemaphoreType.DMA((3,)), pltpu.SemaphoreType.DMA((3,)), pltpu.SemaphoreType.DMA],
        name="scatter_grads",
    )(g)


def _swap_with_sibling(a):
    def body(a_ref, b_ref, send_sem, recv_sem):
        x, y, c = _place()
        cp = pltpu.make_async_remote_copy(
            src_ref=a_ref, dst_ref=b_ref, send_sem=send_sem, recv_sem=recv_sem, device_id=(x, y, 1 - c), device_id_type=MESH
        )
        cp.start()
        cp.wait()

    return pl.pallas_call(
        body, out_shape=SDS(a.shape, a.dtype), in_specs=[ANY], out_specs=ANY,
        scratch_shapes=[pltpu.SemaphoreType.DMA, pltpu.SemaphoreType.DMA], name="swap_sibling",
    )(a)


def _allreduce_small(v):
    r = v.shape[0]

    def body(v_ref, out_ref, buf_ref, send_sems, recv_sems):
        x, y, c = _place()
        buf_ref[0] = v_ref[...]
        copies = []
        for k in range(1, 8):
            peer = (_flip(x, k >> 2 & 1), _flip(y, k >> 1 & 1), _flip(c, k & 1))
            cp = pltpu.make_async_remote_copy(
                src_ref=v_ref, dst_ref=buf_ref.at[k], send_sem=send_sems.at[k - 1], recv_sem=recv_sems.at[k - 1],
                device_id=peer, device_id_type=MESH,
            )
            cp.start()
            copies.append(cp)
        for cp in copies:
            cp.wait_recv()
        acc = None
        for d in range(8):
            slot = 4 * _flip(x, d >> 2 & 1) + 2 * _flip(y, d >> 1 & 1) + _flip(c, d & 1)
            term = buf_ref[slot]
            acc = term if acc is None else acc + term
        out_ref[...] = acc
        for cp in copies:
            cp.wait_send()

    return pl.pallas_call(
        body, out_shape=SDS(v.shape, F32), in_specs=[VMEM], out_specs=VMEM,
        scratch_shapes=[pltpu.VMEM((8, r, 128), F32), pltpu.SemaphoreType.DMA((7,)), pltpu.SemaphoreType.DMA((7,))],
        name="allreduce_small",
    )(v)


_BIG = (
    ("w_in", 1, False), ("decay_w2_fwd", 1, False), ("decay_w2_bwd", 1, False), ("iclr_a2_fwd", 1, False),
    ("iclr_a2_bwd", 1, False), ("gate_g2", 1, False), ("w_uq", 0, False), ("w_ukv", 1, False), ("w_out", 0, False),
    ("w_ffn_up", 1, False), ("ffn_conv_w", 1, True), ("w_ffn_down", 0, False),
)
_SMALL = (
    "ln_mix_g", "shift_mu_prev", "shift_mu_next", "decay_w0_fwd", "decay_w0_bwd", "iclr_a0_fwd", "iclr_a0_bwd", "k_k",
    "k_a", "r_k", "ln_x_g", "ln_x_b", "q_norm_g", "kv_norm_g", "mla_out_g", "ln_ffn_g", "ffn_conv_b", "ln_final_g",
)
_WEIGHTS = (
    "ln_mix_g", "w_in", "shift_mu_prev", "shift_mu_next", "decay_w0_fwd", "decay_w2_fwd", "decay_w0_bwd", "decay_w2_bwd",
    "iclr_a0_fwd", "iclr_a2_fwd", "iclr_a0_bwd", "iclr_a2_bwd", "gate_g2", "k_k", "k_a", "r_k", "ln_x_g", "ln_x_b",
    "q_norm_g", "w_uq", "kv_norm_g", "w_ukv", "mla_out_g", "w_out", "ln_ffn_g", "w_ffn_up", "ffn_conv_w", "ffn_conv_b",
    "w_ffn_down", "ln_final_g",
)


def _pad_rows(flat, rows):
    return jnp.pad(flat, (0, rows * 128 - flat.shape[0])).reshape(rows, 128)


def _rows_for(n, mult):
    rows = -(-n // 128)
    return -(-rows // mult) * mult


def _pack_shards_bf16(arrs):
    parts = []
    for name, _, raw in _BIG:
        w = arrs[name][0]
        flat = lax.bitcast_convert_type(w, BF16).reshape(-1) if raw else w.astype(BF16).reshape(-1)
        parts.append(_pad_rows(flat, _rows_for(flat.shape[0], 32)))
    return jnp.concatenate(parts, axis=0)


def _unpack_gathered(g4, arrs):
    out, off = {}, 0
    for name, axis, raw in _BIG:
        a, b = arrs[name].shape[1:]
        n = a * b * (2 if raw else 1)
        rows = _rows_for(n, 32)
        seg = g4[:, off:off + rows].reshape(4, rows * 128)[:, :n]
        off += rows
        if raw:
            seg = lax.bitcast_convert_type(seg.reshape(4, a * b, 2), F32)
        seg = seg.reshape(4, a, b)
        out[name] = jnp.concatenate([seg[s] for s in range(4)], axis=1) if axis == 1 else seg.reshape(4 * a, b)
    return out


def _pack_grads(full, arrs):
    parts = []
    for name, axis, _ in _BIG:
        a, b = arrs[name].shape[1:]
        g = full[name]
        sh = g.reshape(a, 4, b).transpose(1, 0, 2) if axis == 1 else g.reshape(4, a, b)
        rows = _rows_for(a * b, 8)
        parts.append(jnp.pad(sh.reshape(4, a * b), ((0, 0), (0, rows * 128 - a * b))).reshape(4, rows, 128))
    packed = jnp.concatenate(parts, axis=1)
    total = packed.shape[1]
    return jnp.pad(packed, ((0, 0), (0, -(-total // 512) * 512 - total), (0, 0)))


def _unpack_grads(g, arrs):
    out, off = {}, 0
    for name, _, _ in _BIG:
        a, b = arrs[name].shape[1:]
        rows = _rows_for(a * b, 8)
        out[name] = g[off:off + rows].reshape(-1)[:a * b].reshape(a, b)
        off += rows
    return out


def _pack_small(vals):
    flat = jnp.concatenate([vals[n].reshape(-1).astype(F32) for n in _SMALL] + [vals["_loss"].reshape(-1)])
    return _pad_rows(flat, _rows_for(flat.shape[0], 8))


def _unpack_small(buf, arrs):
    flat, out, off = buf.reshape(-1), {}, 0
    for n in _SMALL:
        size = arrs[n].size
        out[n] = flat[off:off + size].reshape(arrs[n].shape)
        off += size
    out["_loss"] = flat[off]
    return out


def _rot_cols(w):
    return jnp.concatenate([-w[..., 16:], w[..., :16]], axis=-1)


def _rot_cols_t(g):
    return jnp.concatenate([g[..., 16:], -g[..., :16]], axis=-1)


def _rope_tables(t):
    inv = jnp.power(ROPE_THETA, -jnp.arange(0, ROPE_DIM, 2, dtype=F32) / ROPE_DIM)
    ang = jnp.arange(t, dtype=F32)[:, None] * inv[None, :]
    one, zero = jnp.ones((t, 64), F32), jnp.zeros((t, 64), F32)
    cs = jnp.concatenate([one, jnp.cos(ang), jnp.cos(ang), zero[:, :32]], axis=1)
    sn = jnp.concatenate([zero, jnp.sin(ang), jnp.sin(ang), zero[:, :32]], axis=1)
    return cs, sn


def _block_diag(a, b):
    za = jnp.zeros_like(a)
    return jnp.concatenate([jnp.concatenate([a, za], axis=1), jnp.concatenate([za, b], axis=1)], axis=0)


def kernel(x, ln_mix_g, w_in, shift_mu_prev, shift_mu_next, decay_w0_fwd, decay_w2_fwd, decay_w0_bwd, decay_w2_bwd, iclr_a0_fwd, iclr_a2_fwd, iclr_a0_bwd, iclr_a2_bwd, gate_g2, k_k, k_a, r_k, ln_x_g, ln_x_b, q_norm_g, w_uq, kv_norm_g, w_ukv, mla_out_g, w_out, ln_ffn_g, w_ffn_up, ffn_conv_w, ffn_conv_b, w_ffn_down, ln_final_g, loss_target, m_ln_mix_g, m_w_in, m_shift_mu_prev, m_shift_mu_next, m_decay_w0_fwd, m_decay_w2_fwd, m_decay_w0_bwd, m_decay_w2_bwd, m_iclr_a0_fwd, m_iclr_a2_fwd, m_iclr_a0_bwd, m_iclr_a2_bwd, m_gate_g2, m_k_k, m_k_a, m_r_k, m_ln_x_g, m_ln_x_b, m_q_norm_g, m_w_uq, m_kv_norm_g, m_w_ukv, m_mla_out_g, m_w_out, m_ln_ffn_g, m_w_ffn_up, m_ffn_conv_w, m_ffn_conv_b, m_w_ffn_down, m_ln_final_g, v_ln_mix_g, v_w_in, v_shift_mu_prev, v_shift_mu_next, v_decay_w0_fwd, v_decay_w2_fwd, v_decay_w0_bwd, v_decay_w2_bwd, v_iclr_a0_fwd, v_iclr_a2_fwd, v_iclr_a0_bwd, v_iclr_a2_bwd, v_gate_g2, v_k_k, v_k_a, v_r_k, v_ln_x_g, v_ln_x_b, v_q_norm_g, v_w_uq, v_kv_norm_g, v_w_ukv, v_mla_out_g, v_w_out, v_ln_ffn_g, v_w_ffn_up, v_ffn_conv_w, v_ffn_conv_b, v_w_ffn_down, v_ln_final_g):
    arrs = dict(locals())
    b, t, d = x.shape
    m = b * t
    x2 = x.reshape(m, d)
    tgt = loss_target.reshape(m, d)
    vec = lambda n: arrs[n].reshape(1, -1)

    gathered = _allgather_weights(_pack_shards_bf16(arrs))
    fw = _unpack_gathered(gathered.reshape(N_CHIPS, -1, 128), arrs)
    win = fw["w_in"]
    zc = jnp.zeros((d, 64), BF16)
    w_kr = win[:, 2944:2976]
    win_m = jnp.concatenate([win[:, 1920:2944], zc, w_kr, zc[:, :32], zc, _rot_cols(w_kr), zc[:, :32]], axis=1)
    win_r = win[:, :RWKV_COLS]
    uq = fw["w_uq"].astype(F32).reshape(Q_RANK, HEADS, 96)
    z32 = jnp.zeros((Q_RANK, HEADS, 32), F32)
    wq = jnp.concatenate([uq[..., :64], uq[..., 64:], z32], axis=-1).reshape(Q_RANK, 1024)
    wqr = jnp.concatenate([z32, z32, _rot_cols(uq[..., 64:]), z32], axis=-1).reshape(Q_RANK, 1024)
    ukv = fw["w_ukv"].astype(F32).reshape(KV_RANK, HEADS, 128)
    wk = jnp.concatenate([ukv[..., :64], jnp.zeros_like(ukv[..., :64])], axis=-1).reshape(KV_RANK, 1024)
    wv = ukv[..., 64:].reshape(KV_RANK, 512)
    head = jnp.arange(512) // HEAD_DIM
    rw = dict(
        w0=jnp.concatenate([vec("decay_w0_fwd"), vec("decay_w0_bwd")], axis=1),
        w2=_block_diag(fw["decay_w2_fwd"], fw["decay_w2_bwd"]).astype(F32),
        a0=jnp.concatenate([vec("iclr_a0_fwd"), vec("iclr_a0_bwd")], axis=1),
        a2=_block_diag(fw["iclr_a2_fwd"], fw["iclr_a2_bwd"]).astype(F32),
        g2=fw["gate_g2"].astype(F32), k_k=vec("k_k"), k_a=vec("k_a"), r_k=vec("r_k"), ln_x_g=vec("ln_x_g"), ln_x_b=vec("ln_x_b"),
        ones_bd=(head[:, None] == head[None, :]).astype(F32),
    )
    mp = dict(q_norm_g=vec("q_norm_g"), kv_norm_g=vec("kv_norm_g"), wq=wq, wqr=wqr, wk=wk, wv=wv)
    cs, sn = _rope_tables(t)
    w_up_g, w_up_v = fw["w_ffn_up"][:, :D_FF], fw["w_ffn_up"][:, D_FF:]
    cw, cb = fw["ffn_conv_w"], vec("ffn_conv_b")

    n1 = _rms_fwd(x2, vec("ln_mix_g"), "rms_mix")
    zm = _mm(n1, win_m, "nn", "proj_in_mla")
    zr = _mm(n1, win_r, "nn", "proj_in_rwkv")
    zs = _shift_fwd(zr.reshape(b, t, RWKV_COLS), vec("shift_mu_prev"), vec("shift_mu_next"))
    zs2 = zs.reshape(m, RWKV_COLS)
    wf, wb, kf, kb, kk, kaf, kab, gate = _prep_fwd(zs2, rw)
    r4 = lambda a: a.reshape(b, t, 1, 512)
    f2 = lambda a: a.reshape(m, 512)
    rr, kk4, vv = r4(zs[:, :, :512]), r4(kk), r4(zs[:, :, 1024:1536])
    ops_f = (r4(wf), r4(kf), r4(kaf))
    ops_b = (r4(wb), r4(kb), r4(kab))
    y_f, hist_f, y_b, hist_b = _scan_fwd(rr, kk4, vv, ops_f, ops_b)
    y_f, y_b = f2(y_f), f2(y_b)
    y_rwkv = _post_fwd(y_f, y_b, zs2, kf, kb, gate, rw)
    q, kfull, v = _mla_fwd(zm, cs, sn, mp, t)
    o, lse = _attn_fwd(q, kfull, v, b, t)
    y_mla = _rms_fwd(o, vec("mla_out_g"), "rms_mla_out")
    ymix = jnp.concatenate([y_rwkv, y_mla], axis=1)
    h1 = _mm(ymix, fw["w_out"], "nn", "proj_out", add=x2)
    n2 = _rms_fwd(h1, vec("ln_ffn_g"), "rms_ffn")
    ug = _mm(n2, w_up_g, "nn", "ffn_up_gate")
    uv = _mm(n2, w_up_v, "nn", "ffn_up_val")
    r3f = lambda a: a.reshape(b, t, D_FF)
    act = _ffn_mid_fwd(r3f(ug), r3f(uv), cw, cb).reshape(m, D_FF)
    h2 = _mm(act, fw["w_ffn_down"], "nn", "ffn_down", add=h1)
    loss_tab, dh2, g_ln_final = _final(h2, vec("ln_final_g"), tgt)

    gfull = {}
    dact = _mm(dh2, fw["w_ffn_down"], "nt", "d_ffn_act")
    gfull["w_ffn_down"] = _mm(act, dh2, "tn", "g_ffn_down")
    dug, duv, tab_g, tab_v = _ffn_mid_bwd(r3f(ug), r3f(uv), cw, cb, r3f(dact))
    dug, duv = dug.reshape(m, D_FF), duv.reshape(m, D_FF)
    gfull["ffn_conv_w"] = jnp.concatenate([tab_g[0:3], tab_v[0:3]], axis=1)
    g_conv_b = jnp.concatenate([tab_g[3:4], tab_v[3:4]], axis=1)
    dn2 = _mm(duv, w_up_v, "nt", "d_ffn_in_val", add=_mm(dug, w_up_g, "nt", "d_ffn_in_gate"))
    gfull["w_ffn_up"] = jnp.concatenate([_mm(n2, dug, "tn", "g_ffn_up_gate"), _mm(n2, duv, "tn", "g_ffn_up_val")], axis=1)
    dh1, g_ln_ffn = _rms_bwd(h1, vec("ln_ffn_g"), dn2, "rms_ffn_bwd", dres=dh2)
    dymix = _mm(dh1, fw["w_out"], "nt", "d_mix")
    gfull["w_out"] = _mm(ymix, dh1, "tn", "g_w_out")
    do, g_mla_out = _rms_bwd(o, vec("mla_out_g"), dymix, "rms_mla_out_bwd", dy_block=1)
    dq, dk, dv = _attn_bwd(q, kfull, v, o, lse, do, b, t)
    dzm, g_qn, g_kvn, g_wq, g_wqr, g_wk, g_wv = _mla_bwd(zm, cs, sn, mp, t, dq, dk, dv)
    gq3, gqr3 = g_wq.reshape(Q_RANK, HEADS, 128), g_wqr.reshape(Q_RANK, HEADS, 128)
    gfull["w_uq"] = jnp.concatenate(
        [gq3[..., :64], gq3[..., 64:96] + _rot_cols_t(gqr3[..., 64:96])], axis=-1
    ).reshape(Q_RANK, HEADS * 96)
    gfull["w_ukv"] = jnp.concatenate(
        [g_wk.reshape(KV_RANK, HEADS, 128)[..., :64], g_wv.reshape(KV_RANK, HEADS, 64)], axis=-1
    ).reshape(KV_RANK, 1024)
    dys, dr_p, dk_p, dv_p, dgate, g_rk, g_lnx_g, g_lnx_b = _post_bwd(y_f, y_b, zs2, kf, kb, gate, rw, dymix)
    (dr_f, dwf, dkf, dkk_f, dkaf, dv_f, dr_b, dwb, dkb, dkk_b, dkab, dv_b) = _scan_bwd(
        rr, kk4, vv, r4(dys), ops_f, hist_f, ops_b, hist_b)
    cts = dict(dwf=f2(dwf), dwb=f2(dwb), dkf=f2(dkf), dkb=f2(dkb), dkk_f=f2(dkk_f), dkk_b=f2(dkk_b), dkaf=f2(dkaf), dkab=f2(dkab),
               dr_f=f2(dr_f), dr_b=f2(dr_b), dr_p=dr_p, dk_p=dk_p, dv_p=dv_p, dg=dgate, dv_f=f2(dv_f), dv_b=f2(dv_b))
    dzs, g_w0, g_w2, g_a0, g_a2, g_g2, g_kk, g_ka = _prep_bwd(zs2, rw, cts)
    dzr, g_mu_p, g_mu_n = _shift_bwd(dzs.reshape(b, t, RWKV_COLS), zr.reshape(b, t, RWKV_COLS), vec("shift_mu_prev"), vec("shift_mu_next"))
    dzr = dzr.reshape(m, RWKV_COLS)
    gfull["decay_w2_fwd"], gfull["decay_w2_bwd"] = g_w2[:64, :512], g_w2[64:, 512:]
    gfull["iclr_a2_fwd"], gfull["iclr_a2_bwd"] = g_a2[:64, :512], g_a2[64:, 512:]
    gfull["gate_g2"] = g_g2
    dn1 = _mm(dzr, win_r, "nt", "d_proj_in_rwkv", add=_mm(dzm, win_m, "nt", "d_proj_in_mla"))
    g_m = _mm(n1, dzm, "tn", "g_w_in_mla")
    g_r = _mm(n1, dzr, "tn", "g_w_in_rwkv")
    g_kr = g_m[:, 1088:1120] + _rot_cols_t(g_m[:, 1216:1248])
    gfull["w_in"] = jnp.concatenate([g_r, g_m[:, :1024], g_kr], axis=1)
    dx, g_ln_mix = _rms_bwd(x2, vec("ln_mix_g"), dn1, "rms_mix_bwd", dres=dh1)

    recv = _scatter_to_chips(_pack_grads(gfull, arrs))
    mine = _add_rows([recv[3], recv[0], recv[1], recv[2]], "sum_chips")
    g_big = _unpack_grads(_add_rows([mine, _swap_with_sibling(mine)], "sum_cores"), arrs)
    small = {
        "ln_mix_g": g_ln_mix, "shift_mu_prev": g_mu_p, "shift_mu_next": g_mu_n, "decay_w0_fwd": g_w0[:, :512],
        "decay_w0_bwd": g_w0[:, 512:], "iclr_a0_fwd": g_a0[:, :512], "iclr_a0_bwd": g_a0[:, 512:], "k_k": g_kk, "k_a": g_ka,
        "r_k": g_rk, "ln_x_g": g_lnx_g, "ln_x_b": g_lnx_b, "q_norm_g": g_qn, "kv_norm_g": g_kvn, "mla_out_g": g_mla_out,
        "ln_ffn_g": g_ln_ffn, "ffn_conv_b": g_conv_b, "ln_final_g": g_ln_final,
        "_loss": jnp.pad(loss_tab[0, 0:1], (0, 127)),
    }
    g_small_buf = _allreduce_small(_pack_small(small))
    g_small = _unpack_small(g_small_buf, arrs)

    grads, deltas, new_m, new_v = {}, {}, {}, {}
    for name, _, _ in _BIG:
        shape = arrs[name].shape
        two = lambda a: a.reshape(shape[1:])
        dlt, m2, v2 = _adamw(two(arrs[name]), g_big[name], two(arrs["m_" + name]), two(arrs["v_" + name]), "adamw_" + name)
        grads[name] = g_big[name].reshape(shape)
        deltas[name], new_m[name], new_v[name] = dlt.reshape(shape), m2.reshape(shape), v2.reshape(shape)
    pk = lambda pre: _pack_small({**{n: arrs[pre + n] for n in _SMALL}, "_loss": jnp.zeros((128,), F32)})
    sd, sm, sv = _adamw(pk(""), g_small_buf, pk("m_"), pk("v_"), "adamw_small")
    sd, sm, sv = _unpack_small(sd, arrs), _unpack_small(sm, arrs), _unpack_small(sv, arrs)
    for n in _SMALL:
        grads[n], deltas[n], new_m[n], new_v[n] = g_small[n], sd[n], sm[n], sv[n]

    return (g_small["_loss"], dx.reshape(b, t, d), *[grads[n] for n in _WEIGHTS], *[deltas[n] for n in _WEIGHTS],
            *[new_m[n] for n in _WEIGHTS], *[new_v[n] for n in _WEIGHTS])
```

```python
import functools
import math

import jax
import jax.numpy as jnp
from jax import lax
from jax.experimental import pallas as pl
from jax.experimental.pallas import tpu as pltpu

F32, BF16 = jnp.float32, jnp.bfloat16
HIGHEST = lax.Precision.HIGHEST
MESH = pl.DeviceIdType.MESH
ANY = pl.BlockSpec(memory_space=pl.ANY)
VMEM = pl.BlockSpec(memory_space=pltpu.VMEM)
BS = pl.BlockSpec
SDS = jax.ShapeDtypeStruct

NORM_EPS = 1e-6
GN_EPS = 64e-5
L2_EPS = 1e-12
HEADS = 8
HEAD_DIM = 64
D_RWKV = HEADS * HEAD_DIM
ROPE_DIM = 32
ROPE_THETA = 10000.0
MLA_SCALE = (64 + ROPE_DIM) ** -0.5
Q_RANK, KV_RANK = 768, 256
RWKV_COLS = 1920
MLA_PAD_COLS = Q_RANK + KV_RANK + 256
D_FF = 2816
ADAM_LR, ADAM_B1, ADAM_B2, ADAM_EPS, ADAM_WD, ADAM_STEP = 0.001, 0.9, 0.999, 1e-08, 0.01, 10

V7X_LANES = 128
V7X_VMEM_LIMIT = 48 * 1024 * 1024
SCAN_CHUNK = 16
N_CHIPS = 4


def _cp(*sem):
    return pltpu.CompilerParams(dimension_semantics=sem, vmem_limit_bytes=V7X_VMEM_LIMIT)


def _tile(n, cands=(512, 640, 384, 256, 128)):
    for c in cands:
        if n % c == 0:
            return c
    return n


def _row_tile(n, cap=256):
    best = n
    for t in range(8, cap + 1, 8):
        if n % t == 0:
            best = t
    return best if best <= cap or n <= cap else n


def _rms(x, g):
    ms = jnp.mean(x * x, axis=-1, keepdims=True)
    return x * lax.rsqrt(ms + NORM_EPS) * g


@jax.custom_vjp
def _bdot(x, w):
    return jnp.dot(x.astype(BF16), w.astype(BF16), preferred_element_type=F32)


def _bdot_fwd(x, w):
    return _bdot(x, w), (x, w)


def _bdot_bwd(res, ct):
    x, w = res
    c = ct.astype(BF16)
    dx = lax.dot_general(c, w.astype(BF16), (((1,), (1,)), ((), ())), preferred_element_type=F32)
    dw = lax.dot_general(x.astype(BF16), c, (((0,), (0,)), ((), ())), preferred_element_type=F32)
    return dx.astype(x.dtype), dw.astype(w.dtype)


_bdot.defvjp(_bdot_fwd, _bdot_bwd)


def _headsum(x, ones_bd):
    return jnp.dot(x, ones_bd, precision=HIGHEST, preferred_element_type=F32)


def _prep_fn(zs, w0, w2, a0, a2, g2, k_k, k_a, ones_bd):
    k = zs[:, 512:1024]
    wd = zs[:, 1536:1664]
    ad = zs[:, 1664:1792]
    gd = zs[:, 1792:1920]
    logit = w0 + _bdot(jnp.tanh(wd), w2)
    w = jnp.exp(-math.exp(-0.5) * jax.nn.sigmoid(logit))
    a = jax.nn.sigmoid(a0 + _bdot(ad, a2))
    g = _bdot(jax.nn.sigmoid(gd), g2)
    kkr = k * k_k
    nrm = jnp.sqrt(_headsum(kkr * kkr, ones_bd))
    kk = kkr / jnp.maximum(nrm, L2_EPS)
    a_f, a_b = a[:, :512], a[:, 512:]
    kf = k * (1.0 + (a_f - 1.0) * k_a)
    kb = k * (1.0 + (a_b - 1.0) * k_a)
    return w[:, :512], w[:, 512:], kf, kb, kk, kk * a_f, kk * a_b, g


def _post_fn(y, r, kf, kb, v, g, r_k, ln_g, ln_b, ones_bd):
    mu =_headsum(y, ones_bd) * (1.0 / HEAD_DIM)
    yc = y - mu
    var = _headsum(yc * yc, ones_bd) * (1.0 / HEAD_DIM)
    yn = yc * lax.rsqrt(var + GN_EPS) * ln_g + ln_b
    bonus = _headsum(r * (kf + kb) * r_k, ones_bd) * v
    return (yn + bonus) * g


def _cat8(x):
    return jnp.concatenate([x] * HEADS, axis=1)


def _mla_fn(zm, cs, sn, gq, gkv, wq, wqr, wk, wv):
    cq = zm[:, :Q_RANK]
    ckv = zm[:, Q_RANK:Q_RANK + KV_RANK]
    kr = zm[:, Q_RANK + KV_RANK:Q_RANK + KV_RANK + 128]
    krr = zm[:, Q_RANK + KV_RANK + 128:]
    cqn = _rms(cq, gq)
    ckvn = _rms(ckv, gkv)
    q = _bdot(cqn, wq) * _cat8(cs) + _bdot(cqn, wqr) * _cat8(sn)
    kro = kr * cs + krr * sn
    kfull = _bdot(ckvn, wk) + _cat8(kro)
    v = _bdot(ckvn, wv)
    return q, kfull, v


def _adamw_math(w, g, m, v):
    m2 = ADAM_B1 * m + (1.0 - ADAM_B1) * g
    v2 = ADAM_B2 * v + (1.0 - ADAM_B2) * (g * g)
    m_hat = m2 / (1.0 - ADAM_B1 ** ADAM_STEP)
    v_hat = v2 / (1.0 - ADAM_B2 ** ADAM_STEP)
    delta = -ADAM_LR * (m_hat / (jnp.sqrt(v_hat) + ADAM_EPS) + ADAM_WD * w)
    return delta, m2, v2


_DIMS = {"nn": (((1,), (0,)), ((), ())), "nt": (((1,), (1,)), ((), ())), "tn": (((0,), (0,)), ((), ()))}


def _mm(a, b, mode, name, out_dtype=F32, add=None):
    if mode == "nn":
        (m, k), (_, n) = a.shape, b.shape
    elif mode == "nt":
        (m, k), (n, _) = a.shape, b.shape
    else:
        (k, m), (_, n) = a.shape, b.shape
    tm, tn, tk = _tile(m), _tile(n), _tile(k)
    nk = k // tk

    def body(a_ref, b_ref, *rest):
        if add is None:
            o_ref, acc_ref = rest
        else:
            add_ref, o_ref, acc_ref = rest
        kk = pl.program_id(2)

        @pl.when(kk == 0)
        def _():
            acc_ref[...] = jnp.zeros_like(acc_ref)

        acc_ref[...] += lax.dot_general(
            a_ref[...].astype(BF16), b_ref[...].astype(BF16), _DIMS[mode], preferred_element_type=F32
        )

        @pl.when(kk == nk - 1)
        def _():
            r = acc_ref[...]
            if add is not None:
                r = r + add_ref[...]
            o_ref[...] = r.astype(out_dtype)

    a_spec = BS((tk, tm), lambda i, j, kk: (kk, i)) if mode == "tn" else BS((tm, tk), lambda i, j, kk: (i, kk))
    b_spec = BS((tn, tk), lambda i, j, kk: (j, kk)) if mode == "nt" else BS((tk, tn), lambda i, j, kk: (kk, j))
    o_spec = BS((tm, tn), lambda i, j, kk: (i, j))
    ins, specs = [a, b], [a_spec, b_spec]
    if add is not None:
        ins.append(add)
        specs.append(o_spec)
    return pl.pallas_call(
        body, grid=(m // tm, n // tn, nk), in_specs=specs, out_specs=o_spec, out_shape=SDS((m, n), out_dtype),
        scratch_shapes=[pltpu.VMEM((tm, tn), F32)], compiler_params=_cp("parallel", "parallel", "arbitrary"), name=name,
    )(*ins)


def _rms_fwd(x, g, name):
    m, d = x.shape
    tr = _tile(m)

    def body(x_ref, g_ref, o_ref):
        o_ref[...] = _rms(x_ref[...], g_ref[...]).astype(BF16)

    return pl.pallas_call(
        body, grid=(m // tr,), in_specs=[BS((tr, d), lambda i: (i, 0)), BS((1, d), lambda i: (0, 0))],
        out_specs=BS((tr, d), lambda i: (i, 0)), out_shape=SDS((m, d), BF16), compiler_params=_cp("parallel"), name=name,
    )(x, g)


def _rms_bwd(x, g, dy, name, dres=None, dy_block=0):
    m, d = x.shape
    tr = _row_tile(m)

    def body(x_ref, g_ref, dy_ref, *rest):
        if dres is None:
            dx_ref, dg_ref = rest
        else:
            dres_ref, dx_ref, dg_ref = rest
        _, vjp = jax.vjp(_rms, x_ref[...], g_ref[...])
        dx, dg = vjp(dy_ref[...])
        if dres is not None:
            dx = dx + dres_ref[...]
        dx_ref[...] = dx

        @pl.when(pl.program_id(0) == 0)
        def _():
            dg_ref[...] = jnp.zeros_like(dg_ref)

        dg_ref[...] += dg

    row = BS((tr, d), lambda i: (i, 0))
    vec = BS((1, d), lambda i: (0, 0))
    ins, specs = [x, g, dy], [row, vec, BS((tr, d), lambda i: (i, dy_block))]
    if dres is not None:
        ins.append(dres)
        specs.append(row)
    return pl.pallas_call(
        body, grid=(m // tr,), in_specs=specs, out_specs=[row, vec], out_shape=[SDS((m, d), F32), SDS((1, d), F32)],
        compiler_params=_cp("arbitrary"), name=name,
    )(*ins)


def _final(h, g, tgt):
    m, d = h.shape
    tr = _row_tile(m)

    def loss_fn(hh, gg, tt):
        e = _rms(hh, gg) - tt
        return 0.5 * jnp.sum(e * e) * (1.0 / d)

    def body(h_ref, g_ref, t_ref, l_ref, dh_ref, dg_ref):
        val, (dh, dg) = jax.value_and_grad(loss_fn, argnums=(0, 1))(h_ref[...], g_ref[...], t_ref[...])
        dh_ref[...] = dh

        @pl.when(pl.program_id(0) == 0)
        def _():
            dg_ref[...] = jnp.zeros_like(dg_ref)
            l_ref[...] = jnp.zeros_like(l_ref)

        dg_ref[...] += dg
        l_ref[...] += jnp.full(l_ref.shape, val, F32)

    row = BS((tr, d), lambda i: (i, 0))
    vec = BS((1, d), lambda i: (0, 0))
    return pl.pallas_call(
        body, grid=(m // tr,), in_specs=[row, vec, row], out_specs=[BS((8, 128), lambda i: (0, 0)), row, vec],
        out_shape=[SDS((8, 128), F32), SDS((m, d), F32), SDS((1, d), F32)], compiler_params=_cp("arbitrary"), name="final_loss",
    )(h, g, tgt)


def _prev_next(z, t):
    row = lax.broadcasted_iota(jnp.int32, z.shape, 0)
    zp = jnp.where(row == 0, 0.0, pltpu.roll(z, 1, axis=0))
    zn = jnp.where(row == t - 1, 0.0, pltpu.roll(z, t - 1, axis=0))
    return zp, zn


def _shift_fwd(z3, mu_p, mu_n):
    b, t, c = z3.shape
    nc = c // 128

    def body(z_ref, mp_ref, mn_ref, o_ref):
        z = z_ref[0]
        zp, zn = _prev_next(z, t)
        o_ref[0] = z + mp_ref[...] * (zp - z) + mn_ref[...] * (zn - z)

    blk = BS((1, t, 128), lambda i, j: (i, 0, j))
    vec = BS((1, 128), lambda i, j: (0, j))
    return pl.pallas_call(
        body, grid=(b, nc), in_specs=[blk, vec, vec], out_specs=blk, out_shape=SDS((b, t, c), F32),
        compiler_params=_cp("parallel", "parallel"), name="shift_fwd",
    )(z3, mu_p, mu_n)


def _shift_bwd(dzs3, z3, mu_p, mu_n):
    b, t, c = z3.shape
    nc = c // 128

    def body(d_ref, z_ref, mp_ref, mn_ref, dz_ref, dmp_ref, dmn_ref):
        d, z = d_ref[0], z_ref[0]
        mp, mn = mp_ref[...], mn_ref[...]
        zp, zn = _prev_next(z, t)
        _, dp_next = _prev_next(d * mp, t)
        dn_prev, _ = _prev_next(d * mn, t)
        dz_ref[0] = d * (1.0 - mp - mn) + dp_next + dn_prev

        @pl.when(pl.program_id(1) == 0)
        def _():
            dmp_ref[...] = jnp.zeros_like(dmp_ref)
            dmn_ref[...] = jnp.zeros_like(dmn_ref)

        dmp_ref[...] += jnp.sum(d * (zp - z), axis=0, keepdims=True)
        dmn_ref[...] += jnp.sum(d * (zn - z), axis=0, keepdims=True)

    blk = BS((1, t, 128), lambda j, i: (i, 0, j))
    vec = BS((1, 128), lambda j, i: (0, j))
    return pl.pallas_call(
        body, grid=(nc, b), in_specs=[blk, blk, vec, vec], out_specs=[blk, vec, vec],
        out_shape=[SDS((b, t, c), F32), SDS((1, c), F32), SDS((1, c), F32)],
        compiler_params=_cp("parallel", "arbitrary"), name="shift_bwd",
    )(dzs3, z3, mu_p, mu_n)


def _const(shape):
    nd = len(shape)
    return BS(shape, lambda i: (0,) * nd)


def _prep_fwd(zs, p):
    m = zs.shape[0]
    tr = 256
    params = [p["w0"], p["w2"], p["a0"], p["a2"], p["g2"], p["k_k"], p["k_a"], p["ones_bd"]]

    def body(zs_ref, w0, w2, a0, a2, g2, kk_, ka_, bd, wf, wb, kf, kb, kk, kaf, kab, g):
        outs = _prep_fn(zs_ref[...], w0[...], w2[...], a0[...], a2[...], g2[...], kk_[...], ka_[...], bd[...])
        for ref, val in zip((wf, wb, kf, kb, kk, kaf, kab, g), outs):
            ref[...] = val

    row = BS((tr, 512), lambda i: (i, 0))
    return pl.pallas_call(
        body, grid=(m // tr,), in_specs=[BS((tr, RWKV_COLS), lambda i: (i, 0))] + [_const(q.shape) for q in params],
        out_specs=[row] * 8, out_shape=[SDS((m, 512), F32)] * 8, compiler_params=_cp("parallel"), name="rwkv_prep_fwd",
    )(zs, *params)


def _prep_bwd(zs, p, ct_rows):
    m = zs.shape[0]
    tr = 128
    params = [p["w0"], p["w2"], p["a0"], p["a2"], p["g2"], p["k_k"], p["k_a"]]
    names = ["dwf", "dwb", "dkf", "dkb", "dkk_f", "dkk_b", "dkaf", "dkab", "dr_f", "dr_b", "dr_p", "dk_p", "dv_p", "dg",
             "dv_f", "dv_b"]
    rows = [ct_rows[n] for n in names]

    def body(zs_ref, w0, w2, a0, a2, g2, kk_, ka_, bd, *rest):
        c = {n: r[...] for n, r in zip(names, rest[:len(names)])}
        outs = rest[len(names):]
        dzs_ref, grads = outs[0], outs[1:]
        ones_bd = bd[...]
        _, vjp = jax.vjp(
            lambda *q: _prep_fn(*q, ones_bd), zs_ref[...], w0[...], w2[...], a0[...], a2[...], g2[...], kk_[...], ka_[...]
        )
        cts = (c["dwf"], c["dwb"], c["dkf"] + c["dk_p"], c["dkb"] + c["dk_p"], c["dkk_f"] + c["dkk_b"], c["dkaf"], c["dkab"], c["dg"])
        dzs, *dparams = vjp(cts)
        dr = c["dr_f"] + c["dr_b"] + c["dr_p"]
        dv = c["dv_f"] + c["dv_b"] + c["dv_p"]
        dzs_ref[:, 0:512] = dzs[:, 0:512] + dr
        dzs_ref[:, 512:1024] = dzs[:, 512:1024]
        dzs_ref[:, 1024:1536] = dzs[:, 1024:1536] + dv
        dzs_ref[:, 1536:1920] = dzs[:, 1536:1920]

        @pl.when(pl.program_id(0) == 0)
        def _():
            for gr in grads:
                gr[...] = jnp.zeros_like(gr)

        for gr, val in zip(grads, dparams):
            gr[...] += val

    row = BS((tr, 512), lambda i: (i, 0))
    return pl.pallas_call(
        body, grid=(m // tr,),
        in_specs=[BS((tr, RWKV_COLS), lambda i: (i, 0))] + [_const(q.shape) for q in params] + [_const(p["ones_bd"].shape)]
        + [row] * len(names),
        out_specs=[BS((tr, RWKV_COLS), lambda i: (i, 0))] + [_const(q.shape) for q in params],
        out_shape=[SDS((m, RWKV_COLS), F32)] + [SDS(q.shape, F32) for q in params],
        compiler_params=_cp("arbitrary"), name="rwkv_prep_bwd",
    )(zs, *params, p["ones_bd"], *rows)


def _post_specs(tr):
    r = BS((tr, 512), lambda i: (i, 0))
    v = BS((tr, 512), lambda i: (i, 2))
    row = BS((tr, 512), lambda i: (i, 0))
    return r, v, row


def _post_fwd(y_f, y_b, zs, kf, kb, g, p):
    m = zs.shape[0]
    tr = 256
    r, v, row = _post_specs(tr)
    vecs = [p["r_k"], p["ln_x_g"], p["ln_x_b"], p["ones_bd"]]

    def body(yf, yb, r_ref, v_ref, kf_ref, kb_ref, g_ref, rk, lg, lb, bd, o_ref):
        o_ref[...] = _post_fn(
            yf[...] + yb[...], r_ref[...], kf_ref[...], kb_ref[...], v_ref[...], g_ref[...], rk[...], lg[...], lb[...], bd[...]
        ).astype(BF16)

    return pl.pallas_call(
        body, grid=(m // tr,), in_specs=[row, row, r, v, row, row, row] + [_const(q.shape) for q in vecs],
        out_specs=row, out_shape=SDS((m, 512), BF16), compiler_params=_cp("parallel"), name="rwkv_post_fwd",
    )(y_f, y_b, zs, zs, kf, kb, g, *vecs)


def _post_bwd(y_f, y_b, zs, kf, kb, g, p, dymix):
    m = zs.shape[0]
    tr = 128
    r, v, row = _post_specs(tr)
    vecs = [p["r_k"], p["ln_x_g"], p["ln_x_b"]]

    def body(yf, yb, r_ref, v_ref, kf_ref, kb_ref, g_ref, rk, lg, lb, bd, dy_ref, dyo, dr, dk, dv, dg, drk, dlg, dlb):
        ones_bd = bd[...]
        _, vjp = jax.vjp(
            lambda *q: _post_fn(*q, ones_bd),
            yf[...] + yb[...], r_ref[...], kf_ref[...], kb_ref[...], v_ref[...], g_ref[...], rk[...], lg[...], lb[...],
        )
        c_y, c_r, c_kf, _, c_v, c_g, c_rk, c_lg, c_lb = vjp(dy_ref[...])
        dyo[...] = c_y
        dr[...] = c_r
        dk[...] = c_kf
        dv[...] = c_v
        dg[...] = c_g

        @pl.when(pl.program_id(0) == 0)
        def _():
            for ref in (drk, dlg, dlb):
                ref[...] = jnp.zeros_like(ref)

        drk[...] += c_rk
        dlg[...] += c_lg
        dlb[...] += c_lb

    vec = _const((1, 512))
    return pl.pallas_call(
        body, grid=(m // tr,),
        in_specs=[row, row, r, v, row, row, row] + [_const(q.shape) for q in vecs] + [_const(p["ones_bd"].shape), row],
        out_specs=[row, row, row, row, row, vec, vec, vec],
        out_shape=[SDS((m, 512), F32)] * 5 + [SDS((1, 512), F32)] * 3,
        compiler_params=_cp("arbitrary"), name="rwkv_post_bwd",
    )(y_f, y_b, zs, zs, kf, kb, g, *vecs, p["ones_bd"], dymix)


def _half_ones():
    ri = lax.broadcasted_iota(jnp.int32, (256, 128), 0) & 127
    ci = lax.broadcasted_iota(jnp.int32, (256, 128), 1)
    return jnp.where((ri < 64) == (ci < 64), 1.0, 0.0).astype(BF16)


def _half_sums(xs, ones):
    x = jnp.concatenate(xs, axis=0)
    hi = x.astype(BF16)
    mid = (x - hi.astype(F32)).astype(BF16)
    res = jnp.dot(jnp.concatenate([hi, mid], axis=1), ones, preferred_element_type=F32)
    return [res[64 * i:64 * i + 64] for i in range(len(xs))]


def _scan_specs(b, t):
    nc = t // SCAN_CHUNK
    up, down = (lambda c: c), (lambda c: nc - 1 - c)
    rows = [BS((b, SCAN_CHUNK, 512), lambda c, ci=ci: (0, ci(c), 0)) for ci in (up, down)]
    hist = [BS((SCAN_CHUNK, b * 4, 64, 128), lambda c, ci=ci: (ci(c), 0, 0, 0)) for ci in (up, down)]
    return nc, rows, hist


class _Window:
    def __init__(self, g, ascending):
        self.bases = [pl.multiple_of(g * 8, 8) if asc else pl.multiple_of(SCAN_CHUNK - 8 - g * 8, 8) for asc in ascending]
        self.ascending = ascending
        self.blocks = {}
        self.row_id = lax.broadcasted_iota(jnp.int32, (8, 128), 0)

    def j(self, d, s):
        return s if self.ascending[d] else 7 - s

    def time(self, d, s):
        return self.bases[d] + self.j(d, s)

    def row(self, ref, d, bi, cols, s):
        key = (id(ref), d, bi, cols.start)
        if key not in self.blocks:
            self.blocks[key] = ref[bi, pl.ds(self.bases[d], 8), cols]
        jj = self.j(d, s)
        return self.blocks[key][jj:jj + 1, :]

    def put(self, buf, key, d, s, row):
        prev = buf.get(key)
        new = jnp.broadcast_to(row, (8, 128))
        buf[key] = new if prev is None else jnp.where(self.row_id == self.j(d, s), new, prev)

    def flush(self, buf, refs_of):
        for key, val in buf.items():
            ref, d, bi, cols = refs_of(key)
            ref[bi, pl.ds(self.bases[d], 8), cols] = val


def _pairs(b):
    return [(bi * 4 + p, bi, slice(128 * p, 128 * p + 128)) for bi in range(b) for p in range(4)]


def _colsum(x):
    return jnp.sum(x, axis=0, keepdims=True)


def _eye_mask():
    return (lax.broadcasted_iota(jnp.int32, (64, 128), 1) & 63) == lax.broadcasted_iota(jnp.int32, (64, 128), 0)


def _scan_fwd(r, kk, v, ops_f, ops_b):
    b, t = r.shape[:2]
    nc, rows, hist = _scan_specs(b, t)
    npair = b * 4

    def body(*refs):
        ins, outs, s_ref = refs[:12], refs[12:16], refs[16]
        dirs = [dict(zip(("r", "kk", "v", "w", "k", "ka", "y", "h"), (*ins[6 * d:6 * d + 6], *outs[2 * d:2 * d + 2])))
                for d in (0, 1)]

        @pl.when(pl.program_id(0) == 0)
        def _():
            s_ref[...] = jnp.zeros_like(s_ref)

        ones, eye = _half_ones(), _eye_mask()
        chains = [(d, pr, bi, cols) for d in (0, 1) for pr, bi, cols in _pairs(b)]

        def eight_steps(g, carry):
            win = _Window(g, (True, False))
            ybuf = {}
            for s in range(8):
                s_prev, xa = [], []
                for d, pr, bi, cols in chains:
                    q = dirs[d]
                    st = s_ref[d * npair + pr]
                    q["h"][win.time(d, s), pr] = st
                    s_prev.append(st)
                    xa += [st * win.row(q["kk"], d, bi, cols, s), jnp.where(eye, win.row(q["v"], d, bi, cols, s), 0.0)]
                ra = _half_sums(xa, ones)
                xb = []
                for i, (d, pr, bi, cols) in enumerate(chains):
                    q = dirs[d]
                    s_new = s_prev[i] * win.row(q["w"], d, bi, cols, s) - ra[2 * i] * win.row(q["ka"], d, bi, cols, s) \
                        + ra[2 * i + 1] * win.row(q["k"], d, bi, cols, s)
                    s_ref[d * npair + pr] = s_new
                    xb.append(s_new * win.row(q["r"], d, bi, cols, s))
                rb = _half_sums(xb, ones)
                for i, (d, pr, bi, cols) in enumerate(chains):
                    win.put(ybuf, i, d, s, _colsum(jnp.where(eye, rb[i], 0.0)))
            win.flush(ybuf, lambda i: (dirs[chains[i][0]]["y"], chains[i][0], chains[i][2], chains[i][3]))
            return carry

        lax.fori_loop(0, SCAN_CHUNK // 8, eight_steps, 0)

    row_shape, hist_shape = SDS((b, t, 512), F32), SDS((t, npair, 64, 128), F32)
    return pl.pallas_call(
        body, grid=(nc,), in_specs=[rows[0]] * 6 + [rows[1]] * 6, out_specs=[rows[0], hist[0], rows[1], hist[1]],
        out_shape=[row_shape, hist_shape, row_shape, hist_shape],
        scratch_shapes=[pltpu.VMEM((2 * npair, 64, 128), F32)], compiler_params=_cp("arbitrary"), name="wkv_scan",
    )(r, kk, v, *ops_f, r, kk, v, *ops_b)


def _scan_bwd(r, kk, v, dy, ops_f, hist_f, ops_b, hist_b):
    b, t = r.shape[:2]
    nc, rows, hist = _scan_specs(b, t)
    npair = b * 4
    names_in = ("r", "kk", "v", "dy", "w", "k", "ka", "h")
    names_out = ("dr", "dw", "dk", "dkk", "dka", "dv")

    def body(*refs):
        ins, outs, ds_ref = refs[:16], refs[16:28], refs[28]
        dirs = [dict(zip(names_in + names_out, (*ins[8 * d:8 * d + 8], *outs[6 * d:6 * d + 6]))) for d in (0, 1)]

        @pl.when(pl.program_id(0) == 0)
        def _():
            ds_ref[...] = jnp.zeros_like(ds_ref)

        ones, eye = _half_ones(), _eye_mask()
        chains = [(d, pr, bi, cols) for d in (0, 1) for pr, bi, cols in _pairs(b)]

        def eight_steps(g, carry):
            win = _Window(g, (False, True))
            obuf = {}
            for s in range(8):
                row = lambda name, d, bi, cols: win.row(dirs[d][name], d, bi, cols, s)
                s_prev, xa = [], []
                for d, pr, bi, cols in chains:
                    st = dirs[d]["h"][win.time(d, s), pr]
                    s_prev.append(st)
                    xa += [st * row("kk", d, bi, cols), jnp.where(eye, row("v", d, bi, cols), 0.0),
                           jnp.where(eye, row("dy", d, bi, cols), 0.0)]
                ra = _half_sums(xa, ones)
                ds_now, xb = [], []
                for i, (d, pr, bi, cols) in enumerate(chains):
                    skk, vcol, dycol = ra[3 * i], ra[3 * i + 1], ra[3 * i + 2]
                    ka_r, k_r = row("ka", d, bi, cols), row("k", d, bi, cols)
                    s_new = s_prev[i] * row("w", d, bi, cols) - skk * ka_r + vcol * k_r
                    ds = ds_ref[d * npair + pr] + dycol * row("r", d, bi, cols)
                    win.put(obuf, (i, "dr"), d, s, _colsum(s_new * dycol))
                    win.put(obuf, (i, "dk"), d, s, _colsum(ds * vcol))
                    win.put(obuf, (i, "dka"), d, s, -_colsum(ds * skk))
                    win.put(obuf, (i, "dw"), d, s, _colsum(ds * s_prev[i]))
                    ds_now.append(ds)
                    xb += [ds * k_r, ds * ka_r]
                rb = _half_sums(xb, ones)
                for i, (d, pr, bi, cols) in enumerate(chains):
                    dskk = -rb[2 * i + 1]
                    win.put(obuf, (i, "dv"), d, s, _colsum(jnp.where(eye, rb[2 * i], 0.0)))
                    win.put(obuf, (i, "dkk"), d, s, _colsum(s_prev[i] * dskk))
                    ds_ref[d * npair + pr] = ds_now[i] * row("w", d, bi, cols) + dskk * row("kk", d, bi, cols)
            win.flush(obuf, lambda key: (dirs[chains[key[0]][0]][key[1]], chains[key[0]][0], chains[key[0]][2], chains[key[0]][3]))
            return carry

        lax.fori_loop(0, SCAN_CHUNK // 8, eight_steps, 0)

    row_shape = SDS((b, t, 512), F32)
    return pl.pallas_call(
        body, grid=(nc,), in_specs=[rows[1]] * 7 + [hist[1]] + [rows[0]] * 7 + [hist[0]],
        out_specs=[rows[1]] * 6 + [rows[0]] * 6, out_shape=[row_shape] * 12,
        scratch_shapes=[pltpu.VMEM((2 * npair, 64, 128), F32)], compiler_params=_cp("arbitrary"), name="wkv_scan_bwd",
    )(r, kk, v, dy, *ops_f, hist_f, r, kk, v, dy, *ops_b, hist_b)


def _mla_fwd(zm, cs, sn, p, t):
    m = zm.shape[0]
    tr = 256
    per = t // tr
    params = [p["q_norm_g"], p["kv_norm_g"], p["wq"], p["wqr"], p["wk"], p["wv"]]

    def body(z_ref, cs_ref, sn_ref, gq, gkv, wq, wqr, wk, wv, q_ref, k_ref, v_ref):
        q, kf, v = _mla_fn(z_ref[...], cs_ref[...], sn_ref[...], gq[...], gkv[...], wq[...], wqr[...], wk[...], wv[...])
        q_ref[...] = q.astype(BF16)
        k_ref[...] = kf.astype(BF16)
        v_ref[...] = v.astype(BF16)

    tab = BS((tr, 128), lambda i: (i % per, 0))
    return pl.pallas_call(
        body, grid=(m // tr,), in_specs=[BS((tr, MLA_PAD_COLS), lambda i: (i, 0)), tab, tab] + [_const(q.shape) for q in params],
        out_specs=[BS((tr, 1024), lambda i: (i, 0)), BS((tr, 1024), lambda i: (i, 0)), BS((tr, 512), lambda i: (i, 0))],
        out_shape=[SDS((m, 1024), BF16), SDS((m, 1024), BF16), SDS((m, 512), BF16)], compiler_params=_cp("parallel"), name="mla_prep_fwd",
    )(zm, cs, sn, *params)


def _mla_bwd(zm, cs, sn, p, t, dq, dk, dv):
    m = zm.shape[0]
    tr = 128
    per = t // tr
    params = [p["q_norm_g"], p["kv_norm_g"], p["wq"], p["wqr"], p["wk"], p["wv"]]

    def body(z_ref, cs_ref, sn_ref, gq, gkv, wq, wqr, wk, wv, dq_ref, dk_ref, dv_ref, dz_ref, *grads):
        cs_v, sn_v = cs_ref[...], sn_ref[...]
        _, vjp = jax.vjp(
            lambda *q: _mla_fn(q[0], cs_v, sn_v, *q[1:]), z_ref[...], gq[...], gkv[...], wq[...], wqr[...], wk[...], wv[...]
        )
        dz, *dparams = vjp((dq_ref[...], dk_ref[...], dv_ref[...]))
        dz_ref[...] = dz

        @pl.when(pl.program_id(0) == 0)
        def _():
            for gr in grads:
                gr[...] = jnp.zeros_like(gr)

        for gr, val in zip(grads, dparams):
            gr[...] += val

    tab = BS((tr, 128), lambda i: (i % per, 0))
    wide = BS((tr, 1024), lambda i: (i, 0))
    return pl.pallas_call(
        body, grid=(m // tr,),
        in_specs=[BS((tr, MLA_PAD_COLS), lambda i: (i, 0)), tab, tab] + [_const(q.shape) for q in params]
        + [wide, wide, BS((tr, 512), lambda i: (i, 0))],
        out_specs=[BS((tr, MLA_PAD_COLS), lambda i: (i, 0))] + [_const(q.shape) for q in params],
        out_shape=[SDS((m, MLA_PAD_COLS), F32)] + [SDS(q.shape, F32) for q in params],
        compiler_params=_cp("arbitrary"), name="mla_prep_bwd",
    )(zm, cs, sn, *params, dq, dk, dv)


_NT = (((1,), (1,)), ((), ()))
_TN = (((0,), (0,)), ((), ()))


def _attn_fwd(q, kf, v, b, t):
    m = q.shape[0]
    tq = 256
    nq = t // tq

    def body(q_ref, k_ref, v_ref, o_ref, l_ref):
        lo = lax.broadcasted_iota(jnp.int32, (1, 128), 1) < 64
        v_all = v_ref[...]
        o = jnp.zeros((tq, 128), F32)
        lse = []
        for h in range(2):
            hs = slice(128 * h, 128 * h + 128)
            s = lax.dot_general(q_ref[:, hs], k_ref[:, hs], _NT, preferred_element_type=F32) * MLA_SCALE
            mx = jnp.max(s, axis=1, keepdims=True)
            e = jnp.exp(s - mx)
            den = jnp.sum(e, axis=1, keepdims=True)
            vh = jnp.where(lo if h == 0 else jnp.logical_not(lo), v_all, jnp.zeros_like(v_all))
            o = o + jnp.dot(e.astype(BF16), vh, preferred_element_type=F32) / den
            lse.append(mx + jnp.log(den))
        o_ref[...] = o
        l_ref[...] = jnp.where(lo, lse[0], lse[1])

    return pl.pallas_call(
        body, grid=(b, 4, nq),
        in_specs=[BS((tq, 256), lambda bi, hp, i: (bi * nq + i, hp)), BS((t, 256), lambda bi, hp, i: (bi, hp)),
                  BS((t, 128), lambda bi, hp, i: (bi, hp))],
        out_specs=[BS((tq, 128), lambda bi, hp, i: (bi * nq + i, hp))] * 2,
        out_shape=[SDS((m, 512), F32), SDS((m, 512), F32)], compiler_params=_cp("parallel", "parallel", "arbitrary"), name="attn_fwd",
    )(q, kf, v)


def _attn_bwd(q, kf, v, o, lse, do, b, t):
    m = q.shape[0]
    tq = 256
    nq = t // tq

    def body(q_ref, k_ref, v_ref, o_ref, l_ref, do_ref, dq_ref, dk_ref, dv_ref):
        lo = lax.broadcasted_iota(jnp.int32, (1, 128), 1) < 64

        @pl.when(pl.program_id(2) == 0)
        def _():
            dk_ref[...] = jnp.zeros_like(dk_ref)
            dv_ref[...] = jnp.zeros_like(dv_ref)

        v_all, o_all, l_all, do_all = v_ref[...], o_ref[...], l_ref[...], do_ref[...]
        dv_acc = jnp.zeros((t, 128), F32)
        for h in range(2):
            hs = slice(128 * h, 128 * h + 128)
            mask = lo if h == 0 else jnp.logical_not(lo)
            qh, kh = q_ref[:, hs], k_ref[:, hs]
            s = lax.dot_general(qh, kh, _NT, preferred_element_type=F32) * MLA_SCALE
            lse_h = jnp.max(jnp.where(mask, l_all, -jnp.inf), axis=1, keepdims=True)
            pr = jnp.exp(s - lse_h)
            do_h = jnp.where(mask, do_all, 0.0)
            dp = lax.dot_general(do_h.astype(BF16), v_all, _NT, preferred_element_type=F32)
            dsum = jnp.sum(do_h * o_all, axis=1, keepdims=True)
            ds = (pr * (dp - dsum) * MLA_SCALE).astype(BF16)
            dq_ref[:, hs] = jnp.dot(ds, kh, preferred_element_type=F32)
            dk_ref[:, hs] += lax.dot_general(ds, qh, _TN, preferred_element_type=F32)
            dv_acc = dv_acc + lax.dot_general(pr.astype(BF16), do_h.astype(BF16), _TN, preferred_element_type=F32)
        dv_ref[...] += dv_acc

    qspec = BS((tq, 256), lambda bi, hp, i: (bi * nq + i, hp))
    kspec = BS((t, 256), lambda bi, hp, i: (bi, hp))
    vspec = BS((t, 128), lambda bi, hp, i: (bi, hp))
    ospec = BS((tq, 128), lambda bi, hp, i: (bi * nq + i, hp))
    return pl.pallas_call(
        body, grid=(b, 4, nq), in_specs=[qspec, kspec, vspec, ospec, ospec, ospec], out_specs=[qspec, kspec, vspec],
        out_shape=[SDS((m, 1024), F32), SDS((m, 1024), F32), SDS((m, 512), F32)],
        compiler_params=_cp("parallel", "parallel", "arbitrary"), name="attn_bwd",
    )(q, kf, v, o, lse, do)


def _conv3(u, w_ref, b_ref, t):
    up, un = _prev_next(u, t)
    return w_ref[0:1, :] * up + w_ref[1:2, :] * u + w_ref[2:3, :] * un + b_ref[...], up, un


def _ffn_mid_fwd(ug3, uv3, cw, cb):
    b, t, f = ug3.shape
    nc = f // 256

    def body(ug_ref, uv_ref, wg_ref, wv_ref, bg_ref, bv_ref, a_ref):
        gc, _, _ = _conv3(ug_ref[0], wg_ref, bg_ref, t)
        vc, _, _ = _conv3(uv_ref[0], wv_ref, bv_ref, t)
        a_ref[0] = (gc * jax.nn.sigmoid(gc) * vc).astype(BF16)

    blk = BS((1, t, 256), lambda i, j: (i, 0, j))
    return pl.pallas_call(
        body, grid=(b, nc),
        in_specs=[blk, blk, BS((3, 256), lambda i, j: (0, j)), BS((3, 256), lambda i, j: (0, j + nc)),
                  BS((1, 256), lambda i, j: (0, j)), BS((1, 256), lambda i, j: (0, j + nc))],
        out_specs=blk, out_shape=SDS((b, t, f), BF16), compiler_params=_cp("parallel", "parallel"), name="ffn_mid_fwd",
    )(ug3, uv3, cw, cw, cb, cb)


def _ffn_mid_bwd(ug3, uv3, cw, cb, da3):
    b, t, f = ug3.shape
    nc = f // 256

    def half(u, up, un, dc, w_ref):
        dprev, dnext = _prev_next(dc, t)
        du = w_ref[1:2, :] * dc + w_ref[0:1, :] * dnext + w_ref[2:3, :] * dprev
        sums = [jnp.sum(dc * q, axis=0, keepdims=True) for q in (up, u, un)] + [jnp.sum(dc, axis=0, keepdims=True)]
        row = lax.broadcasted_iota(jnp.int32, (8, 256), 0)
        tab = jnp.zeros((8, 256), F32)
        for i, s in enumerate(sums):
            tab = jnp.where(row == i, s, tab)
        return du, tab

    def body(ug_ref, uv_ref, wg_ref, wv_ref, bg_ref, bv_ref, da_ref, dug_ref, duv_ref, tg_ref, tv_ref):
        ug, uv, da = ug_ref[0], uv_ref[0], da_ref[0]
        gc, gp, gn = _conv3(ug, wg_ref, bg_ref, t)
        vc, vp, vn = _conv3(uv, wv_ref, bv_ref, t)
        sg = jax.nn.sigmoid(gc)
        d_gc = da * vc * (sg * (1.0 + gc * (1.0 - sg)))
        d_vc = da * (gc * sg)
        dug, tg = half(ug, gp, gn, d_gc, wg_ref)
        duv, tv = half(uv, vp, vn, d_vc, wv_ref)
        dug_ref[0] = dug
        duv_ref[0] = duv

        @pl.when(pl.program_id(1) == 0)
        def _():
            tg_ref[...] = jnp.zeros_like(tg_ref)
            tv_ref[...] = jnp.zeros_like(tv_ref)

        tg_ref[...] += tg
        tv_ref[...] += tv

    blk = BS((1, t, 256), lambda j, i: (i, 0, j))
    tab = BS((8, 256), lambda j, i: (0, j))
    return pl.pallas_call(
        body, grid=(nc, b),
        in_specs=[blk, blk, BS((3, 256), lambda j, i: (0, j)), BS((3, 256), lambda j, i: (0, j + nc)),
                  BS((1, 256), lambda j, i: (0, j)), BS((1, 256), lambda j, i: (0, j + nc)), blk],
        out_specs=[blk, blk, tab, tab],
        out_shape=[SDS((b, t, f), F32), SDS((b, t, f), F32), SDS((8, f), F32), SDS((8, f), F32)],
        compiler_params=_cp("parallel", "arbitrary"), name="ffn_mid_bwd",
    )(ug3, uv3, cw, cw, cb, cb, da3)


def _add_rows(parts, name):
    r = parts[0].shape[0]
    tr = _row_tile(r, 1024)
    n = len(parts)

    def body(*refs):
        acc = refs[0][...]
        for q in refs[1:n]:
            acc = acc + q[...]
        refs[n][...] = acc

    row = BS((tr, 128), lambda i: (i, 0))
    return pl.pallas_call(
        body, grid=(r // tr,), in_specs=[row] * n, out_specs=row, out_shape=SDS((r, 128), F32),
        compiler_params=_cp("parallel"), name=name,
    )(*parts)


def _adamw(w, g, m, v, name):
    r, c = w.shape
    tr = _row_tile(r)

    def body(w_ref, g_ref, m_ref, v_ref, d_ref, m2_ref, v2_ref):
        d, m2, v2 = _adamw_math(w_ref[...], g_ref[...], m_ref[...], v_ref[...])
        d_ref[...] = d
        m2_ref[...] = m2
        v2_ref[...] = v2

    blk = BS((tr, c), lambda i: (i, 0))
    return pl.pallas_call(
        body, grid=(r // tr,), in_specs=[blk] * 4, out_specs=[blk] * 3, out_shape=[SDS((r, c), F32)] * 3,
        compiler_params=_cp("parallel"), name=name,
    )(w, g, m, v)


def _place():
    return lax.axis_index("x"), lax.axis_index("y"), lax.axis_index("c")


def _flip(v, bit):
    return 1 - v if bit else v


def _allgather_weights(shard):
    r = shard.shape[0]
    rh = r // 2

    def body(x_ref, out_ref, send_sems, recv_sems, local_sem):
        x, y, c = _place()
        me, sibling = (x, y, c), (x, y, 1 - c)
        chips = [(1 - x, y), (x, 1 - y), (1 - x, 1 - y)]
        mine_src = x_ref.at[pl.ds(c * rh, rh), :]

        def rows(px, py, pc):
            return out_ref.at[pl.ds((4 * px + 2 * py + pc) * rh, rh), :]

        def copy(k, block, to, src=None):
            return pltpu.make_async_remote_copy(
                src_ref=rows(*block) if src is None else src, dst_ref=rows(*block), send_sem=send_sems.at[k],
                recv_sem=recv_sems.at[k], device_id=to, device_id_type=MESH,
            )

        mine = pltpu.make_async_copy(mine_src, rows(*me), local_sem)
        mine.start()
        first = [copy(0, me, sibling, src=mine_src)]
        first += [copy(1 + j, me, (*chip, c), src=mine_src) for j, chip in enumerate(chips)]
        for cp in first:
            cp.start()
        passed = [copy(4 + j, (*chip, c), sibling) for j, chip in enumerate(chips)]
        for j, chip in enumerate(chips):
            copy(1 + j, (*chip, c), me).wait_recv()
            passed[j].start()
        copy(0, sibling, me).wait_recv()
        for j, chip in enumerate(chips):
            copy(4 + j, (*chip, 1 - c), me).wait_recv()
        for cp in first + passed:
            cp.wait_send()
        mine.wait()

    return pl.pallas_call(
        body, out_shape=SDS((8 * rh, 128), shard.dtype), in_specs=[ANY], out_specs=ANY,
        scratch_shapes=[pltpu.SemaphoreType.DMA((7,)), pltpu.SemaphoreType.DMA((7,)), pltpu.SemaphoreType.DMA],
        name="allgather_weights",
    )(shard)


def _scatter_to_chips(g):
    def body(g_ref, recv_ref, send_sems, recv_sems, local_sem):
        x, y, c = _place()
        copies = []
        for j, (fx, fy) in enumerate(((1, 0), (0, 1), (1, 1))):
            px, py = _flip(x, fx), _flip(y, fy)
            cp = pltpu.make_async_remote_copy(
                src_ref=g_ref.at[2 * px + py], dst_ref=recv_ref.at[j], send_sem=send_sems.at[j], recv_sem=recv_sems.at[j],
                device_id=(px, py, c), device_id_type=MESH,
            )
            cp.start()
            copies.append(cp)
        own = pltpu.make_async_copy(g_ref.at[2 * x + y], recv_ref.at[3], local_sem)
        own.start()
        for cp in copies:
            cp.wait_recv()
        for cp in copies:
            cp.wait_send()
        own.wait()

    return pl.pallas_call(
        body, out_shape=SDS(g.shape, g.dtype), in_specs=[ANY], out_specs=ANY,
        scratch_shapes=[pltpu.SemaphoreType.DMA((3,)), pltpu.SemaphoreType.DMA((3,)), pltpu.SemaphoreType.DMA],
        name="scatter_grads",
    )(g)


def _swap_with_sibling(a):
    def body(a_ref, b_ref, send_sem, recv_sem):
        x, y, c = _place()
        cp = pltpu.make_async_remote_copy(
            src_ref=a_ref, dst_ref=b_ref, send_sem=send_sem, recv_sem=recv_sem, device_id=(x, y, 1 - c), device_id_type=MESH
        )
        cp.start()
        cp.wait()

    return pl.pallas_call(
        body, out_shape=SDS(a.shape, a.dtype), in_specs=[ANY], out_specs=ANY,
        scratch_shapes=[pltpu.SemaphoreType.DMA, pltpu.SemaphoreType.DMA], name="swap_sibling",
    )(a)


def _allreduce_small(v):
    r = v.shape[0]

    def body(v_ref, out_ref, buf_ref, send_sems, recv_sems):
        x, y, c = _place()
        buf_ref[0] = v_ref[...]
        copies = []
        for k in range(1, 8):
            peer = (_flip(x, k >> 2 & 1), _flip(y, k >> 1 & 1), _flip(c, k & 1))
            cp = pltpu.make_async_remote_copy(
                src_ref=v_ref, dst_ref=buf_ref.at[k], send_sem=send_sems.at[k - 1], recv_sem=recv_sems.at[k - 1],
                device_id=peer, device_id_type=MESH,
            )
            cp.start()
            copies.append(cp)
        for cp in copies:
            cp.wait_recv()
        acc = None
        for d in range(8):
            slot = 4 * _flip(x, d >> 2 & 1) + 2 * _flip(y, d >> 1 & 1) + _flip(c, d & 1)
            term = buf_ref[slot]
            acc = term if acc is None else acc + term
        out_ref[...] = acc
        for cp in copies:
            cp.wait_send()

    return pl.pallas_call(
        body, out_shape=SDS(v.shape, F32), in_specs=[VMEM], out_specs=VMEM,
        scratch_shapes=[pltpu.VMEM((8, r, 128), F32), pltpu.SemaphoreType.DMA((7,)), pltpu.SemaphoreType.DMA((7,))],
        name="allreduce_small",
    )(v)


_BIG = (
    ("w_in", 1, False), ("decay_w2_fwd", 1, False), ("decay_w2_bwd", 1, False), ("iclr_a2_fwd", 1, False),
    ("iclr_a2_bwd", 1, False), ("gate_g2", 1, False), ("w_uq", 0, False), ("w_ukv", 1, False), ("w_out", 0, False),
    ("w_ffn_up", 1, False), ("ffn_conv_w", 1, True), ("w_ffn_down", 0, False),
)
_SMALL = (
    "ln_mix_g", "shift_mu_prev", "shift_mu_next", "decay_w0_fwd", "decay_w0_bwd", "iclr_a0_fwd", "iclr_a0_bwd", "k_k",
    "k_a", "r_k", "ln_x_g", "ln_x_b", "q_norm_g", "kv_norm_g", "mla_out_g", "ln_ffn_g", "ffn_conv_b", "ln_final_g",
)
_WEIGHTS = (
    "ln_mix_g", "w_in", "shift_mu_prev", "shift_mu_next", "decay_w0_fwd", "decay_w2_fwd", "decay_w0_bwd", "decay_w2_bwd",
    "iclr_a0_fwd", "iclr_a2_fwd", "iclr_a0_bwd", "iclr_a2_bwd", "gate_g2", "k_k", "k_a", "r_k", "ln_x_g", "ln_x_b",
    "q_norm_g", "w_uq", "kv_norm_g", "w_ukv", "mla_out_g", "w_out", "ln_ffn_g", "w_ffn_up", "ffn_conv_w", "ffn_conv_b",
    "w_ffn_down", "ln_final_g",
)


def _pad_rows(flat, rows):
    return jnp.pad(flat, (0, rows * 128 - flat.shape[0])).reshape(rows, 128)


def _rows_for(n, mult):
    rows = -(-n // 128)
    return -(-rows // mult) * mult


def _pack_shards_bf16(arrs):
    parts = []
    for name, _, raw in _BIG:
        w = arrs[name][0]
        flat = lax.bitcast_convert_type(w, BF16).reshape(-1) if raw else w.astype(BF16).reshape(-1)
        parts.append(_pad_rows(flat, _rows_for(flat.shape[0], 32)))
    return jnp.concatenate(parts, axis=0)


def _unpack_gathered(g4, arrs):
    out, off = {}, 0
    for name, axis, raw in _BIG:
        a, b = arrs[name].shape[1:]
        n = a * b * (2 if raw else 1)
        rows = _rows_for(n, 32)
        seg = g4[:, off:off + rows].reshape(4, rows * 128)[:, :n]
        off += rows
        if raw:
            seg = lax.bitcast_convert_type(seg.reshape(4, a * b, 2), F32)
        seg = seg.reshape(4, a, b)
        out[name] = jnp.concatenate([seg[s] for s in range(4)], axis=1) if axis == 1 else seg.reshape(4 * a, b)
    return out


def _pack_grads(full, arrs):
    parts = []
    for name, axis, _ in _BIG:
        a, b = arrs[name].shape[1:]
        g = full[name]
        sh = g.reshape(a, 4, b).transpose(1, 0, 2) if axis == 1 else g.reshape(4, a, b)
        rows = _rows_for(a * b, 8)
        parts.append(jnp.pad(sh.reshape(4, a * b), ((0, 0), (0, rows * 128 - a * b))).reshape(4, rows, 128))
    packed = jnp.concatenate(parts, axis=1)
    total = packed.shape[1]
    return jnp.pad(packed, ((0, 0), (0, -(-total // 512) * 512 - total), (0, 0)))


def _unpack_grads(g, arrs):
    out, off = {}, 0
    for name, _, _ in _BIG:
        a, b = arrs[name].shape[1:]
        rows = _rows_for(a * b, 8)
        out[name] = g[off:off + rows].reshape(-1)[:a * b].reshape(a, b)
        off += rows
    return out


def _pack_small(vals):
    flat = jnp.concatenate([vals[n].reshape(-1).astype(F32) for n in _SMALL] + [vals["_loss"].reshape(-1)])
    return _pad_rows(flat, _rows_for(flat.shape[0], 8))


def _unpack_small(buf, arrs):
    flat, out, off = buf.reshape(-1), {}, 0
    for n in _SMALL:
        size = arrs[n].size
        out[n] = flat[off:off + size].reshape(arrs[n].shape)
        off += size
    out["_loss"] = flat[off]
    return out


def _rot_cols(w):
    return jnp.concatenate([-w[..., 16:], w[..., :16]], axis=-1)


def _rot_cols_t(g):
    return jnp.concatenate([g[..., 16:], -g[..., :16]], axis=-1)


def _rope_tables(t):
    inv = jnp.power(ROPE_THETA, -jnp.arange(0, ROPE_DIM, 2, dtype=F32) / ROPE_DIM)
    ang = jnp.arange(t, dtype=F32)[:, None] * inv[None, :]
    one, zero = jnp.ones((t, 64), F32), jnp.zeros((t, 64), F32)
    cs = jnp.concatenate([one, jnp.cos(ang), jnp.cos(ang), zero[:, :32]], axis=1)
    sn = jnp.concatenate([zero, jnp.sin(ang), jnp.sin(ang), zero[:, :32]], axis=1)
    return cs, sn


def _block_diag(a, b):
    za = jnp.zeros_like(a)
    return jnp.concatenate([jnp.concatenate([a, za], axis=1), jnp.concatenate([za, b], axis=1)], axis=0)


def kernel(x, ln_mix_g, w_in, shift_mu_prev, shift_mu_next, decay_w0_fwd, decay_w2_fwd, decay_w0_bwd, decay_w2_bwd, iclr_a0_fwd, iclr_a2_fwd, iclr_a0_bwd, iclr_a2_bwd, gate_g2, k_k, k_a, r_k, ln_x_g, ln_x_b, q_norm_g, w_uq, kv_norm_g, w_ukv, mla_out_g, w_out, ln_ffn_g, w_ffn_up, ffn_conv_w, ffn_conv_b, w_ffn_down, ln_final_g, loss_target, m_ln_mix_g, m_w_in, m_shift_mu_prev, m_shift_mu_next, m_decay_w0_fwd, m_decay_w2_fwd, m_decay_w0_bwd, m_decay_w2_bwd, m_iclr_a0_fwd, m_iclr_a2_fwd, m_iclr_a0_bwd, m_iclr_a2_bwd, m_gate_g2, m_k_k, m_k_a, m_r_k, m_ln_x_g, m_ln_x_b, m_q_norm_g, m_w_uq, m_kv_norm_g, m_w_ukv, m_mla_out_g, m_w_out, m_ln_ffn_g, m_w_ffn_up, m_ffn_conv_w, m_ffn_conv_b, m_w_ffn_down, m_ln_final_g, v_ln_mix_g, v_w_in, v_shift_mu_prev, v_shift_mu_next, v_decay_w0_fwd, v_decay_w2_fwd, v_decay_w0_bwd, v_decay_w2_bwd, v_iclr_a0_fwd, v_iclr_a2_fwd, v_iclr_a0_bwd, v_iclr_a2_bwd, v_gate_g2, v_k_k, v_k_a, v_r_k, v_ln_x_g, v_ln_x_b, v_q_norm_g, v_w_uq, v_kv_norm_g, v_w_ukv, v_mla_out_g, v_w_out, v_ln_ffn_g, v_w_ffn_up, v_ffn_conv_w, v_ffn_conv_b, v_w_ffn_down, v_ln_final_g):
    arrs = dict(locals())
    b, t, d = x.shape
    m = b * t
    x2 = x.reshape(m, d)
    tgt = loss_target.reshape(m, d)
    vec = lambda n: arrs[n].reshape(1, -1)

    gathered = _allgather_weights(_pack_shards_bf16(arrs))
    fw = _unpack_gathered(gathered.reshape(N_CHIPS, -1, 128), arrs)
    win = fw["w_in"]
    zc = jnp.zeros((d, 64), BF16)
    w_kr = win[:, 2944:2976]
    win_m = jnp.concatenate([win[:, 1920:2944], zc, w_kr, zc[:, :32], zc, _rot_cols(w_kr), zc[:, :32]], axis=1)
    win_r = win[:, :RWKV_COLS]
    uq = fw["w_uq"].astype(F32).reshape(Q_RANK, HEADS, 96)
    z32 = jnp.zeros((Q_RANK, HEADS, 32), F32)
    wq = jnp.concatenate([uq[..., :64], uq[..., 64:], z32], axis=-1).reshape(Q_RANK, 1024)
    wqr = jnp.concatenate([z32, z32, _rot_cols(uq[..., 64:]), z32], axis=-1).reshape(Q_RANK, 1024)
    ukv = fw["w_ukv"].astype(F32).reshape(KV_RANK, HEADS, 128)
    wk = jnp.concatenate([ukv[..., :64], jnp.zeros_like(ukv[..., :64])], axis=-1).reshape(KV_RANK, 1024)
    wv = ukv[..., 64:].reshape(KV_RANK, 512)
    head = jnp.arange(512) // HEAD_DIM
    rw = dict(
        w0=jnp.concatenate([vec("decay_w0_fwd"), vec("decay_w0_bwd")], axis=1),
        w2=_block_diag(fw["decay_w2_fwd"], fw["decay_w2_bwd"]).astype(F32),
        a0=jnp.concatenate([vec("iclr_a0_fwd"), vec("iclr_a0_bwd")], axis=1),
        a2=_block_diag(fw["iclr_a2_fwd"], fw["iclr_a2_bwd"]).astype(F32),
        g2=fw["gate_g2"].astype(F32), k_k=vec("k_k"), k_a=vec("k_a"), r_k=vec("r_k"), ln_x_g=vec("ln_x_g"), ln_x_b=vec("ln_x_b"),
        ones_bd=(head[:, None] == head[None, :]).astype(F32),
    )
    mp = dict(q_norm_g=vec("q_norm_g"), kv_norm_g=vec("kv_norm_g"), wq=wq, wqr=wqr, wk=wk, wv=wv)
    cs, sn = _rope_tables(t)
    w_up_g, w_up_v = fw["w_ffn_up"][:, :D_FF], fw["w_ffn_up"][:, D_FF:]
    cw, cb = fw["ffn_conv_w"], vec("ffn_conv_b")

    n1 = _rms_fwd(x2, vec("ln_mix_g"), "rms_mix")
    zm = _mm(n1, win_m, "nn", "proj_in_mla")
    zr = _mm(n1, win_r, "nn", "proj_in_rwkv")
    zs = _shift_fwd(zr.reshape(b, t, RWKV_COLS), vec("shift_mu_prev"), vec("shift_mu_next"))
    zs2 = zs.reshape(m, RWKV_COLS)
    wf, wb, kf, kb, kk, kaf, kab, gate = _prep_fwd(zs2, rw)
    r4 = lambda a: a.reshape(b, t, 512)
    f2 = lambda a: a.reshape(m, 512)
    rr, kk4, vv = zs[:, :, :512], r4(kk), zs[:, :, 1024:1536]
    ops_f = (r4(wf), r4(kf), r4(kaf))
    ops_b = (r4(wb), r4(kb), r4(kab))
    y_f, hist_f, y_b, hist_b = _scan_fwd(rr, kk4, vv, ops_f, ops_b)
    y_f, y_b = f2(y_f), f2(y_b)
    y_rwkv = _post_fwd(y_f, y_b, zs2, kf, kb, gate, rw)
    q, kfull, v = _mla_fwd(zm, cs, sn, mp, t)
    o, lse = _attn_fwd(q, kfull, v, b, t)
    y_mla = _rms_fwd(o, vec("mla_out_g"), "rms_mla_out")
    ymix = jnp.concatenate([y_rwkv, y_mla], axis=1)
    h1 = _mm(ymix, fw["w_out"], "nn", "proj_out", add=x2)
    n2 = _rms_fwd(h1, vec("ln_ffn_g"), "rms_ffn")
    ug = _mm(n2, w_up_g, "nn", "ffn_up_gate")
    uv = _mm(n2, w_up_v, "nn", "ffn_up_val")
    r3f = lambda a: a.reshape(b, t, D_FF)
    act = _ffn_mid_fwd(r3f(ug), r3f(uv), cw, cb).reshape(m, D_FF)
    h2 = _mm(act, fw["w_ffn_down"], "nn", "ffn_down", add=h1)
    loss_tab, dh2, g_ln_final = _final(h2, vec("ln_final_g"), tgt)

    gfull = {}
    dact = _mm(dh2, fw["w_ffn_down"], "nt", "d_ffn_act")
    gfull["w_ffn_down"] = _mm(act, dh2, "tn", "g_ffn_down")
    dug, duv, tab_g, tab_v = _ffn_mid_bwd(r3f(ug), r3f(uv), cw, cb, r3f(dact))
    dug, duv = dug.reshape(m, D_FF), duv.reshape(m, D_FF)
    gfull["ffn_conv_w"] = jnp.concatenate([tab_g[0:3], tab_v[0:3]], axis=1)
    g_conv_b = jnp.concatenate([tab_g[3:4], tab_v[3:4]], axis=1)
    dn2 = _mm(duv, w_up_v, "nt", "d_ffn_in_val", add=_mm(dug, w_up_g, "nt", "d_ffn_in_gate"))
    gfull["w_ffn_up"] = jnp.concatenate([_mm(n2, dug, "tn", "g_ffn_up_gate"), _mm(n2, duv, "tn", "g_ffn_up_val")], axis=1)
    dh1, g_ln_ffn = _rms_bwd(h1, vec("ln_ffn_g"), dn2, "rms_ffn_bwd", dres=dh2)
    dymix = _mm(dh1, fw["w_out"], "nt", "d_mix")
    gfull["w_out"] = _mm(ymix, dh1, "tn", "g_w_out")
    do, g_mla_out = _rms_bwd(o, vec("mla_out_g"), dymix, "rms_mla_out_bwd", dy_block=1)
    dq, dk, dv = _attn_bwd(q, kfull, v, o, lse, do, b, t)
    dzm, g_qn, g_kvn, g_wq, g_wqr, g_wk, g_wv = _mla_bwd(zm, cs, sn, mp, t, dq, dk, dv)
    gq3, gqr3 = g_wq.reshape(Q_RANK, HEADS, 128), g_wqr.reshape(Q_RANK, HEADS, 128)
    gfull["w_uq"] = jnp.concatenate(
        [gq3[..., :64], gq3[..., 64:96] + _rot_cols_t(gqr3[..., 64:96])], axis=-1
    ).reshape(Q_RANK, HEADS * 96)
    gfull["w_ukv"] = jnp.concatenate(
        [g_wk.reshape(KV_RANK, HEADS, 128)[..., :64], g_wv.reshape(KV_RANK, HEADS, 64)], axis=-1
    ).reshape(KV_RANK, 1024)
    dys, dr_p, dk_p, dv_p, dgate, g_rk, g_lnx_g, g_lnx_b = _post_bwd(y_f, y_b, zs2, kf, kb, gate, rw, dymix)
    (dr_f, dwf, dkf, dkk_f, dkaf, dv_f, dr_b, dwb, dkb, dkk_b, dkab, dv_b) = _scan_bwd(
        rr, kk4, vv, r4(dys), ops_f, hist_f, ops_b, hist_b)
    cts = dict(dwf=f2(dwf), dwb=f2(dwb), dkf=f2(dkf), dkb=f2(dkb), dkk_f=f2(dkk_f), dkk_b=f2(dkk_b), dkaf=f2(dkaf), dkab=f2(dkab),
               dr_f=f2(dr_f), dr_b=f2(dr_b), dr_p=dr_p, dk_p=dk_p, dv_p=dv_p, dg=dgate, dv_f=f2(dv_f), dv_b=f2(dv_b))
    dzs, g_w0, g_w2, g_a0, g_a2, g_g2, g_kk, g_ka = _prep_bwd(zs2, rw, cts)
    dzr, g_mu_p, g_mu_n = _shift_bwd(dzs.reshape(b, t, RWKV_COLS), zr.reshape(b, t, RWKV_COLS), vec("shift_mu_prev"), vec("shift_mu_next"))
    dzr = dzr.reshape(m, RWKV_COLS)
    gfull["decay_w2_fwd"], gfull["decay_w2_bwd"] = g_w2[:64, :512], g_w2[64:, 512:]
    gfull["iclr_a2_fwd"], gfull["iclr_a2_bwd"] = g_a2[:64, :512], g_a2[64:, 512:]
    gfull["gate_g2"] = g_g2
    dn1 = _mm(dzr, win_r, "nt", "d_proj_in_rwkv", add=_mm(dzm, win_m, "nt", "d_proj_in_mla"))
    g_m = _mm(n1, dzm, "tn", "g_w_in_mla")
    g_r = _mm(n1, dzr, "tn", "g_w_in_rwkv")
    g_kr = g_m[:, 1088:1120] + _rot_cols_t(g_m[:, 1216:1248])
    gfull["w_in"] = jnp.concatenate([g_r, g_m[:, :1024], g_kr], axis=1)
    dx, g_ln_mix = _rms_bwd(x2, vec("ln_mix_g"), dn1, "rms_mix_bwd", dres=dh1)

    recv = _scatter_to_chips(_pack_grads(gfull, arrs))
    mine = _add_rows([recv[3], recv[0], recv[1], recv[2]], "sum_chips")
    g_big = _unpack_grads(_add_rows([mine, _swap_with_sibling(mine)], "sum_cores"), arrs)
    small = {
        "ln_mix_g": g_ln_mix, "shift_mu_prev": g_mu_p, "shift_mu_next": g_mu_n, "decay_w0_fwd": g_w0[:, :512],
        "decay_w0_bwd": g_w0[:, 512:], "iclr_a0_fwd": g_a0[:, :512], "iclr_a0_bwd": g_a0[:, 512:], "k_k": g_kk, "k_a": g_ka,
        "r_k": g_rk, "ln_x_g": g_lnx_g, "ln_x_b": g_lnx_b, "q_norm_g": g_qn, "kv_norm_g": g_kvn, "mla_out_g": g_mla_out,
        "ln_ffn_g": g_ln_ffn, "ffn_conv_b": g_conv_b, "ln_final_g": g_ln_final,
        "_loss": jnp.pad(loss_tab[0, 0:1], (0, 127)),
    }
    g_small_buf = _allreduce_small(_pack_small(small))
    g_small = _unpack_small(g_small_buf, arrs)

    grads, deltas, new_m, new_v = {}, {}, {}, {}
    for name, _, _ in _BIG:
        shape = arrs[name].shape
        two = lambda a: a.reshape(shape[1:])
        dlt, m2, v2 = _adamw(two(arrs[name]), g_big[name], two(arrs["m_" + name]), two(arrs["v_" + name]), "adamw_" + name)
        grads[name] = g_big[name].reshape(shape)
        deltas[name], new_m[name], new_v[name] = dlt.reshape(shape), m2.reshape(shape), v2.reshape(shape)
    pk = lambda pre: _pack_small({**{n: arrs[pre + n] for n in _SMALL}, "_loss": jnp.zeros((128,), F32)})
    sd, sm, sv = _adamw(pk(""), g_small_buf, pk("m_"), pk("v_"), "adamw_small")
    sd, sm, sv = _unpack_small(sd, arrs), _unpack_small(sm, arrs), _unpack_small(sv, arrs)
    for n in _SMALL:
        grads[n], deltas[n], new_m[n], new_v[n] = g_small[n], sd[n], sm[n], sv[n]

    return (g_small["_loss"], dx.reshape(b, t, d), *[grads[n] for n in _WEIGHTS], *[deltas[n] for n in _WEIGHTS],
            *[new_m[n] for n in _WEIGHTS], *[new_v[n] for n in _WEIGHTS])
```

```python
import functools
import math

import jax
import jax.numpy as jnp
from jax import lax
from jax.experimental import pallas as pl
from jax.experimental.pallas import tpu as pltpu

F32, BF16 = jnp.float32, jnp.bfloat16
HIGHEST = lax.Precision.HIGHEST
MESH = pl.DeviceIdType.MESH
ANY = pl.BlockSpec(memory_space=pl.ANY)
VMEM = pl.BlockSpec(memory_space=pltpu.VMEM)
BS = pl.BlockSpec
SDS = jax.ShapeDtypeStruct

NORM_EPS = 1e-6
GN_EPS = 64e-5
L2_EPS = 1e-12
HEADS = 8
HEAD_DIM = 64
D_RWKV = HEADS * HEAD_DIM
ROPE_DIM = 32
ROPE_THETA = 10000.0
MLA_SCALE = (64 + ROPE_DIM) ** -0.5
Q_RANK, KV_RANK = 768, 256
RWKV_COLS = 1920
MLA_PAD_COLS = Q_RANK + KV_RANK + 256
D_FF = 2816
ADAM_LR, ADAM_B1, ADAM_B2, ADAM_EPS, ADAM_WD, ADAM_STEP = 0.001, 0.9, 0.999, 1e-08, 0.01, 10

V7X_LANES = 128
V7X_VMEM_LIMIT = 48 * 1024 * 1024
SCAN_CHUNK = 16
N_CHIPS = 4


def _cp(*sem):
    return pltpu.CompilerParams(dimension_semantics=sem, vmem_limit_bytes=V7X_VMEM_LIMIT)


def _tile(n, cands=(512, 640, 384, 256, 128)):
    for c in cands:
        if n % c == 0:
            return c
    return n


def _row_tile(n, cap=256):
    best = n
    for t in range(8, cap + 1, 8):
        if n % t == 0:
            best = t
    return best if best <= cap or n <= cap else n


def _rms(x, g):
    ms = jnp.mean(x * x, axis=-1, keepdims=True)
    return x * lax.rsqrt(ms + NORM_EPS) * g


@jax.custom_vjp
def _bdot(x, w):
    return jnp.dot(x.astype(BF16), w.astype(BF16), preferred_element_type=F32)


def _bdot_fwd(x, w):
    return _bdot(x, w), (x, w)


def _bdot_bwd(res, ct):
    x, w = res
    c = ct.astype(BF16)
    dx = lax.dot_general(c, w.astype(BF16), (((1,), (1,)), ((), ())), preferred_element_type=F32)
    dw = lax.dot_general(x.astype(BF16), c, (((0,), (0,)), ((), ())), preferred_element_type=F32)
    return dx.astype(x.dtype), dw.astype(w.dtype)


_bdot.defvjp(_bdot_fwd, _bdot_bwd)


def _headsum(x, ones_bd):
    return jnp.dot(x, ones_bd, precision=HIGHEST, preferred_element_type=F32)


def _prep_fn(zs, w0, w2, a0, a2, g2, k_k, k_a, ones_bd):
    k = zs[:, 512:1024]
    wd = zs[:, 1536:1664]
    ad = zs[:, 1664:1792]
    gd = zs[:, 1792:1920]
    logit = w0 + _bdot(jnp.tanh(wd), w2)
    w = jnp.exp(-math.exp(-0.5) * jax.nn.sigmoid(logit))
    a = jax.nn.sigmoid(a0 + _bdot(ad, a2))
    g = _bdot(jax.nn.sigmoid(gd), g2)
    kkr = k * k_k
    nrm = jnp.sqrt(_headsum(kkr * kkr, ones_bd))
    kk = kkr / jnp.maximum(nrm, L2_EPS)
    a_f, a_b = a[:, :512], a[:, 512:]
    kf = k * (1.0 + (a_f - 1.0) * k_a)
    kb = k * (1.0 + (a_b - 1.0) * k_a)
    return w[:, :512], w[:, 512:], kf, kb, kk, kk * a_f, kk * a_b, g


def _post_fn(y, r, kf, kb, v, g, r_k, ln_g, ln_b, ones_bd):
    mu =_headsum(y, ones_bd) * (1.0 / HEAD_DIM)
    yc = y - mu
    var = _headsum(yc * yc, ones_bd) * (1.0 / HEAD_DIM)
    yn = yc * lax.rsqrt(var + GN_EPS) * ln_g + ln_b
    bonus = _headsum(r * (kf + kb) * r_k, ones_bd) * v
    return (yn + bonus) * g


def _cat8(x):
    return jnp.concatenate([x] * HEADS, axis=1)


def _mla_fn(zm, cs, sn, gq, gkv, wq, wqr, wk, wv):
    cq = zm[:, :Q_RANK]
    ckv = zm[:, Q_RANK:Q_RANK + KV_RANK]
    kr = zm[:, Q_RANK + KV_RANK:Q_RANK + KV_RANK + 128]
    krr = zm[:, Q_RANK + KV_RANK + 128:]
    cqn = _rms(cq, gq)
    ckvn = _rms(ckv, gkv)
    q = _bdot(cqn, wq) * _cat8(cs) + _bdot(cqn, wqr) * _cat8(sn)
    kro = kr * cs + krr * sn
    kfull = _bdot(ckvn, wk) + _cat8(kro)
    v = _bdot(ckvn, wv)
    return q, kfull, v


def _adamw_math(w, g, m, v):
    m2 = ADAM_B1 * m + (1.0 - ADAM_B1) * g
    v2 = ADAM_B2 * v + (1.0 - ADAM_B2) * (g * g)
    m_hat = m2 / (1.0 - ADAM_B1 ** ADAM_STEP)
    v_hat = v2 / (1.0 - ADAM_B2 ** ADAM_STEP)
    delta = -ADAM_LR * (m_hat / (jnp.sqrt(v_hat) + ADAM_EPS) + ADAM_WD * w)
    return delta, m2, v2


_DIMS = {"nn": (((1,), (0,)), ((), ())), "nt": (((1,), (1,)), ((), ())), "tn": (((0,), (0,)), ((), ()))}


def _mm(a, b, mode, name, out_dtype=F32, add=None):
    if mode == "nn":
        (m, k), (_, n) = a.shape, b.shape
    elif mode == "nt":
        (m, k), (n, _) = a.shape, b.shape
    else:
        (k, m), (_, n) = a.shape, b.shape
    big = (1024, 1408, 768, 640, 512, 384, 256, 128)
    tm, tn, tk = _tile(m, big), _tile(n, big), _tile(k, (512, 1408, 640, 384, 256, 128))
    nk = k // tk

    def body(a_ref, b_ref, *rest):
        if add is None:
            o_ref, acc_ref = rest
        else:
            add_ref, o_ref, acc_ref = rest
        kk = pl.program_id(2)

        @pl.when(kk == 0)
        def _():
            acc_ref[...] = jnp.zeros_like(acc_ref)

        acc_ref[...] += lax.dot_general(
            a_ref[...].astype(BF16), b_ref[...].astype(BF16), _DIMS[mode], preferred_element_type=F32
        )

        @pl.when(kk == nk - 1)
        def _():
            r = acc_ref[...]
            if add is not None:
                r = r + add_ref[...]
            o_ref[...] = r.astype(out_dtype)

    a_spec = BS((tk, tm), lambda i, j, kk: (kk, i)) if mode == "tn" else BS((tm, tk), lambda i, j, kk: (i, kk))
    b_spec = BS((tn, tk), lambda i, j, kk: (j, kk)) if mode == "nt" else BS((tk, tn), lambda i, j, kk: (kk, j))
    o_spec = BS((tm, tn), lambda i, j, kk: (i, j))
    ins, specs = [a, b], [a_spec, b_spec]
    if add is not None:
        ins.append(add)
        specs.append(o_spec)
    return pl.pallas_call(
        body, grid=(m // tm, n // tn, nk), in_specs=specs, out_specs=o_spec, out_shape=SDS((m, n), out_dtype),
        scratch_shapes=[pltpu.VMEM((tm, tn), F32)], compiler_params=_cp("parallel", "parallel", "arbitrary"), name=name,
    )(*ins)


def _rms_fwd(x, g, name):
    m, d = x.shape
    tr = _tile(m)

    def body(x_ref, g_ref, o_ref):
        o_ref[...] = _rms(x_ref[...], g_ref[...]).astype(BF16)

    return pl.pallas_call(
        body, grid=(m // tr,), in_specs=[BS((tr, d), lambda i: (i, 0)), BS((1, d), lambda i: (0, 0))],
        out_specs=BS((tr, d), lambda i: (i, 0)), out_shape=SDS((m, d), BF16), compiler_params=_cp("parallel"), name=name,
    )(x, g)


def _rms_bwd(x, g, dy, name, dres=None, dy_block=0):
    m, d = x.shape
    tr = _row_tile(m)

    def body(x_ref, g_ref, dy_ref, *rest):
        if dres is None:
            dx_ref, dg_ref = rest
        else:
            dres_ref, dx_ref, dg_ref = rest
        _, vjp = jax.vjp(_rms, x_ref[...], g_ref[...])
        dx, dg = vjp(dy_ref[...])
        if dres is not None:
            dx = dx + dres_ref[...]
        dx_ref[...] = dx

        @pl.when(pl.program_id(0) == 0)
        def _():
            dg_ref[...] = jnp.zeros_like(dg_ref)

        dg_ref[...] += dg

    row = BS((tr, d), lambda i: (i, 0))
    vec = BS((1, d), lambda i: (0, 0))
    ins, specs = [x, g, dy], [row, vec, BS((tr, d), lambda i: (i, dy_block))]
    if dres is not None:
        ins.append(dres)
        specs.append(row)
    return pl.pallas_call(
        body, grid=(m // tr,), in_specs=specs, out_specs=[row, vec], out_shape=[SDS((m, d), F32), SDS((1, d), F32)],
        compiler_params=_cp("arbitrary"), name=name,
    )(*ins)


def _final(h, g, tgt):
    m, d = h.shape
    tr = _row_tile(m)

    def loss_fn(hh, gg, tt):
        e = _rms(hh, gg) - tt
        return 0.5 * jnp.sum(e * e) * (1.0 / d)

    def body(h_ref, g_ref, t_ref, l_ref, dh_ref, dg_ref):
        val, (dh, dg) = jax.value_and_grad(loss_fn, argnums=(0, 1))(h_ref[...], g_ref[...], t_ref[...])
        dh_ref[...] = dh

        @pl.when(pl.program_id(0) == 0)
        def _():
            dg_ref[...] = jnp.zeros_like(dg_ref)
            l_ref[...] = jnp.zeros_like(l_ref)

        dg_ref[...] += dg
        l_ref[...] += jnp.full(l_ref.shape, val, F32)

    row = BS((tr, d), lambda i: (i, 0))
    vec = BS((1, d), lambda i: (0, 0))
    return pl.pallas_call(
        body, grid=(m // tr,), in_specs=[row, vec, row], out_specs=[BS((8, 128), lambda i: (0, 0)), row, vec],
        out_shape=[SDS((8, 128), F32), SDS((m, d), F32), SDS((1, d), F32)], compiler_params=_cp("arbitrary"), name="final_loss",
    )(h, g, tgt)


def _prev_next(z, t):
    row = lax.broadcasted_iota(jnp.int32, z.shape, 0)
    zp = jnp.where(row == 0, 0.0, pltpu.roll(z, 1, axis=0))
    zn = jnp.where(row == t - 1, 0.0, pltpu.roll(z, t - 1, axis=0))
    return zp, zn


def _shift_fwd(z3, mu_p, mu_n):
    b, t, c = z3.shape
    nc = c // 128

    def body(z_ref, mp_ref, mn_ref, o_ref):
        z = z_ref[0]
        zp, zn = _prev_next(z, t)
        o_ref[0] = z + mp_ref[...] * (zp - z) + mn_ref[...] * (zn - z)

    blk = BS((1, t, 128), lambda i, j: (i, 0, j))
    vec = BS((1, 128), lambda i, j: (0, j))
    return pl.pallas_call(
        body, grid=(b, nc), in_specs=[blk, vec, vec], out_specs=blk, out_shape=SDS((b, t, c), F32),
        compiler_params=_cp("parallel", "parallel"), name="shift_fwd",
    )(z3, mu_p, mu_n)


def _shift_bwd(dzs3, z3, mu_p, mu_n):
    b, t, c = z3.shape
    nc = c // 128

    def body(d_ref, z_ref, mp_ref, mn_ref, dz_ref, dmp_ref, dmn_ref):
        d, z = d_ref[0], z_ref[0]
        mp, mn = mp_ref[...], mn_ref[...]
        zp, zn = _prev_next(z, t)
        _, dp_next = _prev_next(d * mp, t)
        dn_prev, _ = _prev_next(d * mn, t)
        dz_ref[0] = d * (1.0 - mp - mn) + dp_next + dn_prev

        @pl.when(pl.program_id(1) == 0)
        def _():
            dmp_ref[...] = jnp.zeros_like(dmp_ref)
            dmn_ref[...] = jnp.zeros_like(dmn_ref)

        dmp_ref[...] += jnp.sum(d * (zp - z), axis=0, keepdims=True)
        dmn_ref[...] += jnp.sum(d * (zn - z), axis=0, keepdims=True)

    blk = BS((1, t, 128), lambda j, i: (i, 0, j))
    vec = BS((1, 128), lambda j, i: (0, j))
    return pl.pallas_call(
        body, grid=(nc, b), in_specs=[blk, blk, vec, vec], out_specs=[blk, vec, vec],
        out_shape=[SDS((b, t, c), F32), SDS((1, c), F32), SDS((1, c), F32)],
        compiler_params=_cp("parallel", "arbitrary"), name="shift_bwd",
    )(dzs3, z3, mu_p, mu_n)


def _const(shape):
    nd = len(shape)
    return BS(shape, lambda i: (0,) * nd)


def _prep_fwd(zs, p):
    m = zs.shape[0]
    tr = 256
    params = [p["w0"], p["w2"], p["a0"], p["a2"], p["g2"], p["k_k"], p["k_a"], p["ones_bd"]]

    def body(zs_ref, w0, w2, a0, a2, g2, kk_, ka_, bd, wf, wb, kf, kb, kk, kaf, kab, g):
        outs = _prep_fn(zs_ref[...], w0[...], w2[...], a0[...], a2[...], g2[...], kk_[...], ka_[...], bd[...])
        for ref, val in zip((wf, wb, kf, kb, kk, kaf, kab, g), outs):
            ref[...] = val

    row = BS((tr, 512), lambda i: (i, 0))
    return pl.pallas_call(
        body, grid=(m // tr,), in_specs=[BS((tr, RWKV_COLS), lambda i: (i, 0))] + [_const(q.shape) for q in params],
        out_specs=[row] * 8, out_shape=[SDS((m, 512), F32)] * 8, compiler_params=_cp("parallel"), name="rwkv_prep_fwd",
    )(zs, *params)


def _prep_bwd(zs, p, ct_rows):
    m = zs.shape[0]
    tr = 128
    params = [p["w0"], p["w2"], p["a0"], p["a2"], p["g2"], p["k_k"], p["k_a"]]
    names = ["dwf", "dwb", "dkf", "dkb", "dkk_f", "dkk_b", "dkaf", "dkab", "dr_f", "dr_b", "dr_p", "dk_p", "dv_p", "dg",
             "dv_f", "dv_b"]
    rows = [ct_rows[n] for n in names]

    def body(zs_ref, w0, w2, a0, a2, g2, kk_, ka_, bd, *rest):
        c = {n: r[...] for n, r in zip(names, rest[:len(names)])}
        outs = rest[len(names):]
        dzs_ref, grads = outs[0], outs[1:]
        ones_bd = bd[...]
        _, vjp = jax.vjp(
            lambda *q: _prep_fn(*q, ones_bd), zs_ref[...], w0[...], w2[...], a0[...], a2[...], g2[...], kk_[...], ka_[...]
        )
        cts = (c["dwf"], c["dwb"], c["dkf"] + c["dk_p"], c["dkb"] + c["dk_p"], c["dkk_f"] + c["dkk_b"], c["dkaf"], c["dkab"], c["dg"])
        dzs, *dparams = vjp(cts)
        dr = c["dr_f"] + c["dr_b"] + c["dr_p"]
        dv = c["dv_f"] + c["dv_b"] + c["dv_p"]
        dzs_ref[:, 0:512] = dzs[:, 0:512] + dr
        dzs_ref[:, 512:1024] = dzs[:, 512:1024]
        dzs_ref[:, 1024:1536] = dzs[:, 1024:1536] + dv
        dzs_ref[:, 1536:1920] = dzs[:, 1536:1920]

        @pl.when(pl.program_id(0) == 0)
        def _():
            for gr in grads:
                gr[...] = jnp.zeros_like(gr)

        for gr, val in zip(grads, dparams):
            gr[...] += val

    row = BS((tr, 512), lambda i: (i, 0))
    return pl.pallas_call(
        body, grid=(m // tr,),
        in_specs=[BS((tr, RWKV_COLS), lambda i: (i, 0))] + [_const(q.shape) for q in params] + [_const(p["ones_bd"].shape)]
        + [row] * len(names),
        out_specs=[BS((tr, RWKV_COLS), lambda i: (i, 0))] + [_const(q.shape) for q in params],
        out_shape=[SDS((m, RWKV_COLS), F32)] + [SDS(q.shape, F32) for q in params],
        compiler_params=_cp("arbitrary"), name="rwkv_prep_bwd",
    )(zs, *params, p["ones_bd"], *rows)


def _post_specs(tr):
    r = BS((tr, 512), lambda i: (i, 0))
    v = BS((tr, 512), lambda i: (i, 2))
    row = BS((tr, 512), lambda i: (i, 0))
    return r, v, row


def _post_fwd(y_f, y_b, zs, kf, kb, g, p):
    m = zs.shape[0]
    tr = 256
    r, v, row = _post_specs(tr)
    vecs = [p["r_k"], p["ln_x_g"], p["ln_x_b"], p["ones_bd"]]

    def body(yf, yb, r_ref, v_ref, kf_ref, kb_ref, g_ref, rk, lg, lb, bd, o_ref):
        o_ref[...] = _post_fn(
            yf[...] + yb[...], r_ref[...], kf_ref[...], kb_ref[...], v_ref[...], g_ref[...], rk[...], lg[...], lb[...], bd[...]
        ).astype(BF16)

    return pl.pallas_call(
        body, grid=(m // tr,), in_specs=[row, row, r, v, row, row, row] + [_const(q.shape) for q in vecs],
        out_specs=row, out_shape=SDS((m, 512), BF16), compiler_params=_cp("parallel"), name="rwkv_post_fwd",
    )(y_f, y_b, zs, zs, kf, kb, g, *vecs)


def _post_bwd(y_f, y_b, zs, kf, kb, g, p, dymix):
    m = zs.shape[0]
    tr = 128
    r, v, row = _post_specs(tr)
    vecs = [p["r_k"], p["ln_x_g"], p["ln_x_b"]]

    def body(yf, yb, r_ref, v_ref, kf_ref, kb_ref, g_ref, rk, lg, lb, bd, dy_ref, dyo, dr, dk, dv, dg, drk, dlg, dlb):
        ones_bd = bd[...]
        _, vjp = jax.vjp(
            lambda *q: _post_fn(*q, ones_bd),
            yf[...] + yb[...], r_ref[...], kf_ref[...], kb_ref[...], v_ref[...], g_ref[...], rk[...], lg[...], lb[...],
        )
        c_y, c_r, c_kf, _, c_v, c_g, c_rk, c_lg, c_lb = vjp(dy_ref[...])
        dyo[...] = c_y
        dr[...] = c_r
        dk[...] = c_kf
        dv[...] = c_v
        dg[...] = c_g

        @pl.when(pl.program_id(0) == 0)
        def _():
            for ref in (drk, dlg, dlb):
                ref[...] = jnp.zeros_like(ref)

        drk[...] += c_rk
        dlg[...] += c_lg
        dlb[...] += c_lb

    vec = _const((1, 512))
    return pl.pallas_call(
        body, grid=(m // tr,),
        in_specs=[row, row, r, v, row, row, row] + [_const(q.shape) for q in vecs] + [_const(p["ones_bd"].shape), row],
        out_specs=[row, row, row, row, row, vec, vec, vec],
        out_shape=[SDS((m, 512), F32)] * 5 + [SDS((1, 512), F32)] * 3,
        compiler_params=_cp("arbitrary"), name="rwkv_post_bwd",
    )(y_f, y_b, zs, zs, kf, kb, g, *vecs, p["ones_bd"], dymix)


def _half_ones():
    ri = lax.broadcasted_iota(jnp.int32, (256, 128), 0) & 127
    ci = lax.broadcasted_iota(jnp.int32, (256, 128), 1)
    return jnp.where((ri < 64) == (ci < 64), 1.0, 0.0).astype(BF16)


def _half_sums(xs, ones, exact):
    out = [None] * len(xs)
    for two in (True, False):
        idx = [i for i, e in enumerate(exact) if bool(e) == two]
        if not idx:
            continue
        x = jnp.concatenate([xs[i] for i in idx], axis=0)
        hi = x.astype(BF16)
        if two:
            mid = (x - hi.astype(F32)).astype(BF16)
            res = jnp.dot(jnp.concatenate([hi, mid], axis=1), ones, preferred_element_type=F32)
        else:
            res = jnp.dot(hi, ones[:128], preferred_element_type=F32)
        for j, i in enumerate(idx):
            out[i] = res[64 * j:64 * j + 64]
    return out


def _scan_specs(b, t):
    nc = t // SCAN_CHUNK
    up, down = (lambda c: c), (lambda c: nc - 1 - c)
    rows = [BS((b, SCAN_CHUNK, 512), lambda c, ci=ci: (0, ci(c), 0)) for ci in (up, down)]
    vrows = [BS((b, SCAN_CHUNK, 512), lambda c, ci=ci: (0, ci(c), 2)) for ci in (up, down)]
    hist = [BS((SCAN_CHUNK, b * 4, 64, 128), lambda c, ci=ci: (ci(c), 0, 0, 0)) for ci in (up, down)]
    return nc, rows, vrows, hist


class _Window:
    def __init__(self, g, ascending):
        self.bases = [pl.multiple_of(g * 8, 8) if asc else pl.multiple_of(SCAN_CHUNK - 8 - g * 8, 8) for asc in ascending]
        self.ascending = ascending
        self.blocks = {}
        self.row_id = lax.broadcasted_iota(jnp.int32, (8, 128), 0)

    def j(self, d, s):
        return s if self.ascending[d] else 7 - s

    def time(self, d, s):
        return self.bases[d] + self.j(d, s)

    def row(self, ref, d, bi, cols, s):
        key = (id(ref), d, bi, cols.start)
        if key not in self.blocks:
            self.blocks[key] = ref[bi, pl.ds(self.bases[d], 8), cols]
        jj = self.j(d, s)
        return self.blocks[key][jj:jj + 1, :]

    def put(self, buf, key, d, s, row):
        prev = buf.get(key)
        new = jnp.broadcast_to(row, (8, 128))
        buf[key] = new if prev is None else jnp.where(self.row_id == self.j(d, s), new, prev)

    def flush(self, buf, refs_of):
        for key, val in buf.items():
            ref, d, bi, cols = refs_of(key)
            ref[bi, pl.ds(self.bases[d], 8), cols] = val


def _pairs(b):
    return [(bi * 4 + p, bi, slice(128 * p, 128 * p + 128)) for bi in range(b) for p in range(4)]


def _colsum(x):
    return jnp.sum(x, axis=0, keepdims=True)


def _eye_mask():
    return (lax.broadcasted_iota(jnp.int32, (64, 128), 1) & 63) == lax.broadcasted_iota(jnp.int32, (64, 128), 0)


def _scan_fwd(zs, kk, ops_f, ops_b):
    b, t = zs.shape[:2]
    nc, rows, vrows, hist = _scan_specs(b, t)
    npair = b * 4

    def body(*refs):
        ins, outs, s_ref = refs[:12], refs[12:16], refs[16]
        dirs = [dict(zip(("r", "kk", "v", "w", "k", "ka", "y", "h"), (*ins[6 * d:6 * d + 6], *outs[2 * d:2 * d + 2])))
                for d in (0, 1)]

        @pl.when(pl.program_id(0) == 0)
        def _():
            s_ref[...] = jnp.zeros_like(s_ref)

        ones, eye = _half_ones(), _eye_mask()
        chains = [(d, pr, bi, cols) for d in (0, 1) for pr, bi, cols in _pairs(b)]

        def eight_steps(g, carry):
            win = _Window(g, (True, False))
            ybuf = {}
            for s in range(8):
                s_prev, xa = [], []
                for d, pr, bi, cols in chains:
                    q = dirs[d]
                    st = s_ref[d * npair + pr]
                    q["h"][win.time(d, s), pr] = st
                    s_prev.append(st)
                    xa += [st * win.row(q["kk"], d, bi, cols, s), jnp.where(eye, win.row(q["v"], d, bi, cols, s), 0.0)]
                ra = _half_sums(xa, ones, (True, False) * len(chains))
                xb = []
                for i, (d, pr, bi, cols) in enumerate(chains):
                    q = dirs[d]
                    s_new = s_prev[i] * win.row(q["w"], d, bi, cols, s) - ra[2 * i] * win.row(q["ka"], d, bi, cols, s) \
                        + ra[2 * i + 1] * win.row(q["k"], d, bi, cols, s)
                    s_ref[d * npair + pr] = s_new
                    xb.append(s_new * win.row(q["r"], d, bi, cols, s))
                rb = _half_sums(xb, ones, (False,) * len(chains))
                for i, (d, pr, bi, cols) in enumerate(chains):
                    win.put(ybuf, i, d, s, _colsum(jnp.where(eye, rb[i], 0.0)))
            win.flush(ybuf, lambda i: (dirs[chains[i][0]]["y"], chains[i][0], chains[i][2], chains[i][3]))
            return carry

        lax.fori_loop(0, SCAN_CHUNK // 8, eight_steps, 0)

    row_shape, hist_shape = SDS((b, t, 512), F32), SDS((t, npair, 64, 128), F32)
    return pl.pallas_call(
        body, grid=(nc,), in_specs=sum(([rows[d], rows[d], vrows[d]] + [rows[d]] * 3 for d in (0, 1)), []),
        out_specs=[rows[0], hist[0], rows[1], hist[1]], out_shape=[row_shape, hist_shape, row_shape, hist_shape],
        scratch_shapes=[pltpu.VMEM((2 * npair, 64, 128), F32)], compiler_params=_cp("arbitrary"), name="wkv_scan",
    )(zs, kk, zs, *ops_f, zs, kk, zs, *ops_b)


def _scan_bwd(zs, kk, dy, ops_f, hist_f, ops_b, hist_b):
    b, t = zs.shape[:2]
    nc, rows, vrows, hist = _scan_specs(b, t)
    npair = b * 4
    names_in = ("r", "kk", "v", "dy", "w", "k", "ka", "h")
    names_out = ("dr", "dw", "dk", "dkk", "dka", "dv")

    def body(*refs):
        ins, outs, ds_ref = refs[:16], refs[16:28], refs[28]
        dirs = [dict(zip(names_in + names_out, (*ins[8 * d:8 * d + 8], *outs[6 * d:6 * d + 6]))) for d in (0, 1)]

        @pl.when(pl.program_id(0) == 0)
        def _():
            ds_ref[...] = jnp.zeros_like(ds_ref)

        ones, eye = _half_ones(), _eye_mask()
        chains = [(d, pr, bi, cols) for d in (0, 1) for pr, bi, cols in _pairs(b)]

        def eight_steps(g, carry):
            win = _Window(g, (False, True))
            obuf = {}
            for s in range(8):
                row = lambda name, d, bi, cols: win.row(dirs[d][name], d, bi, cols, s)
                s_prev, xa = [], []
                for d, pr, bi, cols in chains:
                    st = dirs[d]["h"][win.time(d, s), pr]
                    s_prev.append(st)
                    xa += [st * row("kk", d, bi, cols), jnp.where(eye, row("v", d, bi, cols), 0.0),
                           jnp.where(eye, row("dy", d, bi, cols), 0.0)]
                ra = _half_sums(xa, ones, (True, False, False) * len(chains))
                ds_now, xb = [], []
                for i, (d, pr, bi, cols) in enumerate(chains):
                    skk, vcol, dycol = ra[3 * i], ra[3 * i + 1], ra[3 * i + 2]
                    ka_r, k_r = row("ka", d, bi, cols), row("k", d, bi, cols)
                    s_new = s_prev[i] * row("w", d, bi, cols) - skk * ka_r + vcol * k_r
                    ds = ds_ref[d * npair + pr] + dycol * row("r", d, bi, cols)
                    win.put(obuf, (i, "dr"), d, s, _colsum(s_new * dycol))
                    win.put(obuf, (i, "dk"), d, s, _colsum(ds * vcol))
                    win.put(obuf, (i, "dka"), d, s, -_colsum(ds * skk))
                    win.put(obuf, (i, "dw"), d, s, _colsum(ds * s_prev[i]))
                    ds_now.append(ds)
                    xb += [ds * k_r, ds * ka_r]
                rb = _half_sums(xb, ones, (False, True) * len(chains))
                for i, (d, pr, bi, cols) in enumerate(chains):
                    dskk = -rb[2 * i + 1]
                    win.put(obuf, (i, "dv"), d, s, _colsum(jnp.where(eye, rb[2 * i], 0.0)))
                    win.put(obuf, (i, "dkk"), d, s, _colsum(s_prev[i] * dskk))
                    ds_ref[d * npair + pr] = ds_now[i] * row("w", d, bi, cols) + dskk * row("kk", d, bi, cols)
            win.flush(obuf, lambda key: (dirs[chains[key[0]][0]][key[1]], chains[key[0]][0], chains[key[0]][2], chains[key[0]][3]))
            return carry

        lax.fori_loop(0, SCAN_CHUNK // 8, eight_steps, 0)

    row_shape = SDS((b, t, 512), F32)
    return pl.pallas_call(
        body, grid=(nc,), in_specs=sum(([rows[d], rows[d], vrows[d]] + [rows[d]] * 4 + [hist[d]] for d in (1, 0)), []),
        out_specs=[rows[1]] * 6 + [rows[0]] * 6, out_shape=[row_shape] * 12,
        scratch_shapes=[pltpu.VMEM((2 * npair, 64, 128), F32)], compiler_params=_cp("arbitrary"), name="wkv_scan_bwd",
    )(zs, kk, zs, dy, *ops_f, hist_f, zs, kk, zs, dy, *ops_b, hist_b)


def _mla_fwd(zm, cs, sn, p, t):
    m = zm.shape[0]
    tr = 256
    per = t // tr
    params = [p["q_norm_g"], p["kv_norm_g"], p["wq"], p["wqr"], p["wk"], p["wv"]]

    def body(z_ref, cs_ref, sn_ref, gq, gkv, wq, wqr, wk, wv, q_ref, k_ref, v_ref):
        q, kf, v = _mla_fn(z_ref[...], cs_ref[...], sn_ref[...], gq[...], gkv[...], wq[...], wqr[...], wk[...], wv[...])
        q_ref[...] = q.astype(BF16)
        k_ref[...] = kf.astype(BF16)
        v_ref[...] = v.astype(BF16)

    tab = BS((tr, 128), lambda i: (i % per, 0))
    return pl.pallas_call(
        body, grid=(m // tr,), in_specs=[BS((tr, MLA_PAD_COLS), lambda i: (i, 0)), tab, tab] + [_const(q.shape) for q in params],
        out_specs=[BS((tr, 1024), lambda i: (i, 0)), BS((tr, 1024), lambda i: (i, 0)), BS((tr, 512), lambda i: (i, 0))],
        out_shape=[SDS((m, 1024), BF16), SDS((m, 1024), BF16), SDS((m, 512), BF16)], compiler_params=_cp("parallel"), name="mla_prep_fwd",
    )(zm, cs, sn, *params)


def _mla_bwd(zm, cs, sn, p, t, dq, dk, dv):
    m = zm.shape[0]
    tr = 128
    per = t // tr
    params = [p["q_norm_g"], p["kv_norm_g"], p["wq"], p["wqr"], p["wk"], p["wv"]]

    def body(z_ref, cs_ref, sn_ref, gq, gkv, wq, wqr, wk, wv, dq_ref, dk_ref, dv_ref, dz_ref, *grads):
        cs_v, sn_v = cs_ref[...], sn_ref[...]
        _, vjp = jax.vjp(
            lambda *q: _mla_fn(q[0], cs_v, sn_v, *q[1:]), z_ref[...], gq[...], gkv[...], wq[...], wqr[...], wk[...], wv[...]
        )
        dz, *dparams = vjp((dq_ref[...], dk_ref[...], dv_ref[...]))
        dz_ref[...] = dz

        @pl.when(pl.program_id(0) == 0)
        def _():
            for gr in grads:
                gr[...] = jnp.zeros_like(gr)

        for gr, val in zip(grads, dparams):
            gr[...] += val

    tab = BS((tr, 128), lambda i: (i % per, 0))
    wide = BS((tr, 1024), lambda i: (i, 0))
    return pl.pallas_call(
        body, grid=(m // tr,),
        in_specs=[BS((tr, MLA_PAD_COLS), lambda i: (i, 0)), tab, tab] + [_const(q.shape) for q in params]
        + [wide, wide, BS((tr, 512), lambda i: (i, 0))],
        out_specs=[BS((tr, MLA_PAD_COLS), lambda i: (i, 0))] + [_const(q.shape) for q in params],
        out_shape=[SDS((m, MLA_PAD_COLS), F32)] + [SDS(q.shape, F32) for q in params],
        compiler_params=_cp("arbitrary"), name="mla_prep_bwd",
    )(zm, cs, sn, *params, dq, dk, dv)


_NT = (((1,), (1,)), ((), ()))
_TN = (((0,), (0,)), ((), ()))


def _attn_fwd(q, kf, v, b, t):
    m = q.shape[0]
    tq = 256
    nq = t // tq

    def body(q_ref, k_ref, v_ref, o_ref, l_ref):
        lo = lax.broadcasted_iota(jnp.int32, (1, 128), 1) < 64
        v_all = v_ref[...]
        o = jnp.zeros((tq, 128), F32)
        lse = []
        for h in range(2):
            hs = slice(128 * h, 128 * h + 128)
            s = lax.dot_general(q_ref[:, hs], k_ref[:, hs], _NT, preferred_element_type=F32) * MLA_SCALE
            mx = jnp.max(s, axis=1, keepdims=True)
            e = jnp.exp(s - mx)
            den = jnp.sum(e, axis=1, keepdims=True)
            vh = jnp.where(lo if h == 0 else jnp.logical_not(lo), v_all, jnp.zeros_like(v_all))
            o = o + jnp.dot(e.astype(BF16), vh, preferred_element_type=F32) / den
            lse.append(mx + jnp.log(den))
        o_ref[...] = o
        l_ref[...] = jnp.where(lo, lse[0], lse[1])

    return pl.pallas_call(
        body, grid=(b, 4, nq),
        in_specs=[BS((tq, 256), lambda bi, hp, i: (bi * nq + i, hp)), BS((t, 256), lambda bi, hp, i: (bi, hp)),
                  BS((t, 128), lambda bi, hp, i: (bi, hp))],
        out_specs=[BS((tq, 128), lambda bi, hp, i: (bi * nq + i, hp))] * 2,
        out_shape=[SDS((m, 512), F32), SDS((m, 512), F32)], compiler_params=_cp("parallel", "parallel", "arbitrary"), name="attn_fwd",
    )(q, kf, v)


def _attn_bwd(q, kf, v, o, lse, do, b, t):
    m = q.shape[0]
    tq = 256
    nq = t // tq

    def body(q_ref, k_ref, v_ref, o_ref, l_ref, do_ref, dq_ref, dk_ref, dv_ref):
        lo = lax.broadcasted_iota(jnp.int32, (1, 128), 1) < 64

        @pl.when(pl.program_id(2) == 0)
        def _():
            dk_ref[...] = jnp.zeros_like(dk_ref)
            dv_ref[...] = jnp.zeros_like(dv_ref)

        v_all, o_all, l_all, do_all = v_ref[...], o_ref[...], l_ref[...], do_ref[...]
        dv_acc = jnp.zeros((t, 128), F32)
        for h in range(2):
            hs = slice(128 * h, 128 * h + 128)
            mask = lo if h == 0 else jnp.logical_not(lo)
            qh, kh = q_ref[:, hs], k_ref[:, hs]
            s = lax.dot_general(qh, kh, _NT, preferred_element_type=F32) * MLA_SCALE
            lse_h = jnp.max(jnp.where(mask, l_all, -jnp.inf), axis=1, keepdims=True)
            pr = jnp.exp(s - lse_h)
            do_h = jnp.where(mask, do_all, 0.0)
            dp = lax.dot_general(do_h.astype(BF16), v_all, _NT, preferred_element_type=F32)
            dsum = jnp.sum(do_h * o_all, axis=1, keepdims=True)
            ds = (pr * (dp - dsum) * MLA_SCALE).astype(BF16)
            dq_ref[:, hs] = jnp.dot(ds, kh, preferred_element_type=F32)
            dk_ref[:, hs] += lax.dot_general(ds, qh, _TN, preferred_element_type=F32)
            dv_acc = dv_acc + lax.dot_general(pr.astype(BF16), do_h.astype(BF16), _TN, preferred_element_type=F32)
        dv_ref[...] += dv_acc

    qspec = BS((tq, 256), lambda bi, hp, i: (bi * nq + i, hp))
    kspec = BS((t, 256), lambda bi, hp, i: (bi, hp))
    vspec = BS((t, 128), lambda bi, hp, i: (bi, hp))
    ospec = BS((tq, 128), lambda bi, hp, i: (bi * nq + i, hp))
    return pl.pallas_call(
        body, grid=(b, 4, nq), in_specs=[qspec, kspec, vspec, ospec, ospec, ospec], out_specs=[qspec, kspec, vspec],
        out_shape=[SDS((m, 1024), F32), SDS((m, 1024), F32), SDS((m, 512), F32)],
        compiler_params=_cp("parallel", "parallel", "arbitrary"), name="attn_bwd",
    )(q, kf, v, o, lse, do)


def _conv3(u, w_ref, b_ref, t):
    up, un = _prev_next(u, t)
    return w_ref[0:1, :] * up + w_ref[1:2, :] * u + w_ref[2:3, :] * un + b_ref[...], up, un


def _ffn_mid_fwd(ug3, uv3, cw, cb):
    b, t, f = ug3.shape
    nc = f // 256

    def body(ug_ref, uv_ref, wg_ref, wv_ref, bg_ref, bv_ref, a_ref):
        gc, _, _ = _conv3(ug_ref[0], wg_ref, bg_ref, t)
        vc, _, _ = _conv3(uv_ref[0], wv_ref, bv_ref, t)
        a_ref[0] = (gc * jax.nn.sigmoid(gc) * vc).astype(BF16)

    blk = BS((1, t, 256), lambda i, j: (i, 0, j))
    return pl.pallas_call(
        body, grid=(b, nc),
        in_specs=[blk, blk, BS((3, 256), lambda i, j: (0, j)), BS((3, 256), lambda i, j: (0, j + nc)),
                  BS((1, 256), lambda i, j: (0, j)), BS((1, 256), lambda i, j: (0, j + nc))],
        out_specs=blk, out_shape=SDS((b, t, f), BF16), compiler_params=_cp("parallel", "parallel"), name="ffn_mid_fwd",
    )(ug3, uv3, cw, cw, cb, cb)


def _ffn_mid_bwd(ug3, uv3, cw, cb, da3):
    b, t, f = ug3.shape
    nc = f // 256

    def half(u, up, un, dc, w_ref):
        dprev, dnext = _prev_next(dc, t)
        du = w_ref[1:2, :] * dc + w_ref[0:1, :] * dnext + w_ref[2:3, :] * dprev
        sums = [jnp.sum(dc * q, axis=0, keepdims=True) for q in (up, u, un)] + [jnp.sum(dc, axis=0, keepdims=True)]
        row = lax.broadcasted_iota(jnp.int32, (8, 256), 0)
        tab = jnp.zeros((8, 256), F32)
        for i, s in enumerate(sums):
            tab = jnp.where(row == i, s, tab)
        return du, tab

    def body(ug_ref, uv_ref, wg_ref, wv_ref, bg_ref, bv_ref, da_ref, dug_ref, duv_ref, tg_ref, tv_ref):
        ug, uv, da = ug_ref[0], uv_ref[0], da_ref[0]
        gc, gp, gn = _conv3(ug, wg_ref, bg_ref, t)
        vc, vp, vn = _conv3(uv, wv_ref, bv_ref, t)
        sg = jax.nn.sigmoid(gc)
        d_gc = da * vc * (sg * (1.0 + gc * (1.0 - sg)))
        d_vc = da * (gc * sg)
        dug, tg = half(ug, gp, gn, d_gc, wg_ref)
        duv, tv = half(uv, vp, vn, d_vc, wv_ref)
        dug_ref[0] = dug
        duv_ref[0] = duv

        @pl.when(pl.program_id(1) == 0)
        def _():
            tg_ref[...] = jnp.zeros_like(tg_ref)
            tv_ref[...] = jnp.zeros_like(tv_ref)

        tg_ref[...] += tg
        tv_ref[...] += tv

    blk = BS((1, t, 256), lambda j, i: (i, 0, j))
    tab = BS((8, 256), lambda j, i: (0, j))
    return pl.pallas_call(
        body, grid=(nc, b),
        in_specs=[blk, blk, BS((3, 256), lambda j, i: (0, j)), BS((3, 256), lambda j, i: (0, j + nc)),
                  BS((1, 256), lambda j, i: (0, j)), BS((1, 256), lambda j, i: (0, j + nc)), blk],
        out_specs=[blk, blk, tab, tab],
        out_shape=[SDS((b, t, f), F32), SDS((b, t, f), F32), SDS((8, f), F32), SDS((8, f), F32)],
        compiler_params=_cp("parallel", "arbitrary"), name="ffn_mid_bwd",
    )(ug3, uv3, cw, cw, cb, cb, da3)


def _add_rows(parts, name, out_dtype=F32):
    r = parts[0].shape[0]
    tr = _row_tile(r, 1024)
    n = len(parts)

    def body(*refs):
        acc = refs[0][...].astype(F32)
        for q in refs[1:n]:
            acc = acc + q[...].astype(F32)
        refs[n][...] = acc.astype(out_dtype)

    row = BS((tr, 128), lambda i: (i, 0))
    return pl.pallas_call(
        body, grid=(r // tr,), in_specs=[row] * n, out_specs=row, out_shape=SDS((r, 128), out_dtype),
        compiler_params=_cp("parallel"), name=name,
    )(*parts)


def _adamw(w, g, m, v, name):
    r, c = w.shape
    tr = _row_tile(r)

    def body(w_ref, g_ref, m_ref, v_ref, d_ref, m2_ref, v2_ref):
        d, m2, v2 = _adamw_math(w_ref[...], g_ref[...], m_ref[...], v_ref[...])
        d_ref[...] = d
        m2_ref[...] = m2
        v2_ref[...] = v2

    blk = BS((tr, c), lambda i: (i, 0))
    return pl.pallas_call(
        body, grid=(r // tr,), in_specs=[blk] * 4, out_specs=[blk] * 3, out_shape=[SDS((r, c), F32)] * 3,
        compiler_params=_cp("parallel"), name=name,
    )(w, g, m, v)


def _place():
    return lax.axis_index("x"), lax.axis_index("y"), lax.axis_index("c")


def _flip(v, bit):
    return 1 - v if bit else v


def _allgather_weights(shard):
    r = shard.shape[0]
    rh = r // 2

    def body(x_ref, out_ref, send_sems, recv_sems, local_sem):
        x, y, c = _place()
        me, sibling = (x, y, c), (x, y, 1 - c)
        chips = [(1 - x, y), (x, 1 - y), (1 - x, 1 - y)]
        mine_src = x_ref.at[pl.ds(c * rh, rh), :]

        def rows(px, py, pc):
            return out_ref.at[pl.ds((4 * px + 2 * py + pc) * rh, rh), :]

        def copy(k, block, to, src=None):
            return pltpu.make_async_remote_copy(
                src_ref=rows(*block) if src is None else src, dst_ref=rows(*block), send_sem=send_sems.at[k],
                recv_sem=recv_sems.at[k], device_id=to, device_id_type=MESH,
            )

        mine = pltpu.make_async_copy(mine_src, rows(*me), local_sem)
        mine.start()
        first = [copy(0, me, sibling, src=mine_src)]
        first += [copy(1 + j, me, (*chip, c), src=mine_src) for j, chip in enumerate(chips)]
        for cp in first:
            cp.start()
        passed = [copy(4 + j, (*chip, c), sibling) for j, chip in enumerate(chips)]
        for j, chip in enumerate(chips):
            copy(1 + j, (*chip, c), me).wait_recv()
            passed[j].start()
        copy(0, sibling, me).wait_recv()
        for j, chip in enumerate(chips):
            copy(4 + j, (*chip, 1 - c), me).wait_recv()
        for cp in first + passed:
            cp.wait_send()
        mine.wait()

    return pl.pallas_call(
        body, out_shape=SDS((8 * rh, 128), shard.dtype), in_specs=[ANY], out_specs=ANY,
        scratch_shapes=[pltpu.SemaphoreType.DMA((7,)), pltpu.SemaphoreType.DMA((7,)), pltpu.SemaphoreType.DMA],
        name="allgather_weights",
    )(shard)


def _scatter_to_chips(g):
    def body(g_ref, recv_ref, send_sems, recv_sems, local_sem):
        x, y, c = _place()
        copies = []
        for j, (fx, fy) in enumerate(((1, 0), (0, 1), (1, 1))):
            px, py = _flip(x, fx), _flip(y, fy)
            cp = pltpu.make_async_remote_copy(
                src_ref=g_ref.at[2 * px + py], dst_ref=recv_ref.at[j], send_sem=send_sems.at[j], recv_sem=recv_sems.at[j],
                device_id=(px, py, c), device_id_type=MESH,
            )
            cp.start()
            copies.append(cp)
        own = pltpu.make_async_copy(g_ref.at[2 * x + y], recv_ref.at[3], local_sem)
        own.start()
        for cp in copies:
            cp.wait_recv()
        for cp in copies:
            cp.wait_send()
        own.wait()

    return pl.pallas_call(
        body, out_shape=SDS(g.shape, g.dtype), in_specs=[ANY], out_specs=ANY,
        scratch_shapes=[pltpu.SemaphoreType.DMA((3,)), pltpu.SemaphoreType.DMA((3,)), pltpu.SemaphoreType.DMA],
        name="scatter_grads",
    )(g)


def _split_between_cores(g):
    n, r, _ = g.shape
    rh = r // 2

    def body(g_ref, own_ref, sib_ref, send_sem, recv_sem, local_sem):
        x, y, c = _place()
        away = pltpu.make_async_remote_copy(
            src_ref=g_ref.at[:, pl.ds((1 - c) * rh, rh), :], dst_ref=sib_ref, send_sem=send_sem, recv_sem=recv_sem,
            device_id=(x, y, 1 - c), device_id_type=MESH,
        )
        away.start()
        keep = pltpu.make_async_copy(g_ref.at[:, pl.ds(c * rh, rh), :], own_ref, local_sem)
        keep.start()
        away.wait()
        keep.wait()

    half = SDS((n, rh, 128), g.dtype)
    return pl.pallas_call(
        body, out_shape=[half, half], in_specs=[ANY], out_specs=[ANY, ANY],
        scratch_shapes=[pltpu.SemaphoreType.DMA, pltpu.SemaphoreType.DMA, pltpu.SemaphoreType.DMA], name="split_cores",
    )(g)


def _join_cores(a):
    rh = a.shape[0]

    def body(a_ref, out_ref, send_sem, recv_sem, local_sem):
        x, y, c = _place()
        mine = out_ref.at[pl.ds(c * rh, rh), :]
        away = pltpu.make_async_remote_copy(
            src_ref=a_ref, dst_ref=mine, send_sem=send_sem, recv_sem=recv_sem, device_id=(x, y, 1 - c), device_id_type=MESH
        )
        away.start()
        keep = pltpu.make_async_copy(a_ref, mine, local_sem)
        keep.start()
        pltpu.make_async_remote_copy(
            src_ref=a_ref, dst_ref=out_ref.at[pl.ds((1 - c) * rh, rh), :], send_sem=send_sem, recv_sem=recv_sem,
            device_id=(x, y, 1 - c), device_id_type=MESH,
        ).wait_recv()
        away.wait_send()
        keep.wait()

    return pl.pallas_call(
        body, out_shape=SDS((2 * rh, 128), a.dtype), in_specs=[ANY], out_specs=ANY,
        scratch_shapes=[pltpu.SemaphoreType.DMA, pltpu.SemaphoreType.DMA, pltpu.SemaphoreType.DMA], name="join_cores",
    )(a)


def _allreduce_small(v):
    r = v.shape[0]

    def body(v_ref, out_ref, buf_ref, send_sems, recv_sems):
        x, y, c = _place()
        buf_ref[0] = v_ref[...]
        copies = []
        for k in range(1, 8):
            peer = (_flip(x, k >> 2 & 1), _flip(y, k >> 1 & 1), _flip(c, k & 1))
            cp = pltpu.make_async_remote_copy(
                src_ref=v_ref, dst_ref=buf_ref.at[k], send_sem=send_sems.at[k - 1], recv_sem=recv_sems.at[k - 1],
                device_id=peer, device_id_type=MESH,
            )
            cp.start()
            copies.append(cp)
        for cp in copies:
            cp.wait_recv()
        acc = None
        for d in range(8):
            slot = 4 * _flip(x, d >> 2 & 1) + 2 * _flip(y, d >> 1 & 1) + _flip(c, d & 1)
            term = buf_ref[slot]
            acc = term if acc is None else acc + term
        out_ref[...] = acc
        for cp in copies:
            cp.wait_send()

    return pl.pallas_call(
        body, out_shape=SDS(v.shape, F32), in_specs=[VMEM], out_specs=VMEM,
        scratch_shapes=[pltpu.VMEM((8, r, 128), F32), pltpu.SemaphoreType.DMA((7,)), pltpu.SemaphoreType.DMA((7,))],
        name="allreduce_small",
    )(v)


_BIG = (
    ("w_in", 1, False), ("decay_w2_fwd", 1, False), ("decay_w2_bwd", 1, False), ("iclr_a2_fwd", 1, False),
    ("iclr_a2_bwd", 1, False), ("gate_g2", 1, False), ("w_uq", 0, False), ("w_ukv", 1, False), ("w_out", 0, False),
    ("w_ffn_up", 1, False), ("ffn_conv_w", 1, True), ("w_ffn_down", 0, False),
)
_SMALL = (
    "ln_mix_g", "shift_mu_prev", "shift_mu_next", "decay_w0_fwd", "decay_w0_bwd", "iclr_a0_fwd", "iclr_a0_bwd", "k_k",
    "k_a", "r_k", "ln_x_g", "ln_x_b", "q_norm_g", "kv_norm_g", "mla_out_g", "ln_ffn_g", "ffn_conv_b", "ln_final_g",
)
_WEIGHTS = (
    "ln_mix_g", "w_in", "shift_mu_prev", "shift_mu_next", "decay_w0_fwd", "decay_w2_fwd", "decay_w0_bwd", "decay_w2_bwd",
    "iclr_a0_fwd", "iclr_a2_fwd", "iclr_a0_bwd", "iclr_a2_bwd", "gate_g2", "k_k", "k_a", "r_k", "ln_x_g", "ln_x_b",
    "q_norm_g", "w_uq", "kv_norm_g", "w_ukv", "mla_out_g", "w_out", "ln_ffn_g", "w_ffn_up", "ffn_conv_w", "ffn_conv_b",
    "w_ffn_down", "ln_final_g",
)


def _pad_rows(flat, rows):
    return jnp.pad(flat, (0, rows * 128 - flat.shape[0])).reshape(rows, 128)


def _rows_for(n, mult):
    rows = -(-n // 128)
    return -(-rows // mult) * mult


def _pack_shards_bf16(arrs):
    parts = []
    for name, _, raw in _BIG:
        w = arrs[name][0]
        flat = lax.bitcast_convert_type(w, BF16).reshape(-1) if raw else w.astype(BF16).reshape(-1)
        parts.append(_pad_rows(flat, _rows_for(flat.shape[0], 32)))
    return jnp.concatenate(parts, axis=0)


def _unpack_gathered(g4, arrs):
    out, off = {}, 0
    for name, axis, raw in _BIG:
        a, b = arrs[name].shape[1:]
        n = a * b * (2 if raw else 1)
        rows = _rows_for(n, 32)
        seg = g4[:, off:off + rows].reshape(4, rows * 128)[:, :n]
        off += rows
        if raw:
            seg = lax.bitcast_convert_type(seg.reshape(4, a * b, 2), F32)
        seg = seg.reshape(4, a, b)
        out[name] = jnp.concatenate([seg[s] for s in range(4)], axis=1) if axis == 1 else seg.reshape(4 * a, b)
    return out


def _pack_grads(full, arrs):
    parts = []
    for name, axis, _ in _BIG:
        a, b = arrs[name].shape[1:]
        g = full[name]
        sh = g.reshape(a, 4, b).transpose(1, 0, 2) if axis == 1 else g.reshape(4, a, b)
        rows = _rows_for(a * b, 8)
        parts.append(jnp.pad(sh.reshape(4, a * b), ((0, 0), (0, rows * 128 - a * b))).reshape(4, rows, 128))
    packed = jnp.concatenate(parts, axis=1)
    total = packed.shape[1]
    return jnp.pad(packed, ((0, 0), (0, -(-total // 1024) * 1024 - total), (0, 0)))


def _unpack_grads(g, arrs):
    out, off = {}, 0
    for name, _, _ in _BIG:
        a, b = arrs[name].shape[1:]
        rows = _rows_for(a * b, 8)
        out[name] = g[off:off + rows].reshape(-1)[:a * b].reshape(a, b)
        off += rows
    return out


def _pack_small(vals):
    flat = jnp.concatenate([vals[n].reshape(-1).astype(F32) for n in _SMALL] + [vals["_loss"].reshape(-1)])
    return _pad_rows(flat, _rows_for(flat.shape[0], 8))


def _unpack_small(buf, arrs):
    flat, out, off = buf.reshape(-1), {}, 0
    for n in _SMALL:
        size = arrs[n].size
        out[n] = flat[off:off + size].reshape(arrs[n].shape)
        off += size
    out["_loss"] = flat[off]
    return out


def _rot_cols(w):
    return jnp.concatenate([-w[..., 16:], w[..., :16]], axis=-1)


def _rot_cols_t(g):
    return jnp.concatenate([g[..., 16:], -g[..., :16]], axis=-1)


def _rope_tables(t):
    inv = jnp.power(ROPE_THETA, -jnp.arange(0, ROPE_DIM, 2, dtype=F32) / ROPE_DIM)
    ang = jnp.arange(t, dtype=F32)[:, None] * inv[None, :]
    one, zero = jnp.ones((t, 64), F32), jnp.zeros((t, 64), F32)
    cs = jnp.concatenate([one, jnp.cos(ang), jnp.cos(ang), zero[:, :32]], axis=1)
    sn = jnp.concatenate([zero, jnp.sin(ang), jnp.sin(ang), zero[:, :32]], axis=1)
    return cs, sn


def _block_diag(a, b):
    za = jnp.zeros_like(a)
    return jnp.concatenate([jnp.concatenate([a, za], axis=1), jnp.concatenate([za, b], axis=1)], axis=0)


def kernel(x, ln_mix_g, w_in, shift_mu_prev, shift_mu_next, decay_w0_fwd, decay_w2_fwd, decay_w0_bwd, decay_w2_bwd, iclr_a0_fwd, iclr_a2_fwd, iclr_a0_bwd, iclr_a2_bwd, gate_g2, k_k, k_a, r_k, ln_x_g, ln_x_b, q_norm_g, w_uq, kv_norm_g, w_ukv, mla_out_g, w_out, ln_ffn_g, w_ffn_up, ffn_conv_w, ffn_conv_b, w_ffn_down, ln_final_g, loss_target, m_ln_mix_g, m_w_in, m_shift_mu_prev, m_shift_mu_next, m_decay_w0_fwd, m_decay_w2_fwd, m_decay_w0_bwd, m_decay_w2_bwd, m_iclr_a0_fwd, m_iclr_a2_fwd, m_iclr_a0_bwd, m_iclr_a2_bwd, m_gate_g2, m_k_k, m_k_a, m_r_k, m_ln_x_g, m_ln_x_b, m_q_norm_g, m_w_uq, m_kv_norm_g, m_w_ukv, m_mla_out_g, m_w_out, m_ln_ffn_g, m_w_ffn_up, m_ffn_conv_w, m_ffn_conv_b, m_w_ffn_down, m_ln_final_g, v_ln_mix_g, v_w_in, v_shift_mu_prev, v_shift_mu_next, v_decay_w0_fwd, v_decay_w2_fwd, v_decay_w0_bwd, v_decay_w2_bwd, v_iclr_a0_fwd, v_iclr_a2_fwd, v_iclr_a0_bwd, v_iclr_a2_bwd, v_gate_g2, v_k_k, v_k_a, v_r_k, v_ln_x_g, v_ln_x_b, v_q_norm_g, v_w_uq, v_kv_norm_g, v_w_ukv, v_mla_out_g, v_w_out, v_ln_ffn_g, v_w_ffn_up, v_ffn_conv_w, v_ffn_conv_b, v_w_ffn_down, v_ln_final_g):
    arrs = dict(locals())
    b, t, d = x.shape
    m = b * t
    x2 = x.reshape(m, d)
    tgt = loss_target.reshape(m, d)
    vec = lambda n: arrs[n].reshape(1, -1)

    gathered = _allgather_weights(_pack_shards_bf16(arrs))
    fw = _unpack_gathered(gathered.reshape(N_CHIPS, -1, 128), arrs)
    win = fw["w_in"]
    zc = jnp.zeros((d, 64), BF16)
    w_kr = win[:, 2944:2976]
    win_m = jnp.concatenate([win[:, 1920:2944], zc, w_kr, zc[:, :32], zc, _rot_cols(w_kr), zc[:, :32]], axis=1)
    win_r = win[:, :RWKV_COLS]
    uq = fw["w_uq"].astype(F32).reshape(Q_RANK, HEADS, 96)
    z32 = jnp.zeros((Q_RANK, HEADS, 32), F32)
    wq = jnp.concatenate([uq[..., :64], uq[..., 64:], z32], axis=-1).reshape(Q_RANK, 1024)
    wqr = jnp.concatenate([z32, z32, _rot_cols(uq[..., 64:]), z32], axis=-1).reshape(Q_RANK, 1024)
    ukv = fw["w_ukv"].astype(F32).reshape(KV_RANK, HEADS, 128)
    wk = jnp.concatenate([ukv[..., :64], jnp.zeros_like(ukv[..., :64])], axis=-1).reshape(KV_RANK, 1024)
    wv = ukv[..., 64:].reshape(KV_RANK, 512)
    head = jnp.arange(512) // HEAD_DIM
    rw = dict(
        w0=jnp.concatenate([vec("decay_w0_fwd"), vec("decay_w0_bwd")], axis=1),
        w2=_block_diag(fw["decay_w2_fwd"], fw["decay_w2_bwd"]).astype(F32),
        a0=jnp.concatenate([vec("iclr_a0_fwd"), vec("iclr_a0_bwd")], axis=1),
        a2=_block_diag(fw["iclr_a2_fwd"], fw["iclr_a2_bwd"]).astype(F32),
        g2=fw["gate_g2"].astype(F32), k_k=vec("k_k"), k_a=vec("k_a"), r_k=vec("r_k"), ln_x_g=vec("ln_x_g"), ln_x_b=vec("ln_x_b"),
        ones_bd=(head[:, None] == head[None, :]).astype(F32),
    )
    mp = dict(q_norm_g=vec("q_norm_g"), kv_norm_g=vec("kv_norm_g"), wq=wq, wqr=wqr, wk=wk, wv=wv)
    cs, sn = _rope_tables(t)
    w_up_g, w_up_v = fw["w_ffn_up"][:, :D_FF], fw["w_ffn_up"][:, D_FF:]
    cw, cb = fw["ffn_conv_w"], vec("ffn_conv_b")

    n1 = _rms_fwd(x2, vec("ln_mix_g"), "rms_mix")
    zm = _mm(n1, win_m, "nn", "proj_in_mla")
    zr = _mm(n1, win_r, "nn", "proj_in_rwkv")
    zs = _shift_fwd(zr.reshape(b, t, RWKV_COLS), vec("shift_mu_prev"), vec("shift_mu_next"))
    zs2 = zs.reshape(m, RWKV_COLS)
    wf, wb, kf, kb, kk, kaf, kab, gate = _prep_fwd(zs2, rw)
    r4 = lambda a: a.reshape(b, t, 512)
    f2 = lambda a: a.reshape(m, 512)
    kk4 = r4(kk)
    ops_f = (r4(wf), r4(kf), r4(kaf))
    ops_b = (r4(wb), r4(kb), r4(kab))
    y_f, hist_f, y_b, hist_b = _scan_fwd(zs, kk4, ops_f, ops_b)
    y_f, y_b = f2(y_f), f2(y_b)
    y_rwkv = _post_fwd(y_f, y_b, zs2, kf, kb, gate, rw)
    q, kfull, v = _mla_fwd(zm, cs, sn, mp, t)
    o, lse = _attn_fwd(q, kfull, v, b, t)
    y_mla = _rms_fwd(o, vec("mla_out_g"), "rms_mla_out")
    ymix = jnp.concatenate([y_rwkv, y_mla], axis=1)
    h1 = _mm(ymix, fw["w_out"], "nn", "proj_out", add=x2)
    n2 = _rms_fwd(h1, vec("ln_ffn_g"), "rms_ffn")
    ug = _mm(n2, w_up_g, "nn", "ffn_up_gate")
    uv = _mm(n2, w_up_v, "nn", "ffn_up_val")
    r3f = lambda a: a.reshape(b, t, D_FF)
    act = _ffn_mid_fwd(r3f(ug), r3f(uv), cw, cb).reshape(m, D_FF)
    h2 = _mm(act, fw["w_ffn_down"], "nn", "ffn_down", add=h1)
    loss_tab, dh2, g_ln_final = _final(h2, vec("ln_final_g"), tgt)

    gfull = {}
    dact = _mm(dh2, fw["w_ffn_down"], "nt", "d_ffn_act")
    gfull["w_ffn_down"] = _mm(act, dh2, "tn", "g_ffn_down")
    dug, duv, tab_g, tab_v = _ffn_mid_bwd(r3f(ug), r3f(uv), cw, cb, r3f(dact))
    dug, duv = dug.reshape(m, D_FF), duv.reshape(m, D_FF)
    gfull["ffn_conv_w"] = jnp.concatenate([tab_g[0:3], tab_v[0:3]], axis=1)
    g_conv_b = jnp.concatenate([tab_g[3:4], tab_v[3:4]], axis=1)
    dn2 = _mm(duv, w_up_v, "nt", "d_ffn_in_val", add=_mm(dug, w_up_g, "nt", "d_ffn_in_gate"))
    gfull["w_ffn_up"] = jnp.concatenate([_mm(n2, dug, "tn", "g_ffn_up_gate"), _mm(n2, duv, "tn", "g_ffn_up_val")], axis=1)
    dh1, g_ln_ffn = _rms_bwd(h1, vec("ln_ffn_g"), dn2, "rms_ffn_bwd", dres=dh2)
    dymix = _mm(dh1, fw["w_out"], "nt", "d_mix")
    gfull["w_out"] = _mm(ymix, dh1, "tn", "g_w_out")
    do, g_mla_out = _rms_bwd(o, vec("mla_out_g"), dymix, "rms_mla_out_bwd", dy_block=1)
    dq, dk, dv = _attn_bwd(q, kfull, v, o, lse, do, b, t)
    dzm, g_qn, g_kvn, g_wq, g_wqr, g_wk, g_wv = _mla_bwd(zm, cs, sn, mp, t, dq, dk, dv)
    gq3, gqr3 = g_wq.reshape(Q_RANK, HEADS, 128), g_wqr.reshape(Q_RANK, HEADS, 128)
    gfull["w_uq"] = jnp.concatenate(
        [gq3[..., :64], gq3[..., 64:96] + _rot_cols_t(gqr3[..., 64:96])], axis=-1
    ).reshape(Q_RANK, HEADS * 96)
    gfull["w_ukv"] = jnp.concatenate(
        [g_wk.reshape(KV_RANK, HEADS, 128)[..., :64], g_wv.reshape(KV_RANK, HEADS, 64)], axis=-1
    ).reshape(KV_RANK, 1024)
    dys, dr_p, dk_p, dv_p, dgate, g_rk, g_lnx_g, g_lnx_b = _post_bwd(y_f, y_b, zs2, kf, kb, gate, rw, dymix)
    (dr_f, dwf, dkf, dkk_f, dkaf, dv_f, dr_b, dwb, dkb, dkk_b, dkab, dv_b) = _scan_bwd(
        zs, kk4, r4(dys), ops_f, hist_f, ops_b, hist_b)
    cts = dict(dwf=f2(dwf), dwb=f2(dwb), dkf=f2(dkf), dkb=f2(dkb), dkk_f=f2(dkk_f), dkk_b=f2(dkk_b), dkaf=f2(dkaf), dkab=f2(dkab),
               dr_f=f2(dr_f), dr_b=f2(dr_b), dr_p=dr_p, dk_p=dk_p, dv_p=dv_p, dg=dgate, dv_f=f2(dv_f), dv_b=f2(dv_b))
    dzs, g_w0, g_w2, g_a0, g_a2, g_g2, g_kk, g_ka = _prep_bwd(zs2, rw, cts)
    dzr, g_mu_p, g_mu_n = _shift_bwd(dzs.reshape(b, t, RWKV_COLS), zr.reshape(b, t, RWKV_COLS), vec("shift_mu_prev"), vec("shift_mu_next"))
    dzr = dzr.reshape(m, RWKV_COLS)
    gfull["decay_w2_fwd"], gfull["decay_w2_bwd"] = g_w2[:64, :512], g_w2[64:, 512:]
    gfull["iclr_a2_fwd"], gfull["iclr_a2_bwd"] = g_a2[:64, :512], g_a2[64:, 512:]
    gfull["gate_g2"] = g_g2
    dn1 = _mm(dzr, win_r, "nt", "d_proj_in_rwkv", add=_mm(dzm, win_m, "nt", "d_proj_in_mla"))
    g_m = _mm(n1, dzm, "tn", "g_w_in_mla")
    g_r = _mm(n1, dzr, "tn", "g_w_in_rwkv")
    g_kr = g_m[:, 1088:1120] + _rot_cols_t(g_m[:, 1216:1248])
    gfull["w_in"] = jnp.concatenate([g_r, g_m[:, :1024], g_kr], axis=1)
    dx, g_ln_mix = _rms_bwd(x2, vec("ln_mix_g"), dn1, "rms_mix_bwd", dres=dh1)

    own, sib = _split_between_cores(_pack_grads(gfull, arrs))
    rh = own.shape[1]
    chip_part = _add_rows([own.reshape(4 * rh, 128), sib.reshape(4 * rh, 128)], "sum_cores", BF16)
    recv = _scatter_to_chips(chip_part.reshape(4, rh, 128))
    g_big = _unpack_grads(_join_cores(_add_rows([recv[3], recv[0], recv[1], recv[2]], "sum_chips")), arrs)
    small = {
        "ln_mix_g": g_ln_mix, "shift_mu_prev": g_mu_p, "shift_mu_next": g_mu_n, "decay_w0_fwd": g_w0[:, :512],
        "decay_w0_bwd": g_w0[:, 512:], "iclr_a0_fwd": g_a0[:, :512], "iclr_a0_bwd": g_a0[:, 512:], "k_k": g_kk, "k_a": g_ka,
        "r_k": g_rk, "ln_x_g": g_lnx_g, "ln_x_b": g_lnx_b, "q_norm_g": g_qn, "kv_norm_g": g_kvn, "mla_out_g": g_mla_out,
        "ln_ffn_g": g_ln_ffn, "ffn_conv_b": g_conv_b, "ln_final_g": g_ln_final,
        "_loss": jnp.pad(loss_tab[0, 0:1], (0, 127)),
    }
    g_small_buf = _allreduce_small(_pack_small(small))
    g_small = _unpack_small(g_small_buf, arrs)

    grads, deltas, new_m, new_v = {}, {}, {}, {}
    for name, _, _ in _BIG:
        shape = arrs[name].shape
        two = lambda a: a.reshape(shape[1:])
        dlt, m2, v2 = _adamw(two(arrs[name]), g_big[name], two(arrs["m_" + name]), two(arrs["v_" + name]), "adamw_" + name)
        grads[name] = g_big[name].reshape(shape)
        deltas[name], new_m[name], new_v[name] = dlt.reshape(shape), m2.reshape(shape), v2.reshape(shape)
    pk = lambda pre: _pack_small({**{n: arrs[pre + n] for n in _SMALL}, "_loss": jnp.zeros((128,), F32)})
    sd, sm, sv = _adamw(pk(""), g_small_buf, pk("m_"), pk("v_"), "adamw_small")
    sd, sm, sv = _unpack_small(sd, arrs), _unpack_small(sm, arrs), _unpack_small(sv, arrs)
    for n in _SMALL:
        grads[n], deltas[n], new_m[n], new_v[n] = g_small[n], sd[n], sm[n], sv[n]

    return (g_small["_loss"], dx.reshape(b, t, d), *[grads[n] for n in _WEIGHTS], *[deltas[n] for n in _WEIGHTS],
            *[new_m[n] for n in _WEIGHTS], *[new_v[n] for n in _WEIGHTS])
```

```python
import functools
import math

import jax
import jax.numpy as jnp
from jax import lax
from jax.experimental import pallas as pl
from jax.experimental.pallas import tpu as pltpu

F32, BF16 = jnp.float32, jnp.bfloat16
HIGHEST = lax.Precision.HIGHEST
MESH = pl.DeviceIdType.MESH
ANY = pl.BlockSpec(memory_space=pl.ANY)
VMEM = pl.BlockSpec(memory_space=pltpu.VMEM)
BS = pl.BlockSpec
SDS = jax.ShapeDtypeStruct

NORM_EPS = 1e-6
GN_EPS = 64e-5
L2_EPS = 1e-12
HEADS = 8
HEAD_DIM = 64
D_RWKV = HEADS * HEAD_DIM
ROPE_DIM = 32
ROPE_THETA = 10000.0
MLA_SCALE = (64 + ROPE_DIM) ** -0.5
Q_RANK, KV_RANK = 768, 256
RWKV_COLS = 1920
MLA_PAD_COLS = Q_RANK + KV_RANK + 256
D_FF = 2816
ADAM_LR, ADAM_B1, ADAM_B2, ADAM_EPS, ADAM_WD, ADAM_STEP = 0.001, 0.9, 0.999, 1e-08, 0.01, 10

V7X_LANES = 128
V7X_VMEM_LIMIT = 48 * 1024 * 1024
SCAN_CHUNK = 16
N_CHIPS = 4


def _cp(*sem):
    return pltpu.CompilerParams(dimension_semantics=sem, vmem_limit_bytes=V7X_VMEM_LIMIT)


def _tile(n, cands=(512, 640, 384, 256, 128)):
    for c in cands:
        if n % c == 0:
            return c
    return n


def _row_tile(n, cap=256):
    best = n
    for t in range(8, cap + 1, 8):
        if n % t == 0:
            best = t
    return best if best <= cap or n <= cap else n


def _rms(x, g):
    ms = jnp.mean(x * x, axis=-1, keepdims=True)
    return x * lax.rsqrt(ms + NORM_EPS) * g


@jax.custom_vjp
def _bdot(x, w):
    return jnp.dot(x.astype(BF16), w.astype(BF16), preferred_element_type=F32)


def _bdot_fwd(x, w):
    return _bdot(x, w), (x, w)


def _bdot_bwd(res, ct):
    x, w = res
    c = ct.astype(BF16)
    dx = lax.dot_general(c, w.astype(BF16), (((1,), (1,)), ((), ())), preferred_element_type=F32)
    dw = lax.dot_general(x.astype(BF16), c, (((0,), (0,)), ((), ())), preferred_element_type=F32)
    return dx.astype(x.dtype), dw.astype(w.dtype)


_bdot.defvjp(_bdot_fwd, _bdot_bwd)


def _headsum(x, ones_bd):
    return jnp.dot(x, ones_bd, precision=HIGHEST, preferred_element_type=F32)


def _prep_fn(zs, w0, w2, a0, a2, g2, k_k, k_a, ones_bd):
    k = zs[:, 512:1024]
    wd = zs[:, 1536:1664]
    ad = zs[:, 1664:1792]
    gd = zs[:, 1792:1920]
    logit = w0 + _bdot(jnp.tanh(wd), w2)
    w = jnp.exp(-math.exp(-0.5) * jax.nn.sigmoid(logit))
    a = jax.nn.sigmoid(a0 + _bdot(ad, a2))
    g = _bdot(jax.nn.sigmoid(gd), g2)
    kkr = k * k_k
    nrm = jnp.sqrt(_headsum(kkr * kkr, ones_bd))
    kk = kkr / jnp.maximum(nrm, L2_EPS)
    a_f, a_b = a[:, :512], a[:, 512:]
    kf = k * (1.0 + (a_f - 1.0) * k_a)
    kb = k * (1.0 + (a_b - 1.0) * k_a)
    return w[:, :512], w[:, 512:], kf, kb, kk, kk * a_f, kk * a_b, g


def _post_fn(y, r, kf, kb, v, g, r_k, ln_g, ln_b, ones_bd):
    mu =_headsum(y, ones_bd) * (1.0 / HEAD_DIM)
    yc = y - mu
    var = _headsum(yc * yc, ones_bd) * (1.0 / HEAD_DIM)
    yn = yc * lax.rsqrt(var + GN_EPS) * ln_g + ln_b
    bonus = _headsum(r * (kf + kb) * r_k, ones_bd) * v
    return (yn + bonus) * g


def _cat8(x):
    return jnp.concatenate([x] * HEADS, axis=1)


def _mla_fn(zm, cs, sn, gq, gkv, wq, wqr, wk, wv):
    cq = zm[:, :Q_RANK]
    ckv = zm[:, Q_RANK:Q_RANK + KV_RANK]
    kr = zm[:, Q_RANK + KV_RANK:Q_RANK + KV_RANK + 128]
    krr = zm[:, Q_RANK + KV_RANK + 128:]
    cqn = _rms(cq, gq)
    ckvn = _rms(ckv, gkv)
    q = _bdot(cqn, wq) * _cat8(cs) + _bdot(cqn, wqr) * _cat8(sn)
    kro = kr * cs + krr * sn
    kfull = _bdot(ckvn, wk) + _cat8(kro)
    v = _bdot(ckvn, wv)
    return q, kfull, v


def _adamw_math(w, g, m, v):
    m2 = ADAM_B1 * m + (1.0 - ADAM_B1) * g
    v2 = ADAM_B2 * v + (1.0 - ADAM_B2) * (g * g)
    m_hat = m2 / (1.0 - ADAM_B1 ** ADAM_STEP)
    v_hat = v2 / (1.0 - ADAM_B2 ** ADAM_STEP)
    delta = -ADAM_LR * (m_hat / (jnp.sqrt(v_hat) + ADAM_EPS) + ADAM_WD * w)
    return delta, m2, v2


_DIMS = {"nn": (((1,), (0,)), ((), ())), "nt": (((1,), (1,)), ((), ())), "tn": (((0,), (0,)), ((), ()))}


def _mm(a, b, mode, name, out_dtype=F32, add=None):
    if mode == "nn":
        (m, k), (_, n) = a.shape, b.shape
    elif mode == "nt":
        (m, k), (n, _) = a.shape, b.shape
    else:
        (k, m), (_, n) = a.shape, b.shape
    big = (1024, 1408, 768, 640, 512, 384, 256, 128)
    tm, tn, tk = _tile(m, big), _tile(n, big), _tile(k, (512, 1408, 640, 384, 256, 128))
    nk = k // tk

    def body(a_ref, b_ref, *rest):
        if add is None:
            o_ref, acc_ref = rest
        else:
            add_ref, o_ref, acc_ref = rest
        kk = pl.program_id(2)

        @pl.when(kk == 0)
        def _():
            acc_ref[...] = jnp.zeros_like(acc_ref)

        acc_ref[...] += lax.dot_general(
            a_ref[...].astype(BF16), b_ref[...].astype(BF16), _DIMS[mode], preferred_element_type=F32
        )

        @pl.when(kk == nk - 1)
        def _():
            r = acc_ref[...]
            if add is not None:
                r = r + add_ref[...]
            o_ref[...] = r.astype(out_dtype)

    a_spec = BS((tk, tm), lambda i, j, kk: (kk, i)) if mode == "tn" else BS((tm, tk), lambda i, j, kk: (i, kk))
    b_spec = BS((tn, tk), lambda i, j, kk: (j, kk)) if mode == "nt" else BS((tk, tn), lambda i, j, kk: (kk, j))
    o_spec = BS((tm, tn), lambda i, j, kk: (i, j))
    ins, specs = [a, b], [a_spec, b_spec]
    if add is not None:
        ins.append(add)
        specs.append(o_spec)
    return pl.pallas_call(
        body, grid=(m // tm, n // tn, nk), in_specs=specs, out_specs=o_spec, out_shape=SDS((m, n), out_dtype),
        scratch_shapes=[pltpu.VMEM((tm, tn), F32)], compiler_params=_cp("parallel", "parallel", "arbitrary"), name=name,
    )(*ins)


def _rms_fwd(x, g, name):
    m, d = x.shape
    tr = _tile(m)

    def body(x_ref, g_ref, o_ref):
        o_ref[...] = _rms(x_ref[...], g_ref[...]).astype(BF16)

    return pl.pallas_call(
        body, grid=(m // tr,), in_specs=[BS((tr, d), lambda i: (i, 0)), BS((1, d), lambda i: (0, 0))],
        out_specs=BS((tr, d), lambda i: (i, 0)), out_shape=SDS((m, d), BF16), compiler_params=_cp("parallel"), name=name,
    )(x, g)


def _rms_bwd(x, g, dy, name, dres=None, dy_block=0):
    m, d = x.shape
    tr = _row_tile(m)

    def body(x_ref, g_ref, dy_ref, *rest):
        if dres is None:
            dx_ref, dg_ref = rest
        else:
            dres_ref, dx_ref, dg_ref = rest
        _, vjp = jax.vjp(_rms, x_ref[...], g_ref[...])
        dx, dg = vjp(dy_ref[...])
        if dres is not None:
            dx = dx + dres_ref[...]
        dx_ref[...] = dx

        @pl.when(pl.program_id(0) == 0)
        def _():
            dg_ref[...] = jnp.zeros_like(dg_ref)

        dg_ref[...] += dg

    row = BS((tr, d), lambda i: (i, 0))
    vec = BS((1, d), lambda i: (0, 0))
    ins, specs = [x, g, dy], [row, vec, BS((tr, d), lambda i: (i, dy_block))]
    if dres is not None:
        ins.append(dres)
        specs.append(row)
    return pl.pallas_call(
        body, grid=(m // tr,), in_specs=specs, out_specs=[row, vec], out_shape=[SDS((m, d), F32), SDS((1, d), F32)],
        compiler_params=_cp("arbitrary"), name=name,
    )(*ins)


def _final(h, g, tgt):
    m, d = h.shape
    tr = _row_tile(m)

    def loss_fn(hh, gg, tt):
        e = _rms(hh, gg) - tt
        return 0.5 * jnp.sum(e * e) * (1.0 / d)

    def body(h_ref, g_ref, t_ref, l_ref, dh_ref, dg_ref):
        val, (dh, dg) = jax.value_and_grad(loss_fn, argnums=(0, 1))(h_ref[...], g_ref[...], t_ref[...])
        dh_ref[...] = dh

        @pl.when(pl.program_id(0) == 0)
        def _():
            dg_ref[...] = jnp.zeros_like(dg_ref)
            l_ref[...] = jnp.zeros_like(l_ref)

        dg_ref[...] += dg
        l_ref[...] += jnp.full(l_ref.shape, val, F32)

    row = BS((tr, d), lambda i: (i, 0))
    vec = BS((1, d), lambda i: (0, 0))
    return pl.pallas_call(
        body, grid=(m // tr,), in_specs=[row, vec, row], out_specs=[BS((8, 128), lambda i: (0, 0)), row, vec],
        out_shape=[SDS((8, 128), F32), SDS((m, d), F32), SDS((1, d), F32)], compiler_params=_cp("arbitrary"), name="final_loss",
    )(h, g, tgt)


def _prev_next(z, t):
    row = lax.broadcasted_iota(jnp.int32, z.shape, 0)
    zp = jnp.where(row == 0, 0.0, pltpu.roll(z, 1, axis=0))
    zn = jnp.where(row == t - 1, 0.0, pltpu.roll(z, t - 1, axis=0))
    return zp, zn


def _shift_fwd(z3, mu_p, mu_n):
    b, t, c = z3.shape
    nc = c // 128

    def body(z_ref, mp_ref, mn_ref, o_ref):
        z = z_ref[0]
        zp, zn = _prev_next(z, t)
        o_ref[0] = z + mp_ref[...] * (zp - z) + mn_ref[...] * (zn - z)

    blk = BS((1, t, 128), lambda i, j: (i, 0, j))
    vec = BS((1, 128), lambda i, j: (0, j))
    return pl.pallas_call(
        body, grid=(b, nc), in_specs=[blk, vec, vec], out_specs=blk, out_shape=SDS((b, t, c), F32),
        compiler_params=_cp("parallel", "parallel"), name="shift_fwd",
    )(z3, mu_p, mu_n)


def _shift_bwd(dzs3, z3, mu_p, mu_n):
    b, t, c = z3.shape
    nc = c // 128

    def body(d_ref, z_ref, mp_ref, mn_ref, dz_ref, dmp_ref, dmn_ref):
        d, z = d_ref[0], z_ref[0]
        mp, mn = mp_ref[...], mn_ref[...]
        zp, zn = _prev_next(z, t)
        _, dp_next = _prev_next(d * mp, t)
        dn_prev, _ = _prev_next(d * mn, t)
        dz_ref[0] = d * (1.0 - mp - mn) + dp_next + dn_prev

        @pl.when(pl.program_id(1) == 0)
        def _():
            dmp_ref[...] = jnp.zeros_like(dmp_ref)
            dmn_ref[...] = jnp.zeros_like(dmn_ref)

        dmp_ref[...] += jnp.sum(d * (zp - z), axis=0, keepdims=True)
        dmn_ref[...] += jnp.sum(d * (zn - z), axis=0, keepdims=True)

    blk = BS((1, t, 128), lambda j, i: (i, 0, j))
    vec = BS((1, 128), lambda j, i: (0, j))
    return pl.pallas_call(
        body, grid=(nc, b), in_specs=[blk, blk, vec, vec], out_specs=[blk, vec, vec],
        out_shape=[SDS((b, t, c), F32), SDS((1, c), F32), SDS((1, c), F32)],
        compiler_params=_cp("parallel", "arbitrary"), name="shift_bwd",
    )(dzs3, z3, mu_p, mu_n)


def _const(shape):
    nd = len(shape)
    return BS(shape, lambda i: (0,) * nd)


def _prep_fwd(zs, p):
    m = zs.shape[0]
    tr = 256
    params = [p["w0"], p["w2"], p["a0"], p["a2"], p["g2"], p["k_k"], p["k_a"], p["ones_bd"]]

    def body(zs_ref, w0, w2, a0, a2, g2, kk_, ka_, bd, wf, wb, kf, kb, kk, kaf, kab, g):
        outs = _prep_fn(zs_ref[...], w0[...], w2[...], a0[...], a2[...], g2[...], kk_[...], ka_[...], bd[...])
        for ref, val in zip((wf, wb, kf, kb, kk, kaf, kab, g), outs):
            ref[...] = val

    row = BS((tr, 512), lambda i: (i, 0))
    return pl.pallas_call(
        body, grid=(m // tr,), in_specs=[BS((tr, RWKV_COLS), lambda i: (i, 0))] + [_const(q.shape) for q in params],
        out_specs=[row] * 8, out_shape=[SDS((m, 512), F32)] * 8, compiler_params=_cp("parallel"), name="rwkv_prep_fwd",
    )(zs, *params)


def _prep_bwd(zs, p, ct_rows):
    m = zs.shape[0]
    tr = 128
    params = [p["w0"], p["w2"], p["a0"], p["a2"], p["g2"], p["k_k"], p["k_a"]]
    names = ["dwf", "dwb", "dkf", "dkb", "dkk_f", "dkk_b", "dkaf", "dkab", "dr_f", "dr_b", "dr_p", "dk_p", "dv_p", "dg",
             "dv_f", "dv_b"]
    rows = [ct_rows[n] for n in names]

    def body(zs_ref, w0, w2, a0, a2, g2, kk_, ka_, bd, *rest):
        c = {n: r[...] for n, r in zip(names, rest[:len(names)])}
        outs = rest[len(names):]
        dzs_ref, grads = outs[0], outs[1:]
        ones_bd = bd[...]
        _, vjp = jax.vjp(
            lambda *q: _prep_fn(*q, ones_bd), zs_ref[...], w0[...], w2[...], a0[...], a2[...], g2[...], kk_[...], ka_[...]
        )
        cts = (c["dwf"], c["dwb"], c["dkf"] + c["dk_p"], c["dkb"] + c["dk_p"], c["dkk_f"] + c["dkk_b"], c["dkaf"], c["dkab"], c["dg"])
        dzs, *dparams = vjp(cts)
        dr = c["dr_f"] + c["dr_b"] + c["dr_p"]
        dv = c["dv_f"] + c["dv_b"] + c["dv_p"]
        dzs_ref[:, 0:512] = dzs[:, 0:512] + dr
        dzs_ref[:, 512:1024] = dzs[:, 512:1024]
        dzs_ref[:, 1024:1536] = dzs[:, 1024:1536] + dv
        dzs_ref[:, 1536:1920] = dzs[:, 1536:1920]

        @pl.when(pl.program_id(0) == 0)
        def _():
            for gr in grads:
                gr[...] = jnp.zeros_like(gr)

        for gr, val in zip(grads, dparams):
            gr[...] += val

    row = BS((tr, 512), lambda i: (i, 0))
    return pl.pallas_call(
        body, grid=(m // tr,),
        in_specs=[BS((tr, RWKV_COLS), lambda i: (i, 0))] + [_const(q.shape) for q in params] + [_const(p["ones_bd"].shape)]
        + [row] * len(names),
        out_specs=[BS((tr, RWKV_COLS), lambda i: (i, 0))] + [_const(q.shape) for q in params],
        out_shape=[SDS((m, RWKV_COLS), F32)] + [SDS(q.shape, F32) for q in params],
        compiler_params=_cp("arbitrary"), name="rwkv_prep_bwd",
    )(zs, *params, p["ones_bd"], *rows)


def _post_specs(tr):
    r = BS((tr, 512), lambda i: (i, 0))
    v = BS((tr, 512), lambda i: (i, 2))
    row = BS((tr, 512), lambda i: (i, 0))
    return r, v, row


def _post_fwd(y_f, y_b, zs, kf, kb, g, p):
    m = zs.shape[0]
    tr = 256
    r, v, row = _post_specs(tr)
    vecs = [p["r_k"], p["ln_x_g"], p["ln_x_b"], p["ones_bd"]]

    def body(yf, yb, r_ref, v_ref, kf_ref, kb_ref, g_ref, rk, lg, lb, bd, o_ref):
        o_ref[...] = _post_fn(
            yf[...] + yb[...], r_ref[...], kf_ref[...], kb_ref[...], v_ref[...], g_ref[...], rk[...], lg[...], lb[...], bd[...]
        ).astype(BF16)

    return pl.pallas_call(
        body, grid=(m // tr,), in_specs=[row, row, r, v, row, row, row] + [_const(q.shape) for q in vecs],
        out_specs=row, out_shape=SDS((m, 512), BF16), compiler_params=_cp("parallel"), name="rwkv_post_fwd",
    )(y_f, y_b, zs, zs, kf, kb, g, *vecs)


def _post_bwd(y_f, y_b, zs, kf, kb, g, p, dymix):
    m = zs.shape[0]
    tr = 128
    r, v, row = _post_specs(tr)
    vecs = [p["r_k"], p["ln_x_g"], p["ln_x_b"]]

    def body(yf, yb, r_ref, v_ref, kf_ref, kb_ref, g_ref, rk, lg, lb, bd, dy_ref, dyo, dr, dk, dv, dg, drk, dlg, dlb):
        ones_bd = bd[...]
        _, vjp = jax.vjp(
            lambda *q: _post_fn(*q, ones_bd),
            yf[...] + yb[...], r_ref[...], kf_ref[...], kb_ref[...], v_ref[...], g_ref[...], rk[...], lg[...], lb[...],
        )
        c_y, c_r, c_kf, _, c_v, c_g, c_rk, c_lg, c_lb = vjp(dy_ref[...])
        dyo[...] = c_y
        dr[...] = c_r
        dk[...] = c_kf
        dv[...] = c_v
        dg[...] = c_g

        @pl.when(pl.program_id(0) == 0)
        def _():
            for ref in (drk, dlg, dlb):
                ref[...] = jnp.zeros_like(ref)

        drk[...] += c_rk
        dlg[...] += c_lg
        dlb[...] += c_lb

    vec = _const((1, 512))
    return pl.pallas_call(
        body, grid=(m // tr,),
        in_specs=[row, row, r, v, row, row, row] + [_const(q.shape) for q in vecs] + [_const(p["ones_bd"].shape), row],
        out_specs=[row, row, row, row, row, vec, vec, vec],
        out_shape=[SDS((m, 512), F32)] * 5 + [SDS((1, 512), F32)] * 3,
        compiler_params=_cp("arbitrary"), name="rwkv_post_bwd",
    )(y_f, y_b, zs, zs, kf, kb, g, *vecs, p["ones_bd"], dymix)


def _half_ones():
    ri = lax.broadcasted_iota(jnp.int32, (256, 128), 0) & 127
    ci = lax.broadcasted_iota(jnp.int32, (256, 128), 1)
    return jnp.where((ri < 64) == (ci < 64), 1.0, 0.0).astype(BF16)


def _half_sums(xs, ones, exact):
    out = [None] * len(xs)
    for two in (True, False):
        idx = [i for i, e in enumerate(exact) if bool(e) == two]
        if not idx:
            continue
        x = jnp.concatenate([xs[i] for i in idx], axis=0)
        hi = x.astype(BF16)
        if two:
            mid = (x - hi.astype(F32)).astype(BF16)
            res = jnp.dot(jnp.concatenate([hi, mid], axis=1), ones, preferred_element_type=F32)
        else:
            res = jnp.dot(hi, ones[:128], preferred_element_type=F32)
        for j, i in enumerate(idx):
            out[i] = res[64 * j:64 * j + 64]
    return out


def _scan_specs(b, t):
    nc = t // SCAN_CHUNK
    up, down = (lambda c: c), (lambda c: nc - 1 - c)
    rows = [BS((b, SCAN_CHUNK, 512), lambda c, ci=ci: (0, ci(c), 0)) for ci in (up, down)]
    vrows = [BS((b, SCAN_CHUNK, 512), lambda c, ci=ci: (0, ci(c), 2)) for ci in (up, down)]
    hist = [BS((SCAN_CHUNK, b * 4, 64, 128), lambda c, ci=ci: (ci(c), 0, 0, 0)) for ci in (up, down)]
    return nc, rows, vrows, hist


class _Window:
    def __init__(self, g, ascending):
        self.bases = [pl.multiple_of(g * 8, 8) if asc else pl.multiple_of(SCAN_CHUNK - 8 - g * 8, 8) for asc in ascending]
        self.ascending = ascending
        self.blocks = {}
        self.row_id = lax.broadcasted_iota(jnp.int32, (8, 128), 0)

    def j(self, d, s):
        return s if self.ascending[d] else 7 - s

    def time(self, d, s):
        return self.bases[d] + self.j(d, s)

    def row(self, ref, d, bi, cols, s):
        key = (id(ref), d, bi, cols.start)
        if key not in self.blocks:
            self.blocks[key] = ref[bi, pl.ds(self.bases[d], 8), cols]
        jj = self.j(d, s)
        return self.blocks[key][jj:jj + 1, :]

    def put(self, buf, key, d, s, row):
        prev = buf.get(key)
        new = jnp.broadcast_to(row, (8, 128))
        buf[key] = new if prev is None else jnp.where(self.row_id == self.j(d, s), new, prev)

    def flush(self, buf, refs_of):
        for key, val in buf.items():
            ref, d, bi, cols = refs_of(key)
            ref[bi, pl.ds(self.bases[d], 8), cols] = val


def _pairs(b):
    return [(bi * 4 + p, bi, slice(128 * p, 128 * p + 128)) for bi in range(b) for p in range(4)]


def _colsum(x):
    return jnp.sum(x, axis=0, keepdims=True)


def _eye_mask():
    return (lax.broadcasted_iota(jnp.int32, (64, 128), 1) & 63) == lax.broadcasted_iota(jnp.int32, (64, 128), 0)


def _scan_fwd(zs, kk, ops_f, ops_b):
    b, t = zs.shape[:2]
    nc, rows, vrows, hist = _scan_specs(b, t)
    npair = b * 4

    def body(*refs):
        ins, outs, s_ref = refs[:12], refs[12:16], refs[16]
        dirs = [dict(zip(("r", "kk", "v", "w", "k", "ka", "y", "h"), (*ins[6 * d:6 * d + 6], *outs[2 * d:2 * d + 2])))
                for d in (0, 1)]

        @pl.when(pl.program_id(0) == 0)
        def _():
            s_ref[...] = jnp.zeros_like(s_ref)

        ones, eye = _half_ones(), _eye_mask()
        chains = [(d, pr, bi, cols) for d in (0, 1) for pr, bi, cols in _pairs(b)]

        def eight_steps(g, carry):
            win = _Window(g, (True, False))
            ybuf = {}
            for s in range(8):
                s_prev, xa = [], []
                for d, pr, bi, cols in chains:
                    q = dirs[d]
                    st = s_ref[d * npair + pr]
                    q["h"][win.time(d, s), pr] = st
                    s_prev.append(st)
                    xa += [st * win.row(q["kk"], d, bi, cols, s), jnp.where(eye, win.row(q["v"], d, bi, cols, s), 0.0)]
                ra = _half_sums(xa, ones, (True, False) * len(chains))
                xb = []
                for i, (d, pr, bi, cols) in enumerate(chains):
                    q = dirs[d]
                    s_new = s_prev[i] * win.row(q["w"], d, bi, cols, s) - ra[2 * i] * win.row(q["ka"], d, bi, cols, s) \
                        + ra[2 * i + 1] * win.row(q["k"], d, bi, cols, s)
                    s_ref[d * npair + pr] = s_new
                    xb.append(s_new * win.row(q["r"], d, bi, cols, s))
                rb = _half_sums(xb, ones, (False,) * len(chains))
                for i, (d, pr, bi, cols) in enumerate(chains):
                    win.put(ybuf, i, d, s, _colsum(jnp.where(eye, rb[i], 0.0)))
            win.flush(ybuf, lambda i: (dirs[chains[i][0]]["y"], chains[i][0], chains[i][2], chains[i][3]))
            return carry

        lax.fori_loop(0, SCAN_CHUNK // 8, eight_steps, 0)

    row_shape, hist_shape = SDS((b, t, 512), F32), SDS((t, npair, 64, 128), F32)
    return pl.pallas_call(
        body, grid=(nc,), in_specs=sum(([rows[d], rows[d], vrows[d]] + [rows[d]] * 3 for d in (0, 1)), []),
        out_specs=[rows[0], hist[0], rows[1], hist[1]], out_shape=[row_shape, hist_shape, row_shape, hist_shape],
        scratch_shapes=[pltpu.VMEM((2 * npair, 64, 128), F32)], compiler_params=_cp("arbitrary"), name="wkv_scan",
    )(zs, kk, zs, *ops_f, zs, kk, zs, *ops_b)


def _scan_bwd(zs, kk, dy, ops_f, hist_f, ops_b, hist_b):
    b, t = zs.shape[:2]
    nc, rows, vrows, hist = _scan_specs(b, t)
    npair = b * 4
    names_in = ("r", "kk", "v", "dy", "w", "k", "ka", "h")
    names_out = ("dr", "dw", "dk", "dkk", "dka", "dv")

    def body(*refs):
        ins, outs, ds_ref = refs[:16], refs[16:28], refs[28]
        dirs = [dict(zip(names_in + names_out, (*ins[8 * d:8 * d + 8], *outs[6 * d:6 * d + 6]))) for d in (0, 1)]

        @pl.when(pl.program_id(0) == 0)
        def _():
            ds_ref[...] = jnp.zeros_like(ds_ref)

        ones, eye = _half_ones(), _eye_mask()
        chains = [(d, pr, bi, cols) for d in (0, 1) for pr, bi, cols in _pairs(b)]

        def eight_steps(g, carry):
            win = _Window(g, (False, True))
            obuf = {}
            for s in range(8):
                row = lambda name, d, bi, cols: win.row(dirs[d][name], d, bi, cols, s)
                s_prev, xa = [], []
                for d, pr, bi, cols in chains:
                    st = dirs[d]["h"][win.time(d, s), pr]
                    s_prev.append(st)
                    xa += [st * row("kk", d, bi, cols), jnp.where(eye, row("v", d, bi, cols), 0.0),
                           jnp.where(eye, row("dy", d, bi, cols), 0.0)]
                ra = _half_sums(xa, ones, (True, False, False) * len(chains))
                ds_now, xb = [], []
                for i, (d, pr, bi, cols) in enumerate(chains):
                    skk, vcol, dycol = ra[3 * i], ra[3 * i + 1], ra[3 * i + 2]
                    ka_r, k_r = row("ka", d, bi, cols), row("k", d, bi, cols)
                    s_new = s_prev[i] * row("w", d, bi, cols) - skk * ka_r + vcol * k_r
                    ds = ds_ref[d * npair + pr] + dycol * row("r", d, bi, cols)
                    win.put(obuf, (i, "dr"), d, s, _colsum(s_new * dycol))
                    win.put(obuf, (i, "dk"), d, s, _colsum(ds * vcol))
                    win.put(obuf, (i, "dka"), d, s, -_colsum(ds * skk))
                    win.put(obuf, (i, "dw"), d, s, _colsum(ds * s_prev[i]))
                    ds_now.append(ds)
                    xb += [ds * k_r, ds * ka_r]
                rb = _half_sums(xb, ones, (False, True) * len(chains))
                for i, (d, pr, bi, cols) in enumerate(chains):
                    dskk = -rb[2 * i + 1]
                    win.put(obuf, (i, "dv"), d, s, _colsum(jnp.where(eye, rb[2 * i], 0.0)))
                    win.put(obuf, (i, "dkk"), d, s, _colsum(s_prev[i] * dskk))
                    ds_ref[d * npair + pr] = ds_now[i] * row("w", d, bi, cols) + dskk * row("kk", d, bi, cols)
            win.flush(obuf, lambda key: (dirs[chains[key[0]][0]][key[1]], chains[key[0]][0], chains[key[0]][2], chains[key[0]][3]))
            return carry

        lax.fori_loop(0, SCAN_CHUNK // 8, eight_steps, 0)

    row_shape = SDS((b, t, 512), F32)
    return pl.pallas_call(
        body, grid=(nc,), in_specs=sum(([rows[d], rows[d], vrows[d]] + [rows[d]] * 4 + [hist[d]] for d in (1, 0)), []),
        out_specs=[rows[1]] * 6 + [rows[0]] * 6, out_shape=[row_shape] * 12,
        scratch_shapes=[pltpu.VMEM((2 * npair, 64, 128), F32)], compiler_params=_cp("arbitrary"), name="wkv_scan_bwd",
    )(zs, kk, zs, dy, *ops_f, hist_f, zs, kk, zs, dy, *ops_b, hist_b)


def _mla_fwd(zm, cs, sn, p, t):
    m = zm.shape[0]
    tr = 256
    per = t // tr
    params = [p["q_norm_g"], p["kv_norm_g"], p["wq"], p["wqr"], p["wk"], p["wv"]]

    def body(z_ref, cs_ref, sn_ref, gq, gkv, wq, wqr, wk, wv, q_ref, k_ref, v_ref):
        q, kf, v = _mla_fn(z_ref[...], cs_ref[...], sn_ref[...], gq[...], gkv[...], wq[...], wqr[...], wk[...], wv[...])
        q_ref[...] = q.astype(BF16)
        k_ref[...] = kf.astype(BF16)
        v_ref[...] = v.astype(BF16)

    tab = BS((tr, 128), lambda i: (i % per, 0))
    return pl.pallas_call(
        body, grid=(m // tr,), in_specs=[BS((tr, MLA_PAD_COLS), lambda i: (i, 0)), tab, tab] + [_const(q.shape) for q in params],
        out_specs=[BS((tr, 1024), lambda i: (i, 0)), BS((tr, 1024), lambda i: (i, 0)), BS((tr, 512), lambda i: (i, 0))],
        out_shape=[SDS((m, 1024), BF16), SDS((m, 1024), BF16), SDS((m, 512), BF16)], compiler_params=_cp("parallel"), name="mla_prep_fwd",
    )(zm, cs, sn, *params)


def _mla_bwd(zm, cs, sn, p, t, dq, dk, dv):
    m = zm.shape[0]
    tr = 128
    per = t // tr
    params = [p["q_norm_g"], p["kv_norm_g"], p["wq"], p["wqr"], p["wk"], p["wv"]]

    def body(z_ref, cs_ref, sn_ref, gq, gkv, wq, wqr, wk, wv, dq_ref, dk_ref, dv_ref, dz_ref, *grads):
        cs_v, sn_v = cs_ref[...], sn_ref[...]
        _, vjp = jax.vjp(
            lambda *q: _mla_fn(q[0], cs_v, sn_v, *q[1:]), z_ref[...], gq[...], gkv[...], wq[...], wqr[...], wk[...], wv[...]
        )
        dz, *dparams = vjp((dq_ref[...], dk_ref[...], dv_ref[...]))
        dz_ref[...] = dz

        @pl.when(pl.program_id(0) == 0)
        def _():
            for gr in grads:
                gr[...] = jnp.zeros_like(gr)

        for gr, val in zip(grads, dparams):
            gr[...] += val

    tab = BS((tr, 128), lambda i: (i % per, 0))
    wide = BS((tr, 1024), lambda i: (i, 0))
    return pl.pallas_call(
        body, grid=(m // tr,),
        in_specs=[BS((tr, MLA_PAD_COLS), lambda i: (i, 0)), tab, tab] + [_const(q.shape) for q in params]
        + [wide, wide, BS((tr, 512), lambda i: (i, 0))],
        out_specs=[BS((tr, MLA_PAD_COLS), lambda i: (i, 0))] + [_const(q.shape) for q in params],
        out_shape=[SDS((m, MLA_PAD_COLS), F32)] + [SDS(q.shape, F32) for q in params],
        compiler_params=_cp("arbitrary"), name="mla_prep_bwd",
    )(zm, cs, sn, *params, dq, dk, dv)


_NT = (((1,), (1,)), ((), ()))
_TN = (((0,), (0,)), ((), ()))


def _attn_fwd(q, kf, v, b, t):
    m = q.shape[0]
    tq = 256
    nq = t // tq

    def body(q_ref, k_ref, v_ref, o_ref, l_ref):
        lo = lax.broadcasted_iota(jnp.int32, (1, 128), 1) < 64
        v_all = v_ref[...]
        o = jnp.zeros((tq, 128), F32)
        lse = []
        for h in range(2):
            hs = slice(128 * h, 128 * h + 128)
            s = lax.dot_general(q_ref[:, hs], k_ref[:, hs], _NT, preferred_element_type=F32) * MLA_SCALE
            mx = jnp.max(s, axis=1, keepdims=True)
            e = jnp.exp(s - mx)
            den = jnp.sum(e, axis=1, keepdims=True)
            vh = jnp.where(lo if h == 0 else jnp.logical_not(lo), v_all, jnp.zeros_like(v_all))
            o = o + jnp.dot(e.astype(BF16), vh, preferred_element_type=F32) / den
            lse.append(mx + jnp.log(den))
        o_ref[...] = o
        l_ref[...] = jnp.where(lo, lse[0], lse[1])

    return pl.pallas_call(
        body, grid=(b, 4, nq),
        in_specs=[BS((tq, 256), lambda bi, hp, i: (bi * nq + i, hp)), BS((t, 256), lambda bi, hp, i: (bi, hp)),
                  BS((t, 128), lambda bi, hp, i: (bi, hp))],
        out_specs=[BS((tq, 128), lambda bi, hp, i: (bi * nq + i, hp))] * 2,
        out_shape=[SDS((m, 512), F32), SDS((m, 512), F32)], compiler_params=_cp("parallel", "parallel", "arbitrary"), name="attn_fwd",
    )(q, kf, v)


def _attn_bwd(q, kf, v, o, lse, do, b, t):
    m = q.shape[0]
    tq = 256
    nq = t // tq

    def body(q_ref, k_ref, v_ref, o_ref, l_ref, do_ref, dq_ref, dk_ref, dv_ref):
        lo = lax.broadcasted_iota(jnp.int32, (1, 128), 1) < 64

        @pl.when(pl.program_id(2) == 0)
        def _():
            dk_ref[...] = jnp.zeros_like(dk_ref)
            dv_ref[...] = jnp.zeros_like(dv_ref)

        v_all, o_all, l_all, do_all = v_ref[...], o_ref[...], l_ref[...], do_ref[...]
        dv_acc = jnp.zeros((t, 128), F32)
        for h in range(2):
            hs = slice(128 * h, 128 * h + 128)
            mask = lo if h == 0 else jnp.logical_not(lo)
            qh, kh = q_ref[:, hs], k_ref[:, hs]
            s = lax.dot_general(qh, kh, _NT, preferred_element_type=F32) * MLA_SCALE
            lse_h = jnp.max(jnp.where(mask, l_all, -jnp.inf), axis=1, keepdims=True)
            pr = jnp.exp(s - lse_h)
            do_h = jnp.where(mask, do_all, 0.0)
            dp = lax.dot_general(do_h.astype(BF16), v_all, _NT, preferred_element_type=F32)
            dsum = jnp.sum(do_h * o_all, axis=1, keepdims=True)
            ds = (pr * (dp - dsum) * MLA_SCALE).astype(BF16)
            dq_ref[:, hs] = jnp.dot(ds, kh, preferred_element_type=F32)
            dk_ref[:, hs] += lax.dot_general(ds, qh, _TN, preferred_element_type=F32)
            dv_acc = dv_acc + lax.dot_general(pr.astype(BF16), do_h.astype(BF16), _TN, preferred_element_type=F32)
        dv_ref[...] += dv_acc

    qspec = BS((tq, 256), lambda bi, hp, i: (bi * nq + i, hp))
    kspec = BS((t, 256), lambda bi, hp, i: (bi, hp))
    vspec = BS((t, 128), lambda bi, hp, i: (bi, hp))
    ospec = BS((tq, 128), lambda bi, hp, i: (bi * nq + i, hp))
    return pl.pallas_call(
        body, grid=(b, 4, nq), in_specs=[qspec, kspec, vspec, ospec, ospec, ospec], out_specs=[qspec, kspec, vspec],
        out_shape=[SDS((m, 1024), F32), SDS((m, 1024), F32), SDS((m, 512), F32)],
        compiler_params=_cp("parallel", "parallel", "arbitrary"), name="attn_bwd",
    )(q, kf, v, o, lse, do)


def _conv3(u, w_ref, b_ref, t):
    up, un = _prev_next(u, t)
    return w_ref[0:1, :] * up + w_ref[1:2, :] * u + w_ref[2:3, :] * un + b_ref[...], up, un


def _ffn_mid_fwd(ug3, uv3, cw, cb):
    b, t, f = ug3.shape
    nc = f // 256

    def body(ug_ref, uv_ref, wg_ref, wv_ref, bg_ref, bv_ref, a_ref):
        gc, _, _ = _conv3(ug_ref[0], wg_ref, bg_ref, t)
        vc, _, _ = _conv3(uv_ref[0], wv_ref, bv_ref, t)
        a_ref[0] = (gc * jax.nn.sigmoid(gc) * vc).astype(BF16)

    blk = BS((1, t, 256), lambda i, j: (i, 0, j))
    return pl.pallas_call(
        body, grid=(b, nc),
        in_specs=[blk, blk, BS((3, 256), lambda i, j: (0, j)), BS((3, 256), lambda i, j: (0, j + nc)),
                  BS((1, 256), lambda i, j: (0, j)), BS((1, 256), lambda i, j: (0, j + nc))],
        out_specs=blk, out_shape=SDS((b, t, f), BF16), compiler_params=_cp("parallel", "parallel"), name="ffn_mid_fwd",
    )(ug3, uv3, cw, cw, cb, cb)


def _ffn_mid_bwd(ug3, uv3, cw, cb, da3):
    b, t, f = ug3.shape
    nc = f // 256

    def half(u, up, un, dc, w_ref):
        dprev, dnext = _prev_next(dc, t)
        du = w_ref[1:2, :] * dc + w_ref[0:1, :] * dnext + w_ref[2:3, :] * dprev
        sums = [jnp.sum(dc * q, axis=0, keepdims=True) for q in (up, u, un)] + [jnp.sum(dc, axis=0, keepdims=True)]
        row = lax.broadcasted_iota(jnp.int32, (8, 256), 0)
        tab = jnp.zeros((8, 256), F32)
        for i, s in enumerate(sums):
            tab = jnp.where(row == i, s, tab)
        return du, tab

    def body(ug_ref, uv_ref, wg_ref, wv_ref, bg_ref, bv_ref, da_ref, dug_ref, duv_ref, tg_ref, tv_ref):
        ug, uv, da = ug_ref[0], uv_ref[0], da_ref[0]
        gc, gp, gn = _conv3(ug, wg_ref, bg_ref, t)
        vc, vp, vn = _conv3(uv, wv_ref, bv_ref, t)
        sg = jax.nn.sigmoid(gc)
        d_gc = da * vc * (sg * (1.0 + gc * (1.0 - sg)))
        d_vc = da * (gc * sg)
        dug, tg = half(ug, gp, gn, d_gc, wg_ref)
        duv, tv = half(uv, vp, vn, d_vc, wv_ref)
        dug_ref[0] = dug
        duv_ref[0] = duv

        @pl.when(pl.program_id(1) == 0)
        def _():
            tg_ref[...] = jnp.zeros_like(tg_ref)
            tv_ref[...] = jnp.zeros_like(tv_ref)

        tg_ref[...] += tg
        tv_ref[...] += tv

    blk = BS((1, t, 256), lambda j, i: (i, 0, j))
    tab = BS((8, 256), lambda j, i: (0, j))
    return pl.pallas_call(
        body, grid=(nc, b),
        in_specs=[blk, blk, BS((3, 256), lambda j, i: (0, j)), BS((3, 256), lambda j, i: (0, j + nc)),
                  BS((1, 256), lambda j, i: (0, j)), BS((1, 256), lambda j, i: (0, j + nc)), blk],
        out_specs=[blk, blk, tab, tab],
        out_shape=[SDS((b, t, f), F32), SDS((b, t, f), F32), SDS((8, f), F32), SDS((8, f), F32)],
        compiler_params=_cp("parallel", "arbitrary"), name="ffn_mid_bwd",
    )(ug3, uv3, cw, cw, cb, cb, da3)


def _add_rows(parts, name, out_dtype=F32):
    r = parts[0].shape[0]
    tr = _row_tile(r, 1024)
    n = len(parts)

    def body(*refs):
        acc = refs[0][...].astype(F32)
        for q in refs[1:n]:
            acc = acc + q[...].astype(F32)
        refs[n][...] = acc.astype(out_dtype)

    row = BS((tr, 128), lambda i: (i, 0))
    return pl.pallas_call(
        body, grid=(r // tr,), in_specs=[row] * n, out_specs=row, out_shape=SDS((r, 128), out_dtype),
        compiler_params=_cp("parallel"), name=name,
    )(*parts)


def _adamw(w, g, m, v, name):
    r, c = w.shape
    tr = _row_tile(r)

    def body(w_ref, g_ref, m_ref, v_ref, d_ref, m2_ref, v2_ref):
        d, m2, v2 = _adamw_math(w_ref[...], g_ref[...], m_ref[...], v_ref[...])
        d_ref[...] = d
        m2_ref[...] = m2
        v2_ref[...] = v2

    blk = BS((tr, c), lambda i: (i, 0))
    return pl.pallas_call(
        body, grid=(r // tr,), in_specs=[blk] * 4, out_specs=[blk] * 3, out_shape=[SDS((r, c), F32)] * 3,
        compiler_params=_cp("parallel"), name=name,
    )(w, g, m, v)


def _place():
    return lax.axis_index("x"), lax.axis_index("y"), lax.axis_index("c")


def _flip(v, bit):
    return 1 - v if bit else v


def _allgather_weights(shard):
    r = shard.shape[0]
    rh = r // 2

    def body(x_ref, out_ref, send_sems, recv_sems):
        x, y, c = _place()
        me, sibling = (x, y, c), (x, y, 1 - c)
        chips = [(1 - x, y), (x, 1 - y), (1 - x, 1 - y)]
        mine_src = x_ref.at[pl.ds(c * rh, rh), :]

        def rows(px, py, pc):
            return out_ref.at[pl.ds((4 * px + 2 * py + pc) * rh, rh), :]

        def copy(k, block, to, src=None):
            return pltpu.make_async_remote_copy(
                src_ref=rows(*block) if src is None else src, dst_ref=rows(*block), send_sem=send_sems.at[k],
                recv_sem=recv_sems.at[k], device_id=to, device_id_type=MESH,
            )

        first = [copy(j, me, (*chip, c), src=mine_src) for j, chip in enumerate(chips)]
        for cp in first:
            cp.start()
        passed = [copy(3 + j, (*chip, c), sibling) for j, chip in enumerate(chips)]
        for j, chip in enumerate(chips):
            copy(j, (*chip, c), me).wait_recv()
            passed[j].start()
        for j, chip in enumerate(chips):
            copy(3 + j, (*chip, 1 - c), me).wait_recv()
        for cp in first + passed:
            cp.wait_send()

    return pl.pallas_call(
        body, out_shape=SDS((8 * rh, 128), shard.dtype), in_specs=[ANY], out_specs=ANY,
        scratch_shapes=[pltpu.SemaphoreType.DMA((6,)), pltpu.SemaphoreType.DMA((6,))], name="allgather_weights",
    )(shard)


def _scatter_to_chips(g):
    def body(g_ref, recv_ref, send_sems, recv_sems):
        x, y, c = _place()
        copies = []
        for j, (fx, fy) in enumerate(((1, 0), (0, 1), (1, 1))):
            px, py = _flip(x, fx), _flip(y, fy)
            cp = pltpu.make_async_remote_copy(
                src_ref=g_ref.at[2 * px + py], dst_ref=recv_ref.at[j], send_sem=send_sems.at[j], recv_sem=recv_sems.at[j],
                device_id=(px, py, c), device_id_type=MESH,
            )
            cp.start()
            copies.append(cp)
        for cp in copies:
            cp.wait_recv()
        for cp in copies:
            cp.wait_send()

    return pl.pallas_call(
        body, out_shape=SDS((3,) + g.shape[1:], g.dtype), in_specs=[ANY], out_specs=ANY,
        scratch_shapes=[pltpu.SemaphoreType.DMA((3,)), pltpu.SemaphoreType.DMA((3,))], name="scatter_grads",
    )(g)


def _send_to_sibling(a, half_of_rows):
    rh = a.shape[1] // 2

    def body(a_ref, b_ref, send_sem, recv_sem):
        x, y, c = _place()
        src = a_ref.at[:, pl.ds((1 - c) * rh, rh), :] if half_of_rows else a_ref
        cp = pltpu.make_async_remote_copy(
            src_ref=src, dst_ref=b_ref, send_sem=send_sem, recv_sem=recv_sem, device_id=(x, y, 1 - c), device_id_type=MESH
        )
        cp.start()
        cp.wait()

    shape = (a.shape[0], rh, 128) if half_of_rows else a.shape
    return pl.pallas_call(
        body, out_shape=SDS(shape, a.dtype), in_specs=[ANY], out_specs=ANY,
        scratch_shapes=[pltpu.SemaphoreType.DMA, pltpu.SemaphoreType.DMA],
        name="sibling_halves" if half_of_rows else "sibling_swap",
    )(a)


def _allreduce_small(v):
    r = v.shape[0]

    def body(v_ref, out_ref, buf_ref, send_sems, recv_sems):
        x, y, c = _place()
        buf_ref[0] = v_ref[...]
        copies = []
        for k in range(1, 8):
            peer = (_flip(x, k >> 2 & 1), _flip(y, k >> 1 & 1), _flip(c, k & 1))
            cp = pltpu.make_async_remote_copy(
                src_ref=v_ref, dst_ref=buf_ref.at[k], send_sem=send_sems.at[k - 1], recv_sem=recv_sems.at[k - 1],
                device_id=peer, device_id_type=MESH,
            )
            cp.start()
            copies.append(cp)
        for cp in copies:
            cp.wait_recv()
        acc = None
        for d in range(8):
            slot = 4 * _flip(x, d >> 2 & 1) + 2 * _flip(y, d >> 1 & 1) + _flip(c, d & 1)
            term = buf_ref[slot]
            acc = term if acc is None else acc + term
        out_ref[...] = acc
        for cp in copies:
            cp.wait_send()

    return pl.pallas_call(
        body, out_shape=SDS(v.shape, F32), in_specs=[VMEM], out_specs=VMEM,
        scratch_shapes=[pltpu.VMEM((8, r, 128), F32), pltpu.SemaphoreType.DMA((7,)), pltpu.SemaphoreType.DMA((7,))],
        name="allreduce_small",
    )(v)


_BIG = (
    ("w_in", 1, False), ("decay_w2_fwd", 1, False), ("decay_w2_bwd", 1, False), ("iclr_a2_fwd", 1, False),
    ("iclr_a2_bwd", 1, False), ("gate_g2", 1, False), ("w_uq", 0, False), ("w_ukv", 1, False), ("w_out", 0, False),
    ("w_ffn_up", 1, False), ("ffn_conv_w", 1, True), ("w_ffn_down", 0, False),
)
_SMALL = (
    "ln_mix_g", "shift_mu_prev", "shift_mu_next", "decay_w0_fwd", "decay_w0_bwd", "iclr_a0_fwd", "iclr_a0_bwd", "k_k",
    "k_a", "r_k", "ln_x_g", "ln_x_b", "q_norm_g", "kv_norm_g", "mla_out_g", "ln_ffn_g", "ffn_conv_b", "ln_final_g",
)
_WEIGHTS = (
    "ln_mix_g", "w_in", "shift_mu_prev", "shift_mu_next", "decay_w0_fwd", "decay_w2_fwd", "decay_w0_bwd", "decay_w2_bwd",
    "iclr_a0_fwd", "iclr_a2_fwd", "iclr_a0_bwd", "iclr_a2_bwd", "gate_g2", "k_k", "k_a", "r_k", "ln_x_g", "ln_x_b",
    "q_norm_g", "w_uq", "kv_norm_g", "w_ukv", "mla_out_g", "w_out", "ln_ffn_g", "w_ffn_up", "ffn_conv_w", "ffn_conv_b",
    "w_ffn_down", "ln_final_g",
)


def _pad_rows(flat, rows):
    return jnp.pad(flat, (0, rows * 128 - flat.shape[0])).reshape(rows, 128)


def _rows_for(n, mult):
    rows = -(-n // 128)
    return -(-rows // mult) * mult


def _pack_shards_bf16(arrs):
    parts = []
    for name, _, raw in _BIG:
        w = arrs[name][0]
        flat = lax.bitcast_convert_type(w, BF16).reshape(-1) if raw else w.astype(BF16).reshape(-1)
        parts.append(_pad_rows(flat, _rows_for(flat.shape[0], 32)))
    return jnp.concatenate(parts, axis=0)


def _unpack_gathered(g4, arrs):
    out, off = {}, 0
    for name, axis, raw in _BIG:
        a, b = arrs[name].shape[1:]
        n = a * b * (2 if raw else 1)
        rows = _rows_for(n, 32)
        seg = g4[:, off:off + rows].reshape(4, rows * 128)[:, :n]
        off += rows
        if raw:
            seg = lax.bitcast_convert_type(seg.reshape(4, a * b, 2), F32)
        seg = seg.reshape(4, a, b)
        out[name] = jnp.concatenate([seg[s] for s in range(4)], axis=1) if axis == 1 else seg.reshape(4 * a, b)
    return out


def _pack_grads(full, arrs):
    parts = []
    for name, axis, _ in _BIG:
        a, b = arrs[name].shape[1:]
        g = full[name]
        sh = g.reshape(a, 4, b).transpose(1, 0, 2) if axis == 1 else g.reshape(4, a, b)
        rows = _rows_for(a * b, 8)
        parts.append(jnp.pad(sh.reshape(4, a * b), ((0, 0), (0, rows * 128 - a * b))).reshape(4, rows, 128))
    packed = jnp.concatenate(parts, axis=1)
    total = packed.shape[1]
    return jnp.pad(packed, ((0, 0), (0, -(-total // 1024) * 1024 - total), (0, 0)))


def _unpack_grads(g, arrs):
    out, off = {}, 0
    for name, _, _ in _BIG:
        a, b = arrs[name].shape[1:]
        rows = _rows_for(a * b, 8)
        out[name] = g[off:off + rows].reshape(-1)[:a * b].reshape(a, b)
        off += rows
    return out


def _pack_small(vals):
    flat = jnp.concatenate([vals[n].reshape(-1).astype(F32) for n in _SMALL] + [vals["_loss"].reshape(-1)])
    return _pad_rows(flat, _rows_for(flat.shape[0], 8))


def _unpack_small(buf, arrs):
    flat, out, off = buf.reshape(-1), {}, 0
    for n in _SMALL:
        size = arrs[n].size
        out[n] = flat[off:off + size].reshape(arrs[n].shape)
        off += size
    out["_loss"] = flat[off]
    return out


def _rot_cols(w):
    return jnp.concatenate([-w[..., 16:], w[..., :16]], axis=-1)


def _rot_cols_t(g):
    return jnp.concatenate([g[..., 16:], -g[..., :16]], axis=-1)


def _rope_tables(t):
    inv = jnp.power(ROPE_THETA, -jnp.arange(0, ROPE_DIM, 2, dtype=F32) / ROPE_DIM)
    ang = jnp.arange(t, dtype=F32)[:, None] * inv[None, :]
    one, zero = jnp.ones((t, 64), F32), jnp.zeros((t, 64), F32)
    cs = jnp.concatenate([one, jnp.cos(ang), jnp.cos(ang), zero[:, :32]], axis=1)
    sn = jnp.concatenate([zero, jnp.sin(ang), jnp.sin(ang), zero[:, :32]], axis=1)
    return cs, sn


def _block_diag(a, b):
    za = jnp.zeros_like(a)
    return jnp.concatenate([jnp.concatenate([a, za], axis=1), jnp.concatenate([za, b], axis=1)], axis=0)


def kernel(x, ln_mix_g, w_in, shift_mu_prev, shift_mu_next, decay_w0_fwd, decay_w2_fwd, decay_w0_bwd, decay_w2_bwd, iclr_a0_fwd, iclr_a2_fwd, iclr_a0_bwd, iclr_a2_bwd, gate_g2, k_k, k_a, r_k, ln_x_g, ln_x_b, q_norm_g, w_uq, kv_norm_g, w_ukv, mla_out_g, w_out, ln_ffn_g, w_ffn_up, ffn_conv_w, ffn_conv_b, w_ffn_down, ln_final_g, loss_target, m_ln_mix_g, m_w_in, m_shift_mu_prev, m_shift_mu_next, m_decay_w0_fwd, m_decay_w2_fwd, m_decay_w0_bwd, m_decay_w2_bwd, m_iclr_a0_fwd, m_iclr_a2_fwd, m_iclr_a0_bwd, m_iclr_a2_bwd, m_gate_g2, m_k_k, m_k_a, m_r_k, m_ln_x_g, m_ln_x_b, m_q_norm_g, m_w_uq, m_kv_norm_g, m_w_ukv, m_mla_out_g, m_w_out, m_ln_ffn_g, m_w_ffn_up, m_ffn_conv_w, m_ffn_conv_b, m_w_ffn_down, m_ln_final_g, v_ln_mix_g, v_w_in, v_shift_mu_prev, v_shift_mu_next, v_decay_w0_fwd, v_decay_w2_fwd, v_decay_w0_bwd, v_decay_w2_bwd, v_iclr_a0_fwd, v_iclr_a2_fwd, v_iclr_a0_bwd, v_iclr_a2_bwd, v_gate_g2, v_k_k, v_k_a, v_r_k, v_ln_x_g, v_ln_x_b, v_q_norm_g, v_w_uq, v_kv_norm_g, v_w_ukv, v_mla_out_g, v_w_out, v_ln_ffn_g, v_w_ffn_up, v_ffn_conv_w, v_ffn_conv_b, v_w_ffn_down, v_ln_final_g):
    arrs = dict(locals())
    b, t, d = x.shape
    m = b * t
    x2 = x.reshape(m, d)
    tgt = loss_target.reshape(m, d)
    vec = lambda n: arrs[n].reshape(1, -1)

    core = lax.axis_index("c")
    chip = 2 * lax.axis_index("x") + lax.axis_index("y")
    my_shard = _pack_shards_bf16(arrs)
    gathered = _allgather_weights(my_shard).reshape(N_CHIPS, -1, 128)
    gathered = lax.dynamic_update_index_in_dim(gathered, my_shard, chip, axis=0)
    fw = _unpack_gathered(gathered, arrs)
    win = fw["w_in"]
    zc = jnp.zeros((d, 64), BF16)
    w_kr = win[:, 2944:2976]
    win_m = jnp.concatenate([win[:, 1920:2944], zc, w_kr, zc[:, :32], zc, _rot_cols(w_kr), zc[:, :32]], axis=1)
    win_r = win[:, :RWKV_COLS]
    uq = fw["w_uq"].astype(F32).reshape(Q_RANK, HEADS, 96)
    z32 = jnp.zeros((Q_RANK, HEADS, 32), F32)
    wq = jnp.concatenate([uq[..., :64], uq[..., 64:], z32], axis=-1).reshape(Q_RANK, 1024)
    wqr = jnp.concatenate([z32, z32, _rot_cols(uq[..., 64:]), z32], axis=-1).reshape(Q_RANK, 1024)
    ukv = fw["w_ukv"].astype(F32).reshape(KV_RANK, HEADS, 128)
    wk = jnp.concatenate([ukv[..., :64], jnp.zeros_like(ukv[..., :64])], axis=-1).reshape(KV_RANK, 1024)
    wv = ukv[..., 64:].reshape(KV_RANK, 512)
    head = jnp.arange(512) // HEAD_DIM
    rw = dict(
        w0=jnp.concatenate([vec("decay_w0_fwd"), vec("decay_w0_bwd")], axis=1),
        w2=_block_diag(fw["decay_w2_fwd"], fw["decay_w2_bwd"]).astype(F32),
        a0=jnp.concatenate([vec("iclr_a0_fwd"), vec("iclr_a0_bwd")], axis=1),
        a2=_block_diag(fw["iclr_a2_fwd"], fw["iclr_a2_bwd"]).astype(F32),
        g2=fw["gate_g2"].astype(F32), k_k=vec("k_k"), k_a=vec("k_a"), r_k=vec("r_k"), ln_x_g=vec("ln_x_g"), ln_x_b=vec("ln_x_b"),
        ones_bd=(head[:, None] == head[None, :]).astype(F32),
    )
    mp = dict(q_norm_g=vec("q_norm_g"), kv_norm_g=vec("kv_norm_g"), wq=wq, wqr=wqr, wk=wk, wv=wv)
    cs, sn = _rope_tables(t)
    w_up_g, w_up_v = fw["w_ffn_up"][:, :D_FF], fw["w_ffn_up"][:, D_FF:]
    cw, cb = fw["ffn_conv_w"], vec("ffn_conv_b")

    n1 = _rms_fwd(x2, vec("ln_mix_g"), "rms_mix")
    zm = _mm(n1, win_m, "nn", "proj_in_mla")
    zr = _mm(n1, win_r, "nn", "proj_in_rwkv")
    zs = _shift_fwd(zr.reshape(b, t, RWKV_COLS), vec("shift_mu_prev"), vec("shift_mu_next"))
    zs2 = zs.reshape(m, RWKV_COLS)
    wf, wb, kf, kb, kk, kaf, kab, gate = _prep_fwd(zs2, rw)
    r4 = lambda a: a.reshape(b, t, 512)
    f2 = lambda a: a.reshape(m, 512)
    kk4 = r4(kk)
    ops_f = (r4(wf), r4(kf), r4(kaf))
    ops_b = (r4(wb), r4(kb), r4(kab))
    y_f, hist_f, y_b, hist_b = _scan_fwd(zs, kk4, ops_f, ops_b)
    y_f, y_b = f2(y_f), f2(y_b)
    y_rwkv = _post_fwd(y_f, y_b, zs2, kf, kb, gate, rw)
    q, kfull, v = _mla_fwd(zm, cs, sn, mp, t)
    o, lse = _attn_fwd(q, kfull, v, b, t)
    y_mla = _rms_fwd(o, vec("mla_out_g"), "rms_mla_out")
    ymix = jnp.concatenate([y_rwkv, y_mla], axis=1)
    h1 = _mm(ymix, fw["w_out"], "nn", "proj_out", add=x2)
    n2 = _rms_fwd(h1, vec("ln_ffn_g"), "rms_ffn")
    ug = _mm(n2, w_up_g, "nn", "ffn_up_gate")
    uv = _mm(n2, w_up_v, "nn", "ffn_up_val")
    r3f = lambda a: a.reshape(b, t, D_FF)
    act = _ffn_mid_fwd(r3f(ug), r3f(uv), cw, cb).reshape(m, D_FF)
    h2 = _mm(act, fw["w_ffn_down"], "nn", "ffn_down", add=h1)
    loss_tab, dh2, g_ln_final = _final(h2, vec("ln_final_g"), tgt)

    gfull = {}
    dact = _mm(dh2, fw["w_ffn_down"], "nt", "d_ffn_act")
    gfull["w_ffn_down"] = _mm(act, dh2, "tn", "g_ffn_down")
    dug, duv, tab_g, tab_v = _ffn_mid_bwd(r3f(ug), r3f(uv), cw, cb, r3f(dact))
    dug, duv = dug.reshape(m, D_FF), duv.reshape(m, D_FF)
    gfull["ffn_conv_w"] = jnp.concatenate([tab_g[0:3], tab_v[0:3]], axis=1)
    g_conv_b = jnp.concatenate([tab_g[3:4], tab_v[3:4]], axis=1)
    dn2 = _mm(duv, w_up_v, "nt", "d_ffn_in_val", add=_mm(dug, w_up_g, "nt", "d_ffn_in_gate"))
    gfull["w_ffn_up"] = jnp.concatenate([_mm(n2, dug, "tn", "g_ffn_up_gate"), _mm(n2, duv, "tn", "g_ffn_up_val")], axis=1)
    dh1, g_ln_ffn = _rms_bwd(h1, vec("ln_ffn_g"), dn2, "rms_ffn_bwd", dres=dh2)
    dymix = _mm(dh1, fw["w_out"], "nt", "d_mix")
    gfull["w_out"] = _mm(ymix, dh1, "tn", "g_w_out")
    do, g_mla_out = _rms_bwd(o, vec("mla_out_g"), dymix, "rms_mla_out_bwd", dy_block=1)
    dq, dk, dv = _attn_bwd(q, kfull, v, o, lse, do, b, t)
    dzm, g_qn, g_kvn, g_wq, g_wqr, g_wk, g_wv = _mla_bwd(zm, cs, sn, mp, t, dq, dk, dv)
    gq3, gqr3 = g_wq.reshape(Q_RANK, HEADS, 128), g_wqr.reshape(Q_RANK, HEADS, 128)
    gfull["w_uq"] = jnp.concatenate(
        [gq3[..., :64], gq3[..., 64:96] + _rot_cols_t(gqr3[..., 64:96])], axis=-1
    ).reshape(Q_RANK, HEADS * 96)
    gfull["w_ukv"] = jnp.concatenate(
        [g_wk.reshape(KV_RANK, HEADS, 128)[..., :64], g_wv.reshape(KV_RANK, HEADS, 64)], axis=-1
    ).reshape(KV_RANK, 1024)
    dys, dr_p, dk_p, dv_p, dgate, g_rk, g_lnx_g, g_lnx_b = _post_bwd(y_f, y_b, zs2, kf, kb, gate, rw, dymix)
    (dr_f, dwf, dkf, dkk_f, dkaf, dv_f, dr_b, dwb, dkb, dkk_b, dkab, dv_b) = _scan_bwd(
        zs, kk4, r4(dys), ops_f, hist_f, ops_b, hist_b)
    cts = dict(dwf=f2(dwf), dwb=f2(dwb), dkf=f2(dkf), dkb=f2(dkb), dkk_f=f2(dkk_f), dkk_b=f2(dkk_b), dkaf=f2(dkaf), dkab=f2(dkab),
               dr_f=f2(dr_f), dr_b=f2(dr_b), dr_p=dr_p, dk_p=dk_p, dv_p=dv_p, dg=dgate, dv_f=f2(dv_f), dv_b=f2(dv_b))
    dzs, g_w0, g_w2, g_a0, g_a2, g_g2, g_kk, g_ka = _prep_bwd(zs2, rw, cts)
    dzr, g_mu_p, g_mu_n = _shift_bwd(dzs.reshape(b, t, RWKV_COLS), zr.reshape(b, t, RWKV_COLS), vec("shift_mu_prev"), vec("shift_mu_next"))
    dzr = dzr.reshape(m, RWKV_COLS)
    gfull["decay_w2_fwd"], gfull["decay_w2_bwd"] = g_w2[:64, :512], g_w2[64:, 512:]
    gfull["iclr_a2_fwd"], gfull["iclr_a2_bwd"] = g_a2[:64, :512], g_a2[64:, 512:]
    gfull["gate_g2"] = g_g2
    dn1 = _mm(dzr, win_r, "nt", "d_proj_in_rwkv", add=_mm(dzm, win_m, "nt", "d_proj_in_mla"))
    g_m = _mm(n1, dzm, "tn", "g_w_in_mla")
    g_r = _mm(n1, dzr, "tn", "g_w_in_rwkv")
    g_kr = g_m[:, 1088:1120] + _rot_cols_t(g_m[:, 1216:1248])
    gfull["w_in"] = jnp.concatenate([g_r, g_m[:, :1024], g_kr], axis=1)
    dx, g_ln_mix = _rms_bwd(x2, vec("ln_mix_g"), dn1, "rms_mix_bwd", dres=dh1)

    packed = _pack_grads(gfull, arrs)
    rh = packed.shape[1] // 2
    own = lax.dynamic_slice_in_dim(packed, core * rh, rh, axis=1)
    sib = _send_to_sibling(packed, True)
    chip_part = _add_rows([own.reshape(4 * rh, 128), sib.reshape(4 * rh, 128)], "sum_cores", BF16).reshape(4, rh, 128)
    recv = _scatter_to_chips(chip_part)
    mine = lax.dynamic_index_in_dim(chip_part, chip, axis=0, keepdims=False)
    half = _add_rows([mine, recv[0], recv[1], recv[2]], "sum_chips")
    other = _send_to_sibling(half, False)
    lower = jnp.where(core == 0, half, other)
    upper = jnp.where(core == 0, other, half)
    g_big = _unpack_grads(jnp.concatenate([lower, upper], axis=0), arrs)
    small = {
        "ln_mix_g": g_ln_mix, "shift_mu_prev": g_mu_p, "shift_mu_next": g_mu_n, "decay_w0_fwd": g_w0[:, :512],
        "decay_w0_bwd": g_w0[:, 512:], "iclr_a0_fwd": g_a0[:, :512], "iclr_a0_bwd": g_a0[:, 512:], "k_k": g_kk, "k_a": g_ka,
        "r_k": g_rk, "ln_x_g": g_lnx_g, "ln_x_b": g_lnx_b, "q_norm_g": g_qn, "kv_norm_g": g_kvn, "mla_out_g": g_mla_out,
        "ln_ffn_g": g_ln_ffn, "ffn_conv_b": g_conv_b, "ln_final_g": g_ln_final,
        "_loss": jnp.pad(loss_tab[0, 0:1], (0, 127)),
    }
    g_small_buf = _allreduce_small(_pack_small(small))
    g_small = _unpack_small(g_small_buf, arrs)

    grads, deltas, new_m, new_v = {}, {}, {}, {}
    for name, _, _ in _BIG:
        shape = arrs[name].shape
        two = lambda a: a.reshape(shape[1:])
        dlt, m2, v2 = _adamw(two(arrs[name]), g_big[name], two(arrs["m_" + name]), two(arrs["v_" + name]), "adamw_" + name)
        grads[name] = g_big[name].reshape(shape)
        deltas[name], new_m[name], new_v[name] = dlt.reshape(shape), m2.reshape(shape), v2.reshape(shape)
    pk = lambda pre: _pack_small({**{n: arrs[pre + n] for n in _SMALL}, "_loss": jnp.zeros((128,), F32)})
    sd, sm, sv = _adamw(pk(""), g_small_buf, pk("m_"), pk("v_"), "adamw_small")
    sd, sm, sv = _unpack_small(sd, arrs), _unpack_small(sm, arrs), _unpack_small(sv, arrs)
    for n in _SMALL:
        grads[n], deltas[n], new_m[n], new_v[n] = g_small[n], sd[n], sm[n], sv[n]

    return (g_small["_loss"], dx.reshape(b, t, d), *[grads[n] for n in _WEIGHTS], *[deltas[n] for n in _WEIGHTS],
            *[new_m[n] for n in _WEIGHTS], *[new_v[n] for n in _WEIGHTS])
```

```python
import functools
import math

import jax
import jax.numpy as jnp
from jax import lax
from jax.experimental import pallas as pl
from jax.experimental.pallas import tpu as pltpu

F32, BF16 = jnp.float32, jnp.bfloat16
MESH = pl.DeviceIdType.MESH
ANY = pl.BlockSpec(memory_space=pl.ANY)
VMEM = pl.BlockSpec(memory_space=pltpu.VMEM)
BS = pl.BlockSpec
SDS = jax.ShapeDtypeStruct

NORM_EPS = 1e-6
GN_EPS = 64e-5
L2_EPS = 1e-12
HEADS = 8
HEAD_DIM = 64
D_RWKV = HEADS * HEAD_DIM
ROPE_DIM = 32
ROPE_THETA = 10000.0
MLA_SCALE = (64 + ROPE_DIM) ** -0.5
Q_RANK, KV_RANK = 768, 256
RWKV_COLS = 1920
MLA_PAD_COLS = Q_RANK + KV_RANK + 256
D_FF = 2816
ADAM_LR, ADAM_B1, ADAM_B2, ADAM_EPS, ADAM_WD, ADAM_STEP = 0.001, 0.9, 0.999, 1e-08, 0.01, 10

V7X_LANES = 128
V7X_VMEM_LIMIT = 48 * 1024 * 1024
SCAN_CHUNK = 16
N_CHIPS = 4


def _cp(*sem):
    return pltpu.CompilerParams(dimension_semantics=sem, vmem_limit_bytes=V7X_VMEM_LIMIT)


def _tile(n, cands=(512, 640, 384, 256, 128)):
    for c in cands:
        if n % c == 0:
            return c
    return n


def _row_tile(n, cap=256):
    best = n
    for t in range(8, cap + 1, 8):
        if n % t == 0:
            best = t
    return best if best <= cap or n <= cap else n


def _rms(x, g):
    ms = jnp.mean(x * x, axis=-1, keepdims=True)
    return x * lax.rsqrt(ms + NORM_EPS) * g


@jax.custom_vjp
def _bdot(x, w):
    return jnp.dot(x.astype(BF16), w.astype(BF16), preferred_element_type=F32)


def _bdot_fwd(x, w):
    return _bdot(x, w), (x, w)


def _bdot_bwd(res, ct):
    x, w = res
    c = ct.astype(BF16)
    dx = lax.dot_general(c, w.astype(BF16), (((1,), (1,)), ((), ())), preferred_element_type=F32)
    dw = lax.dot_general(x.astype(BF16), c, (((0,), (0,)), ((), ())), preferred_element_type=F32)
    return dx.astype(x.dtype), dw.astype(w.dtype)


_bdot.defvjp(_bdot_fwd, _bdot_bwd)


@jax.custom_vjp
def _headsum(x, ones_bd):
    hi = x.astype(BF16)
    mid = (x - hi.astype(F32)).astype(BF16)
    ob = ones_bd.astype(BF16)
    return jnp.dot(hi, ob, preferred_element_type=F32) + jnp.dot(mid, ob, preferred_element_type=F32)


def _headsum_fwd(x, ones_bd):
    return _headsum(x, ones_bd), ones_bd


def _headsum_bwd(ones_bd, ct):
    return _headsum(ct, ones_bd), jnp.zeros_like(ones_bd)


_headsum.defvjp(_headsum_fwd, _headsum_bwd)


def _prep_fn(zs, w0, w2, a0, a2, g2, k_k, k_a, ones_bd):
    k = zs[:, 512:1024]
    wd = zs[:, 1536:1664]
    ad = zs[:, 1664:1792]
    gd = zs[:, 1792:1920]
    logit = w0 + _bdot(jnp.tanh(wd), w2)
    w = jnp.exp(-math.exp(-0.5) * jax.nn.sigmoid(logit))
    a = jax.nn.sigmoid(a0 + _bdot(ad, a2))
    g = _bdot(jax.nn.sigmoid(gd), g2)
    kkr = k * k_k
    nrm = jnp.sqrt(_headsum(kkr * kkr, ones_bd))
    kk = kkr / jnp.maximum(nrm, L2_EPS)
    a_f, a_b = a[:, :512], a[:, 512:]
    kf = k * (1.0 + (a_f - 1.0) * k_a)
    kb = k * (1.0 + (a_b - 1.0) * k_a)
    return w[:, :512], w[:, 512:], kf, kb, kk, kk * a_f, kk * a_b, g


def _post_fn(y, r, kf, kb, v, g, r_k, ln_g, ln_b, ones_bd):
    mu =_headsum(y, ones_bd) * (1.0 / HEAD_DIM)
    yc = y - mu
    var = _headsum(yc * yc, ones_bd) * (1.0 / HEAD_DIM)
    yn = yc * lax.rsqrt(var + GN_EPS) * ln_g + ln_b
    bonus = _headsum(r * (kf + kb) * r_k, ones_bd) * v
    return (yn + bonus) * g


def _cat8(x):
    return jnp.concatenate([x] * HEADS, axis=1)


def _mla_fn(zm, cs, sn, gq, gkv, wq, wqr, wk, wv):
    cq = zm[:, :Q_RANK]
    ckv = zm[:, Q_RANK:Q_RANK + KV_RANK]
    kr = zm[:, Q_RANK + KV_RANK:Q_RANK + KV_RANK + 128]
    krr = zm[:, Q_RANK + KV_RANK + 128:]
    cqn = _rms(cq, gq)
    ckvn = _rms(ckv, gkv)
    q = _bdot(cqn, wq) * _cat8(cs) + _bdot(cqn, wqr) * _cat8(sn)
    kro = kr * cs + krr * sn
    kfull = _bdot(ckvn, wk) + _cat8(kro)
    v = _bdot(ckvn, wv)
    return q, kfull, v


def _adamw_math(w, g, m, v):
    m2 = ADAM_B1 * m + (1.0 - ADAM_B1) * g
    v2 = ADAM_B2 * v + (1.0 - ADAM_B2) * (g * g)
    m_hat = m2 / (1.0 - ADAM_B1 ** ADAM_STEP)
    v_hat = v2 / (1.0 - ADAM_B2 ** ADAM_STEP)
    delta = -ADAM_LR * (m_hat / (jnp.sqrt(v_hat) + ADAM_EPS) + ADAM_WD * w)
    return delta, m2, v2


_DIMS = {"nn": (((1,), (0,)), ((), ())), "nt": (((1,), (1,)), ((), ())), "tn": (((0,), (0,)), ((), ()))}


def _mm(a, b, mode, name, out_dtype=F32, add=None):
    if mode == "nn":
        (m, k), (_, n) = a.shape, b.shape
    elif mode == "nt":
        (m, k), (n, _) = a.shape, b.shape
    else:
        (k, m), (_, n) = a.shape, b.shape
    big = (1024, 1408, 768, 640, 512, 384, 256, 128)
    tm, tn, tk = _tile(m, big), _tile(n, big), _tile(k, (512, 1408, 640, 384, 256, 128))
    nk = k // tk

    def body(a_ref, b_ref, *rest):
        if add is None:
            o_ref, acc_ref = rest
        else:
            add_ref, o_ref, acc_ref = rest
        kk = pl.program_id(2)

        @pl.when(kk == 0)
        def _():
            acc_ref[...] = jnp.zeros_like(acc_ref)

        acc_ref[...] += lax.dot_general(
            a_ref[...].astype(BF16), b_ref[...].astype(BF16), _DIMS[mode], preferred_element_type=F32
        )

        @pl.when(kk == nk - 1)
        def _():
            r = acc_ref[...]
            if add is not None:
                r = r + add_ref[...]
            o_ref[...] = r.astype(out_dtype)

    a_spec = BS((tk, tm), lambda i, j, kk: (kk, i)) if mode == "tn" else BS((tm, tk), lambda i, j, kk: (i, kk))
    b_spec = BS((tn, tk), lambda i, j, kk: (j, kk)) if mode == "nt" else BS((tk, tn), lambda i, j, kk: (kk, j))
    o_spec = BS((tm, tn), lambda i, j, kk: (i, j))
    ins, specs = [a, b], [a_spec, b_spec]
    if add is not None:
        ins.append(add)
        specs.append(o_spec)
    return pl.pallas_call(
        body, grid=(m // tm, n // tn, nk), in_specs=specs, out_specs=o_spec, out_shape=SDS((m, n), out_dtype),
        scratch_shapes=[pltpu.VMEM((tm, tn), F32)], compiler_params=_cp("parallel", "parallel", "arbitrary"), name=name,
    )(*ins)


def _rms_fwd(x, g, name):
    m, d = x.shape
    tr = _tile(m)

    def body(x_ref, g_ref, o_ref):
        o_ref[...] = _rms(x_ref[...], g_ref[...]).astype(BF16)

    return pl.pallas_call(
        body, grid=(m // tr,), in_specs=[BS((tr, d), lambda i: (i, 0)), BS((1, d), lambda i: (0, 0))],
        out_specs=BS((tr, d), lambda i: (i, 0)), out_shape=SDS((m, d), BF16), compiler_params=_cp("parallel"), name=name,
    )(x, g)


def _rms_bwd(x, g, dy, name, dres=None, dy_block=0):
    m, d = x.shape
    tr = _row_tile(m)

    def body(x_ref, g_ref, dy_ref, *rest):
        if dres is None:
            dx_ref, dg_ref = rest
        else:
            dres_ref, dx_ref, dg_ref = rest
        _, vjp = jax.vjp(_rms, x_ref[...], g_ref[...])
        dx, dg = vjp(dy_ref[...])
        if dres is not None:
            dx = dx + dres_ref[...]
        dx_ref[...] = dx

        @pl.when(pl.program_id(0) == 0)
        def _():
            dg_ref[...] = jnp.zeros_like(dg_ref)

        dg_ref[...] += dg

    row = BS((tr, d), lambda i: (i, 0))
    vec = BS((1, d), lambda i: (0, 0))
    ins, specs = [x, g, dy], [row, vec, BS((tr, d), lambda i: (i, dy_block))]
    if dres is not None:
        ins.append(dres)
        specs.append(row)
    return pl.pallas_call(
        body, grid=(m // tr,), in_specs=specs, out_specs=[row, vec], out_shape=[SDS((m, d), F32), SDS((1, d), F32)],
        compiler_params=_cp("arbitrary"), name=name,
    )(*ins)


def _final(h, g, tgt):
    m, d = h.shape
    tr = _row_tile(m)

    def loss_fn(hh, gg, tt):
        e = _rms(hh, gg) - tt
        return 0.5 * jnp.sum(e * e) * (1.0 / d)

    def body(h_ref, g_ref, t_ref, l_ref, dh_ref, dg_ref):
        val, (dh, dg) = jax.value_and_grad(loss_fn, argnums=(0, 1))(h_ref[...], g_ref[...], t_ref[...])
        dh_ref[...] = dh

        @pl.when(pl.program_id(0) == 0)
        def _():
            dg_ref[...] = jnp.zeros_like(dg_ref)
            l_ref[...] = jnp.zeros_like(l_ref)

        dg_ref[...] += dg
        l_ref[...] += jnp.full(l_ref.shape, val, F32)

    row = BS((tr, d), lambda i: (i, 0))
    vec = BS((1, d), lambda i: (0, 0))
    return pl.pallas_call(
        body, grid=(m // tr,), in_specs=[row, vec, row], out_specs=[BS((8, 128), lambda i: (0, 0)), row, vec],
        out_shape=[SDS((8, 128), F32), SDS((m, d), F32), SDS((1, d), F32)], compiler_params=_cp("arbitrary"), name="final_loss",
    )(h, g, tgt)


def _prev_next(z, t):
    row = lax.broadcasted_iota(jnp.int32, z.shape, 0)
    zp = jnp.where(row == 0, 0.0, pltpu.roll(z, 1, axis=0))
    zn = jnp.where(row == t - 1, 0.0, pltpu.roll(z, t - 1, axis=0))
    return zp, zn


def _shift_fwd(z3, mu_p, mu_n):
    b, t, c = z3.shape
    nc = c // 128

    def body(z_ref, mp_ref, mn_ref, o_ref):
        z = z_ref[0]
        zp, zn = _prev_next(z, t)
        o_ref[0] = z + mp_ref[...] * (zp - z) + mn_ref[...] * (zn - z)

    blk = BS((1, t, 128), lambda i, j: (i, 0, j))
    vec = BS((1, 128), lambda i, j: (0, j))
    return pl.pallas_call(
        body, grid=(b, nc), in_specs=[blk, vec, vec], out_specs=blk, out_shape=SDS((b, t, c), F32),
        compiler_params=_cp("parallel", "parallel"), name="shift_fwd",
    )(z3, mu_p, mu_n)


def _shift_bwd(dzs3, z3, mu_p, mu_n):
    b, t, c = z3.shape
    nc = c // 128

    def body(d_ref, z_ref, mp_ref, mn_ref, dz_ref, dmp_ref, dmn_ref):
        d, z = d_ref[0], z_ref[0]
        mp, mn = mp_ref[...], mn_ref[...]
        zp, zn = _prev_next(z, t)
        _, dp_next = _prev_next(d * mp, t)
        dn_prev, _ = _prev_next(d * mn, t)
        dz_ref[0] = d * (1.0 - mp - mn) + dp_next + dn_prev

        @pl.when(pl.program_id(1) == 0)
        def _():
            dmp_ref[...] = jnp.zeros_like(dmp_ref)
            dmn_ref[...] = jnp.zeros_like(dmn_ref)

        dmp_ref[...] += jnp.sum(d * (zp - z), axis=0, keepdims=True)
        dmn_ref[...] += jnp.sum(d * (zn - z), axis=0, keepdims=True)

    blk = BS((1, t, 128), lambda j, i: (i, 0, j))
    vec = BS((1, 128), lambda j, i: (0, j))
    return pl.pallas_call(
        body, grid=(nc, b), in_specs=[blk, blk, vec, vec], out_specs=[blk, vec, vec],
        out_shape=[SDS((b, t, c), F32), SDS((1, c), F32), SDS((1, c), F32)],
        compiler_params=_cp("parallel", "arbitrary"), name="shift_bwd",
    )(dzs3, z3, mu_p, mu_n)


def _const(shape):
    nd = len(shape)
    return BS(shape, lambda i: (0,) * nd)


def _prep_fwd(zs, p):
    m = zs.shape[0]
    tr = 256
    params = [p["w0"], p["w2"], p["a0"], p["a2"], p["g2"], p["k_k"], p["k_a"], p["ones_bd"]]

    def body(zs_ref, w0, w2, a0, a2, g2, kk_, ka_, bd, wf, wb, kf, kb, kk, kaf, kab, g):
        outs = _prep_fn(zs_ref[...], w0[...], w2[...], a0[...], a2[...], g2[...], kk_[...], ka_[...], bd[...])
        for ref, val in zip((wf, wb, kf, kb, kk, kaf, kab, g), outs):
            ref[...] = val

    row = BS((tr, 512), lambda i: (i, 0))
    return pl.pallas_call(
        body, grid=(m // tr,), in_specs=[BS((tr, RWKV_COLS), lambda i: (i, 0))] + [_const(q.shape) for q in params],
        out_specs=[row] * 8, out_shape=[SDS((m, 512), F32)] * 8, compiler_params=_cp("parallel"), name="rwkv_prep_fwd",
    )(zs, *params)


def _prep_bwd(zs, p, ct_rows):
    m = zs.shape[0]
    tr = 128
    params = [p["w0"], p["w2"], p["a0"], p["a2"], p["g2"], p["k_k"], p["k_a"]]
    names = ["dwf", "dwb", "dkf", "dkb", "dkk_f", "dkk_b", "dkaf", "dkab", "dr_f", "dr_b", "dr_p", "dk_p", "dv_p", "dg",
             "dv_f", "dv_b"]
    rows = [ct_rows[n] for n in names]

    def body(zs_ref, w0, w2, a0, a2, g2, kk_, ka_, bd, *rest):
        c = {n: r[...] for n, r in zip(names, rest[:len(names)])}
        outs = rest[len(names):]
        dzs_ref, grads = outs[0], outs[1:]
        ones_bd = bd[...]
        _, vjp = jax.vjp(
            lambda *q: _prep_fn(*q, ones_bd), zs_ref[...], w0[...], w2[...], a0[...], a2[...], g2[...], kk_[...], ka_[...]
        )
        cts = (c["dwf"], c["dwb"], c["dkf"] + c["dk_p"], c["dkb"] + c["dk_p"], c["dkk_f"] + c["dkk_b"], c["dkaf"], c["dkab"], c["dg"])
        dzs, *dparams = vjp(cts)
        dr = c["dr_f"] + c["dr_b"] + c["dr_p"]
        dv = c["dv_f"] + c["dv_b"] + c["dv_p"]
        dzs_ref[:, 0:512] = dzs[:, 0:512] + dr
        dzs_ref[:, 512:1024] = dzs[:, 512:1024]
        dzs_ref[:, 1024:1536] = dzs[:, 1024:1536] + dv
        dzs_ref[:, 1536:1920] = dzs[:, 1536:1920]

        @pl.when(pl.program_id(0) == 0)
        def _():
            for gr in grads:
                gr[...] = jnp.zeros_like(gr)

        for gr, val in zip(grads, dparams):
            gr[...] += val

    row = BS((tr, 512), lambda i: (i, 0))
    return pl.pallas_call(
        body, grid=(m // tr,),
        in_specs=[BS((tr, RWKV_COLS), lambda i: (i, 0))] + [_const(q.shape) for q in params] + [_const(p["ones_bd"].shape)]
        + [row] * len(names),
        out_specs=[BS((tr, RWKV_COLS), lambda i: (i, 0))] + [_const(q.shape) for q in params],
        out_shape=[SDS((m, RWKV_COLS), F32)] + [SDS(q.shape, F32) for q in params],
        compiler_params=_cp("arbitrary"), name="rwkv_prep_bwd",
    )(zs, *params, p["ones_bd"], *rows)


def _post_specs(tr):
    r = BS((tr, 512), lambda i: (i, 0))
    v = BS((tr, 512), lambda i: (i, 2))
    row = BS((tr, 512), lambda i: (i, 0))
    return r, v, row


def _post_fwd(y_f, y_b, zs, kf, kb, g, p):
    m = zs.shape[0]
    tr = 256
    r, v, row = _post_specs(tr)
    vecs = [p["r_k"], p["ln_x_g"], p["ln_x_b"], p["ones_bd"]]

    def body(yf, yb, r_ref, v_ref, kf_ref, kb_ref, g_ref, rk, lg, lb, bd, o_ref):
        o_ref[...] = _post_fn(
            yf[...] + yb[...], r_ref[...], kf_ref[...], kb_ref[...], v_ref[...], g_ref[...], rk[...], lg[...], lb[...], bd[...]
        ).astype(BF16)

    return pl.pallas_call(
        body, grid=(m // tr,), in_specs=[row, row, r, v, row, row, row] + [_const(q.shape) for q in vecs],
        out_specs=row, out_shape=SDS((m, 512), BF16), compiler_params=_cp("parallel"), name="rwkv_post_fwd",
    )(y_f, y_b, zs, zs, kf, kb, g, *vecs)


def _post_bwd(y_f, y_b, zs, kf, kb, g, p, dymix):
    m = zs.shape[0]
    tr = 128
    r, v, row = _post_specs(tr)
    vecs = [p["r_k"], p["ln_x_g"], p["ln_x_b"]]

    def body(yf, yb, r_ref, v_ref, kf_ref, kb_ref, g_ref, rk, lg, lb, bd, dy_ref, dyo, dr, dk, dv, dg, drk, dlg, dlb):
        ones_bd = bd[...]
        _, vjp = jax.vjp(
            lambda *q: _post_fn(*q, ones_bd),
            yf[...] + yb[...], r_ref[...], kf_ref[...], kb_ref[...], v_ref[...], g_ref[...], rk[...], lg[...], lb[...],
        )
        c_y, c_r, c_kf, _, c_v, c_g, c_rk, c_lg, c_lb = vjp(dy_ref[...])
        dyo[...] = c_y
        dr[...] = c_r
        dk[...] = c_kf
        dv[...] = c_v
        dg[...] = c_g

        @pl.when(pl.program_id(0) == 0)
        def _():
            for ref in (drk, dlg, dlb):
                ref[...] = jnp.zeros_like(ref)

        drk[...] += c_rk
        dlg[...] += c_lg
        dlb[...] += c_lb

    vec = _const((1, 512))
    return pl.pallas_call(
        body, grid=(m // tr,),
        in_specs=[row, row, r, v, row, row, row] + [_const(q.shape) for q in vecs] + [_const(p["ones_bd"].shape), row],
        out_specs=[row, row, row, row, row, vec, vec, vec],
        out_shape=[SDS((m, 512), F32)] * 5 + [SDS((1, 512), F32)] * 3,
        compiler_params=_cp("arbitrary"), name="rwkv_post_bwd",
    )(y_f, y_b, zs, zs, kf, kb, g, *vecs, p["ones_bd"], dymix)


def _half_ones():
    ri = lax.broadcasted_iota(jnp.int32, (256, 128), 0) & 127
    ci = lax.broadcasted_iota(jnp.int32, (256, 128), 1)
    return jnp.where((ri < 64) == (ci < 64), 1.0, 0.0).astype(BF16)


def _half_sums(xs, ones, exact):
    out = [None] * len(xs)
    for two in (True, False):
        idx = [i for i, e in enumerate(exact) if bool(e) == two]
        if not idx:
            continue
        x = jnp.concatenate([xs[i] for i in idx], axis=0)
        hi = x.astype(BF16)
        if two:
            mid = (x - hi.astype(F32)).astype(BF16)
            res = jnp.dot(jnp.concatenate([hi, mid], axis=1), ones, preferred_element_type=F32)
        else:
            res = jnp.dot(hi, ones[:128], preferred_element_type=F32)
        for j, i in enumerate(idx):
            out[i] = res[64 * j:64 * j + 64]
    return out


def _scan_specs(b, t):
    nc = t // SCAN_CHUNK
    up, down = (lambda c: c), (lambda c: nc - 1 - c)
    rows = [BS((b, SCAN_CHUNK, 512), lambda c, ci=ci: (0, ci(c), 0)) for ci in (up, down)]
    vrows = [BS((b, SCAN_CHUNK, 512), lambda c, ci=ci: (0, ci(c), 2)) for ci in (up, down)]
    hist = [BS((SCAN_CHUNK, b * 4, 64, 128), lambda c, ci=ci: (ci(c), 0, 0, 0)) for ci in (up, down)]
    return nc, rows, vrows, hist


class _Window:
    def __init__(self, g, ascending):
        self.bases = [pl.multiple_of(g * 8, 8) if asc else pl.multiple_of(SCAN_CHUNK - 8 - g * 8, 8) for asc in ascending]
        self.ascending = ascending
        self.blocks = {}
        self.row_id = lax.broadcasted_iota(jnp.int32, (8, 128), 0)

    def j(self, d, s):
        return s if self.ascending[d] else 7 - s

    def time(self, d, s):
        return self.bases[d] + self.j(d, s)

    def row(self, ref, d, bi, cols, s):
        key = (id(ref), d, bi, cols.start)
        if key not in self.blocks:
            self.blocks[key] = ref[bi, pl.ds(self.bases[d], 8), cols]
        jj = self.j(d, s)
        return self.blocks[key][jj:jj + 1, :]

    def put(self, buf, key, d, s, row):
        prev = buf.get(key)
        new = jnp.broadcast_to(row, (8, 128))
        buf[key] = new if prev is None else jnp.where(self.row_id == self.j(d, s), new, prev)

    def flush(self, buf, refs_of):
        for key, val in buf.items():
            ref, d, bi, cols = refs_of(key)
            ref[bi, pl.ds(self.bases[d], 8), cols] = val


def _pairs(b):
    return [(bi * 4 + p, bi, slice(128 * p, 128 * p + 128)) for bi in range(b) for p in range(4)]


def _colsum(x):
    return jnp.sum(x, axis=0, keepdims=True)


def _eye_mask():
    return (lax.broadcasted_iota(jnp.int32, (64, 128), 1) & 63) == lax.broadcasted_iota(jnp.int32, (64, 128), 0)


def _scan_fwd(zs, kk, ops_f, ops_b):
    b, t = zs.shape[:2]
    nc, rows, vrows, hist = _scan_specs(b, t)
    npair = b * 4

    def body(*refs):
        ins, outs, s_ref = refs[:12], refs[12:16], refs[16]
        dirs = [dict(zip(("r", "kk", "v", "w", "k", "ka", "y", "h"), (*ins[6 * d:6 * d + 6], *outs[2 * d:2 * d + 2])))
                for d in (0, 1)]

        @pl.when(pl.program_id(0) == 0)
        def _():
            s_ref[...] = jnp.zeros_like(s_ref)

        ones, eye = _half_ones(), _eye_mask()
        chains = [(d, pr, bi, cols) for d in (0, 1) for pr, bi, cols in _pairs(b)]

        def eight_steps(g, carry):
            win = _Window(g, (True, False))
            ybuf = {}
            for s in range(8):
                s_prev, xa = [], []
                for d, pr, bi, cols in chains:
                    q = dirs[d]
                    st = s_ref[d * npair + pr]
                    q["h"][win.time(d, s), pr] = st
                    s_prev.append(st)
                    xa += [st * win.row(q["kk"], d, bi, cols, s), jnp.where(eye, win.row(q["v"], d, bi, cols, s), 0.0)]
                ra = _half_sums(xa, ones, (True, False) * len(chains))
                xb = []
                for i, (d, pr, bi, cols) in enumerate(chains):
                    q = dirs[d]
                    s_new = s_prev[i] * win.row(q["w"], d, bi, cols, s) - ra[2 * i] * win.row(q["ka"], d, bi, cols, s) \
                        + ra[2 * i + 1] * win.row(q["k"], d, bi, cols, s)
                    s_ref[d * npair + pr] = s_new
                    xb.append(s_new * win.row(q["r"], d, bi, cols, s))
                rb = _half_sums(xb, ones, (False,) * len(chains))
                for i, (d, pr, bi, cols) in enumerate(chains):
                    win.put(ybuf, i, d, s, _colsum(jnp.where(eye, rb[i], 0.0)))
            win.flush(ybuf, lambda i: (dirs[chains[i][0]]["y"], chains[i][0], chains[i][2], chains[i][3]))
            return carry

        lax.fori_loop(0, SCAN_CHUNK // 8, eight_steps, 0)

    row_shape, hist_shape = SDS((b, t, 512), F32), SDS((t, npair, 64, 128), F32)
    state = (2 * npair, 64, 128)
    return pl.pallas_call(
        body, grid=(nc,), in_specs=sum(([rows[d], rows[d], vrows[d]] + [rows[d]] * 3 for d in (0, 1)), []),
        out_specs=[rows[0], hist[0], rows[1], hist[1], BS(state, lambda c: (0, 0, 0))],
        out_shape=[row_shape, hist_shape, row_shape, hist_shape, SDS(state, F32)],
        compiler_params=_cp("arbitrary"), name="wkv_scan",
    )(zs, kk, zs, *ops_f, zs, kk, zs, *ops_b)


def _scan_bwd(zs, kk, dy, ops_f, hist_f, ops_b, hist_b, s_last):
    b, t = zs.shape[:2]
    nc, rows, vrows, hist = _scan_specs(b, t)
    npair = b * 4
    names_in = ("r", "kk", "v", "dy", "w", "k", "ka", "h")
    names_out = ("dr", "dw", "dk", "dkk", "dka", "dv")

    def body(*refs):
        ins, last_ref, outs, ds_ref, after_ref = refs[:16], refs[16], refs[17:29], refs[29], refs[30]
        dirs = [dict(zip(names_in + names_out, (*ins[8 * d:8 * d + 8], *outs[6 * d:6 * d + 6]))) for d in (0, 1)]

        @pl.when(pl.program_id(0) == 0)
        def _():
            ds_ref[...] = jnp.zeros_like(ds_ref)
            after_ref[...] = last_ref[...]

        ones, eye = _half_ones(), _eye_mask()
        chains = [(d, pr, bi, cols) for d in (0, 1) for pr, bi, cols in _pairs(b)]

        def eight_steps(g, carry):
            win = _Window(g, (False, True))
            obuf = {}
            s_after = [after_ref[d * npair + pr] for d, pr, _, _ in chains]
            for s in range(8):
                row = lambda name, d, bi, cols: win.row(dirs[d][name], d, bi, cols, s)
                s_prev, xa = [], []
                for d, pr, bi, cols in chains:
                    st = dirs[d]["h"][win.time(d, s), pr]
                    s_prev.append(st)
                    xa += [st * row("kk", d, bi, cols), jnp.where(eye, row("v", d, bi, cols), 0.0),
                           jnp.where(eye, row("dy", d, bi, cols), 0.0)]
                ra = _half_sums(xa, ones, (False,) * len(xa))
                ds_now, xb = [], []
                for i, (d, pr, bi, cols) in enumerate(chains):
                    skk, vcol, dycol = ra[3 * i], ra[3 * i + 1], ra[3 * i + 2]
                    ds = ds_ref[d * npair + pr] + dycol * row("r", d, bi, cols)
                    win.put(obuf, (i, "dr"), d, s, _colsum(s_after[i] * dycol))
                    win.put(obuf, (i, "dk"), d, s, _colsum(ds * vcol))
                    win.put(obuf, (i, "dka"), d, s, -_colsum(ds * skk))
                    win.put(obuf, (i, "dw"), d, s, _colsum(ds * s_prev[i]))
                    ds_now.append(ds)
                    xb += [ds * row("k", d, bi, cols), ds * row("ka", d, bi, cols)]
                rb = _half_sums(xb, ones, (False,) * len(xb))
                for i, (d, pr, bi, cols) in enumerate(chains):
                    dskk_neg = rb[2 * i + 1]
                    win.put(obuf, (i, "dv"), d, s, _colsum(jnp.where(eye, rb[2 * i], 0.0)))
                    win.put(obuf, (i, "dkk"), d, s, -_colsum(s_prev[i] * dskk_neg))
                    ds_ref[d * npair + pr] = ds_now[i] * row("w", d, bi, cols) - dskk_neg * row("kk", d, bi, cols)
                s_after = s_prev
            for i, (d, pr, _, _) in enumerate(chains):
                after_ref[d * npair + pr] = s_after[i]
            win.flush(obuf, lambda key: (dirs[chains[key[0]][0]][key[1]], chains[key[0]][0], chains[key[0]][2], chains[key[0]][3]))
            return carry

        lax.fori_loop(0, SCAN_CHUNK // 8, eight_steps, 0)

    row_shape = SDS((b, t, 512), F32)
    state = (2 * npair, 64, 128)
    return pl.pallas_call(
        body, grid=(nc,),
        in_specs=sum(([rows[d], rows[d], vrows[d]] + [rows[d]] * 4 + [hist[d]] for d in (1, 0)), [])
        + [BS(state, lambda c: (0, 0, 0))],
        out_specs=[rows[1]] * 6 + [rows[0]] * 6, out_shape=[row_shape] * 12,
        scratch_shapes=[pltpu.VMEM(state, F32), pltpu.VMEM(state, F32)], compiler_params=_cp("arbitrary"), name="wkv_scan_bwd",
    )(zs, kk, zs, dy, *ops_f, hist_f, zs, kk, zs, dy, *ops_b, hist_b, s_last)


def _mla_fwd(zm, cs, sn, p, t):
    m = zm.shape[0]
    tr = 256
    per = t // tr
    params = [p["q_norm_g"], p["kv_norm_g"], p["wq"], p["wqr"], p["wk"], p["wv"]]

    def body(z_ref, cs_ref, sn_ref, gq, gkv, wq, wqr, wk, wv, q_ref, k_ref, v_ref):
        q, kf, v = _mla_fn(z_ref[...], cs_ref[...], sn_ref[...], gq[...], gkv[...], wq[...], wqr[...], wk[...], wv[...])
        q_ref[...] = q.astype(BF16)
        k_ref[...] = kf.astype(BF16)
        v_ref[...] = v.astype(BF16)

    tab = BS((tr, 128), lambda i: (i % per, 0))
    return pl.pallas_call(
        body, grid=(m // tr,), in_specs=[BS((tr, MLA_PAD_COLS), lambda i: (i, 0)), tab, tab] + [_const(q.shape) for q in params],
        out_specs=[BS((tr, 1024), lambda i: (i, 0)), BS((tr, 1024), lambda i: (i, 0)), BS((tr, 512), lambda i: (i, 0))],
        out_shape=[SDS((m, 1024), BF16), SDS((m, 1024), BF16), SDS((m, 512), BF16)], compiler_params=_cp("parallel"), name="mla_prep_fwd",
    )(zm, cs, sn, *params)


def _mla_bwd(zm, cs, sn, p, t, dq, dk, dv):
    m = zm.shape[0]
    tr = 128
    per = t // tr
    params = [p["q_norm_g"], p["kv_norm_g"], p["wq"], p["wqr"], p["wk"], p["wv"]]

    def body(z_ref, cs_ref, sn_ref, gq, gkv, wq, wqr, wk, wv, dq_ref, dk_ref, dv_ref, dz_ref, *grads):
        cs_v, sn_v = cs_ref[...], sn_ref[...]
        _, vjp = jax.vjp(
            lambda *q: _mla_fn(q[0], cs_v, sn_v, *q[1:]), z_ref[...], gq[...], gkv[...], wq[...], wqr[...], wk[...], wv[...]
        )
        dz, *dparams = vjp((dq_ref[...], dk_ref[...], dv_ref[...]))
        dz_ref[...] = dz

        @pl.when(pl.program_id(0) == 0)
        def _():
            for gr in grads:
                gr[...] = jnp.zeros_like(gr)

        for gr, val in zip(grads, dparams):
            gr[...] += val

    tab = BS((tr, 128), lambda i: (i % per, 0))
    wide = BS((tr, 1024), lambda i: (i, 0))
    return pl.pallas_call(
        body, grid=(m // tr,),
        in_specs=[BS((tr, MLA_PAD_COLS), lambda i: (i, 0)), tab, tab] + [_const(q.shape) for q in params]
        + [wide, wide, BS((tr, 512), lambda i: (i, 0))],
        out_specs=[BS((tr, MLA_PAD_COLS), lambda i: (i, 0))] + [_const(q.shape) for q in params],
        out_shape=[SDS((m, MLA_PAD_COLS), F32)] + [SDS(q.shape, F32) for q in params],
        compiler_params=_cp("arbitrary"), name="mla_prep_bwd",
    )(zm, cs, sn, *params, dq, dk, dv)


_NT = (((1,), (1,)), ((), ()))
_TN = (((0,), (0,)), ((), ()))


def _attn_fwd(q, kf, v, b, t):
    m = q.shape[0]
    tq = 256
    nq = t // tq

    def body(q_ref, k_ref, v_ref, o_ref, l_ref):
        lo = lax.broadcasted_iota(jnp.int32, (1, 128), 1) < 64
        v_all = v_ref[...]
        o = jnp.zeros((tq, 128), F32)
        lse = []
        for h in range(2):
            hs = slice(128 * h, 128 * h + 128)
            s = lax.dot_general(q_ref[:, hs], k_ref[:, hs], _NT, preferred_element_type=F32) * MLA_SCALE
            mx = jnp.max(s, axis=1, keepdims=True)
            e = jnp.exp(s - mx)
            den = jnp.sum(e, axis=1, keepdims=True)
            vh = jnp.where(lo if h == 0 else jnp.logical_not(lo), v_all, jnp.zeros_like(v_all))
            o = o + jnp.dot(e.astype(BF16), vh, preferred_element_type=F32) / den
            lse.append(mx + jnp.log(den))
        o_ref[...] = o
        l_ref[...] = jnp.where(lo, lse[0], lse[1])

    return pl.pallas_call(
        body, grid=(b, 4, nq),
        in_specs=[BS((tq, 256), lambda bi, hp, i: (bi * nq + i, hp)), BS((t, 256), lambda bi, hp, i: (bi, hp)),
                  BS((t, 128), lambda bi, hp, i: (bi, hp))],
        out_specs=[BS((tq, 128), lambda bi, hp, i: (bi * nq + i, hp))] * 2,
        out_shape=[SDS((m, 512), F32), SDS((m, 512), F32)], compiler_params=_cp("parallel", "parallel", "arbitrary"), name="attn_fwd",
    )(q, kf, v)


def _attn_bwd(q, kf, v, o, lse, do, b, t):
    m = q.shape[0]
    tq = 256
    nq = t // tq

    def body(q_ref, k_ref, v_ref, o_ref, l_ref, do_ref, dq_ref, dk_ref, dv_ref):
        lo = lax.broadcasted_iota(jnp.int32, (1, 128), 1) < 64

        @pl.when(pl.program_id(2) == 0)
        def _():
            dk_ref[...] = jnp.zeros_like(dk_ref)
            dv_ref[...] = jnp.zeros_like(dv_ref)

        v_all, o_all, l_all, do_all = v_ref[...], o_ref[...], l_ref[...], do_ref[...]
        dv_acc = jnp.zeros((t, 128), F32)
        for h in range(2):
            hs = slice(128 * h, 128 * h + 128)
            mask = lo if h == 0 else jnp.logical_not(lo)
            qh, kh = q_ref[:, hs], k_ref[:, hs]
            s = lax.dot_general(qh, kh, _NT, preferred_element_type=F32) * MLA_SCALE
            lse_h = jnp.max(jnp.where(mask, l_all, -jnp.inf), axis=1, keepdims=True)
            pr = jnp.exp(s - lse_h)
            do_h = jnp.where(mask, do_all, 0.0)
            dp = lax.dot_general(do_h.astype(BF16), v_all, _NT, preferred_element_type=F32)
            dsum = jnp.sum(do_h * o_all, axis=1, keepdims=True)
            ds = (pr * (dp - dsum) * MLA_SCALE).astype(BF16)
            dq_ref[:, hs] = jnp.dot(ds, kh, preferred_element_type=F32)
            dk_ref[:, hs] += lax.dot_general(ds, qh, _TN, preferred_element_type=F32)
            dv_acc = dv_acc + lax.dot_general(pr.astype(BF16), do_h.astype(BF16), _TN, preferred_element_type=F32)
        dv_ref[...] += dv_acc

    qspec = BS((tq, 256), lambda bi, hp, i: (bi * nq + i, hp))
    kspec = BS((t, 256), lambda bi, hp, i: (bi, hp))
    vspec = BS((t, 128), lambda bi, hp, i: (bi, hp))
    ospec = BS((tq, 128), lambda bi, hp, i: (bi * nq + i, hp))
    return pl.pallas_call(
        body, grid=(b, 4, nq), in_specs=[qspec, kspec, vspec, ospec, ospec, ospec], out_specs=[qspec, kspec, vspec],
        out_shape=[SDS((m, 1024), F32), SDS((m, 1024), F32), SDS((m, 512), F32)],
        compiler_params=_cp("parallel", "parallel", "arbitrary"), name="attn_bwd",
    )(q, kf, v, o, lse, do)


def _conv3(u, w_ref, b_ref, t):
    up, un = _prev_next(u, t)
    return w_ref[0:1, :] * up + w_ref[1:2, :] * u + w_ref[2:3, :] * un + b_ref[...], up, un


def _ffn_mid_fwd(ug3, uv3, cw, cb):
    b, t, f = ug3.shape
    nc = f // 256

    def body(ug_ref, uv_ref, wg_ref, wv_ref, bg_ref, bv_ref, a_ref):
        gc, _, _ = _conv3(ug_ref[0], wg_ref, bg_ref, t)
        vc, _, _ = _conv3(uv_ref[0], wv_ref, bv_ref, t)
        a_ref[0] = (gc * jax.nn.sigmoid(gc) * vc).astype(BF16)

    blk = BS((1, t, 256), lambda i, j: (i, 0, j))
    return pl.pallas_call(
        body, grid=(b, nc),
        in_specs=[blk, blk, BS((3, 256), lambda i, j: (0, j)), BS((3, 256), lambda i, j: (0, j + nc)),
                  BS((1, 256), lambda i, j: (0, j)), BS((1, 256), lambda i, j: (0, j + nc))],
        out_specs=blk, out_shape=SDS((b, t, f), BF16), compiler_params=_cp("parallel", "parallel"), name="ffn_mid_fwd",
    )(ug3, uv3, cw, cw, cb, cb)


def _ffn_mid_bwd(ug3, uv3, cw, cb, da3):
    b, t, f = ug3.shape
    nc = f // 256

    def half(u, up, un, dc, w_ref):
        dprev, dnext = _prev_next(dc, t)
        du = w_ref[1:2, :] * dc + w_ref[0:1, :] * dnext + w_ref[2:3, :] * dprev
        sums = [jnp.sum(dc * q, axis=0, keepdims=True) for q in (up, u, un)] + [jnp.sum(dc, axis=0, keepdims=True)]
        row = lax.broadcasted_iota(jnp.int32, (8, 256), 0)
        tab = jnp.zeros((8, 256), F32)
        for i, s in enumerate(sums):
            tab = jnp.where(row == i, s, tab)
        return du, tab

    def body(ug_ref, uv_ref, wg_ref, wv_ref, bg_ref, bv_ref, da_ref, dug_ref, duv_ref, tg_ref, tv_ref):
        ug, uv, da = ug_ref[0], uv_ref[0], da_ref[0]
        gc, gp, gn = _conv3(ug, wg_ref, bg_ref, t)
        vc, vp, vn = _conv3(uv, wv_ref, bv_ref, t)
        sg = jax.nn.sigmoid(gc)
        d_gc = da * vc * (sg * (1.0 + gc * (1.0 - sg)))
        d_vc = da * (gc * sg)
        dug, tg = half(ug, gp, gn, d_gc, wg_ref)
        duv, tv = half(uv, vp, vn, d_vc, wv_ref)
        dug_ref[0] = dug
        duv_ref[0] = duv

        @pl.when(pl.program_id(1) == 0)
        def _():
            tg_ref[...] = jnp.zeros_like(tg_ref)
            tv_ref[...] = jnp.zeros_like(tv_ref)

        tg_ref[...] += tg
        tv_ref[...] += tv

    blk = BS((1, t, 256), lambda j, i: (i, 0, j))
    tab = BS((8, 256), lambda j, i: (0, j))
    return pl.pallas_call(
        body, grid=(nc, b),
        in_specs=[blk, blk, BS((3, 256), lambda j, i: (0, j)), BS((3, 256), lambda j, i: (0, j + nc)),
                  BS((1, 256), lambda j, i: (0, j)), BS((1, 256), lambda j, i: (0, j + nc)), blk],
        out_specs=[blk, blk, tab, tab],
        out_shape=[SDS((b, t, f), F32), SDS((b, t, f), F32), SDS((8, f), F32), SDS((8, f), F32)],
        compiler_params=_cp("parallel", "arbitrary"), name="ffn_mid_bwd",
    )(ug3, uv3, cw, cw, cb, cb, da3)


def _add_rows(parts, name, out_dtype=F32):
    r = parts[0].shape[0]
    tr = _row_tile(r, 1024)
    n = len(parts)

    def body(*refs):
        acc = refs[0][...].astype(F32)
        for q in refs[1:n]:
            acc = acc + q[...].astype(F32)
        refs[n][...] = acc.astype(out_dtype)

    row = BS((tr, 128), lambda i: (i, 0))
    return pl.pallas_call(
        body, grid=(r // tr,), in_specs=[row] * n, out_specs=row, out_shape=SDS((r, 128), out_dtype),
        compiler_params=_cp("parallel"), name=name,
    )(*parts)


def _adamw(w, g, m, v, name):
    r, c = w.shape
    tr = _row_tile(r)

    def body(w_ref, g_ref, m_ref, v_ref, d_ref, m2_ref, v2_ref):
        d, m2, v2 = _adamw_math(w_ref[...], g_ref[...], m_ref[...], v_ref[...])
        d_ref[...] = d
        m2_ref[...] = m2
        v2_ref[...] = v2

    blk = BS((tr, c), lambda i: (i, 0))
    return pl.pallas_call(
        body, grid=(r // tr,), in_specs=[blk] * 4, out_specs=[blk] * 3, out_shape=[SDS((r, c), F32)] * 3,
        compiler_params=_cp("parallel"), name=name,
    )(w, g, m, v)


def _place():
    return lax.axis_index("x"), lax.axis_index("y"), lax.axis_index("c")


def _flip(v, bit):
    return 1 - v if bit else v


def _allgather_weights(shard):
    r = shard.shape[0]
    rh = r // 2

    def body(x_ref, out_ref, send_sems, recv_sems):
        x, y, c = _place()
        me, sibling = (x, y, c), (x, y, 1 - c)
        chips = [(1 - x, y), (x, 1 - y), (1 - x, 1 - y)]
        mine_src = x_ref.at[pl.ds(c * rh, rh), :]

        def rows(px, py, pc):
            return out_ref.at[pl.ds((4 * px + 2 * py + pc) * rh, rh), :]

        def copy(k, block, to, src=None):
            return pltpu.make_async_remote_copy(
                src_ref=rows(*block) if src is None else src, dst_ref=rows(*block), send_sem=send_sems.at[k],
                recv_sem=recv_sems.at[k], device_id=to, device_id_type=MESH,
            )

        first = [copy(j, me, (*chip, c), src=mine_src) for j, chip in enumerate(chips)]
        for cp in first:
            cp.start()
        passed = [copy(3 + j, (*chip, c), sibling) for j, chip in enumerate(chips)]
        for j, chip in enumerate(chips):
            copy(j, (*chip, c), me).wait_recv()
            passed[j].start()
        for j, chip in enumerate(chips):
            copy(3 + j, (*chip, 1 - c), me).wait_recv()
        for cp in first + passed:
            cp.wait_send()

    return pl.pallas_call(
        body, out_shape=SDS((8 * rh, 128), shard.dtype), in_specs=[ANY], out_specs=ANY,
        scratch_shapes=[pltpu.SemaphoreType.DMA((6,)), pltpu.SemaphoreType.DMA((6,))], name="allgather_weights",
    )(shard)


def _scatter_to_chips(g):
    def body(g_ref, recv_ref, send_sems, recv_sems):
        x, y, c = _place()
        copies = []
        for j, (fx, fy) in enumerate(((1, 0), (0, 1), (1, 1))):
            px, py = _flip(x, fx), _flip(y, fy)
            cp = pltpu.make_async_remote_copy(
                src_ref=g_ref.at[2 * px + py], dst_ref=recv_ref.at[j], send_sem=send_sems.at[j], recv_sem=recv_sems.at[j],
                device_id=(px, py, c), device_id_type=MESH,
            )
            cp.start()
            copies.append(cp)
        for cp in copies:
            cp.wait_recv()
        for cp in copies:
            cp.wait_send()

    return pl.pallas_call(
        body, out_shape=SDS((3,) + g.shape[1:], g.dtype), in_specs=[ANY], out_specs=ANY,
        scratch_shapes=[pltpu.SemaphoreType.DMA((3,)), pltpu.SemaphoreType.DMA((3,))], name="scatter_grads",
    )(g)


def _send_to_sibling(a, half_of_rows):
    rh = a.shape[1] // 2

    def body(a_ref, b_ref, send_sem, recv_sem):
        x, y, c = _place()
        src = a_ref.at[:, pl.ds((1 - c) * rh, rh), :] if half_of_rows else a_ref
        cp = pltpu.make_async_remote_copy(
            src_ref=src, dst_ref=b_ref, send_sem=send_sem, recv_sem=recv_sem, device_id=(x, y, 1 - c), device_id_type=MESH
        )
        cp.start()
        cp.wait()

    shape = (a.shape[0], rh, 128) if half_of_rows else a.shape
    return pl.pallas_call(
        body, out_shape=SDS(shape, a.dtype), in_specs=[ANY], out_specs=ANY,
        scratch_shapes=[pltpu.SemaphoreType.DMA, pltpu.SemaphoreType.DMA],
        name="sibling_halves" if half_of_rows else "sibling_swap",
    )(a)


def _allreduce_small(v):
    r = v.shape[0]

    def body(v_ref, out_ref, buf_ref, send_sems, recv_sems):
        x, y, c = _place()
        buf_ref[0] = v_ref[...]
        copies = []
        for k in range(1, 8):
            peer = (_flip(x, k >> 2 & 1), _flip(y, k >> 1 & 1), _flip(c, k & 1))
            cp = pltpu.make_async_remote_copy(
                src_ref=v_ref, dst_ref=buf_ref.at[k], send_sem=send_sems.at[k - 1], recv_sem=recv_sems.at[k - 1],
                device_id=peer, device_id_type=MESH,
            )
            cp.start()
            copies.append(cp)
        for cp in copies:
            cp.wait_recv()
        acc = None
        for d in range(8):
            slot = 4 * _flip(x, d >> 2 & 1) + 2 * _flip(y, d >> 1 & 1) + _flip(c, d & 1)
            term = buf_ref[slot]
            acc = term if acc is None else acc + term
        out_ref[...] = acc
        for cp in copies:
            cp.wait_send()

    return pl.pallas_call(
        body, out_shape=SDS(v.shape, F32), in_specs=[VMEM], out_specs=VMEM,
        scratch_shapes=[pltpu.VMEM((8, r, 128), F32), pltpu.SemaphoreType.DMA((7,)), pltpu.SemaphoreType.DMA((7,))],
        name="allreduce_small",
    )(v)


_BIG = (
    ("w_in", 1, False), ("decay_w2_fwd", 1, False), ("decay_w2_bwd", 1, False), ("iclr_a2_fwd", 1, False),
    ("iclr_a2_bwd", 1, False), ("gate_g2", 1, False), ("w_uq", 0, False), ("w_ukv", 1, False), ("w_out", 0, False),
    ("w_ffn_up", 1, False), ("ffn_conv_w", 1, True), ("w_ffn_down", 0, False),
)
_SMALL = (
    "ln_mix_g", "shift_mu_prev", "shift_mu_next", "decay_w0_fwd", "decay_w0_bwd", "iclr_a0_fwd", "iclr_a0_bwd", "k_k",
    "k_a", "r_k", "ln_x_g", "ln_x_b", "q_norm_g", "kv_norm_g", "mla_out_g", "ln_ffn_g", "ffn_conv_b", "ln_final_g",
)
_WEIGHTS = (
    "ln_mix_g", "w_in", "shift_mu_prev", "shift_mu_next", "decay_w0_fwd", "decay_w2_fwd", "decay_w0_bwd", "decay_w2_bwd",
    "iclr_a0_fwd", "iclr_a2_fwd", "iclr_a0_bwd", "iclr_a2_bwd", "gate_g2", "k_k", "k_a", "r_k", "ln_x_g", "ln_x_b",
    "q_norm_g", "w_uq", "kv_norm_g", "w_ukv", "mla_out_g", "w_out", "ln_ffn_g", "w_ffn_up", "ffn_conv_w", "ffn_conv_b",
    "w_ffn_down", "ln_final_g",
)


def _pad_rows(flat, rows):
    return jnp.pad(flat, (0, rows * 128 - flat.shape[0])).reshape(rows, 128)


def _rows_for(n, mult):
    rows = -(-n // 128)
    return -(-rows // mult) * mult


def _pack_shards_bf16(arrs):
    parts = []
    for name, _, raw in _BIG:
        w = arrs[name][0]
        flat = lax.bitcast_convert_type(w, BF16).reshape(-1) if raw else w.astype(BF16).reshape(-1)
        parts.append(_pad_rows(flat, _rows_for(flat.shape[0], 32)))
    return jnp.concatenate(parts, axis=0)


def _unpack_gathered(g4, arrs):
    out, off = {}, 0
    for name, axis, raw in _BIG:
        a, b = arrs[name].shape[1:]
        n = a * b * (2 if raw else 1)
        rows = _rows_for(n, 32)
        seg = g4[:, off:off + rows].reshape(4, rows * 128)[:, :n]
        off += rows
        if raw:
            seg = lax.bitcast_convert_type(seg.reshape(4, a * b, 2), F32)
        seg = seg.reshape(4, a, b)
        out[name] = jnp.concatenate([seg[s] for s in range(4)], axis=1) if axis == 1 else seg.reshape(4 * a, b)
    return out


def _pack_grads(full, arrs):
    parts = []
    for name, axis, _ in _BIG:
        a, b = arrs[name].shape[1:]
        g = full[name]
        sh = g.reshape(a, 4, b).transpose(1, 0, 2) if axis == 1 else g.reshape(4, a, b)
        rows = _rows_for(a * b, 8)
        parts.append(jnp.pad(sh.reshape(4, a * b), ((0, 0), (0, rows * 128 - a * b))).reshape(4, rows, 128))
    packed = jnp.concatenate(parts, axis=1)
    total = packed.shape[1]
    return jnp.pad(packed, ((0, 0), (0, -(-total // 1024) * 1024 - total), (0, 0)))


def _unpack_grads(g, arrs):
    out, off = {}, 0
    for name, _, _ in _BIG:
        a, b = arrs[name].shape[1:]
        rows = _rows_for(a * b, 8)
        out[name] = g[off:off + rows].reshape(-1)[:a * b].reshape(a, b)
        off += rows
    return out


def _pack_small(vals):
    flat = jnp.concatenate([vals[n].reshape(-1).astype(F32) for n in _SMALL] + [vals["_loss"].reshape(-1)])
    return _pad_rows(flat, _rows_for(flat.shape[0], 8))


def _unpack_small(buf, arrs):
    flat, out, off = buf.reshape(-1), {}, 0
    for n in _SMALL:
        size = arrs[n].size
        out[n] = flat[off:off + size].reshape(arrs[n].shape)
        off += size
    out["_loss"] = flat[off]
    return out


def _rot_cols(w):
    return jnp.concatenate([-w[..., 16:], w[..., :16]], axis=-1)


def _rot_cols_t(g):
    return jnp.concatenate([g[..., 16:], -g[..., :16]], axis=-1)


def _rope_tables(t):
    inv = jnp.power(ROPE_THETA, -jnp.arange(0, ROPE_DIM, 2, dtype=F32) / ROPE_DIM)
    ang = jnp.arange(t, dtype=F32)[:, None] * inv[None, :]
    one, zero = jnp.ones((t, 64), F32), jnp.zeros((t, 64), F32)
    cs = jnp.concatenate([one, jnp.cos(ang), jnp.cos(ang), zero[:, :32]], axis=1)
    sn = jnp.concatenate([zero, jnp.sin(ang), jnp.sin(ang), zero[:, :32]], axis=1)
    return cs, sn


def _block_diag(a, b):
    za = jnp.zeros_like(a)
    return jnp.concatenate([jnp.concatenate([a, za], axis=1), jnp.concatenate([za, b], axis=1)], axis=0)


def kernel(x, ln_mix_g, w_in, shift_mu_prev, shift_mu_next, decay_w0_fwd, decay_w2_fwd, decay_w0_bwd, decay_w2_bwd, iclr_a0_fwd, iclr_a2_fwd, iclr_a0_bwd, iclr_a2_bwd, gate_g2, k_k, k_a, r_k, ln_x_g, ln_x_b, q_norm_g, w_uq, kv_norm_g, w_ukv, mla_out_g, w_out, ln_ffn_g, w_ffn_up, ffn_conv_w, ffn_conv_b, w_ffn_down, ln_final_g, loss_target, m_ln_mix_g, m_w_in, m_shift_mu_prev, m_shift_mu_next, m_decay_w0_fwd, m_decay_w2_fwd, m_decay_w0_bwd, m_decay_w2_bwd, m_iclr_a0_fwd, m_iclr_a2_fwd, m_iclr_a0_bwd, m_iclr_a2_bwd, m_gate_g2, m_k_k, m_k_a, m_r_k, m_ln_x_g, m_ln_x_b, m_q_norm_g, m_w_uq, m_kv_norm_g, m_w_ukv, m_mla_out_g, m_w_out, m_ln_ffn_g, m_w_ffn_up, m_ffn_conv_w, m_ffn_conv_b, m_w_ffn_down, m_ln_final_g, v_ln_mix_g, v_w_in, v_shift_mu_prev, v_shift_mu_next, v_decay_w0_fwd, v_decay_w2_fwd, v_decay_w0_bwd, v_decay_w2_bwd, v_iclr_a0_fwd, v_iclr_a2_fwd, v_iclr_a0_bwd, v_iclr_a2_bwd, v_gate_g2, v_k_k, v_k_a, v_r_k, v_ln_x_g, v_ln_x_b, v_q_norm_g, v_w_uq, v_kv_norm_g, v_w_ukv, v_mla_out_g, v_w_out, v_ln_ffn_g, v_w_ffn_up, v_ffn_conv_w, v_ffn_conv_b, v_w_ffn_down, v_ln_final_g):
    arrs = dict(locals())
    b, t, d = x.shape
    m = b * t
    x2 = x.reshape(m, d)
    tgt = loss_target.reshape(m, d)
    vec = lambda n: arrs[n].reshape(1, -1)

    core = lax.axis_index("c")
    chip = 2 * lax.axis_index("x") + lax.axis_index("y")
    my_shard = _pack_shards_bf16(arrs)
    gathered = _allgather_weights(my_shard).reshape(N_CHIPS, -1, 128)
    gathered = lax.dynamic_update_index_in_dim(gathered, my_shard, chip, axis=0)
    fw = _unpack_gathered(gathered, arrs)
    win = fw["w_in"]
    zc = jnp.zeros((d, 64), BF16)
    w_kr = win[:, 2944:2976]
    win_m = jnp.concatenate([win[:, 1920:2944], zc, w_kr, zc[:, :32], zc, _rot_cols(w_kr), zc[:, :32]], axis=1)
    win_r = win[:, :RWKV_COLS]
    uq = fw["w_uq"].astype(F32).reshape(Q_RANK, HEADS, 96)
    z32 = jnp.zeros((Q_RANK, HEADS, 32), F32)
    wq = jnp.concatenate([uq[..., :64], uq[..., 64:], z32], axis=-1).reshape(Q_RANK, 1024)
    wqr = jnp.concatenate([z32, z32, _rot_cols(uq[..., 64:]), z32], axis=-1).reshape(Q_RANK, 1024)
    ukv = fw["w_ukv"].astype(F32).reshape(KV_RANK, HEADS, 128)
    wk = jnp.concatenate([ukv[..., :64], jnp.zeros_like(ukv[..., :64])], axis=-1).reshape(KV_RANK, 1024)
    wv = ukv[..., 64:].reshape(KV_RANK, 512)
    head = jnp.arange(512) // HEAD_DIM
    rw = dict(
        w0=jnp.concatenate([vec("decay_w0_fwd"), vec("decay_w0_bwd")], axis=1),
        w2=_block_diag(fw["decay_w2_fwd"], fw["decay_w2_bwd"]).astype(F32),
        a0=jnp.concatenate([vec("iclr_a0_fwd"), vec("iclr_a0_bwd")], axis=1),
        a2=_block_diag(fw["iclr_a2_fwd"], fw["iclr_a2_bwd"]).astype(F32),
        g2=fw["gate_g2"].astype(F32), k_k=vec("k_k"), k_a=vec("k_a"), r_k=vec("r_k"), ln_x_g=vec("ln_x_g"), ln_x_b=vec("ln_x_b"),
        ones_bd=(head[:, None] == head[None, :]).astype(F32),
    )
    mp = dict(q_norm_g=vec("q_norm_g"), kv_norm_g=vec("kv_norm_g"), wq=wq, wqr=wqr, wk=wk, wv=wv)
    cs, sn = _rope_tables(t)
    w_up_g, w_up_v = fw["w_ffn_up"][:, :D_FF], fw["w_ffn_up"][:, D_FF:]
    cw, cb = fw["ffn_conv_w"], vec("ffn_conv_b")

    n1 = _rms_fwd(x2, vec("ln_mix_g"), "rms_mix")
    zm = _mm(n1, win_m, "nn", "proj_in_mla")
    zr = _mm(n1, win_r, "nn", "proj_in_rwkv")
    zs = _shift_fwd(zr.reshape(b, t, RWKV_COLS), vec("shift_mu_prev"), vec("shift_mu_next"))
    zs2 = zs.reshape(m, RWKV_COLS)
    wf, wb, kf, kb, kk, kaf, kab, gate = _prep_fwd(zs2, rw)
    r4 = lambda a: a.reshape(b, t, 512)
    f2 = lambda a: a.reshape(m, 512)
    kk4 = r4(kk)
    ops_f = (r4(wf), r4(kf), r4(kaf))
    ops_b = (r4(wb), r4(kb), r4(kab))
    y_f, hist_f, y_b, hist_b, s_last = _scan_fwd(zs, kk4, ops_f, ops_b)
    y_f, y_b = f2(y_f), f2(y_b)
    y_rwkv = _post_fwd(y_f, y_b, zs2, kf, kb, gate, rw)
    q, kfull, v = _mla_fwd(zm, cs, sn, mp, t)
    o, lse = _attn_fwd(q, kfull, v, b, t)
    y_mla = _rms_fwd(o, vec("mla_out_g"), "rms_mla_out")
    ymix = jnp.concatenate([y_rwkv, y_mla], axis=1)
    h1 = _mm(ymix, fw["w_out"], "nn", "proj_out", add=x2)
    n2 = _rms_fwd(h1, vec("ln_ffn_g"), "rms_ffn")
    ug = _mm(n2, w_up_g, "nn", "ffn_up_gate")
    uv = _mm(n2, w_up_v, "nn", "ffn_up_val")
    r3f = lambda a: a.reshape(b, t, D_FF)
    act = _ffn_mid_fwd(r3f(ug), r3f(uv), cw, cb).reshape(m, D_FF)
    h2 = _mm(act, fw["w_ffn_down"], "nn", "ffn_down", add=h1)
    loss_tab, dh2, g_ln_final = _final(h2, vec("ln_final_g"), tgt)

    gfull = {}
    dact = _mm(dh2, fw["w_ffn_down"], "nt", "d_ffn_act")
    gfull["w_ffn_down"] = _mm(act, dh2, "tn", "g_ffn_down")
    dug, duv, tab_g, tab_v = _ffn_mid_bwd(r3f(ug), r3f(uv), cw, cb, r3f(dact))
    dug, duv = dug.reshape(m, D_FF), duv.reshape(m, D_FF)
    gfull["ffn_conv_w"] = jnp.concatenate([tab_g[0:3], tab_v[0:3]], axis=1)
    g_conv_b = jnp.concatenate([tab_g[3:4], tab_v[3:4]], axis=1)
    dn2 = _mm(duv, w_up_v, "nt", "d_ffn_in_val", add=_mm(dug, w_up_g, "nt", "d_ffn_in_gate"))
    gfull["w_ffn_up"] = jnp.concatenate([_mm(n2, dug, "tn", "g_ffn_up_gate"), _mm(n2, duv, "tn", "g_ffn_up_val")], axis=1)
    dh1, g_ln_ffn = _rms_bwd(h1, vec("ln_ffn_g"), dn2, "rms_ffn_bwd", dres=dh2)
    dymix = _mm(dh1, fw["w_out"], "nt", "d_mix")
    gfull["w_out"] = _mm(ymix, dh1, "tn", "g_w_out")
    do, g_mla_out = _rms_bwd(o, vec("mla_out_g"), dymix, "rms_mla_out_bwd", dy_block=1)
    dq, dk, dv = _attn_bwd(q, kfull, v, o, lse, do, b, t)
    dzm, g_qn, g_kvn, g_wq, g_wqr, g_wk, g_wv = _mla_bwd(zm, cs, sn, mp, t, dq, dk, dv)
    gq3, gqr3 = g_wq.reshape(Q_RANK, HEADS, 128), g_wqr.reshape(Q_RANK, HEADS, 128)
    gfull["w_uq"] = jnp.concatenate(
        [gq3[..., :64], gq3[..., 64:96] + _rot_cols_t(gqr3[..., 64:96])], axis=-1
    ).reshape(Q_RANK, HEADS * 96)
    gfull["w_ukv"] = jnp.concatenate(
        [g_wk.reshape(KV_RANK, HEADS, 128)[..., :64], g_wv.reshape(KV_RANK, HEADS, 64)], axis=-1
    ).reshape(KV_RANK, 1024)
    dys, dr_p, dk_p, dv_p, dgate, g_rk, g_lnx_g, g_lnx_b = _post_bwd(y_f, y_b, zs2, kf, kb, gate, rw, dymix)
    (dr_f, dwf, dkf, dkk_f, dkaf, dv_f, dr_b, dwb, dkb, dkk_b, dkab, dv_b) = _scan_bwd(
        zs, kk4, r4(dys), ops_f, hist_f, ops_b, hist_b, s_last)
    cts = dict(dwf=f2(dwf), dwb=f2(dwb), dkf=f2(dkf), dkb=f2(dkb), dkk_f=f2(dkk_f), dkk_b=f2(dkk_b), dkaf=f2(dkaf), dkab=f2(dkab),
               dr_f=f2(dr_f), dr_b=f2(dr_b), dr_p=dr_p, dk_p=dk_p, dv_p=dv_p, dg=dgate, dv_f=f2(dv_f), dv_b=f2(dv_b))
    dzs, g_w0, g_w2, g_a0, g_a2, g_g2, g_kk, g_ka = _prep_bwd(zs2, rw, cts)
    dzr, g_mu_p, g_mu_n = _shift_bwd(dzs.reshape(b, t, RWKV_COLS), zr.reshape(b, t, RWKV_COLS), vec("shift_mu_prev"), vec("shift_mu_next"))
    dzr = dzr.reshape(m, RWKV_COLS)
    gfull["decay_w2_fwd"], gfull["decay_w2_bwd"] = g_w2[:64, :512], g_w2[64:, 512:]
    gfull["iclr_a2_fwd"], gfull["iclr_a2_bwd"] = g_a2[:64, :512], g_a2[64:, 512:]
    gfull["gate_g2"] = g_g2
    dn1 = _mm(dzr, win_r, "nt", "d_proj_in_rwkv", add=_mm(dzm, win_m, "nt", "d_proj_in_mla"))
    g_m = _mm(n1, dzm, "tn", "g_w_in_mla")
    g_r = _mm(n1, dzr, "tn", "g_w_in_rwkv")
    g_kr = g_m[:, 1088:1120] + _rot_cols_t(g_m[:, 1216:1248])
    gfull["w_in"] = jnp.concatenate([g_r, g_m[:, :1024], g_kr], axis=1)
    dx, g_ln_mix = _rms_bwd(x2, vec("ln_mix_g"), dn1, "rms_mix_bwd", dres=dh1)

    packed = _pack_grads(gfull, arrs)
    rh = packed.shape[1] // 2
    own = lax.dynamic_slice_in_dim(packed, core * rh, rh, axis=1)
    sib = _send_to_sibling(packed, True)
    chip_part = _add_rows([own.reshape(4 * rh, 128), sib.reshape(4 * rh, 128)], "sum_cores", BF16).reshape(4, rh, 128)
    recv = _scatter_to_chips(chip_part)
    mine = lax.dynamic_index_in_dim(chip_part, chip, axis=0, keepdims=False)
    half = _add_rows([mine, recv[0], recv[1], recv[2]], "sum_chips")
    other = _send_to_sibling(half, False)
    lower = jnp.where(core == 0, half, other)
    upper = jnp.where(core == 0, other, half)
    g_big = _unpack_grads(jnp.concatenate([lower, upper], axis=0), arrs)
    small = {
        "ln_mix_g": g_ln_mix, "shift_mu_prev": g_mu_p, "shift_mu_next": g_mu_n, "decay_w0_fwd": g_w0[:, :512],
        "decay_w0_bwd": g_w0[:, 512:], "iclr_a0_fwd": g_a0[:, :512], "iclr_a0_bwd": g_a0[:, 512:], "k_k": g_kk, "k_a": g_ka,
        "r_k": g_rk, "ln_x_g": g_lnx_g, "ln_x_b": g_lnx_b, "q_norm_g": g_qn, "kv_norm_g": g_kvn, "mla_out_g": g_mla_out,
        "ln_ffn_g": g_ln_ffn, "ffn_conv_b": g_conv_b, "ln_final_g": g_ln_final,
        "_loss": jnp.pad(loss_tab[0, 0:1], (0, 127)),
    }
    g_small_buf = _allreduce_small(_pack_small(small))
    g_small = _unpack_small(g_small_buf, arrs)

    grads, deltas, new_m, new_v = {}, {}, {}, {}
    for name, _, _ in _BIG:
        shape = arrs[name].shape
        two = lambda a: a.reshape(shape[1:])
        dlt, m2, v2 = _adamw(two(arrs[name]), g_big[name], two(arrs["m_" + name]), two(arrs["v_" + name]), "adamw_" + name)
        grads[name] = g_big[name].reshape(shape)
        deltas[name], new_m[name], new_v[name] = dlt.reshape(shape), m2.reshape(shape), v2.reshape(shape)
    pk = lambda pre: _pack_small({**{n: arrs[pre + n] for n in _SMALL}, "_loss": jnp.zeros((128,), F32)})
    sd, sm, sv = _adamw(pk(""), g_small_buf, pk("m_"), pk("v_"), "adamw_small")
    sd, sm, sv = _unpack_small(sd, arrs), _unpack_small(sm, arrs), _unpack_small(sv, arrs)
    for n in _SMALL:
        grads[n], deltas[n], new_m[n], new_v[n] = g_small[n], sd[n], sm[n], sv[n]

    return (g_small["_loss"], dx.reshape(b, t, d), *[grads[n] for n in _WEIGHTS], *[deltas[n] for n in _WEIGHTS],
            *[new_m[n] for n in _WEIGHTS], *[new_v[n] for n in _WEIGHTS])
```

```python
import functools
import math

import jax
import jax.numpy as jnp
from jax import lax
from jax.experimental import pallas as pl
from jax.experimental.pallas import tpu as pltpu

F32, BF16 = jnp.float32, jnp.bfloat16
MESH = pl.DeviceIdType.MESH
ANY = pl.BlockSpec(memory_space=pl.ANY)
VMEM = pl.BlockSpec(memory_space=pltpu.VMEM)
BS = pl.BlockSpec
SDS = jax.ShapeDtypeStruct

NORM_EPS = 1e-6
GN_EPS = 64e-5
L2_EPS = 1e-12
HEADS = 8
HEAD_DIM = 64
D_RWKV = HEADS * HEAD_DIM
ROPE_DIM = 32
ROPE_THETA = 10000.0
MLA_SCALE = (64 + ROPE_DIM) ** -0.5
Q_RANK, KV_RANK = 768, 256
RWKV_COLS = 1920
MLA_PAD_COLS = Q_RANK + KV_RANK + 256
D_FF = 2816
ADAM_LR, ADAM_B1, ADAM_B2, ADAM_EPS, ADAM_WD, ADAM_STEP = 0.001, 0.9, 0.999, 1e-08, 0.01, 10

V7X_LANES = 128
V7X_VMEM_LIMIT = 48 * 1024 * 1024
SCAN_CHUNK = 16
N_CHIPS = 4


def _cp(*sem):
    return pltpu.CompilerParams(dimension_semantics=sem, vmem_limit_bytes=V7X_VMEM_LIMIT)


def _tile(n, cands=(512, 640, 384, 256, 128)):
    for c in cands:
        if n % c == 0:
            return c
    return n


def _row_tile(n, cap=256):
    best = n
    for t in range(8, cap + 1, 8):
        if n % t == 0:
            best = t
    return best if best <= cap or n <= cap else n


def _rms(x, g):
    ms = jnp.mean(x * x, axis=-1, keepdims=True)
    return x * lax.rsqrt(ms + NORM_EPS) * g


@jax.custom_vjp
def _bdot(x, w):
    return jnp.dot(x.astype(BF16), w.astype(BF16), preferred_element_type=F32)


def _bdot_fwd(x, w):
    return _bdot(x, w), (x, w)


def _bdot_bwd(res, ct):
    x, w = res
    c = ct.astype(BF16)
    dx = lax.dot_general(c, w.astype(BF16), (((1,), (1,)), ((), ())), preferred_element_type=F32)
    dw = lax.dot_general(x.astype(BF16), c, (((0,), (0,)), ((), ())), preferred_element_type=F32)
    return dx.astype(x.dtype), dw.astype(w.dtype)


_bdot.defvjp(_bdot_fwd, _bdot_bwd)


@jax.custom_vjp
def _headsum(x, ones_bd):
    hi = x.astype(BF16)
    mid = (x - hi.astype(F32)).astype(BF16)
    ob = ones_bd.astype(BF16)
    return jnp.dot(hi, ob, preferred_element_type=F32) + jnp.dot(mid, ob, preferred_element_type=F32)


def _headsum_fwd(x, ones_bd):
    return _headsum(x, ones_bd), ones_bd


def _headsum_bwd(ones_bd, ct):
    return _headsum(ct, ones_bd), jnp.zeros_like(ones_bd)


_headsum.defvjp(_headsum_fwd, _headsum_bwd)


def _prep_fn(zs, w0, w2, a0, a2, g2, k_k, k_a, ones_bd):
    k = zs[:, 512:1024]
    wd = zs[:, 1536:1664]
    ad = zs[:, 1664:1792]
    gd = zs[:, 1792:1920]
    logit = w0 + _bdot(jnp.tanh(wd), w2)
    w = jnp.exp(-math.exp(-0.5) * jax.nn.sigmoid(logit))
    a = jax.nn.sigmoid(a0 + _bdot(ad, a2))
    g = _bdot(jax.nn.sigmoid(gd), g2)
    kkr = k * k_k
    nrm = jnp.sqrt(_headsum(kkr * kkr, ones_bd))
    kk = kkr / jnp.maximum(nrm, L2_EPS)
    a_f, a_b = a[:, :512], a[:, 512:]
    kf = k * (1.0 + (a_f - 1.0) * k_a)
    kb = k * (1.0 + (a_b - 1.0) * k_a)
    return w[:, :512], w[:, 512:], kf, kb, kk, kk * a_f, kk * a_b, g


def _post_fn(y, r, kf, kb, v, g, r_k, ln_g, ln_b, ones_bd):
    mu =_headsum(y, ones_bd) * (1.0 / HEAD_DIM)
    yc = y - mu
    var = _headsum(yc * yc, ones_bd) * (1.0 / HEAD_DIM)
    yn = yc * lax.rsqrt(var + GN_EPS) * ln_g + ln_b
    bonus = _headsum(r * (kf + kb) * r_k, ones_bd) * v
    return (yn + bonus) * g


def _cat8(x):
    return jnp.concatenate([x] * HEADS, axis=1)


def _mla_fn(zm, cs, sn, gq, gkv, wq, wqr, wk, wv):
    cq = zm[:, :Q_RANK]
    ckv = zm[:, Q_RANK:Q_RANK + KV_RANK]
    kr = zm[:, Q_RANK + KV_RANK:Q_RANK + KV_RANK + 128]
    krr = zm[:, Q_RANK + KV_RANK + 128:]
    cqn = _rms(cq, gq)
    ckvn = _rms(ckv, gkv)
    q = _bdot(cqn, wq) * _cat8(cs) + _bdot(cqn, wqr) * _cat8(sn)
    kro = kr * cs + krr * sn
    kfull = _bdot(ckvn, wk) + _cat8(kro)
    v = _bdot(ckvn, wv)
    return q, kfull, v


def _adamw_math(w, g, m, v):
    m2 = ADAM_B1 * m + (1.0 - ADAM_B1) * g
    v2 = ADAM_B2 * v + (1.0 - ADAM_B2) * (g * g)
    m_hat = m2 / (1.0 - ADAM_B1 ** ADAM_STEP)
    v_hat = v2 / (1.0 - ADAM_B2 ** ADAM_STEP)
    delta = -ADAM_LR * (m_hat / (jnp.sqrt(v_hat) + ADAM_EPS) + ADAM_WD * w)
    return delta, m2, v2


_DIMS = {"nn": (((1,), (0,)), ((), ())), "nt": (((1,), (1,)), ((), ())), "tn": (((0,), (0,)), ((), ()))}


def _mm(a, b, mode, name, out_dtype=F32, add=None):
    if mode == "nn":
        (m, k), (_, n) = a.shape, b.shape
    elif mode == "nt":
        (m, k), (n, _) = a.shape, b.shape
    else:
        (k, m), (_, n) = a.shape, b.shape
    big = (1024, 1408, 768, 640, 512, 384, 256, 128)
    tm, tn, tk = _tile(m, big), _tile(n, big), _tile(k, (512, 1408, 640, 384, 256, 128))
    nk = k // tk

    def body(a_ref, b_ref, *rest):
        if add is None:
            o_ref, acc_ref = rest
        else:
            add_ref, o_ref, acc_ref = rest
        kk = pl.program_id(2)

        @pl.when(kk == 0)
        def _():
            acc_ref[...] = jnp.zeros_like(acc_ref)

        acc_ref[...] += lax.dot_general(
            a_ref[...].astype(BF16), b_ref[...].astype(BF16), _DIMS[mode], preferred_element_type=F32
        )

        @pl.when(kk == nk - 1)
        def _():
            r = acc_ref[...]
            if add is not None:
                r = r + add_ref[...]
            o_ref[...] = r.astype(out_dtype)

    a_spec = BS((tk, tm), lambda i, j, kk: (kk, i)) if mode == "tn" else BS((tm, tk), lambda i, j, kk: (i, kk))
    b_spec = BS((tn, tk), lambda i, j, kk: (j, kk)) if mode == "nt" else BS((tk, tn), lambda i, j, kk: (kk, j))
    o_spec = BS((tm, tn), lambda i, j, kk: (i, j))
    ins, specs = [a, b], [a_spec, b_spec]
    if add is not None:
        ins.append(add)
        specs.append(o_spec)
    return pl.pallas_call(
        body, grid=(m // tm, n // tn, nk), in_specs=specs, out_specs=o_spec, out_shape=SDS((m, n), out_dtype),
        scratch_shapes=[pltpu.VMEM((tm, tn), F32)], compiler_params=_cp("parallel", "parallel", "arbitrary"), name=name,
    )(*ins)


def _rms_fwd(x, g, name):
    m, d = x.shape
    tr = _tile(m)

    def body(x_ref, g_ref, o_ref):
        o_ref[...] = _rms(x_ref[...], g_ref[...]).astype(BF16)

    return pl.pallas_call(
        body, grid=(m // tr,), in_specs=[BS((tr, d), lambda i: (i, 0)), BS((1, d), lambda i: (0, 0))],
        out_specs=BS((tr, d), lambda i: (i, 0)), out_shape=SDS((m, d), BF16), compiler_params=_cp("parallel"), name=name,
    )(x, g)


def _rms_bwd(x, g, dy, name, dres=None, dy_block=0):
    m, d = x.shape
    tr = _row_tile(m)

    def body(x_ref, g_ref, dy_ref, *rest):
        if dres is None:
            dx_ref, dg_ref = rest
        else:
            dres_ref, dx_ref, dg_ref = rest
        _, vjp = jax.vjp(_rms, x_ref[...], g_ref[...])
        dx, dg = vjp(dy_ref[...])
        if dres is not None:
            dx = dx + dres_ref[...]
        dx_ref[...] = dx

        @pl.when(pl.program_id(0) == 0)
        def _():
            dg_ref[...] = jnp.zeros_like(dg_ref)

        dg_ref[...] += dg

    row = BS((tr, d), lambda i: (i, 0))
    vec = BS((1, d), lambda i: (0, 0))
    ins, specs = [x, g, dy], [row, vec, BS((tr, d), lambda i: (i, dy_block))]
    if dres is not None:
        ins.append(dres)
        specs.append(row)
    return pl.pallas_call(
        body, grid=(m // tr,), in_specs=specs, out_specs=[row, vec], out_shape=[SDS((m, d), F32), SDS((1, d), F32)],
        compiler_params=_cp("arbitrary"), name=name,
    )(*ins)


def _final(h, g, tgt):
    m, d = h.shape
    tr = _row_tile(m)

    def loss_fn(hh, gg, tt):
        e = _rms(hh, gg) - tt
        return 0.5 * jnp.sum(e * e) * (1.0 / d)

    def body(h_ref, g_ref, t_ref, l_ref, dh_ref, dg_ref):
        val, (dh, dg) = jax.value_and_grad(loss_fn, argnums=(0, 1))(h_ref[...], g_ref[...], t_ref[...])
        dh_ref[...] = dh

        @pl.when(pl.program_id(0) == 0)
        def _():
            dg_ref[...] = jnp.zeros_like(dg_ref)
            l_ref[...] = jnp.zeros_like(l_ref)

        dg_ref[...] += dg
        l_ref[...] += jnp.full(l_ref.shape, val, F32)

    row = BS((tr, d), lambda i: (i, 0))
    vec = BS((1, d), lambda i: (0, 0))
    return pl.pallas_call(
        body, grid=(m // tr,), in_specs=[row, vec, row], out_specs=[BS((8, 128), lambda i: (0, 0)), row, vec],
        out_shape=[SDS((8, 128), F32), SDS((m, d), F32), SDS((1, d), F32)], compiler_params=_cp("arbitrary"), name="final_loss",
    )(h, g, tgt)


def _prev_next(z, t):
    row = lax.broadcasted_iota(jnp.int32, z.shape, 0)
    zp = jnp.where(row == 0, 0.0, pltpu.roll(z, 1, axis=0))
    zn = jnp.where(row == t - 1, 0.0, pltpu.roll(z, t - 1, axis=0))
    return zp, zn


def _shift_fwd(z3, mu_p, mu_n):
    b, t, c = z3.shape
    nc = c // 128

    def body(z_ref, mp_ref, mn_ref, o_ref):
        z = z_ref[0]
        zp, zn = _prev_next(z, t)
        o_ref[0] = z + mp_ref[...] * (zp - z) + mn_ref[...] * (zn - z)

    blk = BS((1, t, 128), lambda i, j: (i, 0, j))
    vec = BS((1, 128), lambda i, j: (0, j))
    return pl.pallas_call(
        body, grid=(b, nc), in_specs=[blk, vec, vec], out_specs=blk, out_shape=SDS((b, t, c), F32),
        compiler_params=_cp("parallel", "parallel"), name="shift_fwd",
    )(z3, mu_p, mu_n)


def _shift_bwd(dzs3, z3, mu_p, mu_n):
    b, t, c = z3.shape
    nc = c // 128

    def body(d_ref, z_ref, mp_ref, mn_ref, dz_ref, dmp_ref, dmn_ref):
        d, z = d_ref[0], z_ref[0]
        mp, mn = mp_ref[...], mn_ref[...]
        zp, zn = _prev_next(z, t)
        _, dp_next = _prev_next(d * mp, t)
        dn_prev, _ = _prev_next(d * mn, t)
        dz_ref[0] = (d * (1.0 - mp - mn) + dp_next + dn_prev).astype(BF16)

        @pl.when(pl.program_id(1) == 0)
        def _():
            dmp_ref[...] = jnp.zeros_like(dmp_ref)
            dmn_ref[...] = jnp.zeros_like(dmn_ref)

        dmp_ref[...] += jnp.sum(d * (zp - z), axis=0, keepdims=True)
        dmn_ref[...] += jnp.sum(d * (zn - z), axis=0, keepdims=True)

    blk = BS((1, t, 128), lambda j, i: (i, 0, j))
    vec = BS((1, 128), lambda j, i: (0, j))
    return pl.pallas_call(
        body, grid=(nc, b), in_specs=[blk, blk, vec, vec], out_specs=[blk, vec, vec],
        out_shape=[SDS((b, t, c), BF16), SDS((1, c), F32), SDS((1, c), F32)],
        compiler_params=_cp("parallel", "arbitrary"), name="shift_bwd",
    )(dzs3, z3, mu_p, mu_n)


def _const(shape):
    nd = len(shape)
    return BS(shape, lambda i: (0,) * nd)


def _prep_fwd(zs, p):
    m = zs.shape[0]
    tr = 256
    params = [p["w0"], p["w2"], p["a0"], p["a2"], p["g2"], p["k_k"], p["k_a"], p["ones_bd"]]

    def body(zs_ref, w0, w2, a0, a2, g2, kk_, ka_, bd, wf, wb, kf, kb, kk, kaf, kab, g):
        outs = _prep_fn(zs_ref[...], w0[...], w2[...], a0[...], a2[...], g2[...], kk_[...], ka_[...], bd[...])
        for ref, val in zip((wf, wb, kf, kb, kk, kaf, kab, g), outs):
            ref[...] = val

    row = BS((tr, 512), lambda i: (i, 0))
    return pl.pallas_call(
        body, grid=(m // tr,), in_specs=[BS((tr, RWKV_COLS), lambda i: (i, 0))] + [_const(q.shape) for q in params],
        out_specs=[row] * 8, out_shape=[SDS((m, 512), F32)] * 8, compiler_params=_cp("parallel"), name="rwkv_prep_fwd",
    )(zs, *params)


def _prep_bwd(zs, p, ct_rows):
    m = zs.shape[0]
    tr = 128
    params = [p["w0"], p["w2"], p["a0"], p["a2"], p["g2"], p["k_k"], p["k_a"]]
    names = ["dwf", "dwb", "dkf", "dkb", "dkk_f", "dkk_b", "dkaf", "dkab", "dr_f", "dr_b", "dr_p", "dk_p", "dv_p", "dg",
             "dv_f", "dv_b"]
    rows = [ct_rows[n] for n in names]

    def body(zs_ref, w0, w2, a0, a2, g2, kk_, ka_, bd, *rest):
        c = {n: r[...] for n, r in zip(names, rest[:len(names)])}
        outs = rest[len(names):]
        dzs_ref, grads = outs[0], outs[1:]
        ones_bd = bd[...]
        _, vjp = jax.vjp(
            lambda *q: _prep_fn(*q, ones_bd), zs_ref[...], w0[...], w2[...], a0[...], a2[...], g2[...], kk_[...], ka_[...]
        )
        cts = (c["dwf"], c["dwb"], c["dkf"] + c["dk_p"], c["dkb"] + c["dk_p"], c["dkk_f"] + c["dkk_b"], c["dkaf"], c["dkab"], c["dg"])
        dzs, *dparams = vjp(cts)
        dr = c["dr_f"] + c["dr_b"] + c["dr_p"]
        dv = c["dv_f"] + c["dv_b"] + c["dv_p"]
        dzs_ref[:, 0:512] = dzs[:, 0:512] + dr
        dzs_ref[:, 512:1024] = dzs[:, 512:1024]
        dzs_ref[:, 1024:1536] = dzs[:, 1024:1536] + dv
        dzs_ref[:, 1536:1920] = dzs[:, 1536:1920]

        @pl.when(pl.program_id(0) == 0)
        def _():
            for gr in grads:
                gr[...] = jnp.zeros_like(gr)

        for gr, val in zip(grads, dparams):
            gr[...] += val

    row = BS((tr, 512), lambda i: (i, 0))
    return pl.pallas_call(
        body, grid=(m // tr,),
        in_specs=[BS((tr, RWKV_COLS), lambda i: (i, 0))] + [_const(q.shape) for q in params] + [_const(p["ones_bd"].shape)]
        + [row] * len(names),
        out_specs=[BS((tr, RWKV_COLS), lambda i: (i, 0))] + [_const(q.shape) for q in params],
        out_shape=[SDS((m, RWKV_COLS), F32)] + [SDS(q.shape, F32) for q in params],
        compiler_params=_cp("arbitrary"), name="rwkv_prep_bwd",
    )(zs, *params, p["ones_bd"], *rows)


def _post_specs(tr):
    r = BS((tr, 512), lambda i: (i, 0))
    v = BS((tr, 512), lambda i: (i, 2))
    row = BS((tr, 512), lambda i: (i, 0))
    return r, v, row


def _post_fwd(y_f, y_b, zs, kf, kb, g, p):
    m = zs.shape[0]
    tr = 256
    r, v, row = _post_specs(tr)
    vecs = [p["r_k"], p["ln_x_g"], p["ln_x_b"], p["ones_bd"]]

    def body(yf, yb, r_ref, v_ref, kf_ref, kb_ref, g_ref, rk, lg, lb, bd, o_ref):
        o_ref[...] = _post_fn(
            yf[...] + yb[...], r_ref[...], kf_ref[...], kb_ref[...], v_ref[...], g_ref[...], rk[...], lg[...], lb[...], bd[...]
        ).astype(BF16)

    return pl.pallas_call(
        body, grid=(m // tr,), in_specs=[row, row, r, v, row, row, row] + [_const(q.shape) for q in vecs],
        out_specs=row, out_shape=SDS((m, 512), BF16), compiler_params=_cp("parallel"), name="rwkv_post_fwd",
    )(y_f, y_b, zs, zs, kf, kb, g, *vecs)


def _post_bwd(y_f, y_b, zs, kf, kb, g, p, dymix):
    m = zs.shape[0]
    tr = 128
    r, v, row = _post_specs(tr)
    vecs = [p["r_k"], p["ln_x_g"], p["ln_x_b"]]

    def body(yf, yb, r_ref, v_ref, kf_ref, kb_ref, g_ref, rk, lg, lb, bd, dy_ref, dyo, dr, dk, dv, dg, drk, dlg, dlb):
        ones_bd = bd[...]
        _, vjp = jax.vjp(
            lambda *q: _post_fn(*q, ones_bd),
            yf[...] + yb[...], r_ref[...], kf_ref[...], kb_ref[...], v_ref[...], g_ref[...], rk[...], lg[...], lb[...],
        )
        c_y, c_r, c_kf, _, c_v, c_g, c_rk, c_lg, c_lb = vjp(dy_ref[...])
        dyo[...] = c_y
        dr[...] = c_r
        dk[...] = c_kf
        dv[...] = c_v
        dg[...] = c_g

        @pl.when(pl.program_id(0) == 0)
        def _():
            for ref in (drk, dlg, dlb):
                ref[...] = jnp.zeros_like(ref)

        drk[...] += c_rk
        dlg[...] += c_lg
        dlb[...] += c_lb

    vec = _const((1, 512))
    return pl.pallas_call(
        body, grid=(m // tr,),
        in_specs=[row, row, r, v, row, row, row] + [_const(q.shape) for q in vecs] + [_const(p["ones_bd"].shape), row],
        out_specs=[row, row, row, row, row, vec, vec, vec],
        out_shape=[SDS((m, 512), F32)] * 5 + [SDS((1, 512), F32)] * 3,
        compiler_params=_cp("arbitrary"), name="rwkv_post_bwd",
    )(y_f, y_b, zs, zs, kf, kb, g, *vecs, p["ones_bd"], dymix)


def _half_ones():
    ri = lax.broadcasted_iota(jnp.int32, (256, 128), 0) & 127
    ci = lax.broadcasted_iota(jnp.int32, (256, 128), 1)
    return jnp.where((ri < 64) == (ci < 64), 1.0, 0.0).astype(BF16)


def _half_sums(xs, ones, exact):
    out = [None] * len(xs)
    for two in (True, False):
        idx = [i for i, e in enumerate(exact) if bool(e) == two]
        if not idx:
            continue
        x = jnp.concatenate([xs[i] for i in idx], axis=0)
        hi = x.astype(BF16)
        if two:
            mid = (x - hi.astype(F32)).astype(BF16)
            res = jnp.dot(jnp.concatenate([hi, mid], axis=1), ones, preferred_element_type=F32)
        else:
            res = jnp.dot(hi, ones[:128], preferred_element_type=F32)
        for j, i in enumerate(idx):
            out[i] = res[64 * j:64 * j + 64]
    return out


def _scan_specs(b, t):
    nc = t // SCAN_CHUNK
    up, down = (lambda c: c), (lambda c: nc - 1 - c)
    rows = [BS((b, SCAN_CHUNK, 512), lambda c, ci=ci: (0, ci(c), 0)) for ci in (up, down)]
    vrows = [BS((b, SCAN_CHUNK, 512), lambda c, ci=ci: (0, ci(c), 2)) for ci in (up, down)]
    hist = [BS((SCAN_CHUNK, b * 4, 64, 128), lambda c, ci=ci: (ci(c), 0, 0, 0)) for ci in (up, down)]
    return nc, rows, vrows, hist


class _Window:
    def __init__(self, g, ascending):
        self.bases = [pl.multiple_of(g * 8, 8) if asc else pl.multiple_of(SCAN_CHUNK - 8 - g * 8, 8) for asc in ascending]
        self.ascending = ascending
        self.blocks = {}
        self.row_id = lax.broadcasted_iota(jnp.int32, (8, 128), 0)

    def j(self, d, s):
        return s if self.ascending[d] else 7 - s

    def time(self, d, s):
        return self.bases[d] + self.j(d, s)

    def row(self, ref, d, bi, cols, s):
        key = (id(ref), d, bi, cols.start)
        if key not in self.blocks:
            self.blocks[key] = ref[bi, pl.ds(self.bases[d], 8), cols]
        jj = self.j(d, s)
        return self.blocks[key][jj:jj + 1, :]

    def put(self, buf, key, d, s, row):
        prev = buf.get(key)
        new = jnp.broadcast_to(row, (8, 128))
        buf[key] = new if prev is None else jnp.where(self.row_id == self.j(d, s), new, prev)

    def flush(self, buf, refs_of):
        for key, val in buf.items():
            ref, d, bi, cols = refs_of(key)
            ref[bi, pl.ds(self.bases[d], 8), cols] = val


def _pairs(b):
    return [(bi * 4 + p, bi, slice(128 * p, 128 * p + 128)) for bi in range(b) for p in range(4)]


def _colsum(x):
    return jnp.sum(x, axis=0, keepdims=True)


def _eye_mask():
    return (lax.broadcasted_iota(jnp.int32, (64, 128), 1) & 63) == lax.broadcasted_iota(jnp.int32, (64, 128), 0)


def _scan_fwd(zs, kk, ops_f, ops_b, shard):
    b, t = zs.shape[:2]
    nc, rows, vrows, hist = _scan_specs(b, t)
    npair = b * 4

    def body(*refs):
        ins, shard_ref, outs, s_ref = refs[:12], refs[12], refs[13:17], refs[17]
        gather = (shard_ref, *refs[18:21])
        dirs = [dict(zip(("r", "kk", "v", "w", "k", "ka", "y", "h"), (*ins[6 * d:6 * d + 6], *outs[2 * d:2 * d + 2])))
                for d in (0, 1)]

        @pl.when(pl.program_id(0) == 0)
        def _():
            s_ref[...] = jnp.zeros_like(s_ref)
            _gather_halves(*gather, "start")

        @pl.when(pl.program_id(0) == nc - 1)
        def _():
            _gather_halves(*gather, "finish")

        ones, eye = _half_ones(), _eye_mask()
        chains = [(d, pr, bi, cols) for d in (0, 1) for pr, bi, cols in _pairs(b)]

        def eight_steps(g, carry):
            win = _Window(g, (True, False))
            ybuf = {}
            for s in range(8):
                s_prev, xa = [], []
                for d, pr, bi, cols in chains:
                    q = dirs[d]
                    st = s_ref[d * npair + pr]
                    q["h"][win.time(d, s), pr] = st
                    s_prev.append(st)
                    xa += [st * win.row(q["kk"], d, bi, cols, s), jnp.where(eye, win.row(q["v"], d, bi, cols, s), 0.0)]
                ra = _half_sums(xa, ones, (True, False) * len(chains))
                xb = []
                for i, (d, pr, bi, cols) in enumerate(chains):
                    q = dirs[d]
                    s_new = s_prev[i] * win.row(q["w"], d, bi, cols, s) - ra[2 * i] * win.row(q["ka"], d, bi, cols, s) \
                        + ra[2 * i + 1] * win.row(q["k"], d, bi, cols, s)
                    s_ref[d * npair + pr] = s_new
                    xb.append(s_new * win.row(q["r"], d, bi, cols, s))
                rb = _half_sums(xb, ones, (False,) * len(chains))
                for i, (d, pr, bi, cols) in enumerate(chains):
                    win.put(ybuf, i, d, s, _colsum(jnp.where(eye, rb[i], 0.0)))
            win.flush(ybuf, lambda i: (dirs[chains[i][0]]["y"], chains[i][0], chains[i][2], chains[i][3]))
            return carry

        lax.fori_loop(0, SCAN_CHUNK // 8, eight_steps, 0)

    row_shape, hist_shape = SDS((b, t, 512), F32), SDS((t, npair, 64, 128), F32)
    state = (2 * npair, 64, 128)
    return pl.pallas_call(
        body, grid=(nc,), in_specs=sum(([rows[d], rows[d], vrows[d]] + [rows[d]] * 3 for d in (0, 1)), []) + [ANY],
        out_specs=[rows[0], hist[0], rows[1], hist[1], BS(state, lambda c: (0, 0, 0)), ANY],
        out_shape=[row_shape, hist_shape, row_shape, hist_shape, SDS(state, F32), _gathered_shape(shard)],
        scratch_shapes=_gather_sems(), compiler_params=_cp("arbitrary"), name="wkv_scan",
    )(zs, kk, zs, *ops_f, zs, kk, zs, *ops_b, shard)


def _scan_bwd(zs, kk, dy, ops_f, hist_f, ops_b, hist_b, s_last, partials):
    b, t = zs.shape[:2]
    nc, rows, vrows, hist = _scan_specs(b, t)
    npair = b * 4
    names_in = ("r", "kk", "v", "dy", "w", "k", "ka", "h")
    names_out = ("dr", "dw", "dk", "dkk", "dka", "dv")

    def body(*refs):
        ins, last_ref, part_ref, outs, recv_ref = refs[:16], refs[16], refs[17], refs[18:30], refs[30]
        ds_ref, after_ref = refs[31], refs[32]
        scatter = (part_ref, recv_ref, refs[33], refs[34])
        dirs = [dict(zip(names_in + names_out, (*ins[8 * d:8 * d + 8], *outs[6 * d:6 * d + 6]))) for d in (0, 1)]

        @pl.when(pl.program_id(0) == 0)
        def _():
            ds_ref[...] = jnp.zeros_like(ds_ref)
            after_ref[...] = last_ref[...]
            _scatter_partials(*scatter, "start")

        @pl.when(pl.program_id(0) == nc - 1)
        def _():
            _scatter_partials(*scatter, "finish")

        ones, eye = _half_ones(), _eye_mask()
        chains = [(d, pr, bi, cols) for d in (0, 1) for pr, bi, cols in _pairs(b)]

        def eight_steps(g, carry):
            win = _Window(g, (False, True))
            obuf = {}
            s_after = [after_ref[d * npair + pr] for d, pr, _, _ in chains]
            for s in range(8):
                row = lambda name, d, bi, cols: win.row(dirs[d][name], d, bi, cols, s)
                s_prev, xa = [], []
                for d, pr, bi, cols in chains:
                    st = dirs[d]["h"][win.time(d, s), pr]
                    s_prev.append(st)
                    xa += [st * row("kk", d, bi, cols), jnp.where(eye, row("v", d, bi, cols), 0.0),
                           jnp.where(eye, row("dy", d, bi, cols), 0.0)]
                ra = _half_sums(xa, ones, (False,) * len(xa))
                ds_now, xb = [], []
                for i, (d, pr, bi, cols) in enumerate(chains):
                    skk, vcol, dycol = ra[3 * i], ra[3 * i + 1], ra[3 * i + 2]
                    ds = ds_ref[d * npair + pr] + dycol * row("r", d, bi, cols)
                    win.put(obuf, (i, "dr"), d, s, _colsum(s_after[i] * dycol))
                    win.put(obuf, (i, "dk"), d, s, _colsum(ds * vcol))
                    win.put(obuf, (i, "dka"), d, s, -_colsum(ds * skk))
                    win.put(obuf, (i, "dw"), d, s, _colsum(ds * s_prev[i]))
                    ds_now.append(ds)
                    xb += [ds * row("k", d, bi, cols), ds * row("ka", d, bi, cols)]
                rb = _half_sums(xb, ones, (False,) * len(xb))
                for i, (d, pr, bi, cols) in enumerate(chains):
                    dskk_neg = rb[2 * i + 1]
                    win.put(obuf, (i, "dv"), d, s, _colsum(jnp.where(eye, rb[2 * i], 0.0)))
                    win.put(obuf, (i, "dkk"), d, s, -_colsum(s_prev[i] * dskk_neg))
                    ds_ref[d * npair + pr] = ds_now[i] * row("w", d, bi, cols) - dskk_neg * row("kk", d, bi, cols)
                s_after = s_prev
            for i, (d, pr, _, _) in enumerate(chains):
                after_ref[d * npair + pr] = s_after[i]
            win.flush(obuf, lambda key: (dirs[chains[key[0]][0]][key[1]], chains[key[0]][0], chains[key[0]][2], chains[key[0]][3]))
            return carry

        lax.fori_loop(0, SCAN_CHUNK // 8, eight_steps, 0)

    row_shape = SDS((b, t, 512), F32)
    state = (2 * npair, 64, 128)
    return pl.pallas_call(
        body, grid=(nc,),
        in_specs=sum(([rows[d], rows[d], vrows[d]] + [rows[d]] * 4 + [hist[d]] for d in (1, 0)), [])
        + [BS(state, lambda c: (0, 0, 0)), ANY],
        out_specs=[rows[1]] * 6 + [rows[0]] * 6 + [ANY],
        out_shape=[row_shape] * 12 + [SDS((3,) + partials.shape[1:], partials.dtype)],
        scratch_shapes=[pltpu.VMEM(state, F32), pltpu.VMEM(state, F32)] + _scatter_sems(),
        compiler_params=_cp("arbitrary"), name="wkv_scan_bwd",
    )(zs, kk, zs, dy, *ops_f, hist_f, zs, kk, zs, dy, *ops_b, hist_b, s_last, partials)


def _mla_fwd(zm, cs, sn, p, t):
    m = zm.shape[0]
    tr = 256
    per = t // tr
    params = [p["q_norm_g"], p["kv_norm_g"], p["wq"], p["wqr"], p["wk"], p["wv"]]

    def body(z_ref, cs_ref, sn_ref, gq, gkv, wq, wqr, wk, wv, q_ref, k_ref, v_ref):
        q, kf, v = _mla_fn(z_ref[...], cs_ref[...], sn_ref[...], gq[...], gkv[...], wq[...], wqr[...], wk[...], wv[...])
        q_ref[...] = q.astype(BF16)
        k_ref[...] = kf.astype(BF16)
        v_ref[...] = v.astype(BF16)

    tab = BS((tr, 128), lambda i: (i % per, 0))
    return pl.pallas_call(
        body, grid=(m // tr,), in_specs=[BS((tr, MLA_PAD_COLS), lambda i: (i, 0)), tab, tab] + [_const(q.shape) for q in params],
        out_specs=[BS((tr, 1024), lambda i: (i, 0)), BS((tr, 1024), lambda i: (i, 0)), BS((tr, 512), lambda i: (i, 0))],
        out_shape=[SDS((m, 1024), BF16), SDS((m, 1024), BF16), SDS((m, 512), BF16)], compiler_params=_cp("parallel"), name="mla_prep_fwd",
    )(zm, cs, sn, *params)


def _mla_bwd(zm, cs, sn, p, t, dq, dk, dv):
    m = zm.shape[0]
    tr = 128
    per = t // tr
    params = [p["q_norm_g"], p["kv_norm_g"], p["wq"], p["wqr"], p["wk"], p["wv"]]

    def body(z_ref, cs_ref, sn_ref, gq, gkv, wq, wqr, wk, wv, dq_ref, dk_ref, dv_ref, dz_ref, *grads):
        cs_v, sn_v = cs_ref[...], sn_ref[...]
        _, vjp = jax.vjp(
            lambda *q: _mla_fn(q[0], cs_v, sn_v, *q[1:]), z_ref[...], gq[...], gkv[...], wq[...], wqr[...], wk[...], wv[...]
        )
        dz, *dparams = vjp((dq_ref[...], dk_ref[...], dv_ref[...]))
        dz_ref[...] = dz.astype(BF16)

        @pl.when(pl.program_id(0) == 0)
        def _():
            for gr in grads:
                gr[...] = jnp.zeros_like(gr)

        for gr, val in zip(grads, dparams):
            gr[...] += val

    tab = BS((tr, 128), lambda i: (i % per, 0))
    wide = BS((tr, 1024), lambda i: (i, 0))
    return pl.pallas_call(
        body, grid=(m // tr,),
        in_specs=[BS((tr, MLA_PAD_COLS), lambda i: (i, 0)), tab, tab] + [_const(q.shape) for q in params]
        + [wide, wide, BS((tr, 512), lambda i: (i, 0))],
        out_specs=[BS((tr, MLA_PAD_COLS), lambda i: (i, 0))] + [_const(q.shape) for q in params],
        out_shape=[SDS((m, MLA_PAD_COLS), BF16)] + [SDS(q.shape, F32) for q in params],
        compiler_params=_cp("arbitrary"), name="mla_prep_bwd",
    )(zm, cs, sn, *params, dq, dk, dv)


_NT = (((1,), (1,)), ((), ()))
_TN = (((0,), (0,)), ((), ()))


def _attn_fwd(q, kf, v, b, t):
    m = q.shape[0]
    tq = 256
    nq = t // tq

    def body(q_ref, k_ref, v_ref, o_ref, l_ref):
        lo = lax.broadcasted_iota(jnp.int32, (1, 128), 1) < 64
        v_all = v_ref[...]
        o = jnp.zeros((tq, 128), F32)
        lse = []
        for h in range(2):
            hs = slice(128 * h, 128 * h + 128)
            s = lax.dot_general(q_ref[:, hs], k_ref[:, hs], _NT, preferred_element_type=F32) * MLA_SCALE
            mx = jnp.max(s, axis=1, keepdims=True)
            e = jnp.exp(s - mx)
            den = jnp.sum(e, axis=1, keepdims=True)
            vh = jnp.where(lo if h == 0 else jnp.logical_not(lo), v_all, jnp.zeros_like(v_all))
            o = o + jnp.dot(e.astype(BF16), vh, preferred_element_type=F32) / den
            lse.append(mx + jnp.log(den))
        o_ref[...] = o
        l_ref[...] = jnp.where(lo, lse[0], lse[1])

    return pl.pallas_call(
        body, grid=(b, 4, nq),
        in_specs=[BS((tq, 256), lambda bi, hp, i: (bi * nq + i, hp)), BS((t, 256), lambda bi, hp, i: (bi, hp)),
                  BS((t, 128), lambda bi, hp, i: (bi, hp))],
        out_specs=[BS((tq, 128), lambda bi, hp, i: (bi * nq + i, hp))] * 2,
        out_shape=[SDS((m, 512), F32), SDS((m, 512), F32)], compiler_params=_cp("parallel", "parallel", "arbitrary"), name="attn_fwd",
    )(q, kf, v)


def _attn_bwd(q, kf, v, o, lse, do, b, t):
    m = q.shape[0]
    tq = 256
    nq = t // tq

    def body(q_ref, k_ref, v_ref, o_ref, l_ref, do_ref, dq_ref, dk_ref, dv_ref):
        lo = lax.broadcasted_iota(jnp.int32, (1, 128), 1) < 64

        @pl.when(pl.program_id(2) == 0)
        def _():
            dk_ref[...] = jnp.zeros_like(dk_ref)
            dv_ref[...] = jnp.zeros_like(dv_ref)

        v_all, o_all, l_all, do_all = v_ref[...], o_ref[...], l_ref[...], do_ref[...]
        dv_acc = jnp.zeros((t, 128), F32)
        for h in range(2):
            hs = slice(128 * h, 128 * h + 128)
            mask = lo if h == 0 else jnp.logical_not(lo)
            qh, kh = q_ref[:, hs], k_ref[:, hs]
            s = lax.dot_general(qh, kh, _NT, preferred_element_type=F32) * MLA_SCALE
            lse_h = jnp.max(jnp.where(mask, l_all, -jnp.inf), axis=1, keepdims=True)
            pr = jnp.exp(s - lse_h)
            do_h = jnp.where(mask, do_all, 0.0)
            dp = lax.dot_general(do_h.astype(BF16), v_all, _NT, preferred_element_type=F32)
            dsum = jnp.sum(do_h * o_all, axis=1, keepdims=True)
            ds = (pr * (dp - dsum) * MLA_SCALE).astype(BF16)
            dq_ref[:, hs] = jnp.dot(ds, kh, preferred_element_type=F32)
            dk_ref[:, hs] += lax.dot_general(ds, qh, _TN, preferred_element_type=F32)
            dv_acc = dv_acc + lax.dot_general(pr.astype(BF16), do_h.astype(BF16), _TN, preferred_element_type=F32)
        dv_ref[...] += dv_acc

    qspec = BS((tq, 256), lambda bi, hp, i: (bi * nq + i, hp))
    kspec = BS((t, 256), lambda bi, hp, i: (bi, hp))
    vspec = BS((t, 128), lambda bi, hp, i: (bi, hp))
    ospec = BS((tq, 128), lambda bi, hp, i: (bi * nq + i, hp))
    return pl.pallas_call(
        body, grid=(b, 4, nq), in_specs=[qspec, kspec, vspec, ospec, ospec, ospec], out_specs=[qspec, kspec, vspec],
        out_shape=[SDS((m, 1024), F32), SDS((m, 1024), F32), SDS((m, 512), F32)],
        compiler_params=_cp("parallel", "parallel", "arbitrary"), name="attn_bwd",
    )(q, kf, v, o, lse, do)


def _conv3(u, w_ref, b_ref, t):
    up, un = _prev_next(u, t)
    return w_ref[0:1, :] * up + w_ref[1:2, :] * u + w_ref[2:3, :] * un + b_ref[...], up, un


def _ffn_mid_fwd(ug3, uv3, cw, cb):
    b, t, f = ug3.shape
    nc = f // 256

    def body(ug_ref, uv_ref, wg_ref, wv_ref, bg_ref, bv_ref, a_ref):
        gc, _, _ = _conv3(ug_ref[0], wg_ref, bg_ref, t)
        vc, _, _ = _conv3(uv_ref[0], wv_ref, bv_ref, t)
        a_ref[0] = (gc * jax.nn.sigmoid(gc) * vc).astype(BF16)

    blk = BS((1, t, 256), lambda i, j: (i, 0, j))
    return pl.pallas_call(
        body, grid=(b, nc),
        in_specs=[blk, blk, BS((3, 256), lambda i, j: (0, j)), BS((3, 256), lambda i, j: (0, j + nc)),
                  BS((1, 256), lambda i, j: (0, j)), BS((1, 256), lambda i, j: (0, j + nc))],
        out_specs=blk, out_shape=SDS((b, t, f), BF16), compiler_params=_cp("parallel", "parallel"), name="ffn_mid_fwd",
    )(ug3, uv3, cw, cw, cb, cb)


def _ffn_mid_bwd(ug3, uv3, cw, cb, da3):
    b, t, f = ug3.shape
    nc = f // 256

    def half(u, up, un, dc, w_ref):
        dprev, dnext = _prev_next(dc, t)
        du = w_ref[1:2, :] * dc + w_ref[0:1, :] * dnext + w_ref[2:3, :] * dprev
        sums = [jnp.sum(dc * q, axis=0, keepdims=True) for q in (up, u, un)] + [jnp.sum(dc, axis=0, keepdims=True)]
        row = lax.broadcasted_iota(jnp.int32, (8, 256), 0)
        tab = jnp.zeros((8, 256), F32)
        for i, s in enumerate(sums):
            tab = jnp.where(row == i, s, tab)
        return du, tab

    def body(ug_ref, uv_ref, wg_ref, wv_ref, bg_ref, bv_ref, da_ref, dug_ref, duv_ref, tg_ref, tv_ref):
        ug, uv, da = ug_ref[0], uv_ref[0], da_ref[0]
        gc, gp, gn = _conv3(ug, wg_ref, bg_ref, t)
        vc, vp, vn = _conv3(uv, wv_ref, bv_ref, t)
        sg = jax.nn.sigmoid(gc)
        d_gc = da * vc * (sg * (1.0 + gc * (1.0 - sg)))
        d_vc = da * (gc * sg)
        dug, tg = half(ug, gp, gn, d_gc, wg_ref)
        duv, tv = half(uv, vp, vn, d_vc, wv_ref)
        dug_ref[0] = dug.astype(BF16)
        duv_ref[0] = duv.astype(BF16)

        @pl.when(pl.program_id(1) == 0)
        def _():
            tg_ref[...] = jnp.zeros_like(tg_ref)
            tv_ref[...] = jnp.zeros_like(tv_ref)

        tg_ref[...] += tg
        tv_ref[...] += tv

    blk = BS((1, t, 256), lambda j, i: (i, 0, j))
    tab = BS((8, 256), lambda j, i: (0, j))
    return pl.pallas_call(
        body, grid=(nc, b),
        in_specs=[blk, blk, BS((3, 256), lambda j, i: (0, j)), BS((3, 256), lambda j, i: (0, j + nc)),
                  BS((1, 256), lambda j, i: (0, j)), BS((1, 256), lambda j, i: (0, j + nc)), blk],
        out_specs=[blk, blk, tab, tab],
        out_shape=[SDS((b, t, f), BF16), SDS((b, t, f), BF16), SDS((8, f), F32), SDS((8, f), F32)],
        compiler_params=_cp("parallel", "arbitrary"), name="ffn_mid_bwd",
    )(ug3, uv3, cw, cw, cb, cb, da3)


def _add_rows(parts, name, out_dtype=F32):
    r = parts[0].shape[0]
    tr = _row_tile(r, 1024)
    n = len(parts)

    def body(*refs):
        acc = refs[0][...].astype(F32)
        for q in refs[1:n]:
            acc = acc + q[...].astype(F32)
        refs[n][...] = acc.astype(out_dtype)

    row = BS((tr, 128), lambda i: (i, 0))
    return pl.pallas_call(
        body, grid=(r // tr,), in_specs=[row] * n, out_specs=row, out_shape=SDS((r, 128), out_dtype),
        compiler_params=_cp("parallel"), name=name,
    )(*parts)


def _adamw(w, g, m, v, name):
    r, c = w.shape
    tr = _row_tile(r)

    def body(w_ref, g_ref, m_ref, v_ref, d_ref, m2_ref, v2_ref):
        d, m2, v2 = _adamw_math(w_ref[...], g_ref[...], m_ref[...], v_ref[...])
        d_ref[...] = d
        m2_ref[...] = m2
        v2_ref[...] = v2

    blk = BS((tr, c), lambda i: (i, 0))
    return pl.pallas_call(
        body, grid=(r // tr,), in_specs=[blk] * 4, out_specs=[blk] * 3, out_shape=[SDS((r, c), F32)] * 3,
        compiler_params=_cp("parallel"), name=name,
    )(w, g, m, v)


def _place():
    return lax.axis_index("x"), lax.axis_index("y"), lax.axis_index("c")


def _flip(v, bit):
    return 1 - v if bit else v


def _allgather_weights(shard):
    def body(x_ref, out_ref, send_sems, recv_sems):
        _gather_halves(x_ref, out_ref, send_sems, recv_sems, "start")
        _gather_halves(x_ref, out_ref, send_sems, recv_sems, "finish")

    return pl.pallas_call(
        body, out_shape=_gathered_shape(shard), in_specs=[ANY], out_specs=ANY, scratch_shapes=_gather_sems(),
        name="allgather_weights",
    )(shard)


def _gathered_shape(shard):
    return SDS((8 * (shard.shape[0] // 2), 128), shard.dtype)


def _gather_sems():
    return [pltpu.SemaphoreType.DMA((6,)), pltpu.SemaphoreType.DMA((6,))]


def _gather_halves(x_ref, out_ref, send_sems, recv_sems, phase):
    rh = x_ref.shape[0] // 2
    x, y, c = _place()
    me, sibling = (x, y, c), (x, y, 1 - c)
    chips = [(1 - x, y), (x, 1 - y), (1 - x, 1 - y)]
    mine_src = x_ref.at[pl.ds(c * rh, rh), :]

    def rows(px, py, pc):
        return out_ref.at[pl.ds((4 * px + 2 * py + pc) * rh, rh), :]

    def copy(k, block, to, src=None):
        return pltpu.make_async_remote_copy(
            src_ref=rows(*block) if src is None else src, dst_ref=rows(*block), send_sem=send_sems.at[k],
            recv_sem=recv_sems.at[k], device_id=to, device_id_type=MESH,
        )

    first = [copy(j, me, (*chip, c), src=mine_src) for j, chip in enumerate(chips)]
    if phase == "start":
        for cp in first:
            cp.start()
        return
    passed = [copy(3 + j, (*chip, c), sibling) for j, chip in enumerate(chips)]
    for j, chip in enumerate(chips):
        copy(j, (*chip, c), me).wait_recv()
        passed[j].start()
    for j, chip in enumerate(chips):
        copy(3 + j, (*chip, 1 - c), me).wait_recv()
    for cp in first + passed:
        cp.wait_send()


def _scatter_partials(g_ref, recv_ref, send_sems, recv_sems, phase):
    x, y, c = _place()
    copies = []
    for j, (fx, fy) in enumerate(((1, 0), (0, 1), (1, 1))):
        px, py = _flip(x, fx), _flip(y, fy)
        copies.append(pltpu.make_async_remote_copy(
            src_ref=g_ref.at[2 * px + py], dst_ref=recv_ref.at[j], send_sem=send_sems.at[j], recv_sem=recv_sems.at[j],
            device_id=(px, py, c), device_id_type=MESH,
        ))
    if phase == "start":
        for cp in copies:
            cp.start()
        return
    for cp in copies:
        cp.wait_recv()
    for cp in copies:
        cp.wait_send()


def _scatter_sems():
    return [pltpu.SemaphoreType.DMA((3,)), pltpu.SemaphoreType.DMA((3,))]


def _scatter_to_chips(g):
    def body(g_ref, recv_ref, send_sems, recv_sems):
        _scatter_partials(g_ref, recv_ref, send_sems, recv_sems, "start")
        _scatter_partials(g_ref, recv_ref, send_sems, recv_sems, "finish")

    return pl.pallas_call(
        body, out_shape=SDS((3,) + g.shape[1:], g.dtype), in_specs=[ANY], out_specs=ANY,
        scratch_shapes=[pltpu.SemaphoreType.DMA((3,)), pltpu.SemaphoreType.DMA((3,))], name="scatter_grads",
    )(g)


def _send_to_sibling(a, half_of_rows):
    rh = a.shape[1] // 2

    def body(a_ref, b_ref, send_sem, recv_sem):
        x, y, c = _place()
        src = a_ref.at[:, pl.ds((1 - c) * rh, rh), :] if half_of_rows else a_ref
        cp = pltpu.make_async_remote_copy(
            src_ref=src, dst_ref=b_ref, send_sem=send_sem, recv_sem=recv_sem, device_id=(x, y, 1 - c), device_id_type=MESH
        )
        cp.start()
        cp.wait()

    shape = (a.shape[0], rh, 128) if half_of_rows else a.shape
    return pl.pallas_call(
        body, out_shape=SDS(shape, a.dtype), in_specs=[ANY], out_specs=ANY,
        scratch_shapes=[pltpu.SemaphoreType.DMA, pltpu.SemaphoreType.DMA],
        name="sibling_halves" if half_of_rows else "sibling_swap",
    )(a)


def _allreduce_small(v):
    r = v.shape[0]

    def body(v_ref, out_ref, buf_ref, send_sems, recv_sems):
        x, y, c = _place()
        buf_ref[0] = v_ref[...]
        copies = []
        for k in range(1, 8):
            peer = (_flip(x, k >> 2 & 1), _flip(y, k >> 1 & 1), _flip(c, k & 1))
            cp = pltpu.make_async_remote_copy(
                src_ref=v_ref, dst_ref=buf_ref.at[k], send_sem=send_sems.at[k - 1], recv_sem=recv_sems.at[k - 1],
                device_id=peer, device_id_type=MESH,
            )
            cp.start()
            copies.append(cp)
        for cp in copies:
            cp.wait_recv()
        acc = None
        for d in range(8):
            slot = 4 * _flip(x, d >> 2 & 1) + 2 * _flip(y, d >> 1 & 1) + _flip(c, d & 1)
            term = buf_ref[slot]
            acc = term if acc is None else acc + term
        out_ref[...] = acc
        for cp in copies:
            cp.wait_send()

    return pl.pallas_call(
        body, out_shape=SDS(v.shape, F32), in_specs=[VMEM], out_specs=VMEM,
        scratch_shapes=[pltpu.VMEM((8, r, 128), F32), pltpu.SemaphoreType.DMA((7,)), pltpu.SemaphoreType.DMA((7,))],
        name="allreduce_small",
    )(v)


_BIG_A = (
    ("w_in", 1, False), ("decay_w2_fwd", 1, False), ("decay_w2_bwd", 1, False), ("iclr_a2_fwd", 1, False),
    ("iclr_a2_bwd", 1, False), ("gate_g2", 1, False),
)
_BIG_B = (
    ("w_uq", 0, False), ("w_ukv", 1, False), ("w_out", 0, False), ("w_ffn_up", 1, False), ("ffn_conv_w", 1, True),
    ("w_ffn_down", 0, False),
)
_BIG = _BIG_A + _BIG_B
_SMALL = (
    "ln_mix_g", "shift_mu_prev", "shift_mu_next", "decay_w0_fwd", "decay_w0_bwd", "iclr_a0_fwd", "iclr_a0_bwd", "k_k",
    "k_a", "r_k", "ln_x_g", "ln_x_b", "q_norm_g", "kv_norm_g", "mla_out_g", "ln_ffn_g", "ffn_conv_b", "ln_final_g",
)
_WEIGHTS = (
    "ln_mix_g", "w_in", "shift_mu_prev", "shift_mu_next", "decay_w0_fwd", "decay_w2_fwd", "decay_w0_bwd", "decay_w2_bwd",
    "iclr_a0_fwd", "iclr_a2_fwd", "iclr_a0_bwd", "iclr_a2_bwd", "gate_g2", "k_k", "k_a", "r_k", "ln_x_g", "ln_x_b",
    "q_norm_g", "w_uq", "kv_norm_g", "w_ukv", "mla_out_g", "w_out", "ln_ffn_g", "w_ffn_up", "ffn_conv_w", "ffn_conv_b",
    "w_ffn_down", "ln_final_g",
)


def _pad_rows(flat, rows):
    return jnp.pad(flat, (0, rows * 128 - flat.shape[0])).reshape(rows, 128)


def _rows_for(n, mult):
    rows = -(-n // 128)
    return -(-rows // mult) * mult


def _pack_shards_bf16(arrs, entries):
    parts = []
    for name, _, raw in entries:
        w = arrs[name][0]
        flat = lax.bitcast_convert_type(w, BF16).reshape(-1) if raw else w.astype(BF16).reshape(-1)
        parts.append(_pad_rows(flat, _rows_for(flat.shape[0], 32)))
    return jnp.concatenate(parts, axis=0)


def _unpack_gathered(g4, arrs, entries):
    out, off = {}, 0
    for name, axis, raw in entries:
        a, b = arrs[name].shape[1:]
        n = a * b * (2 if raw else 1)
        rows = _rows_for(n, 32)
        seg = g4[:, off:off + rows].reshape(4, rows * 128)[:, :n]
        off += rows
        if raw:
            seg = lax.bitcast_convert_type(seg.reshape(4, a * b, 2), F32)
        seg = seg.reshape(4, a, b)
        out[name] = jnp.concatenate([seg[s] for s in range(4)], axis=1) if axis == 1 else seg.reshape(4 * a, b)
    return out


def _pack_grads(full, arrs, entries):
    parts = []
    for name, axis, _ in entries:
        a, b = arrs[name].shape[1:]
        g = full[name]
        sh = g.reshape(a, 4, b).transpose(1, 0, 2) if axis == 1 else g.reshape(4, a, b)
        rows = _rows_for(a * b, 8)
        parts.append(jnp.pad(sh.reshape(4, a * b), ((0, 0), (0, rows * 128 - a * b))).reshape(4, rows, 128))
    total = sum(q.shape[1] for q in parts)
    parts.append(jnp.zeros((4, -(-total // 1024) * 1024 - total, 128), F32))
    return jnp.concatenate(parts, axis=1)


def _unpack_grads(g, arrs, entries):
    out, off = {}, 0
    for name, _, _ in entries:
        a, b = arrs[name].shape[1:]
        rows = _rows_for(a * b, 8)
        out[name] = g[off:off + rows].reshape(-1)[:a * b].reshape(a, b)
        off += rows
    return out


def _pack_small(vals):
    flat = jnp.concatenate([vals[n].reshape(-1).astype(F32) for n in _SMALL] + [vals["_loss"].reshape(-1)])
    return _pad_rows(flat, _rows_for(flat.shape[0], 8))


def _unpack_small(buf, arrs):
    flat, out, off = buf.reshape(-1), {}, 0
    for n in _SMALL:
        size = arrs[n].size
        out[n] = flat[off:off + size].reshape(arrs[n].shape)
        off += size
    out["_loss"] = flat[off]
    return out


def _rot_cols(w):
    return jnp.concatenate([-w[..., 16:], w[..., :16]], axis=-1)


def _rot_cols_t(g):
    return jnp.concatenate([g[..., 16:], -g[..., :16]], axis=-1)


def _rope_tables(t):
    inv = jnp.power(ROPE_THETA, -jnp.arange(0, ROPE_DIM, 2, dtype=F32) / ROPE_DIM)
    ang = jnp.arange(t, dtype=F32)[:, None] * inv[None, :]
    one, zero = jnp.ones((t, 64), F32), jnp.zeros((t, 64), F32)
    cs = jnp.concatenate([one, jnp.cos(ang), jnp.cos(ang), zero[:, :32]], axis=1)
    sn = jnp.concatenate([zero, jnp.sin(ang), jnp.sin(ang), zero[:, :32]], axis=1)
    return cs, sn


def _block_diag(a, b):
    za = jnp.zeros_like(a)
    return jnp.concatenate([jnp.concatenate([a, za], axis=1), jnp.concatenate([za, b], axis=1)], axis=0)


def kernel(x, ln_mix_g, w_in, shift_mu_prev, shift_mu_next, decay_w0_fwd, decay_w2_fwd, decay_w0_bwd, decay_w2_bwd, iclr_a0_fwd, iclr_a2_fwd, iclr_a0_bwd, iclr_a2_bwd, gate_g2, k_k, k_a, r_k, ln_x_g, ln_x_b, q_norm_g, w_uq, kv_norm_g, w_ukv, mla_out_g, w_out, ln_ffn_g, w_ffn_up, ffn_conv_w, ffn_conv_b, w_ffn_down, ln_final_g, loss_target, m_ln_mix_g, m_w_in, m_shift_mu_prev, m_shift_mu_next, m_decay_w0_fwd, m_decay_w2_fwd, m_decay_w0_bwd, m_decay_w2_bwd, m_iclr_a0_fwd, m_iclr_a2_fwd, m_iclr_a0_bwd, m_iclr_a2_bwd, m_gate_g2, m_k_k, m_k_a, m_r_k, m_ln_x_g, m_ln_x_b, m_q_norm_g, m_w_uq, m_kv_norm_g, m_w_ukv, m_mla_out_g, m_w_out, m_ln_ffn_g, m_w_ffn_up, m_ffn_conv_w, m_ffn_conv_b, m_w_ffn_down, m_ln_final_g, v_ln_mix_g, v_w_in, v_shift_mu_prev, v_shift_mu_next, v_decay_w0_fwd, v_decay_w2_fwd, v_decay_w0_bwd, v_decay_w2_bwd, v_iclr_a0_fwd, v_iclr_a2_fwd, v_iclr_a0_bwd, v_iclr_a2_bwd, v_gate_g2, v_k_k, v_k_a, v_r_k, v_ln_x_g, v_ln_x_b, v_q_norm_g, v_w_uq, v_kv_norm_g, v_w_ukv, v_mla_out_g, v_w_out, v_ln_ffn_g, v_w_ffn_up, v_ffn_conv_w, v_ffn_conv_b, v_w_ffn_down, v_ln_final_g):
    arrs = dict(locals())
    b, t, d = x.shape
    m = b * t
    x2 = x.reshape(m, d)
    tgt = loss_target.reshape(m, d)
    vec = lambda n: arrs[n].reshape(1, -1)

    core = lax.axis_index("c")
    chip = 2 * lax.axis_index("x") + lax.axis_index("y")
    def unpack(gathered, shard, entries):
        g4 = lax.dynamic_update_index_in_dim(gathered.reshape(N_CHIPS, -1, 128), shard, chip, axis=0)
        return _unpack_gathered(g4, arrs, entries)

    shard_a, shard_b = _pack_shards_bf16(arrs, _BIG_A), _pack_shards_bf16(arrs, _BIG_B)
    fw = unpack(_allgather_weights(shard_a), shard_a, _BIG_A)
    win = fw["w_in"]
    zc = jnp.zeros((d, 64), BF16)
    w_kr = win[:, 2944:2976]
    win_m = jnp.concatenate([win[:, 1920:2944], zc, w_kr, zc[:, :32], zc, _rot_cols(w_kr), zc[:, :32]], axis=1)
    win_r = win[:, :RWKV_COLS]
    head = jnp.arange(512) // HEAD_DIM
    rw = dict(
        w0=jnp.concatenate([vec("decay_w0_fwd"), vec("decay_w0_bwd")], axis=1),
        w2=_block_diag(fw["decay_w2_fwd"], fw["decay_w2_bwd"]).astype(F32),
        a0=jnp.concatenate([vec("iclr_a0_fwd"), vec("iclr_a0_bwd")], axis=1),
        a2=_block_diag(fw["iclr_a2_fwd"], fw["iclr_a2_bwd"]).astype(F32),
        g2=fw["gate_g2"].astype(F32), k_k=vec("k_k"), k_a=vec("k_a"), r_k=vec("r_k"), ln_x_g=vec("ln_x_g"), ln_x_b=vec("ln_x_b"),
        ones_bd=(head[:, None] == head[None, :]).astype(F32),
    )
    cs, sn = _rope_tables(t)

    n1 = _rms_fwd(x2, vec("ln_mix_g"), "rms_mix")
    zm = _mm(n1, win_m, "nn", "proj_in_mla")
    zr = _mm(n1, win_r, "nn", "proj_in_rwkv")
    zs = _shift_fwd(zr.reshape(b, t, RWKV_COLS), vec("shift_mu_prev"), vec("shift_mu_next"))
    zs2 = zs.reshape(m, RWKV_COLS)
    wf, wb, kf, kb, kk, kaf, kab, gate = _prep_fwd(zs2, rw)
    r4 = lambda a: a.reshape(b, t, 512)
    f2 = lambda a: a.reshape(m, 512)
    kk4 = r4(kk)
    ops_f = (r4(wf), r4(kf), r4(kaf))
    ops_b = (r4(wb), r4(kb), r4(kab))
    y_f, hist_f, y_b, hist_b, s_last, gathered_b = _scan_fwd(zs, kk4, ops_f, ops_b, shard_b)
    fw.update(unpack(gathered_b, shard_b, _BIG_B))
    uq = fw["w_uq"].astype(F32).reshape(Q_RANK, HEADS, 96)
    z32 = jnp.zeros((Q_RANK, HEADS, 32), F32)
    wq = jnp.concatenate([uq[..., :64], uq[..., 64:], z32], axis=-1).reshape(Q_RANK, 1024)
    wqr = jnp.concatenate([z32, z32, _rot_cols(uq[..., 64:]), z32], axis=-1).reshape(Q_RANK, 1024)
    ukv = fw["w_ukv"].astype(F32).reshape(KV_RANK, HEADS, 128)
    wk = jnp.concatenate([ukv[..., :64], jnp.zeros_like(ukv[..., :64])], axis=-1).reshape(KV_RANK, 1024)
    wv = ukv[..., 64:].reshape(KV_RANK, 512)
    mp = dict(q_norm_g=vec("q_norm_g"), kv_norm_g=vec("kv_norm_g"), wq=wq, wqr=wqr, wk=wk, wv=wv)
    w_up_g, w_up_v = fw["w_ffn_up"][:, :D_FF], fw["w_ffn_up"][:, D_FF:]
    cw, cb = fw["ffn_conv_w"], vec("ffn_conv_b")
    y_f, y_b = f2(y_f), f2(y_b)
    y_rwkv = _post_fwd(y_f, y_b, zs2, kf, kb, gate, rw)
    q, kfull, v = _mla_fwd(zm, cs, sn, mp, t)
    o, lse = _attn_fwd(q, kfull, v, b, t)
    y_mla = _rms_fwd(o, vec("mla_out_g"), "rms_mla_out")
    ymix = jnp.concatenate([y_rwkv, y_mla], axis=1)
    h1 = _mm(ymix, fw["w_out"], "nn", "proj_out", add=x2)
    n2 = _rms_fwd(h1, vec("ln_ffn_g"), "rms_ffn")
    ug = _mm(n2, w_up_g, "nn", "ffn_up_gate")
    uv = _mm(n2, w_up_v, "nn", "ffn_up_val")
    r3f = lambda a: a.reshape(b, t, D_FF)
    act = _ffn_mid_fwd(r3f(ug), r3f(uv), cw, cb).reshape(m, D_FF)
    h2 = _mm(act, fw["w_ffn_down"], "nn", "ffn_down", add=h1)
    loss_tab, dh2, g_ln_final = _final(h2, vec("ln_final_g"), tgt)

    gfull = {}
    dact = _mm(dh2, fw["w_ffn_down"], "nt", "d_ffn_act")
    gfull["w_ffn_down"] = _mm(act, dh2, "tn", "g_ffn_down")
    dug, duv, tab_g, tab_v = _ffn_mid_bwd(r3f(ug), r3f(uv), cw, cb, r3f(dact))
    dug, duv = dug.reshape(m, D_FF), duv.reshape(m, D_FF)
    gfull["ffn_conv_w"] = jnp.concatenate([tab_g[0:3], tab_v[0:3]], axis=1)
    g_conv_b = jnp.concatenate([tab_g[3:4], tab_v[3:4]], axis=1)
    dn2 = _mm(duv, w_up_v, "nt", "d_ffn_in_val", add=_mm(dug, w_up_g, "nt", "d_ffn_in_gate"))
    gfull["w_ffn_up"] = jnp.concatenate([_mm(n2, dug, "tn", "g_ffn_up_gate"), _mm(n2, duv, "tn", "g_ffn_up_val")], axis=1)
    dh1, g_ln_ffn = _rms_bwd(h1, vec("ln_ffn_g"), dn2, "rms_ffn_bwd", dres=dh2)
    dymix = _mm(dh1, fw["w_out"], "nt", "d_mix")
    gfull["w_out"] = _mm(ymix, dh1, "tn", "g_w_out")
    do, g_mla_out = _rms_bwd(o, vec("mla_out_g"), dymix, "rms_mla_out_bwd", dy_block=1)
    dq, dk, dv = _attn_bwd(q, kfull, v, o, lse, do, b, t)
    dzm, g_qn, g_kvn, g_wq, g_wqr, g_wk, g_wv = _mla_bwd(zm, cs, sn, mp, t, dq, dk, dv)
    gq3, gqr3 = g_wq.reshape(Q_RANK, HEADS, 128), g_wqr.reshape(Q_RANK, HEADS, 128)
    gfull["w_uq"] = jnp.concatenate(
        [gq3[..., :64], gq3[..., 64:96] + _rot_cols_t(gqr3[..., 64:96])], axis=-1
    ).reshape(Q_RANK, HEADS * 96)
    gfull["w_ukv"] = jnp.concatenate(
        [g_wk.reshape(KV_RANK, HEADS, 128)[..., :64], g_wv.reshape(KV_RANK, HEADS, 64)], axis=-1
    ).reshape(KV_RANK, 1024)
    def cores_first(entries, tag):
        packed = _pack_grads(gfull, arrs, entries)
        rh = packed.shape[1] // 2
        own = lax.dynamic_slice_in_dim(packed, core * rh, rh, axis=1)
        sib = _send_to_sibling(packed, True)
        return _add_rows([own.reshape(4 * rh, 128), sib.reshape(4 * rh, 128)], "sum_cores_" + tag, BF16).reshape(4, rh, 128)

    def chips_then_join(chip_part, recv, entries, tag):
        mine = lax.dynamic_index_in_dim(chip_part, chip, axis=0, keepdims=False)
        half = _add_rows([mine, recv[0], recv[1], recv[2]], "sum_chips_" + tag)
        other = _send_to_sibling(half, False)
        lower = jnp.where(core == 0, half, other)
        upper = jnp.where(core == 0, other, half)
        return _unpack_grads(jnp.concatenate([lower, upper], axis=0), arrs, entries)

    part_b = cores_first(_BIG_B, "b")
    dys, dr_p, dk_p, dv_p, dgate, g_rk, g_lnx_g, g_lnx_b = _post_bwd(y_f, y_b, zs2, kf, kb, gate, rw, dymix)
    (dr_f, dwf, dkf, dkk_f, dkaf, dv_f, dr_b, dwb, dkb, dkk_b, dkab, dv_b, recv_b) = _scan_bwd(
        zs, kk4, r4(dys), ops_f, hist_f, ops_b, hist_b, s_last, part_b)
    g_big = chips_then_join(part_b, recv_b, _BIG_B, "b")
    cts = dict(dwf=f2(dwf), dwb=f2(dwb), dkf=f2(dkf), dkb=f2(dkb), dkk_f=f2(dkk_f), dkk_b=f2(dkk_b), dkaf=f2(dkaf), dkab=f2(dkab),
               dr_f=f2(dr_f), dr_b=f2(dr_b), dr_p=dr_p, dk_p=dk_p, dv_p=dv_p, dg=dgate, dv_f=f2(dv_f), dv_b=f2(dv_b))
    dzs, g_w0, g_w2, g_a0, g_a2, g_g2, g_kk, g_ka = _prep_bwd(zs2, rw, cts)
    dzr, g_mu_p, g_mu_n = _shift_bwd(dzs.reshape(b, t, RWKV_COLS), zr.reshape(b, t, RWKV_COLS), vec("shift_mu_prev"), vec("shift_mu_next"))
    dzr = dzr.reshape(m, RWKV_COLS)
    gfull["decay_w2_fwd"], gfull["decay_w2_bwd"] = g_w2[:64, :512], g_w2[64:, 512:]
    gfull["iclr_a2_fwd"], gfull["iclr_a2_bwd"] = g_a2[:64, :512], g_a2[64:, 512:]
    gfull["gate_g2"] = g_g2
    dn1 = _mm(dzr, win_r, "nt", "d_proj_in_rwkv", add=_mm(dzm, win_m, "nt", "d_proj_in_mla"))
    g_m = _mm(n1, dzm, "tn", "g_w_in_mla")
    g_r = _mm(n1, dzr, "tn", "g_w_in_rwkv")
    g_kr = g_m[:, 1088:1120] + _rot_cols_t(g_m[:, 1216:1248])
    gfull["w_in"] = jnp.concatenate([g_r, g_m[:, :1024], g_kr], axis=1)
    dx, g_ln_mix = _rms_bwd(x2, vec("ln_mix_g"), dn1, "rms_mix_bwd", dres=dh1)

    part_a = cores_first(_BIG_A, "a")
    g_big.update(chips_then_join(part_a, _scatter_to_chips(part_a), _BIG_A, "a"))
    small = {
        "ln_mix_g": g_ln_mix, "shift_mu_prev": g_mu_p, "shift_mu_next": g_mu_n, "decay_w0_fwd": g_w0[:, :512],
        "decay_w0_bwd": g_w0[:, 512:], "iclr_a0_fwd": g_a0[:, :512], "iclr_a0_bwd": g_a0[:, 512:], "k_k": g_kk, "k_a": g_ka,
        "r_k": g_rk, "ln_x_g": g_lnx_g, "ln_x_b": g_lnx_b, "q_norm_g": g_qn, "kv_norm_g": g_kvn, "mla_out_g": g_mla_out,
        "ln_ffn_g": g_ln_ffn, "ffn_conv_b": g_conv_b, "ln_final_g": g_ln_final,
        "_loss": jnp.pad(loss_tab[0, 0:1], (0, 127)),
    }
    g_small_buf = _allreduce_small(_pack_small(small))
    g_small = _unpack_small(g_small_buf, arrs)

    grads, deltas, new_m, new_v = {}, {}, {}, {}
    for name, _, _ in _BIG:
        shape = arrs[name].shape
        two = lambda a: a.reshape(shape[1:])
        dlt, m2, v2 = _adamw(two(arrs[name]), g_big[name], two(arrs["m_" + name]), two(arrs["v_" + name]), "adamw_" + name)
        grads[name] = g_big[name].reshape(shape)
        deltas[name], new_m[name], new_v[name] = dlt.reshape(shape), m2.reshape(shape), v2.reshape(shape)
    pk = lambda pre: _pack_small({**{n: arrs[pre + n] for n in _SMALL}, "_loss": jnp.zeros((128,), F32)})
    sd, sm, sv = _adamw(pk(""), g_small_buf, pk("m_"), pk("v_"), "adamw_small")
    sd, sm, sv = _unpack_small(sd, arrs), _unpack_small(sm, arrs), _unpack_small(sv, arrs)
    for n in _SMALL:
        grads[n], deltas[n], new_m[n], new_v[n] = g_small[n], sd[n], sm[n], sv[n]

    return (g_small["_loss"], dx.reshape(b, t, d), *[grads[n] for n in _WEIGHTS], *[deltas[n] for n in _WEIGHTS],
            *[new_m[n] for n in _WEIGHTS], *[new_v[n] for n in _WEIGHTS])
```

```python
import functools
import math

import jax
import jax.numpy as jnp
from jax import lax
from jax.experimental import pallas as pl
from jax.experimental.pallas import tpu as pltpu

F32, BF16 = jnp.float32, jnp.bfloat16
MESH = pl.DeviceIdType.MESH
ANY = pl.BlockSpec(memory_space=pl.ANY)
VMEM = pl.BlockSpec(memory_space=pltpu.VMEM)
BS = pl.BlockSpec
SDS = jax.ShapeDtypeStruct

NORM_EPS = 1e-6
GN_EPS = 64e-5
L2_EPS = 1e-12
HEADS = 8
HEAD_DIM = 64
D_RWKV = HEADS * HEAD_DIM
ROPE_DIM = 32
ROPE_THETA = 10000.0
MLA_SCALE = (64 + ROPE_DIM) ** -0.5
Q_RANK, KV_RANK = 768, 256
RWKV_COLS = 1920
MLA_PAD_COLS = Q_RANK + KV_RANK + 256
D_FF = 2816
ADAM_LR, ADAM_B1, ADAM_B2, ADAM_EPS, ADAM_WD, ADAM_STEP = 0.001, 0.9, 0.999, 1e-08, 0.01, 10

V7X_LANES = 128
V7X_VMEM_LIMIT = 48 * 1024 * 1024
SCAN_CHUNK = 16
N_CHIPS = 4


def _cp(*sem):
    return pltpu.CompilerParams(dimension_semantics=sem, vmem_limit_bytes=V7X_VMEM_LIMIT)


def _tile(n, cands=(512, 640, 384, 256, 128)):
    for c in cands:
        if n % c == 0:
            return c
    return n


def _row_tile(n, cap=256):
    best = n
    for t in range(8, cap + 1, 8):
        if n % t == 0:
            best = t
    return best if best <= cap or n <= cap else n


def _rms(x, g):
    ms = jnp.mean(x * x, axis=-1, keepdims=True)
    return x * lax.rsqrt(ms + NORM_EPS) * g


@jax.custom_vjp
def _bdot(x, w):
    return jnp.dot(x.astype(BF16), w.astype(BF16), preferred_element_type=F32)


def _bdot_fwd(x, w):
    return _bdot(x, w), (x, w)


def _bdot_bwd(res, ct):
    x, w = res
    c = ct.astype(BF16)
    dx = lax.dot_general(c, w.astype(BF16), (((1,), (1,)), ((), ())), preferred_element_type=F32)
    dw = lax.dot_general(x.astype(BF16), c, (((0,), (0,)), ((), ())), preferred_element_type=F32)
    return dx.astype(x.dtype), dw.astype(w.dtype)


_bdot.defvjp(_bdot_fwd, _bdot_bwd)


@jax.custom_vjp
def _headsum(x, ones_bd):
    hi = x.astype(BF16)
    mid = (x - hi.astype(F32)).astype(BF16)
    ob = ones_bd.astype(BF16)
    return jnp.dot(hi, ob, preferred_element_type=F32) + jnp.dot(mid, ob, preferred_element_type=F32)


def _headsum_fwd(x, ones_bd):
    return _headsum(x, ones_bd), ones_bd


def _headsum_bwd(ones_bd, ct):
    return _headsum(ct, ones_bd), jnp.zeros_like(ones_bd)


_headsum.defvjp(_headsum_fwd, _headsum_bwd)


def _prep_fn(zs, w0, w2, a0, a2, g2, k_k, k_a, ones_bd):
    k = zs[:, 512:1024]
    wd = zs[:, 1536:1664]
    ad = zs[:, 1664:1792]
    gd = zs[:, 1792:1920]
    logit = w0 + _bdot(jnp.tanh(wd), w2)
    w = jnp.exp(-math.exp(-0.5) * jax.nn.sigmoid(logit))
    a = jax.nn.sigmoid(a0 + _bdot(ad, a2))
    g = _bdot(jax.nn.sigmoid(gd), g2)
    kkr = k * k_k
    nrm = jnp.sqrt(_headsum(kkr * kkr, ones_bd))
    kk = kkr / jnp.maximum(nrm, L2_EPS)
    a_f, a_b = a[:, :512], a[:, 512:]
    kf = k * (1.0 + (a_f - 1.0) * k_a)
    kb = k * (1.0 + (a_b - 1.0) * k_a)
    return w[:, :512], w[:, 512:], kf, kb, kk, kk * a_f, kk * a_b, g


def _post_fn(y, r, kf, kb, v, g, r_k, ln_g, ln_b, ones_bd):
    mu =_headsum(y, ones_bd) * (1.0 / HEAD_DIM)
    yc = y - mu
    var = _headsum(yc * yc, ones_bd) * (1.0 / HEAD_DIM)
    yn = yc * lax.rsqrt(var + GN_EPS) * ln_g + ln_b
    bonus = _headsum(r * (kf + kb) * r_k, ones_bd) * v
    return (yn + bonus) * g


def _cat8(x):
    return jnp.concatenate([x] * HEADS, axis=1)


def _mla_fn(zm, cs, sn, gq, gkv, wq, wqr, wk, wv):
    cq = zm[:, :Q_RANK]
    ckv = zm[:, Q_RANK:Q_RANK + KV_RANK]
    kr = zm[:, Q_RANK + KV_RANK:Q_RANK + KV_RANK + 128]
    krr = zm[:, Q_RANK + KV_RANK + 128:]
    cqn = _rms(cq, gq)
    ckvn = _rms(ckv, gkv)
    q = (_bdot(cqn, wq) * _cat8(cs) + _bdot(cqn, wqr) * _cat8(sn)) * MLA_SCALE
    kro = kr * cs + krr * sn
    kfull = _bdot(ckvn, wk) + _cat8(kro)
    v = _bdot(ckvn, wv)
    return q, kfull, v


def _adamw_math(w, g, m, v):
    m2 = ADAM_B1 * m + (1.0 - ADAM_B1) * g
    v2 = ADAM_B2 * v + (1.0 - ADAM_B2) * (g * g)
    m_hat = m2 / (1.0 - ADAM_B1 ** ADAM_STEP)
    v_hat = v2 / (1.0 - ADAM_B2 ** ADAM_STEP)
    delta = -ADAM_LR * (m_hat / (jnp.sqrt(v_hat) + ADAM_EPS) + ADAM_WD * w)
    return delta, m2, v2


_DIMS = {"nn": (((1,), (0,)), ((), ())), "nt": (((1,), (1,)), ((), ())), "tn": (((0,), (0,)), ((), ()))}


def _mm(a, b, mode, name, out_dtype=F32, add=None):
    if mode == "nn":
        (m, k), (_, n) = a.shape, b.shape
    elif mode == "nt":
        (m, k), (n, _) = a.shape, b.shape
    else:
        (k, m), (_, n) = a.shape, b.shape
    big = (1024, 1408, 768, 640, 512, 384, 256, 128)
    tm, tn, tk = _tile(m, big), _tile(n, big), _tile(k, (512, 1408, 640, 384, 256, 128))
    nk = k // tk

    def body(a_ref, b_ref, *rest):
        if add is None:
            o_ref, acc_ref = rest
        else:
            add_ref, o_ref, acc_ref = rest
        kk = pl.program_id(2)

        @pl.when(kk == 0)
        def _():
            acc_ref[...] = jnp.zeros_like(acc_ref)

        acc_ref[...] += lax.dot_general(
            a_ref[...].astype(BF16), b_ref[...].astype(BF16), _DIMS[mode], preferred_element_type=F32
        )

        @pl.when(kk == nk - 1)
        def _():
            r = acc_ref[...]
            if add is not None:
                r = r + add_ref[...]
            o_ref[...] = r.astype(out_dtype)

    a_spec = BS((tk, tm), lambda i, j, kk: (kk, i)) if mode == "tn" else BS((tm, tk), lambda i, j, kk: (i, kk))
    b_spec = BS((tn, tk), lambda i, j, kk: (j, kk)) if mode == "nt" else BS((tk, tn), lambda i, j, kk: (kk, j))
    o_spec = BS((tm, tn), lambda i, j, kk: (i, j))
    ins, specs = [a, b], [a_spec, b_spec]
    if add is not None:
        ins.append(add)
        specs.append(o_spec)
    return pl.pallas_call(
        body, grid=(m // tm, n // tn, nk), in_specs=specs, out_specs=o_spec, out_shape=SDS((m, n), out_dtype),
        scratch_shapes=[pltpu.VMEM((tm, tn), F32)], compiler_params=_cp("parallel", "parallel", "arbitrary"), name=name,
    )(*ins)


def _rms_fwd(x, g, name):
    m, d = x.shape
    tr = _tile(m)

    def body(x_ref, g_ref, o_ref):
        o_ref[...] = _rms(x_ref[...], g_ref[...]).astype(BF16)

    return pl.pallas_call(
        body, grid=(m // tr,), in_specs=[BS((tr, d), lambda i: (i, 0)), BS((1, d), lambda i: (0, 0))],
        out_specs=BS((tr, d), lambda i: (i, 0)), out_shape=SDS((m, d), BF16), compiler_params=_cp("parallel"), name=name,
    )(x, g)


def _rms_bwd(x, g, dy, name, dres=None, dy_block=0):
    m, d = x.shape
    tr = _row_tile(m)

    def body(x_ref, g_ref, dy_ref, *rest):
        if dres is None:
            dx_ref, dg_ref = rest
        else:
            dres_ref, dx_ref, dg_ref = rest
        _, vjp = jax.vjp(_rms, x_ref[...], g_ref[...])
        dx, dg = vjp(dy_ref[...])
        if dres is not None:
            dx = dx + dres_ref[...]
        dx_ref[...] = dx

        @pl.when(pl.program_id(0) == 0)
        def _():
            dg_ref[...] = jnp.zeros_like(dg_ref)

        dg_ref[...] += dg

    row = BS((tr, d), lambda i: (i, 0))
    vec = BS((1, d), lambda i: (0, 0))
    ins, specs = [x, g, dy], [row, vec, BS((tr, d), lambda i: (i, dy_block))]
    if dres is not None:
        ins.append(dres)
        specs.append(row)
    return pl.pallas_call(
        body, grid=(m // tr,), in_specs=specs, out_specs=[row, vec], out_shape=[SDS((m, d), F32), SDS((1, d), F32)],
        compiler_params=_cp("arbitrary"), name=name,
    )(*ins)


def _final(h, g, tgt):
    m, d = h.shape
    tr = _row_tile(m)

    def loss_fn(hh, gg, tt):
        e = _rms(hh, gg) - tt
        return 0.5 * jnp.sum(e * e) * (1.0 / d)

    def body(h_ref, g_ref, t_ref, l_ref, dh_ref, dg_ref):
        val, (dh, dg) = jax.value_and_grad(loss_fn, argnums=(0, 1))(h_ref[...], g_ref[...], t_ref[...])
        dh_ref[...] = dh

        @pl.when(pl.program_id(0) == 0)
        def _():
            dg_ref[...] = jnp.zeros_like(dg_ref)
            l_ref[...] = jnp.zeros_like(l_ref)

        dg_ref[...] += dg
        l_ref[...] += jnp.full(l_ref.shape, val, F32)

    row = BS((tr, d), lambda i: (i, 0))
    vec = BS((1, d), lambda i: (0, 0))
    return pl.pallas_call(
        body, grid=(m // tr,), in_specs=[row, vec, row], out_specs=[BS((8, 128), lambda i: (0, 0)), row, vec],
        out_shape=[SDS((8, 128), F32), SDS((m, d), F32), SDS((1, d), F32)], compiler_params=_cp("arbitrary"), name="final_loss",
    )(h, g, tgt)


def _prev_next(z, t):
    row = lax.broadcasted_iota(jnp.int32, z.shape, 0)
    zp = jnp.where(row == 0, 0.0, pltpu.roll(z, 1, axis=0))
    zn = jnp.where(row == t - 1, 0.0, pltpu.roll(z, t - 1, axis=0))
    return zp, zn


def _shift_fwd(z3, mu_p, mu_n):
    b, t, c = z3.shape
    nc = c // 128

    def body(z_ref, mp_ref, mn_ref, o_ref):
        z = z_ref[0]
        zp, zn = _prev_next(z, t)
        o_ref[0] = z + mp_ref[...] * (zp - z) + mn_ref[...] * (zn - z)

    blk = BS((1, t, 128), lambda i, j: (i, 0, j))
    vec = BS((1, 128), lambda i, j: (0, j))
    return pl.pallas_call(
        body, grid=(b, nc), in_specs=[blk, vec, vec], out_specs=blk, out_shape=SDS((b, t, c), F32),
        compiler_params=_cp("parallel", "parallel"), name="shift_fwd",
    )(z3, mu_p, mu_n)


def _shift_bwd(dzs3, z3, mu_p, mu_n):
    b, t, c = z3.shape
    nc = c // 128

    def body(d_ref, z_ref, mp_ref, mn_ref, dz_ref, dmp_ref, dmn_ref):
        d, z = d_ref[0], z_ref[0]
        mp, mn = mp_ref[...], mn_ref[...]
        zp, zn = _prev_next(z, t)
        _, dp_next = _prev_next(d * mp, t)
        dn_prev, _ = _prev_next(d * mn, t)
        dz_ref[0] = (d * (1.0 - mp - mn) + dp_next + dn_prev).astype(BF16)

        @pl.when(pl.program_id(1) == 0)
        def _():
            dmp_ref[...] = jnp.zeros_like(dmp_ref)
            dmn_ref[...] = jnp.zeros_like(dmn_ref)

        dmp_ref[...] += jnp.sum(d * (zp - z), axis=0, keepdims=True)
        dmn_ref[...] += jnp.sum(d * (zn - z), axis=0, keepdims=True)

    blk = BS((1, t, 128), lambda j, i: (i, 0, j))
    vec = BS((1, 128), lambda j, i: (0, j))
    return pl.pallas_call(
        body, grid=(nc, b), in_specs=[blk, blk, vec, vec], out_specs=[blk, vec, vec],
        out_shape=[SDS((b, t, c), BF16), SDS((1, c), F32), SDS((1, c), F32)],
        compiler_params=_cp("parallel", "arbitrary"), name="shift_bwd",
    )(dzs3, z3, mu_p, mu_n)


def _const(shape):
    nd = len(shape)
    return BS(shape, lambda i: (0,) * nd)


def _prep_fwd(zs, p):
    m = zs.shape[0]
    tr = 256
    params = [p["w0"], p["w2"], p["a0"], p["a2"], p["g2"], p["k_k"], p["k_a"], p["ones_bd"]]

    def body(zs_ref, w0, w2, a0, a2, g2, kk_, ka_, bd, wf, wb, kf, kb, kk, kaf, kab, g):
        outs = _prep_fn(zs_ref[...], w0[...], w2[...], a0[...], a2[...], g2[...], kk_[...], ka_[...], bd[...])
        for ref, val in zip((wf, wb, kf, kb, kk, kaf, kab, g), outs):
            ref[...] = val

    row = BS((tr, 512), lambda i: (i, 0))
    return pl.pallas_call(
        body, grid=(m // tr,), in_specs=[BS((tr, RWKV_COLS), lambda i: (i, 0))] + [_const(q.shape) for q in params],
        out_specs=[row] * 8, out_shape=[SDS((m, 512), F32)] * 8, compiler_params=_cp("parallel"), name="rwkv_prep_fwd",
    )(zs, *params)


def _prep_bwd(zs, p, ct_rows):
    m = zs.shape[0]
    tr = 128
    params = [p["w0"], p["w2"], p["a0"], p["a2"], p["g2"], p["k_k"], p["k_a"]]
    names = ["dwf", "dwb", "dkf", "dkb", "dkk_f", "dkk_b", "dkaf", "dkab", "dr_f", "dr_b", "dr_p", "dk_p", "dv_p", "dg",
             "dv_f", "dv_b"]
    rows = [ct_rows[n] for n in names]

    def body(zs_ref, w0, w2, a0, a2, g2, kk_, ka_, bd, *rest):
        c = {n: r[...] for n, r in zip(names, rest[:len(names)])}
        outs = rest[len(names):]
        dzs_ref, grads = outs[0], outs[1:]
        ones_bd = bd[...]
        _, vjp = jax.vjp(
            lambda *q: _prep_fn(*q, ones_bd), zs_ref[...], w0[...], w2[...], a0[...], a2[...], g2[...], kk_[...], ka_[...]
        )
        cts = (c["dwf"], c["dwb"], c["dkf"] + c["dk_p"], c["dkb"] + c["dk_p"], c["dkk_f"] + c["dkk_b"], c["dkaf"], c["dkab"], c["dg"])
        dzs, *dparams = vjp(cts)
        dr = c["dr_f"] + c["dr_b"] + c["dr_p"]
        dv = c["dv_f"] + c["dv_b"] + c["dv_p"]
        dzs_ref[:, 0:512] = dzs[:, 0:512] + dr
        dzs_ref[:, 512:1024] = dzs[:, 512:1024]
        dzs_ref[:, 1024:1536] = dzs[:, 1024:1536] + dv
        dzs_ref[:, 1536:1920] = dzs[:, 1536:1920]

        @pl.when(pl.program_id(0) == 0)
        def _():
            for gr in grads:
                gr[...] = jnp.zeros_like(gr)

        for gr, val in zip(grads, dparams):
            gr[...] += val

    row = BS((tr, 512), lambda i: (i, 0))
    return pl.pallas_call(
        body, grid=(m // tr,),
        in_specs=[BS((tr, RWKV_COLS), lambda i: (i, 0))] + [_const(q.shape) for q in params] + [_const(p["ones_bd"].shape)]
        + [row] * len(names),
        out_specs=[BS((tr, RWKV_COLS), lambda i: (i, 0))] + [_const(q.shape) for q in params],
        out_shape=[SDS((m, RWKV_COLS), F32)] + [SDS(q.shape, F32) for q in params],
        compiler_params=_cp("arbitrary"), name="rwkv_prep_bwd",
    )(zs, *params, p["ones_bd"], *rows)


def _post_specs(tr):
    r = BS((tr, 512), lambda i: (i, 0))
    v = BS((tr, 512), lambda i: (i, 2))
    row = BS((tr, 512), lambda i: (i, 0))
    return r, v, row


def _post_fwd(y_f, y_b, zs, kf, kb, g, p):
    m = zs.shape[0]
    tr = 256
    r, v, row = _post_specs(tr)
    vecs = [p["r_k"], p["ln_x_g"], p["ln_x_b"], p["ones_bd"]]

    def body(yf, yb, r_ref, v_ref, kf_ref, kb_ref, g_ref, rk, lg, lb, bd, o_ref):
        o_ref[...] = _post_fn(
            yf[...] + yb[...], r_ref[...], kf_ref[...], kb_ref[...], v_ref[...], g_ref[...], rk[...], lg[...], lb[...], bd[...]
        ).astype(BF16)

    return pl.pallas_call(
        body, grid=(m // tr,), in_specs=[row, row, r, v, row, row, row] + [_const(q.shape) for q in vecs],
        out_specs=row, out_shape=SDS((m, 512), BF16), compiler_params=_cp("parallel"), name="rwkv_post_fwd",
    )(y_f, y_b, zs, zs, kf, kb, g, *vecs)


def _post_bwd(y_f, y_b, zs, kf, kb, g, p, dymix):
    m = zs.shape[0]
    tr = 128
    r, v, row = _post_specs(tr)
    vecs = [p["r_k"], p["ln_x_g"], p["ln_x_b"]]

    def body(yf, yb, r_ref, v_ref, kf_ref, kb_ref, g_ref, rk, lg, lb, bd, dy_ref, dyo, dr, dk, dv, dg, drk, dlg, dlb):
        ones_bd = bd[...]
        _, vjp = jax.vjp(
            lambda *q: _post_fn(*q, ones_bd),
            yf[...] + yb[...], r_ref[...], kf_ref[...], kb_ref[...], v_ref[...], g_ref[...], rk[...], lg[...], lb[...],
        )
        c_y, c_r, c_kf, _, c_v, c_g, c_rk, c_lg, c_lb = vjp(dy_ref[...])
        dyo[...] = c_y
        dr[...] = c_r
        dk[...] = c_kf
        dv[...] = c_v
        dg[...] = c_g

        @pl.when(pl.program_id(0) == 0)
        def _():
            for ref in (drk, dlg, dlb):
                ref[...] = jnp.zeros_like(ref)

        drk[...] += c_rk
        dlg[...] += c_lg
        dlb[...] += c_lb

    vec = _const((1, 512))
    return pl.pallas_call(
        body, grid=(m // tr,),
        in_specs=[row, row, r, v, row, row, row] + [_const(q.shape) for q in vecs] + [_const(p["ones_bd"].shape), row],
        out_specs=[row, row, row, row, row, vec, vec, vec],
        out_shape=[SDS((m, 512), F32)] * 5 + [SDS((1, 512), F32)] * 3,
        compiler_params=_cp("arbitrary"), name="rwkv_post_bwd",
    )(y_f, y_b, zs, zs, kf, kb, g, *vecs, p["ones_bd"], dymix)


def _half_ones():
    ri = lax.broadcasted_iota(jnp.int32, (128, 128), 0)
    ci = lax.broadcasted_iota(jnp.int32, (128, 128), 1)
    return jnp.where((ri < 64) == (ci < 64), 1.0, 0.0).astype(BF16)


def _half_sums(xs, ones):
    res = jnp.dot(jnp.concatenate(xs, axis=0).astype(BF16), ones, preferred_element_type=F32)
    return [res[64 * i:64 * i + 64] for i in range(len(xs))]


def _scan_specs(b, t):
    nc = t // SCAN_CHUNK
    up, down = (lambda c: c), (lambda c: nc - 1 - c)
    rows = [BS((b, SCAN_CHUNK, 512), lambda c, ci=ci: (0, ci(c), 0)) for ci in (up, down)]
    vrows = [BS((b, SCAN_CHUNK, 512), lambda c, ci=ci: (0, ci(c), 2)) for ci in (up, down)]
    hist = [BS((SCAN_CHUNK, b * 4, 64, 128), lambda c, ci=ci: (ci(c), 0, 0, 0)) for ci in (up, down)]
    return nc, rows, vrows, hist


class _Window:
    def __init__(self, g, ascending):
        self.bases = [pl.multiple_of(g * 8, 8) if asc else pl.multiple_of(SCAN_CHUNK - 8 - g * 8, 8) for asc in ascending]
        self.ascending = ascending
        self.blocks = {}
        self.row_id = lax.broadcasted_iota(jnp.int32, (8, 128), 0)

    def j(self, d, s):
        return s if self.ascending[d] else 7 - s

    def time(self, d, s):
        return self.bases[d] + self.j(d, s)

    def row(self, ref, d, bi, cols, s):
        key = (id(ref), d, bi, cols.start)
        if key not in self.blocks:
            self.blocks[key] = ref[bi, pl.ds(self.bases[d], 8), cols]
        jj = self.j(d, s)
        return self.blocks[key][jj:jj + 1, :]

    def put(self, buf, key, d, s, row):
        prev = buf.get(key)
        new = jnp.broadcast_to(row, (8, 128))
        buf[key] = new if prev is None else jnp.where(self.row_id == self.j(d, s), new, prev)

    def flush(self, buf, refs_of):
        for key, val in buf.items():
            ref, d, bi, cols = refs_of(key)
            ref[bi, pl.ds(self.bases[d], 8), cols] = val


def _pairs(b):
    return [(bi * 4 + p, bi, slice(128 * p, 128 * p + 128)) for bi in range(b) for p in range(4)]


def _colsum(x):
    return jnp.sum(x, axis=0, keepdims=True)


def _eye_mask():
    return (lax.broadcasted_iota(jnp.int32, (64, 128), 1) & 63) == lax.broadcasted_iota(jnp.int32, (64, 128), 0)


def _scan_fwd(zs, kk, ops_f, ops_b, shard):
    b, t = zs.shape[:2]
    nc, rows, vrows, hist = _scan_specs(b, t)
    npair = b * 4

    def body(*refs):
        ins, shard_ref, outs, s_ref = refs[:12], refs[12], refs[13:17], refs[17]
        gather = (shard_ref, *refs[18:21])
        dirs = [dict(zip(("r", "kk", "v", "w", "k", "ka", "y", "h"), (*ins[6 * d:6 * d + 6], *outs[2 * d:2 * d + 2])))
                for d in (0, 1)]

        @pl.when(pl.program_id(0) == 0)
        def _():
            s_ref[...] = jnp.zeros_like(s_ref)
            _gather_halves(*gather, "start")

        @pl.when(pl.program_id(0) == nc - 1)
        def _():
            _gather_halves(*gather, "finish")

        ones, eye = _half_ones(), _eye_mask()
        chains = [(d, pr, bi, cols) for d in (0, 1) for pr, bi, cols in _pairs(b)]

        def eight_steps(g, carry):
            win = _Window(g, (True, False))
            ybuf = {}
            for s in range(8):
                s_prev, xa = [], []
                for d, pr, bi, cols in chains:
                    q = dirs[d]
                    st = s_ref[d * npair + pr]
                    q["h"][win.time(d, s), pr] = st
                    s_prev.append(st)
                    xa += [st * win.row(q["kk"], d, bi, cols, s), jnp.where(eye, win.row(q["v"], d, bi, cols, s), 0.0)]
                ra = _half_sums(xa, ones)
                xb = []
                for i, (d, pr, bi, cols) in enumerate(chains):
                    q = dirs[d]
                    s_new = s_prev[i] * win.row(q["w"], d, bi, cols, s) - ra[2 * i] * win.row(q["ka"], d, bi, cols, s) \
                        + ra[2 * i + 1] * win.row(q["k"], d, bi, cols, s)
                    s_ref[d * npair + pr] = s_new
                    xb.append(s_new * win.row(q["r"], d, bi, cols, s))
                rb = _half_sums(xb, ones)
                for i, (d, pr, bi, cols) in enumerate(chains):
                    win.put(ybuf, i, d, s, _colsum(jnp.where(eye, rb[i], 0.0)))
            win.flush(ybuf, lambda i: (dirs[chains[i][0]]["y"], chains[i][0], chains[i][2], chains[i][3]))
            return carry

        lax.fori_loop(0, SCAN_CHUNK // 8, eight_steps, 0)

    row_shape, hist_shape = SDS((b, t, 512), F32), SDS((t, npair, 64, 128), F32)
    state = (2 * npair, 64, 128)
    return pl.pallas_call(
        body, grid=(nc,), in_specs=sum(([rows[d], rows[d], vrows[d]] + [rows[d]] * 3 for d in (0, 1)), []) + [ANY],
        out_specs=[rows[0], hist[0], rows[1], hist[1], BS(state, lambda c: (0, 0, 0)), ANY],
        out_shape=[row_shape, hist_shape, row_shape, hist_shape, SDS(state, F32), _gathered_shape(shard)],
        scratch_shapes=_gather_sems(), compiler_params=_cp("arbitrary"), name="wkv_scan",
    )(zs, kk, zs, *ops_f, zs, kk, zs, *ops_b, shard)


def _scan_bwd(zs, kk, dy, ops_f, hist_f, ops_b, hist_b, s_last, partials):
    b, t = zs.shape[:2]
    nc, rows, vrows, hist = _scan_specs(b, t)
    npair = b * 4
    names_in = ("r", "kk", "v", "dy", "w", "k", "ka", "h")
    names_out = ("dr", "dw", "dk", "dkk", "dka", "dv")

    def body(*refs):
        ins, last_ref, part_ref, outs, recv_ref = refs[:16], refs[16], refs[17], refs[18:30], refs[30]
        ds_ref, after_ref = refs[31], refs[32]
        scatter = (part_ref, recv_ref, refs[33], refs[34])
        dirs = [dict(zip(names_in + names_out, (*ins[8 * d:8 * d + 8], *outs[6 * d:6 * d + 6]))) for d in (0, 1)]

        @pl.when(pl.program_id(0) == 0)
        def _():
            ds_ref[...] = jnp.zeros_like(ds_ref)
            after_ref[...] = last_ref[...]
            _scatter_partials(*scatter, "start")

        @pl.when(pl.program_id(0) == nc - 1)
        def _():
            _scatter_partials(*scatter, "finish")

        ones, eye = _half_ones(), _eye_mask()
        chains = [(d, pr, bi, cols) for d in (0, 1) for pr, bi, cols in _pairs(b)]

        def eight_steps(g, carry):
            win = _Window(g, (False, True))
            obuf = {}
            s_after = [after_ref[d * npair + pr] for d, pr, _, _ in chains]
            for s in range(8):
                row = lambda name, d, bi, cols: win.row(dirs[d][name], d, bi, cols, s)
                s_prev, xa = [], []
                for d, pr, bi, cols in chains:
                    st = dirs[d]["h"][win.time(d, s), pr]
                    s_prev.append(st)
                    xa += [st * row("kk", d, bi, cols), jnp.where(eye, row("v", d, bi, cols), 0.0),
                           jnp.where(eye, row("dy", d, bi, cols), 0.0)]
                ra = _half_sums(xa, ones)
                ds_now, xb = [], []
                for i, (d, pr, bi, cols) in enumerate(chains):
                    skk, vcol, dycol = ra[3 * i], ra[3 * i + 1], ra[3 * i + 2]
                    ds = ds_ref[d * npair + pr] + dycol * row("r", d, bi, cols)
                    win.put(obuf, (i, "dr"), d, s, _colsum(s_after[i] * dycol))
                    win.put(obuf, (i, "dk"), d, s, _colsum(ds * vcol))
                    win.put(obuf, (i, "dka"), d, s, -_colsum(ds * skk))
                    win.put(obuf, (i, "dw"), d, s, _colsum(ds * s_prev[i]))
                    ds_now.append(ds)
                    xb += [ds * row("k", d, bi, cols), ds * row("ka", d, bi, cols)]
                rb = _half_sums(xb, ones)
                for i, (d, pr, bi, cols) in enumerate(chains):
                    dskk_neg = rb[2 * i + 1]
                    win.put(obuf, (i, "dv"), d, s, _colsum(jnp.where(eye, rb[2 * i], 0.0)))
                    win.put(obuf, (i, "dkk"), d, s, -_colsum(s_prev[i] * dskk_neg))
                    ds_ref[d * npair + pr] = ds_now[i] * row("w", d, bi, cols) - dskk_neg * row("kk", d, bi, cols)
                s_after = s_prev
            for i, (d, pr, _, _) in enumerate(chains):
                after_ref[d * npair + pr] = s_after[i]
            win.flush(obuf, lambda key: (dirs[chains[key[0]][0]][key[1]], chains[key[0]][0], chains[key[0]][2], chains[key[0]][3]))
            return carry

        lax.fori_loop(0, SCAN_CHUNK // 8, eight_steps, 0)

    row_shape = SDS((b, t, 512), F32)
    state = (2 * npair, 64, 128)
    return pl.pallas_call(
        body, grid=(nc,),
        in_specs=sum(([rows[d], rows[d], vrows[d]] + [rows[d]] * 4 + [hist[d]] for d in (1, 0)), [])
        + [BS(state, lambda c: (0, 0, 0)), ANY],
        out_specs=[rows[1]] * 6 + [rows[0]] * 6 + [ANY],
        out_shape=[row_shape] * 12 + [SDS((3,) + partials.shape[1:], partials.dtype)],
        scratch_shapes=[pltpu.VMEM(state, F32), pltpu.VMEM(state, F32)] + _scatter_sems(),
        compiler_params=_cp("arbitrary"), name="wkv_scan_bwd",
    )(zs, kk, zs, dy, *ops_f, hist_f, zs, kk, zs, dy, *ops_b, hist_b, s_last, partials)


def _mla_fwd(zm, cs, sn, p, t):
    m = zm.shape[0]
    tr = 256
    per = t // tr
    params = [p["q_norm_g"], p["kv_norm_g"], p["wq"], p["wqr"], p["wk"], p["wv"]]

    def body(z_ref, cs_ref, sn_ref, gq, gkv, wq, wqr, wk, wv, q_ref, k_ref, v_ref):
        q, kf, v = _mla_fn(z_ref[...], cs_ref[...], sn_ref[...], gq[...], gkv[...], wq[...], wqr[...], wk[...], wv[...])
        q_ref[...] = q.astype(BF16)
        k_ref[...] = kf.astype(BF16)
        v_ref[...] = v.astype(BF16)

    tab = BS((tr, 128), lambda i: (i % per, 0))
    return pl.pallas_call(
        body, grid=(m // tr,), in_specs=[BS((tr, MLA_PAD_COLS), lambda i: (i, 0)), tab, tab] + [_const(q.shape) for q in params],
        out_specs=[BS((tr, 1024), lambda i: (i, 0)), BS((tr, 1024), lambda i: (i, 0)), BS((tr, 512), lambda i: (i, 0))],
        out_shape=[SDS((m, 1024), BF16), SDS((m, 1024), BF16), SDS((m, 512), BF16)], compiler_params=_cp("parallel"), name="mla_prep_fwd",
    )(zm, cs, sn, *params)


def _mla_bwd(zm, cs, sn, p, t, dq, dk, dv):
    m = zm.shape[0]
    tr = 128
    per = t // tr
    params = [p["q_norm_g"], p["kv_norm_g"], p["wq"], p["wqr"], p["wk"], p["wv"]]

    def body(z_ref, cs_ref, sn_ref, gq, gkv, wq, wqr, wk, wv, dq_ref, dk_ref, dv_ref, dz_ref, *grads):
        cs_v, sn_v = cs_ref[...], sn_ref[...]
        _, vjp = jax.vjp(
            lambda *q: _mla_fn(q[0], cs_v, sn_v, *q[1:]), z_ref[...], gq[...], gkv[...], wq[...], wqr[...], wk[...], wv[...]
        )
        dz, *dparams = vjp((dq_ref[...], dk_ref[...], dv_ref[...]))
        dz_ref[...] = dz.astype(BF16)

        @pl.when(pl.program_id(0) == 0)
        def _():
            for gr in grads:
                gr[...] = jnp.zeros_like(gr)

        for gr, val in zip(grads, dparams):
            gr[...] += val

    tab = BS((tr, 128), lambda i: (i % per, 0))
    wide = BS((tr, 1024), lambda i: (i, 0))
    return pl.pallas_call(
        body, grid=(m // tr,),
        in_specs=[BS((tr, MLA_PAD_COLS), lambda i: (i, 0)), tab, tab] + [_const(q.shape) for q in params]
        + [wide, wide, BS((tr, 512), lambda i: (i, 0))],
        out_specs=[BS((tr, MLA_PAD_COLS), lambda i: (i, 0))] + [_const(q.shape) for q in params],
        out_shape=[SDS((m, MLA_PAD_COLS), BF16)] + [SDS(q.shape, F32) for q in params],
        compiler_params=_cp("arbitrary"), name="mla_prep_bwd",
    )(zm, cs, sn, *params, dq, dk, dv)


_NT = (((1,), (1,)), ((), ()))
_TN = (((0,), (0,)), ((), ()))


def _attn_fwd(q, kf, v, b, t):
    m = q.shape[0]
    tq = 256
    nq = t // tq

    def body(q_ref, k_ref, v_ref, o_ref, l_ref):
        lo = lax.broadcasted_iota(jnp.int32, (1, 128), 1) < 64
        v_all = v_ref[...]
        o = jnp.zeros((tq, 128), F32)
        lse = []
        for h in range(2):
            hs = slice(128 * h, 128 * h + 128)
            s = lax.dot_general(q_ref[:, hs], k_ref[:, hs], _NT, preferred_element_type=F32)
            mx = jnp.max(s, axis=1, keepdims=True)
            e = jnp.exp(s - mx)
            den = jnp.sum(e, axis=1, keepdims=True)
            vh = jnp.where(lo if h == 0 else jnp.logical_not(lo), v_all, jnp.zeros_like(v_all))
            o = o + jnp.dot(e.astype(BF16), vh, preferred_element_type=F32) / den
            lse.append(mx + jnp.log(den))
        o_ref[...] = o
        l_ref[...] = jnp.where(lo, lse[0], lse[1])

    return pl.pallas_call(
        body, grid=(b, 4, nq),
        in_specs=[BS((tq, 256), lambda bi, hp, i: (bi * nq + i, hp)), BS((t, 256), lambda bi, hp, i: (bi, hp)),
                  BS((t, 128), lambda bi, hp, i: (bi, hp))],
        out_specs=[BS((tq, 128), lambda bi, hp, i: (bi * nq + i, hp))] * 2,
        out_shape=[SDS((m, 512), F32), SDS((m, 512), F32)], compiler_params=_cp("parallel", "parallel", "arbitrary"), name="attn_fwd",
    )(q, kf, v)


def _attn_bwd(q, kf, v, o, lse, do, b, t):
    m = q.shape[0]
    tq = 256
    nq = t // tq

    def body(q_ref, k_ref, v_ref, o_ref, l_ref, do_ref, dq_ref, dk_ref, dv_ref):
        lo = lax.broadcasted_iota(jnp.int32, (1, 128), 1) < 64

        @pl.when(pl.program_id(2) == 0)
        def _():
            dk_ref[...] = jnp.zeros_like(dk_ref)
            dv_ref[...] = jnp.zeros_like(dv_ref)

        v_all, o_all, l_all, do_all = v_ref[...], o_ref[...], l_ref[...], do_ref[...]
        dv_acc = jnp.zeros((t, 128), F32)
        for h in range(2):
            hs = slice(128 * h, 128 * h + 128)
            mask = lo if h == 0 else jnp.logical_not(lo)
            qh, kh = q_ref[:, hs], k_ref[:, hs]
            s = lax.dot_general(qh, kh, _NT, preferred_element_type=F32)
            lse_h = jnp.max(jnp.where(mask, l_all, -jnp.inf), axis=1, keepdims=True)
            pr = jnp.exp(s - lse_h)
            do_h = jnp.where(mask, do_all, 0.0)
            dp = lax.dot_general(do_h.astype(BF16), v_all, _NT, preferred_element_type=F32)
            dsum = jnp.sum(do_h * o_all, axis=1, keepdims=True)
            ds = (pr * (dp - dsum)).astype(BF16)
            dq_ref[:, hs] = jnp.dot(ds, kh, preferred_element_type=F32)
            dk_ref[:, hs] += lax.dot_general(ds, qh, _TN, preferred_element_type=F32)
            dv_acc = dv_acc + lax.dot_general(pr.astype(BF16), do_h.astype(BF16), _TN, preferred_element_type=F32)
        dv_ref[...] += dv_acc

    qspec = BS((tq, 256), lambda bi, hp, i: (bi * nq + i, hp))
    kspec = BS((t, 256), lambda bi, hp, i: (bi, hp))
    vspec = BS((t, 128), lambda bi, hp, i: (bi, hp))
    ospec = BS((tq, 128), lambda bi, hp, i: (bi * nq + i, hp))
    return pl.pallas_call(
        body, grid=(b, 4, nq), in_specs=[qspec, kspec, vspec, ospec, ospec, ospec], out_specs=[qspec, kspec, vspec],
        out_shape=[SDS((m, 1024), F32), SDS((m, 1024), F32), SDS((m, 512), F32)],
        compiler_params=_cp("parallel", "parallel", "arbitrary"), name="attn_bwd",
    )(q, kf, v, o, lse, do)


def _conv3(u, w_ref, b_ref, t):
    up, un = _prev_next(u, t)
    return w_ref[0:1, :] * up + w_ref[1:2, :] * u + w_ref[2:3, :] * un + b_ref[...], up, un


def _ffn_mid_fwd(ug3, uv3, cw, cb):
    b, t, f = ug3.shape
    nc = f // 256

    def body(ug_ref, uv_ref, wg_ref, wv_ref, bg_ref, bv_ref, a_ref):
        gc, _, _ = _conv3(ug_ref[0], wg_ref, bg_ref, t)
        vc, _, _ = _conv3(uv_ref[0], wv_ref, bv_ref, t)
        a_ref[0] = (gc * jax.nn.sigmoid(gc) * vc).astype(BF16)

    blk = BS((1, t, 256), lambda i, j: (i, 0, j))
    return pl.pallas_call(
        body, grid=(b, nc),
        in_specs=[blk, blk, BS((3, 256), lambda i, j: (0, j)), BS((3, 256), lambda i, j: (0, j + nc)),
                  BS((1, 256), lambda i, j: (0, j)), BS((1, 256), lambda i, j: (0, j + nc))],
        out_specs=blk, out_shape=SDS((b, t, f), BF16), compiler_params=_cp("parallel", "parallel"), name="ffn_mid_fwd",
    )(ug3, uv3, cw, cw, cb, cb)


def _ffn_mid_bwd(ug3, uv3, cw, cb, da3):
    b, t, f = ug3.shape
    nc = f // 256

    def half(u, up, un, dc, w_ref):
        dprev, dnext = _prev_next(dc, t)
        du = w_ref[1:2, :] * dc + w_ref[0:1, :] * dnext + w_ref[2:3, :] * dprev
        sums = [jnp.sum(dc * q, axis=0, keepdims=True) for q in (up, u, un)] + [jnp.sum(dc, axis=0, keepdims=True)]
        row = lax.broadcasted_iota(jnp.int32, (8, 256), 0)
        tab = jnp.zeros((8, 256), F32)
        for i, s in enumerate(sums):
            tab = jnp.where(row == i, s, tab)
        return du, tab

    def body(ug_ref, uv_ref, wg_ref, wv_ref, bg_ref, bv_ref, da_ref, dug_ref, duv_ref, tg_ref, tv_ref):
        ug, uv, da = ug_ref[0], uv_ref[0], da_ref[0]
        gc, gp, gn = _conv3(ug, wg_ref, bg_ref, t)
        vc, vp, vn = _conv3(uv, wv_ref, bv_ref, t)
        sg = jax.nn.sigmoid(gc)
        d_gc = da * vc * (sg * (1.0 + gc * (1.0 - sg)))
        d_vc = da * (gc * sg)
        dug, tg = half(ug, gp, gn, d_gc, wg_ref)
        duv, tv = half(uv, vp, vn, d_vc, wv_ref)
        dug_ref[0] = dug.astype(BF16)
        duv_ref[0] = duv.astype(BF16)

        @pl.when(pl.program_id(1) == 0)
        def _():
            tg_ref[...] = jnp.zeros_like(tg_ref)
            tv_ref[...] = jnp.zeros_like(tv_ref)

        tg_ref[...] += tg
        tv_ref[...] += tv

    blk = BS((1, t, 256), lambda j, i: (i, 0, j))
    tab = BS((8, 256), lambda j, i: (0, j))
    return pl.pallas_call(
        body, grid=(nc, b),
        in_specs=[blk, blk, BS((3, 256), lambda j, i: (0, j)), BS((3, 256), lambda j, i: (0, j + nc)),
                  BS((1, 256), lambda j, i: (0, j)), BS((1, 256), lambda j, i: (0, j + nc)), blk],
        out_specs=[blk, blk, tab, tab],
        out_shape=[SDS((b, t, f), BF16), SDS((b, t, f), BF16), SDS((8, f), F32), SDS((8, f), F32)],
        compiler_params=_cp("parallel", "arbitrary"), name="ffn_mid_bwd",
    )(ug3, uv3, cw, cw, cb, cb, da3)


def _add_rows(parts, name, out_dtype=F32):
    r = parts[0].shape[0]
    tr = _row_tile(r, 1024)
    n = len(parts)

    def body(*refs):
        acc = refs[0][...].astype(F32)
        for q in refs[1:n]:
            acc = acc + q[...].astype(F32)
        refs[n][...] = acc.astype(out_dtype)

    row = BS((tr, 128), lambda i: (i, 0))
    return pl.pallas_call(
        body, grid=(r // tr,), in_specs=[row] * n, out_specs=row, out_shape=SDS((r, 128), out_dtype),
        compiler_params=_cp("parallel"), name=name,
    )(*parts)


def _adamw(w, g, m, v, name):
    lead = w.shape[:-2]
    r, c = w.shape[-2:]
    tr = _row_tile(r)

    def body(w_ref, g_ref, m_ref, v_ref, d_ref, m2_ref, v2_ref):
        d, m2, v2 = _adamw_math(w_ref[...], g_ref[...], m_ref[...], v_ref[...])
        d_ref[...] = d
        m2_ref[...] = m2
        v2_ref[...] = v2

    blk = BS((1,) * len(lead) + (tr, c), lambda i: (0,) * len(lead) + (i, 0))
    return pl.pallas_call(
        body, grid=(r // tr,), in_specs=[blk] * 4, out_specs=[blk] * 3, out_shape=[SDS(w.shape, F32)] * 3,
        compiler_params=_cp("parallel"), name=name,
    )(w, g, m, v)


def _place():
    return lax.axis_index("x"), lax.axis_index("y"), lax.axis_index("c")


def _flip(v, bit):
    return 1 - v if bit else v


def _allgather_weights(shard):
    def body(x_ref, out_ref, send_sems, recv_sems):
        _gather_halves(x_ref, out_ref, send_sems, recv_sems, "start")
        _gather_halves(x_ref, out_ref, send_sems, recv_sems, "finish")

    return pl.pallas_call(
        body, out_shape=_gathered_shape(shard), in_specs=[ANY], out_specs=ANY, scratch_shapes=_gather_sems(),
        name="allgather_weights",
    )(shard)


def _gathered_shape(shard):
    return SDS((8 * (shard.shape[0] // 2), 128), shard.dtype)


def _gather_sems():
    return [pltpu.SemaphoreType.DMA((6,)), pltpu.SemaphoreType.DMA((6,))]


def _gather_halves(x_ref, out_ref, send_sems, recv_sems, phase):
    rh = x_ref.shape[0] // 2
    x, y, c = _place()
    me, sibling = (x, y, c), (x, y, 1 - c)
    chips = [(1 - x, y), (x, 1 - y), (1 - x, 1 - y)]
    mine_src = x_ref.at[pl.ds(c * rh, rh), :]

    def rows(px, py, pc):
        return out_ref.at[pl.ds((4 * px + 2 * py + pc) * rh, rh), :]

    def copy(k, block, to, src=None):
        return pltpu.make_async_remote_copy(
            src_ref=rows(*block) if src is None else src, dst_ref=rows(*block), send_sem=send_sems.at[k],
            recv_sem=recv_sems.at[k], device_id=to, device_id_type=MESH,
        )

    first = [copy(j, me, (*chip, c), src=mine_src) for j, chip in enumerate(chips)]
    if phase == "start":
        for cp in first:
            cp.start()
        return
    passed = [copy(3 + j, (*chip, c), sibling) for j, chip in enumerate(chips)]
    for j, chip in enumerate(chips):
        copy(j, (*chip, c), me).wait_recv()
        passed[j].start()
    for j, chip in enumerate(chips):
        copy(3 + j, (*chip, 1 - c), me).wait_recv()
    for cp in first + passed:
        cp.wait_send()


def _scatter_partials(g_ref, recv_ref, send_sems, recv_sems, phase):
    x, y, c = _place()
    copies = []
    for j, (fx, fy) in enumerate(((1, 0), (0, 1), (1, 1))):
        px, py = _flip(x, fx), _flip(y, fy)
        copies.append(pltpu.make_async_remote_copy(
            src_ref=g_ref.at[2 * px + py], dst_ref=recv_ref.at[j], send_sem=send_sems.at[j], recv_sem=recv_sems.at[j],
            device_id=(px, py, c), device_id_type=MESH,
        ))
    if phase == "start":
        for cp in copies:
            cp.start()
        return
    for cp in copies:
        cp.wait_recv()
    for cp in copies:
        cp.wait_send()


def _scatter_sems():
    return [pltpu.SemaphoreType.DMA((3,)), pltpu.SemaphoreType.DMA((3,))]


def _scatter_to_chips(g):
    def body(g_ref, recv_ref, send_sems, recv_sems):
        _scatter_partials(g_ref, recv_ref, send_sems, recv_sems, "start")
        _scatter_partials(g_ref, recv_ref, send_sems, recv_sems, "finish")

    return pl.pallas_call(
        body, out_shape=SDS((3,) + g.shape[1:], g.dtype), in_specs=[ANY], out_specs=ANY,
        scratch_shapes=[pltpu.SemaphoreType.DMA((3,)), pltpu.SemaphoreType.DMA((3,))], name="scatter_grads",
    )(g)


def _send_to_sibling(a, half_of_rows):
    rh = a.shape[1] // 2

    def body(a_ref, b_ref, send_sem, recv_sem):
        x, y, c = _place()
        src = a_ref.at[:, pl.ds((1 - c) * rh, rh), :] if half_of_rows else a_ref
        cp = pltpu.make_async_remote_copy(
            src_ref=src, dst_ref=b_ref, send_sem=send_sem, recv_sem=recv_sem, device_id=(x, y, 1 - c), device_id_type=MESH
        )
        cp.start()
        cp.wait()

    shape = (a.shape[0], rh, 128) if half_of_rows else a.shape
    return pl.pallas_call(
        body, out_shape=SDS(shape, a.dtype), in_specs=[ANY], out_specs=ANY,
        scratch_shapes=[pltpu.SemaphoreType.DMA, pltpu.SemaphoreType.DMA],
        name="sibling_halves" if half_of_rows else "sibling_swap",
    )(a)


def _allreduce_small(v):
    r = v.shape[0]

    def body(v_ref, out_ref, buf_ref, send_sems, recv_sems):
        x, y, c = _place()
        buf_ref[0] = v_ref[...]
        copies = []
        for k in range(1, 8):
            peer = (_flip(x, k >> 2 & 1), _flip(y, k >> 1 & 1), _flip(c, k & 1))
            cp = pltpu.make_async_remote_copy(
                src_ref=v_ref, dst_ref=buf_ref.at[k], send_sem=send_sems.at[k - 1], recv_sem=recv_sems.at[k - 1],
                device_id=peer, device_id_type=MESH,
            )
            cp.start()
            copies.append(cp)
        for cp in copies:
            cp.wait_recv()
        acc = None
        for d in range(8):
            slot = 4 * _flip(x, d >> 2 & 1) + 2 * _flip(y, d >> 1 & 1) + _flip(c, d & 1)
            term = buf_ref[slot]
            acc = term if acc is None else acc + term
        out_ref[...] = acc
        for cp in copies:
            cp.wait_send()

    return pl.pallas_call(
        body, out_shape=SDS(v.shape, F32), in_specs=[VMEM], out_specs=VMEM,
        scratch_shapes=[pltpu.VMEM((8, r, 128), F32), pltpu.SemaphoreType.DMA((7,)), pltpu.SemaphoreType.DMA((7,))],
        name="allreduce_small",
    )(v)


_BIG_A = (
    ("w_in", 1, False), ("decay_w2_fwd", 1, False), ("decay_w2_bwd", 1, False), ("iclr_a2_fwd", 1, False),
    ("iclr_a2_bwd", 1, False), ("gate_g2", 1, False),
)
_BIG_B = (
    ("w_uq", 0, False), ("w_ukv", 1, False), ("w_out", 0, False), ("w_ffn_up", 1, False), ("ffn_conv_w", 1, True),
    ("w_ffn_down", 0, False),
)
_BIG = _BIG_A + _BIG_B
_SMALL = (
    "ln_mix_g", "shift_mu_prev", "shift_mu_next", "decay_w0_fwd", "decay_w0_bwd", "iclr_a0_fwd", "iclr_a0_bwd", "k_k",
    "k_a", "r_k", "ln_x_g", "ln_x_b", "q_norm_g", "kv_norm_g", "mla_out_g", "ln_ffn_g", "ffn_conv_b", "ln_final_g",
)
_WEIGHTS = (
    "ln_mix_g", "w_in", "shift_mu_prev", "shift_mu_next", "decay_w0_fwd", "decay_w2_fwd", "decay_w0_bwd", "decay_w2_bwd",
    "iclr_a0_fwd", "iclr_a2_fwd", "iclr_a0_bwd", "iclr_a2_bwd", "gate_g2", "k_k", "k_a", "r_k", "ln_x_g", "ln_x_b",
    "q_norm_g", "w_uq", "kv_norm_g", "w_ukv", "mla_out_g", "w_out", "ln_ffn_g", "w_ffn_up", "ffn_conv_w", "ffn_conv_b",
    "w_ffn_down", "ln_final_g",
)


def _pad_rows(flat, rows):
    return jnp.pad(flat, (0, rows * 128 - flat.shape[0])).reshape(rows, 128)


def _rows_for(n, mult):
    rows = -(-n // 128)
    return -(-rows // mult) * mult


def _pack_shards_bf16(arrs, entries):
    parts = []
    for name, _, raw in entries:
        w = arrs[name][0]
        flat = lax.bitcast_convert_type(w, BF16).reshape(-1) if raw else w.astype(BF16).reshape(-1)
        parts.append(_pad_rows(flat, _rows_for(flat.shape[0], 32)))
    return jnp.concatenate(parts, axis=0)


def _unpack_gathered(g4, arrs, entries):
    out, off = {}, 0
    for name, axis, raw in entries:
        a, b = arrs[name].shape[1:]
        n = a * b * (2 if raw else 1)
        rows = _rows_for(n, 32)
        seg = g4[:, off:off + rows].reshape(4, rows * 128)[:, :n]
        off += rows
        if raw:
            seg = lax.bitcast_convert_type(seg.reshape(4, a * b, 2), F32)
        seg = seg.reshape(4, a, b)
        out[name] = jnp.concatenate([seg[s] for s in range(4)], axis=1) if axis == 1 else seg.reshape(4 * a, b)
    return out


def _pack_grads(full, arrs, entries):
    parts = []
    for name, axis, _ in entries:
        a, b = arrs[name].shape[1:]
        g = full[name]
        sh = g.reshape(a, 4, b).transpose(1, 0, 2) if axis == 1 else g.reshape(4, a, b)
        rows = _rows_for(a * b, 8)
        parts.append(jnp.pad(sh.reshape(4, a * b), ((0, 0), (0, rows * 128 - a * b))).reshape(4, rows, 128))
    total = sum(q.shape[1] for q in parts)
    parts.append(jnp.zeros((4, -(-total // 1024) * 1024 - total, 128), F32))
    return jnp.concatenate(parts, axis=1)


def _unpack_grads(g, arrs, entries):
    out, off = {}, 0
    for name, _, _ in entries:
        a, b = arrs[name].shape[1:]
        rows = _rows_for(a * b, 8)
        out[name] = g[off:off + rows].reshape(-1)[:a * b].reshape(1, a, b)
        off += rows
    return out


def _pack_small(vals):
    flat = jnp.concatenate([vals[n].reshape(-1).astype(F32) for n in _SMALL] + [vals["_loss"].reshape(-1)])
    return _pad_rows(flat, _rows_for(flat.shape[0], 8))


def _unpack_small(buf, arrs):
    flat, out, off = buf.reshape(-1), {}, 0
    for n in _SMALL:
        size = arrs[n].size
        out[n] = flat[off:off + size].reshape(arrs[n].shape)
        off += size
    out["_loss"] = flat[off]
    return out


def _rot_cols(w):
    return jnp.concatenate([-w[..., 16:], w[..., :16]], axis=-1)


def _rot_cols_t(g):
    return jnp.concatenate([g[..., 16:], -g[..., :16]], axis=-1)


def _rope_tables(t):
    inv = jnp.power(ROPE_THETA, -jnp.arange(0, ROPE_DIM, 2, dtype=F32) / ROPE_DIM)
    ang = jnp.arange(t, dtype=F32)[:, None] * inv[None, :]
    one, zero = jnp.ones((t, 64), F32), jnp.zeros((t, 64), F32)
    cs = jnp.concatenate([one, jnp.cos(ang), jnp.cos(ang), zero[:, :32]], axis=1)
    sn = jnp.concatenate([zero, jnp.sin(ang), jnp.sin(ang), zero[:, :32]], axis=1)
    return cs, sn


def _block_diag(a, b):
    za = jnp.zeros_like(a)
    return jnp.concatenate([jnp.concatenate([a, za], axis=1), jnp.concatenate([za, b], axis=1)], axis=0)


def kernel(x, ln_mix_g, w_in, shift_mu_prev, shift_mu_next, decay_w0_fwd, decay_w2_fwd, decay_w0_bwd, decay_w2_bwd, iclr_a0_fwd, iclr_a2_fwd, iclr_a0_bwd, iclr_a2_bwd, gate_g2, k_k, k_a, r_k, ln_x_g, ln_x_b, q_norm_g, w_uq, kv_norm_g, w_ukv, mla_out_g, w_out, ln_ffn_g, w_ffn_up, ffn_conv_w, ffn_conv_b, w_ffn_down, ln_final_g, loss_target, m_ln_mix_g, m_w_in, m_shift_mu_prev, m_shift_mu_next, m_decay_w0_fwd, m_decay_w2_fwd, m_decay_w0_bwd, m_decay_w2_bwd, m_iclr_a0_fwd, m_iclr_a2_fwd, m_iclr_a0_bwd, m_iclr_a2_bwd, m_gate_g2, m_k_k, m_k_a, m_r_k, m_ln_x_g, m_ln_x_b, m_q_norm_g, m_w_uq, m_kv_norm_g, m_w_ukv, m_mla_out_g, m_w_out, m_ln_ffn_g, m_w_ffn_up, m_ffn_conv_w, m_ffn_conv_b, m_w_ffn_down, m_ln_final_g, v_ln_mix_g, v_w_in, v_shift_mu_prev, v_shift_mu_next, v_decay_w0_fwd, v_decay_w2_fwd, v_decay_w0_bwd, v_decay_w2_bwd, v_iclr_a0_fwd, v_iclr_a2_fwd, v_iclr_a0_bwd, v_iclr_a2_bwd, v_gate_g2, v_k_k, v_k_a, v_r_k, v_ln_x_g, v_ln_x_b, v_q_norm_g, v_w_uq, v_kv_norm_g, v_w_ukv, v_mla_out_g, v_w_out, v_ln_ffn_g, v_w_ffn_up, v_ffn_conv_w, v_ffn_conv_b, v_w_ffn_down, v_ln_final_g):
    arrs = dict(locals())
    b, t, d = x.shape
    m = b * t
    x2 = x.reshape(m, d)
    tgt = loss_target.reshape(m, d)
    vec = lambda n: arrs[n].reshape(1, -1)

    core = lax.axis_index("c")
    chip = 2 * lax.axis_index("x") + lax.axis_index("y")
    def unpack(gathered, shard, entries):
        g4 = lax.dynamic_update_index_in_dim(gathered.reshape(N_CHIPS, -1, 128), shard, chip, axis=0)
        return _unpack_gathered(g4, arrs, entries)

    shard_a, shard_b = _pack_shards_bf16(arrs, _BIG_A), _pack_shards_bf16(arrs, _BIG_B)
    fw = unpack(_allgather_weights(shard_a), shard_a, _BIG_A)
    win = fw["w_in"]
    zc = jnp.zeros((d, 64), BF16)
    w_kr = win[:, 2944:2976]
    win_m = jnp.concatenate([win[:, 1920:2944], zc, w_kr, zc[:, :32], zc, _rot_cols(w_kr), zc[:, :32]], axis=1)
    win_r = win[:, :RWKV_COLS]
    head = jnp.arange(512) // HEAD_DIM
    rw = dict(
        w0=jnp.concatenate([vec("decay_w0_fwd"), vec("decay_w0_bwd")], axis=1),
        w2=_block_diag(fw["decay_w2_fwd"], fw["decay_w2_bwd"]).astype(F32),
        a0=jnp.concatenate([vec("iclr_a0_fwd"), vec("iclr_a0_bwd")], axis=1),
        a2=_block_diag(fw["iclr_a2_fwd"], fw["iclr_a2_bwd"]).astype(F32),
        g2=fw["gate_g2"].astype(F32), k_k=vec("k_k"), k_a=vec("k_a"), r_k=vec("r_k"), ln_x_g=vec("ln_x_g"), ln_x_b=vec("ln_x_b"),
        ones_bd=(head[:, None] == head[None, :]).astype(F32),
    )
    cs, sn = _rope_tables(t)

    n1 = _rms_fwd(x2, vec("ln_mix_g"), "rms_mix")
    zm = _mm(n1, win_m, "nn", "proj_in_mla")
    zr = _mm(n1, win_r, "nn", "proj_in_rwkv")
    zs = _shift_fwd(zr.reshape(b, t, RWKV_COLS), vec("shift_mu_prev"), vec("shift_mu_next"))
    zs2 = zs.reshape(m, RWKV_COLS)
    wf, wb, kf, kb, kk, kaf, kab, gate = _prep_fwd(zs2, rw)
    r4 = lambda a: a.reshape(b, t, 512)
    f2 = lambda a: a.reshape(m, 512)
    kk4 = r4(kk)
    ops_f = (r4(wf), r4(kf), r4(kaf))
    ops_b = (r4(wb), r4(kb), r4(kab))
    y_f, hist_f, y_b, hist_b, s_last, gathered_b = _scan_fwd(zs, kk4, ops_f, ops_b, shard_b)
    fw.update(unpack(gathered_b, shard_b, _BIG_B))
    uq = fw["w_uq"].astype(F32).reshape(Q_RANK, HEADS, 96)
    z32 = jnp.zeros((Q_RANK, HEADS, 32), F32)
    wq = jnp.concatenate([uq[..., :64], uq[..., 64:], z32], axis=-1).reshape(Q_RANK, 1024)
    wqr = jnp.concatenate([z32, z32, _rot_cols(uq[..., 64:]), z32], axis=-1).reshape(Q_RANK, 1024)
    ukv = fw["w_ukv"].astype(F32).reshape(KV_RANK, HEADS, 128)
    wk = jnp.concatenate([ukv[..., :64], jnp.zeros_like(ukv[..., :64])], axis=-1).reshape(KV_RANK, 1024)
    wv = ukv[..., 64:].reshape(KV_RANK, 512)
    mp = dict(q_norm_g=vec("q_norm_g"), kv_norm_g=vec("kv_norm_g"), wq=wq, wqr=wqr, wk=wk, wv=wv)
    w_up_g, w_up_v = fw["w_ffn_up"][:, :D_FF], fw["w_ffn_up"][:, D_FF:]
    cw, cb = fw["ffn_conv_w"], vec("ffn_conv_b")
    y_f, y_b = f2(y_f), f2(y_b)
    y_rwkv = _post_fwd(y_f, y_b, zs2, kf, kb, gate, rw)
    q, kfull, v = _mla_fwd(zm, cs, sn, mp, t)
    o, lse = _attn_fwd(q, kfull, v, b, t)
    y_mla = _rms_fwd(o, vec("mla_out_g"), "rms_mla_out")
    ymix = jnp.concatenate([y_rwkv, y_mla], axis=1)
    h1 = _mm(ymix, fw["w_out"], "nn", "proj_out", add=x2)
    n2 = _rms_fwd(h1, vec("ln_ffn_g"), "rms_ffn")
    ug = _mm(n2, w_up_g, "nn", "ffn_up_gate")
    uv = _mm(n2, w_up_v, "nn", "ffn_up_val")
    r3f = lambda a: a.reshape(b, t, D_FF)
    act = _ffn_mid_fwd(r3f(ug), r3f(uv), cw, cb).reshape(m, D_FF)
    h2 = _mm(act, fw["w_ffn_down"], "nn", "ffn_down", add=h1)
    loss_tab, dh2, g_ln_final = _final(h2, vec("ln_final_g"), tgt)

    gfull = {}
    dact = _mm(dh2, fw["w_ffn_down"], "nt", "d_ffn_act")
    gfull["w_ffn_down"] = _mm(act, dh2, "tn", "g_ffn_down")
    dug, duv, tab_g, tab_v = _ffn_mid_bwd(r3f(ug), r3f(uv), cw, cb, r3f(dact))
    dug, duv = dug.reshape(m, D_FF), duv.reshape(m, D_FF)
    gfull["ffn_conv_w"] = jnp.concatenate([tab_g[0:3], tab_v[0:3]], axis=1)
    g_conv_b = jnp.concatenate([tab_g[3:4], tab_v[3:4]], axis=1)
    dn2 = _mm(duv, w_up_v, "nt", "d_ffn_in_val", add=_mm(dug, w_up_g, "nt", "d_ffn_in_gate"))
    gfull["w_ffn_up"] = jnp.concatenate([_mm(n2, dug, "tn", "g_ffn_up_gate"), _mm(n2, duv, "tn", "g_ffn_up_val")], axis=1)
    dh1, g_ln_ffn = _rms_bwd(h1, vec("ln_ffn_g"), dn2, "rms_ffn_bwd", dres=dh2)
    dymix = _mm(dh1, fw["w_out"], "nt", "d_mix")
    gfull["w_out"] = _mm(ymix, dh1, "tn", "g_w_out")
    do, g_mla_out = _rms_bwd(o, vec("mla_out_g"), dymix, "rms_mla_out_bwd", dy_block=1)
    dq, dk, dv = _attn_bwd(q, kfull, v, o, lse, do, b, t)
    dzm, g_qn, g_kvn, g_wq, g_wqr, g_wk, g_wv = _mla_bwd(zm, cs, sn, mp, t, dq, dk, dv)
    gq3, gqr3 = g_wq.reshape(Q_RANK, HEADS, 128), g_wqr.reshape(Q_RANK, HEADS, 128)
    gfull["w_uq"] = jnp.concatenate(
        [gq3[..., :64], gq3[..., 64:96] + _rot_cols_t(gqr3[..., 64:96])], axis=-1
    ).reshape(Q_RANK, HEADS * 96)
    gfull["w_ukv"] = jnp.concatenate(
        [g_wk.reshape(KV_RANK, HEADS, 128)[..., :64], g_wv.reshape(KV_RANK, HEADS, 64)], axis=-1
    ).reshape(KV_RANK, 1024)
    def cores_first(entries, tag):
        packed = _pack_grads(gfull, arrs, entries)
        rh = packed.shape[1] // 2
        own = lax.dynamic_slice_in_dim(packed, core * rh, rh, axis=1)
        sib = _send_to_sibling(packed, True)
        return _add_rows([own.reshape(4 * rh, 128), sib.reshape(4 * rh, 128)], "sum_cores_" + tag, BF16).reshape(4, rh, 128)

    def chips_then_join(chip_part, recv, entries, tag):
        mine = lax.dynamic_index_in_dim(chip_part, chip, axis=0, keepdims=False)
        half = _add_rows([mine, recv[0], recv[1], recv[2]], "sum_chips_" + tag)
        other = _send_to_sibling(half, False)
        lower = jnp.where(core == 0, half, other)
        upper = jnp.where(core == 0, other, half)
        return _unpack_grads(jnp.concatenate([lower, upper], axis=0), arrs, entries)

    part_b = cores_first(_BIG_B, "b")
    dys, dr_p, dk_p, dv_p, dgate, g_rk, g_lnx_g, g_lnx_b = _post_bwd(y_f, y_b, zs2, kf, kb, gate, rw, dymix)
    (dr_f, dwf, dkf, dkk_f, dkaf, dv_f, dr_b, dwb, dkb, dkk_b, dkab, dv_b, recv_b) = _scan_bwd(
        zs, kk4, r4(dys), ops_f, hist_f, ops_b, hist_b, s_last, part_b)
    g_big = chips_then_join(part_b, recv_b, _BIG_B, "b")
    cts = dict(dwf=f2(dwf), dwb=f2(dwb), dkf=f2(dkf), dkb=f2(dkb), dkk_f=f2(dkk_f), dkk_b=f2(dkk_b), dkaf=f2(dkaf), dkab=f2(dkab),
               dr_f=f2(dr_f), dr_b=f2(dr_b), dr_p=dr_p, dk_p=dk_p, dv_p=dv_p, dg=dgate, dv_f=f2(dv_f), dv_b=f2(dv_b))
    dzs, g_w0, g_w2, g_a0, g_a2, g_g2, g_kk, g_ka = _prep_bwd(zs2, rw, cts)
    dzr, g_mu_p, g_mu_n = _shift_bwd(dzs.reshape(b, t, RWKV_COLS), zr.reshape(b, t, RWKV_COLS), vec("shift_mu_prev"), vec("shift_mu_next"))
    dzr = dzr.reshape(m, RWKV_COLS)
    gfull["decay_w2_fwd"], gfull["decay_w2_bwd"] = g_w2[:64, :512], g_w2[64:, 512:]
    gfull["iclr_a2_fwd"], gfull["iclr_a2_bwd"] = g_a2[:64, :512], g_a2[64:, 512:]
    gfull["gate_g2"] = g_g2
    dn1 = _mm(dzr, win_r, "nt", "d_proj_in_rwkv", add=_mm(dzm, win_m, "nt", "d_proj_in_mla"))
    g_m = _mm(n1, dzm, "tn", "g_w_in_mla")
    g_r = _mm(n1, dzr, "tn", "g_w_in_rwkv")
    g_kr = g_m[:, 1088:1120] + _rot_cols_t(g_m[:, 1216:1248])
    gfull["w_in"] = jnp.concatenate([g_r, g_m[:, :1024], g_kr], axis=1)
    dx, g_ln_mix = _rms_bwd(x2, vec("ln_mix_g"), dn1, "rms_mix_bwd", dres=dh1)

    part_a = cores_first(_BIG_A, "a")
    g_big.update(chips_then_join(part_a, _scatter_to_chips(part_a), _BIG_A, "a"))
    small = {
        "ln_mix_g": g_ln_mix, "shift_mu_prev": g_mu_p, "shift_mu_next": g_mu_n, "decay_w0_fwd": g_w0[:, :512],
        "decay_w0_bwd": g_w0[:, 512:], "iclr_a0_fwd": g_a0[:, :512], "iclr_a0_bwd": g_a0[:, 512:], "k_k": g_kk, "k_a": g_ka,
        "r_k": g_rk, "ln_x_g": g_lnx_g, "ln_x_b": g_lnx_b, "q_norm_g": g_qn, "kv_norm_g": g_kvn, "mla_out_g": g_mla_out,
        "ln_ffn_g": g_ln_ffn, "ffn_conv_b": g_conv_b, "ln_final_g": g_ln_final,
        "_loss": jnp.pad(loss_tab[0, 0:1], (0, 127)),
    }
    g_small_buf = _allreduce_small(_pack_small(small))
    g_small = _unpack_small(g_small_buf, arrs)

    grads, deltas, new_m, new_v = {}, {}, {}, {}
    for name, _, _ in _BIG:
        grads[name] = g_big[name]
        deltas[name], new_m[name], new_v[name] = _adamw(
            arrs[name], g_big[name], arrs["m_" + name], arrs["v_" + name], "adamw_" + name)
    pk = lambda pre: _pack_small({**{n: arrs[pre + n] for n in _SMALL}, "_loss": jnp.zeros((128,), F32)})
    sd, sm, sv = _adamw(pk(""), g_small_buf, pk("m_"), pk("v_"), "adamw_small")
    sd, sm, sv = _unpack_small(sd, arrs), _unpack_small(sm, arrs), _unpack_small(sv, arrs)
    for n in _SMALL:
        grads[n], deltas[n], new_m[n], new_v[n] = g_small[n], sd[n], sm[n], sv[n]

    return (g_small["_loss"], dx.reshape(b, t, d), *[grads[n] for n in _WEIGHTS], *[deltas[n] for n in _WEIGHTS],
            *[new_m[n] for n in _WEIGHTS], *[new_v[n] for n in _WEIGHTS])
```

```python
import functools
import math

import jax
import jax.numpy as jnp
from jax import lax
from jax.experimental import pallas as pl
from jax.experimental.pallas import tpu as pltpu

F32, BF16 = jnp.float32, jnp.bfloat16
MESH = pl.DeviceIdType.MESH
ANY = pl.BlockSpec(memory_space=pl.ANY)
VMEM = pl.BlockSpec(memory_space=pltpu.VMEM)
BS = pl.BlockSpec
SDS = jax.ShapeDtypeStruct

NORM_EPS = 1e-6
GN_EPS = 64e-5
L2_EPS = 1e-12
HEADS = 8
HEAD_DIM = 64
D_RWKV = HEADS * HEAD_DIM
ROPE_DIM = 32
ROPE_THETA = 10000.0
MLA_SCALE = (64 + ROPE_DIM) ** -0.5
Q_RANK, KV_RANK = 768, 256
RWKV_COLS = 1920
MLA_PAD_COLS = Q_RANK + KV_RANK + 256
D_FF = 2816
ADAM_LR, ADAM_B1, ADAM_B2, ADAM_EPS, ADAM_WD, ADAM_STEP = 0.001, 0.9, 0.999, 1e-08, 0.01, 10

V7X_LANES = 128
V7X_VMEM_LIMIT = 48 * 1024 * 1024
SCAN_CHUNK = 16
N_CHIPS = 4


def _cp(*sem):
    return pltpu.CompilerParams(dimension_semantics=sem, vmem_limit_bytes=V7X_VMEM_LIMIT)


def _tile(n, cands=(512, 640, 384, 256, 128)):
    for c in cands:
        if n % c == 0:
            return c
    return n


def _row_tile(n, cap=256):
    best = n
    for t in range(8, cap + 1, 8):
        if n % t == 0:
            best = t
    return best if best <= cap or n <= cap else n


def _rms(x, g):
    ms = jnp.mean(x * x, axis=-1, keepdims=True)
    return x * lax.rsqrt(ms + NORM_EPS) * g


@jax.custom_vjp
def _bdot(x, w):
    return jnp.dot(x.astype(BF16), w.astype(BF16), preferred_element_type=F32)


def _bdot_fwd(x, w):
    return _bdot(x, w), (x, w)


def _bdot_bwd(res, ct):
    x, w = res
    c = ct.astype(BF16)
    dx = lax.dot_general(c, w.astype(BF16), (((1,), (1,)), ((), ())), preferred_element_type=F32)
    dw = lax.dot_general(x.astype(BF16), c, (((0,), (0,)), ((), ())), preferred_element_type=F32)
    return dx.astype(x.dtype), dw.astype(w.dtype)


_bdot.defvjp(_bdot_fwd, _bdot_bwd)


@jax.custom_vjp
def _headsum(x, ones_bd):
    hi = x.astype(BF16)
    mid = (x - hi.astype(F32)).astype(BF16)
    ob = ones_bd.astype(BF16)
    return jnp.dot(hi, ob, preferred_element_type=F32) + jnp.dot(mid, ob, preferred_element_type=F32)


def _headsum_fwd(x, ones_bd):
    return _headsum(x, ones_bd), ones_bd


def _headsum_bwd(ones_bd, ct):
    return _headsum(ct, ones_bd), jnp.zeros_like(ones_bd)


_headsum.defvjp(_headsum_fwd, _headsum_bwd)


def _prep_fn(zs, w0, w2, a0, a2, g2, k_k, k_a, ones_bd):
    k = zs[:, 512:1024]
    wd = zs[:, 1536:1664]
    ad = zs[:, 1664:1792]
    gd = zs[:, 1792:1920]
    logit = w0 + _bdot(jnp.tanh(wd), w2)
    w = jnp.exp(-math.exp(-0.5) * jax.nn.sigmoid(logit))
    a = jax.nn.sigmoid(a0 + _bdot(ad, a2))
    g = _bdot(jax.nn.sigmoid(gd), g2)
    kkr = k * k_k
    nrm = jnp.sqrt(_headsum(kkr * kkr, ones_bd))
    kk = kkr / jnp.maximum(nrm, L2_EPS)
    a_f, a_b = a[:, :512], a[:, 512:]
    kf = k * (1.0 + (a_f - 1.0) * k_a)
    kb = k * (1.0 + (a_b - 1.0) * k_a)
    return w[:, :512], w[:, 512:], kf, kb, kk, kk * a_f, kk * a_b, g


def _post_fn(y, r, kf, kb, v, g, r_k, ln_g, ln_b, ones_bd):
    mu =_headsum(y, ones_bd) * (1.0 / HEAD_DIM)
    yc = y - mu
    var = _headsum(yc * yc, ones_bd) * (1.0 / HEAD_DIM)
    yn = yc * lax.rsqrt(var + GN_EPS) * ln_g + ln_b
    bonus = _headsum(r * (kf + kb) * r_k, ones_bd) * v
    return (yn + bonus) * g


def _cat8(x):
    return jnp.concatenate([x] * HEADS, axis=1)


def _mla_fn(zm, cs, sn, gq, gkv, wq, wqr, wk, wv):
    cq = zm[:, :Q_RANK]
    ckv = zm[:, Q_RANK:Q_RANK + KV_RANK]
    kr = zm[:, Q_RANK + KV_RANK:Q_RANK + KV_RANK + 128]
    krr = zm[:, Q_RANK + KV_RANK + 128:]
    cqn = _rms(cq, gq)
    ckvn = _rms(ckv, gkv)
    q = (_bdot(cqn, wq) * _cat8(cs) + _bdot(cqn, wqr) * _cat8(sn)) * MLA_SCALE
    kro = kr * cs + krr * sn
    kfull = _bdot(ckvn, wk) + _cat8(kro)
    v = _bdot(ckvn, wv)
    return q, kfull, v


def _adamw_math(w, g, m, v):
    m2 = ADAM_B1 * m + (1.0 - ADAM_B1) * g
    v2 = ADAM_B2 * v + (1.0 - ADAM_B2) * (g * g)
    m_hat = m2 / (1.0 - ADAM_B1 ** ADAM_STEP)
    v_hat = v2 / (1.0 - ADAM_B2 ** ADAM_STEP)
    delta = -ADAM_LR * (m_hat / (jnp.sqrt(v_hat) + ADAM_EPS) + ADAM_WD * w)
    return delta, m2, v2


_DIMS = {"nn": (((1,), (0,)), ((), ())), "nt": (((1,), (1,)), ((), ())), "tn": (((0,), (0,)), ((), ()))}


def _mm(a, b, mode, name, out_dtype=F32, add=None):
    if mode == "nn":
        (m, k), (_, n) = a.shape, b.shape
    elif mode == "nt":
        (m, k), (n, _) = a.shape, b.shape
    else:
        (k, m), (_, n) = a.shape, b.shape
    big = (1024, 1408, 768, 640, 512, 384, 256, 128)
    tm, tn, tk = _tile(m, big), _tile(n, big), _tile(k, (512, 1408, 640, 384, 256, 128))
    nk = k // tk

    def body(a_ref, b_ref, *rest):
        if add is None:
            o_ref, acc_ref = rest
        else:
            add_ref, o_ref, acc_ref = rest
        kk = pl.program_id(2)

        @pl.when(kk == 0)
        def _():
            acc_ref[...] = jnp.zeros_like(acc_ref)

        acc_ref[...] += lax.dot_general(
            a_ref[...].astype(BF16), b_ref[...].astype(BF16), _DIMS[mode], preferred_element_type=F32
        )

        @pl.when(kk == nk - 1)
        def _():
            r = acc_ref[...]
            if add is not None:
                r = r + add_ref[...]
            o_ref[...] = r.astype(out_dtype)

    a_spec = BS((tk, tm), lambda i, j, kk: (kk, i)) if mode == "tn" else BS((tm, tk), lambda i, j, kk: (i, kk))
    b_spec = BS((tn, tk), lambda i, j, kk: (j, kk)) if mode == "nt" else BS((tk, tn), lambda i, j, kk: (kk, j))
    o_spec = BS((tm, tn), lambda i, j, kk: (i, j))
    ins, specs = [a, b], [a_spec, b_spec]
    if add is not None:
        ins.append(add)
        specs.append(o_spec)
    return pl.pallas_call(
        body, grid=(m // tm, n // tn, nk), in_specs=specs, out_specs=o_spec, out_shape=SDS((m, n), out_dtype),
        scratch_shapes=[pltpu.VMEM((tm, tn), F32)], compiler_params=_cp("parallel", "parallel", "arbitrary"), name=name,
    )(*ins)


def _rms_fwd(x, g, name):
    m, d = x.shape
    tr = _tile(m)

    def body(x_ref, g_ref, o_ref):
        o_ref[...] = _rms(x_ref[...], g_ref[...]).astype(BF16)

    return pl.pallas_call(
        body, grid=(m // tr,), in_specs=[BS((tr, d), lambda i: (i, 0)), BS((1, d), lambda i: (0, 0))],
        out_specs=BS((tr, d), lambda i: (i, 0)), out_shape=SDS((m, d), BF16), compiler_params=_cp("parallel"), name=name,
    )(x, g)


def _rms_bwd(x, g, dy, name, dres=None, dy_block=0):
    m, d = x.shape
    tr = _row_tile(m)

    def body(x_ref, g_ref, dy_ref, *rest):
        if dres is None:
            dx_ref, dg_ref = rest
        else:
            dres_ref, dx_ref, dg_ref = rest
        _, vjp = jax.vjp(_rms, x_ref[...], g_ref[...])
        dx, dg = vjp(dy_ref[...])
        if dres is not None:
            dx = dx + dres_ref[...]
        dx_ref[...] = dx

        @pl.when(pl.program_id(0) == 0)
        def _():
            dg_ref[...] = jnp.zeros_like(dg_ref)

        dg_ref[...] += dg

    row = BS((tr, d), lambda i: (i, 0))
    vec = BS((1, d), lambda i: (0, 0))
    ins, specs = [x, g, dy], [row, vec, BS((tr, d), lambda i: (i, dy_block))]
    if dres is not None:
        ins.append(dres)
        specs.append(row)
    return pl.pallas_call(
        body, grid=(m // tr,), in_specs=specs, out_specs=[row, vec], out_shape=[SDS((m, d), F32), SDS((1, d), F32)],
        compiler_params=_cp("arbitrary"), name=name,
    )(*ins)


def _final(h, g, tgt):
    m, d = h.shape
    tr = _row_tile(m)

    def loss_fn(hh, gg, tt):
        e = _rms(hh, gg) - tt
        return 0.5 * jnp.sum(e * e) * (1.0 / d)

    def body(h_ref, g_ref, t_ref, l_ref, dh_ref, dg_ref):
        val, (dh, dg) = jax.value_and_grad(loss_fn, argnums=(0, 1))(h_ref[...], g_ref[...], t_ref[...])
        dh_ref[...] = dh

        @pl.when(pl.program_id(0) == 0)
        def _():
            dg_ref[...] = jnp.zeros_like(dg_ref)
            l_ref[...] = jnp.zeros_like(l_ref)

        dg_ref[...] += dg
        l_ref[...] += jnp.full(l_ref.shape, val, F32)

    row = BS((tr, d), lambda i: (i, 0))
    vec = BS((1, d), lambda i: (0, 0))
    return pl.pallas_call(
        body, grid=(m // tr,), in_specs=[row, vec, row], out_specs=[BS((8, 128), lambda i: (0, 0)), row, vec],
        out_shape=[SDS((8, 128), F32), SDS((m, d), F32), SDS((1, d), F32)], compiler_params=_cp("arbitrary"), name="final_loss",
    )(h, g, tgt)


def _prev_next(z, t):
    row = lax.broadcasted_iota(jnp.int32, z.shape, 0)
    zp = jnp.where(row == 0, 0.0, pltpu.roll(z, 1, axis=0))
    zn = jnp.where(row == t - 1, 0.0, pltpu.roll(z, t - 1, axis=0))
    return zp, zn


def _shift_fwd(z3, mu_p, mu_n):
    b, t, c = z3.shape
    nc = c // 128

    def body(z_ref, mp_ref, mn_ref, o_ref):
        z = z_ref[0]
        zp, zn = _prev_next(z, t)
        o_ref[0] = z + mp_ref[...] * (zp - z) + mn_ref[...] * (zn - z)

    blk = BS((1, t, 128), lambda i, j: (i, 0, j))
    vec = BS((1, 128), lambda i, j: (0, j))
    return pl.pallas_call(
        body, grid=(b, nc), in_specs=[blk, vec, vec], out_specs=blk, out_shape=SDS((b, t, c), F32),
        compiler_params=_cp("parallel", "parallel"), name="shift_fwd",
    )(z3, mu_p, mu_n)


def _shift_bwd(dzs3, z3, mu_p, mu_n):
    b, t, c = z3.shape
    nc = c // 128

    def body(d_ref, z_ref, mp_ref, mn_ref, dz_ref, dmp_ref, dmn_ref):
        d, z = d_ref[0], z_ref[0]
        mp, mn = mp_ref[...], mn_ref[...]
        zp, zn = _prev_next(z, t)
        _, dp_next = _prev_next(d * mp, t)
        dn_prev, _ = _prev_next(d * mn, t)
        dz_ref[0] = (d * (1.0 - mp - mn) + dp_next + dn_prev).astype(BF16)

        @pl.when(pl.program_id(1) == 0)
        def _():
            dmp_ref[...] = jnp.zeros_like(dmp_ref)
            dmn_ref[...] = jnp.zeros_like(dmn_ref)

        dmp_ref[...] += jnp.sum(d * (zp - z), axis=0, keepdims=True)
        dmn_ref[...] += jnp.sum(d * (zn - z), axis=0, keepdims=True)

    blk = BS((1, t, 128), lambda j, i: (i, 0, j))
    vec = BS((1, 128), lambda j, i: (0, j))
    return pl.pallas_call(
        body, grid=(nc, b), in_specs=[blk, blk, vec, vec], out_specs=[blk, vec, vec],
        out_shape=[SDS((b, t, c), BF16), SDS((1, c), F32), SDS((1, c), F32)],
        compiler_params=_cp("parallel", "arbitrary"), name="shift_bwd",
    )(dzs3, z3, mu_p, mu_n)


def _const(shape):
    nd = len(shape)
    return BS(shape, lambda i: (0,) * nd)


def _prep_fwd(zs, p):
    m = zs.shape[0]
    tr = 256
    params = [p["w0"], p["w2"], p["a0"], p["a2"], p["g2"], p["k_k"], p["k_a"], p["ones_bd"]]

    def body(zs_ref, w0, w2, a0, a2, g2, kk_, ka_, bd, wf, wb, kf, kb, kk, kaf, kab, g):
        outs = _prep_fn(zs_ref[...], w0[...], w2[...], a0[...], a2[...], g2[...], kk_[...], ka_[...], bd[...])
        for ref, val in zip((wf, wb, kf, kb, kk, kaf, kab, g), outs):
            ref[...] = val

    row = BS((tr, 512), lambda i: (i, 0))
    return pl.pallas_call(
        body, grid=(m // tr,), in_specs=[BS((tr, RWKV_COLS), lambda i: (i, 0))] + [_const(q.shape) for q in params],
        out_specs=[row] * 8, out_shape=[SDS((m, 512), F32)] * 8, compiler_params=_cp("parallel"), name="rwkv_prep_fwd",
    )(zs, *params)


def _prep_bwd(zs, p, ct_rows):
    m = zs.shape[0]
    tr = 128
    params = [p["w0"], p["w2"], p["a0"], p["a2"], p["g2"], p["k_k"], p["k_a"]]
    names = ["dwf", "dwb", "dkf", "dkb", "dkk_f", "dkk_b", "dkaf", "dkab", "dr_f", "dr_b", "dr_p", "dk_p", "dv_p", "dg",
             "dv_f", "dv_b"]
    rows = [ct_rows[n] for n in names]

    def body(zs_ref, w0, w2, a0, a2, g2, kk_, ka_, bd, *rest):
        c = {n: r[...] for n, r in zip(names, rest[:len(names)])}
        outs = rest[len(names):]
        dzs_ref, grads = outs[0], outs[1:]
        ones_bd = bd[...]
        _, vjp = jax.vjp(
            lambda *q: _prep_fn(*q, ones_bd), zs_ref[...], w0[...], w2[...], a0[...], a2[...], g2[...], kk_[...], ka_[...]
        )
        cts = (c["dwf"], c["dwb"], c["dkf"] + c["dk_p"], c["dkb"] + c["dk_p"], c["dkk_f"] + c["dkk_b"], c["dkaf"], c["dkab"], c["dg"])
        dzs, *dparams = vjp(cts)
        dr = c["dr_f"] + c["dr_b"] + c["dr_p"]
        dv = c["dv_f"] + c["dv_b"] + c["dv_p"]
        dzs_ref[:, 0:512] = dzs[:, 0:512] + dr
        dzs_ref[:, 512:1024] = dzs[:, 512:1024]
        dzs_ref[:, 1024:1536] = dzs[:, 1024:1536] + dv
        dzs_ref[:, 1536:1920] = dzs[:, 1536:1920]

        @pl.when(pl.program_id(0) == 0)
        def _():
            for gr in grads:
                gr[...] = jnp.zeros_like(gr)

        for gr, val in zip(grads, dparams):
            gr[...] += val

    row = BS((tr, 512), lambda i: (i, 0))
    return pl.pallas_call(
        body, grid=(m // tr,),
        in_specs=[BS((tr, RWKV_COLS), lambda i: (i, 0))] + [_const(q.shape) for q in params] + [_const(p["ones_bd"].shape)]
        + [row] * len(names),
        out_specs=[BS((tr, RWKV_COLS), lambda i: (i, 0))] + [_const(q.shape) for q in params],
        out_shape=[SDS((m, RWKV_COLS), F32)] + [SDS(q.shape, F32) for q in params],
        compiler_params=_cp("arbitrary"), name="rwkv_prep_bwd",
    )(zs, *params, p["ones_bd"], *rows)


def _post_specs(tr):
    r = BS((tr, 512), lambda i: (i, 0))
    v = BS((tr, 512), lambda i: (i, 2))
    row = BS((tr, 512), lambda i: (i, 0))
    return r, v, row


def _post_fwd(y_f, y_b, zs, kf, kb, g, p):
    m = zs.shape[0]
    tr = 256
    r, v, row = _post_specs(tr)
    vecs = [p["r_k"], p["ln_x_g"], p["ln_x_b"], p["ones_bd"]]

    def body(yf, yb, r_ref, v_ref, kf_ref, kb_ref, g_ref, rk, lg, lb, bd, o_ref):
        o_ref[...] = _post_fn(
            yf[...] + yb[...], r_ref[...], kf_ref[...], kb_ref[...], v_ref[...], g_ref[...], rk[...], lg[...], lb[...], bd[...]
        ).astype(BF16)

    return pl.pallas_call(
        body, grid=(m // tr,), in_specs=[row, row, r, v, row, row, row] + [_const(q.shape) for q in vecs],
        out_specs=row, out_shape=SDS((m, 512), BF16), compiler_params=_cp("parallel"), name="rwkv_post_fwd",
    )(y_f, y_b, zs, zs, kf, kb, g, *vecs)


def _post_bwd(y_f, y_b, zs, kf, kb, g, p, dymix):
    m = zs.shape[0]
    tr = 128
    r, v, row = _post_specs(tr)
    vecs = [p["r_k"], p["ln_x_g"], p["ln_x_b"]]

    def body(yf, yb, r_ref, v_ref, kf_ref, kb_ref, g_ref, rk, lg, lb, bd, dy_ref, dyo, dr, dk, dv, dg, drk, dlg, dlb):
        ones_bd = bd[...]
        _, vjp = jax.vjp(
            lambda *q: _post_fn(*q, ones_bd),
            yf[...] + yb[...], r_ref[...], kf_ref[...], kb_ref[...], v_ref[...], g_ref[...], rk[...], lg[...], lb[...],
        )
        c_y, c_r, c_kf, _, c_v, c_g, c_rk, c_lg, c_lb = vjp(dy_ref[...])
        dyo[...] = c_y
        dr[...] = c_r
        dk[...] = c_kf
        dv[...] = c_v
        dg[...] = c_g

        @pl.when(pl.program_id(0) == 0)
        def _():
            for ref in (drk, dlg, dlb):
                ref[...] = jnp.zeros_like(ref)

        drk[...] += c_rk
        dlg[...] += c_lg
        dlb[...] += c_lb

    vec = _const((1, 512))
    return pl.pallas_call(
        body, grid=(m // tr,),
        in_specs=[row, row, r, v, row, row, row] + [_const(q.shape) for q in vecs] + [_const(p["ones_bd"].shape), row],
        out_specs=[row, row, row, row, row, vec, vec, vec],
        out_shape=[SDS((m, 512), F32)] * 5 + [SDS((1, 512), F32)] * 3,
        compiler_params=_cp("arbitrary"), name="rwkv_post_bwd",
    )(y_f, y_b, zs, zs, kf, kb, g, *vecs, p["ones_bd"], dymix)


SCAN_MXU_GROUPS = 2


def _half_ones():
    ri = lax.broadcasted_iota(jnp.int32, (128, 128), 0)
    ci = lax.broadcasted_iota(jnp.int32, (128, 128), 1)
    return jnp.where((ri < 64) == (ci < 64), 1.0, 0.0).astype(BF16)


def _half_sums(xs, ones):
    out = []
    per = -(-len(xs) // SCAN_MXU_GROUPS)
    for g in range(0, len(xs), per):
        part = xs[g:g + per]
        res = jnp.dot(jnp.concatenate(part, axis=0).astype(BF16), ones, preferred_element_type=F32)
        out += [res[64 * i:64 * i + 64] for i in range(len(part))]
    return out


def _scan_specs(b, t):
    nc = t // SCAN_CHUNK
    up, down = (lambda c: c), (lambda c: nc - 1 - c)
    rows = [BS((b, SCAN_CHUNK, 512), lambda c, ci=ci: (0, ci(c), 0)) for ci in (up, down)]
    vrows = [BS((b, SCAN_CHUNK, 512), lambda c, ci=ci: (0, ci(c), 2)) for ci in (up, down)]
    hist = [BS((SCAN_CHUNK, b * 4, 64, 128), lambda c, ci=ci: (ci(c), 0, 0, 0)) for ci in (up, down)]
    return nc, rows, vrows, hist


class _Window:
    def __init__(self, g, ascending):
        self.bases = [pl.multiple_of(g * 8, 8) if asc else pl.multiple_of(SCAN_CHUNK - 8 - g * 8, 8) for asc in ascending]
        self.ascending = ascending
        self.blocks = {}
        self.row_id = lax.broadcasted_iota(jnp.int32, (8, 128), 0)

    def j(self, d, s):
        return s if self.ascending[d] else 7 - s

    def time(self, d, s):
        return self.bases[d] + self.j(d, s)

    def row(self, ref, d, bi, cols, s):
        key = (id(ref), d, bi, cols.start)
        if key not in self.blocks:
            self.blocks[key] = ref[bi, pl.ds(self.bases[d], 8), cols]
        jj = self.j(d, s)
        return self.blocks[key][jj:jj + 1, :]

    def put(self, buf, key, d, s, row):
        prev = buf.get(key)
        new = jnp.broadcast_to(row, (8, 128))
        buf[key] = new if prev is None else jnp.where(self.row_id == self.j(d, s), new, prev)

    def flush(self, buf, refs_of):
        for key, val in buf.items():
            ref, d, bi, cols = refs_of(key)
            ref[bi, pl.ds(self.bases[d], 8), cols] = val


def _pairs(b):
    return [(bi * 4 + p, bi, slice(128 * p, 128 * p + 128)) for bi in range(b) for p in range(4)]


def _colsum(x):
    return jnp.sum(x, axis=0, keepdims=True)


def _eye_mask():
    return (lax.broadcasted_iota(jnp.int32, (64, 128), 1) & 63) == lax.broadcasted_iota(jnp.int32, (64, 128), 0)


def _scan_fwd(zs, kk, ops_f, ops_b, shard):
    b, t = zs.shape[:2]
    nc, rows, vrows, hist = _scan_specs(b, t)
    npair = b * 4

    def body(*refs):
        ins, shard_ref, outs, s_ref = refs[:12], refs[12], refs[13:17], refs[17]
        gather = (shard_ref, *refs[18:21])
        dirs = [dict(zip(("r", "kk", "v", "w", "k", "ka", "y", "h"), (*ins[6 * d:6 * d + 6], *outs[2 * d:2 * d + 2])))
                for d in (0, 1)]

        @pl.when(pl.program_id(0) == 0)
        def _():
            s_ref[...] = jnp.zeros_like(s_ref)
            _gather_halves(*gather, "start")

        @pl.when(pl.program_id(0) == nc - 1)
        def _():
            _gather_halves(*gather, "finish")

        ones, eye = _half_ones(), _eye_mask()
        chains = [(d, pr, bi, cols) for d in (0, 1) for pr, bi, cols in _pairs(b)]

        def eight_steps(g, carry):
            win = _Window(g, (True, False))
            ybuf = {}
            for s in range(8):
                s_prev, xa = [], []
                for d, pr, bi, cols in chains:
                    q = dirs[d]
                    st = s_ref[d * npair + pr]
                    q["h"][win.time(d, s), pr] = st
                    s_prev.append(st)
                    xa += [st * win.row(q["kk"], d, bi, cols, s), jnp.where(eye, win.row(q["v"], d, bi, cols, s), 0.0)]
                ra = _half_sums(xa, ones)
                xb = []
                for i, (d, pr, bi, cols) in enumerate(chains):
                    q = dirs[d]
                    s_new = s_prev[i] * win.row(q["w"], d, bi, cols, s) - ra[2 * i] * win.row(q["ka"], d, bi, cols, s) \
                        + ra[2 * i + 1] * win.row(q["k"], d, bi, cols, s)
                    s_ref[d * npair + pr] = s_new
                    xb.append(s_new * win.row(q["r"], d, bi, cols, s))
                rb = _half_sums(xb, ones)
                for i, (d, pr, bi, cols) in enumerate(chains):
                    win.put(ybuf, i, d, s, _colsum(jnp.where(eye, rb[i], 0.0)))
            win.flush(ybuf, lambda i: (dirs[chains[i][0]]["y"], chains[i][0], chains[i][2], chains[i][3]))
            return carry

        lax.fori_loop(0, SCAN_CHUNK // 8, eight_steps, 0)

    row_shape, hist_shape = SDS((b, t, 512), F32), SDS((t, npair, 64, 128), F32)
    state = (2 * npair, 64, 128)
    return pl.pallas_call(
        body, grid=(nc,), in_specs=sum(([rows[d], rows[d], vrows[d]] + [rows[d]] * 3 for d in (0, 1)), []) + [ANY],
        out_specs=[rows[0], hist[0], rows[1], hist[1], BS(state, lambda c: (0, 0, 0)), ANY],
        out_shape=[row_shape, hist_shape, row_shape, hist_shape, SDS(state, F32), _gathered_shape(shard)],
        scratch_shapes=_gather_sems(), compiler_params=_cp("arbitrary"), name="wkv_scan",
    )(zs, kk, zs, *ops_f, zs, kk, zs, *ops_b, shard)


def _scan_bwd(zs, kk, dy, ops_f, hist_f, ops_b, hist_b, s_last, partials):
    b, t = zs.shape[:2]
    nc, rows, vrows, hist = _scan_specs(b, t)
    npair = b * 4
    names_in = ("r", "kk", "v", "dy", "w", "k", "ka", "h")
    names_out = ("dr", "dw", "dk", "dkk", "dka", "dv")

    def body(*refs):
        ins, last_ref, part_ref, outs, recv_ref = refs[:16], refs[16], refs[17], refs[18:30], refs[30]
        ds_ref, after_ref = refs[31], refs[32]
        scatter = (part_ref, recv_ref, refs[33], refs[34])
        dirs = [dict(zip(names_in + names_out, (*ins[8 * d:8 * d + 8], *outs[6 * d:6 * d + 6]))) for d in (0, 1)]

        @pl.when(pl.program_id(0) == 0)
        def _():
            ds_ref[...] = jnp.zeros_like(ds_ref)
            after_ref[...] = last_ref[...]
            _scatter_partials(*scatter, "start")

        @pl.when(pl.program_id(0) == nc - 1)
        def _():
            _scatter_partials(*scatter, "finish")

        ones, eye = _half_ones(), _eye_mask()
        chains = [(d, pr, bi, cols) for d in (0, 1) for pr, bi, cols in _pairs(b)]

        def eight_steps(g, carry):
            win = _Window(g, (False, True))
            obuf = {}
            s_after = [after_ref[d * npair + pr] for d, pr, _, _ in chains]
            for s in range(8):
                row = lambda name, d, bi, cols: win.row(dirs[d][name], d, bi, cols, s)
                s_prev, xa = [], []
                for d, pr, bi, cols in chains:
                    st = dirs[d]["h"][win.time(d, s), pr]
                    s_prev.append(st)
                    xa += [st * row("kk", d, bi, cols), jnp.where(eye, row("v", d, bi, cols), 0.0),
                           jnp.where(eye, row("dy", d, bi, cols), 0.0)]
                ra = _half_sums(xa, ones)
                ds_now, xb = [], []
                for i, (d, pr, bi, cols) in enumerate(chains):
                    skk, vcol, dycol = ra[3 * i], ra[3 * i + 1], ra[3 * i + 2]
                    ds = ds_ref[d * npair + pr] + dycol * row("r", d, bi, cols)
                    win.put(obuf, (i, "dr"), d, s, _colsum(s_after[i] * dycol))
                    win.put(obuf, (i, "dk"), d, s, _colsum(ds * vcol))
                    win.put(obuf, (i, "dka"), d, s, -_colsum(ds * skk))
                    win.put(obuf, (i, "dw"), d, s, _colsum(ds * s_prev[i]))
                    ds_now.append(ds)
                    xb += [ds * row("k", d, bi, cols), ds * row("ka", d, bi, cols)]
                rb = _half_sums(xb, ones)
                for i, (d, pr, bi, cols) in enumerate(chains):
                    dskk_neg = rb[2 * i + 1]
                    win.put(obuf, (i, "dv"), d, s, _colsum(jnp.where(eye, rb[2 * i], 0.0)))
                    win.put(obuf, (i, "dkk"), d, s, -_colsum(s_prev[i] * dskk_neg))
                    ds_ref[d * npair + pr] = ds_now[i] * row("w", d, bi, cols) - dskk_neg * row("kk", d, bi, cols)
                s_after = s_prev
            for i, (d, pr, _, _) in enumerate(chains):
                after_ref[d * npair + pr] = s_after[i]
            win.flush(obuf, lambda key: (dirs[chains[key[0]][0]][key[1]], chains[key[0]][0], chains[key[0]][2], chains[key[0]][3]))
            return carry

        lax.fori_loop(0, SCAN_CHUNK // 8, eight_steps, 0)

    row_shape = SDS((b, t, 512), F32)
    state = (2 * npair, 64, 128)
    return pl.pallas_call(
        body, grid=(nc,),
        in_specs=sum(([rows[d], rows[d], vrows[d]] + [rows[d]] * 4 + [hist[d]] for d in (1, 0)), [])
        + [BS(state, lambda c: (0, 0, 0)), ANY],
        out_specs=[rows[1]] * 6 + [rows[0]] * 6 + [ANY],
        out_shape=[row_shape] * 12 + [SDS((3,) + partials.shape[1:], partials.dtype)],
        scratch_shapes=[pltpu.VMEM(state, F32), pltpu.VMEM(state, F32)] + _scatter_sems(),
        compiler_params=_cp("arbitrary"), name="wkv_scan_bwd",
    )(zs, kk, zs, dy, *ops_f, hist_f, zs, kk, zs, dy, *ops_b, hist_b, s_last, partials)


def _mla_fwd(zm, cs, sn, p, t):
    m = zm.shape[0]
    tr = 256
    per = t // tr
    params = [p["q_norm_g"], p["kv_norm_g"], p["wq"], p["wqr"], p["wk"], p["wv"]]

    def body(z_ref, cs_ref, sn_ref, gq, gkv, wq, wqr, wk, wv, q_ref, k_ref, v_ref):
        q, kf, v = _mla_fn(z_ref[...], cs_ref[...], sn_ref[...], gq[...], gkv[...], wq[...], wqr[...], wk[...], wv[...])
        q_ref[...] = q.astype(BF16)
        k_ref[...] = kf.astype(BF16)
        v_ref[...] = v.astype(BF16)

    tab = BS((tr, 128), lambda i: (i % per, 0))
    return pl.pallas_call(
        body, grid=(m // tr,), in_specs=[BS((tr, MLA_PAD_COLS), lambda i: (i, 0)), tab, tab] + [_const(q.shape) for q in params],
        out_specs=[BS((tr, 1024), lambda i: (i, 0)), BS((tr, 1024), lambda i: (i, 0)), BS((tr, 512), lambda i: (i, 0))],
        out_shape=[SDS((m, 1024), BF16), SDS((m, 1024), BF16), SDS((m, 512), BF16)], compiler_params=_cp("parallel"), name="mla_prep_fwd",
    )(zm, cs, sn, *params)


def _mla_bwd(zm, cs, sn, p, t, dq, dk, dv):
    m = zm.shape[0]
    tr = 128
    per = t // tr
    params = [p["q_norm_g"], p["kv_norm_g"], p["wq"], p["wqr"], p["wk"], p["wv"]]

    def body(z_ref, cs_ref, sn_ref, gq, gkv, wq, wqr, wk, wv, dq_ref, dk_ref, dv_ref, dz_ref, *grads):
        cs_v, sn_v = cs_ref[...], sn_ref[...]
        _, vjp = jax.vjp(
            lambda *q: _mla_fn(q[0], cs_v, sn_v, *q[1:]), z_ref[...], gq[...], gkv[...], wq[...], wqr[...], wk[...], wv[...]
        )
        dz, *dparams = vjp((dq_ref[...], dk_ref[...], dv_ref[...]))
        dz_ref[...] = dz.astype(BF16)

        @pl.when(pl.program_id(0) == 0)
        def _():
            for gr in grads:
                gr[...] = jnp.zeros_like(gr)

        for gr, val in zip(grads, dparams):
            gr[...] += val

    tab = BS((tr, 128), lambda i: (i % per, 0))
    wide = BS((tr, 1024), lambda i: (i, 0))
    return pl.pallas_call(
        body, grid=(m // tr,),
        in_specs=[BS((tr, MLA_PAD_COLS), lambda i: (i, 0)), tab, tab] + [_const(q.shape) for q in params]
        + [wide, wide, BS((tr, 512), lambda i: (i, 0))],
        out_specs=[BS((tr, MLA_PAD_COLS), lambda i: (i, 0))] + [_const(q.shape) for q in params],
        out_shape=[SDS((m, MLA_PAD_COLS), BF16)] + [SDS(q.shape, F32) for q in params],
        compiler_params=_cp("arbitrary"), name="mla_prep_bwd",
    )(zm, cs, sn, *params, dq, dk, dv)


_NT = (((1,), (1,)), ((), ()))
_TN = (((0,), (0,)), ((), ()))


def _attn_fwd(q, kf, v, b, t):
    m = q.shape[0]
    tq = 256
    nq = t // tq

    def body(q_ref, k_ref, v_ref, o_ref, l_ref):
        lo = lax.broadcasted_iota(jnp.int32, (1, 128), 1) < 64
        v_all = v_ref[...]
        o = jnp.zeros((tq, 128), F32)
        lse = []
        for h in range(2):
            hs = slice(128 * h, 128 * h + 128)
            s = lax.dot_general(q_ref[:, hs], k_ref[:, hs], _NT, preferred_element_type=F32)
            mx = jnp.max(s, axis=1, keepdims=True)
            e = jnp.exp(s - mx)
            den = jnp.sum(e, axis=1, keepdims=True)
            vh = jnp.where(lo if h == 0 else jnp.logical_not(lo), v_all, jnp.zeros_like(v_all))
            o = o + jnp.dot(e.astype(BF16), vh, preferred_element_type=F32) / den
            lse.append(mx + jnp.log(den))
        o_ref[...] = o
        l_ref[...] = jnp.where(lo, lse[0], lse[1])

    return pl.pallas_call(
        body, grid=(b, 4, nq),
        in_specs=[BS((tq, 256), lambda bi, hp, i: (bi * nq + i, hp)), BS((t, 256), lambda bi, hp, i: (bi, hp)),
                  BS((t, 128), lambda bi, hp, i: (bi, hp))],
        out_specs=[BS((tq, 128), lambda bi, hp, i: (bi * nq + i, hp))] * 2,
        out_shape=[SDS((m, 512), F32), SDS((m, 512), F32)], compiler_params=_cp("parallel", "parallel", "arbitrary"), name="attn_fwd",
    )(q, kf, v)


def _attn_bwd(q, kf, v, o, lse, do, b, t):
    m = q.shape[0]
    tq = 256
    nq = t // tq

    def body(q_ref, k_ref, v_ref, o_ref, l_ref, do_ref, dq_ref, dk_ref, dv_ref):
        lo = lax.broadcasted_iota(jnp.int32, (1, 128), 1) < 64

        @pl.when(pl.program_id(2) == 0)
        def _():
            dk_ref[...] = jnp.zeros_like(dk_ref)
            dv_ref[...] = jnp.zeros_like(dv_ref)

        v_all, o_all, l_all, do_all = v_ref[...], o_ref[...], l_ref[...], do_ref[...]
        dv_acc = jnp.zeros((t, 128), F32)
        for h in range(2):
            hs = slice(128 * h, 128 * h + 128)
            mask = lo if h == 0 else jnp.logical_not(lo)
            qh, kh = q_ref[:, hs], k_ref[:, hs]
            s = lax.dot_general(qh, kh, _NT, preferred_element_type=F32)
            lse_h = jnp.max(jnp.where(mask, l_all, -jnp.inf), axis=1, keepdims=True)
            pr = jnp.exp(s - lse_h)
            do_h = jnp.where(mask, do_all, 0.0)
            dp = lax.dot_general(do_h.astype(BF16), v_all, _NT, preferred_element_type=F32)
            dsum = jnp.sum(do_h * o_all, axis=1, keepdims=True)
            ds = (pr * (dp - dsum)).astype(BF16)
            dq_ref[:, hs] = jnp.dot(ds, kh, preferred_element_type=F32)
            dk_ref[:, hs] += lax.dot_general(ds, qh, _TN, preferred_element_type=F32)
            dv_acc = dv_acc + lax.dot_general(pr.astype(BF16), do_h.astype(BF16), _TN, preferred_element_type=F32)
        dv_ref[...] += dv_acc

    qspec = BS((tq, 256), lambda bi, hp, i: (bi * nq + i, hp))
    kspec = BS((t, 256), lambda bi, hp, i: (bi, hp))
    vspec = BS((t, 128), lambda bi, hp, i: (bi, hp))
    ospec = BS((tq, 128), lambda bi, hp, i: (bi * nq + i, hp))
    return pl.pallas_call(
        body, grid=(b, 4, nq), in_specs=[qspec, kspec, vspec, ospec, ospec, ospec], out_specs=[qspec, kspec, vspec],
        out_shape=[SDS((m, 1024), F32), SDS((m, 1024), F32), SDS((m, 512), F32)],
        compiler_params=_cp("parallel", "parallel", "arbitrary"), name="attn_bwd",
    )(q, kf, v, o, lse, do)


def _conv3(u, w_ref, b_ref, t):
    up, un = _prev_next(u, t)
    return w_ref[0:1, :] * up + w_ref[1:2, :] * u + w_ref[2:3, :] * un + b_ref[...], up, un


def _ffn_mid_fwd(ug3, uv3, cw, cb):
    b, t, f = ug3.shape
    nc = f // 256

    def body(ug_ref, uv_ref, wg_ref, wv_ref, bg_ref, bv_ref, a_ref):
        gc, _, _ = _conv3(ug_ref[0], wg_ref, bg_ref, t)
        vc, _, _ = _conv3(uv_ref[0], wv_ref, bv_ref, t)
        a_ref[0] = (gc * jax.nn.sigmoid(gc) * vc).astype(BF16)

    blk = BS((1, t, 256), lambda i, j: (i, 0, j))
    return pl.pallas_call(
        body, grid=(b, nc),
        in_specs=[blk, blk, BS((3, 256), lambda i, j: (0, j)), BS((3, 256), lambda i, j: (0, j + nc)),
                  BS((1, 256), lambda i, j: (0, j)), BS((1, 256), lambda i, j: (0, j + nc))],
        out_specs=blk, out_shape=SDS((b, t, f), BF16), compiler_params=_cp("parallel", "parallel"), name="ffn_mid_fwd",
    )(ug3, uv3, cw, cw, cb, cb)


def _ffn_mid_bwd(ug3, uv3, cw, cb, da3):
    b, t, f = ug3.shape
    nc = f // 256

    def half(u, up, un, dc, w_ref):
        dprev, dnext = _prev_next(dc, t)
        du = w_ref[1:2, :] * dc + w_ref[0:1, :] * dnext + w_ref[2:3, :] * dprev
        sums = [jnp.sum(dc * q, axis=0, keepdims=True) for q in (up, u, un)] + [jnp.sum(dc, axis=0, keepdims=True)]
        row = lax.broadcasted_iota(jnp.int32, (8, 256), 0)
        tab = jnp.zeros((8, 256), F32)
        for i, s in enumerate(sums):
            tab = jnp.where(row == i, s, tab)
        return du, tab

    def body(ug_ref, uv_ref, wg_ref, wv_ref, bg_ref, bv_ref, da_ref, dug_ref, duv_ref, tg_ref, tv_ref):
        ug, uv, da = ug_ref[0], uv_ref[0], da_ref[0]
        gc, gp, gn = _conv3(ug, wg_ref, bg_ref, t)
        vc, vp, vn = _conv3(uv, wv_ref, bv_ref, t)
        sg = jax.nn.sigmoid(gc)
        d_gc = da * vc * (sg * (1.0 + gc * (1.0 - sg)))
        d_vc = da * (gc * sg)
        dug, tg = half(ug, gp, gn, d_gc, wg_ref)
        duv, tv = half(uv, vp, vn, d_vc, wv_ref)
        dug_ref[0] = dug.astype(BF16)
        duv_ref[0] = duv.astype(BF16)

        @pl.when(pl.program_id(1) == 0)
        def _():
            tg_ref[...] = jnp.zeros_like(tg_ref)
            tv_ref[...] = jnp.zeros_like(tv_ref)

        tg_ref[...] += tg
        tv_ref[...] += tv

    blk = BS((1, t, 256), lambda j, i: (i, 0, j))
    tab = BS((8, 256), lambda j, i: (0, j))
    return pl.pallas_call(
        body, grid=(nc, b),
        in_specs=[blk, blk, BS((3, 256), lambda j, i: (0, j)), BS((3, 256), lambda j, i: (0, j + nc)),
                  BS((1, 256), lambda j, i: (0, j)), BS((1, 256), lambda j, i: (0, j + nc)), blk],
        out_specs=[blk, blk, tab, tab],
        out_shape=[SDS((b, t, f), BF16), SDS((b, t, f), BF16), SDS((8, f), F32), SDS((8, f), F32)],
        compiler_params=_cp("parallel", "arbitrary"), name="ffn_mid_bwd",
    )(ug3, uv3, cw, cw, cb, cb, da3)


def _add_rows(parts, name, out_dtype=F32):
    r = parts[0].shape[0]
    tr = _row_tile(r, 1024)
    n = len(parts)

    def body(*refs):
        acc = refs[0][...].astype(F32)
        for q in refs[1:n]:
            acc = acc + q[...].astype(F32)
        refs[n][...] = acc.astype(out_dtype)

    row = BS((tr, 128), lambda i: (i, 0))
    return pl.pallas_call(
        body, grid=(r // tr,), in_specs=[row] * n, out_specs=row, out_shape=SDS((r, 128), out_dtype),
        compiler_params=_cp("parallel"), name=name,
    )(*parts)


def _adamw(w, g, m, v, name):
    lead = w.shape[:-2]
    r, c = w.shape[-2:]
    tr = _row_tile(r)

    def body(w_ref, g_ref, m_ref, v_ref, d_ref, m2_ref, v2_ref):
        d, m2, v2 = _adamw_math(w_ref[...], g_ref[...], m_ref[...], v_ref[...])
        d_ref[...] = d
        m2_ref[...] = m2
        v2_ref[...] = v2

    blk = BS((1,) * len(lead) + (tr, c), lambda i: (0,) * len(lead) + (i, 0))
    return pl.pallas_call(
        body, grid=(r // tr,), in_specs=[blk] * 4, out_specs=[blk] * 3, out_shape=[SDS(w.shape, F32)] * 3,
        compiler_params=_cp("parallel"), name=name,
    )(w, g, m, v)


def _place():
    return lax.axis_index("x"), lax.axis_index("y"), lax.axis_index("c")


def _flip(v, bit):
    return 1 - v if bit else v


def _allgather_weights(shard):
    def body(x_ref, out_ref, send_sems, recv_sems):
        _gather_halves(x_ref, out_ref, send_sems, recv_sems, "start")
        _gather_halves(x_ref, out_ref, send_sems, recv_sems, "finish")

    return pl.pallas_call(
        body, out_shape=_gathered_shape(shard), in_specs=[ANY], out_specs=ANY, scratch_shapes=_gather_sems(),
        name="allgather_weights",
    )(shard)


def _gathered_shape(shard):
    return SDS((8 * (shard.shape[0] // 2), 128), shard.dtype)


def _gather_sems():
    return [pltpu.SemaphoreType.DMA((6,)), pltpu.SemaphoreType.DMA((6,))]


def _gather_halves(x_ref, out_ref, send_sems, recv_sems, phase):
    rh = x_ref.shape[0] // 2
    x, y, c = _place()
    me, sibling = (x, y, c), (x, y, 1 - c)
    chips = [(1 - x, y), (x, 1 - y), (1 - x, 1 - y)]
    mine_src = x_ref.at[pl.ds(c * rh, rh), :]

    def rows(px, py, pc):
        return out_ref.at[pl.ds((4 * px + 2 * py + pc) * rh, rh), :]

    def copy(k, block, to, src=None):
        return pltpu.make_async_remote_copy(
            src_ref=rows(*block) if src is None else src, dst_ref=rows(*block), send_sem=send_sems.at[k],
            recv_sem=recv_sems.at[k], device_id=to, device_id_type=MESH,
        )

    first = [copy(j, me, (*chip, c), src=mine_src) for j, chip in enumerate(chips)]
    if phase == "start":
        for cp in first:
            cp.start()
        return
    passed = [copy(3 + j, (*chip, c), sibling) for j, chip in enumerate(chips)]
    for j, chip in enumerate(chips):
        copy(j, (*chip, c), me).wait_recv()
        passed[j].start()
    for j, chip in enumerate(chips):
        copy(3 + j, (*chip, 1 - c), me).wait_recv()
    for cp in first + passed:
        cp.wait_send()


def _scatter_partials(g_ref, recv_ref, send_sems, recv_sems, phase):
    x, y, c = _place()
    copies = []
    for j, (fx, fy) in enumerate(((1, 0), (0, 1), (1, 1))):
        px, py = _flip(x, fx), _flip(y, fy)
        copies.append(pltpu.make_async_remote_copy(
            src_ref=g_ref.at[2 * px + py], dst_ref=recv_ref.at[j], send_sem=send_sems.at[j], recv_sem=recv_sems.at[j],
            device_id=(px, py, c), device_id_type=MESH,
        ))
    if phase == "start":
        for cp in copies:
            cp.start()
        return
    for cp in copies:
        cp.wait_recv()
    for cp in copies:
        cp.wait_send()


def _scatter_sems():
    return [pltpu.SemaphoreType.DMA((3,)), pltpu.SemaphoreType.DMA((3,))]


def _scatter_to_chips(g):
    def body(g_ref, recv_ref, send_sems, recv_sems):
        _scatter_partials(g_ref, recv_ref, send_sems, recv_sems, "start")
        _scatter_partials(g_ref, recv_ref, send_sems, recv_sems, "finish")

    return pl.pallas_call(
        body, out_shape=SDS((3,) + g.shape[1:], g.dtype), in_specs=[ANY], out_specs=ANY,
        scratch_shapes=[pltpu.SemaphoreType.DMA((3,)), pltpu.SemaphoreType.DMA((3,))], name="scatter_grads",
    )(g)


def _send_to_sibling(a, half_of_rows):
    rh = a.shape[1] // 2

    def body(a_ref, b_ref, send_sem, recv_sem):
        x, y, c = _place()
        src = a_ref.at[:, pl.ds((1 - c) * rh, rh), :] if half_of_rows else a_ref
        cp = pltpu.make_async_remote_copy(
            src_ref=src, dst_ref=b_ref, send_sem=send_sem, recv_sem=recv_sem, device_id=(x, y, 1 - c), device_id_type=MESH
        )
        cp.start()
        cp.wait()

    shape = (a.shape[0], rh, 128) if half_of_rows else a.shape
    return pl.pallas_call(
        body, out_shape=SDS(shape, a.dtype), in_specs=[ANY], out_specs=ANY,
        scratch_shapes=[pltpu.SemaphoreType.DMA, pltpu.SemaphoreType.DMA],
        name="sibling_halves" if half_of_rows else "sibling_swap",
    )(a)


def _allreduce_small(v):
    r = v.shape[0]

    def body(v_ref, out_ref, buf_ref, send_sems, recv_sems):
        x, y, c = _place()
        buf_ref[0] = v_ref[...]
        copies = []
        for k in range(1, 8):
            peer = (_flip(x, k >> 2 & 1), _flip(y, k >> 1 & 1), _flip(c, k & 1))
            cp = pltpu.make_async_remote_copy(
                src_ref=v_ref, dst_ref=buf_ref.at[k], send_sem=send_sems.at[k - 1], recv_sem=recv_sems.at[k - 1],
                device_id=peer, device_id_type=MESH,
            )
            cp.start()
            copies.append(cp)
        for cp in copies:
            cp.wait_recv()
        acc = None
        for d in range(8):
            slot = 4 * _flip(x, d >> 2 & 1) + 2 * _flip(y, d >> 1 & 1) + _flip(c, d & 1)
            term = buf_ref[slot]
            acc = term if acc is None else acc + term
        out_ref[...] = acc
        for cp in copies:
            cp.wait_send()

    return pl.pallas_call(
        body, out_shape=SDS(v.shape, F32), in_specs=[VMEM], out_specs=VMEM,
        scratch_shapes=[pltpu.VMEM((8, r, 128), F32), pltpu.SemaphoreType.DMA((7,)), pltpu.SemaphoreType.DMA((7,))],
        name="allreduce_small",
    )(v)


_BIG_A = (
    ("w_in", 1, False), ("decay_w2_fwd", 1, False), ("decay_w2_bwd", 1, False), ("iclr_a2_fwd", 1, False),
    ("iclr_a2_bwd", 1, False), ("gate_g2", 1, False),
)
_BIG_B = (
    ("w_uq", 0, False), ("w_ukv", 1, False), ("w_out", 0, False), ("w_ffn_up", 1, False), ("ffn_conv_w", 1, True),
    ("w_ffn_down", 0, False),
)
_BIG = _BIG_A + _BIG_B
_SMALL = (
    "ln_mix_g", "shift_mu_prev", "shift_mu_next", "decay_w0_fwd", "decay_w0_bwd", "iclr_a0_fwd", "iclr_a0_bwd", "k_k",
    "k_a", "r_k", "ln_x_g", "ln_x_b", "q_norm_g", "kv_norm_g", "mla_out_g", "ln_ffn_g", "ffn_conv_b", "ln_final_g",
)
_WEIGHTS = (
    "ln_mix_g", "w_in", "shift_mu_prev", "shift_mu_next", "decay_w0_fwd", "decay_w2_fwd", "decay_w0_bwd", "decay_w2_bwd",
    "iclr_a0_fwd", "iclr_a2_fwd", "iclr_a0_bwd", "iclr_a2_bwd", "gate_g2", "k_k", "k_a", "r_k", "ln_x_g", "ln_x_b",
    "q_norm_g", "w_uq", "kv_norm_g", "w_ukv", "mla_out_g", "w_out", "ln_ffn_g", "w_ffn_up", "ffn_conv_w", "ffn_conv_b",
    "w_ffn_down", "ln_final_g",
)


def _pad_rows(flat, rows):
    return jnp.pad(flat, (0, rows * 128 - flat.shape[0])).reshape(rows, 128)


def _rows_for(n, mult):
    rows = -(-n // 128)
    return -(-rows // mult) * mult


def _pack_shards_bf16(arrs, entries):
    parts = []
    for name, _, raw in entries:
        w = arrs[name][0]
        flat = lax.bitcast_convert_type(w, BF16).reshape(-1) if raw else w.astype(BF16).reshape(-1)
        parts.append(_pad_rows(flat, _rows_for(flat.shape[0], 32)))
    return jnp.concatenate(parts, axis=0)


def _unpack_gathered(g4, arrs, entries):
    out, off = {}, 0
    for name, axis, raw in entries:
        a, b = arrs[name].shape[1:]
        n = a * b * (2 if raw else 1)
        rows = _rows_for(n, 32)
        seg = g4[:, off:off + rows].reshape(4, rows * 128)[:, :n]
        off += rows
        if raw:
            seg = lax.bitcast_convert_type(seg.reshape(4, a * b, 2), F32)
        seg = seg.reshape(4, a, b)
        out[name] = jnp.concatenate([seg[s] for s in range(4)], axis=1) if axis == 1 else seg.reshape(4 * a, b)
    return out


def _pack_grads(full, arrs, entries):
    parts = []
    for name, axis, _ in entries:
        a, b = arrs[name].shape[1:]
        g = full[name]
        sh = g.reshape(a, 4, b).transpose(1, 0, 2) if axis == 1 else g.reshape(4, a, b)
        rows = _rows_for(a * b, 8)
        parts.append(jnp.pad(sh.reshape(4, a * b), ((0, 0), (0, rows * 128 - a * b))).reshape(4, rows, 128))
    total = sum(q.shape[1] for q in parts)
    parts.append(jnp.zeros((4, -(-total // 1024) * 1024 - total, 128), F32))
    return jnp.concatenate(parts, axis=1)


def _unpack_grads(g, arrs, entries):
    out, off = {}, 0
    for name, _, _ in entries:
        a, b = arrs[name].shape[1:]
        rows = _rows_for(a * b, 8)
        out[name] = g[off:off + rows].reshape(-1)[:a * b].reshape(1, a, b)
        off += rows
    return out


def _pack_small(vals):
    flat = jnp.concatenate([vals[n].reshape(-1).astype(F32) for n in _SMALL] + [vals["_loss"].reshape(-1)])
    return _pad_rows(flat, _rows_for(flat.shape[0], 8))


def _unpack_small(buf, arrs):
    flat, out, off = buf.reshape(-1), {}, 0
    for n in _SMALL:
        size = arrs[n].size
        out[n] = flat[off:off + size].reshape(arrs[n].shape)
        off += size
    out["_loss"] = flat[off]
    return out


def _rot_cols(w):
    return jnp.concatenate([-w[..., 16:], w[..., :16]], axis=-1)


def _rot_cols_t(g):
    return jnp.concatenate([g[..., 16:], -g[..., :16]], axis=-1)


def _rope_tables(t):
    inv = jnp.power(ROPE_THETA, -jnp.arange(0, ROPE_DIM, 2, dtype=F32) / ROPE_DIM)
    ang = jnp.arange(t, dtype=F32)[:, None] * inv[None, :]
    one, zero = jnp.ones((t, 64), F32), jnp.zeros((t, 64), F32)
    cs = jnp.concatenate([one, jnp.cos(ang), jnp.cos(ang), zero[:, :32]], axis=1)
    sn = jnp.concatenate([zero, jnp.sin(ang), jnp.sin(ang), zero[:, :32]], axis=1)
    return cs, sn


def _block_diag(a, b):
    za = jnp.zeros_like(a)
    return jnp.concatenate([jnp.concatenate([a, za], axis=1), jnp.concatenate([za, b], axis=1)], axis=0)


def kernel(x, ln_mix_g, w_in, shift_mu_prev, shift_mu_next, decay_w0_fwd, decay_w2_fwd, decay_w0_bwd, decay_w2_bwd, iclr_a0_fwd, iclr_a2_fwd, iclr_a0_bwd, iclr_a2_bwd, gate_g2, k_k, k_a, r_k, ln_x_g, ln_x_b, q_norm_g, w_uq, kv_norm_g, w_ukv, mla_out_g, w_out, ln_ffn_g, w_ffn_up, ffn_conv_w, ffn_conv_b, w_ffn_down, ln_final_g, loss_target, m_ln_mix_g, m_w_in, m_shift_mu_prev, m_shift_mu_next, m_decay_w0_fwd, m_decay_w2_fwd, m_decay_w0_bwd, m_decay_w2_bwd, m_iclr_a0_fwd, m_iclr_a2_fwd, m_iclr_a0_bwd, m_iclr_a2_bwd, m_gate_g2, m_k_k, m_k_a, m_r_k, m_ln_x_g, m_ln_x_b, m_q_norm_g, m_w_uq, m_kv_norm_g, m_w_ukv, m_mla_out_g, m_w_out, m_ln_ffn_g, m_w_ffn_up, m_ffn_conv_w, m_ffn_conv_b, m_w_ffn_down, m_ln_final_g, v_ln_mix_g, v_w_in, v_shift_mu_prev, v_shift_mu_next, v_decay_w0_fwd, v_decay_w2_fwd, v_decay_w0_bwd, v_decay_w2_bwd, v_iclr_a0_fwd, v_iclr_a2_fwd, v_iclr_a0_bwd, v_iclr_a2_bwd, v_gate_g2, v_k_k, v_k_a, v_r_k, v_ln_x_g, v_ln_x_b, v_q_norm_g, v_w_uq, v_kv_norm_g, v_w_ukv, v_mla_out_g, v_w_out, v_ln_ffn_g, v_w_ffn_up, v_ffn_conv_w, v_ffn_conv_b, v_w_ffn_down, v_ln_final_g):
    arrs = dict(locals())
    b, t, d = x.shape
    m = b * t
    x2 = x.reshape(m, d)
    tgt = loss_target.reshape(m, d)
    vec = lambda n: arrs[n].reshape(1, -1)

    core = lax.axis_index("c")
    chip = 2 * lax.axis_index("x") + lax.axis_index("y")
    def unpack(gathered, shard, entries):
        g4 = lax.dynamic_update_index_in_dim(gathered.reshape(N_CHIPS, -1, 128), shard, chip, axis=0)
        return _unpack_gathered(g4, arrs, entries)

    shard_a, shard_b = _pack_shards_bf16(arrs, _BIG_A), _pack_shards_bf16(arrs, _BIG_B)
    fw = unpack(_allgather_weights(shard_a), shard_a, _BIG_A)
    win = fw["w_in"]
    zc = jnp.zeros((d, 64), BF16)
    w_kr = win[:, 2944:2976]
    win_m = jnp.concatenate([win[:, 1920:2944], zc, w_kr, zc[:, :32], zc, _rot_cols(w_kr), zc[:, :32]], axis=1)
    win_r = win[:, :RWKV_COLS]
    head = jnp.arange(512) // HEAD_DIM
    rw = dict(
        w0=jnp.concatenate([vec("decay_w0_fwd"), vec("decay_w0_bwd")], axis=1),
        w2=_block_diag(fw["decay_w2_fwd"], fw["decay_w2_bwd"]).astype(F32),
        a0=jnp.concatenate([vec("iclr_a0_fwd"), vec("iclr_a0_bwd")], axis=1),
        a2=_block_diag(fw["iclr_a2_fwd"], fw["iclr_a2_bwd"]).astype(F32),
        g2=fw["gate_g2"].astype(F32), k_k=vec("k_k"), k_a=vec("k_a"), r_k=vec("r_k"), ln_x_g=vec("ln_x_g"), ln_x_b=vec("ln_x_b"),
        ones_bd=(head[:, None] == head[None, :]).astype(F32),
    )
    cs, sn = _rope_tables(t)

    n1 = _rms_fwd(x2, vec("ln_mix_g"), "rms_mix")
    zm = _mm(n1, win_m, "nn", "proj_in_mla")
    zr = _mm(n1, win_r, "nn", "proj_in_rwkv")
    zs = _shift_fwd(zr.reshape(b, t, RWKV_COLS), vec("shift_mu_prev"), vec("shift_mu_next"))
    zs2 = zs.reshape(m, RWKV_COLS)
    wf, wb, kf, kb, kk, kaf, kab, gate = _prep_fwd(zs2, rw)
    r4 = lambda a: a.reshape(b, t, 512)
    f2 = lambda a: a.reshape(m, 512)
    kk4 = r4(kk)
    ops_f = (r4(wf), r4(kf), r4(kaf))
    ops_b = (r4(wb), r4(kb), r4(kab))
    y_f, hist_f, y_b, hist_b, s_last, gathered_b = _scan_fwd(zs, kk4, ops_f, ops_b, shard_b)
    fw.update(unpack(gathered_b, shard_b, _BIG_B))
    uq = fw["w_uq"].astype(F32).reshape(Q_RANK, HEADS, 96)
    z32 = jnp.zeros((Q_RANK, HEADS, 32), F32)
    wq = jnp.concatenate([uq[..., :64], uq[..., 64:], z32], axis=-1).reshape(Q_RANK, 1024)
    wqr = jnp.concatenate([z32, z32, _rot_cols(uq[..., 64:]), z32], axis=-1).reshape(Q_RANK, 1024)
    ukv = fw["w_ukv"].astype(F32).reshape(KV_RANK, HEADS, 128)
    wk = jnp.concatenate([ukv[..., :64], jnp.zeros_like(ukv[..., :64])], axis=-1).reshape(KV_RANK, 1024)
    wv = ukv[..., 64:].reshape(KV_RANK, 512)
    mp = dict(q_norm_g=vec("q_norm_g"), kv_norm_g=vec("kv_norm_g"), wq=wq, wqr=wqr, wk=wk, wv=wv)
    w_up_g, w_up_v = fw["w_ffn_up"][:, :D_FF], fw["w_ffn_up"][:, D_FF:]
    cw, cb = fw["ffn_conv_w"], vec("ffn_conv_b")
    y_f, y_b = f2(y_f), f2(y_b)
    y_rwkv = _post_fwd(y_f, y_b, zs2, kf, kb, gate, rw)
    q, kfull, v = _mla_fwd(zm, cs, sn, mp, t)
    o, lse = _attn_fwd(q, kfull, v, b, t)
    y_mla = _rms_fwd(o, vec("mla_out_g"), "rms_mla_out")
    ymix = jnp.concatenate([y_rwkv, y_mla], axis=1)
    h1 = _mm(ymix, fw["w_out"], "nn", "proj_out", add=x2)
    n2 = _rms_fwd(h1, vec("ln_ffn_g"), "rms_ffn")
    ug = _mm(n2, w_up_g, "nn", "ffn_up_gate")
    uv = _mm(n2, w_up_v, "nn", "ffn_up_val")
    r3f = lambda a: a.reshape(b, t, D_FF)
    act = _ffn_mid_fwd(r3f(ug), r3f(uv), cw, cb).reshape(m, D_FF)
    h2 = _mm(act, fw["w_ffn_down"], "nn", "ffn_down", add=h1)
    loss_tab, dh2, g_ln_final = _final(h2, vec("ln_final_g"), tgt)

    gfull = {}
    dact = _mm(dh2, fw["w_ffn_down"], "nt", "d_ffn_act")
    gfull["w_ffn_down"] = _mm(act, dh2, "tn", "g_ffn_down")
    dug, duv, tab_g, tab_v = _ffn_mid_bwd(r3f(ug), r3f(uv), cw, cb, r3f(dact))
    dug, duv = dug.reshape(m, D_FF), duv.reshape(m, D_FF)
    gfull["ffn_conv_w"] = jnp.concatenate([tab_g[0:3], tab_v[0:3]], axis=1)
    g_conv_b = jnp.concatenate([tab_g[3:4], tab_v[3:4]], axis=1)
    dn2 = _mm(duv, w_up_v, "nt", "d_ffn_in_val", add=_mm(dug, w_up_g, "nt", "d_ffn_in_gate"))
    gfull["w_ffn_up"] = jnp.concatenate([_mm(n2, dug, "tn", "g_ffn_up_gate"), _mm(n2, duv, "tn", "g_ffn_up_val")], axis=1)
    dh1, g_ln_ffn = _rms_bwd(h1, vec("ln_ffn_g"), dn2, "rms_ffn_bwd", dres=dh2)
    dymix = _mm(dh1, fw["w_out"], "nt", "d_mix")
    gfull["w_out"] = _mm(ymix, dh1, "tn", "g_w_out")
    do, g_mla_out = _rms_bwd(o, vec("mla_out_g"), dymix, "rms_mla_out_bwd", dy_block=1)
    dq, dk, dv = _attn_bwd(q, kfull, v, o, lse, do, b, t)
    dzm, g_qn, g_kvn, g_wq, g_wqr, g_wk, g_wv = _mla_bwd(zm, cs, sn, mp, t, dq, dk, dv)
    gq3, gqr3 = g_wq.reshape(Q_RANK, HEADS, 128), g_wqr.reshape(Q_RANK, HEADS, 128)
    gfull["w_uq"] = jnp.concatenate(
        [gq3[..., :64], gq3[..., 64:96] + _rot_cols_t(gqr3[..., 64:96])], axis=-1
    ).reshape(Q_RANK, HEADS * 96)
    gfull["w_ukv"] = jnp.concatenate(
        [g_wk.reshape(KV_RANK, HEADS, 128)[..., :64], g_wv.reshape(KV_RANK, HEADS, 64)], axis=-1
    ).reshape(KV_RANK, 1024)
    def cores_first(entries, tag):
        packed = _pack_grads(gfull, arrs, entries)
        rh = packed.shape[1] // 2
        own = lax.dynamic_slice_in_dim(packed, core * rh, rh, axis=1)
        sib = _send_to_sibling(packed, True)
        return _add_rows([own.reshape(4 * rh, 128), sib.reshape(4 * rh, 128)], "sum_cores_" + tag, BF16).reshape(4, rh, 128)

    def chips_then_join(chip_part, recv, entries, tag):
        mine = lax.dynamic_index_in_dim(chip_part, chip, axis=0, keepdims=False)
        half = _add_rows([mine, recv[0], recv[1], recv[2]], "sum_chips_" + tag)
        other = _send_to_sibling(half, False)
        lower = jnp.where(core == 0, half, other)
        upper = jnp.where(core == 0, other, half)
        return _unpack_grads(jnp.concatenate([lower, upper], axis=0), arrs, entries)

    part_b = cores_first(_BIG_B, "b")
    dys, dr_p, dk_p, dv_p, dgate, g_rk, g_lnx_g, g_lnx_b = _post_bwd(y_f, y_b, zs2, kf, kb, gate, rw, dymix)
    (dr_f, dwf, dkf, dkk_f, dkaf, dv_f, dr_b, dwb, dkb, dkk_b, dkab, dv_b, recv_b) = _scan_bwd(
        zs, kk4, r4(dys), ops_f, hist_f, ops_b, hist_b, s_last, part_b)
    g_big = chips_then_join(part_b, recv_b, _BIG_B, "b")
    cts = dict(dwf=f2(dwf), dwb=f2(dwb), dkf=f2(dkf), dkb=f2(dkb), dkk_f=f2(dkk_f), dkk_b=f2(dkk_b), dkaf=f2(dkaf), dkab=f2(dkab),
               dr_f=f2(dr_f), dr_b=f2(dr_b), dr_p=dr_p, dk_p=dk_p, dv_p=dv_p, dg=dgate, dv_f=f2(dv_f), dv_b=f2(dv_b))
    dzs, g_w0, g_w2, g_a0, g_a2, g_g2, g_kk, g_ka = _prep_bwd(zs2, rw, cts)
    dzr, g_mu_p, g_mu_n = _shift_bwd(dzs.reshape(b, t, RWKV_COLS), zr.reshape(b, t, RWKV_COLS), vec("shift_mu_prev"), vec("shift_mu_next"))
    dzr = dzr.reshape(m, RWKV_COLS)
    gfull["decay_w2_fwd"], gfull["decay_w2_bwd"] = g_w2[:64, :512], g_w2[64:, 512:]
    gfull["iclr_a2_fwd"], gfull["iclr_a2_bwd"] = g_a2[:64, :512], g_a2[64:, 512:]
    gfull["gate_g2"] = g_g2
    dn1 = _mm(dzr, win_r, "nt", "d_proj_in_rwkv", add=_mm(dzm, win_m, "nt", "d_proj_in_mla"))
    g_m = _mm(n1, dzm, "tn", "g_w_in_mla")
    g_r = _mm(n1, dzr, "tn", "g_w_in_rwkv")
    g_kr = g_m[:, 1088:1120] + _rot_cols_t(g_m[:, 1216:1248])
    gfull["w_in"] = jnp.concatenate([g_r, g_m[:, :1024], g_kr], axis=1)
    dx, g_ln_mix = _rms_bwd(x2, vec("ln_mix_g"), dn1, "rms_mix_bwd", dres=dh1)

    part_a = cores_first(_BIG_A, "a")
    g_big.update(chips_then_join(part_a, _scatter_to_chips(part_a), _BIG_A, "a"))
    small = {
        "ln_mix_g": g_ln_mix, "shift_mu_prev": g_mu_p, "shift_mu_next": g_mu_n, "decay_w0_fwd": g_w0[:, :512],
        "decay_w0_bwd": g_w0[:, 512:], "iclr_a0_fwd": g_a0[:, :512], "iclr_a0_bwd": g_a0[:, 512:], "k_k": g_kk, "k_a": g_ka,
        "r_k": g_rk, "ln_x_g": g_lnx_g, "ln_x_b": g_lnx_b, "q_norm_g": g_qn, "kv_norm_g": g_kvn, "mla_out_g": g_mla_out,
        "ln_ffn_g": g_ln_ffn, "ffn_conv_b": g_conv_b, "ln_final_g": g_ln_final,
        "_loss": jnp.pad(loss_tab[0, 0:1], (0, 127)),
    }
    g_small_buf = _allreduce_small(_pack_small(small))
    g_small = _unpack_small(g_small_buf, arrs)

    grads, deltas, new_m, new_v = {}, {}, {}, {}
    for name, _, _ in _BIG:
        grads[name] = g_big[name]
        deltas[name], new_m[name], new_v[name] = _adamw(
            arrs[name], g_big[name], arrs["m_" + name], arrs["v_" + name], "adamw_" + name)
    pk = lambda pre: _pack_small({**{n: arrs[pre + n] for n in _SMALL}, "_loss": jnp.zeros((128,), F32)})
    sd, sm, sv = _adamw(pk(""), g_small_buf, pk("m_"), pk("v_"), "adamw_small")
    sd, sm, sv = _unpack_small(sd, arrs), _unpack_small(sm, arrs), _unpack_small(sv, arrs)
    for n in _SMALL:
        grads[n], deltas[n], new_m[n], new_v[n] = g_small[n], sd[n], sm[n], sv[n]

    return (g_small["_loss"], dx.reshape(b, t, d), *[grads[n] for n in _WEIGHTS], *[deltas[n] for n in _WEIGHTS],
            *[new_m[n] for n in _WEIGHTS], *[new_v[n] for n in _WEIGHTS])
```

```python
import functools
import math

import jax
import jax.numpy as jnp
from jax import lax
from jax.experimental import pallas as pl
from jax.experimental.pallas import tpu as pltpu

F32, BF16 = jnp.float32, jnp.bfloat16
MESH = pl.DeviceIdType.MESH
ANY = pl.BlockSpec(memory_space=pl.ANY)
VMEM = pl.BlockSpec(memory_space=pltpu.VMEM)
BS = pl.BlockSpec
SDS = jax.ShapeDtypeStruct

NORM_EPS = 1e-6
GN_EPS = 64e-5
L2_EPS = 1e-12
HEADS = 8
HEAD_DIM = 64
D_RWKV = HEADS * HEAD_DIM
ROPE_DIM = 32
ROPE_THETA = 10000.0
MLA_SCALE = (64 + ROPE_DIM) ** -0.5
Q_RANK, KV_RANK = 768, 256
RWKV_COLS = 1920
MLA_PAD_COLS = Q_RANK + KV_RANK + 256
D_FF = 2816
ADAM_LR, ADAM_B1, ADAM_B2, ADAM_EPS, ADAM_WD, ADAM_STEP = 0.001, 0.9, 0.999, 1e-08, 0.01, 10

V7X_LANES = 128
V7X_VMEM_LIMIT = 56 * 1024 * 1024
SCAN_CHUNK = 32
N_CHIPS = 4


def _cp(*sem):
    return pltpu.CompilerParams(dimension_semantics=sem, vmem_limit_bytes=V7X_VMEM_LIMIT)


def _tile(n, cands=(512, 640, 384, 256, 128)):
    for c in cands:
        if n % c == 0:
            return c
    return n


def _row_tile(n, cap=256):
    best = n
    for t in range(8, cap + 1, 8):
        if n % t == 0:
            best = t
    return best if best <= cap or n <= cap else n


def _rms(x, g):
    ms = jnp.mean(x * x, axis=-1, keepdims=True)
    return x * lax.rsqrt(ms + NORM_EPS) * g


@jax.custom_vjp
def _bdot(x, w):
    return jnp.dot(x.astype(BF16), w.astype(BF16), preferred_element_type=F32)


def _bdot_fwd(x, w):
    return _bdot(x, w), (x, w)


def _bdot_bwd(res, ct):
    x, w = res
    c = ct.astype(BF16)
    dx = lax.dot_general(c, w.astype(BF16), (((1,), (1,)), ((), ())), preferred_element_type=F32)
    dw = lax.dot_general(x.astype(BF16), c, (((0,), (0,)), ((), ())), preferred_element_type=F32)
    return dx.astype(x.dtype), dw.astype(w.dtype)


_bdot.defvjp(_bdot_fwd, _bdot_bwd)


@jax.custom_vjp
def _headsum(x, ones_bd):
    hi = x.astype(BF16)
    mid = (x - hi.astype(F32)).astype(BF16)
    ob = ones_bd.astype(BF16)
    return jnp.dot(hi, ob, preferred_element_type=F32) + jnp.dot(mid, ob, preferred_element_type=F32)


def _headsum_fwd(x, ones_bd):
    return _headsum(x, ones_bd), ones_bd


def _headsum_bwd(ones_bd, ct):
    return _headsum(ct, ones_bd), jnp.zeros_like(ones_bd)


_headsum.defvjp(_headsum_fwd, _headsum_bwd)


def _prep_fn(zs, w0, w2, a0, a2, g2, k_k, k_a, ones_bd):
    k = zs[:, 512:1024]
    wd = zs[:, 1536:1664]
    ad = zs[:, 1664:1792]
    gd = zs[:, 1792:1920]
    logit = w0 + _bdot(jnp.tanh(wd), w2)
    w = jnp.exp(-math.exp(-0.5) * jax.nn.sigmoid(logit))
    a = jax.nn.sigmoid(a0 + _bdot(ad, a2))
    g = _bdot(jax.nn.sigmoid(gd), g2)
    kkr = k * k_k
    nrm = jnp.sqrt(_headsum(kkr * kkr, ones_bd))
    kk = kkr / jnp.maximum(nrm, L2_EPS)
    a_f, a_b = a[:, :512], a[:, 512:]
    kf = k * (1.0 + (a_f - 1.0) * k_a)
    kb = k * (1.0 + (a_b - 1.0) * k_a)
    return w[:, :512], w[:, 512:], kf, kb, kk, kk * a_f, kk * a_b, g


def _post_fn(y, r, kf, kb, v, g, r_k, ln_g, ln_b, ones_bd):
    mu =_headsum(y, ones_bd) * (1.0 / HEAD_DIM)
    yc = y - mu
    var = _headsum(yc * yc, ones_bd) * (1.0 / HEAD_DIM)
    yn = yc * lax.rsqrt(var + GN_EPS) * ln_g + ln_b
    bonus = _headsum(r * (kf + kb) * r_k, ones_bd) * v
    return (yn + bonus) * g


def _cat8(x):
    return jnp.concatenate([x] * HEADS, axis=1)


def _mla_fn(zm, cs, sn, gq, gkv, wq, wqr, wk, wv):
    cq = zm[:, :Q_RANK]
    ckv = zm[:, Q_RANK:Q_RANK + KV_RANK]
    kr = zm[:, Q_RANK + KV_RANK:Q_RANK + KV_RANK + 128]
    krr = zm[:, Q_RANK + KV_RANK + 128:]
    cqn = _rms(cq, gq)
    ckvn = _rms(ckv, gkv)
    q = (_bdot(cqn, wq) * _cat8(cs) + _bdot(cqn, wqr) * _cat8(sn)) * MLA_SCALE
    kro = kr * cs + krr * sn
    kfull = _bdot(ckvn, wk) + _cat8(kro)
    v = _bdot(ckvn, wv)
    return q, kfull, v


def _adamw_math(w, g, m, v):
    m2 = ADAM_B1 * m + (1.0 - ADAM_B1) * g
    v2 = ADAM_B2 * v + (1.0 - ADAM_B2) * (g * g)
    m_hat = m2 / (1.0 - ADAM_B1 ** ADAM_STEP)
    v_hat = v2 / (1.0 - ADAM_B2 ** ADAM_STEP)
    delta = -ADAM_LR * (m_hat / (jnp.sqrt(v_hat) + ADAM_EPS) + ADAM_WD * w)
    return delta, m2, v2


_DIMS = {"nn": (((1,), (0,)), ((), ())), "nt": (((1,), (1,)), ((), ())), "tn": (((0,), (0,)), ((), ()))}


def _mm(a, b, mode, name, out_dtype=F32, add=None):
    if mode == "nn":
        (m, k), (_, n) = a.shape, b.shape
    elif mode == "nt":
        (m, k), (n, _) = a.shape, b.shape
    else:
        (k, m), (_, n) = a.shape, b.shape
    big = (1024, 1408, 768, 640, 512, 384, 256, 128)
    tm, tn, tk = _tile(m, big), _tile(n, big), _tile(k, (512, 1408, 640, 384, 256, 128))
    nk = k // tk

    def body(a_ref, b_ref, *rest):
        if add is None:
            o_ref, acc_ref = rest
        else:
            add_ref, o_ref, acc_ref = rest
        kk = pl.program_id(2)

        @pl.when(kk == 0)
        def _():
            acc_ref[...] = jnp.zeros_like(acc_ref)

        acc_ref[...] += lax.dot_general(
            a_ref[...].astype(BF16), b_ref[...].astype(BF16), _DIMS[mode], preferred_element_type=F32
        )

        @pl.when(kk == nk - 1)
        def _():
            r = acc_ref[...]
            if add is not None:
                r = r + add_ref[...]
            o_ref[...] = r.astype(out_dtype)

    a_spec = BS((tk, tm), lambda i, j, kk: (kk, i)) if mode == "tn" else BS((tm, tk), lambda i, j, kk: (i, kk))
    b_spec = BS((tn, tk), lambda i, j, kk: (j, kk)) if mode == "nt" else BS((tk, tn), lambda i, j, kk: (kk, j))
    o_spec = BS((tm, tn), lambda i, j, kk: (i, j))
    ins, specs = [a, b], [a_spec, b_spec]
    if add is not None:
        ins.append(add)
        specs.append(o_spec)
    return pl.pallas_call(
        body, grid=(m // tm, n // tn, nk), in_specs=specs, out_specs=o_spec, out_shape=SDS((m, n), out_dtype),
        scratch_shapes=[pltpu.VMEM((tm, tn), F32)], compiler_params=_cp("parallel", "parallel", "arbitrary"), name=name,
    )(*ins)


def _rms_fwd(x, g, name):
    m, d = x.shape
    tr = _tile(m)

    def body(x_ref, g_ref, o_ref):
        o_ref[...] = _rms(x_ref[...], g_ref[...]).astype(BF16)

    return pl.pallas_call(
        body, grid=(m // tr,), in_specs=[BS((tr, d), lambda i: (i, 0)), BS((1, d), lambda i: (0, 0))],
        out_specs=BS((tr, d), lambda i: (i, 0)), out_shape=SDS((m, d), BF16), compiler_params=_cp("parallel"), name=name,
    )(x, g)


def _rms_bwd(x, g, dy, name, dres=None, dy_block=0):
    m, d = x.shape
    tr = _row_tile(m)

    def body(x_ref, g_ref, dy_ref, *rest):
        if dres is None:
            dx_ref, dg_ref = rest
        else:
            dres_ref, dx_ref, dg_ref = rest
        _, vjp = jax.vjp(_rms, x_ref[...], g_ref[...])
        dx, dg = vjp(dy_ref[...])
        if dres is not None:
            dx = dx + dres_ref[...]
        dx_ref[...] = dx

        @pl.when(pl.program_id(0) == 0)
        def _():
            dg_ref[...] = jnp.zeros_like(dg_ref)

        dg_ref[...] += dg

    row = BS((tr, d), lambda i: (i, 0))
    vec = BS((1, d), lambda i: (0, 0))
    ins, specs = [x, g, dy], [row, vec, BS((tr, d), lambda i: (i, dy_block))]
    if dres is not None:
        ins.append(dres)
        specs.append(row)
    return pl.pallas_call(
        body, grid=(m // tr,), in_specs=specs, out_specs=[row, vec], out_shape=[SDS((m, d), F32), SDS((1, d), F32)],
        compiler_params=_cp("arbitrary"), name=name,
    )(*ins)


def _final(h, g, tgt):
    m, d = h.shape
    tr = _row_tile(m)

    def loss_fn(hh, gg, tt):
        e = _rms(hh, gg) - tt
        return 0.5 * jnp.sum(e * e) * (1.0 / d)

    def body(h_ref, g_ref, t_ref, l_ref, dh_ref, dg_ref):
        val, (dh, dg) = jax.value_and_grad(loss_fn, argnums=(0, 1))(h_ref[...], g_ref[...], t_ref[...])
        dh_ref[...] = dh

        @pl.when(pl.program_id(0) == 0)
        def _():
            dg_ref[...] = jnp.zeros_like(dg_ref)
            l_ref[...] = jnp.zeros_like(l_ref)

        dg_ref[...] += dg
        l_ref[...] += jnp.full(l_ref.shape, val, F32)

    row = BS((tr, d), lambda i: (i, 0))
    vec = BS((1, d), lambda i: (0, 0))
    return pl.pallas_call(
        body, grid=(m // tr,), in_specs=[row, vec, row], out_specs=[BS((8, 128), lambda i: (0, 0)), row, vec],
        out_shape=[SDS((8, 128), F32), SDS((m, d), F32), SDS((1, d), F32)], compiler_params=_cp("arbitrary"), name="final_loss",
    )(h, g, tgt)


def _prev_next(z, t):
    row = lax.broadcasted_iota(jnp.int32, z.shape, 0)
    zp = jnp.where(row == 0, 0.0, pltpu.roll(z, 1, axis=0))
    zn = jnp.where(row == t - 1, 0.0, pltpu.roll(z, t - 1, axis=0))
    return zp, zn


def _shift_fwd(z3, mu_p, mu_n):
    b, t, c = z3.shape
    nc = c // 128

    def body(z_ref, mp_ref, mn_ref, o_ref):
        z = z_ref[0]
        zp, zn = _prev_next(z, t)
        o_ref[0] = z + mp_ref[...] * (zp - z) + mn_ref[...] * (zn - z)

    blk = BS((1, t, 128), lambda i, j: (i, 0, j))
    vec = BS((1, 128), lambda i, j: (0, j))
    return pl.pallas_call(
        body, grid=(b, nc), in_specs=[blk, vec, vec], out_specs=blk, out_shape=SDS((b, t, c), F32),
        compiler_params=_cp("parallel", "parallel"), name="shift_fwd",
    )(z3, mu_p, mu_n)


def _shift_bwd(dzs3, z3, mu_p, mu_n):
    b, t, c = z3.shape
    nc = c // 128

    def body(d_ref, z_ref, mp_ref, mn_ref, dz_ref, dmp_ref, dmn_ref):
        d, z = d_ref[0], z_ref[0]
        mp, mn = mp_ref[...], mn_ref[...]
        zp, zn = _prev_next(z, t)
        _, dp_next = _prev_next(d * mp, t)
        dn_prev, _ = _prev_next(d * mn, t)
        dz_ref[0] = (d * (1.0 - mp - mn) + dp_next + dn_prev).astype(BF16)

        @pl.when(pl.program_id(1) == 0)
        def _():
            dmp_ref[...] = jnp.zeros_like(dmp_ref)
            dmn_ref[...] = jnp.zeros_like(dmn_ref)

        dmp_ref[...] += jnp.sum(d * (zp - z), axis=0, keepdims=True)
        dmn_ref[...] += jnp.sum(d * (zn - z), axis=0, keepdims=True)

    blk = BS((1, t, 128), lambda j, i: (i, 0, j))
    vec = BS((1, 128), lambda j, i: (0, j))
    return pl.pallas_call(
        body, grid=(nc, b), in_specs=[blk, blk, vec, vec], out_specs=[blk, vec, vec],
        out_shape=[SDS((b, t, c), BF16), SDS((1, c), F32), SDS((1, c), F32)],
        compiler_params=_cp("parallel", "arbitrary"), name="shift_bwd",
    )(dzs3, z3, mu_p, mu_n)


def _const(shape):
    nd = len(shape)
    return BS(shape, lambda i: (0,) * nd)


def _prep_fwd(zs, p):
    m = zs.shape[0]
    tr = 256
    params = [p["w0"], p["w2"], p["a0"], p["a2"], p["g2"], p["k_k"], p["k_a"], p["ones_bd"]]

    def body(zs_ref, w0, w2, a0, a2, g2, kk_, ka_, bd, wf, wb, kf, kb, kk, kaf, kab, g):
        outs = _prep_fn(zs_ref[...], w0[...], w2[...], a0[...], a2[...], g2[...], kk_[...], ka_[...], bd[...])
        for ref, val in zip((wf, wb, kf, kb, kk, kaf, kab, g), outs):
            ref[...] = val

    row = BS((tr, 512), lambda i: (i, 0))
    return pl.pallas_call(
        body, grid=(m // tr,), in_specs=[BS((tr, RWKV_COLS), lambda i: (i, 0))] + [_const(q.shape) for q in params],
        out_specs=[row] * 8, out_shape=[SDS((m, 512), F32)] * 8, compiler_params=_cp("parallel"), name="rwkv_prep_fwd",
    )(zs, *params)


def _prep_bwd(zs, p, ct_rows):
    m = zs.shape[0]
    tr = 128
    params = [p["w0"], p["w2"], p["a0"], p["a2"], p["g2"], p["k_k"], p["k_a"]]
    names = ["dwf", "dwb", "dkf", "dkb", "dkk_f", "dkk_b", "dkaf", "dkab", "dr_f", "dr_b", "dr_p", "dk_p", "dv_p", "dg",
             "dv_f", "dv_b"]
    rows = [ct_rows[n] for n in names]

    def body(zs_ref, w0, w2, a0, a2, g2, kk_, ka_, bd, *rest):
        c = {n: r[...] for n, r in zip(names, rest[:len(names)])}
        outs = rest[len(names):]
        dzs_ref, grads = outs[0], outs[1:]
        ones_bd = bd[...]
        _, vjp = jax.vjp(
            lambda *q: _prep_fn(*q, ones_bd), zs_ref[...], w0[...], w2[...], a0[...], a2[...], g2[...], kk_[...], ka_[...]
        )
        cts = (c["dwf"], c["dwb"], c["dkf"] + c["dk_p"], c["dkb"] + c["dk_p"], c["dkk_f"] + c["dkk_b"], c["dkaf"], c["dkab"], c["dg"])
        dzs, *dparams = vjp(cts)
        dr = c["dr_f"] + c["dr_b"] + c["dr_p"]
        dv = c["dv_f"] + c["dv_b"] + c["dv_p"]
        dzs_ref[:, 0:512] = dzs[:, 0:512] + dr
        dzs_ref[:, 512:1024] = dzs[:, 512:1024]
        dzs_ref[:, 1024:1536] = dzs[:, 1024:1536] + dv
        dzs_ref[:, 1536:1920] = dzs[:, 1536:1920]

        @pl.when(pl.program_id(0) == 0)
        def _():
            for gr in grads:
                gr[...] = jnp.zeros_like(gr)

        for gr, val in zip(grads, dparams):
            gr[...] += val

    row = BS((tr, 512), lambda i: (i, 0))
    return pl.pallas_call(
        body, grid=(m // tr,),
        in_specs=[BS((tr, RWKV_COLS), lambda i: (i, 0))] + [_const(q.shape) for q in params] + [_const(p["ones_bd"].shape)]
        + [row] * len(names),
        out_specs=[BS((tr, RWKV_COLS), lambda i: (i, 0))] + [_const(q.shape) for q in params],
        out_shape=[SDS((m, RWKV_COLS), F32)] + [SDS(q.shape, F32) for q in params],
        compiler_params=_cp("arbitrary"), name="rwkv_prep_bwd",
    )(zs, *params, p["ones_bd"], *rows)


def _post_specs(tr):
    r = BS((tr, 512), lambda i: (i, 0))
    v = BS((tr, 512), lambda i: (i, 2))
    row = BS((tr, 512), lambda i: (i, 0))
    return r, v, row


def _post_fwd(y_f, y_b, zs, kf, kb, g, p):
    m = zs.shape[0]
    tr = 256
    r, v, row = _post_specs(tr)
    vecs = [p["r_k"], p["ln_x_g"], p["ln_x_b"], p["ones_bd"]]

    def body(yf, yb, r_ref, v_ref, kf_ref, kb_ref, g_ref, rk, lg, lb, bd, o_ref):
        o_ref[...] = _post_fn(
            yf[...] + yb[...], r_ref[...], kf_ref[...], kb_ref[...], v_ref[...], g_ref[...], rk[...], lg[...], lb[...], bd[...]
        ).astype(BF16)

    return pl.pallas_call(
        body, grid=(m // tr,), in_specs=[row, row, r, v, row, row, row] + [_const(q.shape) for q in vecs],
        out_specs=row, out_shape=SDS((m, 512), BF16), compiler_params=_cp("parallel"), name="rwkv_post_fwd",
    )(y_f, y_b, zs, zs, kf, kb, g, *vecs)


def _post_bwd(y_f, y_b, zs, kf, kb, g, p, dymix):
    m = zs.shape[0]
    tr = 128
    r, v, row = _post_specs(tr)
    vecs = [p["r_k"], p["ln_x_g"], p["ln_x_b"]]

    def body(yf, yb, r_ref, v_ref, kf_ref, kb_ref, g_ref, rk, lg, lb, bd, dy_ref, dyo, dr, dk, dv, dg, drk, dlg, dlb):
        ones_bd = bd[...]
        _, vjp = jax.vjp(
            lambda *q: _post_fn(*q, ones_bd),
            yf[...] + yb[...], r_ref[...], kf_ref[...], kb_ref[...], v_ref[...], g_ref[...], rk[...], lg[...], lb[...],
        )
        c_y, c_r, c_kf, _, c_v, c_g, c_rk, c_lg, c_lb = vjp(dy_ref[...])
        dyo[...] = c_y
        dr[...] = c_r
        dk[...] = c_kf
        dv[...] = c_v
        dg[...] = c_g

        @pl.when(pl.program_id(0) == 0)
        def _():
            for ref in (drk, dlg, dlb):
                ref[...] = jnp.zeros_like(ref)

        drk[...] += c_rk
        dlg[...] += c_lg
        dlb[...] += c_lb

    vec = _const((1, 512))
    return pl.pallas_call(
        body, grid=(m // tr,),
        in_specs=[row, row, r, v, row, row, row] + [_const(q.shape) for q in vecs] + [_const(p["ones_bd"].shape), row],
        out_specs=[row, row, row, row, row, vec, vec, vec],
        out_shape=[SDS((m, 512), F32)] * 5 + [SDS((1, 512), F32)] * 3,
        compiler_params=_cp("arbitrary"), name="rwkv_post_bwd",
    )(y_f, y_b, zs, zs, kf, kb, g, *vecs, p["ones_bd"], dymix)


SCAN_MXU_GROUPS = 2


def _half_ones():
    ri = lax.broadcasted_iota(jnp.int32, (128, 128), 0)
    ci = lax.broadcasted_iota(jnp.int32, (128, 128), 1)
    return jnp.where((ri < 64) == (ci < 64), 1.0, 0.0).astype(BF16)


def _half_sums(xs, ones):
    out = []
    per = -(-len(xs) // SCAN_MXU_GROUPS)
    for g in range(0, len(xs), per):
        part = xs[g:g + per]
        res = jnp.dot(jnp.concatenate(part, axis=0).astype(BF16), ones, preferred_element_type=F32)
        out += [res[64 * i:64 * i + 64] for i in range(len(part))]
    return out


def _scan_specs(b, t):
    nc = t // SCAN_CHUNK
    up, down = (lambda c: c), (lambda c: nc - 1 - c)
    rows = [BS((b, SCAN_CHUNK, 512), lambda c, ci=ci: (0, ci(c), 0)) for ci in (up, down)]
    vrows = [BS((b, SCAN_CHUNK, 512), lambda c, ci=ci: (0, ci(c), 2)) for ci in (up, down)]
    hist = [BS((SCAN_CHUNK, b * 4, 64, 128), lambda c, ci=ci: (ci(c), 0, 0, 0)) for ci in (up, down)]
    return nc, rows, vrows, hist


class _Window:
    def __init__(self, g, ascending):
        self.bases = [pl.multiple_of(g * 8, 8) if asc else pl.multiple_of(SCAN_CHUNK - 8 - g * 8, 8) for asc in ascending]
        self.ascending = ascending
        self.blocks = {}
        self.row_id = lax.broadcasted_iota(jnp.int32, (8, 128), 0)

    def j(self, d, s):
        return s if self.ascending[d] else 7 - s

    def time(self, d, s):
        return self.bases[d] + self.j(d, s)

    def row(self, ref, d, bi, cols, s):
        key = (id(ref), d, bi, cols.start)
        if key not in self.blocks:
            self.blocks[key] = ref[bi, pl.ds(self.bases[d], 8), cols]
        jj = self.j(d, s)
        return self.blocks[key][jj:jj + 1, :]

    def put(self, buf, key, d, s, row):
        prev = buf.get(key)
        new = jnp.broadcast_to(row, (8, 128))
        buf[key] = new if prev is None else jnp.where(self.row_id == self.j(d, s), new, prev)

    def flush(self, buf, refs_of):
        for key, val in buf.items():
            ref, d, bi, cols = refs_of(key)
            ref[bi, pl.ds(self.bases[d], 8), cols] = val


def _pairs(b):
    return [(bi * 4 + p, bi, slice(128 * p, 128 * p + 128)) for bi in range(b) for p in range(4)]


def _colsum(x):
    return jnp.sum(x, axis=0, keepdims=True)


def _eye_mask():
    return (lax.broadcasted_iota(jnp.int32, (64, 128), 1) & 63) == lax.broadcasted_iota(jnp.int32, (64, 128), 0)


def _scan_fwd(zs, kk, ops_f, ops_b, shard):
    b, t = zs.shape[:2]
    nc, rows, vrows, hist = _scan_specs(b, t)
    npair = b * 4

    def body(*refs):
        ins, shard_ref, outs, s_ref = refs[:12], refs[12], refs[13:17], refs[17]
        gather = (shard_ref, *refs[18:21])
        dirs = [dict(zip(("r", "kk", "v", "w", "k", "ka", "y", "h"), (*ins[6 * d:6 * d + 6], *outs[2 * d:2 * d + 2])))
                for d in (0, 1)]

        @pl.when(pl.program_id(0) == 0)
        def _():
            s_ref[...] = jnp.zeros_like(s_ref)
            _gather_halves(*gather, "start")

        @pl.when(pl.program_id(0) == nc - 1)
        def _():
            _gather_halves(*gather, "finish")

        ones, eye = _half_ones(), _eye_mask()
        chains = [(d, pr, bi, cols) for d in (0, 1) for pr, bi, cols in _pairs(b)]

        def eight_steps(g, carry):
            win = _Window(g, (True, False))
            ybuf = {}
            for s in range(8):
                s_prev, xa = [], []
                for d, pr, bi, cols in chains:
                    q = dirs[d]
                    st = s_ref[d * npair + pr]
                    q["h"][win.time(d, s), pr] = st
                    s_prev.append(st)
                    xa += [st * win.row(q["kk"], d, bi, cols, s), jnp.where(eye, win.row(q["v"], d, bi, cols, s), 0.0)]
                ra = _half_sums(xa, ones)
                xb = []
                for i, (d, pr, bi, cols) in enumerate(chains):
                    q = dirs[d]
                    s_new = s_prev[i] * win.row(q["w"], d, bi, cols, s) - ra[2 * i] * win.row(q["ka"], d, bi, cols, s) \
                        + ra[2 * i + 1] * win.row(q["k"], d, bi, cols, s)
                    s_ref[d * npair + pr] = s_new
                    xb.append(s_new * win.row(q["r"], d, bi, cols, s))
                rb = _half_sums(xb, ones)
                for i, (d, pr, bi, cols) in enumerate(chains):
                    win.put(ybuf, i, d, s, _colsum(jnp.where(eye, rb[i], 0.0)))
            win.flush(ybuf, lambda i: (dirs[chains[i][0]]["y"], chains[i][0], chains[i][2], chains[i][3]))
            return carry

        lax.fori_loop(0, SCAN_CHUNK // 8, eight_steps, 0)

    row_shape, hist_shape = SDS((b, t, 512), F32), SDS((t, npair, 64, 128), F32)
    state = (2 * npair, 64, 128)
    return pl.pallas_call(
        body, grid=(nc,), in_specs=sum(([rows[d], rows[d], vrows[d]] + [rows[d]] * 3 for d in (0, 1)), []) + [ANY],
        out_specs=[rows[0], hist[0], rows[1], hist[1], BS(state, lambda c: (0, 0, 0)), ANY],
        out_shape=[row_shape, hist_shape, row_shape, hist_shape, SDS(state, F32), _gathered_shape(shard)],
        scratch_shapes=_gather_sems(), compiler_params=_cp("arbitrary"), name="wkv_scan",
    )(zs, kk, zs, *ops_f, zs, kk, zs, *ops_b, shard)


def _scan_bwd(zs, kk, dy, ops_f, hist_f, ops_b, hist_b, s_last, partials):
    b, t = zs.shape[:2]
    nc, rows, vrows, hist = _scan_specs(b, t)
    npair = b * 4
    names_in = ("r", "kk", "v", "dy", "w", "k", "ka", "h")
    names_out = ("dr", "dw", "dk", "dkk", "dka", "dv")

    def body(*refs):
        ins, last_ref, part_ref, outs, recv_ref = refs[:16], refs[16], refs[17], refs[18:30], refs[30]
        ds_ref, after_ref = refs[31], refs[32]
        scatter = (part_ref, recv_ref, refs[33], refs[34])
        dirs = [dict(zip(names_in + names_out, (*ins[8 * d:8 * d + 8], *outs[6 * d:6 * d + 6]))) for d in (0, 1)]

        @pl.when(pl.program_id(0) == 0)
        def _():
            ds_ref[...] = jnp.zeros_like(ds_ref)
            after_ref[...] = last_ref[...]
            _scatter_partials(*scatter, "start")

        @pl.when(pl.program_id(0) == nc - 1)
        def _():
            _scatter_partials(*scatter, "finish")

        ones, eye = _half_ones(), _eye_mask()
        chains = [(d, pr, bi, cols) for d in (0, 1) for pr, bi, cols in _pairs(b)]

        def eight_steps(g, carry):
            win = _Window(g, (False, True))
            obuf = {}
            s_after = [after_ref[d * npair + pr] for d, pr, _, _ in chains]
            for s in range(8):
                row = lambda name, d, bi, cols: win.row(dirs[d][name], d, bi, cols, s)
                s_prev, xa = [], []
                for d, pr, bi, cols in chains:
                    st = dirs[d]["h"][win.time(d, s), pr]
                    s_prev.append(st)
                    xa += [st * row("kk", d, bi, cols), jnp.where(eye, row("v", d, bi, cols), 0.0),
                           jnp.where(eye, row("dy", d, bi, cols), 0.0)]
                ra = _half_sums(xa, ones)
                ds_now, xb = [], []
                for i, (d, pr, bi, cols) in enumerate(chains):
                    skk, vcol, dycol = ra[3 * i], ra[3 * i + 1], ra[3 * i + 2]
                    ds = ds_ref[d * npair + pr] + dycol * row("r", d, bi, cols)
                    win.put(obuf, (i, "dr"), d, s, _colsum(s_after[i] * dycol))
                    win.put(obuf, (i, "dk"), d, s, _colsum(ds * vcol))
                    win.put(obuf, (i, "dka"), d, s, -_colsum(ds * skk))
                    win.put(obuf, (i, "dw"), d, s, _colsum(ds * s_prev[i]))
                    ds_now.append(ds)
                    xb += [ds * row("k", d, bi, cols), ds * row("ka", d, bi, cols)]
                rb = _half_sums(xb, ones)
                for i, (d, pr, bi, cols) in enumerate(chains):
                    dskk_neg = rb[2 * i + 1]
                    win.put(obuf, (i, "dv"), d, s, _colsum(jnp.where(eye, rb[2 * i], 0.0)))
                    win.put(obuf, (i, "dkk"), d, s, -_colsum(s_prev[i] * dskk_neg))
                    ds_ref[d * npair + pr] = ds_now[i] * row("w", d, bi, cols) - dskk_neg * row("kk", d, bi, cols)
                s_after = s_prev
            for i, (d, pr, _, _) in enumerate(chains):
                after_ref[d * npair + pr] = s_after[i]
            win.flush(obuf, lambda key: (dirs[chains[key[0]][0]][key[1]], chains[key[0]][0], chains[key[0]][2], chains[key[0]][3]))
            return carry

        lax.fori_loop(0, SCAN_CHUNK // 8, eight_steps, 0)

    row_shape = SDS((b, t, 512), F32)
    state = (2 * npair, 64, 128)
    return pl.pallas_call(
        body, grid=(nc,),
        in_specs=sum(([rows[d], rows[d], vrows[d]] + [rows[d]] * 4 + [hist[d]] for d in (1, 0)), [])
        + [BS(state, lambda c: (0, 0, 0)), ANY],
        out_specs=[rows[1]] * 6 + [rows[0]] * 6 + [ANY],
        out_shape=[row_shape] * 12 + [SDS((3,) + partials.shape[1:], partials.dtype)],
        scratch_shapes=[pltpu.VMEM(state, F32), pltpu.VMEM(state, F32)] + _scatter_sems(),
        compiler_params=_cp("arbitrary"), name="wkv_scan_bwd",
    )(zs, kk, zs, dy, *ops_f, hist_f, zs, kk, zs, dy, *ops_b, hist_b, s_last, partials)


def _mla_fwd(zm, cs, sn, p, t):
    m = zm.shape[0]
    tr = 256
    per = t // tr
    params = [p["q_norm_g"], p["kv_norm_g"], p["wq"], p["wqr"], p["wk"], p["wv"]]

    def body(z_ref, cs_ref, sn_ref, gq, gkv, wq, wqr, wk, wv, q_ref, k_ref, v_ref):
        q, kf, v = _mla_fn(z_ref[...], cs_ref[...], sn_ref[...], gq[...], gkv[...], wq[...], wqr[...], wk[...], wv[...])
        q_ref[...] = q.astype(BF16)
        k_ref[...] = kf.astype(BF16)
        v_ref[...] = v.astype(BF16)

    tab = BS((tr, 128), lambda i: (i % per, 0))
    return pl.pallas_call(
        body, grid=(m // tr,), in_specs=[BS((tr, MLA_PAD_COLS), lambda i: (i, 0)), tab, tab] + [_const(q.shape) for q in params],
        out_specs=[BS((tr, 1024), lambda i: (i, 0)), BS((tr, 1024), lambda i: (i, 0)), BS((tr, 512), lambda i: (i, 0))],
        out_shape=[SDS((m, 1024), BF16), SDS((m, 1024), BF16), SDS((m, 512), BF16)], compiler_params=_cp("parallel"), name="mla_prep_fwd",
    )(zm, cs, sn, *params)


def _mla_bwd(zm, cs, sn, p, t, dq, dk, dv):
    m = zm.shape[0]
    tr = 128
    per = t // tr
    params = [p["q_norm_g"], p["kv_norm_g"], p["wq"], p["wqr"], p["wk"], p["wv"]]

    def body(z_ref, cs_ref, sn_ref, gq, gkv, wq, wqr, wk, wv, dq_ref, dk_ref, dv_ref, dz_ref, *grads):
        cs_v, sn_v = cs_ref[...], sn_ref[...]
        _, vjp = jax.vjp(
            lambda *q: _mla_fn(q[0], cs_v, sn_v, *q[1:]), z_ref[...], gq[...], gkv[...], wq[...], wqr[...], wk[...], wv[...]
        )
        dz, *dparams = vjp((dq_ref[...], dk_ref[...], dv_ref[...]))
        dz_ref[...] = dz.astype(BF16)

        @pl.when(pl.program_id(0) == 0)
        def _():
            for gr in grads:
                gr[...] = jnp.zeros_like(gr)

        for gr, val in zip(grads, dparams):
            gr[...] += val

    tab = BS((tr, 128), lambda i: (i % per, 0))
    wide = BS((tr, 1024), lambda i: (i, 0))
    return pl.pallas_call(
        body, grid=(m // tr,),
        in_specs=[BS((tr, MLA_PAD_COLS), lambda i: (i, 0)), tab, tab] + [_const(q.shape) for q in params]
        + [wide, wide, BS((tr, 512), lambda i: (i, 0))],
        out_specs=[BS((tr, MLA_PAD_COLS), lambda i: (i, 0))] + [_const(q.shape) for q in params],
        out_shape=[SDS((m, MLA_PAD_COLS), BF16)] + [SDS(q.shape, F32) for q in params],
        compiler_params=_cp("arbitrary"), name="mla_prep_bwd",
    )(zm, cs, sn, *params, dq, dk, dv)


_NT = (((1,), (1,)), ((), ()))
_TN = (((0,), (0,)), ((), ()))


def _attn_fwd(q, kf, v, b, t):
    m = q.shape[0]
    tq = 256
    nq = t // tq

    def body(q_ref, k_ref, v_ref, o_ref, l_ref):
        lo = lax.broadcasted_iota(jnp.int32, (1, 128), 1) < 64
        v_all = v_ref[...]
        o = jnp.zeros((tq, 128), F32)
        lse = []
        for h in range(2):
            hs = slice(128 * h, 128 * h + 128)
            s = lax.dot_general(q_ref[:, hs], k_ref[:, hs], _NT, preferred_element_type=F32)
            mx = jnp.max(s, axis=1, keepdims=True)
            e = jnp.exp(s - mx)
            den = jnp.sum(e, axis=1, keepdims=True)
            vh = jnp.where(lo if h == 0 else jnp.logical_not(lo), v_all, jnp.zeros_like(v_all))
            o = o + jnp.dot(e.astype(BF16), vh, preferred_element_type=F32) / den
            lse.append(mx + jnp.log(den))
        o_ref[...] = o
        l_ref[...] = jnp.where(lo, lse[0], lse[1])

    return pl.pallas_call(
        body, grid=(b, 4, nq),
        in_specs=[BS((tq, 256), lambda bi, hp, i: (bi * nq + i, hp)), BS((t, 256), lambda bi, hp, i: (bi, hp)),
                  BS((t, 128), lambda bi, hp, i: (bi, hp))],
        out_specs=[BS((tq, 128), lambda bi, hp, i: (bi * nq + i, hp))] * 2,
        out_shape=[SDS((m, 512), F32), SDS((m, 512), F32)], compiler_params=_cp("parallel", "parallel", "arbitrary"), name="attn_fwd",
    )(q, kf, v)


def _attn_bwd(q, kf, v, o, lse, do, b, t):
    m = q.shape[0]
    tq = 256
    nq = t // tq

    def body(q_ref, k_ref, v_ref, o_ref, l_ref, do_ref, dq_ref, dk_ref, dv_ref):
        lo = lax.broadcasted_iota(jnp.int32, (1, 128), 1) < 64

        @pl.when(pl.program_id(2) == 0)
        def _():
            dk_ref[...] = jnp.zeros_like(dk_ref)
            dv_ref[...] = jnp.zeros_like(dv_ref)

        v_all, o_all, l_all, do_all = v_ref[...], o_ref[...], l_ref[...], do_ref[...]
        dv_acc = jnp.zeros((t, 128), F32)
        for h in range(2):
            hs = slice(128 * h, 128 * h + 128)
            mask = lo if h == 0 else jnp.logical_not(lo)
            qh, kh = q_ref[:, hs], k_ref[:, hs]
            s = lax.dot_general(qh, kh, _NT, preferred_element_type=F32)
            lse_h = jnp.max(jnp.where(mask, l_all, -jnp.inf), axis=1, keepdims=True)
            pr = jnp.exp(s - lse_h)
            do_h = jnp.where(mask, do_all, 0.0)
            dp = lax.dot_general(do_h.astype(BF16), v_all, _NT, preferred_element_type=F32)
            dsum = jnp.sum(do_h * o_all, axis=1, keepdims=True)
            ds = (pr * (dp - dsum)).astype(BF16)
            dq_ref[:, hs] = jnp.dot(ds, kh, preferred_element_type=F32)
            dk_ref[:, hs] += lax.dot_general(ds, qh, _TN, preferred_element_type=F32)
            dv_acc = dv_acc + lax.dot_general(pr.astype(BF16), do_h.astype(BF16), _TN, preferred_element_type=F32)
        dv_ref[...] += dv_acc

    qspec = BS((tq, 256), lambda bi, hp, i: (bi * nq + i, hp))
    kspec = BS((t, 256), lambda bi, hp, i: (bi, hp))
    vspec = BS((t, 128), lambda bi, hp, i: (bi, hp))
    ospec = BS((tq, 128), lambda bi, hp, i: (bi * nq + i, hp))
    return pl.pallas_call(
        body, grid=(b, 4, nq), in_specs=[qspec, kspec, vspec, ospec, ospec, ospec], out_specs=[qspec, kspec, vspec],
        out_shape=[SDS((m, 1024), F32), SDS((m, 1024), F32), SDS((m, 512), F32)],
        compiler_params=_cp("parallel", "parallel", "arbitrary"), name="attn_bwd",
    )(q, kf, v, o, lse, do)


def _conv3(u, w_ref, b_ref, t):
    up, un = _prev_next(u, t)
    return w_ref[0:1, :] * up + w_ref[1:2, :] * u + w_ref[2:3, :] * un + b_ref[...], up, un


def _ffn_mid_fwd(ug3, uv3, cw, cb):
    b, t, f = ug3.shape
    nc = f // 256

    def body(ug_ref, uv_ref, wg_ref, wv_ref, bg_ref, bv_ref, a_ref):
        gc, _, _ = _conv3(ug_ref[0], wg_ref, bg_ref, t)
        vc, _, _ = _conv3(uv_ref[0], wv_ref, bv_ref, t)
        a_ref[0] = (gc * jax.nn.sigmoid(gc) * vc).astype(BF16)

    blk = BS((1, t, 256), lambda i, j: (i, 0, j))
    return pl.pallas_call(
        body, grid=(b, nc),
        in_specs=[blk, blk, BS((3, 256), lambda i, j: (0, j)), BS((3, 256), lambda i, j: (0, j + nc)),
                  BS((1, 256), lambda i, j: (0, j)), BS((1, 256), lambda i, j: (0, j + nc))],
        out_specs=blk, out_shape=SDS((b, t, f), BF16), compiler_params=_cp("parallel", "parallel"), name="ffn_mid_fwd",
    )(ug3, uv3, cw, cw, cb, cb)


def _ffn_mid_bwd(ug3, uv3, cw, cb, da3):
    b, t, f = ug3.shape
    nc = f // 256

    def half(u, up, un, dc, w_ref):
        dprev, dnext = _prev_next(dc, t)
        du = w_ref[1:2, :] * dc + w_ref[0:1, :] * dnext + w_ref[2:3, :] * dprev
        sums = [jnp.sum(dc * q, axis=0, keepdims=True) for q in (up, u, un)] + [jnp.sum(dc, axis=0, keepdims=True)]
        row = lax.broadcasted_iota(jnp.int32, (8, 256), 0)
        tab = jnp.zeros((8, 256), F32)
        for i, s in enumerate(sums):
            tab = jnp.where(row == i, s, tab)
        return du, tab

    def body(ug_ref, uv_ref, wg_ref, wv_ref, bg_ref, bv_ref, da_ref, dug_ref, duv_ref, tg_ref, tv_ref):
        ug, uv, da = ug_ref[0], uv_ref[0], da_ref[0]
        gc, gp, gn = _conv3(ug, wg_ref, bg_ref, t)
        vc, vp, vn = _conv3(uv, wv_ref, bv_ref, t)
        sg = jax.nn.sigmoid(gc)
        d_gc = da * vc * (sg * (1.0 + gc * (1.0 - sg)))
        d_vc = da * (gc * sg)
        dug, tg = half(ug, gp, gn, d_gc, wg_ref)
        duv, tv = half(uv, vp, vn, d_vc, wv_ref)
        dug_ref[0] = dug.astype(BF16)
        duv_ref[0] = duv.astype(BF16)

        @pl.when(pl.program_id(1) == 0)
        def _():
            tg_ref[...] = jnp.zeros_like(tg_ref)
            tv_ref[...] = jnp.zeros_like(tv_ref)

        tg_ref[...] += tg
        tv_ref[...] += tv

    blk = BS((1, t, 256), lambda j, i: (i, 0, j))
    tab = BS((8, 256), lambda j, i: (0, j))
    return pl.pallas_call(
        body, grid=(nc, b),
        in_specs=[blk, blk, BS((3, 256), lambda j, i: (0, j)), BS((3, 256), lambda j, i: (0, j + nc)),
                  BS((1, 256), lambda j, i: (0, j)), BS((1, 256), lambda j, i: (0, j + nc)), blk],
        out_specs=[blk, blk, tab, tab],
        out_shape=[SDS((b, t, f), BF16), SDS((b, t, f), BF16), SDS((8, f), F32), SDS((8, f), F32)],
        compiler_params=_cp("parallel", "arbitrary"), name="ffn_mid_bwd",
    )(ug3, uv3, cw, cw, cb, cb, da3)


def _add_rows(parts, name, out_dtype=F32):
    r = parts[0].shape[0]
    tr = _row_tile(r, 1024)
    n = len(parts)

    def body(*refs):
        acc = refs[0][...].astype(F32)
        for q in refs[1:n]:
            acc = acc + q[...].astype(F32)
        refs[n][...] = acc.astype(out_dtype)

    row = BS((tr, 128), lambda i: (i, 0))
    return pl.pallas_call(
        body, grid=(r // tr,), in_specs=[row] * n, out_specs=row, out_shape=SDS((r, 128), out_dtype),
        compiler_params=_cp("parallel"), name=name,
    )(*parts)


def _adamw(w, g, m, v, name):
    lead = w.shape[:-2]
    r, c = w.shape[-2:]
    tr = _row_tile(r)

    def body(w_ref, g_ref, m_ref, v_ref, d_ref, m2_ref, v2_ref):
        d, m2, v2 = _adamw_math(w_ref[...], g_ref[...], m_ref[...], v_ref[...])
        d_ref[...] = d
        m2_ref[...] = m2
        v2_ref[...] = v2

    blk = BS((1,) * len(lead) + (tr, c), lambda i: (0,) * len(lead) + (i, 0))
    return pl.pallas_call(
        body, grid=(r // tr,), in_specs=[blk] * 4, out_specs=[blk] * 3, out_shape=[SDS(w.shape, F32)] * 3,
        compiler_params=_cp("parallel"), name=name,
    )(w, g, m, v)


def _place():
    return lax.axis_index("x"), lax.axis_index("y"), lax.axis_index("c")


def _flip(v, bit):
    return 1 - v if bit else v


def _allgather_weights(shard):
    def body(x_ref, out_ref, send_sems, recv_sems):
        _gather_halves(x_ref, out_ref, send_sems, recv_sems, "start")
        _gather_halves(x_ref, out_ref, send_sems, recv_sems, "finish")

    return pl.pallas_call(
        body, out_shape=_gathered_shape(shard), in_specs=[ANY], out_specs=ANY, scratch_shapes=_gather_sems(),
        name="allgather_weights",
    )(shard)


def _gathered_shape(shard):
    return SDS((8 * (shard.shape[0] // 2), 128), shard.dtype)


def _gather_sems():
    return [pltpu.SemaphoreType.DMA((6,)), pltpu.SemaphoreType.DMA((6,))]


def _gather_halves(x_ref, out_ref, send_sems, recv_sems, phase):
    rh = x_ref.shape[0] // 2
    x, y, c = _place()
    me, sibling = (x, y, c), (x, y, 1 - c)
    chips = [(1 - x, y), (x, 1 - y), (1 - x, 1 - y)]
    mine_src = x_ref.at[pl.ds(c * rh, rh), :]

    def rows(px, py, pc):
        return out_ref.at[pl.ds((4 * px + 2 * py + pc) * rh, rh), :]

    def copy(k, block, to, src=None):
        return pltpu.make_async_remote_copy(
            src_ref=rows(*block) if src is None else src, dst_ref=rows(*block), send_sem=send_sems.at[k],
            recv_sem=recv_sems.at[k], device_id=to, device_id_type=MESH,
        )

    first = [copy(j, me, (*chip, c), src=mine_src) for j, chip in enumerate(chips)]
    if phase == "start":
        for cp in first:
            cp.start()
        return
    passed = [copy(3 + j, (*chip, c), sibling) for j, chip in enumerate(chips)]
    for j, chip in enumerate(chips):
        copy(j, (*chip, c), me).wait_recv()
        passed[j].start()
    for j, chip in enumerate(chips):
        copy(3 + j, (*chip, 1 - c), me).wait_recv()
    for cp in first + passed:
        cp.wait_send()


def _scatter_partials(g_ref, recv_ref, send_sems, recv_sems, phase):
    x, y, c = _place()
    copies = []
    for j, (fx, fy) in enumerate(((1, 0), (0, 1), (1, 1))):
        px, py = _flip(x, fx), _flip(y, fy)
        copies.append(pltpu.make_async_remote_copy(
            src_ref=g_ref.at[2 * px + py], dst_ref=recv_ref.at[j], send_sem=send_sems.at[j], recv_sem=recv_sems.at[j],
            device_id=(px, py, c), device_id_type=MESH,
        ))
    if phase == "start":
        for cp in copies:
            cp.start()
        return
    for cp in copies:
        cp.wait_recv()
    for cp in copies:
        cp.wait_send()


def _scatter_sems():
    return [pltpu.SemaphoreType.DMA((3,)), pltpu.SemaphoreType.DMA((3,))]


def _scatter_to_chips(g):
    def body(g_ref, recv_ref, send_sems, recv_sems):
        _scatter_partials(g_ref, recv_ref, send_sems, recv_sems, "start")
        _scatter_partials(g_ref, recv_ref, send_sems, recv_sems, "finish")

    return pl.pallas_call(
        body, out_shape=SDS((3,) + g.shape[1:], g.dtype), in_specs=[ANY], out_specs=ANY,
        scratch_shapes=[pltpu.SemaphoreType.DMA((3,)), pltpu.SemaphoreType.DMA((3,))], name="scatter_grads",
    )(g)


def _send_to_sibling(a, half_of_rows):
    rh = a.shape[1] // 2

    def body(a_ref, b_ref, send_sem, recv_sem):
        x, y, c = _place()
        src = a_ref.at[:, pl.ds((1 - c) * rh, rh), :] if half_of_rows else a_ref
        cp = pltpu.make_async_remote_copy(
            src_ref=src, dst_ref=b_ref, send_sem=send_sem, recv_sem=recv_sem, device_id=(x, y, 1 - c), device_id_type=MESH
        )
        cp.start()
        cp.wait()

    shape = (a.shape[0], rh, 128) if half_of_rows else a.shape
    return pl.pallas_call(
        body, out_shape=SDS(shape, a.dtype), in_specs=[ANY], out_specs=ANY,
        scratch_shapes=[pltpu.SemaphoreType.DMA, pltpu.SemaphoreType.DMA],
        name="sibling_halves" if half_of_rows else "sibling_swap",
    )(a)


def _allreduce_small(v):
    r = v.shape[0]

    def body(v_ref, out_ref, buf_ref, send_sems, recv_sems):
        x, y, c = _place()
        buf_ref[0] = v_ref[...]
        copies = []
        for k in range(1, 8):
            peer = (_flip(x, k >> 2 & 1), _flip(y, k >> 1 & 1), _flip(c, k & 1))
            cp = pltpu.make_async_remote_copy(
                src_ref=v_ref, dst_ref=buf_ref.at[k], send_sem=send_sems.at[k - 1], recv_sem=recv_sems.at[k - 1],
                device_id=peer, device_id_type=MESH,
            )
            cp.start()
            copies.append(cp)
        for cp in copies:
            cp.wait_recv()
        acc = None
        for d in range(8):
            slot = 4 * _flip(x, d >> 2 & 1) + 2 * _flip(y, d >> 1 & 1) + _flip(c, d & 1)
            term = buf_ref[slot]
            acc = term if acc is None else acc + term
        out_ref[...] = acc
        for cp in copies:
            cp.wait_send()

    return pl.pallas_call(
        body, out_shape=SDS(v.shape, F32), in_specs=[VMEM], out_specs=VMEM,
        scratch_shapes=[pltpu.VMEM((8, r, 128), F32), pltpu.SemaphoreType.DMA((7,)), pltpu.SemaphoreType.DMA((7,))],
        name="allreduce_small",
    )(v)


_BIG_A = (
    ("w_in", 0, False),
    ("decay_w2_fwd", 1, False), ("decay_w2_bwd", 1, False), ("iclr_a2_fwd", 1, False),
    ("iclr_a2_bwd", 1, False), ("gate_g2", 1, False),
)
_BIG_B = (
    ("w_uq", 0, False), ("w_ukv", 1, False), ("w_out", 0, False), ("w_ffn_up", 1, False), ("ffn_conv_w", 1, True),
    ("w_ffn_down", 0, False),
)
_BIG = _BIG_A + _BIG_B
_SMALL = (
    "ln_mix_g", "shift_mu_prev", "shift_mu_next", "decay_w0_fwd", "decay_w0_bwd", "iclr_a0_fwd", "iclr_a0_bwd", "k_k",
    "k_a", "r_k", "ln_x_g", "ln_x_b", "q_norm_g", "kv_norm_g", "mla_out_g", "ln_ffn_g", "ffn_conv_b", "ln_final_g",
)
_WEIGHTS = (
    "ln_mix_g", "w_in", "shift_mu_prev", "shift_mu_next", "decay_w0_fwd", "decay_w2_fwd", "decay_w0_bwd", "decay_w2_bwd",
    "iclr_a0_fwd", "iclr_a2_fwd", "iclr_a0_bwd", "iclr_a2_bwd", "gate_g2", "k_k", "k_a", "r_k", "ln_x_g", "ln_x_b",
    "q_norm_g", "w_uq", "kv_norm_g", "w_ukv", "mla_out_g", "w_out", "ln_ffn_g", "w_ffn_up", "ffn_conv_w", "ffn_conv_b",
    "w_ffn_down", "ln_final_g",
)


def _pad_rows(flat, rows):
    return jnp.pad(flat, (0, rows * 128 - flat.shape[0])).reshape(rows, 128)


def _rows_for(n, mult):
    rows = -(-n // 128)
    return -(-rows // mult) * mult


def _pack_shards_bf16(arrs, entries):
    parts = []
    for name, _, raw in entries:
        w = arrs[name][0]
        flat = lax.bitcast_convert_type(w, BF16).reshape(-1) if raw else w.astype(BF16).reshape(-1)
        parts.append(_pad_rows(flat, _rows_for(flat.shape[0], 32)))
    return jnp.concatenate(parts, axis=0)


def _unpack_gathered(g4, arrs, entries):
    out, off = {}, 0
    for name, axis, raw in entries:
        a, b = arrs[name].shape[1:]
        n = a * b * (2 if raw else 1)
        rows = _rows_for(n, 32)
        seg = g4[:, off:off + rows].reshape(4, rows * 128)[:, :n]
        off += rows
        if raw:
            seg = lax.bitcast_convert_type(seg.reshape(4, a * b, 2), F32)
        seg = seg.reshape(4, a, b)
        out[name] = jnp.concatenate([seg[s] for s in range(4)], axis=1) if axis == 1 else seg.reshape(4 * a, b)
    return out


def _pack_grads(full, arrs, entries):
    parts = []
    for name, axis, _ in entries:
        a, b = arrs[name].shape[1:]
        g = full[name]
        sh = g.reshape(a, 4, b).transpose(1, 0, 2) if axis == 1 else g.reshape(4, a, b)
        rows = _rows_for(a * b, 8)
        parts.append(jnp.pad(sh.reshape(4, a * b), ((0, 0), (0, rows * 128 - a * b))).reshape(4, rows, 128))
    total = sum(q.shape[1] for q in parts)
    parts.append(jnp.zeros((4, -(-total // 1024) * 1024 - total, 128), F32))
    return jnp.concatenate(parts, axis=1)


def _unpack_grads(g, arrs, entries):
    out, off = {}, 0
    for name, _, _ in entries:
        a, b = arrs[name].shape[1:]
        rows = _rows_for(a * b, 8)
        out[name] = g[off:off + rows].reshape(-1)[:a * b].reshape(1, a, b)
        off += rows
    return out


def _pack_small(vals):
    flat = jnp.concatenate([vals[n].reshape(-1).astype(F32) for n in _SMALL] + [vals["_loss"].reshape(-1)])
    return _pad_rows(flat, _rows_for(flat.shape[0], 8))


def _unpack_small(buf, arrs):
    flat, out, off = buf.reshape(-1), {}, 0
    for n in _SMALL:
        size = arrs[n].size
        out[n] = flat[off:off + size].reshape(arrs[n].shape)
        off += size
    out["_loss"] = flat[off]
    return out


def _rot_cols(w):
    return jnp.concatenate([-w[..., 16:], w[..., :16]], axis=-1)


def _rot_cols_t(g):
    return jnp.concatenate([g[..., 16:], -g[..., :16]], axis=-1)


def _rope_tables(t):
    inv = jnp.power(ROPE_THETA, -jnp.arange(0, ROPE_DIM, 2, dtype=F32) / ROPE_DIM)
    ang = jnp.arange(t, dtype=F32)[:, None] * inv[None, :]
    one, zero = jnp.ones((t, 64), F32), jnp.zeros((t, 64), F32)
    cs = jnp.concatenate([one, jnp.cos(ang), jnp.cos(ang), zero[:, :32]], axis=1)
    sn = jnp.concatenate([zero, jnp.sin(ang), jnp.sin(ang), zero[:, :32]], axis=1)
    return cs, sn


def _block_diag(a, b):
    za = jnp.zeros_like(a)
    return jnp.concatenate([jnp.concatenate([a, za], axis=1), jnp.concatenate([za, b], axis=1)], axis=0)


def kernel(x, ln_mix_g, w_in, shift_mu_prev, shift_mu_next, decay_w0_fwd, decay_w2_fwd, decay_w0_bwd, decay_w2_bwd, iclr_a0_fwd, iclr_a2_fwd, iclr_a0_bwd, iclr_a2_bwd, gate_g2, k_k, k_a, r_k, ln_x_g, ln_x_b, q_norm_g, w_uq, kv_norm_g, w_ukv, mla_out_g, w_out, ln_ffn_g, w_ffn_up, ffn_conv_w, ffn_conv_b, w_ffn_down, ln_final_g, loss_target, m_ln_mix_g, m_w_in, m_shift_mu_prev, m_shift_mu_next, m_decay_w0_fwd, m_decay_w2_fwd, m_decay_w0_bwd, m_decay_w2_bwd, m_iclr_a0_fwd, m_iclr_a2_fwd, m_iclr_a0_bwd, m_iclr_a2_bwd, m_gate_g2, m_k_k, m_k_a, m_r_k, m_ln_x_g, m_ln_x_b, m_q_norm_g, m_w_uq, m_kv_norm_g, m_w_ukv, m_mla_out_g, m_w_out, m_ln_ffn_g, m_w_ffn_up, m_ffn_conv_w, m_ffn_conv_b, m_w_ffn_down, m_ln_final_g, v_ln_mix_g, v_w_in, v_shift_mu_prev, v_shift_mu_next, v_decay_w0_fwd, v_decay_w2_fwd, v_decay_w0_bwd, v_decay_w2_bwd, v_iclr_a0_fwd, v_iclr_a2_fwd, v_iclr_a0_bwd, v_iclr_a2_bwd, v_gate_g2, v_k_k, v_k_a, v_r_k, v_ln_x_g, v_ln_x_b, v_q_norm_g, v_w_uq, v_kv_norm_g, v_w_ukv, v_mla_out_g, v_w_out, v_ln_ffn_g, v_w_ffn_up, v_ffn_conv_w, v_ffn_conv_b, v_w_ffn_down, v_ln_final_g):
    arrs = dict(locals())
    for pre in ("", "m_", "v_"):
        arrs[pre + "w_in"] = jnp.swapaxes(arrs[pre + "w_in"], 1, 2)
    b, t, d = x.shape
    m = b * t
    x2 = x.reshape(m, d)
    tgt = loss_target.reshape(m, d)
    vec = lambda n: arrs[n].reshape(1, -1)

    core = lax.axis_index("c")
    chip = 2 * lax.axis_index("x") + lax.axis_index("y")
    def unpack(gathered, shard, entries):
        g4 = lax.dynamic_update_index_in_dim(gathered.reshape(N_CHIPS, -1, 128), shard, chip, axis=0)
        return _unpack_gathered(g4, arrs, entries)

    shard_a, shard_b = _pack_shards_bf16(arrs, _BIG_A), _pack_shards_bf16(arrs, _BIG_B)
    fw = unpack(_allgather_weights(shard_a), shard_a, _BIG_A)
    win = fw["w_in"]
    zc = jnp.zeros((64, d), BF16)
    w_kr = win[2944:2976]
    rot_kr = jnp.swapaxes(_rot_cols(jnp.swapaxes(w_kr, 0, 1)), 0, 1)
    win_m = jnp.concatenate([win[1920:2944], zc, w_kr, zc[:32], zc, rot_kr, zc[:32]], axis=0)
    win_r = win[:RWKV_COLS]
    head = jnp.arange(512) // HEAD_DIM
    rw = dict(
        w0=jnp.concatenate([vec("decay_w0_fwd"), vec("decay_w0_bwd")], axis=1),
        w2=_block_diag(fw["decay_w2_fwd"], fw["decay_w2_bwd"]).astype(F32),
        a0=jnp.concatenate([vec("iclr_a0_fwd"), vec("iclr_a0_bwd")], axis=1),
        a2=_block_diag(fw["iclr_a2_fwd"], fw["iclr_a2_bwd"]).astype(F32),
        g2=fw["gate_g2"].astype(F32), k_k=vec("k_k"), k_a=vec("k_a"), r_k=vec("r_k"), ln_x_g=vec("ln_x_g"), ln_x_b=vec("ln_x_b"),
        ones_bd=(head[:, None] == head[None, :]).astype(F32),
    )
    cs, sn = _rope_tables(t)

    n1 = _rms_fwd(x2, vec("ln_mix_g"), "rms_mix")
    zm = _mm(n1, win_m, "nt", "proj_in_mla")
    zr = _mm(n1, win_r, "nt", "proj_in_rwkv")
    zs = _shift_fwd(zr.reshape(b, t, RWKV_COLS), vec("shift_mu_prev"), vec("shift_mu_next"))
    zs2 = zs.reshape(m, RWKV_COLS)
    wf, wb, kf, kb, kk, kaf, kab, gate = _prep_fwd(zs2, rw)
    r4 = lambda a: a.reshape(b, t, 512)
    f2 = lambda a: a.reshape(m, 512)
    kk4 = r4(kk)
    ops_f = (r4(wf), r4(kf), r4(kaf))
    ops_b = (r4(wb), r4(kb), r4(kab))
    y_f, hist_f, y_b, hist_b, s_last, gathered_b = _scan_fwd(zs, kk4, ops_f, ops_b, shard_b)
    fw.update(unpack(gathered_b, shard_b, _BIG_B))
    uq = fw["w_uq"].astype(F32).reshape(Q_RANK, HEADS, 96)
    z32 = jnp.zeros((Q_RANK, HEADS, 32), F32)
    wq = jnp.concatenate([uq[..., :64], uq[..., 64:], z32], axis=-1).reshape(Q_RANK, 1024)
    wqr = jnp.concatenate([z32, z32, _rot_cols(uq[..., 64:]), z32], axis=-1).reshape(Q_RANK, 1024)
    ukv = fw["w_ukv"].astype(F32).reshape(KV_RANK, HEADS, 128)
    wk = jnp.concatenate([ukv[..., :64], jnp.zeros_like(ukv[..., :64])], axis=-1).reshape(KV_RANK, 1024)
    wv = ukv[..., 64:].reshape(KV_RANK, 512)
    mp = dict(q_norm_g=vec("q_norm_g"), kv_norm_g=vec("kv_norm_g"), wq=wq, wqr=wqr, wk=wk, wv=wv)
    w_up_g, w_up_v = fw["w_ffn_up"][:, :D_FF], fw["w_ffn_up"][:, D_FF:]
    cw, cb = fw["ffn_conv_w"], vec("ffn_conv_b")
    y_f, y_b = f2(y_f), f2(y_b)
    y_rwkv = _post_fwd(y_f, y_b, zs2, kf, kb, gate, rw)
    q, kfull, v = _mla_fwd(zm, cs, sn, mp, t)
    o, lse = _attn_fwd(q, kfull, v, b, t)
    y_mla = _rms_fwd(o, vec("mla_out_g"), "rms_mla_out")
    ymix = jnp.concatenate([y_rwkv, y_mla], axis=1)
    h1 = _mm(ymix, fw["w_out"], "nn", "proj_out", add=x2)
    n2 = _rms_fwd(h1, vec("ln_ffn_g"), "rms_ffn")
    ug = _mm(n2, w_up_g, "nn", "ffn_up_gate")
    uv = _mm(n2, w_up_v, "nn", "ffn_up_val")
    r3f = lambda a: a.reshape(b, t, D_FF)
    act = _ffn_mid_fwd(r3f(ug), r3f(uv), cw, cb).reshape(m, D_FF)
    h2 = _mm(act, fw["w_ffn_down"], "nn", "ffn_down", add=h1)
    loss_tab, dh2, g_ln_final = _final(h2, vec("ln_final_g"), tgt)

    gfull = {}
    dact = _mm(dh2, fw["w_ffn_down"], "nt", "d_ffn_act")
    gfull["w_ffn_down"] = _mm(act, dh2, "tn", "g_ffn_down")
    dug, duv, tab_g, tab_v = _ffn_mid_bwd(r3f(ug), r3f(uv), cw, cb, r3f(dact))
    dug, duv = dug.reshape(m, D_FF), duv.reshape(m, D_FF)
    gfull["ffn_conv_w"] = jnp.concatenate([tab_g[0:3], tab_v[0:3]], axis=1)
    g_conv_b = jnp.concatenate([tab_g[3:4], tab_v[3:4]], axis=1)
    dn2 = _mm(duv, w_up_v, "nt", "d_ffn_in_val", add=_mm(dug, w_up_g, "nt", "d_ffn_in_gate"))
    gfull["w_ffn_up"] = jnp.concatenate([_mm(n2, dug, "tn", "g_ffn_up_gate"), _mm(n2, duv, "tn", "g_ffn_up_val")], axis=1)
    dh1, g_ln_ffn = _rms_bwd(h1, vec("ln_ffn_g"), dn2, "rms_ffn_bwd", dres=dh2)
    dymix = _mm(dh1, fw["w_out"], "nt", "d_mix")
    gfull["w_out"] = _mm(ymix, dh1, "tn", "g_w_out")
    do, g_mla_out = _rms_bwd(o, vec("mla_out_g"), dymix, "rms_mla_out_bwd", dy_block=1)
    dq, dk, dv = _attn_bwd(q, kfull, v, o, lse, do, b, t)
    dzm, g_qn, g_kvn, g_wq, g_wqr, g_wk, g_wv = _mla_bwd(zm, cs, sn, mp, t, dq, dk, dv)
    gq3, gqr3 = g_wq.reshape(Q_RANK, HEADS, 128), g_wqr.reshape(Q_RANK, HEADS, 128)
    gfull["w_uq"] = jnp.concatenate(
        [gq3[..., :64], gq3[..., 64:96] + _rot_cols_t(gqr3[..., 64:96])], axis=-1
    ).reshape(Q_RANK, HEADS * 96)
    gfull["w_ukv"] = jnp.concatenate(
        [g_wk.reshape(KV_RANK, HEADS, 128)[..., :64], g_wv.reshape(KV_RANK, HEADS, 64)], axis=-1
    ).reshape(KV_RANK, 1024)
    def cores_first(entries, tag):
        packed = _pack_grads(gfull, arrs, entries)
        rh = packed.shape[1] // 2
        own = lax.dynamic_slice_in_dim(packed, core * rh, rh, axis=1)
        sib = _send_to_sibling(packed, True)
        return _add_rows([own.reshape(4 * rh, 128), sib.reshape(4 * rh, 128)], "sum_cores_" + tag, BF16).reshape(4, rh, 128)

    def chips_then_join(chip_part, recv, entries, tag):
        mine = lax.dynamic_index_in_dim(chip_part, chip, axis=0, keepdims=False)
        half = _add_rows([mine, recv[0], recv[1], recv[2]], "sum_chips_" + tag)
        other = _send_to_sibling(half, False)
        lower = jnp.where(core == 0, half, other)
        upper = jnp.where(core == 0, other, half)
        return _unpack_grads(jnp.concatenate([lower, upper], axis=0), arrs, entries)

    part_b = cores_first(_BIG_B, "b")
    dys, dr_p, dk_p, dv_p, dgate, g_rk, g_lnx_g, g_lnx_b = _post_bwd(y_f, y_b, zs2, kf, kb, gate, rw, dymix)
    (dr_f, dwf, dkf, dkk_f, dkaf, dv_f, dr_b, dwb, dkb, dkk_b, dkab, dv_b, recv_b) = _scan_bwd(
        zs, kk4, r4(dys), ops_f, hist_f, ops_b, hist_b, s_last, part_b)
    g_big = chips_then_join(part_b, recv_b, _BIG_B, "b")
    cts = dict(dwf=f2(dwf), dwb=f2(dwb), dkf=f2(dkf), dkb=f2(dkb), dkk_f=f2(dkk_f), dkk_b=f2(dkk_b), dkaf=f2(dkaf), dkab=f2(dkab),
               dr_f=f2(dr_f), dr_b=f2(dr_b), dr_p=dr_p, dk_p=dk_p, dv_p=dv_p, dg=dgate, dv_f=f2(dv_f), dv_b=f2(dv_b))
    dzs, g_w0, g_w2, g_a0, g_a2, g_g2, g_kk, g_ka = _prep_bwd(zs2, rw, cts)
    dzr, g_mu_p, g_mu_n = _shift_bwd(dzs.reshape(b, t, RWKV_COLS), zr.reshape(b, t, RWKV_COLS), vec("shift_mu_prev"), vec("shift_mu_next"))
    dzr = dzr.reshape(m, RWKV_COLS)
    gfull["decay_w2_fwd"], gfull["decay_w2_bwd"] = g_w2[:64, :512], g_w2[64:, 512:]
    gfull["iclr_a2_fwd"], gfull["iclr_a2_bwd"] = g_a2[:64, :512], g_a2[64:, 512:]
    gfull["gate_g2"] = g_g2
    dn1 = _mm(dzr, win_r, "nn", "d_proj_in_rwkv", add=_mm(dzm, win_m, "nn", "d_proj_in_mla"))
    g_m = _mm(dzm, n1, "tn", "g_w_in_mla")
    g_r = _mm(dzr, n1, "tn", "g_w_in_rwkv")
    g_kr = g_m[1088:1120] + jnp.swapaxes(_rot_cols_t(jnp.swapaxes(g_m[1216:1248], 0, 1)), 0, 1)
    gfull["w_in"] = jnp.concatenate([g_r, g_m[:1024], g_kr], axis=0)
    dx, g_ln_mix = _rms_bwd(x2, vec("ln_mix_g"), dn1, "rms_mix_bwd", dres=dh1)

    part_a = cores_first(_BIG_A, "a")
    g_big.update(chips_then_join(part_a, _scatter_to_chips(part_a), _BIG_A, "a"))
    small = {
        "ln_mix_g": g_ln_mix, "shift_mu_prev": g_mu_p, "shift_mu_next": g_mu_n, "decay_w0_fwd": g_w0[:, :512],
        "decay_w0_bwd": g_w0[:, 512:], "iclr_a0_fwd": g_a0[:, :512], "iclr_a0_bwd": g_a0[:, 512:], "k_k": g_kk, "k_a": g_ka,
        "r_k": g_rk, "ln_x_g": g_lnx_g, "ln_x_b": g_lnx_b, "q_norm_g": g_qn, "kv_norm_g": g_kvn, "mla_out_g": g_mla_out,
        "ln_ffn_g": g_ln_ffn, "ffn_conv_b": g_conv_b, "ln_final_g": g_ln_final,
        "_loss": jnp.pad(loss_tab[0, 0:1], (0, 127)),
    }
    g_small_buf = _allreduce_small(_pack_small(small))
    g_small = _unpack_small(g_small_buf, arrs)

    grads, deltas, new_m, new_v = {}, {}, {}, {}
    for name, _, _ in _BIG:
        grads[name] = g_big[name]
        deltas[name], new_m[name], new_v[name] = _adamw(
            arrs[name], g_big[name], arrs["m_" + name], arrs["v_" + name], "adamw_" + name)
    pk = lambda pre: _pack_small({**{n: arrs[pre + n] for n in _SMALL}, "_loss": jnp.zeros((128,), F32)})
    sd, sm, sv = _adamw(pk(""), g_small_buf, pk("m_"), pk("v_"), "adamw_small")
    sd, sm, sv = _unpack_small(sd, arrs), _unpack_small(sm, arrs), _unpack_small(sv, arrs)
    for n in _SMALL:
        grads[n], deltas[n], new_m[n], new_v[n] = g_small[n], sd[n], sm[n], sv[n]
    for group in (grads, deltas, new_m, new_v):
        group["w_in"] = jnp.swapaxes(group["w_in"], 1, 2)

    return (g_small["_loss"], dx.reshape(b, t, d), *[grads[n] for n in _WEIGHTS], *[deltas[n] for n in _WEIGHTS],
            *[new_m[n] for n in _WEIGHTS], *[new_v[n] for n in _WEIGHTS])
```

```python
import functools
import math

import jax
import jax.numpy as jnp
from jax import lax
from jax.experimental import pallas as pl
from jax.experimental.pallas import tpu as pltpu

F32, BF16 = jnp.float32, jnp.bfloat16
MESH = pl.DeviceIdType.MESH
ANY = pl.BlockSpec(memory_space=pl.ANY)
VMEM = pl.BlockSpec(memory_space=pltpu.VMEM)
BS = pl.BlockSpec
SDS = jax.ShapeDtypeStruct

NORM_EPS = 1e-6
GN_EPS = 64e-5
L2_EPS = 1e-12
HEADS = 8
HEAD_DIM = 64
D_RWKV = HEADS * HEAD_DIM
ROPE_DIM = 32
ROPE_THETA = 10000.0
MLA_SCALE = (64 + ROPE_DIM) ** -0.5
Q_RANK, KV_RANK = 768, 256
RWKV_COLS = 1920
MLA_PAD_COLS = Q_RANK + KV_RANK + 256
D_FF = 2816
ADAM_LR, ADAM_B1, ADAM_B2, ADAM_EPS, ADAM_WD, ADAM_STEP = 0.001, 0.9, 0.999, 1e-08, 0.01, 10

V7X_LANES = 128
V7X_VMEM_LIMIT = 56 * 1024 * 1024
SCAN_CHUNK = 32
N_CHIPS = 4


def _cp(*sem):
    return pltpu.CompilerParams(dimension_semantics=sem, vmem_limit_bytes=V7X_VMEM_LIMIT)


def _tile(n, cands=(512, 640, 384, 256, 128)):
    for c in cands:
        if n % c == 0:
            return c
    return n


def _row_tile(n, cap=256):
    best = n
    for t in range(8, cap + 1, 8):
        if n % t == 0:
            best = t
    return best if best <= cap or n <= cap else n


def _rms(x, g):
    ms = jnp.mean(x * x, axis=-1, keepdims=True)
    return x * lax.rsqrt(ms + NORM_EPS) * g


@jax.custom_vjp
def _bdot(x, w):
    return jnp.dot(x.astype(BF16), w.astype(BF16), preferred_element_type=F32)


def _bdot_fwd(x, w):
    return _bdot(x, w), (x, w)


def _bdot_bwd(res, ct):
    x, w = res
    c = ct.astype(BF16)
    dx = lax.dot_general(c, w.astype(BF16), (((1,), (1,)), ((), ())), preferred_element_type=F32)
    dw = lax.dot_general(x.astype(BF16), c, (((0,), (0,)), ((), ())), preferred_element_type=F32)
    return dx.astype(x.dtype), dw.astype(w.dtype)


_bdot.defvjp(_bdot_fwd, _bdot_bwd)


@jax.custom_vjp
def _headsum(x, ones_bd):
    hi = x.astype(BF16)
    mid = (x - hi.astype(F32)).astype(BF16)
    ob = ones_bd.astype(BF16)
    return jnp.dot(hi, ob, preferred_element_type=F32) + jnp.dot(mid, ob, preferred_element_type=F32)


def _headsum_fwd(x, ones_bd):
    return _headsum(x, ones_bd), ones_bd


def _headsum_bwd(ones_bd, ct):
    return _headsum(ct, ones_bd), jnp.zeros_like(ones_bd)


_headsum.defvjp(_headsum_fwd, _headsum_bwd)


def _prep_fn(zs, w0, w2, a0, a2, g2, k_k, k_a, ones_bd):
    k = zs[:, 512:1024]
    wd = zs[:, 1536:1664]
    ad = zs[:, 1664:1792]
    gd = zs[:, 1792:1920]
    logit = w0 + _bdot(jnp.tanh(wd), w2)
    w = jnp.exp(-math.exp(-0.5) * jax.nn.sigmoid(logit))
    a = jax.nn.sigmoid(a0 + _bdot(ad, a2))
    g = _bdot(jax.nn.sigmoid(gd), g2)
    kkr = k * k_k
    nrm = jnp.sqrt(_headsum(kkr * kkr, ones_bd))
    kk = kkr / jnp.maximum(nrm, L2_EPS)
    a_f, a_b = a[:, :512], a[:, 512:]
    kf = k * (1.0 + (a_f - 1.0) * k_a)
    kb = k * (1.0 + (a_b - 1.0) * k_a)
    return w[:, :512], w[:, 512:], kf, kb, kk, kk * a_f, kk * a_b, g


def _post_fn(y, r, kf, kb, v, g, r_k, ln_g, ln_b, ones_bd):
    mu =_headsum(y, ones_bd) * (1.0 / HEAD_DIM)
    yc = y - mu
    var = _headsum(yc * yc, ones_bd) * (1.0 / HEAD_DIM)
    yn = yc * lax.rsqrt(var + GN_EPS) * ln_g + ln_b
    bonus = _headsum(r * (kf + kb) * r_k, ones_bd) * v
    return (yn + bonus) * g


def _cat8(x):
    return jnp.concatenate([x] * HEADS, axis=1)


def _mla_fn(zm, cs, sn, gq, gkv, wq, wqr, wk, wv):
    cq = zm[:, :Q_RANK]
    ckv = zm[:, Q_RANK:Q_RANK + KV_RANK]
    kr = zm[:, Q_RANK + KV_RANK:Q_RANK + KV_RANK + 128]
    krr = zm[:, Q_RANK + KV_RANK + 128:]
    cqn = _rms(cq, gq)
    ckvn = _rms(ckv, gkv)
    q = (_bdot(cqn, wq) * _cat8(cs) + _bdot(cqn, wqr) * _cat8(sn)) * MLA_SCALE
    kro = kr * cs + krr * sn
    kfull = _bdot(ckvn, wk) + _cat8(kro)
    v = _bdot(ckvn, wv)
    return q, kfull, v


def _adamw_math(w, g, m, v):
    m2 = ADAM_B1 * m + (1.0 - ADAM_B1) * g
    v2 = ADAM_B2 * v + (1.0 - ADAM_B2) * (g * g)
    m_hat = m2 / (1.0 - ADAM_B1 ** ADAM_STEP)
    v_hat = v2 / (1.0 - ADAM_B2 ** ADAM_STEP)
    delta = -ADAM_LR * (m_hat / (jnp.sqrt(v_hat) + ADAM_EPS) + ADAM_WD * w)
    return delta, m2, v2


_DIMS = {"nn": (((1,), (0,)), ((), ())), "nt": (((1,), (1,)), ((), ())), "tn": (((0,), (0,)), ((), ()))}


def _mm(a, b, mode, name, out_dtype=F32, add=None, column_blocks=None):
    if mode == "nn":
        (m, k), (_, n) = a.shape, b.shape
    elif mode == "nt":
        (m, k), (n, _) = a.shape, b.shape
    else:
        (k, m), (_, n) = a.shape, b.shape
    big = (1024, 1408, 768, 640, 512, 384, 256, 128)
    tm, tn, tk = _tile(m, big), column_blocks or _tile(n, big), _tile(k, (512, 1408, 640, 384, 256, 128))
    nk = k // tk

    def body(a_ref, b_ref, *rest):
        if add is None:
            o_ref, acc_ref = rest
        else:
            add_ref, o_ref, acc_ref = rest
        kk = pl.program_id(2)

        @pl.when(kk == 0)
        def _():
            acc_ref[...] = jnp.zeros_like(acc_ref)

        acc_ref[...] += lax.dot_general(
            a_ref[...].astype(BF16), b_ref[...].astype(BF16), _DIMS[mode], preferred_element_type=F32
        )

        @pl.when(kk == nk - 1)
        def _():
            r = acc_ref[...]
            if add is not None:
                r = r + add_ref[...]
            o_ref[...] = r.astype(out_dtype).reshape(o_ref.shape)

    a_spec = BS((tk, tm), lambda i, j, kk: (kk, i)) if mode == "tn" else BS((tm, tk), lambda i, j, kk: (i, kk))
    b_spec = BS((tn, tk), lambda i, j, kk: (j, kk)) if mode == "nt" else BS((tk, tn), lambda i, j, kk: (kk, j))
    o_spec = BS((tm, tn), lambda i, j, kk: (i, j))
    ins, specs = [a, b], [a_spec, b_spec]
    if add is not None:
        ins.append(add)
        specs.append(o_spec)
    out_shape = SDS((m, n), out_dtype)
    if column_blocks:
        assert add is None
        o_spec, out_shape = BS((1, tm, tn), lambda i, j, kk: (j, i, 0)), SDS((n // tn, m, tn), out_dtype)
    return pl.pallas_call(
        body, grid=(m // tm, n // tn, nk), in_specs=specs, out_specs=o_spec, out_shape=out_shape,
        scratch_shapes=[pltpu.VMEM((tm, tn), F32)], compiler_params=_cp("parallel", "parallel", "arbitrary"), name=name,
    )(*ins)


def _rms_fwd(x, g, name):
    m, d = x.shape
    tr = _tile(m)

    def body(x_ref, g_ref, o_ref):
        o_ref[...] = _rms(x_ref[...], g_ref[...]).astype(BF16)

    return pl.pallas_call(
        body, grid=(m // tr,), in_specs=[BS((tr, d), lambda i: (i, 0)), BS((1, d), lambda i: (0, 0))],
        out_specs=BS((tr, d), lambda i: (i, 0)), out_shape=SDS((m, d), BF16), compiler_params=_cp("parallel"), name=name,
    )(x, g)


def _rms_bwd(x, g, dy, name, dres=None, dy_block=0):
    m, d = x.shape
    tr = _row_tile(m)

    def body(x_ref, g_ref, dy_ref, *rest):
        if dres is None:
            dx_ref, dg_ref = rest
        else:
            dres_ref, dx_ref, dg_ref = rest
        _, vjp = jax.vjp(_rms, x_ref[...], g_ref[...])
        dx, dg = vjp(dy_ref[...])
        if dres is not None:
            dx = dx + dres_ref[...]
        dx_ref[...] = dx

        @pl.when(pl.program_id(0) == 0)
        def _():
            dg_ref[...] = jnp.zeros_like(dg_ref)

        dg_ref[...] += dg

    row = BS((tr, d), lambda i: (i, 0))
    vec = BS((1, d), lambda i: (0, 0))
    ins, specs = [x, g, dy], [row, vec, BS((tr, d), lambda i: (i, dy_block))]
    if dres is not None:
        ins.append(dres)
        specs.append(row)
    return pl.pallas_call(
        body, grid=(m // tr,), in_specs=specs, out_specs=[row, vec], out_shape=[SDS((m, d), F32), SDS((1, d), F32)],
        compiler_params=_cp("arbitrary"), name=name,
    )(*ins)


def _final(h, g, tgt):
    m, d = h.shape
    tr = _row_tile(m)

    def loss_fn(hh, gg, tt):
        e = _rms(hh, gg) - tt
        return 0.5 * jnp.sum(e * e) * (1.0 / d)

    def body(h_ref, g_ref, t_ref, l_ref, dh_ref, dg_ref):
        val, (dh, dg) = jax.value_and_grad(loss_fn, argnums=(0, 1))(h_ref[...], g_ref[...], t_ref[...])
        dh_ref[...] = dh

        @pl.when(pl.program_id(0) == 0)
        def _():
            dg_ref[...] = jnp.zeros_like(dg_ref)
            l_ref[...] = jnp.zeros_like(l_ref)

        dg_ref[...] += dg
        l_ref[...] += jnp.full(l_ref.shape, val, F32)

    row = BS((tr, d), lambda i: (i, 0))
    vec = BS((1, d), lambda i: (0, 0))
    return pl.pallas_call(
        body, grid=(m // tr,), in_specs=[row, vec, row], out_specs=[BS((8, 128), lambda i: (0, 0)), row, vec],
        out_shape=[SDS((8, 128), F32), SDS((m, d), F32), SDS((1, d), F32)], compiler_params=_cp("arbitrary"), name="final_loss",
    )(h, g, tgt)


def _prev_next(z, t):
    row = lax.broadcasted_iota(jnp.int32, z.shape, 0)
    zp = jnp.where(row == 0, 0.0, pltpu.roll(z, 1, axis=0))
    zn = jnp.where(row == t - 1, 0.0, pltpu.roll(z, t - 1, axis=0))
    return zp, zn


def _shift_fwd(z3, mu_p, mu_n):
    b, t, c = z3.shape
    nc = c // 128

    def body(z_ref, mp_ref, mn_ref, o_ref):
        z = z_ref[0]
        zp, zn = _prev_next(z, t)
        o_ref[0] = z + mp_ref[...] * (zp - z) + mn_ref[...] * (zn - z)

    blk = BS((1, t, 128), lambda i, j: (i, 0, j))
    vec = BS((1, 128), lambda i, j: (0, j))
    return pl.pallas_call(
        body, grid=(b, nc), in_specs=[blk, vec, vec], out_specs=blk, out_shape=SDS((b, t, c), F32),
        compiler_params=_cp("parallel", "parallel"), name="shift_fwd",
    )(z3, mu_p, mu_n)


def _shift_bwd(dzs3, z3, mu_p, mu_n):
    b, t, c = z3.shape
    nc = c // 128

    def body(d_ref, z_ref, mp_ref, mn_ref, dz_ref, dmp_ref, dmn_ref):
        d, z = d_ref[0], z_ref[0]
        mp, mn = mp_ref[...], mn_ref[...]
        zp, zn = _prev_next(z, t)
        _, dp_next = _prev_next(d * mp, t)
        dn_prev, _ = _prev_next(d * mn, t)
        dz_ref[0] = (d * (1.0 - mp - mn) + dp_next + dn_prev).astype(BF16)

        @pl.when(pl.program_id(1) == 0)
        def _():
            dmp_ref[...] = jnp.zeros_like(dmp_ref)
            dmn_ref[...] = jnp.zeros_like(dmn_ref)

        dmp_ref[...] += jnp.sum(d * (zp - z), axis=0, keepdims=True)
        dmn_ref[...] += jnp.sum(d * (zn - z), axis=0, keepdims=True)

    blk = BS((1, t, 128), lambda j, i: (i, 0, j))
    vec = BS((1, 128), lambda j, i: (0, j))
    return pl.pallas_call(
        body, grid=(nc, b), in_specs=[blk, blk, vec, vec], out_specs=[blk, vec, vec],
        out_shape=[SDS((b, t, c), BF16), SDS((1, c), F32), SDS((1, c), F32)],
        compiler_params=_cp("parallel", "arbitrary"), name="shift_bwd",
    )(dzs3, z3, mu_p, mu_n)


def _const(shape):
    nd = len(shape)
    return BS(shape, lambda i: (0,) * nd)


def _prep_fwd(zs, p):
    m = zs.shape[0]
    tr = 256
    params = [p["w0"], p["w2"], p["a0"], p["a2"], p["g2"], p["k_k"], p["k_a"], p["ones_bd"]]

    def body(zs_ref, w0, w2, a0, a2, g2, kk_, ka_, bd, wf, wb, kf, kb, kk, kaf, kab, g):
        outs = _prep_fn(zs_ref[...], w0[...], w2[...], a0[...], a2[...], g2[...], kk_[...], ka_[...], bd[...])
        for ref, val in zip((wf, wb, kf, kb, kk, kaf, kab, g), outs):
            ref[...] = val

    row = BS((tr, 512), lambda i: (i, 0))
    return pl.pallas_call(
        body, grid=(m // tr,), in_specs=[BS((tr, RWKV_COLS), lambda i: (i, 0))] + [_const(q.shape) for q in params],
        out_specs=[row] * 8, out_shape=[SDS((m, 512), F32)] * 8, compiler_params=_cp("parallel"), name="rwkv_prep_fwd",
    )(zs, *params)


def _prep_bwd(zs, p, ct_rows):
    m = zs.shape[0]
    tr = 128
    params = [p["w0"], p["w2"], p["a0"], p["a2"], p["g2"], p["k_k"], p["k_a"]]
    names = ["dwf", "dwb", "dkf", "dkb", "dkk_f", "dkk_b", "dkaf", "dkab", "dr_f", "dr_b", "dr_p", "dk_p", "dv_p", "dg",
             "dv_f", "dv_b"]
    rows = [ct_rows[n] for n in names]

    def body(zs_ref, w0, w2, a0, a2, g2, kk_, ka_, bd, *rest):
        c = {n: r[...] for n, r in zip(names, rest[:len(names)])}
        outs = rest[len(names):]
        dzs_ref, grads = outs[0], outs[1:]
        ones_bd = bd[...]
        _, vjp = jax.vjp(
            lambda *q: _prep_fn(*q, ones_bd), zs_ref[...], w0[...], w2[...], a0[...], a2[...], g2[...], kk_[...], ka_[...]
        )
        cts = (c["dwf"], c["dwb"], c["dkf"] + c["dk_p"], c["dkb"] + c["dk_p"], c["dkk_f"] + c["dkk_b"], c["dkaf"], c["dkab"], c["dg"])
        dzs, *dparams = vjp(cts)
        dr = c["dr_f"] + c["dr_b"] + c["dr_p"]
        dv = c["dv_f"] + c["dv_b"] + c["dv_p"]
        dzs_ref[:, 0:512] = dzs[:, 0:512] + dr
        dzs_ref[:, 512:1024] = dzs[:, 512:1024]
        dzs_ref[:, 1024:1536] = dzs[:, 1024:1536] + dv
        dzs_ref[:, 1536:1920] = dzs[:, 1536:1920]

        @pl.when(pl.program_id(0) == 0)
        def _():
            for gr in grads:
                gr[...] = jnp.zeros_like(gr)

        for gr, val in zip(grads, dparams):
            gr[...] += val

    row = BS((tr, 512), lambda i: (i, 0))
    return pl.pallas_call(
        body, grid=(m // tr,),
        in_specs=[BS((tr, RWKV_COLS), lambda i: (i, 0))] + [_const(q.shape) for q in params] + [_const(p["ones_bd"].shape)]
        + [row] * len(names),
        out_specs=[BS((tr, RWKV_COLS), lambda i: (i, 0))] + [_const(q.shape) for q in params],
        out_shape=[SDS((m, RWKV_COLS), F32)] + [SDS(q.shape, F32) for q in params],
        compiler_params=_cp("arbitrary"), name="rwkv_prep_bwd",
    )(zs, *params, p["ones_bd"], *rows)


def _post_specs(tr):
    r = BS((tr, 512), lambda i: (i, 0))
    v = BS((tr, 512), lambda i: (i, 2))
    row = BS((tr, 512), lambda i: (i, 0))
    return r, v, row


def _post_fwd(y_f, y_b, zs, kf, kb, g, p):
    m = zs.shape[0]
    tr = 256
    r, v, row = _post_specs(tr)
    vecs = [p["r_k"], p["ln_x_g"], p["ln_x_b"], p["ones_bd"]]

    def body(yf, yb, r_ref, v_ref, kf_ref, kb_ref, g_ref, rk, lg, lb, bd, o_ref):
        o_ref[...] = _post_fn(
            yf[...] + yb[...], r_ref[...], kf_ref[...], kb_ref[...], v_ref[...], g_ref[...], rk[...], lg[...], lb[...], bd[...]
        ).astype(BF16)

    return pl.pallas_call(
        body, grid=(m // tr,), in_specs=[row, row, r, v, row, row, row] + [_const(q.shape) for q in vecs],
        out_specs=row, out_shape=SDS((m, 512), BF16), compiler_params=_cp("parallel"), name="rwkv_post_fwd",
    )(y_f, y_b, zs, zs, kf, kb, g, *vecs)


def _post_bwd(y_f, y_b, zs, kf, kb, g, p, dymix):
    m = zs.shape[0]
    tr = 128
    r, v, row = _post_specs(tr)
    vecs = [p["r_k"], p["ln_x_g"], p["ln_x_b"]]

    def body(yf, yb, r_ref, v_ref, kf_ref, kb_ref, g_ref, rk, lg, lb, bd, dy_ref, dyo, dr, dk, dv, dg, drk, dlg, dlb):
        ones_bd = bd[...]
        _, vjp = jax.vjp(
            lambda *q: _post_fn(*q, ones_bd),
            yf[...] + yb[...], r_ref[...], kf_ref[...], kb_ref[...], v_ref[...], g_ref[...], rk[...], lg[...], lb[...],
        )
        c_y, c_r, c_kf, _, c_v, c_g, c_rk, c_lg, c_lb = vjp(dy_ref[...])
        dyo[...] = c_y
        dr[...] = c_r
        dk[...] = c_kf
        dv[...] = c_v
        dg[...] = c_g

        @pl.when(pl.program_id(0) == 0)
        def _():
            for ref in (drk, dlg, dlb):
                ref[...] = jnp.zeros_like(ref)

        drk[...] += c_rk
        dlg[...] += c_lg
        dlb[...] += c_lb

    vec = _const((1, 512))
    return pl.pallas_call(
        body, grid=(m // tr,),
        in_specs=[row, row, r, v, row, row, row] + [_const(q.shape) for q in vecs] + [_const(p["ones_bd"].shape), row],
        out_specs=[row, row, row, row, row, vec, vec, vec],
        out_shape=[SDS((m, 512), F32)] * 5 + [SDS((1, 512), F32)] * 3,
        compiler_params=_cp("arbitrary"), name="rwkv_post_bwd",
    )(y_f, y_b, zs, zs, kf, kb, g, *vecs, p["ones_bd"], dymix)


SCAN_MXU_GROUPS = 2


def _half_ones():
    ri = lax.broadcasted_iota(jnp.int32, (128, 128), 0)
    ci = lax.broadcasted_iota(jnp.int32, (128, 128), 1)
    return jnp.where((ri < 64) == (ci < 64), 1.0, 0.0).astype(BF16)


def _half_sums(xs, ones):
    out = []
    per = -(-len(xs) // SCAN_MXU_GROUPS)
    for g in range(0, len(xs), per):
        part = xs[g:g + per]
        res = jnp.dot(jnp.concatenate(part, axis=0).astype(BF16), ones, preferred_element_type=F32)
        out += [res[64 * i:64 * i + 64] for i in range(len(part))]
    return out


def _scan_specs(b, t):
    nc = t // SCAN_CHUNK
    up, down = (lambda c: c), (lambda c: nc - 1 - c)
    rows = [BS((b, SCAN_CHUNK, 512), lambda c, ci=ci: (0, ci(c), 0)) for ci in (up, down)]
    vrows = [BS((b, SCAN_CHUNK, 512), lambda c, ci=ci: (0, ci(c), 2)) for ci in (up, down)]
    hist = [BS((SCAN_CHUNK, b * 4, 64, 128), lambda c, ci=ci: (ci(c), 0, 0, 0)) for ci in (up, down)]
    return nc, rows, vrows, hist


class _Window:
    def __init__(self, g, ascending):
        self.bases = [pl.multiple_of(g * 8, 8) if asc else pl.multiple_of(SCAN_CHUNK - 8 - g * 8, 8) for asc in ascending]
        self.ascending = ascending
        self.blocks = {}
        self.row_id = lax.broadcasted_iota(jnp.int32, (8, 128), 0)

    def j(self, d, s):
        return s if self.ascending[d] else 7 - s

    def time(self, d, s):
        return self.bases[d] + self.j(d, s)

    def row(self, ref, d, bi, cols, s):
        key = (id(ref), d, bi, cols.start)
        if key not in self.blocks:
            self.blocks[key] = ref[bi, pl.ds(self.bases[d], 8), cols]
        jj = self.j(d, s)
        return self.blocks[key][jj:jj + 1, :]

    def put(self, buf, key, d, s, row):
        prev = buf.get(key)
        new = jnp.broadcast_to(row, (8, 128))
        buf[key] = new if prev is None else jnp.where(self.row_id == self.j(d, s), new, prev)

    def flush(self, buf, refs_of):
        for key, val in buf.items():
            ref, d, bi, cols = refs_of(key)
            ref[bi, pl.ds(self.bases[d], 8), cols] = val


def _pairs(b):
    return [(bi * 4 + p, bi, slice(128 * p, 128 * p + 128)) for bi in range(b) for p in range(4)]


def _colsum(x):
    return jnp.sum(x, axis=0, keepdims=True)


def _pair_matvec(row, mat):
    rid = lax.broadcasted_iota(jnp.int32, (8, 64), 0)
    lhs = jnp.where(rid == 0, row[:, :64], jnp.where(rid == 1, row[:, 64:], 0.0))
    out = jnp.dot(lhs.astype(BF16), mat.astype(BF16), preferred_element_type=F32)
    lo = lax.broadcasted_iota(jnp.int32, (1, 128), 1) < 64
    return jnp.where(lo, out[0:1], out[1:2])


def _eye_mask():
    return (lax.broadcasted_iota(jnp.int32, (64, 128), 1) & 63) == lax.broadcasted_iota(jnp.int32, (64, 128), 0)


def _scan_fwd(zs, kk, ops_f, ops_b, shard):
    b, t = zs.shape[:2]
    nc, rows, vrows, hist = _scan_specs(b, t)
    npair = b * 4

    def body(*refs):
        ins, shard_ref, outs, s_ref = refs[:12], refs[12], refs[13:17], refs[17]
        gather = (shard_ref, *refs[18:21])
        dirs = [dict(zip(("r", "kk", "v", "w", "k", "ka", "y", "h"), (*ins[6 * d:6 * d + 6], *outs[2 * d:2 * d + 2])))
                for d in (0, 1)]

        @pl.when(pl.program_id(0) == 0)
        def _():
            s_ref[...] = jnp.zeros_like(s_ref)
            _gather_halves(*gather, "start")

        @pl.when(pl.program_id(0) == nc - 1)
        def _():
            _gather_halves(*gather, "finish")

        ones, eye = _half_ones(), _eye_mask()
        chains = [(d, pr, bi, cols) for d in (0, 1) for pr, bi, cols in _pairs(b)]

        def eight_steps(g, carry):
            win = _Window(g, (True, False))
            ybuf = {}
            for s in range(8):
                s_prev, xa = [], []
                for d, pr, bi, cols in chains:
                    q = dirs[d]
                    st = s_ref[d * npair + pr]
                    q["h"][win.time(d, s), pr] = st
                    s_prev.append(st)
                    xa += [st * win.row(q["kk"], d, bi, cols, s), jnp.where(eye, win.row(q["v"], d, bi, cols, s), 0.0)]
                ra = _half_sums(xa, ones)
                xb = []
                for i, (d, pr, bi, cols) in enumerate(chains):
                    q = dirs[d]
                    s_new = s_prev[i] * win.row(q["w"], d, bi, cols, s) - ra[2 * i] * win.row(q["ka"], d, bi, cols, s) \
                        + ra[2 * i + 1] * win.row(q["k"], d, bi, cols, s)
                    s_ref[d * npair + pr] = s_new
                    xb.append(s_new * win.row(q["r"], d, bi, cols, s))
                rb = _half_sums(xb, ones)
                for i, (d, pr, bi, cols) in enumerate(chains):
                    win.put(ybuf, i, d, s, _colsum(jnp.where(eye, rb[i], 0.0)))
            win.flush(ybuf, lambda i: (dirs[chains[i][0]]["y"], chains[i][0], chains[i][2], chains[i][3]))
            return carry

        lax.fori_loop(0, SCAN_CHUNK // 8, eight_steps, 0)

    row_shape, hist_shape = SDS((b, t, 512), F32), SDS((t, npair, 64, 128), F32)
    state = (2 * npair, 64, 128)
    return pl.pallas_call(
        body, grid=(nc,), in_specs=sum(([rows[d], rows[d], vrows[d]] + [rows[d]] * 3 for d in (0, 1)), []) + [ANY],
        out_specs=[rows[0], hist[0], rows[1], hist[1], BS(state, lambda c: (0, 0, 0)), ANY],
        out_shape=[row_shape, hist_shape, row_shape, hist_shape, SDS(state, F32), _gathered_shape(shard)],
        scratch_shapes=_gather_sems(), compiler_params=_cp("arbitrary"), name="wkv_scan",
    )(zs, kk, zs, *ops_f, zs, kk, zs, *ops_b, shard)


def _scan_bwd(zs, kk, dy, ops_f, hist_f, ops_b, hist_b, s_last, partials):
    b, t = zs.shape[:2]
    nc, rows, vrows, hist = _scan_specs(b, t)
    npair = b * 4
    names_in = ("r", "kk", "v", "dy", "w", "k", "ka", "h")
    names_out = ("dr", "dw", "dk", "dkk", "dka", "dv")

    def body(*refs):
        ins, last_ref, part_ref, outs, recv_ref = refs[:16], refs[16], refs[17], refs[18:30], refs[30]
        ds_ref, after_ref = refs[31], refs[32]
        scatter = (part_ref, recv_ref, refs[33], refs[34])
        dirs = [dict(zip(names_in + names_out, (*ins[8 * d:8 * d + 8], *outs[6 * d:6 * d + 6]))) for d in (0, 1)]

        @pl.when(pl.program_id(0) == 0)
        def _():
            ds_ref[...] = jnp.zeros_like(ds_ref)
            after_ref[...] = last_ref[...]
            _scatter_partials(*scatter, "start")

        @pl.when(pl.program_id(0) == nc - 1)
        def _():
            _scatter_partials(*scatter, "finish")

        ones, eye = _half_ones(), _eye_mask()
        chains = [(d, pr, bi, cols) for d in (0, 1) for pr, bi, cols in _pairs(b)]

        def eight_steps(g, carry):
            win = _Window(g, (False, True))
            obuf = {}
            s_after = [after_ref[d * npair + pr] for d, pr, _, _ in chains]
            for s in range(8):
                row = lambda name, d, bi, cols: win.row(dirs[d][name], d, bi, cols, s)
                s_prev, xa = [], []
                for d, pr, bi, cols in chains:
                    st = dirs[d]["h"][win.time(d, s), pr]
                    s_prev.append(st)
                    xa += [st * row("kk", d, bi, cols), jnp.where(eye, row("dy", d, bi, cols), 0.0)]
                ra = _half_sums(xa, ones)
                ds_now, xb = [], []
                for i, (d, pr, bi, cols) in enumerate(chains):
                    skk, dycol = ra[2 * i], ra[2 * i + 1]
                    ds = ds_ref[d * npair + pr] + dycol * row("r", d, bi, cols)
                    win.put(obuf, (i, "dr"), d, s, _pair_matvec(row("dy", d, bi, cols), s_after[i]))
                    win.put(obuf, (i, "dk"), d, s, _pair_matvec(row("v", d, bi, cols), ds))
                    win.put(obuf, (i, "dka"), d, s, -_colsum(ds * skk))
                    win.put(obuf, (i, "dw"), d, s, _colsum(ds * s_prev[i]))
                    ds_now.append(ds)
                    xb += [ds * row("k", d, bi, cols), ds * row("ka", d, bi, cols)]
                rb = _half_sums(xb, ones)
                for i, (d, pr, bi, cols) in enumerate(chains):
                    dskk_neg = rb[2 * i + 1]
                    win.put(obuf, (i, "dv"), d, s, _colsum(jnp.where(eye, rb[2 * i], 0.0)))
                    win.put(obuf, (i, "dkk"), d, s, -_colsum(s_prev[i] * dskk_neg))
                    ds_ref[d * npair + pr] = ds_now[i] * row("w", d, bi, cols) - dskk_neg * row("kk", d, bi, cols)
                s_after = s_prev
            for i, (d, pr, _, _) in enumerate(chains):
                after_ref[d * npair + pr] = s_after[i]
            win.flush(obuf, lambda key: (dirs[chains[key[0]][0]][key[1]], chains[key[0]][0], chains[key[0]][2], chains[key[0]][3]))
            return carry

        lax.fori_loop(0, SCAN_CHUNK // 8, eight_steps, 0)

    row_shape = SDS((b, t, 512), F32)
    state = (2 * npair, 64, 128)
    return pl.pallas_call(
        body, grid=(nc,),
        in_specs=sum(([rows[d], rows[d], vrows[d]] + [rows[d]] * 4 + [hist[d]] for d in (1, 0)), [])
        + [BS(state, lambda c: (0, 0, 0)), ANY],
        out_specs=[rows[1]] * 6 + [rows[0]] * 6 + [ANY],
        out_shape=[row_shape] * 12 + [SDS((3,) + partials.shape[1:], partials.dtype)],
        scratch_shapes=[pltpu.VMEM(state, F32), pltpu.VMEM(state, F32)] + _scatter_sems(),
        compiler_params=_cp("arbitrary"), name="wkv_scan_bwd",
    )(zs, kk, zs, dy, *ops_f, hist_f, zs, kk, zs, dy, *ops_b, hist_b, s_last, partials)


def _mla_fwd(zm, cs, sn, p, t):
    m = zm.shape[0]
    tr = 256
    per = t // tr
    params = [p["q_norm_g"], p["kv_norm_g"], p["wq"], p["wqr"], p["wk"], p["wv"]]

    def body(z_ref, cs_ref, sn_ref, gq, gkv, wq, wqr, wk, wv, q_ref, k_ref, v_ref):
        q, kf, v = _mla_fn(z_ref[...], cs_ref[...], sn_ref[...], gq[...], gkv[...], wq[...], wqr[...], wk[...], wv[...])
        q_ref[...] = q.astype(BF16)
        k_ref[...] = kf.astype(BF16)
        v_ref[...] = v.astype(BF16)

    tab = BS((tr, 128), lambda i: (i % per, 0))
    return pl.pallas_call(
        body, grid=(m // tr,), in_specs=[BS((tr, MLA_PAD_COLS), lambda i: (i, 0)), tab, tab] + [_const(q.shape) for q in params],
        out_specs=[BS((tr, 1024), lambda i: (i, 0)), BS((tr, 1024), lambda i: (i, 0)), BS((tr, 512), lambda i: (i, 0))],
        out_shape=[SDS((m, 1024), BF16), SDS((m, 1024), BF16), SDS((m, 512), BF16)], compiler_params=_cp("parallel"), name="mla_prep_fwd",
    )(zm, cs, sn, *params)


def _mla_bwd(zm, cs, sn, p, t, dq, dk, dv):
    m = zm.shape[0]
    tr = 128
    per = t // tr
    params = [p["q_norm_g"], p["kv_norm_g"], p["wq"], p["wqr"], p["wk"], p["wv"]]

    def body(z_ref, cs_ref, sn_ref, gq, gkv, wq, wqr, wk, wv, dq_ref, dk_ref, dv_ref, dz_ref, *grads):
        cs_v, sn_v = cs_ref[...], sn_ref[...]
        _, vjp = jax.vjp(
            lambda *q: _mla_fn(q[0], cs_v, sn_v, *q[1:]), z_ref[...], gq[...], gkv[...], wq[...], wqr[...], wk[...], wv[...]
        )
        dz, *dparams = vjp((dq_ref[...], dk_ref[...], dv_ref[...]))
        dz_ref[...] = dz.astype(BF16)

        @pl.when(pl.program_id(0) == 0)
        def _():
            for gr in grads:
                gr[...] = jnp.zeros_like(gr)

        for gr, val in zip(grads, dparams):
            gr[...] += val

    tab = BS((tr, 128), lambda i: (i % per, 0))
    wide = BS((tr, 1024), lambda i: (i, 0))
    return pl.pallas_call(
        body, grid=(m // tr,),
        in_specs=[BS((tr, MLA_PAD_COLS), lambda i: (i, 0)), tab, tab] + [_const(q.shape) for q in params]
        + [wide, wide, BS((tr, 512), lambda i: (i, 0))],
        out_specs=[BS((tr, MLA_PAD_COLS), lambda i: (i, 0))] + [_const(q.shape) for q in params],
        out_shape=[SDS((m, MLA_PAD_COLS), BF16)] + [SDS(q.shape, F32) for q in params],
        compiler_params=_cp("arbitrary"), name="mla_prep_bwd",
    )(zm, cs, sn, *params, dq, dk, dv)


_NT = (((1,), (1,)), ((), ()))
_TN = (((0,), (0,)), ((), ()))


def _attn_fwd(q, kf, v, b, t):
    m = q.shape[0]
    tq = 256
    nq = t // tq

    def body(q_ref, k_ref, v_ref, o_ref, l_ref):
        lo = lax.broadcasted_iota(jnp.int32, (1, 128), 1) < 64
        v_all = v_ref[...]
        o = jnp.zeros((tq, 128), F32)
        lse = []
        for h in range(2):
            hs = slice(128 * h, 128 * h + 128)
            s = lax.dot_general(q_ref[:, hs], k_ref[:, hs], _NT, preferred_element_type=F32)
            mx = jnp.max(s, axis=1, keepdims=True)
            e = jnp.exp(s - mx)
            den = jnp.sum(e, axis=1, keepdims=True)
            vh = jnp.where(lo if h == 0 else jnp.logical_not(lo), v_all, jnp.zeros_like(v_all))
            o = o + jnp.dot(e.astype(BF16), vh, preferred_element_type=F32) / den
            lse.append(mx + jnp.log(den))
        o_ref[...] = o
        l_ref[...] = jnp.where(lo, lse[0], lse[1])

    return pl.pallas_call(
        body, grid=(b, 4, nq),
        in_specs=[BS((tq, 256), lambda bi, hp, i: (bi * nq + i, hp)), BS((t, 256), lambda bi, hp, i: (bi, hp)),
                  BS((t, 128), lambda bi, hp, i: (bi, hp))],
        out_specs=[BS((tq, 128), lambda bi, hp, i: (bi * nq + i, hp))] * 2,
        out_shape=[SDS((m, 512), F32), SDS((m, 512), F32)], compiler_params=_cp("parallel", "parallel", "arbitrary"), name="attn_fwd",
    )(q, kf, v)


def _attn_bwd(q, kf, v, o, lse, do, b, t):
    m = q.shape[0]
    tq = 256
    nq = t // tq

    def body(q_ref, k_ref, v_ref, o_ref, l_ref, do_ref, dq_ref, dk_ref, dv_ref):
        lo = lax.broadcasted_iota(jnp.int32, (1, 128), 1) < 64

        @pl.when(pl.program_id(2) == 0)
        def _():
            dk_ref[...] = jnp.zeros_like(dk_ref)
            dv_ref[...] = jnp.zeros_like(dv_ref)

        v_all, o_all, l_all, do_all = v_ref[...], o_ref[...], l_ref[...], do_ref[...]
        dv_acc = jnp.zeros((t, 128), F32)
        for h in range(2):
            hs = slice(128 * h, 128 * h + 128)
            mask = lo if h == 0 else jnp.logical_not(lo)
            qh, kh = q_ref[:, hs], k_ref[:, hs]
            s = lax.dot_general(qh, kh, _NT, preferred_element_type=F32)
            lse_h = jnp.max(jnp.where(mask, l_all, -jnp.inf), axis=1, keepdims=True)
            pr = jnp.exp(s - lse_h)
            do_h = jnp.where(mask, do_all, 0.0)
            dp = lax.dot_general(do_h.astype(BF16), v_all, _NT, preferred_element_type=F32)
            dsum = jnp.sum(do_h * o_all, axis=1, keepdims=True)
            ds = (pr * (dp - dsum)).astype(BF16)
            dq_ref[:, hs] = jnp.dot(ds, kh, preferred_element_type=F32)
            dk_ref[:, hs] += lax.dot_general(ds, qh, _TN, preferred_element_type=F32)
            dv_acc = dv_acc + lax.dot_general(pr.astype(BF16), do_h.astype(BF16), _TN, preferred_element_type=F32)
        dv_ref[...] += dv_acc

    qspec = BS((tq, 256), lambda bi, hp, i: (bi * nq + i, hp))
    kspec = BS((t, 256), lambda bi, hp, i: (bi, hp))
    vspec = BS((t, 128), lambda bi, hp, i: (bi, hp))
    ospec = BS((tq, 128), lambda bi, hp, i: (bi * nq + i, hp))
    return pl.pallas_call(
        body, grid=(b, 4, nq), in_specs=[qspec, kspec, vspec, ospec, ospec, ospec], out_specs=[qspec, kspec, vspec],
        out_shape=[SDS((m, 1024), F32), SDS((m, 1024), F32), SDS((m, 512), F32)],
        compiler_params=_cp("parallel", "parallel", "arbitrary"), name="attn_bwd",
    )(q, kf, v, o, lse, do)


def _conv3(u, w_ref, b_ref, t):
    up, un = _prev_next(u, t)
    return w_ref[0:1, :] * up + w_ref[1:2, :] * u + w_ref[2:3, :] * un + b_ref[...], up, un


def _ffn_mid_fwd(ug3, uv3, cw, cb):
    b, t, f = ug3.shape
    nc = f // 256

    def body(ug_ref, uv_ref, wg_ref, wv_ref, bg_ref, bv_ref, a_ref):
        gc, _, _ = _conv3(ug_ref[0], wg_ref, bg_ref, t)
        vc, _, _ = _conv3(uv_ref[0], wv_ref, bv_ref, t)
        a_ref[0] = (gc * jax.nn.sigmoid(gc) * vc).astype(BF16)

    blk = BS((1, t, 256), lambda i, j: (i, 0, j))
    return pl.pallas_call(
        body, grid=(b, nc),
        in_specs=[blk, blk, BS((3, 256), lambda i, j: (0, j)), BS((3, 256), lambda i, j: (0, j + nc)),
                  BS((1, 256), lambda i, j: (0, j)), BS((1, 256), lambda i, j: (0, j + nc))],
        out_specs=blk, out_shape=SDS((b, t, f), BF16), compiler_params=_cp("parallel", "parallel"), name="ffn_mid_fwd",
    )(ug3, uv3, cw, cw, cb, cb)


def _ffn_mid_bwd(ug3, uv3, cw, cb, da3):
    b, t, f = ug3.shape
    nc = f // 256

    def half(u, up, un, dc, w_ref):
        dprev, dnext = _prev_next(dc, t)
        du = w_ref[1:2, :] * dc + w_ref[0:1, :] * dnext + w_ref[2:3, :] * dprev
        sums = [jnp.sum(dc * q, axis=0, keepdims=True) for q in (up, u, un)] + [jnp.sum(dc, axis=0, keepdims=True)]
        row = lax.broadcasted_iota(jnp.int32, (8, 256), 0)
        tab = jnp.zeros((8, 256), F32)
        for i, s in enumerate(sums):
            tab = jnp.where(row == i, s, tab)
        return du, tab

    def body(ug_ref, uv_ref, wg_ref, wv_ref, bg_ref, bv_ref, da_ref, dug_ref, duv_ref, tg_ref, tv_ref):
        ug, uv, da = ug_ref[0], uv_ref[0], da_ref[0]
        gc, gp, gn = _conv3(ug, wg_ref, bg_ref, t)
        vc, vp, vn = _conv3(uv, wv_ref, bv_ref, t)
        sg = jax.nn.sigmoid(gc)
        d_gc = da * vc * (sg * (1.0 + gc * (1.0 - sg)))
        d_vc = da * (gc * sg)
        dug, tg = half(ug, gp, gn, d_gc, wg_ref)
        duv, tv = half(uv, vp, vn, d_vc, wv_ref)
        dug_ref[0] = dug.astype(BF16)
        duv_ref[0] = duv.astype(BF16)

        @pl.when(pl.program_id(1) == 0)
        def _():
            tg_ref[...] = jnp.zeros_like(tg_ref)
            tv_ref[...] = jnp.zeros_like(tv_ref)

        tg_ref[...] += tg
        tv_ref[...] += tv

    blk = BS((1, t, 256), lambda j, i: (i, 0, j))
    tab = BS((8, 256), lambda j, i: (0, j))
    return pl.pallas_call(
        body, grid=(nc, b),
        in_specs=[blk, blk, BS((3, 256), lambda j, i: (0, j)), BS((3, 256), lambda j, i: (0, j + nc)),
                  BS((1, 256), lambda j, i: (0, j)), BS((1, 256), lambda j, i: (0, j + nc)), blk],
        out_specs=[blk, blk, tab, tab],
        out_shape=[SDS((b, t, f), BF16), SDS((b, t, f), BF16), SDS((8, f), F32), SDS((8, f), F32)],
        compiler_params=_cp("parallel", "arbitrary"), name="ffn_mid_bwd",
    )(ug3, uv3, cw, cw, cb, cb, da3)


def _add_rows(parts, name, out_dtype=F32):
    r = parts[0].shape[0]
    tr = _row_tile(r, 1024)
    n = len(parts)

    def body(*refs):
        acc = refs[0][...].astype(F32)
        for q in refs[1:n]:
            acc = acc + q[...].astype(F32)
        refs[n][...] = acc.astype(out_dtype)

    row = BS((tr, 128), lambda i: (i, 0))
    return pl.pallas_call(
        body, grid=(r // tr,), in_specs=[row] * n, out_specs=row, out_shape=SDS((r, 128), out_dtype),
        compiler_params=_cp("parallel"), name=name,
    )(*parts)


def _adamw(w, g, m, v, name):
    lead = w.shape[:-2]
    r, c = w.shape[-2:]
    tr = _row_tile(r)

    def body(w_ref, g_ref, m_ref, v_ref, d_ref, m2_ref, v2_ref):
        d, m2, v2 = _adamw_math(w_ref[...], g_ref[...], m_ref[...], v_ref[...])
        d_ref[...] = d
        m2_ref[...] = m2
        v2_ref[...] = v2

    blk = BS((1,) * len(lead) + (tr, c), lambda i: (0,) * len(lead) + (i, 0))
    return pl.pallas_call(
        body, grid=(r // tr,), in_specs=[blk] * 4, out_specs=[blk] * 3, out_shape=[SDS(w.shape, F32)] * 3,
        compiler_params=_cp("parallel"), name=name,
    )(w, g, m, v)


def _place():
    return lax.axis_index("x"), lax.axis_index("y"), lax.axis_index("c")


def _flip(v, bit):
    return 1 - v if bit else v


def _allgather_weights(shard):
    def body(x_ref, out_ref, send_sems, recv_sems):
        _gather_halves(x_ref, out_ref, send_sems, recv_sems, "start")
        _gather_halves(x_ref, out_ref, send_sems, recv_sems, "finish")

    return pl.pallas_call(
        body, out_shape=_gathered_shape(shard), in_specs=[ANY], out_specs=ANY, scratch_shapes=_gather_sems(),
        name="allgather_weights",
    )(shard)


def _gathered_shape(shard):
    return SDS((8 * (shard.shape[0] // 2), 128), shard.dtype)


def _gather_sems():
    return [pltpu.SemaphoreType.DMA((6,)), pltpu.SemaphoreType.DMA((6,))]


def _gather_halves(x_ref, out_ref, send_sems, recv_sems, phase):
    rh = x_ref.shape[0] // 2
    x, y, c = _place()
    me, sibling = (x, y, c), (x, y, 1 - c)
    chips = [(1 - x, y), (x, 1 - y), (1 - x, 1 - y)]
    mine_src = x_ref.at[pl.ds(c * rh, rh), :]

    def rows(px, py, pc):
        return out_ref.at[pl.ds((4 * px + 2 * py + pc) * rh, rh), :]

    def copy(k, block, to, src=None):
        return pltpu.make_async_remote_copy(
            src_ref=rows(*block) if src is None else src, dst_ref=rows(*block), send_sem=send_sems.at[k],
            recv_sem=recv_sems.at[k], device_id=to, device_id_type=MESH,
        )

    first = [copy(j, me, (*chip, c), src=mine_src) for j, chip in enumerate(chips)]
    if phase == "start":
        for cp in first:
            cp.start()
        return
    passed = [copy(3 + j, (*chip, c), sibling) for j, chip in enumerate(chips)]
    for j, chip in enumerate(chips):
        copy(j, (*chip, c), me).wait_recv()
        passed[j].start()
    for j, chip in enumerate(chips):
        copy(3 + j, (*chip, 1 - c), me).wait_recv()
    for cp in first + passed:
        cp.wait_send()


def _scatter_partials(g_ref, recv_ref, send_sems, recv_sems, phase):
    x, y, c = _place()
    copies = []
    for j, (fx, fy) in enumerate(((1, 0), (0, 1), (1, 1))):
        px, py = _flip(x, fx), _flip(y, fy)
        copies.append(pltpu.make_async_remote_copy(
            src_ref=g_ref.at[2 * px + py], dst_ref=recv_ref.at[j], send_sem=send_sems.at[j], recv_sem=recv_sems.at[j],
            device_id=(px, py, c), device_id_type=MESH,
        ))
    if phase == "start":
        for cp in copies:
            cp.start()
        return
    for cp in copies:
        cp.wait_recv()
    for cp in copies:
        cp.wait_send()


def _scatter_sems():
    return [pltpu.SemaphoreType.DMA((3,)), pltpu.SemaphoreType.DMA((3,))]


def _scatter_to_chips(g):
    def body(g_ref, recv_ref, send_sems, recv_sems):
        _scatter_partials(g_ref, recv_ref, send_sems, recv_sems, "start")
        _scatter_partials(g_ref, recv_ref, send_sems, recv_sems, "finish")

    return pl.pallas_call(
        body, out_shape=SDS((3,) + g.shape[1:], g.dtype), in_specs=[ANY], out_specs=ANY,
        scratch_shapes=[pltpu.SemaphoreType.DMA((3,)), pltpu.SemaphoreType.DMA((3,))], name="scatter_grads",
    )(g)


def _send_to_sibling(a, half_of_rows):
    rh = a.shape[1] // 2

    def body(a_ref, b_ref, send_sem, recv_sem):
        x, y, c = _place()
        src = a_ref.at[:, pl.ds((1 - c) * rh, rh), :] if half_of_rows else a_ref
        cp = pltpu.make_async_remote_copy(
            src_ref=src, dst_ref=b_ref, send_sem=send_sem, recv_sem=recv_sem, device_id=(x, y, 1 - c), device_id_type=MESH
        )
        cp.start()
        cp.wait()

    shape = (a.shape[0], rh, 128) if half_of_rows else a.shape
    return pl.pallas_call(
        body, out_shape=SDS(shape, a.dtype), in_specs=[ANY], out_specs=ANY,
        scratch_shapes=[pltpu.SemaphoreType.DMA, pltpu.SemaphoreType.DMA],
        name="sibling_halves" if half_of_rows else "sibling_swap",
    )(a)


def _allreduce_small(v):
    r = v.shape[0]

    def body(v_ref, out_ref, buf_ref, send_sems, recv_sems):
        x, y, c = _place()
        buf_ref[0] = v_ref[...]
        copies = []
        for k in range(1, 8):
            peer = (_flip(x, k >> 2 & 1), _flip(y, k >> 1 & 1), _flip(c, k & 1))
            cp = pltpu.make_async_remote_copy(
                src_ref=v_ref, dst_ref=buf_ref.at[k], send_sem=send_sems.at[k - 1], recv_sem=recv_sems.at[k - 1],
                device_id=peer, device_id_type=MESH,
            )
            cp.start()
            copies.append(cp)
        for cp in copies:
            cp.wait_recv()
        acc = None
        for d in range(8):
            slot = 4 * _flip(x, d >> 2 & 1) + 2 * _flip(y, d >> 1 & 1) + _flip(c, d & 1)
            term = buf_ref[slot]
            acc = term if acc is None else acc + term
        out_ref[...] = acc
        for cp in copies:
            cp.wait_send()

    return pl.pallas_call(
        body, out_shape=SDS(v.shape, F32), in_specs=[VMEM], out_specs=VMEM,
        scratch_shapes=[pltpu.VMEM((8, r, 128), F32), pltpu.SemaphoreType.DMA((7,)), pltpu.SemaphoreType.DMA((7,))],
        name="allreduce_small",
    )(v)


_BIG_A = (
    ("w_in", 0, False),
    ("decay_w2_fwd", 1, False), ("decay_w2_bwd", 1, False), ("iclr_a2_fwd", 1, False),
    ("iclr_a2_bwd", 1, False), ("gate_g2", 1, False),
)
_BIG_B = (
    ("w_uq", 0, False), ("w_ukv", 1, False), ("w_out", 0, False), ("w_ffn_up", 1, False), ("ffn_conv_w", 1, True),
    ("w_ffn_down", 0, False),
)
_BIG = _BIG_A + _BIG_B
_SMALL = (
    "ln_mix_g", "shift_mu_prev", "shift_mu_next", "decay_w0_fwd", "decay_w0_bwd", "iclr_a0_fwd", "iclr_a0_bwd", "k_k",
    "k_a", "r_k", "ln_x_g", "ln_x_b", "q_norm_g", "kv_norm_g", "mla_out_g", "ln_ffn_g", "ffn_conv_b", "ln_final_g",
)
_WEIGHTS = (
    "ln_mix_g", "w_in", "shift_mu_prev", "shift_mu_next", "decay_w0_fwd", "decay_w2_fwd", "decay_w0_bwd", "decay_w2_bwd",
    "iclr_a0_fwd", "iclr_a2_fwd", "iclr_a0_bwd", "iclr_a2_bwd", "gate_g2", "k_k", "k_a", "r_k", "ln_x_g", "ln_x_b",
    "q_norm_g", "w_uq", "kv_norm_g", "w_ukv", "mla_out_g", "w_out", "ln_ffn_g", "w_ffn_up", "ffn_conv_w", "ffn_conv_b",
    "w_ffn_down", "ln_final_g",
)


def _pad_rows(flat, rows):
    return jnp.pad(flat, (0, rows * 128 - flat.shape[0])).reshape(rows, 128)


def _rows_for(n, mult):
    rows = -(-n // 128)
    return -(-rows // mult) * mult


def _pack_shards_bf16(arrs, entries):
    parts = []
    for name, _, raw in entries:
        w = arrs[name][0]
        flat = lax.bitcast_convert_type(w, BF16).reshape(-1) if raw else w.astype(BF16).reshape(-1)
        parts.append(_pad_rows(flat, _rows_for(flat.shape[0], 32)))
    return jnp.concatenate(parts, axis=0)


def _unpack_gathered(g4, arrs, entries):
    out, off = {}, 0
    for name, axis, raw in entries:
        a, b = arrs[name].shape[1:]
        n = a * b * (2 if raw else 1)
        rows = _rows_for(n, 32)
        seg = g4[:, off:off + rows].reshape(4, rows * 128)[:, :n]
        off += rows
        if raw:
            seg = lax.bitcast_convert_type(seg.reshape(4, a * b, 2), F32)
        seg = seg.reshape(4, a, b)
        out[name] = jnp.concatenate([seg[s] for s in range(4)], axis=1) if axis == 1 else seg.reshape(4 * a, b)
    return out


def _pack_grads(full, arrs, entries):
    parts = []
    for name, axis, _ in entries:
        a, b = arrs[name].shape[1:]
        g = full[name]
        if g.ndim == 3:
            sh = g
        else:
            sh = g.reshape(a, 4, b).transpose(1, 0, 2) if axis == 1 else g.reshape(4, a, b)
        rows = _rows_for(a * b, 8)
        parts.append(jnp.pad(sh.reshape(4, a * b), ((0, 0), (0, rows * 128 - a * b))).reshape(4, rows, 128))
    total = sum(q.shape[1] for q in parts)
    parts.append(jnp.zeros((4, -(-total // 1024) * 1024 - total, 128), F32))
    return jnp.concatenate(parts, axis=1)


def _unpack_grads(g, arrs, entries):
    out, off = {}, 0
    for name, _, _ in entries:
        a, b = arrs[name].shape[1:]
        rows = _rows_for(a * b, 8)
        out[name] = g[off:off + rows].reshape(-1)[:a * b].reshape(1, a, b)
        off += rows
    return out


def _pack_small(vals):
    flat = jnp.concatenate([vals[n].reshape(-1).astype(F32) for n in _SMALL] + [vals["_loss"].reshape(-1)])
    return _pad_rows(flat, _rows_for(flat.shape[0], 8))


def _unpack_small(buf, arrs):
    flat, out, off = buf.reshape(-1), {}, 0
    for n in _SMALL:
        size = arrs[n].size
        out[n] = flat[off:off + size].reshape(arrs[n].shape)
        off += size
    out["_loss"] = flat[off]
    return out


def _rot_cols(w):
    return jnp.concatenate([-w[..., 16:], w[..., :16]], axis=-1)


def _rot_cols_t(g):
    return jnp.concatenate([g[..., 16:], -g[..., :16]], axis=-1)


def _rope_tables(t):
    inv = jnp.power(ROPE_THETA, -jnp.arange(0, ROPE_DIM, 2, dtype=F32) / ROPE_DIM)
    ang = jnp.arange(t, dtype=F32)[:, None] * inv[None, :]
    one, zero = jnp.ones((t, 64), F32), jnp.zeros((t, 64), F32)
    cs = jnp.concatenate([one, jnp.cos(ang), jnp.cos(ang), zero[:, :32]], axis=1)
    sn = jnp.concatenate([zero, jnp.sin(ang), jnp.sin(ang), zero[:, :32]], axis=1)
    return cs, sn


def _block_diag(a, b):
    za = jnp.zeros_like(a)
    return jnp.concatenate([jnp.concatenate([a, za], axis=1), jnp.concatenate([za, b], axis=1)], axis=0)


def kernel(x, ln_mix_g, w_in, shift_mu_prev, shift_mu_next, decay_w0_fwd, decay_w2_fwd, decay_w0_bwd, decay_w2_bwd, iclr_a0_fwd, iclr_a2_fwd, iclr_a0_bwd, iclr_a2_bwd, gate_g2, k_k, k_a, r_k, ln_x_g, ln_x_b, q_norm_g, w_uq, kv_norm_g, w_ukv, mla_out_g, w_out, ln_ffn_g, w_ffn_up, ffn_conv_w, ffn_conv_b, w_ffn_down, ln_final_g, loss_target, m_ln_mix_g, m_w_in, m_shift_mu_prev, m_shift_mu_next, m_decay_w0_fwd, m_decay_w2_fwd, m_decay_w0_bwd, m_decay_w2_bwd, m_iclr_a0_fwd, m_iclr_a2_fwd, m_iclr_a0_bwd, m_iclr_a2_bwd, m_gate_g2, m_k_k, m_k_a, m_r_k, m_ln_x_g, m_ln_x_b, m_q_norm_g, m_w_uq, m_kv_norm_g, m_w_ukv, m_mla_out_g, m_w_out, m_ln_ffn_g, m_w_ffn_up, m_ffn_conv_w, m_ffn_conv_b, m_w_ffn_down, m_ln_final_g, v_ln_mix_g, v_w_in, v_shift_mu_prev, v_shift_mu_next, v_decay_w0_fwd, v_decay_w2_fwd, v_decay_w0_bwd, v_decay_w2_bwd, v_iclr_a0_fwd, v_iclr_a2_fwd, v_iclr_a0_bwd, v_iclr_a2_bwd, v_gate_g2, v_k_k, v_k_a, v_r_k, v_ln_x_g, v_ln_x_b, v_q_norm_g, v_w_uq, v_kv_norm_g, v_w_ukv, v_mla_out_g, v_w_out, v_ln_ffn_g, v_w_ffn_up, v_ffn_conv_w, v_ffn_conv_b, v_w_ffn_down, v_ln_final_g):
    arrs = dict(locals())
    for pre in ("", "m_", "v_"):
        arrs[pre + "w_in"] = jnp.swapaxes(arrs[pre + "w_in"], 1, 2)
    b, t, d = x.shape
    m = b * t
    x2 = x.reshape(m, d)
    tgt = loss_target.reshape(m, d)
    vec = lambda n: arrs[n].reshape(1, -1)

    core = lax.axis_index("c")
    chip = 2 * lax.axis_index("x") + lax.axis_index("y")
    def unpack(gathered, shard, entries):
        g4 = lax.dynamic_update_index_in_dim(gathered.reshape(N_CHIPS, -1, 128), shard, chip, axis=0)
        return _unpack_gathered(g4, arrs, entries)

    shard_a, shard_b = _pack_shards_bf16(arrs, _BIG_A), _pack_shards_bf16(arrs, _BIG_B)
    fw = unpack(_allgather_weights(shard_a), shard_a, _BIG_A)
    win = fw["w_in"]
    zc = jnp.zeros((64, d), BF16)
    w_kr = win[2944:2976]
    rot_kr = jnp.swapaxes(_rot_cols(jnp.swapaxes(w_kr, 0, 1)), 0, 1)
    win_m = jnp.concatenate([win[1920:2944], zc, w_kr, zc[:32], zc, rot_kr, zc[:32]], axis=0)
    win_r = win[:RWKV_COLS]
    head = jnp.arange(512) // HEAD_DIM
    rw = dict(
        w0=jnp.concatenate([vec("decay_w0_fwd"), vec("decay_w0_bwd")], axis=1),
        w2=_block_diag(fw["decay_w2_fwd"], fw["decay_w2_bwd"]).astype(F32),
        a0=jnp.concatenate([vec("iclr_a0_fwd"), vec("iclr_a0_bwd")], axis=1),
        a2=_block_diag(fw["iclr_a2_fwd"], fw["iclr_a2_bwd"]).astype(F32),
        g2=fw["gate_g2"].astype(F32), k_k=vec("k_k"), k_a=vec("k_a"), r_k=vec("r_k"), ln_x_g=vec("ln_x_g"), ln_x_b=vec("ln_x_b"),
        ones_bd=(head[:, None] == head[None, :]).astype(F32),
    )
    cs, sn = _rope_tables(t)

    n1 = _rms_fwd(x2, vec("ln_mix_g"), "rms_mix")
    zm = _mm(n1, win_m, "nt", "proj_in_mla")
    zr = _mm(n1, win_r, "nt", "proj_in_rwkv")
    zs = _shift_fwd(zr.reshape(b, t, RWKV_COLS), vec("shift_mu_prev"), vec("shift_mu_next"))
    zs2 = zs.reshape(m, RWKV_COLS)
    wf, wb, kf, kb, kk, kaf, kab, gate = _prep_fwd(zs2, rw)
    r4 = lambda a: a.reshape(b, t, 512)
    f2 = lambda a: a.reshape(m, 512)
    kk4 = r4(kk)
    ops_f = (r4(wf), r4(kf), r4(kaf))
    ops_b = (r4(wb), r4(kb), r4(kab))
    y_f, hist_f, y_b, hist_b, s_last, gathered_b = _scan_fwd(zs, kk4, ops_f, ops_b, shard_b)
    fw.update(unpack(gathered_b, shard_b, _BIG_B))
    uq = fw["w_uq"].astype(F32).reshape(Q_RANK, HEADS, 96)
    z32 = jnp.zeros((Q_RANK, HEADS, 32), F32)
    wq = jnp.concatenate([uq[..., :64], uq[..., 64:], z32], axis=-1).reshape(Q_RANK, 1024)
    wqr = jnp.concatenate([z32, z32, _rot_cols(uq[..., 64:]), z32], axis=-1).reshape(Q_RANK, 1024)
    ukv = fw["w_ukv"].astype(F32).reshape(KV_RANK, HEADS, 128)
    wk = jnp.concatenate([ukv[..., :64], jnp.zeros_like(ukv[..., :64])], axis=-1).reshape(KV_RANK, 1024)
    wv = ukv[..., 64:].reshape(KV_RANK, 512)
    mp = dict(q_norm_g=vec("q_norm_g"), kv_norm_g=vec("kv_norm_g"), wq=wq, wqr=wqr, wk=wk, wv=wv)
    w_up_g, w_up_v = fw["w_ffn_up"][:, :D_FF], fw["w_ffn_up"][:, D_FF:]
    cw, cb = fw["ffn_conv_w"], vec("ffn_conv_b")
    y_f, y_b = f2(y_f), f2(y_b)
    y_rwkv = _post_fwd(y_f, y_b, zs2, kf, kb, gate, rw)
    q, kfull, v = _mla_fwd(zm, cs, sn, mp, t)
    o, lse = _attn_fwd(q, kfull, v, b, t)
    y_mla = _rms_fwd(o, vec("mla_out_g"), "rms_mla_out")
    ymix = jnp.concatenate([y_rwkv, y_mla], axis=1)
    h1 = _mm(ymix, fw["w_out"], "nn", "proj_out", add=x2)
    n2 = _rms_fwd(h1, vec("ln_ffn_g"), "rms_ffn")
    ug = _mm(n2, w_up_g, "nn", "ffn_up_gate")
    uv = _mm(n2, w_up_v, "nn", "ffn_up_val")
    r3f = lambda a: a.reshape(b, t, D_FF)
    act = _ffn_mid_fwd(r3f(ug), r3f(uv), cw, cb).reshape(m, D_FF)
    h2 = _mm(act, fw["w_ffn_down"], "nn", "ffn_down", add=h1)
    loss_tab, dh2, g_ln_final = _final(h2, vec("ln_final_g"), tgt)

    gfull = {}
    dact = _mm(dh2, fw["w_ffn_down"], "nt", "d_ffn_act")
    gfull["w_ffn_down"] = _mm(act, dh2, "tn", "g_ffn_down")
    dug, duv, tab_g, tab_v = _ffn_mid_bwd(r3f(ug), r3f(uv), cw, cb, r3f(dact))
    dug, duv = dug.reshape(m, D_FF), duv.reshape(m, D_FF)
    gfull["ffn_conv_w"] = jnp.concatenate([tab_g[0:3], tab_v[0:3]], axis=1)
    g_conv_b = jnp.concatenate([tab_g[3:4], tab_v[3:4]], axis=1)
    dn2 = _mm(duv, w_up_v, "nt", "d_ffn_in_val", add=_mm(dug, w_up_g, "nt", "d_ffn_in_gate"))
    shard_cols = arrs["w_ffn_up"].shape[2]
    gfull["w_ffn_up"] = jnp.concatenate([_mm(n2, dug, "tn", "g_ffn_up_gate", column_blocks=shard_cols),
                                         _mm(n2, duv, "tn", "g_ffn_up_val", column_blocks=shard_cols)], axis=0)
    dh1, g_ln_ffn = _rms_bwd(h1, vec("ln_ffn_g"), dn2, "rms_ffn_bwd", dres=dh2)
    dymix = _mm(dh1, fw["w_out"], "nt", "d_mix")
    gfull["w_out"] = _mm(ymix, dh1, "tn", "g_w_out")
    do, g_mla_out = _rms_bwd(o, vec("mla_out_g"), dymix, "rms_mla_out_bwd", dy_block=1)
    dq, dk, dv = _attn_bwd(q, kfull, v, o, lse, do, b, t)
    dzm, g_qn, g_kvn, g_wq, g_wqr, g_wk, g_wv = _mla_bwd(zm, cs, sn, mp, t, dq, dk, dv)
    gq3, gqr3 = g_wq.reshape(Q_RANK, HEADS, 128), g_wqr.reshape(Q_RANK, HEADS, 128)
    gfull["w_uq"] = jnp.concatenate(
        [gq3[..., :64], gq3[..., 64:96] + _rot_cols_t(gqr3[..., 64:96])], axis=-1
    ).reshape(Q_RANK, HEADS * 96)
    gfull["w_ukv"] = jnp.concatenate(
        [g_wk.reshape(KV_RANK, HEADS, 128)[..., :64], g_wv.reshape(KV_RANK, HEADS, 64)], axis=-1
    ).reshape(KV_RANK, 1024)
    def cores_first(entries, tag):
        packed = _pack_grads(gfull, arrs, entries)
        rh = packed.shape[1] // 2
        own = lax.dynamic_slice_in_dim(packed, core * rh, rh, axis=1)
        sib = _send_to_sibling(packed, True)
        return _add_rows([own.reshape(4 * rh, 128), sib.reshape(4 * rh, 128)], "sum_cores_" + tag, BF16).reshape(4, rh, 128)

    def chips_then_join(chip_part, recv, entries, tag):
        mine = lax.dynamic_index_in_dim(chip_part, chip, axis=0, keepdims=False)
        half = _add_rows([mine, recv[0], recv[1], recv[2]], "sum_chips_" + tag)
        other = _send_to_sibling(half, False)
        lower = jnp.where(core == 0, half, other)
        upper = jnp.where(core == 0, other, half)
        return _unpack_grads(jnp.concatenate([lower, upper], axis=0), arrs, entries)

    part_b = cores_first(_BIG_B, "b")
    dys, dr_p, dk_p, dv_p, dgate, g_rk, g_lnx_g, g_lnx_b = _post_bwd(y_f, y_b, zs2, kf, kb, gate, rw, dymix)
    (dr_f, dwf, dkf, dkk_f, dkaf, dv_f, dr_b, dwb, dkb, dkk_b, dkab, dv_b, recv_b) = _scan_bwd(
        zs, kk4, r4(dys), ops_f, hist_f, ops_b, hist_b, s_last, part_b)
    g_big = chips_then_join(part_b, recv_b, _BIG_B, "b")
    cts = dict(dwf=f2(dwf), dwb=f2(dwb), dkf=f2(dkf), dkb=f2(dkb), dkk_f=f2(dkk_f), dkk_b=f2(dkk_b), dkaf=f2(dkaf), dkab=f2(dkab),
               dr_f=f2(dr_f), dr_b=f2(dr_b), dr_p=dr_p, dk_p=dk_p, dv_p=dv_p, dg=dgate, dv_f=f2(dv_f), dv_b=f2(dv_b))
    dzs, g_w0, g_w2, g_a0, g_a2, g_g2, g_kk, g_ka = _prep_bwd(zs2, rw, cts)
    dzr, g_mu_p, g_mu_n = _shift_bwd(dzs.reshape(b, t, RWKV_COLS), zr.reshape(b, t, RWKV_COLS), vec("shift_mu_prev"), vec("shift_mu_next"))
    dzr = dzr.reshape(m, RWKV_COLS)
    gfull["decay_w2_fwd"], gfull["decay_w2_bwd"] = g_w2[:64, :512], g_w2[64:, 512:]
    gfull["iclr_a2_fwd"], gfull["iclr_a2_bwd"] = g_a2[:64, :512], g_a2[64:, 512:]
    gfull["gate_g2"] = g_g2
    dn1 = _mm(dzr, win_r, "nn", "d_proj_in_rwkv", add=_mm(dzm, win_m, "nn", "d_proj_in_mla"))
    g_m = _mm(dzm, n1, "tn", "g_w_in_mla")
    g_r = _mm(dzr, n1, "tn", "g_w_in_rwkv")
    g_kr = g_m[1088:1120] + jnp.swapaxes(_rot_cols_t(jnp.swapaxes(g_m[1216:1248], 0, 1)), 0, 1)
    gfull["w_in"] = jnp.concatenate([g_r, g_m[:1024], g_kr], axis=0)
    dx, g_ln_mix = _rms_bwd(x2, vec("ln_mix_g"), dn1, "rms_mix_bwd", dres=dh1)

    part_a = cores_first(_BIG_A, "a")
    g_big.update(chips_then_join(part_a, _scatter_to_chips(part_a), _BIG_A, "a"))
    small = {
        "ln_mix_g": g_ln_mix, "shift_mu_prev": g_mu_p, "shift_mu_next": g_mu_n, "decay_w0_fwd": g_w0[:, :512],
        "decay_w0_bwd": g_w0[:, 512:], "iclr_a0_fwd": g_a0[:, :512], "iclr_a0_bwd": g_a0[:, 512:], "k_k": g_kk, "k_a": g_ka,
        "r_k": g_rk, "ln_x_g": g_lnx_g, "ln_x_b": g_lnx_b, "q_norm_g": g_qn, "kv_norm_g": g_kvn, "mla_out_g": g_mla_out,
        "ln_ffn_g": g_ln_ffn, "ffn_conv_b": g_conv_b, "ln_final_g": g_ln_final,
        "_loss": jnp.pad(loss_tab[0, 0:1], (0, 127)),
    }
    g_small_buf = _allreduce_small(_pack_small(small))
    g_small = _unpack_small(g_small_buf, arrs)

    grads, deltas, new_m, new_v = {}, {}, {}, {}
    for name, _, _ in _BIG:
        grads[name] = g_big[name]
        deltas[name], new_m[name], new_v[name] = _adamw(
            arrs[name], g_big[name], arrs["m_" + name], arrs["v_" + name], "adamw_" + name)
    pk = lambda pre: _pack_small({**{n: arrs[pre + n] for n in _SMALL}, "_loss": jnp.zeros((128,), F32)})
    sd, sm, sv = _adamw(pk(""), g_small_buf, pk("m_"), pk("v_"), "adamw_small")
    sd, sm, sv = _unpack_small(sd, arrs), _unpack_small(sm, arrs), _unpack_small(sv, arrs)
    for n in _SMALL:
        grads[n], deltas[n], new_m[n], new_v[n] = g_small[n], sd[n], sm[n], sv[n]
    for group in (grads, deltas, new_m, new_v):
        group["w_in"] = jnp.swapaxes(group["w_in"], 1, 2)

    return (g_small["_loss"], dx.reshape(b, t, d), *[grads[n] for n in _WEIGHTS], *[deltas[n] for n in _WEIGHTS],
            *[new_m[n] for n in _WEIGHTS], *[new_v[n] for n in _WEIGHTS])
```

```python
import functools
import math

import jax
import jax.numpy as jnp
from jax import lax
from jax.experimental import pallas as pl
from jax.experimental.pallas import tpu as pltpu

F32, BF16 = jnp.float32, jnp.bfloat16
MESH = pl.DeviceIdType.MESH
ANY = pl.BlockSpec(memory_space=pl.ANY)
VMEM = pl.BlockSpec(memory_space=pltpu.VMEM)
BS = pl.BlockSpec
SDS = jax.ShapeDtypeStruct

NORM_EPS = 1e-6
GN_EPS = 64e-5
L2_EPS = 1e-12
HEADS = 8
HEAD_DIM = 64
D_RWKV = HEADS * HEAD_DIM
ROPE_DIM = 32
ROPE_THETA = 10000.0
MLA_SCALE = (64 + ROPE_DIM) ** -0.5
Q_RANK, KV_RANK = 768, 256
RWKV_COLS = 1920
MLA_PAD_COLS = Q_RANK + KV_RANK + 256
D_FF = 2816
ADAM_LR, ADAM_B1, ADAM_B2, ADAM_EPS, ADAM_WD, ADAM_STEP = 0.001, 0.9, 0.999, 1e-08, 0.01, 10

V7X_LANES = 128
V7X_VMEM_LIMIT = 56 * 1024 * 1024
SCAN_CHUNK = 32
N_CHIPS = 4


def _cp(*sem):
    return pltpu.CompilerParams(dimension_semantics=sem, vmem_limit_bytes=V7X_VMEM_LIMIT)


def _tile(n, cands=(512, 640, 384, 256, 128)):
    for c in cands:
        if n % c == 0:
            return c
    return n


def _row_tile(n, cap=256):
    best = n
    for t in range(8, cap + 1, 8):
        if n % t == 0:
            best = t
    return best if best <= cap or n <= cap else n


def _rms(x, g):
    ms = jnp.mean(x * x, axis=-1, keepdims=True)
    return x * lax.rsqrt(ms + NORM_EPS) * g


@jax.custom_vjp
def _bdot(x, w):
    return jnp.dot(x.astype(BF16), w.astype(BF16), preferred_element_type=F32)


def _bdot_fwd(x, w):
    return _bdot(x, w), (x, w)


def _bdot_bwd(res, ct):
    x, w = res
    c = ct.astype(BF16)
    dx = lax.dot_general(c, w.astype(BF16), (((1,), (1,)), ((), ())), preferred_element_type=F32)
    dw = lax.dot_general(x.astype(BF16), c, (((0,), (0,)), ((), ())), preferred_element_type=F32)
    return dx.astype(x.dtype), dw.astype(w.dtype)


_bdot.defvjp(_bdot_fwd, _bdot_bwd)


@jax.custom_vjp
def _headsum(x, ones_bd):
    hi = x.astype(BF16)
    mid = (x - hi.astype(F32)).astype(BF16)
    ob = ones_bd.astype(BF16)
    return jnp.dot(hi, ob, preferred_element_type=F32) + jnp.dot(mid, ob, preferred_element_type=F32)


def _headsum_fwd(x, ones_bd):
    return _headsum(x, ones_bd), ones_bd


def _headsum_bwd(ones_bd, ct):
    return _headsum(ct, ones_bd), jnp.zeros_like(ones_bd)


_headsum.defvjp(_headsum_fwd, _headsum_bwd)


def _prep_fn(zs, w0, w2, a0, a2, g2, k_k, k_a, ones_bd):
    k = zs[:, 512:1024]
    wd = zs[:, 1536:1664]
    ad = zs[:, 1664:1792]
    gd = zs[:, 1792:1920]
    logit = w0 + _bdot(jnp.tanh(wd), w2)
    w = jnp.exp(-math.exp(-0.5) * jax.nn.sigmoid(logit))
    a = jax.nn.sigmoid(a0 + _bdot(ad, a2))
    g = _bdot(jax.nn.sigmoid(gd), g2)
    kkr = k * k_k
    nrm = jnp.sqrt(_headsum(kkr * kkr, ones_bd))
    kk = kkr / jnp.maximum(nrm, L2_EPS)
    a_f, a_b = a[:, :512], a[:, 512:]
    kf = k * (1.0 + (a_f - 1.0) * k_a)
    kb = k * (1.0 + (a_b - 1.0) * k_a)
    return w[:, :512], w[:, 512:], kf, kb, kk, kk * a_f, kk * a_b, g


def _post_fn(y, r, kf, kb, v, g, r_k, ln_g, ln_b, ones_bd):
    mu =_headsum(y, ones_bd) * (1.0 / HEAD_DIM)
    yc = y - mu
    var = _headsum(yc * yc, ones_bd) * (1.0 / HEAD_DIM)
    yn = yc * lax.rsqrt(var + GN_EPS) * ln_g + ln_b
    bonus = _headsum(r * (kf + kb) * r_k, ones_bd) * v
    return (yn + bonus) * g


def _cat8(x):
    return jnp.concatenate([x] * HEADS, axis=1)


def _mla_fn(zm, cs, sn, gq, gkv, wq, wqr, wk, wv):
    cq = zm[:, :Q_RANK]
    ckv = zm[:, Q_RANK:Q_RANK + KV_RANK]
    kr = zm[:, Q_RANK + KV_RANK:Q_RANK + KV_RANK + 128]
    krr = zm[:, Q_RANK + KV_RANK + 128:]
    cqn = _rms(cq, gq)
    ckvn = _rms(ckv, gkv)
    q = (_bdot(cqn, wq) * _cat8(cs) + _bdot(cqn, wqr) * _cat8(sn)) * MLA_SCALE
    kro = kr * cs + krr * sn
    kfull = _bdot(ckvn, wk) + _cat8(kro)
    v = _bdot(ckvn, wv)
    return q, kfull, v


def _adamw_math(w, g, m, v):
    m2 = ADAM_B1 * m + (1.0 - ADAM_B1) * g
    v2 = ADAM_B2 * v + (1.0 - ADAM_B2) * (g * g)
    m_hat = m2 / (1.0 - ADAM_B1 ** ADAM_STEP)
    v_hat = v2 / (1.0 - ADAM_B2 ** ADAM_STEP)
    delta = -ADAM_LR * (m_hat / (jnp.sqrt(v_hat) + ADAM_EPS) + ADAM_WD * w)
    return delta, m2, v2


_DIMS = {"nn": (((1,), (0,)), ((), ())), "nt": (((1,), (1,)), ((), ())), "tn": (((0,), (0,)), ((), ()))}


def _mm(a, b, mode, name, out_dtype=F32, add=None, column_blocks=None):
    if mode == "nn":
        (m, k), (_, n) = a.shape, b.shape
    elif mode == "nt":
        (m, k), (n, _) = a.shape, b.shape
    else:
        (k, m), (_, n) = a.shape, b.shape
    big = (1024, 1408, 768, 640, 512, 384, 256, 128)
    tm, tn, tk = _tile(m, big), column_blocks or _tile(n, big), _tile(k, (512, 1408, 640, 384, 256, 128))
    nk = k // tk

    def body(a_ref, b_ref, *rest):
        if add is None:
            o_ref, acc_ref = rest
        else:
            add_ref, o_ref, acc_ref = rest
        kk = pl.program_id(2)

        @pl.when(kk == 0)
        def _():
            acc_ref[...] = jnp.zeros_like(acc_ref)

        acc_ref[...] += lax.dot_general(
            a_ref[...].astype(BF16), b_ref[...].astype(BF16), _DIMS[mode], preferred_element_type=F32
        )

        @pl.when(kk == nk - 1)
        def _():
            r = acc_ref[...]
            if add is not None:
                r = r + add_ref[...]
            o_ref[...] = r.astype(out_dtype).reshape(o_ref.shape)

    a_spec = BS((tk, tm), lambda i, j, kk: (kk, i)) if mode == "tn" else BS((tm, tk), lambda i, j, kk: (i, kk))
    b_spec = BS((tn, tk), lambda i, j, kk: (j, kk)) if mode == "nt" else BS((tk, tn), lambda i, j, kk: (kk, j))
    o_spec = BS((tm, tn), lambda i, j, kk: (i, j))
    ins, specs = [a, b], [a_spec, b_spec]
    if add is not None:
        ins.append(add)
        specs.append(o_spec)
    out_shape = SDS((m, n), out_dtype)
    if column_blocks:
        assert add is None
        o_spec, out_shape = BS((1, tm, tn), lambda i, j, kk: (j, i, 0)), SDS((n // tn, m, tn), out_dtype)
    return pl.pallas_call(
        body, grid=(m // tm, n // tn, nk), in_specs=specs, out_specs=o_spec, out_shape=out_shape,
        scratch_shapes=[pltpu.VMEM((tm, tn), F32)], compiler_params=_cp("parallel", "parallel", "arbitrary"), name=name,
    )(*ins)


def _rms_fwd(x, g, name):
    m, d = x.shape
    tr = _tile(m)

    def body(x_ref, g_ref, o_ref):
        o_ref[...] = _rms(x_ref[...], g_ref[...]).astype(BF16)

    return pl.pallas_call(
        body, grid=(m // tr,), in_specs=[BS((tr, d), lambda i: (i, 0)), BS((1, d), lambda i: (0, 0))],
        out_specs=BS((tr, d), lambda i: (i, 0)), out_shape=SDS((m, d), BF16), compiler_params=_cp("parallel"), name=name,
    )(x, g)


def _rms_bwd(x, g, dy, name, dres=None, dy_block=0):
    m, d = x.shape
    tr = _row_tile(m)

    def body(x_ref, g_ref, dy_ref, *rest):
        if dres is None:
            dx_ref, dg_ref = rest
        else:
            dres_ref, dx_ref, dg_ref = rest
        _, vjp = jax.vjp(_rms, x_ref[...], g_ref[...])
        dx, dg = vjp(dy_ref[...])
        if dres is not None:
            dx = dx + dres_ref[...]
        dx_ref[...] = dx

        @pl.when(pl.program_id(0) == 0)
        def _():
            dg_ref[...] = jnp.zeros_like(dg_ref)

        dg_ref[...] += dg

    row = BS((tr, d), lambda i: (i, 0))
    vec = BS((1, d), lambda i: (0, 0))
    ins, specs = [x, g, dy], [row, vec, BS((tr, d), lambda i: (i, dy_block))]
    if dres is not None:
        ins.append(dres)
        specs.append(row)
    return pl.pallas_call(
        body, grid=(m // tr,), in_specs=specs, out_specs=[row, vec], out_shape=[SDS((m, d), F32), SDS((1, d), F32)],
        compiler_params=_cp("arbitrary"), name=name,
    )(*ins)


def _final(h, g, tgt):
    m, d = h.shape
    tr = _row_tile(m)

    def loss_fn(hh, gg, tt):
        e = _rms(hh, gg) - tt
        return 0.5 * jnp.sum(e * e) * (1.0 / d)

    def body(h_ref, g_ref, t_ref, l_ref, dh_ref, dg_ref):
        val, (dh, dg) = jax.value_and_grad(loss_fn, argnums=(0, 1))(h_ref[...], g_ref[...], t_ref[...])
        dh_ref[...] = dh

        @pl.when(pl.program_id(0) == 0)
        def _():
            dg_ref[...] = jnp.zeros_like(dg_ref)
            l_ref[...] = jnp.zeros_like(l_ref)

        dg_ref[...] += dg
        l_ref[...] += jnp.full(l_ref.shape, val, F32)

    row = BS((tr, d), lambda i: (i, 0))
    vec = BS((1, d), lambda i: (0, 0))
    return pl.pallas_call(
        body, grid=(m // tr,), in_specs=[row, vec, row], out_specs=[BS((8, 128), lambda i: (0, 0)), row, vec],
        out_shape=[SDS((8, 128), F32), SDS((m, d), F32), SDS((1, d), F32)], compiler_params=_cp("arbitrary"), name="final_loss",
    )(h, g, tgt)


def _prev_next(z, t):
    row = lax.broadcasted_iota(jnp.int32, z.shape, 0)
    zp = jnp.where(row == 0, 0.0, pltpu.roll(z, 1, axis=0))
    zn = jnp.where(row == t - 1, 0.0, pltpu.roll(z, t - 1, axis=0))
    return zp, zn


def _shift_fwd(z3, mu_p, mu_n):
    b, t, c = z3.shape
    nc = c // 128

    def body(z_ref, mp_ref, mn_ref, o_ref):
        z = z_ref[0]
        zp, zn = _prev_next(z, t)
        o_ref[0] = z + mp_ref[...] * (zp - z) + mn_ref[...] * (zn - z)

    blk = BS((1, t, 128), lambda i, j: (i, 0, j))
    vec = BS((1, 128), lambda i, j: (0, j))
    return pl.pallas_call(
        body, grid=(b, nc), in_specs=[blk, vec, vec], out_specs=blk, out_shape=SDS((b, t, c), F32),
        compiler_params=_cp("parallel", "parallel"), name="shift_fwd",
    )(z3, mu_p, mu_n)


def _shift_bwd(dzs3, z3, mu_p, mu_n):
    b, t, c = z3.shape
    nc = c // 128

    def body(d_ref, z_ref, mp_ref, mn_ref, dz_ref, dmp_ref, dmn_ref):
        d, z = d_ref[0], z_ref[0]
        mp, mn = mp_ref[...], mn_ref[...]
        zp, zn = _prev_next(z, t)
        _, dp_next = _prev_next(d * mp, t)
        dn_prev, _ = _prev_next(d * mn, t)
        dz_ref[0] = (d * (1.0 - mp - mn) + dp_next + dn_prev).astype(BF16)

        @pl.when(pl.program_id(1) == 0)
        def _():
            dmp_ref[...] = jnp.zeros_like(dmp_ref)
            dmn_ref[...] = jnp.zeros_like(dmn_ref)

        dmp_ref[...] += jnp.sum(d * (zp - z), axis=0, keepdims=True)
        dmn_ref[...] += jnp.sum(d * (zn - z), axis=0, keepdims=True)

    blk = BS((1, t, 128), lambda j, i: (i, 0, j))
    vec = BS((1, 128), lambda j, i: (0, j))
    return pl.pallas_call(
        body, grid=(nc, b), in_specs=[blk, blk, vec, vec], out_specs=[blk, vec, vec],
        out_shape=[SDS((b, t, c), BF16), SDS((1, c), F32), SDS((1, c), F32)],
        compiler_params=_cp("parallel", "arbitrary"), name="shift_bwd",
    )(dzs3, z3, mu_p, mu_n)


def _const(shape):
    nd = len(shape)
    return BS(shape, lambda i: (0,) * nd)


def _prep_fwd(zs, p):
    m = zs.shape[0]
    tr = 256
    params = [p["w0"], p["w2"], p["a0"], p["a2"], p["g2"], p["k_k"], p["k_a"], p["ones_bd"]]

    def body(zs_ref, w0, w2, a0, a2, g2, kk_, ka_, bd, wf, wb, kf, kb, kk, kaf, kab, g):
        outs = _prep_fn(zs_ref[...], w0[...], w2[...], a0[...], a2[...], g2[...], kk_[...], ka_[...], bd[...])
        for ref, val in zip((wf, wb, kf, kb, kk, kaf, kab, g), outs):
            ref[...] = val

    row = BS((tr, 512), lambda i: (i, 0))
    return pl.pallas_call(
        body, grid=(m // tr,), in_specs=[BS((tr, RWKV_COLS), lambda i: (i, 0))] + [_const(q.shape) for q in params],
        out_specs=[row] * 8, out_shape=[SDS((m, 512), F32)] * 8, compiler_params=_cp("parallel"), name="rwkv_prep_fwd",
    )(zs, *params)


def _prep_bwd(zs, p, ct_rows):
    m = zs.shape[0]
    tr = 128
    params = [p["w0"], p["w2"], p["a0"], p["a2"], p["g2"], p["k_k"], p["k_a"]]
    names = ["dwf", "dwb", "dkf", "dkb", "dkk_f", "dkk_b", "dkaf", "dkab", "dr_f", "dr_b", "dr_p", "dk_p", "dv_p", "dg",
             "dv_f", "dv_b"]
    rows = [ct_rows[n] for n in names]

    def body(zs_ref, w0, w2, a0, a2, g2, kk_, ka_, bd, *rest):
        c = {n: r[...] for n, r in zip(names, rest[:len(names)])}
        outs = rest[len(names):]
        dzs_ref, grads = outs[0], outs[1:]
        ones_bd = bd[...]
        _, vjp = jax.vjp(
            lambda *q: _prep_fn(*q, ones_bd), zs_ref[...], w0[...], w2[...], a0[...], a2[...], g2[...], kk_[...], ka_[...]
        )
        cts = (c["dwf"], c["dwb"], c["dkf"] + c["dk_p"], c["dkb"] + c["dk_p"], c["dkk_f"] + c["dkk_b"], c["dkaf"], c["dkab"], c["dg"])
        dzs, *dparams = vjp(cts)
        dr = c["dr_f"] + c["dr_b"] + c["dr_p"]
        dv = c["dv_f"] + c["dv_b"] + c["dv_p"]
        dzs_ref[:, 0:512] = dzs[:, 0:512] + dr
        dzs_ref[:, 512:1024] = dzs[:, 512:1024]
        dzs_ref[:, 1024:1536] = dzs[:, 1024:1536] + dv
        dzs_ref[:, 1536:1920] = dzs[:, 1536:1920]

        @pl.when(pl.program_id(0) == 0)
        def _():
            for gr in grads:
                gr[...] = jnp.zeros_like(gr)

        for gr, val in zip(grads, dparams):
            gr[...] += val

    row = BS((tr, 512), lambda i: (i, 0))
    return pl.pallas_call(
        body, grid=(m // tr,),
        in_specs=[BS((tr, RWKV_COLS), lambda i: (i, 0))] + [_const(q.shape) for q in params] + [_const(p["ones_bd"].shape)]
        + [row] * len(names),
        out_specs=[BS((tr, RWKV_COLS), lambda i: (i, 0))] + [_const(q.shape) for q in params],
        out_shape=[SDS((m, RWKV_COLS), F32)] + [SDS(q.shape, F32) for q in params],
        compiler_params=_cp("arbitrary"), name="rwkv_prep_bwd",
    )(zs, *params, p["ones_bd"], *rows)


def _post_specs(tr):
    r = BS((tr, 512), lambda i: (i, 0))
    v = BS((tr, 512), lambda i: (i, 2))
    row = BS((tr, 512), lambda i: (i, 0))
    return r, v, row


def _post_fwd(y_f, y_b, zs, kf, kb, g, p):
    m = zs.shape[0]
    tr = 256
    r, v, row = _post_specs(tr)
    vecs = [p["r_k"], p["ln_x_g"], p["ln_x_b"], p["ones_bd"]]

    def body(yf, yb, r_ref, v_ref, kf_ref, kb_ref, g_ref, rk, lg, lb, bd, o_ref):
        o_ref[...] = _post_fn(
            yf[...] + yb[...], r_ref[...], kf_ref[...], kb_ref[...], v_ref[...], g_ref[...], rk[...], lg[...], lb[...], bd[...]
        ).astype(BF16)

    return pl.pallas_call(
        body, grid=(m // tr,), in_specs=[row, row, r, v, row, row, row] + [_const(q.shape) for q in vecs],
        out_specs=row, out_shape=SDS((m, 512), BF16), compiler_params=_cp("parallel"), name="rwkv_post_fwd",
    )(y_f, y_b, zs, zs, kf, kb, g, *vecs)


def _post_bwd(y_f, y_b, zs, kf, kb, g, p, dymix):
    m = zs.shape[0]
    tr = 128
    r, v, row = _post_specs(tr)
    vecs = [p["r_k"], p["ln_x_g"], p["ln_x_b"]]

    def body(yf, yb, r_ref, v_ref, kf_ref, kb_ref, g_ref, rk, lg, lb, bd, dy_ref, dyo, dr, dk, dv, dg, drk, dlg, dlb):
        ones_bd = bd[...]
        _, vjp = jax.vjp(
            lambda *q: _post_fn(*q, ones_bd),
            yf[...] + yb[...], r_ref[...], kf_ref[...], kb_ref[...], v_ref[...], g_ref[...], rk[...], lg[...], lb[...],
        )
        c_y, c_r, c_kf, _, c_v, c_g, c_rk, c_lg, c_lb = vjp(dy_ref[...])
        dyo[...] = c_y
        dr[...] = c_r
        dk[...] = c_kf
        dv[...] = c_v
        dg[...] = c_g

        @pl.when(pl.program_id(0) == 0)
        def _():
            for ref in (drk, dlg, dlb):
                ref[...] = jnp.zeros_like(ref)

        drk[...] += c_rk
        dlg[...] += c_lg
        dlb[...] += c_lb

    vec = _const((1, 512))
    return pl.pallas_call(
        body, grid=(m // tr,),
        in_specs=[row, row, r, v, row, row, row] + [_const(q.shape) for q in vecs] + [_const(p["ones_bd"].shape), row],
        out_specs=[row, row, row, row, row, vec, vec, vec],
        out_shape=[SDS((m, 512), F32)] * 5 + [SDS((1, 512), F32)] * 3,
        compiler_params=_cp("arbitrary"), name="rwkv_post_bwd",
    )(y_f, y_b, zs, zs, kf, kb, g, *vecs, p["ones_bd"], dymix)


SCAN_MXU_GROUPS = 2


def _half_ones():
    ri = lax.broadcasted_iota(jnp.int32, (128, 128), 0)
    ci = lax.broadcasted_iota(jnp.int32, (128, 128), 1)
    return jnp.where((ri < 64) == (ci < 64), 1.0, 0.0).astype(BF16)


def _half_sums(xs, ones):
    out = []
    per = -(-len(xs) // SCAN_MXU_GROUPS)
    for g in range(0, len(xs), per):
        part = xs[g:g + per]
        res = jnp.dot(jnp.concatenate(part, axis=0).astype(BF16), ones, preferred_element_type=F32)
        out += [res[64 * i:64 * i + 64] for i in range(len(part))]
    return out


def _scan_specs(b, t):
    nc = t // SCAN_CHUNK
    up, down = (lambda c: c), (lambda c: nc - 1 - c)
    rows = [BS((b, SCAN_CHUNK, 512), lambda c, ci=ci: (0, ci(c), 0)) for ci in (up, down)]
    vrows = [BS((b, SCAN_CHUNK, 512), lambda c, ci=ci: (0, ci(c), 2)) for ci in (up, down)]
    hist = [BS((SCAN_CHUNK, b * 4, 64, 128), lambda c, ci=ci: (ci(c), 0, 0, 0)) for ci in (up, down)]
    return nc, rows, vrows, hist


class _Window:
    def __init__(self, g, ascending):
        self.bases = [pl.multiple_of(g * 8, 8) if asc else pl.multiple_of(SCAN_CHUNK - 8 - g * 8, 8) for asc in ascending]
        self.ascending = ascending
        self.blocks = {}
        self.row_id = lax.broadcasted_iota(jnp.int32, (8, 128), 0)

    def j(self, d, s):
        return s if self.ascending[d] else 7 - s

    def time(self, d, s):
        return self.bases[d] + self.j(d, s)

    def row(self, ref, d, bi, cols, s):
        key = (id(ref), d, bi, cols.start)
        if key not in self.blocks:
            self.blocks[key] = ref[bi, pl.ds(self.bases[d], 8), cols]
        jj = self.j(d, s)
        return self.blocks[key][jj:jj + 1, :]

    def put(self, buf, key, d, s, row):
        prev = buf.get(key)
        new = jnp.broadcast_to(row, (8, 128))
        buf[key] = new if prev is None else jnp.where(self.row_id == self.j(d, s), new, prev)

    def flush(self, buf, refs_of):
        for key, val in buf.items():
            ref, d, bi, cols = refs_of(key)
            ref[bi, pl.ds(self.bases[d], 8), cols] = val


def _pairs(b):
    return [(bi * 4 + p, bi, slice(128 * p, 128 * p + 128)) for bi in range(b) for p in range(4)]


def _colsum(x):
    return jnp.sum(x, axis=0, keepdims=True)


def _pair_matvec(row, mat):
    rid = lax.broadcasted_iota(jnp.int32, (8, 64), 0)
    lhs = jnp.where(rid == 0, row[:, :64], jnp.where(rid == 1, row[:, 64:], 0.0))
    out = jnp.dot(lhs.astype(BF16), mat.astype(BF16), preferred_element_type=F32)
    lo = lax.broadcasted_iota(jnp.int32, (1, 128), 1) < 64
    return jnp.where(lo, out[0:1], out[1:2])


def _eye_mask():
    return (lax.broadcasted_iota(jnp.int32, (64, 128), 1) & 63) == lax.broadcasted_iota(jnp.int32, (64, 128), 0)


def _scan_fwd(zs, kk, ops_f, ops_b, shard):
    b, t = zs.shape[:2]
    nc, rows, vrows, hist = _scan_specs(b, t)
    npair = b * 4

    def body(*refs):
        ins, shard_ref, outs, s_ref = refs[:12], refs[12], refs[13:17], refs[17]
        gather = (shard_ref, *refs[18:21])
        dirs = [dict(zip(("r", "kk", "v", "w", "k", "ka", "y", "h"), (*ins[6 * d:6 * d + 6], *outs[2 * d:2 * d + 2])))
                for d in (0, 1)]

        @pl.when(pl.program_id(0) == 0)
        def _():
            s_ref[...] = jnp.zeros_like(s_ref)
            _gather_halves(*gather, "start")

        @pl.when(pl.program_id(0) == nc - 1)
        def _():
            _gather_halves(*gather, "finish")

        ones, eye = _half_ones(), _eye_mask()
        chains = [(d, pr, bi, cols) for d in (0, 1) for pr, bi, cols in _pairs(b)]

        def eight_steps(g, carry):
            win = _Window(g, (True, False))
            ybuf = {}
            for s in range(8):
                s_prev, xa = [], []
                for d, pr, bi, cols in chains:
                    q = dirs[d]
                    st = s_ref[d * npair + pr]
                    q["h"][win.time(d, s), pr] = st
                    s_prev.append(st)
                    xa += [st * win.row(q["kk"], d, bi, cols, s), jnp.where(eye, win.row(q["v"], d, bi, cols, s), 0.0)]
                ra = _half_sums(xa, ones)
                xb = []
                for i, (d, pr, bi, cols) in enumerate(chains):
                    q = dirs[d]
                    s_new = s_prev[i] * win.row(q["w"], d, bi, cols, s) - ra[2 * i] * win.row(q["ka"], d, bi, cols, s) \
                        + ra[2 * i + 1] * win.row(q["k"], d, bi, cols, s)
                    s_ref[d * npair + pr] = s_new
                    xb.append(s_new * win.row(q["r"], d, bi, cols, s))
                rb = _half_sums(xb, ones)
                for i, (d, pr, bi, cols) in enumerate(chains):
                    win.put(ybuf, i, d, s, _colsum(jnp.where(eye, rb[i], 0.0)))
            win.flush(ybuf, lambda i: (dirs[chains[i][0]]["y"], chains[i][0], chains[i][2], chains[i][3]))
            return carry

        lax.fori_loop(0, SCAN_CHUNK // 8, eight_steps, 0)

    row_shape, hist_shape = SDS((b, t, 512), F32), SDS((t, npair, 64, 128), F32)
    state = (2 * npair, 64, 128)
    return pl.pallas_call(
        body, grid=(nc,), in_specs=sum(([rows[d], rows[d], vrows[d]] + [rows[d]] * 3 for d in (0, 1)), []) + [ANY],
        out_specs=[rows[0], hist[0], rows[1], hist[1], BS(state, lambda c: (0, 0, 0)), ANY],
        out_shape=[row_shape, hist_shape, row_shape, hist_shape, SDS(state, F32), _gathered_shape(shard)],
        scratch_shapes=_gather_sems(), compiler_params=_cp("arbitrary"), name="wkv_scan",
    )(zs, kk, zs, *ops_f, zs, kk, zs, *ops_b, shard)


def _scan_bwd(zs, kk, dy, ops_f, hist_f, ops_b, hist_b, s_last, partials):
    b, t = zs.shape[:2]
    nc, rows, vrows, hist = _scan_specs(b, t)
    npair = b * 4
    names_in = ("r", "kk", "v", "dy", "w", "k", "ka", "h")
    names_out = ("dr", "dw", "dk", "dkk", "dka", "dv")

    def body(*refs):
        ins, last_ref, part_ref, outs, recv_ref = refs[:16], refs[16], refs[17], refs[18:30], refs[30]
        ds_ref, after_ref = refs[31], refs[32]
        scatter = (part_ref, recv_ref, refs[33], refs[34])
        dirs = [dict(zip(names_in + names_out, (*ins[8 * d:8 * d + 8], *outs[6 * d:6 * d + 6]))) for d in (0, 1)]

        @pl.when(pl.program_id(0) == 0)
        def _():
            ds_ref[...] = jnp.zeros_like(ds_ref)
            after_ref[...] = last_ref[...]
            _scatter_to_all(*scatter, "start")

        @pl.when(pl.program_id(0) == nc - 1)
        def _():
            _scatter_to_all(*scatter, "finish")

        ones, eye = _half_ones(), _eye_mask()
        chains = [(d, pr, bi, cols) for d in (0, 1) for pr, bi, cols in _pairs(b)]

        def eight_steps(g, carry):
            win = _Window(g, (False, True))
            obuf = {}
            s_after = [after_ref[d * npair + pr] for d, pr, _, _ in chains]
            for s in range(8):
                row = lambda name, d, bi, cols: win.row(dirs[d][name], d, bi, cols, s)
                s_prev, xa = [], []
                for d, pr, bi, cols in chains:
                    st = dirs[d]["h"][win.time(d, s), pr]
                    s_prev.append(st)
                    xa += [st * row("kk", d, bi, cols), jnp.where(eye, row("dy", d, bi, cols), 0.0)]
                ra = _half_sums(xa, ones)
                ds_now, xb = [], []
                for i, (d, pr, bi, cols) in enumerate(chains):
                    skk, dycol = ra[2 * i], ra[2 * i + 1]
                    ds = ds_ref[d * npair + pr] + dycol * row("r", d, bi, cols)
                    win.put(obuf, (i, "dr"), d, s, _pair_matvec(row("dy", d, bi, cols), s_after[i]))
                    win.put(obuf, (i, "dk"), d, s, _pair_matvec(row("v", d, bi, cols), ds))
                    win.put(obuf, (i, "dka"), d, s, -_colsum(ds * skk))
                    win.put(obuf, (i, "dw"), d, s, _colsum(ds * s_prev[i]))
                    ds_now.append(ds)
                    xb += [ds * row("k", d, bi, cols), ds * row("ka", d, bi, cols)]
                rb = _half_sums(xb, ones)
                for i, (d, pr, bi, cols) in enumerate(chains):
                    dskk_neg = rb[2 * i + 1]
                    win.put(obuf, (i, "dv"), d, s, _colsum(jnp.where(eye, rb[2 * i], 0.0)))
                    win.put(obuf, (i, "dkk"), d, s, -_colsum(s_prev[i] * dskk_neg))
                    ds_ref[d * npair + pr] = ds_now[i] * row("w", d, bi, cols) - dskk_neg * row("kk", d, bi, cols)
                s_after = s_prev
            for i, (d, pr, _, _) in enumerate(chains):
                after_ref[d * npair + pr] = s_after[i]
            win.flush(obuf, lambda key: (dirs[chains[key[0]][0]][key[1]], chains[key[0]][0], chains[key[0]][2], chains[key[0]][3]))
            return carry

        lax.fori_loop(0, SCAN_CHUNK // 8, eight_steps, 0)

    row_shape = SDS((b, t, 512), F32)
    state = (2 * npair, 64, 128)
    return pl.pallas_call(
        body, grid=(nc,),
        in_specs=sum(([rows[d], rows[d], vrows[d]] + [rows[d]] * 4 + [hist[d]] for d in (1, 0)), [])
        + [BS(state, lambda c: (0, 0, 0)), ANY],
        out_specs=[rows[1]] * 6 + [rows[0]] * 6 + [ANY],
        out_shape=[row_shape] * 12 + [SDS((7, partials.shape[1] // 2, 128), partials.dtype)],
        scratch_shapes=[pltpu.VMEM(state, F32), pltpu.VMEM(state, F32), pltpu.SemaphoreType.DMA((7,)),
                        pltpu.SemaphoreType.DMA((7,))],
        compiler_params=_cp("arbitrary"), name="wkv_scan_bwd",
    )(zs, kk, zs, dy, *ops_f, hist_f, zs, kk, zs, dy, *ops_b, hist_b, s_last, partials)


def _mla_fwd(zm, cs, sn, p, t):
    m = zm.shape[0]
    tr = 256
    per = t // tr
    params = [p["q_norm_g"], p["kv_norm_g"], p["wq"], p["wqr"], p["wk"], p["wv"]]

    def body(z_ref, cs_ref, sn_ref, gq, gkv, wq, wqr, wk, wv, q_ref, k_ref, v_ref):
        q, kf, v = _mla_fn(z_ref[...], cs_ref[...], sn_ref[...], gq[...], gkv[...], wq[...], wqr[...], wk[...], wv[...])
        q_ref[...] = q.astype(BF16)
        k_ref[...] = kf.astype(BF16)
        v_ref[...] = v.astype(BF16)

    tab = BS((tr, 128), lambda i: (i % per, 0))
    return pl.pallas_call(
        body, grid=(m // tr,), in_specs=[BS((tr, MLA_PAD_COLS), lambda i: (i, 0)), tab, tab] + [_const(q.shape) for q in params],
        out_specs=[BS((tr, 1024), lambda i: (i, 0)), BS((tr, 1024), lambda i: (i, 0)), BS((tr, 512), lambda i: (i, 0))],
        out_shape=[SDS((m, 1024), BF16), SDS((m, 1024), BF16), SDS((m, 512), BF16)], compiler_params=_cp("parallel"), name="mla_prep_fwd",
    )(zm, cs, sn, *params)


def _mla_bwd(zm, cs, sn, p, t, dq, dk, dv):
    m = zm.shape[0]
    tr = 128
    per = t // tr
    params = [p["q_norm_g"], p["kv_norm_g"], p["wq"], p["wqr"], p["wk"], p["wv"]]

    def body(z_ref, cs_ref, sn_ref, gq, gkv, wq, wqr, wk, wv, dq_ref, dk_ref, dv_ref, dz_ref, *grads):
        cs_v, sn_v = cs_ref[...], sn_ref[...]
        _, vjp = jax.vjp(
            lambda *q: _mla_fn(q[0], cs_v, sn_v, *q[1:]), z_ref[...], gq[...], gkv[...], wq[...], wqr[...], wk[...], wv[...]
        )
        dz, *dparams = vjp((dq_ref[...], dk_ref[...], dv_ref[...]))
        dz_ref[...] = dz.astype(BF16)

        @pl.when(pl.program_id(0) == 0)
        def _():
            for gr in grads:
                gr[...] = jnp.zeros_like(gr)

        for gr, val in zip(grads, dparams):
            gr[...] += val

    tab = BS((tr, 128), lambda i: (i % per, 0))
    wide = BS((tr, 1024), lambda i: (i, 0))
    return pl.pallas_call(
        body, grid=(m // tr,),
        in_specs=[BS((tr, MLA_PAD_COLS), lambda i: (i, 0)), tab, tab] + [_const(q.shape) for q in params]
        + [wide, wide, BS((tr, 512), lambda i: (i, 0))],
        out_specs=[BS((tr, MLA_PAD_COLS), lambda i: (i, 0))] + [_const(q.shape) for q in params],
        out_shape=[SDS((m, MLA_PAD_COLS), BF16)] + [SDS(q.shape, F32) for q in params],
        compiler_params=_cp("arbitrary"), name="mla_prep_bwd",
    )(zm, cs, sn, *params, dq, dk, dv)


_NT = (((1,), (1,)), ((), ()))
_TN = (((0,), (0,)), ((), ()))


def _attn_fwd(q, kf, v, b, t):
    m = q.shape[0]
    tq = 256
    nq = t // tq

    def body(q_ref, k_ref, v_ref, o_ref, l_ref):
        lo = lax.broadcasted_iota(jnp.int32, (1, 128), 1) < 64
        v_all = v_ref[...]
        o = jnp.zeros((tq, 128), F32)
        lse = []
        for h in range(2):
            hs = slice(128 * h, 128 * h + 128)
            s = lax.dot_general(q_ref[:, hs], k_ref[:, hs], _NT, preferred_element_type=F32)
            mx = jnp.max(s, axis=1, keepdims=True)
            e = jnp.exp(s - mx)
            den = jnp.sum(e, axis=1, keepdims=True)
            vh = jnp.where(lo if h == 0 else jnp.logical_not(lo), v_all, jnp.zeros_like(v_all))
            o = o + jnp.dot(e.astype(BF16), vh, preferred_element_type=F32) / den
            lse.append(mx + jnp.log(den))
        o_ref[...] = o
        l_ref[...] = jnp.where(lo, lse[0], lse[1])

    return pl.pallas_call(
        body, grid=(b, 4, nq),
        in_specs=[BS((tq, 256), lambda bi, hp, i: (bi * nq + i, hp)), BS((t, 256), lambda bi, hp, i: (bi, hp)),
                  BS((t, 128), lambda bi, hp, i: (bi, hp))],
        out_specs=[BS((tq, 128), lambda bi, hp, i: (bi * nq + i, hp))] * 2,
        out_shape=[SDS((m, 512), F32), SDS((m, 512), F32)], compiler_params=_cp("parallel", "parallel", "arbitrary"), name="attn_fwd",
    )(q, kf, v)


def _attn_bwd(q, kf, v, o, lse, do, b, t):
    m = q.shape[0]
    tq = 256
    nq = t // tq

    def body(q_ref, k_ref, v_ref, o_ref, l_ref, do_ref, dq_ref, dk_ref, dv_ref):
        lo = lax.broadcasted_iota(jnp.int32, (1, 128), 1) < 64

        @pl.when(pl.program_id(2) == 0)
        def _():
            dk_ref[...] = jnp.zeros_like(dk_ref)
            dv_ref[...] = jnp.zeros_like(dv_ref)

        v_all, o_all, l_all, do_all = v_ref[...], o_ref[...], l_ref[...], do_ref[...]
        dv_acc = jnp.zeros((t, 128), F32)
        for h in range(2):
            hs = slice(128 * h, 128 * h + 128)
            mask = lo if h == 0 else jnp.logical_not(lo)
            qh, kh = q_ref[:, hs], k_ref[:, hs]
            s = lax.dot_general(qh, kh, _NT, preferred_element_type=F32)
            lse_h = jnp.max(jnp.where(mask, l_all, -jnp.inf), axis=1, keepdims=True)
            pr = jnp.exp(s - lse_h)
            do_h = jnp.where(mask, do_all, 0.0)
            dp = lax.dot_general(do_h.astype(BF16), v_all, _NT, preferred_element_type=F32)
            dsum = jnp.sum(do_h * o_all, axis=1, keepdims=True)
            ds = (pr * (dp - dsum)).astype(BF16)
            dq_ref[:, hs] = jnp.dot(ds, kh, preferred_element_type=F32)
            dk_ref[:, hs] += lax.dot_general(ds, qh, _TN, preferred_element_type=F32)
            dv_acc = dv_acc + lax.dot_general(pr.astype(BF16), do_h.astype(BF16), _TN, preferred_element_type=F32)
        dv_ref[...] += dv_acc

    qspec = BS((tq, 256), lambda bi, hp, i: (bi * nq + i, hp))
    kspec = BS((t, 256), lambda bi, hp, i: (bi, hp))
    vspec = BS((t, 128), lambda bi, hp, i: (bi, hp))
    ospec = BS((tq, 128), lambda bi, hp, i: (bi * nq + i, hp))
    return pl.pallas_call(
        body, grid=(b, 4, nq), in_specs=[qspec, kspec, vspec, ospec, ospec, ospec], out_specs=[qspec, kspec, vspec],
        out_shape=[SDS((m, 1024), F32), SDS((m, 1024), F32), SDS((m, 512), F32)],
        compiler_params=_cp("parallel", "parallel", "arbitrary"), name="attn_bwd",
    )(q, kf, v, o, lse, do)


def _conv3(u, w_ref, b_ref, t):
    up, un = _prev_next(u, t)
    return w_ref[0:1, :] * up + w_ref[1:2, :] * u + w_ref[2:3, :] * un + b_ref[...], up, un


def _ffn_mid_fwd(ug3, uv3, cw, cb):
    b, t, f = ug3.shape
    nc = f // 256

    def body(ug_ref, uv_ref, wg_ref, wv_ref, bg_ref, bv_ref, a_ref):
        gc, _, _ = _conv3(ug_ref[0], wg_ref, bg_ref, t)
        vc, _, _ = _conv3(uv_ref[0], wv_ref, bv_ref, t)
        a_ref[0] = (gc * jax.nn.sigmoid(gc) * vc).astype(BF16)

    blk = BS((1, t, 256), lambda i, j: (i, 0, j))
    return pl.pallas_call(
        body, grid=(b, nc),
        in_specs=[blk, blk, BS((3, 256), lambda i, j: (0, j)), BS((3, 256), lambda i, j: (0, j + nc)),
                  BS((1, 256), lambda i, j: (0, j)), BS((1, 256), lambda i, j: (0, j + nc))],
        out_specs=blk, out_shape=SDS((b, t, f), BF16), compiler_params=_cp("parallel", "parallel"), name="ffn_mid_fwd",
    )(ug3, uv3, cw, cw, cb, cb)


def _ffn_mid_bwd(ug3, uv3, cw, cb, da3):
    b, t, f = ug3.shape
    nc = f // 256

    def half(u, up, un, dc, w_ref):
        dprev, dnext = _prev_next(dc, t)
        du = w_ref[1:2, :] * dc + w_ref[0:1, :] * dnext + w_ref[2:3, :] * dprev
        sums = [jnp.sum(dc * q, axis=0, keepdims=True) for q in (up, u, un)] + [jnp.sum(dc, axis=0, keepdims=True)]
        row = lax.broadcasted_iota(jnp.int32, (8, 256), 0)
        tab = jnp.zeros((8, 256), F32)
        for i, s in enumerate(sums):
            tab = jnp.where(row == i, s, tab)
        return du, tab

    def body(ug_ref, uv_ref, wg_ref, wv_ref, bg_ref, bv_ref, da_ref, dug_ref, duv_ref, tg_ref, tv_ref):
        ug, uv, da = ug_ref[0], uv_ref[0], da_ref[0]
        gc, gp, gn = _conv3(ug, wg_ref, bg_ref, t)
        vc, vp, vn = _conv3(uv, wv_ref, bv_ref, t)
        sg = jax.nn.sigmoid(gc)
        d_gc = da * vc * (sg * (1.0 + gc * (1.0 - sg)))
        d_vc = da * (gc * sg)
        dug, tg = half(ug, gp, gn, d_gc, wg_ref)
        duv, tv = half(uv, vp, vn, d_vc, wv_ref)
        dug_ref[0] = dug.astype(BF16)
        duv_ref[0] = duv.astype(BF16)

        @pl.when(pl.program_id(1) == 0)
        def _():
            tg_ref[...] = jnp.zeros_like(tg_ref)
            tv_ref[...] = jnp.zeros_like(tv_ref)

        tg_ref[...] += tg
        tv_ref[...] += tv

    blk = BS((1, t, 256), lambda j, i: (i, 0, j))
    tab = BS((8, 256), lambda j, i: (0, j))
    return pl.pallas_call(
        body, grid=(nc, b),
        in_specs=[blk, blk, BS((3, 256), lambda j, i: (0, j)), BS((3, 256), lambda j, i: (0, j + nc)),
                  BS((1, 256), lambda j, i: (0, j)), BS((1, 256), lambda j, i: (0, j + nc)), blk],
        out_specs=[blk, blk, tab, tab],
        out_shape=[SDS((b, t, f), BF16), SDS((b, t, f), BF16), SDS((8, f), F32), SDS((8, f), F32)],
        compiler_params=_cp("parallel", "arbitrary"), name="ffn_mid_bwd",
    )(ug3, uv3, cw, cw, cb, cb, da3)


def _add_rows(parts, name, out_dtype=F32):
    r = parts[0].shape[0]
    tr = _row_tile(r, 1024)
    n = len(parts)

    def body(*refs):
        acc = refs[0][...].astype(F32)
        for q in refs[1:n]:
            acc = acc + q[...].astype(F32)
        refs[n][...] = acc.astype(out_dtype)

    row = BS((tr, 128), lambda i: (i, 0))
    return pl.pallas_call(
        body, grid=(r // tr,), in_specs=[row] * n, out_specs=row, out_shape=SDS((r, 128), out_dtype),
        compiler_params=_cp("parallel"), name=name,
    )(*parts)


def _adamw(w, g, m, v, name):
    lead = w.shape[:-2]
    r, c = w.shape[-2:]
    tr = _row_tile(r)

    def body(w_ref, g_ref, m_ref, v_ref, d_ref, m2_ref, v2_ref):
        d, m2, v2 = _adamw_math(w_ref[...], g_ref[...], m_ref[...], v_ref[...])
        d_ref[...] = d
        m2_ref[...] = m2
        v2_ref[...] = v2

    blk = BS((1,) * len(lead) + (tr, c), lambda i: (0,) * len(lead) + (i, 0))
    return pl.pallas_call(
        body, grid=(r // tr,), in_specs=[blk] * 4, out_specs=[blk] * 3, out_shape=[SDS(w.shape, F32)] * 3,
        compiler_params=_cp("parallel"), name=name,
    )(w, g, m, v)


def _place():
    return lax.axis_index("x"), lax.axis_index("y"), lax.axis_index("c")


def _flip(v, bit):
    return 1 - v if bit else v


def _allgather_weights(shard):
    def body(x_ref, out_ref, send_sems, recv_sems):
        _gather_halves(x_ref, out_ref, send_sems, recv_sems, "start")
        _gather_halves(x_ref, out_ref, send_sems, recv_sems, "finish")

    return pl.pallas_call(
        body, out_shape=_gathered_shape(shard), in_specs=[ANY], out_specs=ANY, scratch_shapes=_gather_sems(),
        name="allgather_weights",
    )(shard)


def _gathered_shape(shard):
    return SDS((8 * (shard.shape[0] // 2), 128), shard.dtype)


def _gather_sems():
    return [pltpu.SemaphoreType.DMA((6,)), pltpu.SemaphoreType.DMA((6,))]


def _gather_halves(x_ref, out_ref, send_sems, recv_sems, phase):
    rh = x_ref.shape[0] // 2
    x, y, c = _place()
    me, sibling = (x, y, c), (x, y, 1 - c)
    chips = [(1 - x, y), (x, 1 - y), (1 - x, 1 - y)]
    mine_src = x_ref.at[pl.ds(c * rh, rh), :]

    def rows(px, py, pc):
        return out_ref.at[pl.ds((4 * px + 2 * py + pc) * rh, rh), :]

    def copy(k, block, to, src=None):
        return pltpu.make_async_remote_copy(
            src_ref=rows(*block) if src is None else src, dst_ref=rows(*block), send_sem=send_sems.at[k],
            recv_sem=recv_sems.at[k], device_id=to, device_id_type=MESH,
        )

    first = [copy(j, me, (*chip, c), src=mine_src) for j, chip in enumerate(chips)]
    if phase == "start":
        for cp in first:
            cp.start()
        return
    passed = [copy(3 + j, (*chip, c), sibling) for j, chip in enumerate(chips)]
    for j, chip in enumerate(chips):
        copy(j, (*chip, c), me).wait_recv()
        passed[j].start()
    for j, chip in enumerate(chips):
        copy(3 + j, (*chip, 1 - c), me).wait_recv()
    for cp in first + passed:
        cp.wait_send()


def _scatter_partials(g_ref, recv_ref, send_sems, recv_sems, phase):
    x, y, c = _place()
    copies = []
    for j, (fx, fy) in enumerate(((1, 0), (0, 1), (1, 1))):
        px, py = _flip(x, fx), _flip(y, fy)
        copies.append(pltpu.make_async_remote_copy(
            src_ref=g_ref.at[2 * px + py], dst_ref=recv_ref.at[j], send_sem=send_sems.at[j], recv_sem=recv_sems.at[j],
            device_id=(px, py, c), device_id_type=MESH,
        ))
    if phase == "start":
        for cp in copies:
            cp.start()
        return
    for cp in copies:
        cp.wait_recv()
    for cp in copies:
        cp.wait_send()


def _scatter_sems():
    return [pltpu.SemaphoreType.DMA((3,)), pltpu.SemaphoreType.DMA((3,))]


def _scatter_to_all(g_ref, recv_ref, send_sems, recv_sems, phase):
    rh = g_ref.shape[1] // 2
    x, y, c = _place()
    copies = []
    for k in range(1, 8):
        px, py, pc = _flip(x, k >> 2 & 1), _flip(y, k >> 1 & 1), _flip(c, k & 1)
        copies.append(pltpu.make_async_remote_copy(
            src_ref=g_ref.at[2 * px + py, pl.ds(pc * rh, rh), :], dst_ref=recv_ref.at[k - 1], send_sem=send_sems.at[k - 1],
            recv_sem=recv_sems.at[k - 1], device_id=(px, py, pc), device_id_type=MESH,
        ))
    if phase == "start":
        for cp in copies:
            cp.start()
        return
    for cp in copies:
        cp.wait_recv()
    for cp in copies:
        cp.wait_send()


def _scatter_to_chips(g):
    def body(g_ref, recv_ref, send_sems, recv_sems):
        _scatter_partials(g_ref, recv_ref, send_sems, recv_sems, "start")
        _scatter_partials(g_ref, recv_ref, send_sems, recv_sems, "finish")

    return pl.pallas_call(
        body, out_shape=SDS((3,) + g.shape[1:], g.dtype), in_specs=[ANY], out_specs=ANY,
        scratch_shapes=[pltpu.SemaphoreType.DMA((3,)), pltpu.SemaphoreType.DMA((3,))], name="scatter_grads",
    )(g)


def _send_to_sibling(a, half_of_rows):
    rh = a.shape[1] // 2

    def body(a_ref, b_ref, send_sem, recv_sem):
        x, y, c = _place()
        src = a_ref.at[:, pl.ds((1 - c) * rh, rh), :] if half_of_rows else a_ref
        cp = pltpu.make_async_remote_copy(
            src_ref=src, dst_ref=b_ref, send_sem=send_sem, recv_sem=recv_sem, device_id=(x, y, 1 - c), device_id_type=MESH
        )
        cp.start()
        cp.wait()

    shape = (a.shape[0], rh, 128) if half_of_rows else a.shape
    return pl.pallas_call(
        body, out_shape=SDS(shape, a.dtype), in_specs=[ANY], out_specs=ANY,
        scratch_shapes=[pltpu.SemaphoreType.DMA, pltpu.SemaphoreType.DMA],
        name="sibling_halves" if half_of_rows else "sibling_swap",
    )(a)


def _allreduce_small(v):
    r = v.shape[0]

    def body(v_ref, out_ref, buf_ref, send_sems, recv_sems):
        x, y, c = _place()
        buf_ref[0] = v_ref[...]
        copies = []
        for k in range(1, 8):
            peer = (_flip(x, k >> 2 & 1), _flip(y, k >> 1 & 1), _flip(c, k & 1))
            cp = pltpu.make_async_remote_copy(
                src_ref=v_ref, dst_ref=buf_ref.at[k], send_sem=send_sems.at[k - 1], recv_sem=recv_sems.at[k - 1],
                device_id=peer, device_id_type=MESH,
            )
            cp.start()
            copies.append(cp)
        for cp in copies:
            cp.wait_recv()
        acc = None
        for d in range(8):
            slot = 4 * _flip(x, d >> 2 & 1) + 2 * _flip(y, d >> 1 & 1) + _flip(c, d & 1)
            term = buf_ref[slot]
            acc = term if acc is None else acc + term
        out_ref[...] = acc
        for cp in copies:
            cp.wait_send()

    return pl.pallas_call(
        body, out_shape=SDS(v.shape, F32), in_specs=[VMEM], out_specs=VMEM,
        scratch_shapes=[pltpu.VMEM((8, r, 128), F32), pltpu.SemaphoreType.DMA((7,)), pltpu.SemaphoreType.DMA((7,))],
        name="allreduce_small",
    )(v)


_BIG_A = (
    ("w_in", 0, False),
    ("decay_w2_fwd", 1, False), ("decay_w2_bwd", 1, False), ("iclr_a2_fwd", 1, False),
    ("iclr_a2_bwd", 1, False), ("gate_g2", 1, False),
)
_BIG_B = (
    ("w_uq", 0, False), ("w_ukv", 1, False), ("w_out", 0, False), ("w_ffn_up", 1, False), ("ffn_conv_w", 1, True),
    ("w_ffn_down", 0, False),
)
_BIG = _BIG_A + _BIG_B
_SMALL = (
    "ln_mix_g", "shift_mu_prev", "shift_mu_next", "decay_w0_fwd", "decay_w0_bwd", "iclr_a0_fwd", "iclr_a0_bwd", "k_k",
    "k_a", "r_k", "ln_x_g", "ln_x_b", "q_norm_g", "kv_norm_g", "mla_out_g", "ln_ffn_g", "ffn_conv_b", "ln_final_g",
)
_WEIGHTS = (
    "ln_mix_g", "w_in", "shift_mu_prev", "shift_mu_next", "decay_w0_fwd", "decay_w2_fwd", "decay_w0_bwd", "decay_w2_bwd",
    "iclr_a0_fwd", "iclr_a2_fwd", "iclr_a0_bwd", "iclr_a2_bwd", "gate_g2", "k_k", "k_a", "r_k", "ln_x_g", "ln_x_b",
    "q_norm_g", "w_uq", "kv_norm_g", "w_ukv", "mla_out_g", "w_out", "ln_ffn_g", "w_ffn_up", "ffn_conv_w", "ffn_conv_b",
    "w_ffn_down", "ln_final_g",
)


def _pad_rows(flat, rows):
    return jnp.pad(flat, (0, rows * 128 - flat.shape[0])).reshape(rows, 128)


def _rows_for(n, mult):
    rows = -(-n // 128)
    return -(-rows // mult) * mult


def _pack_shards_bf16(arrs, entries):
    parts = []
    for name, _, raw in entries:
        w = arrs[name][0]
        flat = lax.bitcast_convert_type(w, BF16).reshape(-1) if raw else w.astype(BF16).reshape(-1)
        parts.append(_pad_rows(flat, _rows_for(flat.shape[0], 32)))
    return jnp.concatenate(parts, axis=0)


def _unpack_gathered(g4, arrs, entries):
    out, off = {}, 0
    for name, axis, raw in entries:
        a, b = arrs[name].shape[1:]
        n = a * b * (2 if raw else 1)
        rows = _rows_for(n, 32)
        seg = g4[:, off:off + rows].reshape(4, rows * 128)[:, :n]
        off += rows
        if raw:
            seg = lax.bitcast_convert_type(seg.reshape(4, a * b, 2), F32)
        seg = seg.reshape(4, a, b)
        out[name] = jnp.concatenate([seg[s] for s in range(4)], axis=1) if axis == 1 else seg.reshape(4 * a, b)
    return out


def _pack_grads(full, arrs, entries, dtype=F32):
    parts = []
    for name, axis, _ in entries:
        a, b = arrs[name].shape[1:]
        g = full[name]
        if g.ndim == 3:
            sh = g
        else:
            sh = g.reshape(a, 4, b).transpose(1, 0, 2) if axis == 1 else g.reshape(4, a, b)
        rows = _rows_for(a * b, 8)
        flat = sh.reshape(4, a * b).astype(dtype)
        parts.append(jnp.pad(flat, ((0, 0), (0, rows * 128 - a * b))).reshape(4, rows, 128))
    total = sum(q.shape[1] for q in parts)
    parts.append(jnp.zeros((4, -(-total // 1024) * 1024 - total, 128), dtype))
    return jnp.concatenate(parts, axis=1)


def _unpack_grads(g, arrs, entries):
    out, off = {}, 0
    for name, _, _ in entries:
        a, b = arrs[name].shape[1:]
        rows = _rows_for(a * b, 8)
        out[name] = g[off:off + rows].reshape(-1)[:a * b].reshape(1, a, b)
        off += rows
    return out


def _pack_small(vals):
    flat = jnp.concatenate([vals[n].reshape(-1).astype(F32) for n in _SMALL] + [vals["_loss"].reshape(-1)])
    return _pad_rows(flat, _rows_for(flat.shape[0], 8))


def _unpack_small(buf, arrs):
    flat, out, off = buf.reshape(-1), {}, 0
    for n in _SMALL:
        size = arrs[n].size
        out[n] = flat[off:off + size].reshape(arrs[n].shape)
        off += size
    out["_loss"] = flat[off]
    return out


def _rot_cols(w):
    return jnp.concatenate([-w[..., 16:], w[..., :16]], axis=-1)


def _rot_cols_t(g):
    return jnp.concatenate([g[..., 16:], -g[..., :16]], axis=-1)


def _rope_tables(t):
    inv = jnp.power(ROPE_THETA, -jnp.arange(0, ROPE_DIM, 2, dtype=F32) / ROPE_DIM)
    ang = jnp.arange(t, dtype=F32)[:, None] * inv[None, :]
    one, zero = jnp.ones((t, 64), F32), jnp.zeros((t, 64), F32)
    cs = jnp.concatenate([one, jnp.cos(ang), jnp.cos(ang), zero[:, :32]], axis=1)
    sn = jnp.concatenate([zero, jnp.sin(ang), jnp.sin(ang), zero[:, :32]], axis=1)
    return cs, sn


def _block_diag(a, b):
    za = jnp.zeros_like(a)
    return jnp.concatenate([jnp.concatenate([a, za], axis=1), jnp.concatenate([za, b], axis=1)], axis=0)


def kernel(x, ln_mix_g, w_in, shift_mu_prev, shift_mu_next, decay_w0_fwd, decay_w2_fwd, decay_w0_bwd, decay_w2_bwd, iclr_a0_fwd, iclr_a2_fwd, iclr_a0_bwd, iclr_a2_bwd, gate_g2, k_k, k_a, r_k, ln_x_g, ln_x_b, q_norm_g, w_uq, kv_norm_g, w_ukv, mla_out_g, w_out, ln_ffn_g, w_ffn_up, ffn_conv_w, ffn_conv_b, w_ffn_down, ln_final_g, loss_target, m_ln_mix_g, m_w_in, m_shift_mu_prev, m_shift_mu_next, m_decay_w0_fwd, m_decay_w2_fwd, m_decay_w0_bwd, m_decay_w2_bwd, m_iclr_a0_fwd, m_iclr_a2_fwd, m_iclr_a0_bwd, m_iclr_a2_bwd, m_gate_g2, m_k_k, m_k_a, m_r_k, m_ln_x_g, m_ln_x_b, m_q_norm_g, m_w_uq, m_kv_norm_g, m_w_ukv, m_mla_out_g, m_w_out, m_ln_ffn_g, m_w_ffn_up, m_ffn_conv_w, m_ffn_conv_b, m_w_ffn_down, m_ln_final_g, v_ln_mix_g, v_w_in, v_shift_mu_prev, v_shift_mu_next, v_decay_w0_fwd, v_decay_w2_fwd, v_decay_w0_bwd, v_decay_w2_bwd, v_iclr_a0_fwd, v_iclr_a2_fwd, v_iclr_a0_bwd, v_iclr_a2_bwd, v_gate_g2, v_k_k, v_k_a, v_r_k, v_ln_x_g, v_ln_x_b, v_q_norm_g, v_w_uq, v_kv_norm_g, v_w_ukv, v_mla_out_g, v_w_out, v_ln_ffn_g, v_w_ffn_up, v_ffn_conv_w, v_ffn_conv_b, v_w_ffn_down, v_ln_final_g):
    arrs = dict(locals())
    for pre in ("", "m_", "v_"):
        arrs[pre + "w_in"] = jnp.swapaxes(arrs[pre + "w_in"], 1, 2)
    b, t, d = x.shape
    m = b * t
    x2 = x.reshape(m, d)
    tgt = loss_target.reshape(m, d)
    vec = lambda n: arrs[n].reshape(1, -1)

    core = lax.axis_index("c")
    chip = 2 * lax.axis_index("x") + lax.axis_index("y")
    def unpack(gathered, shard, entries):
        g4 = lax.dynamic_update_index_in_dim(gathered.reshape(N_CHIPS, -1, 128), shard, chip, axis=0)
        return _unpack_gathered(g4, arrs, entries)

    shard_a, shard_b = _pack_shards_bf16(arrs, _BIG_A), _pack_shards_bf16(arrs, _BIG_B)
    fw = unpack(_allgather_weights(shard_a), shard_a, _BIG_A)
    win = fw["w_in"]
    zc = jnp.zeros((64, d), BF16)
    w_kr = win[2944:2976]
    rot_kr = jnp.swapaxes(_rot_cols(jnp.swapaxes(w_kr, 0, 1)), 0, 1)
    win_m = jnp.concatenate([win[1920:2944], zc, w_kr, zc[:32], zc, rot_kr, zc[:32]], axis=0)
    win_r = win[:RWKV_COLS]
    head = jnp.arange(512) // HEAD_DIM
    rw = dict(
        w0=jnp.concatenate([vec("decay_w0_fwd"), vec("decay_w0_bwd")], axis=1),
        w2=_block_diag(fw["decay_w2_fwd"], fw["decay_w2_bwd"]).astype(F32),
        a0=jnp.concatenate([vec("iclr_a0_fwd"), vec("iclr_a0_bwd")], axis=1),
        a2=_block_diag(fw["iclr_a2_fwd"], fw["iclr_a2_bwd"]).astype(F32),
        g2=fw["gate_g2"].astype(F32), k_k=vec("k_k"), k_a=vec("k_a"), r_k=vec("r_k"), ln_x_g=vec("ln_x_g"), ln_x_b=vec("ln_x_b"),
        ones_bd=(head[:, None] == head[None, :]).astype(F32),
    )
    cs, sn = _rope_tables(t)

    n1 = _rms_fwd(x2, vec("ln_mix_g"), "rms_mix")
    zm = _mm(n1, win_m, "nt", "proj_in_mla")
    zr = _mm(n1, win_r, "nt", "proj_in_rwkv")
    zs = _shift_fwd(zr.reshape(b, t, RWKV_COLS), vec("shift_mu_prev"), vec("shift_mu_next"))
    zs2 = zs.reshape(m, RWKV_COLS)
    wf, wb, kf, kb, kk, kaf, kab, gate = _prep_fwd(zs2, rw)
    r4 = lambda a: a.reshape(b, t, 512)
    f2 = lambda a: a.reshape(m, 512)
    kk4 = r4(kk)
    ops_f = (r4(wf), r4(kf), r4(kaf))
    ops_b = (r4(wb), r4(kb), r4(kab))
    y_f, hist_f, y_b, hist_b, s_last, gathered_b = _scan_fwd(zs, kk4, ops_f, ops_b, shard_b)
    fw.update(unpack(gathered_b, shard_b, _BIG_B))
    uq = fw["w_uq"].astype(F32).reshape(Q_RANK, HEADS, 96)
    z32 = jnp.zeros((Q_RANK, HEADS, 32), F32)
    wq = jnp.concatenate([uq[..., :64], uq[..., 64:], z32], axis=-1).reshape(Q_RANK, 1024)
    wqr = jnp.concatenate([z32, z32, _rot_cols(uq[..., 64:]), z32], axis=-1).reshape(Q_RANK, 1024)
    ukv = fw["w_ukv"].astype(F32).reshape(KV_RANK, HEADS, 128)
    wk = jnp.concatenate([ukv[..., :64], jnp.zeros_like(ukv[..., :64])], axis=-1).reshape(KV_RANK, 1024)
    wv = ukv[..., 64:].reshape(KV_RANK, 512)
    mp = dict(q_norm_g=vec("q_norm_g"), kv_norm_g=vec("kv_norm_g"), wq=wq, wqr=wqr, wk=wk, wv=wv)
    w_up_g, w_up_v = fw["w_ffn_up"][:, :D_FF], fw["w_ffn_up"][:, D_FF:]
    cw, cb = fw["ffn_conv_w"], vec("ffn_conv_b")
    y_f, y_b = f2(y_f), f2(y_b)
    y_rwkv = _post_fwd(y_f, y_b, zs2, kf, kb, gate, rw)
    q, kfull, v = _mla_fwd(zm, cs, sn, mp, t)
    o, lse = _attn_fwd(q, kfull, v, b, t)
    y_mla = _rms_fwd(o, vec("mla_out_g"), "rms_mla_out")
    ymix = jnp.concatenate([y_rwkv, y_mla], axis=1)
    h1 = _mm(ymix, fw["w_out"], "nn", "proj_out", add=x2)
    n2 = _rms_fwd(h1, vec("ln_ffn_g"), "rms_ffn")
    ug = _mm(n2, w_up_g, "nn", "ffn_up_gate")
    uv = _mm(n2, w_up_v, "nn", "ffn_up_val")
    r3f = lambda a: a.reshape(b, t, D_FF)
    act = _ffn_mid_fwd(r3f(ug), r3f(uv), cw, cb).reshape(m, D_FF)
    h2 = _mm(act, fw["w_ffn_down"], "nn", "ffn_down", add=h1)
    loss_tab, dh2, g_ln_final = _final(h2, vec("ln_final_g"), tgt)

    gfull = {}
    dact = _mm(dh2, fw["w_ffn_down"], "nt", "d_ffn_act")
    gfull["w_ffn_down"] = _mm(act, dh2, "tn", "g_ffn_down")
    dug, duv, tab_g, tab_v = _ffn_mid_bwd(r3f(ug), r3f(uv), cw, cb, r3f(dact))
    dug, duv = dug.reshape(m, D_FF), duv.reshape(m, D_FF)
    gfull["ffn_conv_w"] = jnp.concatenate([tab_g[0:3], tab_v[0:3]], axis=1)
    g_conv_b = jnp.concatenate([tab_g[3:4], tab_v[3:4]], axis=1)
    dn2 = _mm(duv, w_up_v, "nt", "d_ffn_in_val", add=_mm(dug, w_up_g, "nt", "d_ffn_in_gate"))
    shard_cols = arrs["w_ffn_up"].shape[2]
    gfull["w_ffn_up"] = jnp.concatenate([_mm(n2, dug, "tn", "g_ffn_up_gate", column_blocks=shard_cols),
                                         _mm(n2, duv, "tn", "g_ffn_up_val", column_blocks=shard_cols)], axis=0)
    dh1, g_ln_ffn = _rms_bwd(h1, vec("ln_ffn_g"), dn2, "rms_ffn_bwd", dres=dh2)
    dymix = _mm(dh1, fw["w_out"], "nt", "d_mix")
    gfull["w_out"] = _mm(ymix, dh1, "tn", "g_w_out")
    do, g_mla_out = _rms_bwd(o, vec("mla_out_g"), dymix, "rms_mla_out_bwd", dy_block=1)
    dq, dk, dv = _attn_bwd(q, kfull, v, o, lse, do, b, t)
    dzm, g_qn, g_kvn, g_wq, g_wqr, g_wk, g_wv = _mla_bwd(zm, cs, sn, mp, t, dq, dk, dv)
    gq3, gqr3 = g_wq.reshape(Q_RANK, HEADS, 128), g_wqr.reshape(Q_RANK, HEADS, 128)
    gfull["w_uq"] = jnp.concatenate(
        [gq3[..., :64], gq3[..., 64:96] + _rot_cols_t(gqr3[..., 64:96])], axis=-1
    ).reshape(Q_RANK, HEADS * 96)
    gfull["w_ukv"] = jnp.concatenate(
        [g_wk.reshape(KV_RANK, HEADS, 128)[..., :64], g_wv.reshape(KV_RANK, HEADS, 64)], axis=-1
    ).reshape(KV_RANK, 1024)
    def cores_first(entries, tag):
        packed = _pack_grads(gfull, arrs, entries)
        rh = packed.shape[1] // 2
        own = lax.dynamic_slice_in_dim(packed, core * rh, rh, axis=1)
        sib = _send_to_sibling(packed, True)
        return _add_rows([own.reshape(4 * rh, 128), sib.reshape(4 * rh, 128)], "sum_cores_" + tag, BF16).reshape(4, rh, 128)

    def join_halves(half, entries):
        other = _send_to_sibling(half, False)
        lower = jnp.where(core == 0, half, other)
        upper = jnp.where(core == 0, other, half)
        return _unpack_grads(jnp.concatenate([lower, upper], axis=0), arrs, entries)

    part_b = _pack_grads(gfull, arrs, _BIG_B, BF16)
    dys, dr_p, dk_p, dv_p, dgate, g_rk, g_lnx_g, g_lnx_b = _post_bwd(y_f, y_b, zs2, kf, kb, gate, rw, dymix)
    (dr_f, dwf, dkf, dkk_f, dkaf, dv_f, dr_b, dwb, dkb, dkk_b, dkab, dv_b, recv_b) = _scan_bwd(
        zs, kk4, r4(dys), ops_f, hist_f, ops_b, hist_b, s_last, part_b)
    rh_b = part_b.shape[1] // 2
    mine_b = lax.dynamic_slice(part_b, (chip, core * rh_b, 0), (1, rh_b, 128))[0]
    g_big = join_halves(_add_rows([mine_b] + [recv_b[k] for k in range(7)], "sum_devices_b"), _BIG_B)
    cts = dict(dwf=f2(dwf), dwb=f2(dwb), dkf=f2(dkf), dkb=f2(dkb), dkk_f=f2(dkk_f), dkk_b=f2(dkk_b), dkaf=f2(dkaf), dkab=f2(dkab),
               dr_f=f2(dr_f), dr_b=f2(dr_b), dr_p=dr_p, dk_p=dk_p, dv_p=dv_p, dg=dgate, dv_f=f2(dv_f), dv_b=f2(dv_b))
    dzs, g_w0, g_w2, g_a0, g_a2, g_g2, g_kk, g_ka = _prep_bwd(zs2, rw, cts)
    dzr, g_mu_p, g_mu_n = _shift_bwd(dzs.reshape(b, t, RWKV_COLS), zr.reshape(b, t, RWKV_COLS), vec("shift_mu_prev"), vec("shift_mu_next"))
    dzr = dzr.reshape(m, RWKV_COLS)
    gfull["decay_w2_fwd"], gfull["decay_w2_bwd"] = g_w2[:64, :512], g_w2[64:, 512:]
    gfull["iclr_a2_fwd"], gfull["iclr_a2_bwd"] = g_a2[:64, :512], g_a2[64:, 512:]
    gfull["gate_g2"] = g_g2
    dn1 = _mm(dzr, win_r, "nn", "d_proj_in_rwkv", add=_mm(dzm, win_m, "nn", "d_proj_in_mla"))
    g_m = _mm(dzm, n1, "tn", "g_w_in_mla")
    g_r = _mm(dzr, n1, "tn", "g_w_in_rwkv")
    g_kr = g_m[1088:1120] + jnp.swapaxes(_rot_cols_t(jnp.swapaxes(g_m[1216:1248], 0, 1)), 0, 1)
    gfull["w_in"] = jnp.concatenate([g_r, g_m[:1024], g_kr], axis=0)
    dx, g_ln_mix = _rms_bwd(x2, vec("ln_mix_g"), dn1, "rms_mix_bwd", dres=dh1)

    part_a = cores_first(_BIG_A, "a")
    recv_a = _scatter_to_chips(part_a)
    mine_a = lax.dynamic_index_in_dim(part_a, chip, axis=0, keepdims=False)
    g_big.update(join_halves(_add_rows([mine_a, recv_a[0], recv_a[1], recv_a[2]], "sum_chips_a"), _BIG_A))
    small = {
        "ln_mix_g": g_ln_mix, "shift_mu_prev": g_mu_p, "shift_mu_next": g_mu_n, "decay_w0_fwd": g_w0[:, :512],
        "decay_w0_bwd": g_w0[:, 512:], "iclr_a0_fwd": g_a0[:, :512], "iclr_a0_bwd": g_a0[:, 512:], "k_k": g_kk, "k_a": g_ka,
        "r_k": g_rk, "ln_x_g": g_lnx_g, "ln_x_b": g_lnx_b, "q_norm_g": g_qn, "kv_norm_g": g_kvn, "mla_out_g": g_mla_out,
        "ln_ffn_g": g_ln_ffn, "ffn_conv_b": g_conv_b, "ln_final_g": g_ln_final,
        "_loss": jnp.pad(loss_tab[0, 0:1], (0, 127)),
    }
    g_small_buf = _allreduce_small(_pack_small(small))
    g_small = _unpack_small(g_small_buf, arrs)

    grads, deltas, new_m, new_v = {}, {}, {}, {}
    for name, _, _ in _BIG:
        grads[name] = g_big[name]
        deltas[name], new_m[name], new_v[name] = _adamw(
            arrs[name], g_big[name], arrs["m_" + name], arrs["v_" + name], "adamw_" + name)
    pk = lambda pre: _pack_small({**{n: arrs[pre + n] for n in _SMALL}, "_loss": jnp.zeros((128,), F32)})
    sd, sm, sv = _adamw(pk(""), g_small_buf, pk("m_"), pk("v_"), "adamw_small")
    sd, sm, sv = _unpack_small(sd, arrs), _unpack_small(sm, arrs), _unpack_small(sv, arrs)
    for n in _SMALL:
        grads[n], deltas[n], new_m[n], new_v[n] = g_small[n], sd[n], sm[n], sv[n]
    for group in (grads, deltas, new_m, new_v):
        group["w_in"] = jnp.swapaxes(group["w_in"], 1, 2)

    return (g_small["_loss"], dx.reshape(b, t, d), *[grads[n] for n in _WEIGHTS], *[deltas[n] for n in _WEIGHTS],
            *[new_m[n] for n in _WEIGHTS], *[new_v[n] for n in _WEIGHTS])
```

```python
import functools
import math

import jax
import jax.numpy as jnp
from jax import lax
from jax.experimental import pallas as pl
from jax.experimental.pallas import tpu as pltpu

F32, BF16 = jnp.float32, jnp.bfloat16
MESH = pl.DeviceIdType.MESH
ANY = pl.BlockSpec(memory_space=pl.ANY)
VMEM = pl.BlockSpec(memory_space=pltpu.VMEM)
BS = pl.BlockSpec
SDS = jax.ShapeDtypeStruct

NORM_EPS = 1e-6
GN_EPS = 64e-5
L2_EPS = 1e-12
HEADS = 8
HEAD_DIM = 64
D_RWKV = HEADS * HEAD_DIM
ROPE_DIM = 32
ROPE_THETA = 10000.0
MLA_SCALE = (64 + ROPE_DIM) ** -0.5
Q_RANK, KV_RANK = 768, 256
RWKV_COLS = 1920
MLA_PAD_COLS = Q_RANK + KV_RANK + 256
D_FF = 2816
ADAM_LR, ADAM_B1, ADAM_B2, ADAM_EPS, ADAM_WD, ADAM_STEP = 0.001, 0.9, 0.999, 1e-08, 0.01, 10

V7X_LANES = 128
V7X_VMEM_LIMIT = 56 * 1024 * 1024
SCAN_CHUNK = 32
N_CHIPS = 4


def _cp(*sem):
    return pltpu.CompilerParams(dimension_semantics=sem, vmem_limit_bytes=V7X_VMEM_LIMIT)


def _tile(n, cands=(512, 640, 384, 256, 128)):
    for c in cands:
        if n % c == 0:
            return c
    return n


def _row_tile(n, cap=256):
    best = n
    for t in range(8, cap + 1, 8):
        if n % t == 0:
            best = t
    return best if best <= cap or n <= cap else n


def _rms(x, g):
    ms = jnp.mean(x * x, axis=-1, keepdims=True)
    return x * lax.rsqrt(ms + NORM_EPS) * g


@jax.custom_vjp
def _bdot(x, w):
    return jnp.dot(x.astype(BF16), w.astype(BF16), preferred_element_type=F32)


def _bdot_fwd(x, w):
    return _bdot(x, w), (x, w)


def _bdot_bwd(res, ct):
    x, w = res
    c = ct.astype(BF16)
    dx = lax.dot_general(c, w.astype(BF16), (((1,), (1,)), ((), ())), preferred_element_type=F32)
    dw = lax.dot_general(x.astype(BF16), c, (((0,), (0,)), ((), ())), preferred_element_type=F32)
    return dx.astype(x.dtype), dw.astype(w.dtype)


_bdot.defvjp(_bdot_fwd, _bdot_bwd)


@jax.custom_vjp
def _headsum(x, ones_bd):
    hi = x.astype(BF16)
    mid = (x - hi.astype(F32)).astype(BF16)
    ob = ones_bd.astype(BF16)
    return jnp.dot(hi, ob, preferred_element_type=F32) + jnp.dot(mid, ob, preferred_element_type=F32)


def _headsum_fwd(x, ones_bd):
    return _headsum(x, ones_bd), ones_bd


def _headsum_bwd(ones_bd, ct):
    return _headsum(ct, ones_bd), jnp.zeros_like(ones_bd)


_headsum.defvjp(_headsum_fwd, _headsum_bwd)


def _prep_fn(zs, w0, w2, a0, a2, g2, k_k, k_a, ones_bd):
    k = zs[:, 512:1024]
    wd = zs[:, 1536:1664]
    ad = zs[:, 1664:1792]
    gd = zs[:, 1792:1920]
    logit = w0 + _bdot(jnp.tanh(wd), w2)
    w = jnp.exp(-math.exp(-0.5) * jax.nn.sigmoid(logit))
    a = jax.nn.sigmoid(a0 + _bdot(ad, a2))
    g = _bdot(jax.nn.sigmoid(gd), g2)
    kkr = k * k_k
    nrm = jnp.sqrt(_headsum(kkr * kkr, ones_bd))
    kk = kkr / jnp.maximum(nrm, L2_EPS)
    a_f, a_b = a[:, :512], a[:, 512:]
    kf = k * (1.0 + (a_f - 1.0) * k_a)
    kb = k * (1.0 + (a_b - 1.0) * k_a)
    return w[:, :512], w[:, 512:], kf, kb, kk, kk * a_f, kk * a_b, g


def _post_fn(y, r, kf, kb, v, g, r_k, ln_g, ln_b, ones_bd):
    mu =_headsum(y, ones_bd) * (1.0 / HEAD_DIM)
    yc = y - mu
    var = _headsum(yc * yc, ones_bd) * (1.0 / HEAD_DIM)
    yn = yc * lax.rsqrt(var + GN_EPS) * ln_g + ln_b
    bonus = _headsum(r * (kf + kb) * r_k, ones_bd) * v
    return (yn + bonus) * g


def _cat8(x):
    return jnp.concatenate([x] * HEADS, axis=1)


def _mla_fn(zm, cs, sn, gq, gkv, wq, wqr, wk, wv):
    cq = zm[:, :Q_RANK]
    ckv = zm[:, Q_RANK:Q_RANK + KV_RANK]
    kr = zm[:, Q_RANK + KV_RANK:Q_RANK + KV_RANK + 128]
    krr = zm[:, Q_RANK + KV_RANK + 128:]
    cqn = _rms(cq, gq)
    ckvn = _rms(ckv, gkv)
    q = (_bdot(cqn, wq) * _cat8(cs) + _bdot(cqn, wqr) * _cat8(sn)) * MLA_SCALE
    kro = kr * cs + krr * sn
    kfull = _bdot(ckvn, wk) + _cat8(kro)
    v = _bdot(ckvn, wv)
    return q, kfull, v


def _adamw_math(w, g, m, v):
    m2 = ADAM_B1 * m + (1.0 - ADAM_B1) * g
    v2 = ADAM_B2 * v + (1.0 - ADAM_B2) * (g * g)
    m_hat = m2 / (1.0 - ADAM_B1 ** ADAM_STEP)
    v_hat = v2 / (1.0 - ADAM_B2 ** ADAM_STEP)
    delta = -ADAM_LR * (m_hat / (jnp.sqrt(v_hat) + ADAM_EPS) + ADAM_WD * w)
    return delta, m2, v2


_DIMS = {"nn": (((1,), (0,)), ((), ())), "nt": (((1,), (1,)), ((), ())), "tn": (((0,), (0,)), ((), ()))}


def _mm(a, b, mode, name, out_dtype=F32, add=None, column_blocks=None):
    if mode == "nn":
        (m, k), (_, n) = a.shape, b.shape
    elif mode == "nt":
        (m, k), (n, _) = a.shape, b.shape
    else:
        (k, m), (_, n) = a.shape, b.shape
    big = (1024, 1408, 768, 640, 512, 384, 256, 128)
    tm, tn, tk = _tile(m, big), column_blocks or _tile(n, big), _tile(k, (512, 1408, 640, 384, 256, 128))
    nk = k // tk

    def body(a_ref, b_ref, *rest):
        if add is None:
            o_ref, acc_ref = rest
        else:
            add_ref, o_ref, acc_ref = rest
        kk = pl.program_id(2)

        @pl.when(kk == 0)
        def _():
            acc_ref[...] = jnp.zeros_like(acc_ref)

        acc_ref[...] += lax.dot_general(
            a_ref[...].astype(BF16), b_ref[...].astype(BF16), _DIMS[mode], preferred_element_type=F32
        )

        @pl.when(kk == nk - 1)
        def _():
            r = acc_ref[...]
            if add is not None:
                r = r + add_ref[...]
            o_ref[...] = r.astype(out_dtype).reshape(o_ref.shape)

    a_spec = BS((tk, tm), lambda i, j, kk: (kk, i)) if mode == "tn" else BS((tm, tk), lambda i, j, kk: (i, kk))
    b_spec = BS((tn, tk), lambda i, j, kk: (j, kk)) if mode == "nt" else BS((tk, tn), lambda i, j, kk: (kk, j))
    o_spec = BS((tm, tn), lambda i, j, kk: (i, j))
    ins, specs = [a, b], [a_spec, b_spec]
    if add is not None:
        ins.append(add)
        specs.append(o_spec)
    out_shape = SDS((m, n), out_dtype)
    if column_blocks:
        assert add is None
        o_spec, out_shape = BS((1, tm, tn), lambda i, j, kk: (j, i, 0)), SDS((n // tn, m, tn), out_dtype)
    return pl.pallas_call(
        body, grid=(m // tm, n // tn, nk), in_specs=specs, out_specs=o_spec, out_shape=out_shape,
        scratch_shapes=[pltpu.VMEM((tm, tn), F32)], compiler_params=_cp("parallel", "parallel", "arbitrary"), name=name,
    )(*ins)


def _rms_fwd(x, g, name):
    m, d = x.shape
    tr = _tile(m)

    def body(x_ref, g_ref, o_ref):
        o_ref[...] = _rms(x_ref[...], g_ref[...]).astype(BF16)

    return pl.pallas_call(
        body, grid=(m // tr,), in_specs=[BS((tr, d), lambda i: (i, 0)), BS((1, d), lambda i: (0, 0))],
        out_specs=BS((tr, d), lambda i: (i, 0)), out_shape=SDS((m, d), BF16), compiler_params=_cp("parallel"), name=name,
    )(x, g)


def _rms_bwd(x, g, dy, name, dres=None, dy_block=0):
    m, d = x.shape
    tr = _row_tile(m)

    def body(x_ref, g_ref, dy_ref, *rest):
        if dres is None:
            dx_ref, dg_ref = rest
        else:
            dres_ref, dx_ref, dg_ref = rest
        _, vjp = jax.vjp(_rms, x_ref[...], g_ref[...])
        dx, dg = vjp(dy_ref[...])
        if dres is not None:
            dx = dx + dres_ref[...]
        dx_ref[...] = dx

        @pl.when(pl.program_id(0) == 0)
        def _():
            dg_ref[...] = jnp.zeros_like(dg_ref)

        dg_ref[...] += dg

    row = BS((tr, d), lambda i: (i, 0))
    vec = BS((1, d), lambda i: (0, 0))
    ins, specs = [x, g, dy], [row, vec, BS((tr, d), lambda i: (i, dy_block))]
    if dres is not None:
        ins.append(dres)
        specs.append(row)
    return pl.pallas_call(
        body, grid=(m // tr,), in_specs=specs, out_specs=[row, vec], out_shape=[SDS((m, d), F32), SDS((1, d), F32)],
        compiler_params=_cp("arbitrary"), name=name,
    )(*ins)


def _final(h, g, tgt):
    m, d = h.shape
    tr = _row_tile(m)

    def loss_fn(hh, gg, tt):
        e = _rms(hh, gg) - tt
        return 0.5 * jnp.sum(e * e) * (1.0 / d)

    def body(h_ref, g_ref, t_ref, l_ref, dh_ref, dg_ref):
        val, (dh, dg) = jax.value_and_grad(loss_fn, argnums=(0, 1))(h_ref[...], g_ref[...], t_ref[...])
        dh_ref[...] = dh

        @pl.when(pl.program_id(0) == 0)
        def _():
            dg_ref[...] = jnp.zeros_like(dg_ref)
            l_ref[...] = jnp.zeros_like(l_ref)

        dg_ref[...] += dg
        l_ref[...] += jnp.full(l_ref.shape, val, F32)

    row = BS((tr, d), lambda i: (i, 0))
    vec = BS((1, d), lambda i: (0, 0))
    return pl.pallas_call(
        body, grid=(m // tr,), in_specs=[row, vec, row], out_specs=[BS((8, 128), lambda i: (0, 0)), row, vec],
        out_shape=[SDS((8, 128), F32), SDS((m, d), F32), SDS((1, d), F32)], compiler_params=_cp("arbitrary"), name="final_loss",
    )(h, g, tgt)


def _prev_next(z, t):
    row = lax.broadcasted_iota(jnp.int32, z.shape, 0)
    zp = jnp.where(row == 0, 0.0, pltpu.roll(z, 1, axis=0))
    zn = jnp.where(row == t - 1, 0.0, pltpu.roll(z, t - 1, axis=0))
    return zp, zn


def _shift_fwd(z3, mu_p, mu_n):
    b, t, c = z3.shape
    nc = c // 128

    def body(z_ref, mp_ref, mn_ref, o_ref):
        z = z_ref[0]
        zp, zn = _prev_next(z, t)
        o_ref[0] = z + mp_ref[...] * (zp - z) + mn_ref[...] * (zn - z)

    blk = BS((1, t, 128), lambda i, j: (i, 0, j))
    vec = BS((1, 128), lambda i, j: (0, j))
    return pl.pallas_call(
        body, grid=(b, nc), in_specs=[blk, vec, vec], out_specs=blk, out_shape=SDS((b, t, c), F32),
        compiler_params=_cp("parallel", "parallel"), name="shift_fwd",
    )(z3, mu_p, mu_n)


def _shift_bwd(dzs3, z3, mu_p, mu_n):
    b, t, c = z3.shape
    nc = c // 128

    def body(d_ref, z_ref, mp_ref, mn_ref, dz_ref, dmp_ref, dmn_ref):
        d, z = d_ref[0], z_ref[0]
        mp, mn = mp_ref[...], mn_ref[...]
        zp, zn = _prev_next(z, t)
        _, dp_next = _prev_next(d * mp, t)
        dn_prev, _ = _prev_next(d * mn, t)
        dz_ref[0] = (d * (1.0 - mp - mn) + dp_next + dn_prev).astype(BF16)

        @pl.when(pl.program_id(1) == 0)
        def _():
            dmp_ref[...] = jnp.zeros_like(dmp_ref)
            dmn_ref[...] = jnp.zeros_like(dmn_ref)

        dmp_ref[...] += jnp.sum(d * (zp - z), axis=0, keepdims=True)
        dmn_ref[...] += jnp.sum(d * (zn - z), axis=0, keepdims=True)

    blk = BS((1, t, 128), lambda j, i: (i, 0, j))
    vec = BS((1, 128), lambda j, i: (0, j))
    return pl.pallas_call(
        body, grid=(nc, b), in_specs=[blk, blk, vec, vec], out_specs=[blk, vec, vec],
        out_shape=[SDS((b, t, c), BF16), SDS((1, c), F32), SDS((1, c), F32)],
        compiler_params=_cp("parallel", "arbitrary"), name="shift_bwd",
    )(dzs3, z3, mu_p, mu_n)


def _const(shape):
    nd = len(shape)
    return BS(shape, lambda i: (0,) * nd)


def _prep_fwd(zs, p):
    m = zs.shape[0]
    tr = 256
    params = [p["w0"], p["w2"], p["a0"], p["a2"], p["g2"], p["k_k"], p["k_a"], p["ones_bd"]]

    def body(zs_ref, w0, w2, a0, a2, g2, kk_, ka_, bd, wf, wb, kf, kb, kk, kaf, kab, g):
        outs = _prep_fn(zs_ref[...], w0[...], w2[...], a0[...], a2[...], g2[...], kk_[...], ka_[...], bd[...])
        for ref, val in zip((wf, wb, kf, kb, kk, kaf, kab, g), outs):
            ref[...] = val

    row = BS((tr, 512), lambda i: (i, 0))
    return pl.pallas_call(
        body, grid=(m // tr,), in_specs=[BS((tr, RWKV_COLS), lambda i: (i, 0))] + [_const(q.shape) for q in params],
        out_specs=[row] * 8, out_shape=[SDS((m, 512), F32)] * 8, compiler_params=_cp("parallel"), name="rwkv_prep_fwd",
    )(zs, *params)


def _prep_bwd(zs, p, ct_rows):
    m = zs.shape[0]
    tr = 256
    params = [p["w0"], p["w2"], p["a0"], p["a2"], p["g2"], p["k_k"], p["k_a"]]
    names = ["dwf", "dwb", "dkf", "dkb", "dkk_f", "dkk_b", "dkaf", "dkab", "dr_f", "dr_b", "dr_p", "dk_p", "dv_p", "dg",
             "dv_f", "dv_b"]
    rows = [ct_rows[n] for n in names]

    def body(zs_ref, w0, w2, a0, a2, g2, kk_, ka_, bd, *rest):
        c = {n: r[...] for n, r in zip(names, rest[:len(names)])}
        outs = rest[len(names):]
        dzs_ref, grads = outs[0], outs[1:]
        ones_bd = bd[...]
        _, vjp = jax.vjp(
            lambda *q: _prep_fn(*q, ones_bd), zs_ref[...], w0[...], w2[...], a0[...], a2[...], g2[...], kk_[...], ka_[...]
        )
        cts = (c["dwf"], c["dwb"], c["dkf"] + c["dk_p"], c["dkb"] + c["dk_p"], c["dkk_f"] + c["dkk_b"], c["dkaf"], c["dkab"], c["dg"])
        dzs, *dparams = vjp(cts)
        dr = c["dr_f"] + c["dr_b"] + c["dr_p"]
        dv = c["dv_f"] + c["dv_b"] + c["dv_p"]
        dzs_ref[:, 0:512] = dzs[:, 0:512] + dr
        dzs_ref[:, 512:1024] = dzs[:, 512:1024]
        dzs_ref[:, 1024:1536] = dzs[:, 1024:1536] + dv
        dzs_ref[:, 1536:1920] = dzs[:, 1536:1920]

        @pl.when(pl.program_id(0) == 0)
        def _():
            for gr in grads:
                gr[...] = jnp.zeros_like(gr)

        for gr, val in zip(grads, dparams):
            gr[...] += val

    row = BS((tr, 512), lambda i: (i, 0))
    return pl.pallas_call(
        body, grid=(m // tr,),
        in_specs=[BS((tr, RWKV_COLS), lambda i: (i, 0))] + [_const(q.shape) for q in params] + [_const(p["ones_bd"].shape)]
        + [row] * len(names),
        out_specs=[BS((tr, RWKV_COLS), lambda i: (i, 0))] + [_const(q.shape) for q in params],
        out_shape=[SDS((m, RWKV_COLS), F32)] + [SDS(q.shape, F32) for q in params],
        compiler_params=_cp("arbitrary"), name="rwkv_prep_bwd",
    )(zs, *params, p["ones_bd"], *rows)


def _post_specs(tr):
    r = BS((tr, 512), lambda i: (i, 0))
    v = BS((tr, 512), lambda i: (i, 2))
    row = BS((tr, 512), lambda i: (i, 0))
    return r, v, row


def _post_fwd(y_f, y_b, zs, kf, kb, g, p):
    m = zs.shape[0]
    tr = 256
    r, v, row = _post_specs(tr)
    vecs = [p["r_k"], p["ln_x_g"], p["ln_x_b"], p["ones_bd"]]

    def body(yf, yb, r_ref, v_ref, kf_ref, kb_ref, g_ref, rk, lg, lb, bd, o_ref):
        o_ref[...] = _post_fn(
            yf[...] + yb[...], r_ref[...], kf_ref[...], kb_ref[...], v_ref[...], g_ref[...], rk[...], lg[...], lb[...], bd[...]
        ).astype(BF16)

    return pl.pallas_call(
        body, grid=(m // tr,), in_specs=[row, row, r, v, row, row, row] + [_const(q.shape) for q in vecs],
        out_specs=row, out_shape=SDS((m, 512), BF16), compiler_params=_cp("parallel"), name="rwkv_post_fwd",
    )(y_f, y_b, zs, zs, kf, kb, g, *vecs)


def _post_bwd(y_f, y_b, zs, kf, kb, g, p, dymix):
    m = zs.shape[0]
    tr = 256
    r, v, row = _post_specs(tr)
    vecs = [p["r_k"], p["ln_x_g"], p["ln_x_b"]]

    def body(yf, yb, r_ref, v_ref, kf_ref, kb_ref, g_ref, rk, lg, lb, bd, dy_ref, dyo, dr, dk, dv, dg, drk, dlg, dlb):
        ones_bd = bd[...]
        _, vjp = jax.vjp(
            lambda *q: _post_fn(*q, ones_bd),
            yf[...] + yb[...], r_ref[...], kf_ref[...], kb_ref[...], v_ref[...], g_ref[...], rk[...], lg[...], lb[...],
        )
        c_y, c_r, c_kf, _, c_v, c_g, c_rk, c_lg, c_lb = vjp(dy_ref[...])
        dyo[...] = c_y
        dr[...] = c_r
        dk[...] = c_kf
        dv[...] = c_v
        dg[...] = c_g

        @pl.when(pl.program_id(0) == 0)
        def _():
            for ref in (drk, dlg, dlb):
                ref[...] = jnp.zeros_like(ref)

        drk[...] += c_rk
        dlg[...] += c_lg
        dlb[...] += c_lb

    vec = _const((1, 512))
    return pl.pallas_call(
        body, grid=(m // tr,),
        in_specs=[row, row, r, v, row, row, row] + [_const(q.shape) for q in vecs] + [_const(p["ones_bd"].shape), row],
        out_specs=[row, row, row, row, row, vec, vec, vec],
        out_shape=[SDS((m, 512), F32)] * 5 + [SDS((1, 512), F32)] * 3,
        compiler_params=_cp("arbitrary"), name="rwkv_post_bwd",
    )(y_f, y_b, zs, zs, kf, kb, g, *vecs, p["ones_bd"], dymix)


SCAN_MXU_GROUPS = 2


def _half_ones():
    ri = lax.broadcasted_iota(jnp.int32, (128, 128), 0)
    ci = lax.broadcasted_iota(jnp.int32, (128, 128), 1)
    return jnp.where((ri < 64) == (ci < 64), 1.0, 0.0).astype(BF16)


def _half_sums(xs, ones):
    out = []
    per = -(-len(xs) // SCAN_MXU_GROUPS)
    for g in range(0, len(xs), per):
        part = xs[g:g + per]
        res = jnp.dot(jnp.concatenate(part, axis=0).astype(BF16), ones, preferred_element_type=F32)
        out += [res[64 * i:64 * i + 64] for i in range(len(part))]
    return out


def _scan_specs(b, t):
    nc = t // SCAN_CHUNK
    up, down = (lambda c: c), (lambda c: nc - 1 - c)
    rows = [BS((b, SCAN_CHUNK, 512), lambda c, ci=ci: (0, ci(c), 0)) for ci in (up, down)]
    vrows = [BS((b, SCAN_CHUNK, 512), lambda c, ci=ci: (0, ci(c), 2)) for ci in (up, down)]
    hist = [BS((SCAN_CHUNK, b * 4, 64, 128), lambda c, ci=ci: (ci(c), 0, 0, 0)) for ci in (up, down)]
    return nc, rows, vrows, hist


class _Window:
    def __init__(self, g, ascending):
        self.bases = [pl.multiple_of(g * 8, 8) if asc else pl.multiple_of(SCAN_CHUNK - 8 - g * 8, 8) for asc in ascending]
        self.ascending = ascending
        self.blocks = {}
        self.row_id = lax.broadcasted_iota(jnp.int32, (8, 128), 0)

    def j(self, d, s):
        return s if self.ascending[d] else 7 - s

    def time(self, d, s):
        return self.bases[d] + self.j(d, s)

    def row(self, ref, d, bi, cols, s):
        key = (id(ref), d, bi, cols.start)
        if key not in self.blocks:
            self.blocks[key] = ref[bi, pl.ds(self.bases[d], 8), cols]
        jj = self.j(d, s)
        return self.blocks[key][jj:jj + 1, :]

    def put(self, buf, key, d, s, row):
        prev = buf.get(key)
        new = jnp.broadcast_to(row, (8, 128))
        buf[key] = new if prev is None else jnp.where(self.row_id == self.j(d, s), new, prev)

    def flush(self, buf, refs_of):
        for key, val in buf.items():
            ref, d, bi, cols = refs_of(key)
            ref[bi, pl.ds(self.bases[d], 8), cols] = val


def _pairs(b):
    return [(bi * 4 + p, bi, slice(128 * p, 128 * p + 128)) for bi in range(b) for p in range(4)]


def _colsum(x):
    return jnp.sum(x, axis=0, keepdims=True)


def _pair_matvec(row, mat):
    rid = lax.broadcasted_iota(jnp.int32, (8, 64), 0)
    lhs = jnp.where(rid == 0, row[:, :64], jnp.where(rid == 1, row[:, 64:], 0.0))
    out = jnp.dot(lhs.astype(BF16), mat.astype(BF16), preferred_element_type=F32)
    lo = lax.broadcasted_iota(jnp.int32, (1, 128), 1) < 64
    return jnp.where(lo, out[0:1], out[1:2])


def _eye_mask():
    return (lax.broadcasted_iota(jnp.int32, (64, 128), 1) & 63) == lax.broadcasted_iota(jnp.int32, (64, 128), 0)


def _scan_fwd(zs, kk, ops_f, ops_b, shard):
    b, t = zs.shape[:2]
    nc, rows, vrows, hist = _scan_specs(b, t)
    npair = b * 4

    def body(*refs):
        ins, shard_ref, outs, s_ref = refs[:12], refs[12], refs[13:17], refs[17]
        gather = (shard_ref, *refs[18:21])
        dirs = [dict(zip(("r", "kk", "v", "w", "k", "ka", "y", "h"), (*ins[6 * d:6 * d + 6], *outs[2 * d:2 * d + 2])))
                for d in (0, 1)]

        @pl.when(pl.program_id(0) == 0)
        def _():
            s_ref[...] = jnp.zeros_like(s_ref)
            _gather_halves(*gather, "start")

        @pl.when(pl.program_id(0) == nc - 1)
        def _():
            _gather_halves(*gather, "finish")

        ones, eye = _half_ones(), _eye_mask()
        chains = [(d, pr, bi, cols) for d in (0, 1) for pr, bi, cols in _pairs(b)]

        def eight_steps(g, carry):
            win = _Window(g, (True, False))
            ybuf = {}
            for s in range(8):
                s_prev, xa = [], []
                for d, pr, bi, cols in chains:
                    q = dirs[d]
                    st = s_ref[d * npair + pr]
                    q["h"][win.time(d, s), pr] = st
                    s_prev.append(st)
                    xa += [st * win.row(q["kk"], d, bi, cols, s), jnp.where(eye, win.row(q["v"], d, bi, cols, s), 0.0)]
                ra = _half_sums(xa, ones)
                xb = []
                for i, (d, pr, bi, cols) in enumerate(chains):
                    q = dirs[d]
                    s_new = s_prev[i] * win.row(q["w"], d, bi, cols, s) - ra[2 * i] * win.row(q["ka"], d, bi, cols, s) \
                        + ra[2 * i + 1] * win.row(q["k"], d, bi, cols, s)
                    s_ref[d * npair + pr] = s_new
                    xb.append(s_new * win.row(q["r"], d, bi, cols, s))
                rb = _half_sums(xb, ones)
                for i, (d, pr, bi, cols) in enumerate(chains):
                    win.put(ybuf, i, d, s, _colsum(jnp.where(eye, rb[i], 0.0)))
            win.flush(ybuf, lambda i: (dirs[chains[i][0]]["y"], chains[i][0], chains[i][2], chains[i][3]))
            return carry

        lax.fori_loop(0, SCAN_CHUNK // 8, eight_steps, 0)

    row_shape, hist_shape = SDS((b, t, 512), F32), SDS((t, npair, 64, 128), F32)
    state = (2 * npair, 64, 128)
    return pl.pallas_call(
        body, grid=(nc,), in_specs=sum(([rows[d], rows[d], vrows[d]] + [rows[d]] * 3 for d in (0, 1)), []) + [ANY],
        out_specs=[rows[0], hist[0], rows[1], hist[1], BS(state, lambda c: (0, 0, 0)), ANY],
        out_shape=[row_shape, hist_shape, row_shape, hist_shape, SDS(state, F32), _gathered_shape(shard)],
        scratch_shapes=_gather_sems(), compiler_params=_cp("arbitrary"), name="wkv_scan",
    )(zs, kk, zs, *ops_f, zs, kk, zs, *ops_b, shard)


def _scan_bwd(zs, kk, dy, ops_f, hist_f, ops_b, hist_b, s_last, partials):
    b, t = zs.shape[:2]
    nc, rows, vrows, hist = _scan_specs(b, t)
    npair = b * 4
    names_in = ("r", "kk", "v", "dy", "w", "k", "ka", "h")
    names_out = ("dr", "dw", "dk", "dkk", "dka", "dv")

    def body(*refs):
        ins, last_ref, part_ref, outs, recv_ref = refs[:16], refs[16], refs[17], refs[18:30], refs[30]
        ds_ref, after_ref = refs[31], refs[32]
        scatter = (part_ref, recv_ref, refs[33], refs[34])
        dirs = [dict(zip(names_in + names_out, (*ins[8 * d:8 * d + 8], *outs[6 * d:6 * d + 6]))) for d in (0, 1)]

        @pl.when(pl.program_id(0) == 0)
        def _():
            ds_ref[...] = jnp.zeros_like(ds_ref)
            after_ref[...] = last_ref[...]
            _scatter_to_all(*scatter, "start")

        @pl.when(pl.program_id(0) == nc - 1)
        def _():
            _scatter_to_all(*scatter, "finish")

        ones, eye = _half_ones(), _eye_mask()
        chains = [(d, pr, bi, cols) for d in (0, 1) for pr, bi, cols in _pairs(b)]

        def eight_steps(g, carry):
            win = _Window(g, (False, True))
            obuf = {}
            s_after = [after_ref[d * npair + pr] for d, pr, _, _ in chains]
            for s in range(8):
                row = lambda name, d, bi, cols: win.row(dirs[d][name], d, bi, cols, s)
                s_prev, xa = [], []
                for d, pr, bi, cols in chains:
                    st = dirs[d]["h"][win.time(d, s), pr]
                    s_prev.append(st)
                    xa += [st * row("kk", d, bi, cols), jnp.where(eye, row("dy", d, bi, cols), 0.0)]
                ra = _half_sums(xa, ones)
                ds_now, xb = [], []
                for i, (d, pr, bi, cols) in enumerate(chains):
                    skk, dycol = ra[2 * i], ra[2 * i + 1]
                    ds = ds_ref[d * npair + pr] + dycol * row("r", d, bi, cols)
                    win.put(obuf, (i, "dr"), d, s, _pair_matvec(row("dy", d, bi, cols), s_after[i]))
                    win.put(obuf, (i, "dk"), d, s, _pair_matvec(row("v", d, bi, cols), ds))
                    win.put(obuf, (i, "dka"), d, s, -_colsum(ds * skk))
                    win.put(obuf, (i, "dw"), d, s, _colsum(ds * s_prev[i]))
                    ds_now.append(ds)
                    xb += [ds * row("k", d, bi, cols), ds * row("ka", d, bi, cols)]
                rb = _half_sums(xb, ones)
                for i, (d, pr, bi, cols) in enumerate(chains):
                    dskk_neg = rb[2 * i + 1]
                    win.put(obuf, (i, "dv"), d, s, _colsum(jnp.where(eye, rb[2 * i], 0.0)))
                    win.put(obuf, (i, "dkk"), d, s, -_colsum(s_prev[i] * dskk_neg))
                    ds_ref[d * npair + pr] = ds_now[i] * row("w", d, bi, cols) - dskk_neg * row("kk", d, bi, cols)
                s_after = s_prev
            for i, (d, pr, _, _) in enumerate(chains):
                after_ref[d * npair + pr] = s_after[i]
            win.flush(obuf, lambda key: (dirs[chains[key[0]][0]][key[1]], chains[key[0]][0], chains[key[0]][2], chains[key[0]][3]))
            return carry

        lax.fori_loop(0, SCAN_CHUNK // 8, eight_steps, 0)

    row_shape = SDS((b, t, 512), F32)
    state = (2 * npair, 64, 128)
    return pl.pallas_call(
        body, grid=(nc,),
        in_specs=sum(([rows[d], rows[d], vrows[d]] + [rows[d]] * 4 + [hist[d]] for d in (1, 0)), [])
        + [BS(state, lambda c: (0, 0, 0)), ANY],
        out_specs=[rows[1]] * 6 + [rows[0]] * 6 + [ANY],
        out_shape=[row_shape] * 12 + [SDS((7, partials.shape[1] // 2, 128), partials.dtype)],
        scratch_shapes=[pltpu.VMEM(state, F32), pltpu.VMEM(state, F32), pltpu.SemaphoreType.DMA((7,)),
                        pltpu.SemaphoreType.DMA((7,))],
        compiler_params=_cp("arbitrary"), name="wkv_scan_bwd",
    )(zs, kk, zs, dy, *ops_f, hist_f, zs, kk, zs, dy, *ops_b, hist_b, s_last, partials)


def _mla_fwd(zm, cs, sn, p, t):
    m = zm.shape[0]
    tr = 256
    per = t // tr
    params = [p["q_norm_g"], p["kv_norm_g"], p["wq"], p["wqr"], p["wk"], p["wv"]]

    def body(z_ref, cs_ref, sn_ref, gq, gkv, wq, wqr, wk, wv, q_ref, k_ref, v_ref):
        q, kf, v = _mla_fn(z_ref[...], cs_ref[...], sn_ref[...], gq[...], gkv[...], wq[...], wqr[...], wk[...], wv[...])
        q_ref[...] = q.astype(BF16)
        k_ref[...] = kf.astype(BF16)
        v_ref[...] = v.astype(BF16)

    tab = BS((tr, 128), lambda i: (i % per, 0))
    return pl.pallas_call(
        body, grid=(m // tr,), in_specs=[BS((tr, MLA_PAD_COLS), lambda i: (i, 0)), tab, tab] + [_const(q.shape) for q in params],
        out_specs=[BS((tr, 1024), lambda i: (i, 0)), BS((tr, 1024), lambda i: (i, 0)), BS((tr, 512), lambda i: (i, 0))],
        out_shape=[SDS((m, 1024), BF16), SDS((m, 1024), BF16), SDS((m, 512), BF16)], compiler_params=_cp("parallel"), name="mla_prep_fwd",
    )(zm, cs, sn, *params)


def _mla_bwd(zm, cs, sn, p, t, dq, dk, dv):
    m = zm.shape[0]
    tr = 256
    per = t // tr
    params = [p["q_norm_g"], p["kv_norm_g"], p["wq"], p["wqr"], p["wk"], p["wv"]]

    def body(z_ref, cs_ref, sn_ref, gq, gkv, wq, wqr, wk, wv, dq_ref, dk_ref, dv_ref, dz_ref, *grads):
        cs_v, sn_v = cs_ref[...], sn_ref[...]
        _, vjp = jax.vjp(
            lambda *q: _mla_fn(q[0], cs_v, sn_v, *q[1:]), z_ref[...], gq[...], gkv[...], wq[...], wqr[...], wk[...], wv[...]
        )
        dz, *dparams = vjp((dq_ref[...], dk_ref[...], dv_ref[...]))
        dz_ref[...] = dz.astype(BF16)

        @pl.when(pl.program_id(0) == 0)
        def _():
            for gr in grads:
                gr[...] = jnp.zeros_like(gr)

        for gr, val in zip(grads, dparams):
            gr[...] += val

    tab = BS((tr, 128), lambda i: (i % per, 0))
    wide = BS((tr, 1024), lambda i: (i, 0))
    return pl.pallas_call(
        body, grid=(m // tr,),
        in_specs=[BS((tr, MLA_PAD_COLS), lambda i: (i, 0)), tab, tab] + [_const(q.shape) for q in params]
        + [wide, wide, BS((tr, 512), lambda i: (i, 0))],
        out_specs=[BS((tr, MLA_PAD_COLS), lambda i: (i, 0))] + [_const(q.shape) for q in params],
        out_shape=[SDS((m, MLA_PAD_COLS), BF16)] + [SDS(q.shape, F32) for q in params],
        compiler_params=_cp("arbitrary"), name="mla_prep_bwd",
    )(zm, cs, sn, *params, dq, dk, dv)


_NT = (((1,), (1,)), ((), ()))
_TN = (((0,), (0,)), ((), ()))


def _attn_fwd(q, kf, v, b, t):
    m = q.shape[0]
    tq = 256
    nq = t // tq

    def body(q_ref, k_ref, v_ref, o_ref, l_ref):
        lo = lax.broadcasted_iota(jnp.int32, (1, 128), 1) < 64
        v_all = v_ref[...]
        o = jnp.zeros((tq, 128), F32)
        lse = []
        for h in range(2):
            hs = slice(128 * h, 128 * h + 128)
            s = lax.dot_general(q_ref[:, hs], k_ref[:, hs], _NT, preferred_element_type=F32)
            mx = jnp.max(s, axis=1, keepdims=True)
            e = jnp.exp(s - mx)
            den = jnp.sum(e, axis=1, keepdims=True)
            vh = jnp.where(lo if h == 0 else jnp.logical_not(lo), v_all, jnp.zeros_like(v_all))
            o = o + jnp.dot(e.astype(BF16), vh, preferred_element_type=F32) / den
            lse.append(mx + jnp.log(den))
        o_ref[...] = o
        l_ref[...] = jnp.where(lo, lse[0], lse[1])

    return pl.pallas_call(
        body, grid=(b, 4, nq),
        in_specs=[BS((tq, 256), lambda bi, hp, i: (bi * nq + i, hp)), BS((t, 256), lambda bi, hp, i: (bi, hp)),
                  BS((t, 128), lambda bi, hp, i: (bi, hp))],
        out_specs=[BS((tq, 128), lambda bi, hp, i: (bi * nq + i, hp))] * 2,
        out_shape=[SDS((m, 512), F32), SDS((m, 512), F32)], compiler_params=_cp("parallel", "parallel", "arbitrary"), name="attn_fwd",
    )(q, kf, v)


def _attn_bwd(q, kf, v, o, lse, do, b, t):
    m = q.shape[0]
    tq = 256
    nq = t // tq

    def body(q_ref, k_ref, v_ref, o_ref, l_ref, do_ref, dq_ref, dk_ref, dv_ref):
        lo = lax.broadcasted_iota(jnp.int32, (1, 128), 1) < 64

        @pl.when(pl.program_id(2) == 0)
        def _():
            dk_ref[...] = jnp.zeros_like(dk_ref)
            dv_ref[...] = jnp.zeros_like(dv_ref)

        v_all, o_all, l_all, do_all = v_ref[...], o_ref[...], l_ref[...], do_ref[...]
        dv_acc = jnp.zeros((t, 128), F32)
        for h in range(2):
            hs = slice(128 * h, 128 * h + 128)
            mask = lo if h == 0 else jnp.logical_not(lo)
            qh, kh = q_ref[:, hs], k_ref[:, hs]
            s = lax.dot_general(qh, kh, _NT, preferred_element_type=F32)
            lse_h = jnp.max(jnp.where(mask, l_all, -jnp.inf), axis=1, keepdims=True)
            pr = jnp.exp(s - lse_h)
            do_h = jnp.where(mask, do_all, 0.0)
            dp = lax.dot_general(do_h.astype(BF16), v_all, _NT, preferred_element_type=F32)
            dsum = jnp.sum(do_h * o_all, axis=1, keepdims=True)
            ds = (pr * (dp - dsum)).astype(BF16)
            dq_ref[:, hs] = jnp.dot(ds, kh, preferred_element_type=F32)
            dk_ref[:, hs] += lax.dot_general(ds, qh, _TN, preferred_element_type=F32)
            dv_acc = dv_acc + lax.dot_general(pr.astype(BF16), do_h.astype(BF16), _TN, preferred_element_type=F32)
        dv_ref[...] += dv_acc

    qspec = BS((tq, 256), lambda bi, hp, i: (bi * nq + i, hp))
    kspec = BS((t, 256), lambda bi, hp, i: (bi, hp))
    vspec = BS((t, 128), lambda bi, hp, i: (bi, hp))
    ospec = BS((tq, 128), lambda bi, hp, i: (bi * nq + i, hp))
    return pl.pallas_call(
        body, grid=(b, 4, nq), in_specs=[qspec, kspec, vspec, ospec, ospec, ospec], out_specs=[qspec, kspec, vspec],
        out_shape=[SDS((m, 1024), F32), SDS((m, 1024), F32), SDS((m, 512), F32)],
        compiler_params=_cp("parallel", "parallel", "arbitrary"), name="attn_bwd",
    )(q, kf, v, o, lse, do)


def _conv3(u, w_ref, b_ref, t):
    up, un = _prev_next(u, t)
    return w_ref[0:1, :] * up + w_ref[1:2, :] * u + w_ref[2:3, :] * un + b_ref[...], up, un


def _ffn_mid_fwd(ug3, uv3, cw, cb):
    b, t, f = ug3.shape
    nc = f // 256

    def body(ug_ref, uv_ref, wg_ref, wv_ref, bg_ref, bv_ref, a_ref):
        gc, _, _ = _conv3(ug_ref[0], wg_ref, bg_ref, t)
        vc, _, _ = _conv3(uv_ref[0], wv_ref, bv_ref, t)
        a_ref[0] = (gc * jax.nn.sigmoid(gc) * vc).astype(BF16)

    blk = BS((1, t, 256), lambda i, j: (i, 0, j))
    return pl.pallas_call(
        body, grid=(b, nc),
        in_specs=[blk, blk, BS((3, 256), lambda i, j: (0, j)), BS((3, 256), lambda i, j: (0, j + nc)),
                  BS((1, 256), lambda i, j: (0, j)), BS((1, 256), lambda i, j: (0, j + nc))],
        out_specs=blk, out_shape=SDS((b, t, f), BF16), compiler_params=_cp("parallel", "parallel"), name="ffn_mid_fwd",
    )(ug3, uv3, cw, cw, cb, cb)


def _ffn_mid_bwd(ug3, uv3, cw, cb, da3):
    b, t, f = ug3.shape
    nc = f // 256

    def half(u, up, un, dc, w_ref):
        dprev, dnext = _prev_next(dc, t)
        du = w_ref[1:2, :] * dc + w_ref[0:1, :] * dnext + w_ref[2:3, :] * dprev
        sums = [jnp.sum(dc * q, axis=0, keepdims=True) for q in (up, u, un)] + [jnp.sum(dc, axis=0, keepdims=True)]
        row = lax.broadcasted_iota(jnp.int32, (8, 256), 0)
        tab = jnp.zeros((8, 256), F32)
        for i, s in enumerate(sums):
            tab = jnp.where(row == i, s, tab)
        return du, tab

    def body(ug_ref, uv_ref, wg_ref, wv_ref, bg_ref, bv_ref, da_ref, dug_ref, duv_ref, tg_ref, tv_ref):
        ug, uv, da = ug_ref[0], uv_ref[0], da_ref[0]
        gc, gp, gn = _conv3(ug, wg_ref, bg_ref, t)
        vc, vp, vn = _conv3(uv, wv_ref, bv_ref, t)
        sg = jax.nn.sigmoid(gc)
        d_gc = da * vc * (sg * (1.0 + gc * (1.0 - sg)))
        d_vc = da * (gc * sg)
        dug, tg = half(ug, gp, gn, d_gc, wg_ref)
        duv, tv = half(uv, vp, vn, d_vc, wv_ref)
        dug_ref[0] = dug.astype(BF16)
        duv_ref[0] = duv.astype(BF16)

        @pl.when(pl.program_id(1) == 0)
        def _():
            tg_ref[...] = jnp.zeros_like(tg_ref)
            tv_ref[...] = jnp.zeros_like(tv_ref)

        tg_ref[...] += tg
        tv_ref[...] += tv

    blk = BS((1, t, 256), lambda j, i: (i, 0, j))
    tab = BS((8, 256), lambda j, i: (0, j))
    return pl.pallas_call(
        body, grid=(nc, b),
        in_specs=[blk, blk, BS((3, 256), lambda j, i: (0, j)), BS((3, 256), lambda j, i: (0, j + nc)),
                  BS((1, 256), lambda j, i: (0, j)), BS((1, 256), lambda j, i: (0, j + nc)), blk],
        out_specs=[blk, blk, tab, tab],
        out_shape=[SDS((b, t, f), BF16), SDS((b, t, f), BF16), SDS((8, f), F32), SDS((8, f), F32)],
        compiler_params=_cp("parallel", "arbitrary"), name="ffn_mid_bwd",
    )(ug3, uv3, cw, cw, cb, cb, da3)


def _add_rows(parts, name, out_dtype=F32):
    r = parts[0].shape[0]
    tr = _row_tile(r, 1024)
    n = len(parts)

    def body(*refs):
        acc = refs[0][...].astype(F32)
        for q in refs[1:n]:
            acc = acc + q[...].astype(F32)
        refs[n][...] = acc.astype(out_dtype)

    row = BS((tr, 128), lambda i: (i, 0))
    return pl.pallas_call(
        body, grid=(r // tr,), in_specs=[row] * n, out_specs=row, out_shape=SDS((r, 128), out_dtype),
        compiler_params=_cp("parallel"), name=name,
    )(*parts)


def _adamw(w, g, m, v, name):
    lead = w.shape[:-2]
    r, c = w.shape[-2:]
    tr = _row_tile(r)

    def body(w_ref, g_ref, m_ref, v_ref, d_ref, m2_ref, v2_ref):
        d, m2, v2 = _adamw_math(w_ref[...], g_ref[...], m_ref[...], v_ref[...])
        d_ref[...] = d
        m2_ref[...] = m2
        v2_ref[...] = v2

    blk = BS((1,) * len(lead) + (tr, c), lambda i: (0,) * len(lead) + (i, 0))
    return pl.pallas_call(
        body, grid=(r // tr,), in_specs=[blk] * 4, out_specs=[blk] * 3, out_shape=[SDS(w.shape, F32)] * 3,
        compiler_params=_cp("parallel"), name=name,
    )(w, g, m, v)


def _place():
    return lax.axis_index("x"), lax.axis_index("y"), lax.axis_index("c")


def _flip(v, bit):
    return 1 - v if bit else v


def _allgather_weights(shard):
    def body(x_ref, out_ref, send_sems, recv_sems):
        _gather_halves(x_ref, out_ref, send_sems, recv_sems, "start")
        _gather_halves(x_ref, out_ref, send_sems, recv_sems, "finish")

    return pl.pallas_call(
        body, out_shape=_gathered_shape(shard), in_specs=[ANY], out_specs=ANY, scratch_shapes=_gather_sems(),
        name="allgather_weights",
    )(shard)


def _gathered_shape(shard):
    return SDS((8 * (shard.shape[0] // 2), 128), shard.dtype)


def _gather_sems():
    return [pltpu.SemaphoreType.DMA((6,)), pltpu.SemaphoreType.DMA((6,))]


def _gather_halves(x_ref, out_ref, send_sems, recv_sems, phase):
    rh = x_ref.shape[0] // 2
    x, y, c = _place()
    me, sibling = (x, y, c), (x, y, 1 - c)
    chips = [(1 - x, y), (x, 1 - y), (1 - x, 1 - y)]
    mine_src = x_ref.at[pl.ds(c * rh, rh), :]

    def rows(px, py, pc):
        return out_ref.at[pl.ds((4 * px + 2 * py + pc) * rh, rh), :]

    def copy(k, block, to, src=None):
        return pltpu.make_async_remote_copy(
            src_ref=rows(*block) if src is None else src, dst_ref=rows(*block), send_sem=send_sems.at[k],
            recv_sem=recv_sems.at[k], device_id=to, device_id_type=MESH,
        )

    first = [copy(j, me, (*chip, c), src=mine_src) for j, chip in enumerate(chips)]
    if phase == "start":
        for cp in first:
            cp.start()
        return
    passed = [copy(3 + j, (*chip, c), sibling) for j, chip in enumerate(chips)]
    for j, chip in enumerate(chips):
        copy(j, (*chip, c), me).wait_recv()
        passed[j].start()
    for j, chip in enumerate(chips):
        copy(3 + j, (*chip, 1 - c), me).wait_recv()
    for cp in first + passed:
        cp.wait_send()


def _scatter_partials(g_ref, recv_ref, send_sems, recv_sems, phase):
    x, y, c = _place()
    copies = []
    for j, (fx, fy) in enumerate(((1, 0), (0, 1), (1, 1))):
        px, py = _flip(x, fx), _flip(y, fy)
        copies.append(pltpu.make_async_remote_copy(
            src_ref=g_ref.at[2 * px + py], dst_ref=recv_ref.at[j], send_sem=send_sems.at[j], recv_sem=recv_sems.at[j],
            device_id=(px, py, c), device_id_type=MESH,
        ))
    if phase == "start":
        for cp in copies:
            cp.start()
        return
    for cp in copies:
        cp.wait_recv()
    for cp in copies:
        cp.wait_send()


def _scatter_sems():
    return [pltpu.SemaphoreType.DMA((3,)), pltpu.SemaphoreType.DMA((3,))]


def _scatter_to_all(g_ref, recv_ref, send_sems, recv_sems, phase):
    rh = g_ref.shape[1] // 2
    x, y, c = _place()
    copies = []
    for k in range(1, 8):
        px, py, pc = _flip(x, k >> 2 & 1), _flip(y, k >> 1 & 1), _flip(c, k & 1)
        copies.append(pltpu.make_async_remote_copy(
            src_ref=g_ref.at[2 * px + py, pl.ds(pc * rh, rh), :], dst_ref=recv_ref.at[k - 1], send_sem=send_sems.at[k - 1],
            recv_sem=recv_sems.at[k - 1], device_id=(px, py, pc), device_id_type=MESH,
        ))
    if phase == "start":
        for cp in copies:
            cp.start()
        return
    for cp in copies:
        cp.wait_recv()
    for cp in copies:
        cp.wait_send()


def _scatter_to_chips(g):
    def body(g_ref, recv_ref, send_sems, recv_sems):
        _scatter_partials(g_ref, recv_ref, send_sems, recv_sems, "start")
        _scatter_partials(g_ref, recv_ref, send_sems, recv_sems, "finish")

    return pl.pallas_call(
        body, out_shape=SDS((3,) + g.shape[1:], g.dtype), in_specs=[ANY], out_specs=ANY,
        scratch_shapes=[pltpu.SemaphoreType.DMA((3,)), pltpu.SemaphoreType.DMA((3,))], name="scatter_grads",
    )(g)


def _send_to_sibling(a, half_of_rows):
    rh = a.shape[1] // 2

    def body(a_ref, b_ref, send_sem, recv_sem):
        x, y, c = _place()
        src = a_ref.at[:, pl.ds((1 - c) * rh, rh), :] if half_of_rows else a_ref
        cp = pltpu.make_async_remote_copy(
            src_ref=src, dst_ref=b_ref, send_sem=send_sem, recv_sem=recv_sem, device_id=(x, y, 1 - c), device_id_type=MESH
        )
        cp.start()
        cp.wait()

    shape = (a.shape[0], rh, 128) if half_of_rows else a.shape
    return pl.pallas_call(
        body, out_shape=SDS(shape, a.dtype), in_specs=[ANY], out_specs=ANY,
        scratch_shapes=[pltpu.SemaphoreType.DMA, pltpu.SemaphoreType.DMA],
        name="sibling_halves" if half_of_rows else "sibling_swap",
    )(a)


def _allreduce_small(v):
    r = v.shape[0]

    def body(v_ref, out_ref, buf_ref, send_sems, recv_sems):
        x, y, c = _place()
        buf_ref[0] = v_ref[...]
        copies = []
        for k in range(1, 8):
            peer = (_flip(x, k >> 2 & 1), _flip(y, k >> 1 & 1), _flip(c, k & 1))
            cp = pltpu.make_async_remote_copy(
                src_ref=v_ref, dst_ref=buf_ref.at[k], send_sem=send_sems.at[k - 1], recv_sem=recv_sems.at[k - 1],
                device_id=peer, device_id_type=MESH,
            )
            cp.start()
            copies.append(cp)
        for cp in copies:
            cp.wait_recv()
        acc = None
        for d in range(8):
            slot = 4 * _flip(x, d >> 2 & 1) + 2 * _flip(y, d >> 1 & 1) + _flip(c, d & 1)
            term = buf_ref[slot]
            acc = term if acc is None else acc + term
        out_ref[...] = acc
        for cp in copies:
            cp.wait_send()

    return pl.pallas_call(
        body, out_shape=SDS(v.shape, F32), in_specs=[VMEM], out_specs=VMEM,
        scratch_shapes=[pltpu.VMEM((8, r, 128), F32), pltpu.SemaphoreType.DMA((7,)), pltpu.SemaphoreType.DMA((7,))],
        name="allreduce_small",
    )(v)


_BIG_A = (
    ("w_in", 0, False),
    ("decay_w2_fwd", 1, False), ("decay_w2_bwd", 1, False), ("iclr_a2_fwd", 1, False),
    ("iclr_a2_bwd", 1, False), ("gate_g2", 1, False),
)
_BIG_B = (
    ("w_uq", 0, False), ("w_ukv", 1, False), ("w_out", 0, False), ("w_ffn_up", 1, False), ("ffn_conv_w", 1, True),
    ("w_ffn_down", 0, False),
)
_BIG = _BIG_A + _BIG_B
_SMALL = (
    "ln_mix_g", "shift_mu_prev", "shift_mu_next", "decay_w0_fwd", "decay_w0_bwd", "iclr_a0_fwd", "iclr_a0_bwd", "k_k",
    "k_a", "r_k", "ln_x_g", "ln_x_b", "q_norm_g", "kv_norm_g", "mla_out_g", "ln_ffn_g", "ffn_conv_b", "ln_final_g",
)
_WEIGHTS = (
    "ln_mix_g", "w_in", "shift_mu_prev", "shift_mu_next", "decay_w0_fwd", "decay_w2_fwd", "decay_w0_bwd", "decay_w2_bwd",
    "iclr_a0_fwd", "iclr_a2_fwd", "iclr_a0_bwd", "iclr_a2_bwd", "gate_g2", "k_k", "k_a", "r_k", "ln_x_g", "ln_x_b",
    "q_norm_g", "w_uq", "kv_norm_g", "w_ukv", "mla_out_g", "w_out", "ln_ffn_g", "w_ffn_up", "ffn_conv_w", "ffn_conv_b",
    "w_ffn_down", "ln_final_g",
)


def _pad_rows(flat, rows):
    return jnp.pad(flat, (0, rows * 128 - flat.shape[0])).reshape(rows, 128)


def _rows_for(n, mult):
    rows = -(-n // 128)
    return -(-rows // mult) * mult


def _pack_shards_bf16(arrs, entries):
    parts = []
    for name, _, raw in entries:
        w = arrs[name][0]
        flat = lax.bitcast_convert_type(w, BF16).reshape(-1) if raw else w.astype(BF16).reshape(-1)
        parts.append(_pad_rows(flat, _rows_for(flat.shape[0], 32)))
    return jnp.concatenate(parts, axis=0)


def _unpack_gathered(g4, arrs, entries):
    out, off = {}, 0
    for name, axis, raw in entries:
        a, b = arrs[name].shape[1:]
        n = a * b * (2 if raw else 1)
        rows = _rows_for(n, 32)
        seg = g4[:, off:off + rows].reshape(4, rows * 128)[:, :n]
        off += rows
        if raw:
            seg = lax.bitcast_convert_type(seg.reshape(4, a * b, 2), F32)
        seg = seg.reshape(4, a, b)
        out[name] = jnp.concatenate([seg[s] for s in range(4)], axis=1) if axis == 1 else seg.reshape(4 * a, b)
    return out


def _pack_grads(full, arrs, entries, dtype=F32):
    parts = []
    for name, axis, _ in entries:
        a, b = arrs[name].shape[1:]
        g = full[name]
        if g.ndim == 3:
            sh = g
        else:
            sh = g.reshape(a, 4, b).transpose(1, 0, 2) if axis == 1 else g.reshape(4, a, b)
        rows = _rows_for(a * b, 8)
        flat = sh.reshape(4, a * b).astype(dtype)
        parts.append(jnp.pad(flat, ((0, 0), (0, rows * 128 - a * b))).reshape(4, rows, 128))
    total = sum(q.shape[1] for q in parts)
    parts.append(jnp.zeros((4, -(-total // 1024) * 1024 - total, 128), dtype))
    return jnp.concatenate(parts, axis=1)


def _unpack_grads(g, arrs, entries):
    out, off = {}, 0
    for name, _, _ in entries:
        a, b = arrs[name].shape[1:]
        rows = _rows_for(a * b, 8)
        out[name] = g[off:off + rows].reshape(-1)[:a * b].reshape(1, a, b)
        off += rows
    return out


def _pack_small(vals):
    flat = jnp.concatenate([vals[n].reshape(-1).astype(F32) for n in _SMALL] + [vals["_loss"].reshape(-1)])
    return _pad_rows(flat, _rows_for(flat.shape[0], 8))


def _unpack_small(buf, arrs):
    flat, out, off = buf.reshape(-1), {}, 0
    for n in _SMALL:
        size = arrs[n].size
        out[n] = flat[off:off + size].reshape(arrs[n].shape)
        off += size
    out["_loss"] = flat[off]
    return out


def _rot_cols(w):
    return jnp.concatenate([-w[..., 16:], w[..., :16]], axis=-1)


def _rot_cols_t(g):
    return jnp.concatenate([g[..., 16:], -g[..., :16]], axis=-1)


def _rope_tables(t):
    inv = jnp.power(ROPE_THETA, -jnp.arange(0, ROPE_DIM, 2, dtype=F32) / ROPE_DIM)
    ang = jnp.arange(t, dtype=F32)[:, None] * inv[None, :]
    one, zero = jnp.ones((t, 64), F32), jnp.zeros((t, 64), F32)
    cs = jnp.concatenate([one, jnp.cos(ang), jnp.cos(ang), zero[:, :32]], axis=1)
    sn = jnp.concatenate([zero, jnp.sin(ang), jnp.sin(ang), zero[:, :32]], axis=1)
    return cs, sn


def _block_diag(a, b):
    za = jnp.zeros_like(a)
    return jnp.concatenate([jnp.concatenate([a, za], axis=1), jnp.concatenate([za, b], axis=1)], axis=0)


def kernel(x, ln_mix_g, w_in, shift_mu_prev, shift_mu_next, decay_w0_fwd, decay_w2_fwd, decay_w0_bwd, decay_w2_bwd, iclr_a0_fwd, iclr_a2_fwd, iclr_a0_bwd, iclr_a2_bwd, gate_g2, k_k, k_a, r_k, ln_x_g, ln_x_b, q_norm_g, w_uq, kv_norm_g, w_ukv, mla_out_g, w_out, ln_ffn_g, w_ffn_up, ffn_conv_w, ffn_conv_b, w_ffn_down, ln_final_g, loss_target, m_ln_mix_g, m_w_in, m_shift_mu_prev, m_shift_mu_next, m_decay_w0_fwd, m_decay_w2_fwd, m_decay_w0_bwd, m_decay_w2_bwd, m_iclr_a0_fwd, m_iclr_a2_fwd, m_iclr_a0_bwd, m_iclr_a2_bwd, m_gate_g2, m_k_k, m_k_a, m_r_k, m_ln_x_g, m_ln_x_b, m_q_norm_g, m_w_uq, m_kv_norm_g, m_w_ukv, m_mla_out_g, m_w_out, m_ln_ffn_g, m_w_ffn_up, m_ffn_conv_w, m_ffn_conv_b, m_w_ffn_down, m_ln_final_g, v_ln_mix_g, v_w_in, v_shift_mu_prev, v_shift_mu_next, v_decay_w0_fwd, v_decay_w2_fwd, v_decay_w0_bwd, v_decay_w2_bwd, v_iclr_a0_fwd, v_iclr_a2_fwd, v_iclr_a0_bwd, v_iclr_a2_bwd, v_gate_g2, v_k_k, v_k_a, v_r_k, v_ln_x_g, v_ln_x_b, v_q_norm_g, v_w_uq, v_kv_norm_g, v_w_ukv, v_mla_out_g, v_w_out, v_ln_ffn_g, v_w_ffn_up, v_ffn_conv_w, v_ffn_conv_b, v_w_ffn_down, v_ln_final_g):
    arrs = dict(locals())
    for pre in ("", "m_", "v_"):
        arrs[pre + "w_in"] = jnp.swapaxes(arrs[pre + "w_in"], 1, 2)
    b, t, d = x.shape
    m = b * t
    x2 = x.reshape(m, d)
    tgt = loss_target.reshape(m, d)
    vec = lambda n: arrs[n].reshape(1, -1)

    core = lax.axis_index("c")
    chip = 2 * lax.axis_index("x") + lax.axis_index("y")
    def unpack(gathered, shard, entries):
        g4 = lax.dynamic_update_index_in_dim(gathered.reshape(N_CHIPS, -1, 128), shard, chip, axis=0)
        return _unpack_gathered(g4, arrs, entries)

    shard_a, shard_b = _pack_shards_bf16(arrs, _BIG_A), _pack_shards_bf16(arrs, _BIG_B)
    fw = unpack(_allgather_weights(shard_a), shard_a, _BIG_A)
    win = fw["w_in"]
    zc = jnp.zeros((64, d), BF16)
    w_kr = win[2944:2976]
    rot_kr = jnp.swapaxes(_rot_cols(jnp.swapaxes(w_kr, 0, 1)), 0, 1)
    win_m = jnp.concatenate([win[1920:2944], zc, w_kr, zc[:32], zc, rot_kr, zc[:32]], axis=0)
    win_r = win[:RWKV_COLS]
    head = jnp.arange(512) // HEAD_DIM
    rw = dict(
        w0=jnp.concatenate([vec("decay_w0_fwd"), vec("decay_w0_bwd")], axis=1),
        w2=_block_diag(fw["decay_w2_fwd"], fw["decay_w2_bwd"]).astype(F32),
        a0=jnp.concatenate([vec("iclr_a0_fwd"), vec("iclr_a0_bwd")], axis=1),
        a2=_block_diag(fw["iclr_a2_fwd"], fw["iclr_a2_bwd"]).astype(F32),
        g2=fw["gate_g2"].astype(F32), k_k=vec("k_k"), k_a=vec("k_a"), r_k=vec("r_k"), ln_x_g=vec("ln_x_g"), ln_x_b=vec("ln_x_b"),
        ones_bd=(head[:, None] == head[None, :]).astype(F32),
    )
    cs, sn = _rope_tables(t)

    n1 = _rms_fwd(x2, vec("ln_mix_g"), "rms_mix")
    zm = _mm(n1, win_m, "nt", "proj_in_mla")
    zr = _mm(n1, win_r, "nt", "proj_in_rwkv")
    zs = _shift_fwd(zr.reshape(b, t, RWKV_COLS), vec("shift_mu_prev"), vec("shift_mu_next"))
    zs2 = zs.reshape(m, RWKV_COLS)
    wf, wb, kf, kb, kk, kaf, kab, gate = _prep_fwd(zs2, rw)
    r4 = lambda a: a.reshape(b, t, 512)
    f2 = lambda a: a.reshape(m, 512)
    kk4 = r4(kk)
    ops_f = (r4(wf), r4(kf), r4(kaf))
    ops_b = (r4(wb), r4(kb), r4(kab))
    y_f, hist_f, y_b, hist_b, s_last, gathered_b = _scan_fwd(zs, kk4, ops_f, ops_b, shard_b)
    fw.update(unpack(gathered_b, shard_b, _BIG_B))
    uq = fw["w_uq"].astype(F32).reshape(Q_RANK, HEADS, 96)
    z32 = jnp.zeros((Q_RANK, HEADS, 32), F32)
    wq = jnp.concatenate([uq[..., :64], uq[..., 64:], z32], axis=-1).reshape(Q_RANK, 1024)
    wqr = jnp.concatenate([z32, z32, _rot_cols(uq[..., 64:]), z32], axis=-1).reshape(Q_RANK, 1024)
    ukv = fw["w_ukv"].astype(F32).reshape(KV_RANK, HEADS, 128)
    wk = jnp.concatenate([ukv[..., :64], jnp.zeros_like(ukv[..., :64])], axis=-1).reshape(KV_RANK, 1024)
    wv = ukv[..., 64:].reshape(KV_RANK, 512)
    mp = dict(q_norm_g=vec("q_norm_g"), kv_norm_g=vec("kv_norm_g"), wq=wq, wqr=wqr, wk=wk, wv=wv)
    w_up_g, w_up_v = fw["w_ffn_up"][:, :D_FF], fw["w_ffn_up"][:, D_FF:]
    cw, cb = fw["ffn_conv_w"], vec("ffn_conv_b")
    y_f, y_b = f2(y_f), f2(y_b)
    y_rwkv = _post_fwd(y_f, y_b, zs2, kf, kb, gate, rw)
    q, kfull, v = _mla_fwd(zm, cs, sn, mp, t)
    o, lse = _attn_fwd(q, kfull, v, b, t)
    y_mla = _rms_fwd(o, vec("mla_out_g"), "rms_mla_out")
    ymix = jnp.concatenate([y_rwkv, y_mla], axis=1)
    h1 = _mm(ymix, fw["w_out"], "nn", "proj_out", add=x2)
    n2 = _rms_fwd(h1, vec("ln_ffn_g"), "rms_ffn")
    ug = _mm(n2, w_up_g, "nn", "ffn_up_gate")
    uv = _mm(n2, w_up_v, "nn", "ffn_up_val")
    r3f = lambda a: a.reshape(b, t, D_FF)
    act = _ffn_mid_fwd(r3f(ug), r3f(uv), cw, cb).reshape(m, D_FF)
    h2 = _mm(act, fw["w_ffn_down"], "nn", "ffn_down", add=h1)
    loss_tab, dh2, g_ln_final = _final(h2, vec("ln_final_g"), tgt)

    gfull = {}
    dact = _mm(dh2, fw["w_ffn_down"], "nt", "d_ffn_act")
    gfull["w_ffn_down"] = _mm(act, dh2, "tn", "g_ffn_down")
    dug, duv, tab_g, tab_v = _ffn_mid_bwd(r3f(ug), r3f(uv), cw, cb, r3f(dact))
    dug, duv = dug.reshape(m, D_FF), duv.reshape(m, D_FF)
    gfull["ffn_conv_w"] = jnp.concatenate([tab_g[0:3], tab_v[0:3]], axis=1)
    g_conv_b = jnp.concatenate([tab_g[3:4], tab_v[3:4]], axis=1)
    dn2 = _mm(duv, w_up_v, "nt", "d_ffn_in_val", add=_mm(dug, w_up_g, "nt", "d_ffn_in_gate"))
    shard_cols = arrs["w_ffn_up"].shape[2]
    gfull["w_ffn_up"] = jnp.concatenate([_mm(n2, dug, "tn", "g_ffn_up_gate", column_blocks=shard_cols),
                                         _mm(n2, duv, "tn", "g_ffn_up_val", column_blocks=shard_cols)], axis=0)
    dh1, g_ln_ffn = _rms_bwd(h1, vec("ln_ffn_g"), dn2, "rms_ffn_bwd", dres=dh2)
    dymix = _mm(dh1, fw["w_out"], "nt", "d_mix")
    gfull["w_out"] = _mm(ymix, dh1, "tn", "g_w_out")
    do, g_mla_out = _rms_bwd(o, vec("mla_out_g"), dymix, "rms_mla_out_bwd", dy_block=1)
    dq, dk, dv = _attn_bwd(q, kfull, v, o, lse, do, b, t)
    dzm, g_qn, g_kvn, g_wq, g_wqr, g_wk, g_wv = _mla_bwd(zm, cs, sn, mp, t, dq, dk, dv)
    gq3, gqr3 = g_wq.reshape(Q_RANK, HEADS, 128), g_wqr.reshape(Q_RANK, HEADS, 128)
    gfull["w_uq"] = jnp.concatenate(
        [gq3[..., :64], gq3[..., 64:96] + _rot_cols_t(gqr3[..., 64:96])], axis=-1
    ).reshape(Q_RANK, HEADS * 96)
    gfull["w_ukv"] = jnp.concatenate(
        [g_wk.reshape(KV_RANK, HEADS, 128)[..., :64], g_wv.reshape(KV_RANK, HEADS, 64)], axis=-1
    ).reshape(KV_RANK, 1024)
    def cores_first(entries, tag):
        packed = _pack_grads(gfull, arrs, entries)
        rh = packed.shape[1] // 2
        own = lax.dynamic_slice_in_dim(packed, core * rh, rh, axis=1)
        sib = _send_to_sibling(packed, True)
        return _add_rows([own.reshape(4 * rh, 128), sib.reshape(4 * rh, 128)], "sum_cores_" + tag, BF16).reshape(4, rh, 128)

    def join_halves(half, entries):
        other = _send_to_sibling(half, False)
        lower = jnp.where(core == 0, half, other)
        upper = jnp.where(core == 0, other, half)
        return _unpack_grads(jnp.concatenate([lower, upper], axis=0), arrs, entries)

    part_b = _pack_grads(gfull, arrs, _BIG_B, BF16)
    dys, dr_p, dk_p, dv_p, dgate, g_rk, g_lnx_g, g_lnx_b = _post_bwd(y_f, y_b, zs2, kf, kb, gate, rw, dymix)
    (dr_f, dwf, dkf, dkk_f, dkaf, dv_f, dr_b, dwb, dkb, dkk_b, dkab, dv_b, recv_b) = _scan_bwd(
        zs, kk4, r4(dys), ops_f, hist_f, ops_b, hist_b, s_last, part_b)
    rh_b = part_b.shape[1] // 2
    mine_b = lax.dynamic_slice(part_b, (chip, core * rh_b, 0), (1, rh_b, 128))[0]
    g_big = join_halves(_add_rows([mine_b] + [recv_b[k] for k in range(7)], "sum_devices_b"), _BIG_B)
    cts = dict(dwf=f2(dwf), dwb=f2(dwb), dkf=f2(dkf), dkb=f2(dkb), dkk_f=f2(dkk_f), dkk_b=f2(dkk_b), dkaf=f2(dkaf), dkab=f2(dkab),
               dr_f=f2(dr_f), dr_b=f2(dr_b), dr_p=dr_p, dk_p=dk_p, dv_p=dv_p, dg=dgate, dv_f=f2(dv_f), dv_b=f2(dv_b))
    dzs, g_w0, g_w2, g_a0, g_a2, g_g2, g_kk, g_ka = _prep_bwd(zs2, rw, cts)
    dzr, g_mu_p, g_mu_n = _shift_bwd(dzs.reshape(b, t, RWKV_COLS), zr.reshape(b, t, RWKV_COLS), vec("shift_mu_prev"), vec("shift_mu_next"))
    dzr = dzr.reshape(m, RWKV_COLS)
    gfull["decay_w2_fwd"], gfull["decay_w2_bwd"] = g_w2[:64, :512], g_w2[64:, 512:]
    gfull["iclr_a2_fwd"], gfull["iclr_a2_bwd"] = g_a2[:64, :512], g_a2[64:, 512:]
    gfull["gate_g2"] = g_g2
    dn1 = _mm(dzr, win_r, "nn", "d_proj_in_rwkv", add=_mm(dzm, win_m, "nn", "d_proj_in_mla"))
    g_m = _mm(dzm, n1, "tn", "g_w_in_mla")
    g_r = _mm(dzr, n1, "tn", "g_w_in_rwkv")
    g_kr = g_m[1088:1120] + jnp.swapaxes(_rot_cols_t(jnp.swapaxes(g_m[1216:1248], 0, 1)), 0, 1)
    gfull["w_in"] = jnp.concatenate([g_r, g_m[:1024], g_kr], axis=0)
    dx, g_ln_mix = _rms_bwd(x2, vec("ln_mix_g"), dn1, "rms_mix_bwd", dres=dh1)

    part_a = cores_first(_BIG_A, "a")
    recv_a = _scatter_to_chips(part_a)
    mine_a = lax.dynamic_index_in_dim(part_a, chip, axis=0, keepdims=False)
    g_big.update(join_halves(_add_rows([mine_a, recv_a[0], recv_a[1], recv_a[2]], "sum_chips_a"), _BIG_A))
    small = {
        "ln_mix_g": g_ln_mix, "shift_mu_prev": g_mu_p, "shift_mu_next": g_mu_n, "decay_w0_fwd": g_w0[:, :512],
        "decay_w0_bwd": g_w0[:, 512:], "iclr_a0_fwd": g_a0[:, :512], "iclr_a0_bwd": g_a0[:, 512:], "k_k": g_kk, "k_a": g_ka,
        "r_k": g_rk, "ln_x_g": g_lnx_g, "ln_x_b": g_lnx_b, "q_norm_g": g_qn, "kv_norm_g": g_kvn, "mla_out_g": g_mla_out,
        "ln_ffn_g": g_ln_ffn, "ffn_conv_b": g_conv_b, "ln_final_g": g_ln_final,
        "_loss": jnp.pad(loss_tab[0, 0:1], (0, 127)),
    }
    g_small_buf = _allreduce_small(_pack_small(small))
    g_small = _unpack_small(g_small_buf, arrs)

    grads, deltas, new_m, new_v = {}, {}, {}, {}
    for name, _, _ in _BIG:
        grads[name] = g_big[name]
        deltas[name], new_m[name], new_v[name] = _adamw(
            arrs[name], g_big[name], arrs["m_" + name], arrs["v_" + name], "adamw_" + name)
    pk = lambda pre: _pack_small({**{n: arrs[pre + n] for n in _SMALL}, "_loss": jnp.zeros((128,), F32)})
    sd, sm, sv = _adamw(pk(""), g_small_buf, pk("m_"), pk("v_"), "adamw_small")
    sd, sm, sv = _unpack_small(sd, arrs), _unpack_small(sm, arrs), _unpack_small(sv, arrs)
    for n in _SMALL:
        grads[n], deltas[n], new_m[n], new_v[n] = g_small[n], sd[n], sm[n], sv[n]
    for group in (grads, deltas, new_m, new_v):
        group["w_in"] = jnp.swapaxes(group["w_in"], 1, 2)

    return (g_small["_loss"], dx.reshape(b, t, d), *[grads[n] for n in _WEIGHTS], *[deltas[n] for n in _WEIGHTS],
            *[new_m[n] for n in _WEIGHTS], *[new_v[n] for n in _WEIGHTS])
```

```python
import functools
import math

import jax
import jax.numpy as jnp
from jax import lax
from jax.experimental import pallas as pl
from jax.experimental.pallas import tpu as pltpu

F32, BF16 = jnp.float32, jnp.bfloat16
MESH = pl.DeviceIdType.MESH
ANY = pl.BlockSpec(memory_space=pl.ANY)
VMEM = pl.BlockSpec(memory_space=pltpu.VMEM)
BS = pl.BlockSpec
SDS = jax.ShapeDtypeStruct

NORM_EPS = 1e-6
GN_EPS = 64e-5
L2_EPS = 1e-12
HEADS = 8
HEAD_DIM = 64
D_RWKV = HEADS * HEAD_DIM
ROPE_DIM = 32
ROPE_THETA = 10000.0
MLA_SCALE = (64 + ROPE_DIM) ** -0.5
Q_RANK, KV_RANK = 768, 256
RWKV_COLS = 1920
MLA_PAD_COLS = Q_RANK + KV_RANK + 256
D_FF = 2816
ADAM_LR, ADAM_B1, ADAM_B2, ADAM_EPS, ADAM_WD, ADAM_STEP = 0.001, 0.9, 0.999, 1e-08, 0.01, 10

V7X_LANES = 128
V7X_VMEM_LIMIT = 56 * 1024 * 1024
SCAN_CHUNK = 32
N_CHIPS = 4


def _cp(*sem):
    return pltpu.CompilerParams(dimension_semantics=sem, vmem_limit_bytes=V7X_VMEM_LIMIT)


def _tile(n, cands=(512, 640, 384, 256, 128)):
    for c in cands:
        if n % c == 0:
            return c
    return n


def _row_tile(n, cap=256):
    best = n
    for t in range(8, cap + 1, 8):
        if n % t == 0:
            best = t
    return best if best <= cap or n <= cap else n


def _rms(x, g):
    ms = jnp.mean(x * x, axis=-1, keepdims=True)
    return x * lax.rsqrt(ms + NORM_EPS) * g


@jax.custom_vjp
def _bdot(x, w):
    return jnp.dot(x.astype(BF16), w.astype(BF16), preferred_element_type=F32)


def _bdot_fwd(x, w):
    return _bdot(x, w), (x, w)


def _bdot_bwd(res, ct):
    x, w = res
    c = ct.astype(BF16)
    dx = lax.dot_general(c, w.astype(BF16), (((1,), (1,)), ((), ())), preferred_element_type=F32)
    dw = lax.dot_general(x.astype(BF16), c, (((0,), (0,)), ((), ())), preferred_element_type=F32)
    return dx.astype(x.dtype), dw.astype(w.dtype)


_bdot.defvjp(_bdot_fwd, _bdot_bwd)


@jax.custom_vjp
def _headsum(x, ones_bd):
    hi = x.astype(BF16)
    mid = (x - hi.astype(F32)).astype(BF16)
    ob = ones_bd.astype(BF16)
    return jnp.dot(hi, ob, preferred_element_type=F32) + jnp.dot(mid, ob, preferred_element_type=F32)


def _headsum_fwd(x, ones_bd):
    return _headsum(x, ones_bd), ones_bd


def _headsum_bwd(ones_bd, ct):
    return _headsum(ct, ones_bd), jnp.zeros_like(ones_bd)


_headsum.defvjp(_headsum_fwd, _headsum_bwd)


def _prep_fn(zs, w0, w2, a0, a2, g2, k_k, k_a, ones_bd):
    k = zs[:, 512:1024]
    wd = zs[:, 1536:1664]
    ad = zs[:, 1664:1792]
    gd = zs[:, 1792:1920]
    logit = w0 + _bdot(jnp.tanh(wd), w2)
    w = jnp.exp(-math.exp(-0.5) * jax.nn.sigmoid(logit))
    a = jax.nn.sigmoid(a0 + _bdot(ad, a2))
    g = _bdot(jax.nn.sigmoid(gd), g2)
    kkr = k * k_k
    nrm = jnp.sqrt(_headsum(kkr * kkr, ones_bd))
    kk = kkr / jnp.maximum(nrm, L2_EPS)
    a_f, a_b = a[:, :512], a[:, 512:]
    kf = k * (1.0 + (a_f - 1.0) * k_a)
    kb = k * (1.0 + (a_b - 1.0) * k_a)
    return w[:, :512], w[:, 512:], kf, kb, kk, kk * a_f, kk * a_b, g


def _post_fn(y, r, kf, kb, v, g, r_k, ln_g, ln_b, ones_bd):
    mu =_headsum(y, ones_bd) * (1.0 / HEAD_DIM)
    yc = y - mu
    var = _headsum(yc * yc, ones_bd) * (1.0 / HEAD_DIM)
    yn = yc * lax.rsqrt(var + GN_EPS) * ln_g + ln_b
    bonus = _headsum(r * (kf + kb) * r_k, ones_bd) * v
    return (yn + bonus) * g


def _cat8(x):
    return jnp.concatenate([x] * HEADS, axis=1)


def _mla_fn(zm, cs, sn, gq, gkv, wq, wqr, wk, wv):
    cq = zm[:, :Q_RANK]
    ckv = zm[:, Q_RANK:Q_RANK + KV_RANK]
    kr = zm[:, Q_RANK + KV_RANK:Q_RANK + KV_RANK + 128]
    krr = zm[:, Q_RANK + KV_RANK + 128:]
    cqn = _rms(cq, gq)
    ckvn = _rms(ckv, gkv)
    q = (_bdot(cqn, wq) * _cat8(cs) + _bdot(cqn, wqr) * _cat8(sn)) * MLA_SCALE
    kro = kr * cs + krr * sn
    kfull = _bdot(ckvn, wk) + _cat8(kro)
    v = _bdot(ckvn, wv)
    return q, kfull, v


def _adamw_math(w, g, m, v):
    m2 = ADAM_B1 * m + (1.0 - ADAM_B1) * g
    v2 = ADAM_B2 * v + (1.0 - ADAM_B2) * (g * g)
    m_hat = m2 / (1.0 - ADAM_B1 ** ADAM_STEP)
    v_hat = v2 / (1.0 - ADAM_B2 ** ADAM_STEP)
    delta = -ADAM_LR * (m_hat / (jnp.sqrt(v_hat) + ADAM_EPS) + ADAM_WD * w)
    return delta, m2, v2


_DIMS = {"nn": (((1,), (0,)), ((), ())), "nt": (((1,), (1,)), ((), ())), "tn": (((0,), (0,)), ((), ()))}


def _mm(a, b, mode, name, out_dtype=F32, add=None, column_blocks=None):
    if mode == "nn":
        (m, k), (_, n) = a.shape, b.shape
    elif mode == "nt":
        (m, k), (n, _) = a.shape, b.shape
    else:
        (k, m), (_, n) = a.shape, b.shape
    big = (1024, 1408, 768, 640, 512, 384, 256, 128)
    tm, tn, tk = _tile(m, big), column_blocks or _tile(n, big), _tile(k, (512, 1408, 640, 384, 256, 128))
    nk = k // tk

    def body(a_ref, b_ref, *rest):
        if add is None:
            o_ref, acc_ref = rest
        else:
            add_ref, o_ref, acc_ref = rest
        kk = pl.program_id(2)

        @pl.when(kk == 0)
        def _():
            acc_ref[...] = jnp.zeros_like(acc_ref)

        acc_ref[...] += lax.dot_general(
            a_ref[...].astype(BF16), b_ref[...].astype(BF16), _DIMS[mode], preferred_element_type=F32
        )

        @pl.when(kk == nk - 1)
        def _():
            r = acc_ref[...]
            if add is not None:
                r = r + add_ref[...]
            o_ref[...] = r.astype(out_dtype).reshape(o_ref.shape)

    a_spec = BS((tk, tm), lambda i, j, kk: (kk, i)) if mode == "tn" else BS((tm, tk), lambda i, j, kk: (i, kk))
    b_spec = BS((tn, tk), lambda i, j, kk: (j, kk)) if mode == "nt" else BS((tk, tn), lambda i, j, kk: (kk, j))
    o_spec = BS((tm, tn), lambda i, j, kk: (i, j))
    ins, specs = [a, b], [a_spec, b_spec]
    if add is not None:
        ins.append(add)
        specs.append(o_spec)
    out_shape = SDS((m, n), out_dtype)
    if column_blocks:
        assert add is None
        o_spec, out_shape = BS((1, tm, tn), lambda i, j, kk: (j, i, 0)), SDS((n // tn, m, tn), out_dtype)
    return pl.pallas_call(
        body, grid=(m // tm, n // tn, nk), in_specs=specs, out_specs=o_spec, out_shape=out_shape,
        scratch_shapes=[pltpu.VMEM((tm, tn), F32)], compiler_params=_cp("parallel", "parallel", "arbitrary"), name=name,
    )(*ins)


def _rms_fwd(x, g, name):
    m, d = x.shape
    tr = _tile(m)

    def body(x_ref, g_ref, o_ref):
        o_ref[...] = _rms(x_ref[...], g_ref[...]).astype(BF16)

    return pl.pallas_call(
        body, grid=(m // tr,), in_specs=[BS((tr, d), lambda i: (i, 0)), BS((1, d), lambda i: (0, 0))],
        out_specs=BS((tr, d), lambda i: (i, 0)), out_shape=SDS((m, d), BF16), compiler_params=_cp("parallel"), name=name,
    )(x, g)


def _rms_bwd(x, g, dy, name, dres=None, dy_block=0):
    m, d = x.shape
    tr = _row_tile(m, 512)

    def body(x_ref, g_ref, dy_ref, *rest):
        if dres is None:
            dx_ref, dg_ref = rest
        else:
            dres_ref, dx_ref, dg_ref = rest
        _, vjp = jax.vjp(_rms, x_ref[...], g_ref[...])
        dx, dg = vjp(dy_ref[...])
        if dres is not None:
            dx = dx + dres_ref[...]
        dx_ref[...] = dx

        @pl.when(pl.program_id(0) == 0)
        def _():
            dg_ref[...] = jnp.zeros_like(dg_ref)

        dg_ref[...] += dg

    row = BS((tr, d), lambda i: (i, 0))
    vec = BS((1, d), lambda i: (0, 0))
    ins, specs = [x, g, dy], [row, vec, BS((tr, d), lambda i: (i, dy_block))]
    if dres is not None:
        ins.append(dres)
        specs.append(row)
    return pl.pallas_call(
        body, grid=(m // tr,), in_specs=specs, out_specs=[row, vec], out_shape=[SDS((m, d), F32), SDS((1, d), F32)],
        compiler_params=_cp("arbitrary"), name=name,
    )(*ins)


def _final(h, g, tgt):
    m, d = h.shape
    tr = _row_tile(m, 512)

    def loss_fn(hh, gg, tt):
        e = _rms(hh, gg) - tt
        return 0.5 * jnp.sum(e * e) * (1.0 / d)

    def body(h_ref, g_ref, t_ref, l_ref, dh_ref, dg_ref):
        val, (dh, dg) = jax.value_and_grad(loss_fn, argnums=(0, 1))(h_ref[...], g_ref[...], t_ref[...])
        dh_ref[...] = dh

        @pl.when(pl.program_id(0) == 0)
        def _():
            dg_ref[...] = jnp.zeros_like(dg_ref)
            l_ref[...] = jnp.zeros_like(l_ref)

        dg_ref[...] += dg
        l_ref[...] += jnp.full(l_ref.shape, val, F32)

    row = BS((tr, d), lambda i: (i, 0))
    vec = BS((1, d), lambda i: (0, 0))
    return pl.pallas_call(
        body, grid=(m // tr,), in_specs=[row, vec, row], out_specs=[BS((8, 128), lambda i: (0, 0)), row, vec],
        out_shape=[SDS((8, 128), F32), SDS((m, d), F32), SDS((1, d), F32)], compiler_params=_cp("arbitrary"), name="final_loss",
    )(h, g, tgt)


def _prev_next(z, t):
    row = lax.broadcasted_iota(jnp.int32, z.shape, 0)
    zp = jnp.where(row == 0, 0.0, pltpu.roll(z, 1, axis=0))
    zn = jnp.where(row == t - 1, 0.0, pltpu.roll(z, t - 1, axis=0))
    return zp, zn


def _shift_fwd(z3, mu_p, mu_n):
    b, t, c = z3.shape
    nc = c // 128

    def body(z_ref, mp_ref, mn_ref, o_ref):
        z = z_ref[0]
        zp, zn = _prev_next(z, t)
        o_ref[0] = z + mp_ref[...] * (zp - z) + mn_ref[...] * (zn - z)

    blk = BS((1, t, 128), lambda i, j: (i, 0, j))
    vec = BS((1, 128), lambda i, j: (0, j))
    return pl.pallas_call(
        body, grid=(b, nc), in_specs=[blk, vec, vec], out_specs=blk, out_shape=SDS((b, t, c), F32),
        compiler_params=_cp("parallel", "parallel"), name="shift_fwd",
    )(z3, mu_p, mu_n)


def _shift_bwd(dzs3, z3, mu_p, mu_n):
    b, t, c = z3.shape
    nc = c // 128

    def body(d_ref, z_ref, mp_ref, mn_ref, dz_ref, dmp_ref, dmn_ref):
        d, z = d_ref[0], z_ref[0]
        mp, mn = mp_ref[...], mn_ref[...]
        zp, zn = _prev_next(z, t)
        _, dp_next = _prev_next(d * mp, t)
        dn_prev, _ = _prev_next(d * mn, t)
        dz_ref[0] = (d * (1.0 - mp - mn) + dp_next + dn_prev).astype(BF16)

        @pl.when(pl.program_id(1) == 0)
        def _():
            dmp_ref[...] = jnp.zeros_like(dmp_ref)
            dmn_ref[...] = jnp.zeros_like(dmn_ref)

        dmp_ref[...] += jnp.sum(d * (zp - z), axis=0, keepdims=True)
        dmn_ref[...] += jnp.sum(d * (zn - z), axis=0, keepdims=True)

    blk = BS((1, t, 128), lambda j, i: (i, 0, j))
    vec = BS((1, 128), lambda j, i: (0, j))
    return pl.pallas_call(
        body, grid=(nc, b), in_specs=[blk, blk, vec, vec], out_specs=[blk, vec, vec],
        out_shape=[SDS((b, t, c), BF16), SDS((1, c), F32), SDS((1, c), F32)],
        compiler_params=_cp("parallel", "arbitrary"), name="shift_bwd",
    )(dzs3, z3, mu_p, mu_n)


def _const(shape):
    nd = len(shape)
    return BS(shape, lambda i: (0,) * nd)


def _prep_fwd(zs, p):
    m = zs.shape[0]
    tr = 512
    params = [p["w0"], p["w2"], p["a0"], p["a2"], p["g2"], p["k_k"], p["k_a"], p["ones_bd"]]

    def body(zs_ref, w0, w2, a0, a2, g2, kk_, ka_, bd, wf, wb, kf, kb, kk, kaf, kab, g):
        outs = _prep_fn(zs_ref[...], w0[...], w2[...], a0[...], a2[...], g2[...], kk_[...], ka_[...], bd[...])
        for ref, val in zip((wf, wb, kf, kb, kk, kaf, kab, g), outs):
            ref[...] = val

    row = BS((tr, 512), lambda i: (i, 0))
    return pl.pallas_call(
        body, grid=(m // tr,), in_specs=[BS((tr, RWKV_COLS), lambda i: (i, 0))] + [_const(q.shape) for q in params],
        out_specs=[row] * 8, out_shape=[SDS((m, 512), F32)] * 8, compiler_params=_cp("parallel"), name="rwkv_prep_fwd",
    )(zs, *params)


def _prep_bwd(zs, p, ct_rows):
    m = zs.shape[0]
    tr = 256
    params = [p["w0"], p["w2"], p["a0"], p["a2"], p["g2"], p["k_k"], p["k_a"]]
    names = ["dwf", "dwb", "dkf", "dkb", "dkk_f", "dkk_b", "dkaf", "dkab", "dr_f", "dr_b", "dr_p", "dk_p", "dv_p", "dg",
             "dv_f", "dv_b"]
    rows = [ct_rows[n] for n in names]

    def body(zs_ref, w0, w2, a0, a2, g2, kk_, ka_, bd, *rest):
        c = {n: r[...] for n, r in zip(names, rest[:len(names)])}
        outs = rest[len(names):]
        dzs_ref, grads = outs[0], outs[1:]
        ones_bd = bd[...]
        _, vjp = jax.vjp(
            lambda *q: _prep_fn(*q, ones_bd), zs_ref[...], w0[...], w2[...], a0[...], a2[...], g2[...], kk_[...], ka_[...]
        )
        cts = (c["dwf"], c["dwb"], c["dkf"] + c["dk_p"], c["dkb"] + c["dk_p"], c["dkk_f"] + c["dkk_b"], c["dkaf"], c["dkab"], c["dg"])
        dzs, *dparams = vjp(cts)
        dr = c["dr_f"] + c["dr_b"] + c["dr_p"]
        dv = c["dv_f"] + c["dv_b"] + c["dv_p"]
        dzs_ref[:, 0:512] = dzs[:, 0:512] + dr
        dzs_ref[:, 512:1024] = dzs[:, 512:1024]
        dzs_ref[:, 1024:1536] = dzs[:, 1024:1536] + dv
        dzs_ref[:, 1536:1920] = dzs[:, 1536:1920]

        @pl.when(pl.program_id(0) == 0)
        def _():
            for gr in grads:
                gr[...] = jnp.zeros_like(gr)

        for gr, val in zip(grads, dparams):
            gr[...] += val

    row = BS((tr, 512), lambda i: (i, 0))
    return pl.pallas_call(
        body, grid=(m // tr,),
        in_specs=[BS((tr, RWKV_COLS), lambda i: (i, 0))] + [_const(q.shape) for q in params] + [_const(p["ones_bd"].shape)]
        + [row] * len(names),
        out_specs=[BS((tr, RWKV_COLS), lambda i: (i, 0))] + [_const(q.shape) for q in params],
        out_shape=[SDS((m, RWKV_COLS), F32)] + [SDS(q.shape, F32) for q in params],
        compiler_params=_cp("arbitrary"), name="rwkv_prep_bwd",
    )(zs, *params, p["ones_bd"], *rows)


def _post_specs(tr):
    r = BS((tr, 512), lambda i: (i, 0))
    v = BS((tr, 512), lambda i: (i, 2))
    row = BS((tr, 512), lambda i: (i, 0))
    return r, v, row


def _post_fwd(y_f, y_b, zs, kf, kb, g, p):
    m = zs.shape[0]
    tr = 512
    r, v, row = _post_specs(tr)
    vecs = [p["r_k"], p["ln_x_g"], p["ln_x_b"], p["ones_bd"]]

    def body(yf, yb, r_ref, v_ref, kf_ref, kb_ref, g_ref, rk, lg, lb, bd, o_ref):
        o_ref[...] = _post_fn(
            yf[...] + yb[...], r_ref[...], kf_ref[...], kb_ref[...], v_ref[...], g_ref[...], rk[...], lg[...], lb[...], bd[...]
        ).astype(BF16)

    return pl.pallas_call(
        body, grid=(m // tr,), in_specs=[row, row, r, v, row, row, row] + [_const(q.shape) for q in vecs],
        out_specs=row, out_shape=SDS((m, 512), BF16), compiler_params=_cp("parallel"), name="rwkv_post_fwd",
    )(y_f, y_b, zs, zs, kf, kb, g, *vecs)


def _post_bwd(y_f, y_b, zs, kf, kb, g, p, dymix):
    m = zs.shape[0]
    tr = 256
    r, v, row = _post_specs(tr)
    vecs = [p["r_k"], p["ln_x_g"], p["ln_x_b"]]

    def body(yf, yb, r_ref, v_ref, kf_ref, kb_ref, g_ref, rk, lg, lb, bd, dy_ref, dyo, dr, dk, dv, dg, drk, dlg, dlb):
        ones_bd = bd[...]
        _, vjp = jax.vjp(
            lambda *q: _post_fn(*q, ones_bd),
            yf[...] + yb[...], r_ref[...], kf_ref[...], kb_ref[...], v_ref[...], g_ref[...], rk[...], lg[...], lb[...],
        )
        c_y, c_r, c_kf, _, c_v, c_g, c_rk, c_lg, c_lb = vjp(dy_ref[...])
        dyo[...] = c_y
        dr[...] = c_r
        dk[...] = c_kf
        dv[...] = c_v
        dg[...] = c_g

        @pl.when(pl.program_id(0) == 0)
        def _():
            for ref in (drk, dlg, dlb):
                ref[...] = jnp.zeros_like(ref)

        drk[...] += c_rk
        dlg[...] += c_lg
        dlb[...] += c_lb

    vec = _const((1, 512))
    return pl.pallas_call(
        body, grid=(m // tr,),
        in_specs=[row, row, r, v, row, row, row] + [_const(q.shape) for q in vecs] + [_const(p["ones_bd"].shape), row],
        out_specs=[row, row, row, row, row, vec, vec, vec],
        out_shape=[SDS((m, 512), F32)] * 5 + [SDS((1, 512), F32)] * 3,
        compiler_params=_cp("arbitrary"), name="rwkv_post_bwd",
    )(y_f, y_b, zs, zs, kf, kb, g, *vecs, p["ones_bd"], dymix)


SCAN_MXU_GROUPS = 2


def _half_ones():
    ri = lax.broadcasted_iota(jnp.int32, (128, 128), 0)
    ci = lax.broadcasted_iota(jnp.int32, (128, 128), 1)
    return jnp.where((ri < 64) == (ci < 64), 1.0, 0.0).astype(BF16)


def _half_sums(xs, ones):
    out = []
    per = -(-len(xs) // SCAN_MXU_GROUPS)
    for g in range(0, len(xs), per):
        part = xs[g:g + per]
        res = jnp.dot(jnp.concatenate(part, axis=0).astype(BF16), ones, preferred_element_type=F32)
        out += [res[64 * i:64 * i + 64] for i in range(len(part))]
    return out


def _scan_specs(b, t):
    nc = t // SCAN_CHUNK
    up, down = (lambda c: c), (lambda c: nc - 1 - c)
    rows = [BS((b, SCAN_CHUNK, 512), lambda c, ci=ci: (0, ci(c), 0)) for ci in (up, down)]
    vrows = [BS((b, SCAN_CHUNK, 512), lambda c, ci=ci: (0, ci(c), 2)) for ci in (up, down)]
    hist = [BS((SCAN_CHUNK, b * 4, 64, 128), lambda c, ci=ci: (ci(c), 0, 0, 0)) for ci in (up, down)]
    return nc, rows, vrows, hist


class _Window:
    def __init__(self, g, ascending):
        self.bases = [pl.multiple_of(g * 8, 8) if asc else pl.multiple_of(SCAN_CHUNK - 8 - g * 8, 8) for asc in ascending]
        self.ascending = ascending
        self.blocks = {}
        self.row_id = lax.broadcasted_iota(jnp.int32, (8, 128), 0)

    def j(self, d, s):
        return s if self.ascending[d] else 7 - s

    def time(self, d, s):
        return self.bases[d] + self.j(d, s)

    def row(self, ref, d, bi, cols, s):
        key = (id(ref), d, bi, cols.start)
        if key not in self.blocks:
            self.blocks[key] = ref[bi, pl.ds(self.bases[d], 8), cols]
        jj = self.j(d, s)
        return self.blocks[key][jj:jj + 1, :]

    def put(self, buf, key, d, s, row):
        prev = buf.get(key)
        new = jnp.broadcast_to(row, (8, 128))
        buf[key] = new if prev is None else jnp.where(self.row_id == self.j(d, s), new, prev)

    def flush(self, buf, refs_of):
        for key, val in buf.items():
            ref, d, bi, cols = refs_of(key)
            ref[bi, pl.ds(self.bases[d], 8), cols] = val


def _pairs(b):
    return [(bi * 4 + p, bi, slice(128 * p, 128 * p + 128)) for bi in range(b) for p in range(4)]


def _colsum(x):
    return jnp.sum(x, axis=0, keepdims=True)


def _pair_matvec(row, mat):
    rid = lax.broadcasted_iota(jnp.int32, (8, 64), 0)
    lhs = jnp.where(rid == 0, row[:, :64], jnp.where(rid == 1, row[:, 64:], 0.0))
    out = jnp.dot(lhs.astype(BF16), mat.astype(BF16), preferred_element_type=F32)
    lo = lax.broadcasted_iota(jnp.int32, (1, 128), 1) < 64
    return jnp.where(lo, out[0:1], out[1:2])


def _eye_mask():
    return (lax.broadcasted_iota(jnp.int32, (64, 128), 1) & 63) == lax.broadcasted_iota(jnp.int32, (64, 128), 0)


def _scan_fwd(zs, kk, ops_f, ops_b, shard):
    b, t = zs.shape[:2]
    nc, rows, vrows, hist = _scan_specs(b, t)
    npair = b * 4

    def body(*refs):
        ins, shard_ref, outs, s_ref = refs[:12], refs[12], refs[13:17], refs[17]
        gather = (shard_ref, *refs[18:21])
        dirs = [dict(zip(("r", "kk", "v", "w", "k", "ka", "y", "h"), (*ins[6 * d:6 * d + 6], *outs[2 * d:2 * d + 2])))
                for d in (0, 1)]

        @pl.when(pl.program_id(0) == 0)
        def _():
            s_ref[...] = jnp.zeros_like(s_ref)
            _gather_halves(*gather, "start")

        @pl.when(pl.program_id(0) == nc - 1)
        def _():
            _gather_halves(*gather, "finish")

        ones, eye = _half_ones(), _eye_mask()
        chains = [(d, pr, bi, cols) for d in (0, 1) for pr, bi, cols in _pairs(b)]

        def eight_steps(g, carry):
            win = _Window(g, (True, False))
            ybuf = {}
            for s in range(8):
                s_prev, xa = [], []
                for d, pr, bi, cols in chains:
                    q = dirs[d]
                    st = s_ref[d * npair + pr]
                    q["h"][win.time(d, s), pr] = st
                    s_prev.append(st)
                    xa += [st * win.row(q["kk"], d, bi, cols, s), jnp.where(eye, win.row(q["v"], d, bi, cols, s), 0.0)]
                ra = _half_sums(xa, ones)
                xb = []
                for i, (d, pr, bi, cols) in enumerate(chains):
                    q = dirs[d]
                    s_new = s_prev[i] * win.row(q["w"], d, bi, cols, s) - ra[2 * i] * win.row(q["ka"], d, bi, cols, s) \
                        + ra[2 * i + 1] * win.row(q["k"], d, bi, cols, s)
                    s_ref[d * npair + pr] = s_new
                    xb.append(s_new * win.row(q["r"], d, bi, cols, s))
                rb = _half_sums(xb, ones)
                for i, (d, pr, bi, cols) in enumerate(chains):
                    win.put(ybuf, i, d, s, _colsum(jnp.where(eye, rb[i], 0.0)))
            win.flush(ybuf, lambda i: (dirs[chains[i][0]]["y"], chains[i][0], chains[i][2], chains[i][3]))
            return carry

        lax.fori_loop(0, SCAN_CHUNK // 8, eight_steps, 0)

    row_shape, hist_shape = SDS((b, t, 512), F32), SDS((t, npair, 64, 128), F32)
    state = (2 * npair, 64, 128)
    return pl.pallas_call(
        body, grid=(nc,), in_specs=sum(([rows[d], rows[d], vrows[d]] + [rows[d]] * 3 for d in (0, 1)), []) + [ANY],
        out_specs=[rows[0], hist[0], rows[1], hist[1], BS(state, lambda c: (0, 0, 0)), ANY],
        out_shape=[row_shape, hist_shape, row_shape, hist_shape, SDS(state, F32), _gathered_shape(shard)],
        scratch_shapes=_gather_sems(), compiler_params=_cp("arbitrary"), name="wkv_scan",
    )(zs, kk, zs, *ops_f, zs, kk, zs, *ops_b, shard)


def _scan_bwd(zs, kk, dy, ops_f, hist_f, ops_b, hist_b, s_last, partials):
    b, t = zs.shape[:2]
    nc, rows, vrows, hist = _scan_specs(b, t)
    npair = b * 4
    names_in = ("r", "kk", "v", "dy", "w", "k", "ka", "h")
    names_out = ("dr", "dw", "dk", "dkk", "dka", "dv")

    def body(*refs):
        ins, last_ref, part_ref, outs, recv_ref = refs[:16], refs[16], refs[17], refs[18:30], refs[30]
        ds_ref, after_ref = refs[31], refs[32]
        scatter = (part_ref, recv_ref, refs[33], refs[34])
        dirs = [dict(zip(names_in + names_out, (*ins[8 * d:8 * d + 8], *outs[6 * d:6 * d + 6]))) for d in (0, 1)]

        @pl.when(pl.program_id(0) == 0)
        def _():
            ds_ref[...] = jnp.zeros_like(ds_ref)
            after_ref[...] = last_ref[...]
            _scatter_to_all(*scatter, "start")

        @pl.when(pl.program_id(0) == nc - 1)
        def _():
            _scatter_to_all(*scatter, "finish")

        ones, eye = _half_ones(), _eye_mask()
        chains = [(d, pr, bi, cols) for d in (0, 1) for pr, bi, cols in _pairs(b)]

        def eight_steps(g, carry):
            win = _Window(g, (False, True))
            obuf = {}
            s_after = [after_ref[d * npair + pr] for d, pr, _, _ in chains]
            for s in range(8):
                row = lambda name, d, bi, cols: win.row(dirs[d][name], d, bi, cols, s)
                s_prev, xa = [], []
                for d, pr, bi, cols in chains:
                    st = dirs[d]["h"][win.time(d, s), pr]
                    s_prev.append(st)
                    xa += [st * row("kk", d, bi, cols), jnp.where(eye, row("dy", d, bi, cols), 0.0)]
                ra = _half_sums(xa, ones)
                ds_now, xb = [], []
                for i, (d, pr, bi, cols) in enumerate(chains):
                    skk, dycol = ra[2 * i], ra[2 * i + 1]
                    ds = ds_ref[d * npair + pr] + dycol * row("r", d, bi, cols)
                    win.put(obuf, (i, "dr"), d, s, _pair_matvec(row("dy", d, bi, cols), s_after[i]))
                    win.put(obuf, (i, "dk"), d, s, _pair_matvec(row("v", d, bi, cols), ds))
                    win.put(obuf, (i, "dka"), d, s, -_colsum(ds * skk))
                    win.put(obuf, (i, "dw"), d, s, _colsum(ds * s_prev[i]))
                    ds_now.append(ds)
                    xb += [ds * row("k", d, bi, cols), ds * row("ka", d, bi, cols)]
                rb = _half_sums(xb, ones)
                for i, (d, pr, bi, cols) in enumerate(chains):
                    dskk_neg = rb[2 * i + 1]
                    win.put(obuf, (i, "dv"), d, s, _colsum(jnp.where(eye, rb[2 * i], 0.0)))
                    win.put(obuf, (i, "dkk"), d, s, -_colsum(s_prev[i] * dskk_neg))
                    ds_ref[d * npair + pr] = ds_now[i] * row("w", d, bi, cols) - dskk_neg * row("kk", d, bi, cols)
                s_after = s_prev
            for i, (d, pr, _, _) in enumerate(chains):
                after_ref[d * npair + pr] = s_after[i]
            win.flush(obuf, lambda key: (dirs[chains[key[0]][0]][key[1]], chains[key[0]][0], chains[key[0]][2], chains[key[0]][3]))
            return carry

        lax.fori_loop(0, SCAN_CHUNK // 8, eight_steps, 0)

    row_shape = SDS((b, t, 512), F32)
    state = (2 * npair, 64, 128)
    return pl.pallas_call(
        body, grid=(nc,),
        in_specs=sum(([rows[d], rows[d], vrows[d]] + [rows[d]] * 4 + [hist[d]] for d in (1, 0)), [])
        + [BS(state, lambda c: (0, 0, 0)), ANY],
        out_specs=[rows[1]] * 6 + [rows[0]] * 6 + [ANY],
        out_shape=[row_shape] * 12 + [SDS((7, partials.shape[1] // 2, 128), partials.dtype)],
        scratch_shapes=[pltpu.VMEM(state, F32), pltpu.VMEM(state, F32), pltpu.SemaphoreType.DMA((7,)),
                        pltpu.SemaphoreType.DMA((7,))],
        compiler_params=_cp("arbitrary"), name="wkv_scan_bwd",
    )(zs, kk, zs, dy, *ops_f, hist_f, zs, kk, zs, dy, *ops_b, hist_b, s_last, partials)


def _mla_fwd(zm, cs, sn, p, t):
    m = zm.shape[0]
    tr = 512
    per = t // tr
    params = [p["q_norm_g"], p["kv_norm_g"], p["wq"], p["wqr"], p["wk"], p["wv"]]

    def body(z_ref, cs_ref, sn_ref, gq, gkv, wq, wqr, wk, wv, q_ref, k_ref, v_ref):
        q, kf, v = _mla_fn(z_ref[...], cs_ref[...], sn_ref[...], gq[...], gkv[...], wq[...], wqr[...], wk[...], wv[...])
        q_ref[...] = q.astype(BF16)
        k_ref[...] = kf.astype(BF16)
        v_ref[...] = v.astype(BF16)

    tab = BS((tr, 128), lambda i: (i % per, 0))
    return pl.pallas_call(
        body, grid=(m // tr,), in_specs=[BS((tr, MLA_PAD_COLS), lambda i: (i, 0)), tab, tab] + [_const(q.shape) for q in params],
        out_specs=[BS((tr, 1024), lambda i: (i, 0)), BS((tr, 1024), lambda i: (i, 0)), BS((tr, 512), lambda i: (i, 0))],
        out_shape=[SDS((m, 1024), BF16), SDS((m, 1024), BF16), SDS((m, 512), BF16)], compiler_params=_cp("parallel"), name="mla_prep_fwd",
    )(zm, cs, sn, *params)


def _mla_bwd(zm, cs, sn, p, t, dq, dk, dv):
    m = zm.shape[0]
    tr = 256
    per = t // tr
    params = [p["q_norm_g"], p["kv_norm_g"], p["wq"], p["wqr"], p["wk"], p["wv"]]

    def body(z_ref, cs_ref, sn_ref, gq, gkv, wq, wqr, wk, wv, dq_ref, dk_ref, dv_ref, dz_ref, *grads):
        cs_v, sn_v = cs_ref[...], sn_ref[...]
        _, vjp = jax.vjp(
            lambda *q: _mla_fn(q[0], cs_v, sn_v, *q[1:]), z_ref[...], gq[...], gkv[...], wq[...], wqr[...], wk[...], wv[...]
        )
        dz, *dparams = vjp((dq_ref[...], dk_ref[...], dv_ref[...]))
        dz_ref[...] = dz.astype(BF16)

        @pl.when(pl.program_id(0) == 0)
        def _():
            for gr in grads:
                gr[...] = jnp.zeros_like(gr)

        for gr, val in zip(grads, dparams):
            gr[...] += val

    tab = BS((tr, 128), lambda i: (i % per, 0))
    wide = BS((tr, 1024), lambda i: (i, 0))
    return pl.pallas_call(
        body, grid=(m // tr,),
        in_specs=[BS((tr, MLA_PAD_COLS), lambda i: (i, 0)), tab, tab] + [_const(q.shape) for q in params]
        + [wide, wide, BS((tr, 512), lambda i: (i, 0))],
        out_specs=[BS((tr, MLA_PAD_COLS), lambda i: (i, 0))] + [_const(q.shape) for q in params],
        out_shape=[SDS((m, MLA_PAD_COLS), BF16)] + [SDS(q.shape, F32) for q in params],
        compiler_params=_cp("arbitrary"), name="mla_prep_bwd",
    )(zm, cs, sn, *params, dq, dk, dv)


_NT = (((1,), (1,)), ((), ()))
_TN = (((0,), (0,)), ((), ()))


def _attn_fwd(q, kf, v, b, t):
    m = q.shape[0]
    tq = 256
    nq = t // tq

    def body(q_ref, k_ref, v_ref, o_ref, l_ref):
        lo = lax.broadcasted_iota(jnp.int32, (1, 128), 1) < 64
        v_all = v_ref[...]
        o = jnp.zeros((tq, 128), F32)
        lse = []
        for h in range(2):
            hs = slice(128 * h, 128 * h + 128)
            s = lax.dot_general(q_ref[:, hs], k_ref[:, hs], _NT, preferred_element_type=F32)
            mx = jnp.max(s, axis=1, keepdims=True)
            e = jnp.exp(s - mx)
            den = jnp.sum(e, axis=1, keepdims=True)
            vh = jnp.where(lo if h == 0 else jnp.logical_not(lo), v_all, jnp.zeros_like(v_all))
            o = o + jnp.dot(e.astype(BF16), vh, preferred_element_type=F32) / den
            lse.append(mx + jnp.log(den))
        o_ref[...] = o
        l_ref[...] = jnp.where(lo, lse[0], lse[1])

    return pl.pallas_call(
        body, grid=(b, 4, nq),
        in_specs=[BS((tq, 256), lambda bi, hp, i: (bi * nq + i, hp)), BS((t, 256), lambda bi, hp, i: (bi, hp)),
                  BS((t, 128), lambda bi, hp, i: (bi, hp))],
        out_specs=[BS((tq, 128), lambda bi, hp, i: (bi * nq + i, hp))] * 2,
        out_shape=[SDS((m, 512), F32), SDS((m, 512), F32)], compiler_params=_cp("parallel", "parallel", "arbitrary"), name="attn_fwd",
    )(q, kf, v)


def _attn_bwd(q, kf, v, o, lse, do, b, t):
    m = q.shape[0]
    tq = 256
    nq = t // tq

    def body(q_ref, k_ref, v_ref, o_ref, l_ref, do_ref, dq_ref, dk_ref, dv_ref):
        lo = lax.broadcasted_iota(jnp.int32, (1, 128), 1) < 64

        @pl.when(pl.program_id(2) == 0)
        def _():
            dk_ref[...] = jnp.zeros_like(dk_ref)
            dv_ref[...] = jnp.zeros_like(dv_ref)

        v_all, o_all, l_all, do_all = v_ref[...], o_ref[...], l_ref[...], do_ref[...]
        dv_acc = jnp.zeros((t, 128), F32)
        for h in range(2):
            hs = slice(128 * h, 128 * h + 128)
            mask = lo if h == 0 else jnp.logical_not(lo)
            qh, kh = q_ref[:, hs], k_ref[:, hs]
            s = lax.dot_general(qh, kh, _NT, preferred_element_type=F32)
            lse_h = jnp.max(jnp.where(mask, l_all, -jnp.inf), axis=1, keepdims=True)
            pr = jnp.exp(s - lse_h)
            do_h = jnp.where(mask, do_all, 0.0)
            dp = lax.dot_general(do_h.astype(BF16), v_all, _NT, preferred_element_type=F32)
            dsum = jnp.sum(do_h * o_all, axis=1, keepdims=True)
            ds = (pr * (dp - dsum)).astype(BF16)
            dq_ref[:, hs] = jnp.dot(ds, kh, preferred_element_type=F32)
            dk_ref[:, hs] += lax.dot_general(ds, qh, _TN, preferred_element_type=F32)
            dv_acc = dv_acc + lax.dot_general(pr.astype(BF16), do_h.astype(BF16), _TN, preferred_element_type=F32)
        dv_ref[...] += dv_acc

    qspec = BS((tq, 256), lambda bi, hp, i: (bi * nq + i, hp))
    kspec = BS((t, 256), lambda bi, hp, i: (bi, hp))
    vspec = BS((t, 128), lambda bi, hp, i: (bi, hp))
    ospec = BS((tq, 128), lambda bi, hp, i: (bi * nq + i, hp))
    return pl.pallas_call(
        body, grid=(b, 4, nq), in_specs=[qspec, kspec, vspec, ospec, ospec, ospec], out_specs=[qspec, kspec, vspec],
        out_shape=[SDS((m, 1024), F32), SDS((m, 1024), F32), SDS((m, 512), F32)],
        compiler_params=_cp("parallel", "parallel", "arbitrary"), name="attn_bwd",
    )(q, kf, v, o, lse, do)


def _conv3(u, w_ref, b_ref, t):
    up, un = _prev_next(u, t)
    return w_ref[0:1, :] * up + w_ref[1:2, :] * u + w_ref[2:3, :] * un + b_ref[...], up, un


def _ffn_mid_fwd(ug3, uv3, cw, cb):
    b, t, f = ug3.shape
    nc = f // 256

    def body(ug_ref, uv_ref, wg_ref, wv_ref, bg_ref, bv_ref, a_ref):
        gc, _, _ = _conv3(ug_ref[0], wg_ref, bg_ref, t)
        vc, _, _ = _conv3(uv_ref[0], wv_ref, bv_ref, t)
        a_ref[0] = (gc * jax.nn.sigmoid(gc) * vc).astype(BF16)

    blk = BS((1, t, 256), lambda i, j: (i, 0, j))
    return pl.pallas_call(
        body, grid=(b, nc),
        in_specs=[blk, blk, BS((3, 256), lambda i, j: (0, j)), BS((3, 256), lambda i, j: (0, j + nc)),
                  BS((1, 256), lambda i, j: (0, j)), BS((1, 256), lambda i, j: (0, j + nc))],
        out_specs=blk, out_shape=SDS((b, t, f), BF16), compiler_params=_cp("parallel", "parallel"), name="ffn_mid_fwd",
    )(ug3, uv3, cw, cw, cb, cb)


def _ffn_mid_bwd(ug3, uv3, cw, cb, da3):
    b, t, f = ug3.shape
    nc = f // 256

    def half(u, up, un, dc, w_ref):
        dprev, dnext = _prev_next(dc, t)
        du = w_ref[1:2, :] * dc + w_ref[0:1, :] * dnext + w_ref[2:3, :] * dprev
        sums = [jnp.sum(dc * q, axis=0, keepdims=True) for q in (up, u, un)] + [jnp.sum(dc, axis=0, keepdims=True)]
        row = lax.broadcasted_iota(jnp.int32, (8, 256), 0)
        tab = jnp.zeros((8, 256), F32)
        for i, s in enumerate(sums):
            tab = jnp.where(row == i, s, tab)
        return du, tab

    def body(ug_ref, uv_ref, wg_ref, wv_ref, bg_ref, bv_ref, da_ref, dug_ref, duv_ref, tg_ref, tv_ref):
        ug, uv, da = ug_ref[0], uv_ref[0], da_ref[0]
        gc, gp, gn = _conv3(ug, wg_ref, bg_ref, t)
        vc, vp, vn = _conv3(uv, wv_ref, bv_ref, t)
        sg = jax.nn.sigmoid(gc)
        d_gc = da * vc * (sg * (1.0 + gc * (1.0 - sg)))
        d_vc = da * (gc * sg)
        dug, tg = half(ug, gp, gn, d_gc, wg_ref)
        duv, tv = half(uv, vp, vn, d_vc, wv_ref)
        dug_ref[0] = dug.astype(BF16)
        duv_ref[0] = duv.astype(BF16)

        @pl.when(pl.program_id(1) == 0)
        def _():
            tg_ref[...] = jnp.zeros_like(tg_ref)
            tv_ref[...] = jnp.zeros_like(tv_ref)

        tg_ref[...] += tg
        tv_ref[...] += tv

    blk = BS((1, t, 256), lambda j, i: (i, 0, j))
    tab = BS((8, 256), lambda j, i: (0, j))
    return pl.pallas_call(
        body, grid=(nc, b),
        in_specs=[blk, blk, BS((3, 256), lambda j, i: (0, j)), BS((3, 256), lambda j, i: (0, j + nc)),
                  BS((1, 256), lambda j, i: (0, j)), BS((1, 256), lambda j, i: (0, j + nc)), blk],
        out_specs=[blk, blk, tab, tab],
        out_shape=[SDS((b, t, f), BF16), SDS((b, t, f), BF16), SDS((8, f), F32), SDS((8, f), F32)],
        compiler_params=_cp("parallel", "arbitrary"), name="ffn_mid_bwd",
    )(ug3, uv3, cw, cw, cb, cb, da3)


def _add_rows(parts, name, out_dtype=F32):
    r = parts[0].shape[0]
    tr = _row_tile(r, 1024)
    n = len(parts)

    def body(*refs):
        acc = refs[0][...].astype(F32)
        for q in refs[1:n]:
            acc = acc + q[...].astype(F32)
        refs[n][...] = acc.astype(out_dtype)

    row = BS((tr, 128), lambda i: (i, 0))
    return pl.pallas_call(
        body, grid=(r // tr,), in_specs=[row] * n, out_specs=row, out_shape=SDS((r, 128), out_dtype),
        compiler_params=_cp("parallel"), name=name,
    )(*parts)


def _adamw(w, g, m, v, name):
    lead = w.shape[:-2]
    r, c = w.shape[-2:]
    tr = _row_tile(r)

    def body(w_ref, g_ref, m_ref, v_ref, d_ref, m2_ref, v2_ref):
        d, m2, v2 = _adamw_math(w_ref[...], g_ref[...], m_ref[...], v_ref[...])
        d_ref[...] = d
        m2_ref[...] = m2
        v2_ref[...] = v2

    blk = BS((1,) * len(lead) + (tr, c), lambda i: (0,) * len(lead) + (i, 0))
    return pl.pallas_call(
        body, grid=(r // tr,), in_specs=[blk] * 4, out_specs=[blk] * 3, out_shape=[SDS(w.shape, F32)] * 3,
        compiler_params=_cp("parallel"), name=name,
    )(w, g, m, v)


def _place():
    return lax.axis_index("x"), lax.axis_index("y"), lax.axis_index("c")


def _flip(v, bit):
    return 1 - v if bit else v


def _allgather_weights(shard):
    def body(x_ref, out_ref, send_sems, recv_sems):
        _gather_halves(x_ref, out_ref, send_sems, recv_sems, "start")
        _gather_halves(x_ref, out_ref, send_sems, recv_sems, "finish")

    return pl.pallas_call(
        body, out_shape=_gathered_shape(shard), in_specs=[ANY], out_specs=ANY, scratch_shapes=_gather_sems(),
        name="allgather_weights",
    )(shard)


def _gathered_shape(shard):
    return SDS((8 * (shard.shape[0] // 2), 128), shard.dtype)


def _gather_sems():
    return [pltpu.SemaphoreType.DMA((6,)), pltpu.SemaphoreType.DMA((6,))]


def _gather_halves(x_ref, out_ref, send_sems, recv_sems, phase):
    rh = x_ref.shape[0] // 2
    x, y, c = _place()
    me, sibling = (x, y, c), (x, y, 1 - c)
    chips = [(1 - x, y), (x, 1 - y), (1 - x, 1 - y)]
    mine_src = x_ref.at[pl.ds(c * rh, rh), :]

    def rows(px, py, pc):
        return out_ref.at[pl.ds((4 * px + 2 * py + pc) * rh, rh), :]

    def copy(k, block, to, src=None):
        return pltpu.make_async_remote_copy(
            src_ref=rows(*block) if src is None else src, dst_ref=rows(*block), send_sem=send_sems.at[k],
            recv_sem=recv_sems.at[k], device_id=to, device_id_type=MESH,
        )

    first = [copy(j, me, (*chip, c), src=mine_src) for j, chip in enumerate(chips)]
    if phase == "start":
        for cp in first:
            cp.start()
        return
    passed = [copy(3 + j, (*chip, c), sibling) for j, chip in enumerate(chips)]
    for j, chip in enumerate(chips):
        copy(j, (*chip, c), me).wait_recv()
        passed[j].start()
    for j, chip in enumerate(chips):
        copy(3 + j, (*chip, 1 - c), me).wait_recv()
    for cp in first + passed:
        cp.wait_send()


def _scatter_partials(g_ref, recv_ref, send_sems, recv_sems, phase):
    x, y, c = _place()
    copies = []
    for j, (fx, fy) in enumerate(((1, 0), (0, 1), (1, 1))):
        px, py = _flip(x, fx), _flip(y, fy)
        copies.append(pltpu.make_async_remote_copy(
            src_ref=g_ref.at[2 * px + py], dst_ref=recv_ref.at[j], send_sem=send_sems.at[j], recv_sem=recv_sems.at[j],
            device_id=(px, py, c), device_id_type=MESH,
        ))
    if phase == "start":
        for cp in copies:
            cp.start()
        return
    for cp in copies:
        cp.wait_recv()
    for cp in copies:
        cp.wait_send()


def _scatter_sems():
    return [pltpu.SemaphoreType.DMA((3,)), pltpu.SemaphoreType.DMA((3,))]


def _scatter_to_all(g_ref, recv_ref, send_sems, recv_sems, phase):
    rh = g_ref.shape[1] // 2
    x, y, c = _place()
    copies = []
    for k in range(1, 8):
        px, py, pc = _flip(x, k >> 2 & 1), _flip(y, k >> 1 & 1), _flip(c, k & 1)
        copies.append(pltpu.make_async_remote_copy(
            src_ref=g_ref.at[2 * px + py, pl.ds(pc * rh, rh), :], dst_ref=recv_ref.at[k - 1], send_sem=send_sems.at[k - 1],
            recv_sem=recv_sems.at[k - 1], device_id=(px, py, pc), device_id_type=MESH,
        ))
    if phase == "start":
        for cp in copies:
            cp.start()
        return
    for cp in copies:
        cp.wait_recv()
    for cp in copies:
        cp.wait_send()


def _scatter_to_chips(g):
    def body(g_ref, recv_ref, send_sems, recv_sems):
        _scatter_partials(g_ref, recv_ref, send_sems, recv_sems, "start")
        _scatter_partials(g_ref, recv_ref, send_sems, recv_sems, "finish")

    return pl.pallas_call(
        body, out_shape=SDS((3,) + g.shape[1:], g.dtype), in_specs=[ANY], out_specs=ANY,
        scratch_shapes=[pltpu.SemaphoreType.DMA((3,)), pltpu.SemaphoreType.DMA((3,))], name="scatter_grads",
    )(g)


def _send_to_sibling(a, half_of_rows):
    rh = a.shape[1] // 2

    def body(a_ref, b_ref, send_sem, recv_sem):
        x, y, c = _place()
        src = a_ref.at[:, pl.ds((1 - c) * rh, rh), :] if half_of_rows else a_ref
        cp = pltpu.make_async_remote_copy(
            src_ref=src, dst_ref=b_ref, send_sem=send_sem, recv_sem=recv_sem, device_id=(x, y, 1 - c), device_id_type=MESH
        )
        cp.start()
        cp.wait()

    shape = (a.shape[0], rh, 128) if half_of_rows else a.shape
    return pl.pallas_call(
        body, out_shape=SDS(shape, a.dtype), in_specs=[ANY], out_specs=ANY,
        scratch_shapes=[pltpu.SemaphoreType.DMA, pltpu.SemaphoreType.DMA],
        name="sibling_halves" if half_of_rows else "sibling_swap",
    )(a)


def _allreduce_small(v):
    r = v.shape[0]

    def body(v_ref, out_ref, buf_ref, send_sems, recv_sems):
        x, y, c = _place()
        buf_ref[0] = v_ref[...]
        copies = []
        for k in range(1, 8):
            peer = (_flip(x, k >> 2 & 1), _flip(y, k >> 1 & 1), _flip(c, k & 1))
            cp = pltpu.make_async_remote_copy(
                src_ref=v_ref, dst_ref=buf_ref.at[k], send_sem=send_sems.at[k - 1], recv_sem=recv_sems.at[k - 1],
                device_id=peer, device_id_type=MESH,
            )
            cp.start()
            copies.append(cp)
        for cp in copies:
            cp.wait_recv()
        acc = None
        for d in range(8):
            slot = 4 * _flip(x, d >> 2 & 1) + 2 * _flip(y, d >> 1 & 1) + _flip(c, d & 1)
            term = buf_ref[slot]
            acc = term if acc is None else acc + term
        out_ref[...] = acc
        for cp in copies:
            cp.wait_send()

    return pl.pallas_call(
        body, out_shape=SDS(v.shape, F32), in_specs=[VMEM], out_specs=VMEM,
        scratch_shapes=[pltpu.VMEM((8, r, 128), F32), pltpu.SemaphoreType.DMA((7,)), pltpu.SemaphoreType.DMA((7,))],
        name="allreduce_small",
    )(v)


_BIG_A = (
    ("w_in", 0, False),
    ("decay_w2_fwd", 1, False), ("decay_w2_bwd", 1, False), ("iclr_a2_fwd", 1, False),
    ("iclr_a2_bwd", 1, False), ("gate_g2", 1, False),
)
_BIG_B = (
    ("w_uq", 0, False), ("w_ukv", 1, False), ("w_out", 0, False), ("w_ffn_up", 1, False), ("ffn_conv_w", 1, True),
    ("w_ffn_down", 0, False),
)
_BIG = _BIG_A + _BIG_B
_SMALL = (
    "ln_mix_g", "shift_mu_prev", "shift_mu_next", "decay_w0_fwd", "decay_w0_bwd", "iclr_a0_fwd", "iclr_a0_bwd", "k_k",
    "k_a", "r_k", "ln_x_g", "ln_x_b", "q_norm_g", "kv_norm_g", "mla_out_g", "ln_ffn_g", "ffn_conv_b", "ln_final_g",
)
_WEIGHTS = (
    "ln_mix_g", "w_in", "shift_mu_prev", "shift_mu_next", "decay_w0_fwd", "decay_w2_fwd", "decay_w0_bwd", "decay_w2_bwd",
    "iclr_a0_fwd", "iclr_a2_fwd", "iclr_a0_bwd", "iclr_a2_bwd", "gate_g2", "k_k", "k_a", "r_k", "ln_x_g", "ln_x_b",
    "q_norm_g", "w_uq", "kv_norm_g", "w_ukv", "mla_out_g", "w_out", "ln_ffn_g", "w_ffn_up", "ffn_conv_w", "ffn_conv_b",
    "w_ffn_down", "ln_final_g",
)


def _pad_rows(flat, rows):
    return jnp.pad(flat, (0, rows * 128 - flat.shape[0])).reshape(rows, 128)


def _rows_for(n, mult):
    rows = -(-n // 128)
    return -(-rows // mult) * mult


def _pack_shards_bf16(arrs, entries):
    parts = []
    for name, _, raw in entries:
        w = arrs[name][0]
        flat = lax.bitcast_convert_type(w, BF16).reshape(-1) if raw else w.astype(BF16).reshape(-1)
        parts.append(_pad_rows(flat, _rows_for(flat.shape[0], 32)))
    return jnp.concatenate(parts, axis=0)


def _unpack_gathered(g4, arrs, entries):
    out, off = {}, 0
    for name, axis, raw in entries:
        a, b = arrs[name].shape[1:]
        n = a * b * (2 if raw else 1)
        rows = _rows_for(n, 32)
        seg = g4[:, off:off + rows].reshape(4, rows * 128)[:, :n]
        off += rows
        if raw:
            seg = lax.bitcast_convert_type(seg.reshape(4, a * b, 2), F32)
        seg = seg.reshape(4, a, b)
        out[name] = jnp.concatenate([seg[s] for s in range(4)], axis=1) if axis == 1 else seg.reshape(4 * a, b)
    return out


def _pack_grads(full, arrs, entries, dtype=F32):
    parts = []
    for name, axis, _ in entries:
        a, b = arrs[name].shape[1:]
        g = full[name]
        if g.ndim == 3:
            sh = g
        else:
            sh = g.reshape(a, 4, b).transpose(1, 0, 2) if axis == 1 else g.reshape(4, a, b)
        rows = _rows_for(a * b, 8)
        flat = sh.reshape(4, a * b).astype(dtype)
        parts.append(jnp.pad(flat, ((0, 0), (0, rows * 128 - a * b))).reshape(4, rows, 128))
    total = sum(q.shape[1] for q in parts)
    parts.append(jnp.zeros((4, -(-total // 1024) * 1024 - total, 128), dtype))
    return jnp.concatenate(parts, axis=1)


def _unpack_grads(g, arrs, entries):
    out, off = {}, 0
    for name, _, _ in entries:
        a, b = arrs[name].shape[1:]
        rows = _rows_for(a * b, 8)
        out[name] = g[off:off + rows].reshape(-1)[:a * b].reshape(1, a, b)
        off += rows
    return out


def _pack_small(vals):
    flat = jnp.concatenate([vals[n].reshape(-1).astype(F32) for n in _SMALL] + [vals["_loss"].reshape(-1)])
    return _pad_rows(flat, _rows_for(flat.shape[0], 8))


def _unpack_small(buf, arrs):
    flat, out, off = buf.reshape(-1), {}, 0
    for n in _SMALL:
        size = arrs[n].size
        out[n] = flat[off:off + size].reshape(arrs[n].shape)
        off += size
    out["_loss"] = flat[off]
    return out


def _rot_cols(w):
    return jnp.concatenate([-w[..., 16:], w[..., :16]], axis=-1)


def _rot_cols_t(g):
    return jnp.concatenate([g[..., 16:], -g[..., :16]], axis=-1)


def _rope_tables(t):
    inv = jnp.power(ROPE_THETA, -jnp.arange(0, ROPE_DIM, 2, dtype=F32) / ROPE_DIM)
    ang = jnp.arange(t, dtype=F32)[:, None] * inv[None, :]
    one, zero = jnp.ones((t, 64), F32), jnp.zeros((t, 64), F32)
    cs = jnp.concatenate([one, jnp.cos(ang), jnp.cos(ang), zero[:, :32]], axis=1)
    sn = jnp.concatenate([zero, jnp.sin(ang), jnp.sin(ang), zero[:, :32]], axis=1)
    return cs, sn


def _block_diag(a, b):
    za = jnp.zeros_like(a)
    return jnp.concatenate([jnp.concatenate([a, za], axis=1), jnp.concatenate([za, b], axis=1)], axis=0)


def kernel(x, ln_mix_g, w_in, shift_mu_prev, shift_mu_next, decay_w0_fwd, decay_w2_fwd, decay_w0_bwd, decay_w2_bwd, iclr_a0_fwd, iclr_a2_fwd, iclr_a0_bwd, iclr_a2_bwd, gate_g2, k_k, k_a, r_k, ln_x_g, ln_x_b, q_norm_g, w_uq, kv_norm_g, w_ukv, mla_out_g, w_out, ln_ffn_g, w_ffn_up, ffn_conv_w, ffn_conv_b, w_ffn_down, ln_final_g, loss_target, m_ln_mix_g, m_w_in, m_shift_mu_prev, m_shift_mu_next, m_decay_w0_fwd, m_decay_w2_fwd, m_decay_w0_bwd, m_decay_w2_bwd, m_iclr_a0_fwd, m_iclr_a2_fwd, m_iclr_a0_bwd, m_iclr_a2_bwd, m_gate_g2, m_k_k, m_k_a, m_r_k, m_ln_x_g, m_ln_x_b, m_q_norm_g, m_w_uq, m_kv_norm_g, m_w_ukv, m_mla_out_g, m_w_out, m_ln_ffn_g, m_w_ffn_up, m_ffn_conv_w, m_ffn_conv_b, m_w_ffn_down, m_ln_final_g, v_ln_mix_g, v_w_in, v_shift_mu_prev, v_shift_mu_next, v_decay_w0_fwd, v_decay_w2_fwd, v_decay_w0_bwd, v_decay_w2_bwd, v_iclr_a0_fwd, v_iclr_a2_fwd, v_iclr_a0_bwd, v_iclr_a2_bwd, v_gate_g2, v_k_k, v_k_a, v_r_k, v_ln_x_g, v_ln_x_b, v_q_norm_g, v_w_uq, v_kv_norm_g, v_w_ukv, v_mla_out_g, v_w_out, v_ln_ffn_g, v_w_ffn_up, v_ffn_conv_w, v_ffn_conv_b, v_w_ffn_down, v_ln_final_g):
    arrs = dict(locals())
    for pre in ("", "m_", "v_"):
        arrs[pre + "w_in"] = jnp.swapaxes(arrs[pre + "w_in"], 1, 2)
    b, t, d = x.shape
    m = b * t
    x2 = x.reshape(m, d)
    tgt = loss_target.reshape(m, d)
    vec = lambda n: arrs[n].reshape(1, -1)

    core = lax.axis_index("c")
    chip = 2 * lax.axis_index("x") + lax.axis_index("y")
    def unpack(gathered, shard, entries):
        g4 = lax.dynamic_update_index_in_dim(gathered.reshape(N_CHIPS, -1, 128), shard, chip, axis=0)
        return _unpack_gathered(g4, arrs, entries)

    shard_a, shard_b = _pack_shards_bf16(arrs, _BIG_A), _pack_shards_bf16(arrs, _BIG_B)
    fw = unpack(_allgather_weights(shard_a), shard_a, _BIG_A)
    win = fw["w_in"]
    zc = jnp.zeros((64, d), BF16)
    w_kr = win[2944:2976]
    rot_kr = jnp.swapaxes(_rot_cols(jnp.swapaxes(w_kr, 0, 1)), 0, 1)
    win_m = jnp.concatenate([win[1920:2944], zc, w_kr, zc[:32], zc, rot_kr, zc[:32]], axis=0)
    win_r = win[:RWKV_COLS]
    head = jnp.arange(512) // HEAD_DIM
    rw = dict(
        w0=jnp.concatenate([vec("decay_w0_fwd"), vec("decay_w0_bwd")], axis=1),
        w2=_block_diag(fw["decay_w2_fwd"], fw["decay_w2_bwd"]).astype(F32),
        a0=jnp.concatenate([vec("iclr_a0_fwd"), vec("iclr_a0_bwd")], axis=1),
        a2=_block_diag(fw["iclr_a2_fwd"], fw["iclr_a2_bwd"]).astype(F32),
        g2=fw["gate_g2"].astype(F32), k_k=vec("k_k"), k_a=vec("k_a"), r_k=vec("r_k"), ln_x_g=vec("ln_x_g"), ln_x_b=vec("ln_x_b"),
        ones_bd=(head[:, None] == head[None, :]).astype(F32),
    )
    cs, sn = _rope_tables(t)

    n1 = _rms_fwd(x2, vec("ln_mix_g"), "rms_mix")
    zm = _mm(n1, win_m, "nt", "proj_in_mla")
    zr = _mm(n1, win_r, "nt", "proj_in_rwkv")
    zs = _shift_fwd(zr.reshape(b, t, RWKV_COLS), vec("shift_mu_prev"), vec("shift_mu_next"))
    zs2 = zs.reshape(m, RWKV_COLS)
    wf, wb, kf, kb, kk, kaf, kab, gate = _prep_fwd(zs2, rw)
    r4 = lambda a: a.reshape(b, t, 512)
    f2 = lambda a: a.reshape(m, 512)
    kk4 = r4(kk)
    ops_f = (r4(wf), r4(kf), r4(kaf))
    ops_b = (r4(wb), r4(kb), r4(kab))
    y_f, hist_f, y_b, hist_b, s_last, gathered_b = _scan_fwd(zs, kk4, ops_f, ops_b, shard_b)
    fw.update(unpack(gathered_b, shard_b, _BIG_B))
    uq = fw["w_uq"].astype(F32).reshape(Q_RANK, HEADS, 96)
    z32 = jnp.zeros((Q_RANK, HEADS, 32), F32)
    wq = jnp.concatenate([uq[..., :64], uq[..., 64:], z32], axis=-1).reshape(Q_RANK, 1024)
    wqr = jnp.concatenate([z32, z32, _rot_cols(uq[..., 64:]), z32], axis=-1).reshape(Q_RANK, 1024)
    ukv = fw["w_ukv"].astype(F32).reshape(KV_RANK, HEADS, 128)
    wk = jnp.concatenate([ukv[..., :64], jnp.zeros_like(ukv[..., :64])], axis=-1).reshape(KV_RANK, 1024)
    wv = ukv[..., 64:].reshape(KV_RANK, 512)
    mp = dict(q_norm_g=vec("q_norm_g"), kv_norm_g=vec("kv_norm_g"), wq=wq, wqr=wqr, wk=wk, wv=wv)
    w_up_g, w_up_v = fw["w_ffn_up"][:, :D_FF], fw["w_ffn_up"][:, D_FF:]
    cw, cb = fw["ffn_conv_w"], vec("ffn_conv_b")
    y_f, y_b = f2(y_f), f2(y_b)
    y_rwkv = _post_fwd(y_f, y_b, zs2, kf, kb, gate, rw)
    q, kfull, v = _mla_fwd(zm, cs, sn, mp, t)
    o, lse = _attn_fwd(q, kfull, v, b, t)
    y_mla = _rms_fwd(o, vec("mla_out_g"), "rms_mla_out")
    ymix = jnp.concatenate([y_rwkv, y_mla], axis=1)
    h1 = _mm(ymix, fw["w_out"], "nn", "proj_out", add=x2)
    n2 = _rms_fwd(h1, vec("ln_ffn_g"), "rms_ffn")
    ug = _mm(n2, w_up_g, "nn", "ffn_up_gate")
    uv = _mm(n2, w_up_v, "nn", "ffn_up_val")
    r3f = lambda a: a.reshape(b, t, D_FF)
    act = _ffn_mid_fwd(r3f(ug), r3f(uv), cw, cb).reshape(m, D_FF)
    h2 = _mm(act, fw["w_ffn_down"], "nn", "ffn_down", add=h1)
    loss_tab, dh2, g_ln_final = _final(h2, vec("ln_final_g"), tgt)

    gfull = {}
    dact = _mm(dh2, fw["w_ffn_down"], "nt", "d_ffn_act")
    gfull["w_ffn_down"] = _mm(act, dh2, "tn", "g_ffn_down")
    dug, duv, tab_g, tab_v = _ffn_mid_bwd(r3f(ug), r3f(uv), cw, cb, r3f(dact))
    dug, duv = dug.reshape(m, D_FF), duv.reshape(m, D_FF)
    gfull["ffn_conv_w"] = jnp.concatenate([tab_g[0:3], tab_v[0:3]], axis=1)
    g_conv_b = jnp.concatenate([tab_g[3:4], tab_v[3:4]], axis=1)
    dn2 = _mm(duv, w_up_v, "nt", "d_ffn_in_val", add=_mm(dug, w_up_g, "nt", "d_ffn_in_gate"))
    shard_cols = arrs["w_ffn_up"].shape[2]
    gfull["w_ffn_up"] = jnp.concatenate([_mm(n2, dug, "tn", "g_ffn_up_gate", column_blocks=shard_cols),
                                         _mm(n2, duv, "tn", "g_ffn_up_val", column_blocks=shard_cols)], axis=0)
    dh1, g_ln_ffn = _rms_bwd(h1, vec("ln_ffn_g"), dn2, "rms_ffn_bwd", dres=dh2)
    dymix = _mm(dh1, fw["w_out"], "nt", "d_mix")
    gfull["w_out"] = _mm(ymix, dh1, "tn", "g_w_out")
    do, g_mla_out = _rms_bwd(o, vec("mla_out_g"), dymix, "rms_mla_out_bwd", dy_block=1)
    dq, dk, dv = _attn_bwd(q, kfull, v, o, lse, do, b, t)
    dzm, g_qn, g_kvn, g_wq, g_wqr, g_wk, g_wv = _mla_bwd(zm, cs, sn, mp, t, dq, dk, dv)
    gq3, gqr3 = g_wq.reshape(Q_RANK, HEADS, 128), g_wqr.reshape(Q_RANK, HEADS, 128)
    gfull["w_uq"] = jnp.concatenate(
        [gq3[..., :64], gq3[..., 64:96] + _rot_cols_t(gqr3[..., 64:96])], axis=-1
    ).reshape(Q_RANK, HEADS * 96)
    gfull["w_ukv"] = jnp.concatenate(
        [g_wk.reshape(KV_RANK, HEADS, 128)[..., :64], g_wv.reshape(KV_RANK, HEADS, 64)], axis=-1
    ).reshape(KV_RANK, 1024)
    def cores_first(entries, tag):
        packed = _pack_grads(gfull, arrs, entries)
        rh = packed.shape[1] // 2
        own = lax.dynamic_slice_in_dim(packed, core * rh, rh, axis=1)
        sib = _send_to_sibling(packed, True)
        return _add_rows([own.reshape(4 * rh, 128), sib.reshape(4 * rh, 128)], "sum_cores_" + tag, BF16).reshape(4, rh, 128)

    def join_halves(half, entries):
        other = _send_to_sibling(half, False)
        lower = jnp.where(core == 0, half, other)
        upper = jnp.where(core == 0, other, half)
        return _unpack_grads(jnp.concatenate([lower, upper], axis=0), arrs, entries)

    part_b = _pack_grads(gfull, arrs, _BIG_B, BF16)
    dys, dr_p, dk_p, dv_p, dgate, g_rk, g_lnx_g, g_lnx_b = _post_bwd(y_f, y_b, zs2, kf, kb, gate, rw, dymix)
    (dr_f, dwf, dkf, dkk_f, dkaf, dv_f, dr_b, dwb, dkb, dkk_b, dkab, dv_b, recv_b) = _scan_bwd(
        zs, kk4, r4(dys), ops_f, hist_f, ops_b, hist_b, s_last, part_b)
    rh_b = part_b.shape[1] // 2
    mine_b = lax.dynamic_slice(part_b, (chip, core * rh_b, 0), (1, rh_b, 128))[0]
    g_big = join_halves(_add_rows([mine_b] + [recv_b[k] for k in range(7)], "sum_devices_b"), _BIG_B)
    cts = dict(dwf=f2(dwf), dwb=f2(dwb), dkf=f2(dkf), dkb=f2(dkb), dkk_f=f2(dkk_f), dkk_b=f2(dkk_b), dkaf=f2(dkaf), dkab=f2(dkab),
               dr_f=f2(dr_f), dr_b=f2(dr_b), dr_p=dr_p, dk_p=dk_p, dv_p=dv_p, dg=dgate, dv_f=f2(dv_f), dv_b=f2(dv_b))
    dzs, g_w0, g_w2, g_a0, g_a2, g_g2, g_kk, g_ka = _prep_bwd(zs2, rw, cts)
    dzr, g_mu_p, g_mu_n = _shift_bwd(dzs.reshape(b, t, RWKV_COLS), zr.reshape(b, t, RWKV_COLS), vec("shift_mu_prev"), vec("shift_mu_next"))
    dzr = dzr.reshape(m, RWKV_COLS)
    gfull["decay_w2_fwd"], gfull["decay_w2_bwd"] = g_w2[:64, :512], g_w2[64:, 512:]
    gfull["iclr_a2_fwd"], gfull["iclr_a2_bwd"] = g_a2[:64, :512], g_a2[64:, 512:]
    gfull["gate_g2"] = g_g2
    dn1 = _mm(dzr, win_r, "nn", "d_proj_in_rwkv", add=_mm(dzm, win_m, "nn", "d_proj_in_mla"))
    g_m = _mm(dzm, n1, "tn", "g_w_in_mla")
    g_r = _mm(dzr, n1, "tn", "g_w_in_rwkv")
    g_kr = g_m[1088:1120] + jnp.swapaxes(_rot_cols_t(jnp.swapaxes(g_m[1216:1248], 0, 1)), 0, 1)
    gfull["w_in"] = jnp.concatenate([g_r, g_m[:1024], g_kr], axis=0)
    dx, g_ln_mix = _rms_bwd(x2, vec("ln_mix_g"), dn1, "rms_mix_bwd", dres=dh1)

    part_a = cores_first(_BIG_A, "a")
    recv_a = _scatter_to_chips(part_a)
    mine_a = lax.dynamic_index_in_dim(part_a, chip, axis=0, keepdims=False)
    g_big.update(join_halves(_add_rows([mine_a, recv_a[0], recv_a[1], recv_a[2]], "sum_chips_a"), _BIG_A))
    small = {
        "ln_mix_g": g_ln_mix, "shift_mu_prev": g_mu_p, "shift_mu_next": g_mu_n, "decay_w0_fwd": g_w0[:, :512],
        "decay_w0_bwd": g_w0[:, 512:], "iclr_a0_fwd": g_a0[:, :512], "iclr_a0_bwd": g_a0[:, 512:], "k_k": g_kk, "k_a": g_ka,
        "r_k": g_rk, "ln_x_g": g_lnx_g, "ln_x_b": g_lnx_b, "q_norm_g": g_qn, "kv_norm_g": g_kvn, "mla_out_g": g_mla_out,
        "ln_ffn_g": g_ln_ffn, "ffn_conv_b": g_conv_b, "ln_final_g": g_ln_final,
        "_loss": jnp.pad(loss_tab[0, 0:1], (0, 127)),
    }
    g_small_buf = _allreduce_small(_pack_small(small))
    g_small = _unpack_small(g_small_buf, arrs)

    grads, deltas, new_m, new_v = {}, {}, {}, {}
    for name, _, _ in _BIG:
        grads[name] = g_big[name]
        deltas[name], new_m[name], new_v[name] = _adamw(
            arrs[name], g_big[name], arrs["m_" + name], arrs["v_" + name], "adamw_" + name)
    pk = lambda pre: _pack_small({**{n: arrs[pre + n] for n in _SMALL}, "_loss": jnp.zeros((128,), F32)})
    sd, sm, sv = _adamw(pk(""), g_small_buf, pk("m_"), pk("v_"), "adamw_small")
    sd, sm, sv = _unpack_small(sd, arrs), _unpack_small(sm, arrs), _unpack_small(sv, arrs)
    for n in _SMALL:
        grads[n], deltas[n], new_m[n], new_v[n] = g_small[n], sd[n], sm[n], sv[n]
    for group in (grads, deltas, new_m, new_v):
        group["w_in"] = jnp.swapaxes(group["w_in"], 1, 2)

    return (g_small["_loss"], dx.reshape(b, t, d), *[grads[n] for n in _WEIGHTS], *[deltas[n] for n in _WEIGHTS],
            *[new_m[n] for n in _WEIGHTS], *[new_v[n] for n in _WEIGHTS])
```

```python
import functools
import math

import jax
import jax.numpy as jnp
from jax import lax
from jax.experimental import pallas as pl
from jax.experimental.pallas import tpu as pltpu

F32, BF16 = jnp.float32, jnp.bfloat16
MESH = pl.DeviceIdType.MESH
ANY = pl.BlockSpec(memory_space=pl.ANY)
VMEM = pl.BlockSpec(memory_space=pltpu.VMEM)
BS = pl.BlockSpec
SDS = jax.ShapeDtypeStruct

NORM_EPS = 1e-6
GN_EPS = 64e-5
L2_EPS = 1e-12
HEADS = 8
HEAD_DIM = 64
ROPE_DIM = 32
ROPE_THETA = 10000.0
MLA_SCALE = (64 + ROPE_DIM) ** -0.5
Q_RANK, KV_RANK = 768, 256
RWKV_COLS = 1920
MLA_PAD_COLS = Q_RANK + KV_RANK + 256
D_FF = 2816
ADAM_LR, ADAM_B1, ADAM_B2, ADAM_EPS, ADAM_WD, ADAM_STEP = 0.001, 0.9, 0.999, 1e-08, 0.01, 10

V7X_VMEM_LIMIT = 56 * 1024 * 1024
SCAN_CHUNK = 32
N_CHIPS = 4


def _cp(*sem):
    return pltpu.CompilerParams(dimension_semantics=sem, vmem_limit_bytes=V7X_VMEM_LIMIT)


def _tile(n, cands=(512, 640, 384, 256, 128)):
    for c in cands:
        if n % c == 0:
            return c
    return n


def _row_tile(n, cap=256):
    best = n
    for t in range(8, cap + 1, 8):
        if n % t == 0:
            best = t
    return best if best <= cap or n <= cap else n


def _rms(x, g):
    ms = jnp.mean(x * x, axis=-1, keepdims=True)
    return x * lax.rsqrt(ms + NORM_EPS) * g


@jax.custom_vjp
def _bdot(x, w):
    return jnp.dot(x.astype(BF16), w.astype(BF16), preferred_element_type=F32)


def _bdot_fwd(x, w):
    return _bdot(x, w), (x, w)


def _bdot_bwd(res, ct):
    x, w = res
    c = ct.astype(BF16)
    dx = lax.dot_general(c, w.astype(BF16), (((1,), (1,)), ((), ())), preferred_element_type=F32)
    dw = lax.dot_general(x.astype(BF16), c, (((0,), (0,)), ((), ())), preferred_element_type=F32)
    return dx.astype(x.dtype), dw.astype(w.dtype)


_bdot.defvjp(_bdot_fwd, _bdot_bwd)


@jax.custom_vjp
def _headsum(x, ones_bd):
    hi = x.astype(BF16)
    mid = (x - hi.astype(F32)).astype(BF16)
    ob = ones_bd.astype(BF16)
    return jnp.dot(hi, ob, preferred_element_type=F32) + jnp.dot(mid, ob, preferred_element_type=F32)


def _headsum_fwd(x, ones_bd):
    return _headsum(x, ones_bd), ones_bd


def _headsum_bwd(ones_bd, ct):
    return _headsum(ct, ones_bd), jnp.zeros_like(ones_bd)


_headsum.defvjp(_headsum_fwd, _headsum_bwd)


def _prep_fn(zs, w0, w2, a0, a2, g2, k_k, k_a, ones_bd):
    k = zs[:, 512:1024]
    wd = zs[:, 1536:1664]
    ad = zs[:, 1664:1792]
    gd = zs[:, 1792:1920]
    logit = w0 + _bdot(jnp.tanh(wd), w2)
    w = jnp.exp(-math.exp(-0.5) * jax.nn.sigmoid(logit))
    a = jax.nn.sigmoid(a0 + _bdot(ad, a2))
    g = _bdot(jax.nn.sigmoid(gd), g2)
    kkr = k * k_k
    nrm = jnp.sqrt(_headsum(kkr * kkr, ones_bd))
    kk = kkr / jnp.maximum(nrm, L2_EPS)
    a_f, a_b = a[:, :512], a[:, 512:]
    kf = k * (1.0 + (a_f - 1.0) * k_a)
    kb = k * (1.0 + (a_b - 1.0) * k_a)
    return w[:, :512], w[:, 512:], kf, kb, kk, kk * a_f, kk * a_b, g


def _post_fn(y, r, kf, kb, v, g, r_k, ln_g, ln_b, ones_bd):
    mu =_headsum(y, ones_bd) * (1.0 / HEAD_DIM)
    yc = y - mu
    var = _headsum(yc * yc, ones_bd) * (1.0 / HEAD_DIM)
    yn = yc * lax.rsqrt(var + GN_EPS) * ln_g + ln_b
    bonus = _headsum(r * (kf + kb) * r_k, ones_bd) * v
    return (yn + bonus) * g


def _cat8(x):
    return jnp.concatenate([x] * HEADS, axis=1)


def _mla_fn(zm, cs, sn, gq, gkv, wq, wqr, wk, wv):
    cq = zm[:, :Q_RANK]
    ckv = zm[:, Q_RANK:Q_RANK + KV_RANK]
    kr = zm[:, Q_RANK + KV_RANK:Q_RANK + KV_RANK + 128]
    krr = zm[:, Q_RANK + KV_RANK + 128:]
    cqn = _rms(cq, gq)
    ckvn = _rms(ckv, gkv)
    q = (_bdot(cqn, wq) * _cat8(cs) + _bdot(cqn, wqr) * _cat8(sn)) * MLA_SCALE
    kro = kr * cs + krr * sn
    kfull = _bdot(ckvn, wk) + _cat8(kro)
    v = _bdot(ckvn, wv)
    return q, kfull, v


def _adamw_math(w, g, m, v):
    m2 = ADAM_B1 * m + (1.0 - ADAM_B1) * g
    v2 = ADAM_B2 * v + (1.0 - ADAM_B2) * (g * g)
    m_hat = m2 / (1.0 - ADAM_B1 ** ADAM_STEP)
    v_hat = v2 / (1.0 - ADAM_B2 ** ADAM_STEP)
    delta = -ADAM_LR * (m_hat / (jnp.sqrt(v_hat) + ADAM_EPS) + ADAM_WD * w)
    return delta, m2, v2


_DIMS = {"nn": (((1,), (0,)), ((), ())), "nt": (((1,), (1,)), ((), ())), "tn": (((0,), (0,)), ((), ()))}


def _mm(a, b, mode, name, out_dtype=F32, add=None, column_blocks=None):
    if mode == "nn":
        (m, k), (_, n) = a.shape, b.shape
    elif mode == "nt":
        (m, k), (n, _) = a.shape, b.shape
    else:
        (k, m), (_, n) = a.shape, b.shape
    big = (1024, 1408, 768, 640, 512, 384, 256, 128)
    tm, tn, tk = _tile(m, big), column_blocks or _tile(n, big), _tile(k, (512, 1408, 640, 384, 256, 128))
    nk = k // tk

    def body(a_ref, b_ref, *rest):
        if add is None:
            o_ref, acc_ref = rest
        else:
            add_ref, o_ref, acc_ref = rest
        kk = pl.program_id(2)

        @pl.when(kk == 0)
        def _():
            acc_ref[...] = jnp.zeros_like(acc_ref)

        acc_ref[...] += lax.dot_general(
            a_ref[...].astype(BF16), b_ref[...].astype(BF16), _DIMS[mode], preferred_element_type=F32
        )

        @pl.when(kk == nk - 1)
        def _():
            r = acc_ref[...]
            if add is not None:
                r = r + add_ref[...]
            o_ref[...] = r.astype(out_dtype).reshape(o_ref.shape)

    a_spec = BS((tk, tm), lambda i, j, kk: (kk, i)) if mode == "tn" else BS((tm, tk), lambda i, j, kk: (i, kk))
    b_spec = BS((tn, tk), lambda i, j, kk: (j, kk)) if mode == "nt" else BS((tk, tn), lambda i, j, kk: (kk, j))
    o_spec = BS((tm, tn), lambda i, j, kk: (i, j))
    ins, specs = [a, b], [a_spec, b_spec]
    if add is not None:
        ins.append(add)
        specs.append(o_spec)
    out_shape = SDS((m, n), out_dtype)
    if column_blocks:
        assert add is None
        o_spec, out_shape = BS((1, tm, tn), lambda i, j, kk: (j, i, 0)), SDS((n // tn, m, tn), out_dtype)
    return pl.pallas_call(
        body, grid=(m // tm, n // tn, nk), in_specs=specs, out_specs=o_spec, out_shape=out_shape,
        scratch_shapes=[pltpu.VMEM((tm, tn), F32)], compiler_params=_cp("parallel", "parallel", "arbitrary"), name=name,
    )(*ins)


def _rms_fwd(x, g, name):
    m, d = x.shape
    tr = _tile(m)

    def body(x_ref, g_ref, o_ref):
        o_ref[...] = _rms(x_ref[...], g_ref[...]).astype(BF16)

    return pl.pallas_call(
        body, grid=(m // tr,), in_specs=[BS((tr, d), lambda i: (i, 0)), BS((1, d), lambda i: (0, 0))],
        out_specs=BS((tr, d), lambda i: (i, 0)), out_shape=SDS((m, d), BF16), compiler_params=_cp("parallel"), name=name,
    )(x, g)


def _rms_bwd(x, g, dy, name, dres=None, dy_block=0):
    m, d = x.shape
    tr = _row_tile(m, 512)

    def body(x_ref, g_ref, dy_ref, *rest):
        if dres is None:
            dx_ref, dg_ref = rest
        else:
            dres_ref, dx_ref, dg_ref = rest
        _, vjp = jax.vjp(_rms, x_ref[...], g_ref[...])
        dx, dg = vjp(dy_ref[...])
        if dres is not None:
            dx = dx + dres_ref[...]
        dx_ref[...] = dx

        @pl.when(pl.program_id(0) == 0)
        def _():
            dg_ref[...] = jnp.zeros_like(dg_ref)

        dg_ref[...] += dg

    row = BS((tr, d), lambda i: (i, 0))
    vec = BS((1, d), lambda i: (0, 0))
    ins, specs = [x, g, dy], [row, vec, BS((tr, d), lambda i: (i, dy_block))]
    if dres is not None:
        ins.append(dres)
        specs.append(row)
    return pl.pallas_call(
        body, grid=(m // tr,), in_specs=specs, out_specs=[row, vec], out_shape=[SDS((m, d), F32), SDS((1, d), F32)],
        compiler_params=_cp("arbitrary"), name=name,
    )(*ins)


def _final(h, g, tgt):
    m, d = h.shape
    tr = _row_tile(m, 512)

    def loss_fn(hh, gg, tt):
        e = _rms(hh, gg) - tt
        return 0.5 * jnp.sum(e * e) * (1.0 / d)

    def body(h_ref, g_ref, t_ref, l_ref, dh_ref, dg_ref):
        val, (dh, dg) = jax.value_and_grad(loss_fn, argnums=(0, 1))(h_ref[...], g_ref[...], t_ref[...])
        dh_ref[...] = dh

        @pl.when(pl.program_id(0) == 0)
        def _():
            dg_ref[...] = jnp.zeros_like(dg_ref)
            l_ref[...] = jnp.zeros_like(l_ref)

        dg_ref[...] += dg
        l_ref[...] += jnp.full(l_ref.shape, val, F32)

    row = BS((tr, d), lambda i: (i, 0))
    vec = BS((1, d), lambda i: (0, 0))
    return pl.pallas_call(
        body, grid=(m // tr,), in_specs=[row, vec, row], out_specs=[BS((8, 128), lambda i: (0, 0)), row, vec],
        out_shape=[SDS((8, 128), F32), SDS((m, d), F32), SDS((1, d), F32)], compiler_params=_cp("arbitrary"), name="final_loss",
    )(h, g, tgt)


def _prev_next(z, t):
    row = lax.broadcasted_iota(jnp.int32, z.shape, 0)
    zp = jnp.where(row == 0, 0.0, pltpu.roll(z, 1, axis=0))
    zn = jnp.where(row == t - 1, 0.0, pltpu.roll(z, t - 1, axis=0))
    return zp, zn


def _shift_fwd(z3, mu_p, mu_n):
    b, t, c = z3.shape
    nc = c // 128

    def body(z_ref, mp_ref, mn_ref, o_ref):
        z = z_ref[0]
        zp, zn = _prev_next(z, t)
        o_ref[0] = z + mp_ref[...] * (zp - z) + mn_ref[...] * (zn - z)

    blk = BS((1, t, 128), lambda i, j: (i, 0, j))
    vec = BS((1, 128), lambda i, j: (0, j))
    return pl.pallas_call(
        body, grid=(b, nc), in_specs=[blk, vec, vec], out_specs=blk, out_shape=SDS((b, t, c), F32),
        compiler_params=_cp("parallel", "parallel"), name="shift_fwd",
    )(z3, mu_p, mu_n)


def _shift_bwd(dzs3, z3, mu_p, mu_n):
    b, t, c = z3.shape
    nc = c // 128

    def body(d_ref, z_ref, mp_ref, mn_ref, dz_ref, dmp_ref, dmn_ref):
        d, z = d_ref[0], z_ref[0]
        mp, mn = mp_ref[...], mn_ref[...]
        zp, zn = _prev_next(z, t)
        _, dp_next = _prev_next(d * mp, t)
        dn_prev, _ = _prev_next(d * mn, t)
        dz_ref[0] = (d * (1.0 - mp - mn) + dp_next + dn_prev).astype(BF16)

        @pl.when(pl.program_id(1) == 0)
        def _():
            dmp_ref[...] = jnp.zeros_like(dmp_ref)
            dmn_ref[...] = jnp.zeros_like(dmn_ref)

        dmp_ref[...] += jnp.sum(d * (zp - z), axis=0, keepdims=True)
        dmn_ref[...] += jnp.sum(d * (zn - z), axis=0, keepdims=True)

    blk = BS((1, t, 128), lambda j, i: (i, 0, j))
    vec = BS((1, 128), lambda j, i: (0, j))
    return pl.pallas_call(
        body, grid=(nc, b), in_specs=[blk, blk, vec, vec], out_specs=[blk, vec, vec],
        out_shape=[SDS((b, t, c), BF16), SDS((1, c), F32), SDS((1, c), F32)],
        compiler_params=_cp("parallel", "arbitrary"), name="shift_bwd",
    )(dzs3, z3, mu_p, mu_n)


def _const(shape):
    nd = len(shape)
    return BS(shape, lambda i: (0,) * nd)


def _prep_fwd(zs, p):
    m = zs.shape[0]
    tr = 512
    params = [p["w0"], p["w2"], p["a0"], p["a2"], p["g2"], p["k_k"], p["k_a"], p["ones_bd"]]

    def body(zs_ref, w0, w2, a0, a2, g2, kk_, ka_, bd, wf, wb, kf, kb, kk, kaf, kab, g):
        outs = _prep_fn(zs_ref[...], w0[...], w2[...], a0[...], a2[...], g2[...], kk_[...], ka_[...], bd[...])
        for ref, val in zip((wf, wb, kf, kb, kk, kaf, kab, g), outs):
            ref[...] = val

    row = BS((tr, 512), lambda i: (i, 0))
    return pl.pallas_call(
        body, grid=(m // tr,), in_specs=[BS((tr, RWKV_COLS), lambda i: (i, 0))] + [_const(q.shape) for q in params],
        out_specs=[row] * 8, out_shape=[SDS((m, 512), F32)] * 8, compiler_params=_cp("parallel"), name="rwkv_prep_fwd",
    )(zs, *params)


def _prep_bwd(zs, p, ct_rows):
    m = zs.shape[0]
    tr = 256
    params = [p["w0"], p["w2"], p["a0"], p["a2"], p["g2"], p["k_k"], p["k_a"]]
    names = ["dwf", "dwb", "dkf", "dkb", "dkk_f", "dkk_b", "dkaf", "dkab", "dr_f", "dr_b", "dr_p", "dk_p", "dv_p", "dg",
             "dv_f", "dv_b"]
    rows = [ct_rows[n] for n in names]

    def body(zs_ref, w0, w2, a0, a2, g2, kk_, ka_, bd, *rest):
        c = {n: r[...] for n, r in zip(names, rest[:len(names)])}
        outs = rest[len(names):]
        dzs_ref, grads = outs[0], outs[1:]
        ones_bd = bd[...]
        _, vjp = jax.vjp(
            lambda *q: _prep_fn(*q, ones_bd), zs_ref[...], w0[...], w2[...], a0[...], a2[...], g2[...], kk_[...], ka_[...]
        )
        cts = (c["dwf"], c["dwb"], c["dkf"] + c["dk_p"], c["dkb"] + c["dk_p"], c["dkk_f"] + c["dkk_b"], c["dkaf"], c["dkab"], c["dg"])
        dzs, *dparams = vjp(cts)
        dr = c["dr_f"] + c["dr_b"] + c["dr_p"]
        dv = c["dv_f"] + c["dv_b"] + c["dv_p"]
        dzs_ref[:, 0:512] = dzs[:, 0:512] + dr
        dzs_ref[:, 512:1024] = dzs[:, 512:1024]
        dzs_ref[:, 1024:1536] = dzs[:, 1024:1536] + dv
        dzs_ref[:, 1536:1920] = dzs[:, 1536:1920]

        @pl.when(pl.program_id(0) == 0)
        def _():
            for gr in grads:
                gr[...] = jnp.zeros_like(gr)

        for gr, val in zip(grads, dparams):
            gr[...] += val

    row = BS((tr, 512), lambda i: (i, 0))
    return pl.pallas_call(
        body, grid=(m // tr,),
        in_specs=[BS((tr, RWKV_COLS), lambda i: (i, 0))] + [_const(q.shape) for q in params] + [_const(p["ones_bd"].shape)]
        + [row] * len(names),
        out_specs=[BS((tr, RWKV_COLS), lambda i: (i, 0))] + [_const(q.shape) for q in params],
        out_shape=[SDS((m, RWKV_COLS), F32)] + [SDS(q.shape, F32) for q in params],
        compiler_params=_cp("arbitrary"), name="rwkv_prep_bwd",
    )(zs, *params, p["ones_bd"], *rows)


def _post_specs(tr):
    r = BS((tr, 512), lambda i: (i, 0))
    v = BS((tr, 512), lambda i: (i, 2))
    row = BS((tr, 512), lambda i: (i, 0))
    return r, v, row


def _post_fwd(y_f, y_b, zs, kf, kb, g, p):
    m = zs.shape[0]
    tr = 512
    r, v, row = _post_specs(tr)
    vecs = [p["r_k"], p["ln_x_g"], p["ln_x_b"], p["ones_bd"]]

    def body(yf, yb, r_ref, v_ref, kf_ref, kb_ref, g_ref, rk, lg, lb, bd, o_ref):
        o_ref[...] = _post_fn(
            yf[...] + yb[...], r_ref[...], kf_ref[...], kb_ref[...], v_ref[...], g_ref[...], rk[...], lg[...], lb[...], bd[...]
        ).astype(BF16)

    return pl.pallas_call(
        body, grid=(m // tr,), in_specs=[row, row, r, v, row, row, row] + [_const(q.shape) for q in vecs],
        out_specs=row, out_shape=SDS((m, 512), BF16), compiler_params=_cp("parallel"), name="rwkv_post_fwd",
    )(y_f, y_b, zs, zs, kf, kb, g, *vecs)


def _post_bwd(y_f, y_b, zs, kf, kb, g, p, dymix):
    m = zs.shape[0]
    tr = 256
    r, v, row = _post_specs(tr)
    vecs = [p["r_k"], p["ln_x_g"], p["ln_x_b"]]

    def body(yf, yb, r_ref, v_ref, kf_ref, kb_ref, g_ref, rk, lg, lb, bd, dy_ref, dyo, dr, dk, dv, dg, drk, dlg, dlb):
        ones_bd = bd[...]
        _, vjp = jax.vjp(
            lambda *q: _post_fn(*q, ones_bd),
            yf[...] + yb[...], r_ref[...], kf_ref[...], kb_ref[...], v_ref[...], g_ref[...], rk[...], lg[...], lb[...],
        )
        c_y, c_r, c_kf, _, c_v, c_g, c_rk, c_lg, c_lb = vjp(dy_ref[...])
        dyo[...] = c_y
        dr[...] = c_r
        dk[...] = c_kf
        dv[...] = c_v
        dg[...] = c_g

        @pl.when(pl.program_id(0) == 0)
        def _():
            for ref in (drk, dlg, dlb):
                ref[...] = jnp.zeros_like(ref)

        drk[...] += c_rk
        dlg[...] += c_lg
        dlb[...] += c_lb

    vec = _const((1, 512))
    return pl.pallas_call(
        body, grid=(m // tr,),
        in_specs=[row, row, r, v, row, row, row] + [_const(q.shape) for q in vecs] + [_const(p["ones_bd"].shape), row],
        out_specs=[row, row, row, row, row, vec, vec, vec],
        out_shape=[SDS((m, 512), F32)] * 5 + [SDS((1, 512), F32)] * 3,
        compiler_params=_cp("arbitrary"), name="rwkv_post_bwd",
    )(y_f, y_b, zs, zs, kf, kb, g, *vecs, p["ones_bd"], dymix)


SCAN_MXU_GROUPS = 2


def _half_ones():
    ri = lax.broadcasted_iota(jnp.int32, (128, 128), 0)
    ci = lax.broadcasted_iota(jnp.int32, (128, 128), 1)
    return jnp.where((ri < 64) == (ci < 64), 1.0, 0.0).astype(BF16)


def _half_sums(xs, ones):
    out = []
    per = -(-len(xs) // SCAN_MXU_GROUPS)
    for g in range(0, len(xs), per):
        part = xs[g:g + per]
        res = jnp.dot(jnp.concatenate(part, axis=0).astype(BF16), ones, preferred_element_type=F32)
        out += [res[64 * i:64 * i + 64] for i in range(len(part))]
    return out


def _scan_specs(b, t):
    nc = t // SCAN_CHUNK
    up, down = (lambda c: c), (lambda c: nc - 1 - c)
    rows = [BS((b, SCAN_CHUNK, 512), lambda c, ci=ci: (0, ci(c), 0)) for ci in (up, down)]
    vrows = [BS((b, SCAN_CHUNK, 512), lambda c, ci=ci: (0, ci(c), 2)) for ci in (up, down)]
    hist = [BS((SCAN_CHUNK, b * 4, 64, 128), lambda c, ci=ci: (ci(c), 0, 0, 0)) for ci in (up, down)]
    return nc, rows, vrows, hist


class _Window:
    def __init__(self, g, ascending):
        self.bases = [pl.multiple_of(g * 8, 8) if asc else pl.multiple_of(SCAN_CHUNK - 8 - g * 8, 8) for asc in ascending]
        self.ascending = ascending
        self.blocks = {}
        self.row_id = lax.broadcasted_iota(jnp.int32, (8, 128), 0)

    def j(self, d, s):
        return s if self.ascending[d] else 7 - s

    def time(self, d, s):
        return self.bases[d] + self.j(d, s)

    def row(self, ref, d, bi, cols, s):
        key = (id(ref), d, bi, cols.start)
        if key not in self.blocks:
            self.blocks[key] = ref[bi, pl.ds(self.bases[d], 8), cols]
        jj = self.j(d, s)
        return self.blocks[key][jj:jj + 1, :]

    def put(self, buf, key, d, s, row):
        prev = buf.get(key)
        new = jnp.broadcast_to(row, (8, 128))
        buf[key] = new if prev is None else jnp.where(self.row_id == self.j(d, s), new, prev)

    def flush(self, buf, refs_of):
        for key, val in buf.items():
            ref, d, bi, cols = refs_of(key)
            ref[bi, pl.ds(self.bases[d], 8), cols] = val


def _pairs(b):
    return [(bi * 4 + p, bi, slice(128 * p, 128 * p + 128)) for bi in range(b) for p in range(4)]


def _colsum(x):
    return jnp.sum(x, axis=0, keepdims=True)


def _pair_matvec(row, mat):
    rid = lax.broadcasted_iota(jnp.int32, (8, 64), 0)
    lhs = jnp.where(rid == 0, row[:, :64], jnp.where(rid == 1, row[:, 64:], 0.0))
    out = jnp.dot(lhs.astype(BF16), mat.astype(BF16), preferred_element_type=F32)
    lo = lax.broadcasted_iota(jnp.int32, (1, 128), 1) < 64
    return jnp.where(lo, out[0:1], out[1:2])


def _eye_mask():
    return (lax.broadcasted_iota(jnp.int32, (64, 128), 1) & 63) == lax.broadcasted_iota(jnp.int32, (64, 128), 0)


def _scan_fwd(zs, kk, ops_f, ops_b, shard):
    b, t = zs.shape[:2]
    nc, rows, vrows, hist = _scan_specs(b, t)
    npair = b * 4

    def body(*refs):
        ins, shard_ref, outs, s_ref = refs[:12], refs[12], refs[13:17], refs[17]
        gather = (shard_ref, *refs[18:21])
        dirs = [dict(zip(("r", "kk", "v", "w", "k", "ka", "y", "h"), (*ins[6 * d:6 * d + 6], *outs[2 * d:2 * d + 2])))
                for d in (0, 1)]

        @pl.when(pl.program_id(0) == 0)
        def _():
            s_ref[...] = jnp.zeros_like(s_ref)
            _gather_halves(*gather, "start")

        @pl.when(pl.program_id(0) == nc - 1)
        def _():
            _gather_halves(*gather, "finish")

        ones, eye = _half_ones(), _eye_mask()
        chains = [(d, pr, bi, cols) for d in (0, 1) for pr, bi, cols in _pairs(b)]

        def eight_steps(g, carry):
            win = _Window(g, (True, False))
            ybuf = {}
            for s in range(8):
                s_prev, xa = [], []
                for d, pr, bi, cols in chains:
                    q = dirs[d]
                    st = s_ref[d * npair + pr]
                    q["h"][win.time(d, s), pr] = st
                    s_prev.append(st)
                    xa += [st * win.row(q["kk"], d, bi, cols, s), jnp.where(eye, win.row(q["v"], d, bi, cols, s), 0.0)]
                ra = _half_sums(xa, ones)
                xb = []
                for i, (d, pr, bi, cols) in enumerate(chains):
                    q = dirs[d]
                    s_new = s_prev[i] * win.row(q["w"], d, bi, cols, s) - ra[2 * i] * win.row(q["ka"], d, bi, cols, s) \
                        + ra[2 * i + 1] * win.row(q["k"], d, bi, cols, s)
                    s_ref[d * npair + pr] = s_new
                    xb.append(s_new * win.row(q["r"], d, bi, cols, s))
                rb = _half_sums(xb, ones)
                for i, (d, pr, bi, cols) in enumerate(chains):
                    win.put(ybuf, i, d, s, _colsum(jnp.where(eye, rb[i], 0.0)))
            win.flush(ybuf, lambda i: (dirs[chains[i][0]]["y"], chains[i][0], chains[i][2], chains[i][3]))
            return carry

        lax.fori_loop(0, SCAN_CHUNK // 8, eight_steps, 0)

    row_shape, hist_shape = SDS((b, t, 512), F32), SDS((t, npair, 64, 128), F32)
    state = (2 * npair, 64, 128)
    return pl.pallas_call(
        body, grid=(nc,), in_specs=sum(([rows[d], rows[d], vrows[d]] + [rows[d]] * 3 for d in (0, 1)), []) + [ANY],
        out_specs=[rows[0], hist[0], rows[1], hist[1], BS(state, lambda c: (0, 0, 0)), ANY],
        out_shape=[row_shape, hist_shape, row_shape, hist_shape, SDS(state, F32), _gathered_shape(shard)],
        scratch_shapes=_gather_sems(), compiler_params=_cp("arbitrary"), name="wkv_scan",
    )(zs, kk, zs, *ops_f, zs, kk, zs, *ops_b, shard)


def _scan_bwd(zs, kk, dy, ops_f, hist_f, ops_b, hist_b, s_last, partials):
    b, t = zs.shape[:2]
    nc, rows, vrows, hist = _scan_specs(b, t)
    npair = b * 4
    names_in = ("r", "kk", "v", "dy", "w", "k", "ka", "h")
    names_out = ("dr", "dw", "dk", "dkk", "dka", "dv")

    def body(*refs):
        ins, last_ref, part_ref, outs, recv_ref = refs[:16], refs[16], refs[17], refs[18:30], refs[30]
        ds_ref, after_ref = refs[31], refs[32]
        scatter = (part_ref, recv_ref, refs[33], refs[34])
        dirs = [dict(zip(names_in + names_out, (*ins[8 * d:8 * d + 8], *outs[6 * d:6 * d + 6]))) for d in (0, 1)]

        @pl.when(pl.program_id(0) == 0)
        def _():
            ds_ref[...] = jnp.zeros_like(ds_ref)
            after_ref[...] = last_ref[...]
            _scatter_to_all(*scatter, "start")

        @pl.when(pl.program_id(0) == nc - 1)
        def _():
            _scatter_to_all(*scatter, "finish")

        ones, eye = _half_ones(), _eye_mask()
        chains = [(d, pr, bi, cols) for d in (0, 1) for pr, bi, cols in _pairs(b)]

        def eight_steps(g, carry):
            win = _Window(g, (False, True))
            obuf = {}
            s_after = [after_ref[d * npair + pr] for d, pr, _, _ in chains]
            for s in range(8):
                row = lambda name, d, bi, cols: win.row(dirs[d][name], d, bi, cols, s)
                s_prev, xa = [], []
                for d, pr, bi, cols in chains:
                    st = dirs[d]["h"][win.time(d, s), pr]
                    s_prev.append(st)
                    xa += [st * row("kk", d, bi, cols), jnp.where(eye, row("dy", d, bi, cols), 0.0)]
                ra = _half_sums(xa, ones)
                ds_now, xb = [], []
                for i, (d, pr, bi, cols) in enumerate(chains):
                    skk, dycol = ra[2 * i], ra[2 * i + 1]
                    ds = ds_ref[d * npair + pr] + dycol * row("r", d, bi, cols)
                    win.put(obuf, (i, "dr"), d, s, _pair_matvec(row("dy", d, bi, cols), s_after[i]))
                    win.put(obuf, (i, "dk"), d, s, _pair_matvec(row("v", d, bi, cols), ds))
                    win.put(obuf, (i, "dka"), d, s, -_colsum(ds * skk))
                    win.put(obuf, (i, "dw"), d, s, _colsum(ds * s_prev[i]))
                    ds_now.append(ds)
                    xb += [ds * row("k", d, bi, cols), ds * row("ka", d, bi, cols)]
                rb = _half_sums(xb, ones)
                for i, (d, pr, bi, cols) in enumerate(chains):
                    dskk_neg = rb[2 * i + 1]
                    win.put(obuf, (i, "dv"), d, s, _colsum(jnp.where(eye, rb[2 * i], 0.0)))
                    win.put(obuf, (i, "dkk"), d, s, -_colsum(s_prev[i] * dskk_neg))
                    ds_ref[d * npair + pr] = ds_now[i] * row("w", d, bi, cols) - dskk_neg * row("kk", d, bi, cols)
                s_after = s_prev
            for i, (d, pr, _, _) in enumerate(chains):
                after_ref[d * npair + pr] = s_after[i]
            win.flush(obuf, lambda key: (dirs[chains[key[0]][0]][key[1]], chains[key[0]][0], chains[key[0]][2], chains[key[0]][3]))
            return carry

        lax.fori_loop(0, SCAN_CHUNK // 8, eight_steps, 0)

    row_shape = SDS((b, t, 512), F32)
    state = (2 * npair, 64, 128)
    return pl.pallas_call(
        body, grid=(nc,),
        in_specs=sum(([rows[d], rows[d], vrows[d]] + [rows[d]] * 4 + [hist[d]] for d in (1, 0)), [])
        + [BS(state, lambda c: (0, 0, 0)), ANY],
        out_specs=[rows[1]] * 6 + [rows[0]] * 6 + [ANY],
        out_shape=[row_shape] * 12 + [SDS((7, partials.shape[1] // 2, 128), partials.dtype)],
        scratch_shapes=[pltpu.VMEM(state, F32), pltpu.VMEM(state, F32), pltpu.SemaphoreType.DMA((7,)),
                        pltpu.SemaphoreType.DMA((7,))],
        compiler_params=_cp("arbitrary"), name="wkv_scan_bwd",
    )(zs, kk, zs, dy, *ops_f, hist_f, zs, kk, zs, dy, *ops_b, hist_b, s_last, partials)


def _mla_fwd(zm, cs, sn, p, t):
    m = zm.shape[0]
    tr = 512
    per = t // tr
    params = [p["q_norm_g"], p["kv_norm_g"], p["wq"], p["wqr"], p["wk"], p["wv"]]

    def body(z_ref, cs_ref, sn_ref, gq, gkv, wq, wqr, wk, wv, q_ref, k_ref, v_ref):
        q, kf, v = _mla_fn(z_ref[...], cs_ref[...], sn_ref[...], gq[...], gkv[...], wq[...], wqr[...], wk[...], wv[...])
        q_ref[...] = q.astype(BF16)
        k_ref[...] = kf.astype(BF16)
        v_ref[...] = v.astype(BF16)

    tab = BS((tr, 128), lambda i: (i % per, 0))
    return pl.pallas_call(
        body, grid=(m // tr,), in_specs=[BS((tr, MLA_PAD_COLS), lambda i: (i, 0)), tab, tab] + [_const(q.shape) for q in params],
        out_specs=[BS((tr, 1024), lambda i: (i, 0)), BS((tr, 1024), lambda i: (i, 0)), BS((tr, 512), lambda i: (i, 0))],
        out_shape=[SDS((m, 1024), BF16), SDS((m, 1024), BF16), SDS((m, 512), BF16)], compiler_params=_cp("parallel"), name="mla_prep_fwd",
    )(zm, cs, sn, *params)


def _mla_bwd(zm, cs, sn, p, t, dq, dk, dv):
    m = zm.shape[0]
    tr = 256
    per = t // tr
    params = [p["q_norm_g"], p["kv_norm_g"], p["wq"], p["wqr"], p["wk"], p["wv"]]

    def body(z_ref, cs_ref, sn_ref, gq, gkv, wq, wqr, wk, wv, dq_ref, dk_ref, dv_ref, dz_ref, *grads):
        cs_v, sn_v = cs_ref[...], sn_ref[...]
        _, vjp = jax.vjp(
            lambda *q: _mla_fn(q[0], cs_v, sn_v, *q[1:]), z_ref[...], gq[...], gkv[...], wq[...], wqr[...], wk[...], wv[...]
        )
        dz, *dparams = vjp((dq_ref[...], dk_ref[...], dv_ref[...]))
        dz_ref[...] = dz.astype(BF16)

        @pl.when(pl.program_id(0) == 0)
        def _():
            for gr in grads:
                gr[...] = jnp.zeros_like(gr)

        for gr, val in zip(grads, dparams):
            gr[...] += val

    tab = BS((tr, 128), lambda i: (i % per, 0))
    wide = BS((tr, 1024), lambda i: (i, 0))
    return pl.pallas_call(
        body, grid=(m // tr,),
        in_specs=[BS((tr, MLA_PAD_COLS), lambda i: (i, 0)), tab, tab] + [_const(q.shape) for q in params]
        + [wide, wide, BS((tr, 512), lambda i: (i, 0))],
        out_specs=[BS((tr, MLA_PAD_COLS), lambda i: (i, 0))] + [_const(q.shape) for q in params],
        out_shape=[SDS((m, MLA_PAD_COLS), BF16)] + [SDS(q.shape, F32) for q in params],
        compiler_params=_cp("arbitrary"), name="mla_prep_bwd",
    )(zm, cs, sn, *params, dq, dk, dv)


_NT = (((1,), (1,)), ((), ()))
_TN = (((0,), (0,)), ((), ()))


def _attn_fwd(q, kf, v, b, t):
    m = q.shape[0]
    tq = 256
    nq = t // tq

    def body(q_ref, k_ref, v_ref, o_ref, l_ref):
        lo = lax.broadcasted_iota(jnp.int32, (1, 128), 1) < 64
        v_all = v_ref[...]
        o = jnp.zeros((tq, 128), F32)
        lse = []
        for h in range(2):
            hs = slice(128 * h, 128 * h + 128)
            s = lax.dot_general(q_ref[:, hs], k_ref[:, hs], _NT, preferred_element_type=F32)
            mx = jnp.max(s, axis=1, keepdims=True)
            e = jnp.exp(s - mx)
            den = jnp.sum(e, axis=1, keepdims=True)
            vh = jnp.where(lo if h == 0 else jnp.logical_not(lo), v_all, jnp.zeros_like(v_all))
            o = o + jnp.dot(e.astype(BF16), vh, preferred_element_type=F32) / den
            lse.append(mx + jnp.log(den))
        o_ref[...] = o
        l_ref[...] = jnp.where(lo, lse[0], lse[1])

    return pl.pallas_call(
        body, grid=(b, 4, nq),
        in_specs=[BS((tq, 256), lambda bi, hp, i: (bi * nq + i, hp)), BS((t, 256), lambda bi, hp, i: (bi, hp)),
                  BS((t, 128), lambda bi, hp, i: (bi, hp))],
        out_specs=[BS((tq, 128), lambda bi, hp, i: (bi * nq + i, hp))] * 2,
        out_shape=[SDS((m, 512), F32), SDS((m, 512), F32)], compiler_params=_cp("parallel", "parallel", "arbitrary"), name="attn_fwd",
    )(q, kf, v)


def _attn_bwd(q, kf, v, o, lse, do, b, t):
    m = q.shape[0]
    tq = 256
    nq = t // tq

    def body(q_ref, k_ref, v_ref, o_ref, l_ref, do_ref, dq_ref, dk_ref, dv_ref):
        lo = lax.broadcasted_iota(jnp.int32, (1, 128), 1) < 64

        @pl.when(pl.program_id(2) == 0)
        def _():
            dk_ref[...] = jnp.zeros_like(dk_ref)
            dv_ref[...] = jnp.zeros_like(dv_ref)

        v_all, o_all, l_all, do_all = v_ref[...], o_ref[...], l_ref[...], do_ref[...]
        dv_acc = jnp.zeros((t, 128), F32)
        for h in range(2):
            hs = slice(128 * h, 128 * h + 128)
            mask = lo if h == 0 else jnp.logical_not(lo)
            qh, kh = q_ref[:, hs], k_ref[:, hs]
            s = lax.dot_general(qh, kh, _NT, preferred_element_type=F32)
            lse_h = jnp.max(jnp.where(mask, l_all, -jnp.inf), axis=1, keepdims=True)
            pr = jnp.exp(s - lse_h)
            do_h = jnp.where(mask, do_all, 0.0)
            dp = lax.dot_general(do_h.astype(BF16), v_all, _NT, preferred_element_type=F32)
            dsum = jnp.sum(do_h * o_all, axis=1, keepdims=True)
            ds = (pr * (dp - dsum)).astype(BF16)
            dq_ref[:, hs] = jnp.dot(ds, kh, preferred_element_type=F32)
            dk_ref[:, hs] += lax.dot_general(ds, qh, _TN, preferred_element_type=F32)
            dv_acc = dv_acc + lax.dot_general(pr.astype(BF16), do_h.astype(BF16), _TN, preferred_element_type=F32)
        dv_ref[...] += dv_acc

    qspec = BS((tq, 256), lambda bi, hp, i: (bi * nq + i, hp))
    kspec = BS((t, 256), lambda bi, hp, i: (bi, hp))
    vspec = BS((t, 128), lambda bi, hp, i: (bi, hp))
    ospec = BS((tq, 128), lambda bi, hp, i: (bi * nq + i, hp))
    return pl.pallas_call(
        body, grid=(b, 4, nq), in_specs=[qspec, kspec, vspec, ospec, ospec, ospec], out_specs=[qspec, kspec, vspec],
        out_shape=[SDS((m, 1024), F32), SDS((m, 1024), F32), SDS((m, 512), F32)],
        compiler_params=_cp("parallel", "parallel", "arbitrary"), name="attn_bwd",
    )(q, kf, v, o, lse, do)


def _conv3(u, w_ref, b_ref, t):
    up, un = _prev_next(u, t)
    return w_ref[0:1, :] * up + w_ref[1:2, :] * u + w_ref[2:3, :] * un + b_ref[...], up, un


def _ffn_mid_fwd(ug3, uv3, cw, cb):
    b, t, f = ug3.shape
    nc = f // 256

    def body(ug_ref, uv_ref, wg_ref, wv_ref, bg_ref, bv_ref, a_ref):
        gc, _, _ = _conv3(ug_ref[0], wg_ref, bg_ref, t)
        vc, _, _ = _conv3(uv_ref[0], wv_ref, bv_ref, t)
        a_ref[0] = (gc * jax.nn.sigmoid(gc) * vc).astype(BF16)

    blk = BS((1, t, 256), lambda i, j: (i, 0, j))
    return pl.pallas_call(
        body, grid=(b, nc),
        in_specs=[blk, blk, BS((3, 256), lambda i, j: (0, j)), BS((3, 256), lambda i, j: (0, j + nc)),
                  BS((1, 256), lambda i, j: (0, j)), BS((1, 256), lambda i, j: (0, j + nc))],
        out_specs=blk, out_shape=SDS((b, t, f), BF16), compiler_params=_cp("parallel", "parallel"), name="ffn_mid_fwd",
    )(ug3, uv3, cw, cw, cb, cb)


def _ffn_mid_bwd(ug3, uv3, cw, cb, da3):
    b, t, f = ug3.shape
    nc = f // 256

    def half(u, up, un, dc, w_ref):
        dprev, dnext = _prev_next(dc, t)
        du = w_ref[1:2, :] * dc + w_ref[0:1, :] * dnext + w_ref[2:3, :] * dprev
        sums = [jnp.sum(dc * q, axis=0, keepdims=True) for q in (up, u, un)] + [jnp.sum(dc, axis=0, keepdims=True)]
        row = lax.broadcasted_iota(jnp.int32, (8, 256), 0)
        tab = jnp.zeros((8, 256), F32)
        for i, s in enumerate(sums):
            tab = jnp.where(row == i, s, tab)
        return du, tab

    def body(ug_ref, uv_ref, wg_ref, wv_ref, bg_ref, bv_ref, da_ref, dug_ref, duv_ref, tg_ref, tv_ref):
        ug, uv, da = ug_ref[0], uv_ref[0], da_ref[0]
        gc, gp, gn = _conv3(ug, wg_ref, bg_ref, t)
        vc, vp, vn = _conv3(uv, wv_ref, bv_ref, t)
        sg = jax.nn.sigmoid(gc)
        d_gc = da * vc * (sg * (1.0 + gc * (1.0 - sg)))
        d_vc = da * (gc * sg)
        dug, tg = half(ug, gp, gn, d_gc, wg_ref)
        duv, tv = half(uv, vp, vn, d_vc, wv_ref)
        dug_ref[0] = dug.astype(BF16)
        duv_ref[0] = duv.astype(BF16)

        @pl.when(pl.program_id(1) == 0)
        def _():
            tg_ref[...] = jnp.zeros_like(tg_ref)
            tv_ref[...] = jnp.zeros_like(tv_ref)

        tg_ref[...] += tg
        tv_ref[...] += tv

    blk = BS((1, t, 256), lambda j, i: (i, 0, j))
    tab = BS((8, 256), lambda j, i: (0, j))
    return pl.pallas_call(
        body, grid=(nc, b),
        in_specs=[blk, blk, BS((3, 256), lambda j, i: (0, j)), BS((3, 256), lambda j, i: (0, j + nc)),
                  BS((1, 256), lambda j, i: (0, j)), BS((1, 256), lambda j, i: (0, j + nc)), blk],
        out_specs=[blk, blk, tab, tab],
        out_shape=[SDS((b, t, f), BF16), SDS((b, t, f), BF16), SDS((8, f), F32), SDS((8, f), F32)],
        compiler_params=_cp("parallel", "arbitrary"), name="ffn_mid_bwd",
    )(ug3, uv3, cw, cw, cb, cb, da3)


def _add_rows(parts, name, out_dtype=F32):
    r = parts[0].shape[0]
    tr = _row_tile(r, 1024)
    n = len(parts)

    def body(*refs):
        acc = refs[0][...].astype(F32)
        for q in refs[1:n]:
            acc = acc + q[...].astype(F32)
        refs[n][...] = acc.astype(out_dtype)

    row = BS((tr, 128), lambda i: (i, 0))
    return pl.pallas_call(
        body, grid=(r // tr,), in_specs=[row] * n, out_specs=row, out_shape=SDS((r, 128), out_dtype),
        compiler_params=_cp("parallel"), name=name,
    )(*parts)


def _adamw(w, g, m, v, name):
    lead = w.shape[:-2]
    r, c = w.shape[-2:]
    tr = _row_tile(r)

    def body(w_ref, g_ref, m_ref, v_ref, d_ref, m2_ref, v2_ref):
        d, m2, v2 = _adamw_math(w_ref[...], g_ref[...], m_ref[...], v_ref[...])
        d_ref[...] = d
        m2_ref[...] = m2
        v2_ref[...] = v2

    blk = BS((1,) * len(lead) + (tr, c), lambda i: (0,) * len(lead) + (i, 0))
    return pl.pallas_call(
        body, grid=(r // tr,), in_specs=[blk] * 4, out_specs=[blk] * 3, out_shape=[SDS(w.shape, F32)] * 3,
        compiler_params=_cp("parallel"), name=name,
    )(w, g, m, v)


def _place():
    return lax.axis_index("x"), lax.axis_index("y"), lax.axis_index("c")


def _flip(v, bit):
    return 1 - v if bit else v


def _allgather_weights(shard):
    def body(x_ref, out_ref, send_sems, recv_sems):
        _gather_halves(x_ref, out_ref, send_sems, recv_sems, "start")
        _gather_halves(x_ref, out_ref, send_sems, recv_sems, "finish")

    return pl.pallas_call(
        body, out_shape=_gathered_shape(shard), in_specs=[ANY], out_specs=ANY, scratch_shapes=_gather_sems(),
        name="allgather_weights",
    )(shard)


def _gathered_shape(shard):
    return SDS((8 * (shard.shape[0] // 2), 128), shard.dtype)


def _gather_sems():
    return [pltpu.SemaphoreType.DMA((6,)), pltpu.SemaphoreType.DMA((6,))]


def _gather_halves(x_ref, out_ref, send_sems, recv_sems, phase):
    rh = x_ref.shape[0] // 2
    x, y, c = _place()
    me, sibling = (x, y, c), (x, y, 1 - c)
    chips = [(1 - x, y), (x, 1 - y), (1 - x, 1 - y)]
    mine_src = x_ref.at[pl.ds(c * rh, rh), :]

    def rows(px, py, pc):
        return out_ref.at[pl.ds((4 * px + 2 * py + pc) * rh, rh), :]

    def copy(k, block, to, src=None):
        return pltpu.make_async_remote_copy(
            src_ref=rows(*block) if src is None else src, dst_ref=rows(*block), send_sem=send_sems.at[k],
            recv_sem=recv_sems.at[k], device_id=to, device_id_type=MESH,
        )

    first = [copy(j, me, (*chip, c), src=mine_src) for j, chip in enumerate(chips)]
    if phase == "start":
        for cp in first:
            cp.start()
        return
    passed = [copy(3 + j, (*chip, c), sibling) for j, chip in enumerate(chips)]
    for j, chip in enumerate(chips):
        copy(j, (*chip, c), me).wait_recv()
        passed[j].start()
    for j, chip in enumerate(chips):
        copy(3 + j, (*chip, 1 - c), me).wait_recv()
    for cp in first + passed:
        cp.wait_send()


def _scatter_partials(g_ref, recv_ref, send_sems, recv_sems, phase):
    x, y, c = _place()
    copies = []
    for j, (fx, fy) in enumerate(((1, 0), (0, 1), (1, 1))):
        px, py = _flip(x, fx), _flip(y, fy)
        copies.append(pltpu.make_async_remote_copy(
            src_ref=g_ref.at[2 * px + py], dst_ref=recv_ref.at[j], send_sem=send_sems.at[j], recv_sem=recv_sems.at[j],
            device_id=(px, py, c), device_id_type=MESH,
        ))
    if phase == "start":
        for cp in copies:
            cp.start()
        return
    for cp in copies:
        cp.wait_recv()
    for cp in copies:
        cp.wait_send()


def _scatter_to_all(g_ref, recv_ref, send_sems, recv_sems, phase):
    rh = g_ref.shape[1] // 2
    x, y, c = _place()
    copies = []
    for k in range(1, 8):
        px, py, pc = _flip(x, k >> 2 & 1), _flip(y, k >> 1 & 1), _flip(c, k & 1)
        copies.append(pltpu.make_async_remote_copy(
            src_ref=g_ref.at[2 * px + py, pl.ds(pc * rh, rh), :], dst_ref=recv_ref.at[k - 1], send_sem=send_sems.at[k - 1],
            recv_sem=recv_sems.at[k - 1], device_id=(px, py, pc), device_id_type=MESH,
        ))
    if phase == "start":
        for cp in copies:
            cp.start()
        return
    for cp in copies:
        cp.wait_recv()
    for cp in copies:
        cp.wait_send()


def _scatter_to_chips(g):
    def body(g_ref, recv_ref, send_sems, recv_sems):
        _scatter_partials(g_ref, recv_ref, send_sems, recv_sems, "start")
        _scatter_partials(g_ref, recv_ref, send_sems, recv_sems, "finish")

    return pl.pallas_call(
        body, out_shape=SDS((3,) + g.shape[1:], g.dtype), in_specs=[ANY], out_specs=ANY,
        scratch_shapes=[pltpu.SemaphoreType.DMA((3,)), pltpu.SemaphoreType.DMA((3,))], name="scatter_grads",
    )(g)


def _send_to_sibling(a, half_of_rows):
    rh = a.shape[1] // 2

    def body(a_ref, b_ref, send_sem, recv_sem):
        x, y, c = _place()
        src = a_ref.at[:, pl.ds((1 - c) * rh, rh), :] if half_of_rows else a_ref
        cp = pltpu.make_async_remote_copy(
            src_ref=src, dst_ref=b_ref, send_sem=send_sem, recv_sem=recv_sem, device_id=(x, y, 1 - c), device_id_type=MESH
        )
        cp.start()
        cp.wait()

    shape = (a.shape[0], rh, 128) if half_of_rows else a.shape
    return pl.pallas_call(
        body, out_shape=SDS(shape, a.dtype), in_specs=[ANY], out_specs=ANY,
        scratch_shapes=[pltpu.SemaphoreType.DMA, pltpu.SemaphoreType.DMA],
        name="sibling_halves" if half_of_rows else "sibling_swap",
    )(a)


def _allreduce_small(v):
    r = v.shape[0]

    def body(v_ref, out_ref, buf_ref, send_sems, recv_sems):
        x, y, c = _place()
        buf_ref[0] = v_ref[...]
        copies = []
        for k in range(1, 8):
            peer = (_flip(x, k >> 2 & 1), _flip(y, k >> 1 & 1), _flip(c, k & 1))
            cp = pltpu.make_async_remote_copy(
                src_ref=v_ref, dst_ref=buf_ref.at[k], send_sem=send_sems.at[k - 1], recv_sem=recv_sems.at[k - 1],
                device_id=peer, device_id_type=MESH,
            )
            cp.start()
            copies.append(cp)
        for cp in copies:
            cp.wait_recv()
        acc = None
        for d in range(8):
            slot = 4 * _flip(x, d >> 2 & 1) + 2 * _flip(y, d >> 1 & 1) + _flip(c, d & 1)
            term = buf_ref[slot]
            acc = term if acc is None else acc + term
        out_ref[...] = acc
        for cp in copies:
            cp.wait_send()

    return pl.pallas_call(
        body, out_shape=SDS(v.shape, F32), in_specs=[VMEM], out_specs=VMEM,
        scratch_shapes=[pltpu.VMEM((8, r, 128), F32), pltpu.SemaphoreType.DMA((7,)), pltpu.SemaphoreType.DMA((7,))],
        name="allreduce_small",
    )(v)


_BIG_A = (
    ("w_in", 0, False),
    ("decay_w2_fwd", 1, False), ("decay_w2_bwd", 1, False), ("iclr_a2_fwd", 1, False),
    ("iclr_a2_bwd", 1, False), ("gate_g2", 1, False),
)
_BIG_B = (
    ("w_uq", 0, False), ("w_ukv", 1, False), ("w_out", 0, False), ("w_ffn_up", 1, False), ("ffn_conv_w", 1, True),
    ("w_ffn_down", 0, False),
)
_BIG = _BIG_A + _BIG_B
_SMALL = (
    "ln_mix_g", "shift_mu_prev", "shift_mu_next", "decay_w0_fwd", "decay_w0_bwd", "iclr_a0_fwd", "iclr_a0_bwd", "k_k",
    "k_a", "r_k", "ln_x_g", "ln_x_b", "q_norm_g", "kv_norm_g", "mla_out_g", "ln_ffn_g", "ffn_conv_b", "ln_final_g",
)
_WEIGHTS = (
    "ln_mix_g", "w_in", "shift_mu_prev", "shift_mu_next", "decay_w0_fwd", "decay_w2_fwd", "decay_w0_bwd", "decay_w2_bwd",
    "iclr_a0_fwd", "iclr_a2_fwd", "iclr_a0_bwd", "iclr_a2_bwd", "gate_g2", "k_k", "k_a", "r_k", "ln_x_g", "ln_x_b",
    "q_norm_g", "w_uq", "kv_norm_g", "w_ukv", "mla_out_g", "w_out", "ln_ffn_g", "w_ffn_up", "ffn_conv_w", "ffn_conv_b",
    "w_ffn_down", "ln_final_g",
)


def _pad_rows(flat, rows):
    return jnp.pad(flat, (0, rows * 128 - flat.shape[0])).reshape(rows, 128)


def _rows_for(n, mult):
    rows = -(-n // 128)
    return -(-rows // mult) * mult


def _pack_shards_bf16(arrs, entries):
    parts = []
    for name, _, raw in entries:
        w = arrs[name][0]
        flat = lax.bitcast_convert_type(w, BF16).reshape(-1) if raw else w.astype(BF16).reshape(-1)
        parts.append(_pad_rows(flat, _rows_for(flat.shape[0], 32)))
    return jnp.concatenate(parts, axis=0)


def _unpack_gathered(g4, shard, chip, arrs, entries):
    out, off = {}, 0
    mine = (jnp.arange(N_CHIPS) == chip)[:, None, None]
    for name, axis, raw in entries:
        a, b = arrs[name].shape[1:]
        n = a * b * (2 if raw else 1)
        rows = _rows_for(n, 32)
        seg = jnp.where(mine, shard[None, off:off + rows], g4[:, off:off + rows]).reshape(4, rows * 128)[:, :n]
        off += rows
        if raw:
            seg = lax.bitcast_convert_type(seg.reshape(4, a * b, 2), F32)
        seg = seg.reshape(4, a, b)
        out[name] = jnp.concatenate([seg[s] for s in range(4)], axis=1) if axis == 1 else seg.reshape(4 * a, b)
    return out


def _pack_grads(full, arrs, entries, dtype=F32):
    parts = []
    for name, axis, _ in entries:
        a, b = arrs[name].shape[1:]
        g = full[name]
        if g.ndim == 3:
            sh = g
        else:
            sh = g.reshape(a, 4, b).transpose(1, 0, 2) if axis == 1 else g.reshape(4, a, b)
        rows = _rows_for(a * b, 8)
        flat = sh.reshape(4, a * b).astype(dtype)
        parts.append(jnp.pad(flat, ((0, 0), (0, rows * 128 - a * b))).reshape(4, rows, 128))
    total = sum(q.shape[1] for q in parts)
    parts.append(jnp.zeros((4, -(-total // 1024) * 1024 - total, 128), dtype))
    return jnp.concatenate(parts, axis=1)


def _unpack_grads(g, arrs, entries):
    out, off = {}, 0
    for name, _, _ in entries:
        a, b = arrs[name].shape[1:]
        rows = _rows_for(a * b, 8)
        out[name] = g[off:off + rows].reshape(-1)[:a * b].reshape(1, a, b)
        off += rows
    return out


def _pack_small(vals):
    flat = jnp.concatenate([vals[n].reshape(-1).astype(F32) for n in _SMALL] + [vals["_loss"].reshape(-1)])
    return _pad_rows(flat, _rows_for(flat.shape[0], 8))


def _unpack_small(buf, arrs):
    flat, out, off = buf.reshape(-1), {}, 0
    for n in _SMALL:
        size = arrs[n].size
        out[n] = flat[off:off + size].reshape(arrs[n].shape)
        off += size
    out["_loss"] = flat[off]
    return out


def _rot_cols(w):
    return jnp.concatenate([-w[..., 16:], w[..., :16]], axis=-1)


def _rot_cols_t(g):
    return jnp.concatenate([g[..., 16:], -g[..., :16]], axis=-1)


def _rope_tables(t):
    inv = jnp.power(ROPE_THETA, -jnp.arange(0, ROPE_DIM, 2, dtype=F32) / ROPE_DIM)
    ang = jnp.arange(t, dtype=F32)[:, None] * inv[None, :]
    one, zero = jnp.ones((t, 64), F32), jnp.zeros((t, 64), F32)
    cs = jnp.concatenate([one, jnp.cos(ang), jnp.cos(ang), zero[:, :32]], axis=1)
    sn = jnp.concatenate([zero, jnp.sin(ang), jnp.sin(ang), zero[:, :32]], axis=1)
    return cs, sn


def _block_diag(a, b):
    za = jnp.zeros_like(a)
    return jnp.concatenate([jnp.concatenate([a, za], axis=1), jnp.concatenate([za, b], axis=1)], axis=0)


def kernel(x, ln_mix_g, w_in, shift_mu_prev, shift_mu_next, decay_w0_fwd, decay_w2_fwd, decay_w0_bwd, decay_w2_bwd, iclr_a0_fwd, iclr_a2_fwd, iclr_a0_bwd, iclr_a2_bwd, gate_g2, k_k, k_a, r_k, ln_x_g, ln_x_b, q_norm_g, w_uq, kv_norm_g, w_ukv, mla_out_g, w_out, ln_ffn_g, w_ffn_up, ffn_conv_w, ffn_conv_b, w_ffn_down, ln_final_g, loss_target, m_ln_mix_g, m_w_in, m_shift_mu_prev, m_shift_mu_next, m_decay_w0_fwd, m_decay_w2_fwd, m_decay_w0_bwd, m_decay_w2_bwd, m_iclr_a0_fwd, m_iclr_a2_fwd, m_iclr_a0_bwd, m_iclr_a2_bwd, m_gate_g2, m_k_k, m_k_a, m_r_k, m_ln_x_g, m_ln_x_b, m_q_norm_g, m_w_uq, m_kv_norm_g, m_w_ukv, m_mla_out_g, m_w_out, m_ln_ffn_g, m_w_ffn_up, m_ffn_conv_w, m_ffn_conv_b, m_w_ffn_down, m_ln_final_g, v_ln_mix_g, v_w_in, v_shift_mu_prev, v_shift_mu_next, v_decay_w0_fwd, v_decay_w2_fwd, v_decay_w0_bwd, v_decay_w2_bwd, v_iclr_a0_fwd, v_iclr_a2_fwd, v_iclr_a0_bwd, v_iclr_a2_bwd, v_gate_g2, v_k_k, v_k_a, v_r_k, v_ln_x_g, v_ln_x_b, v_q_norm_g, v_w_uq, v_kv_norm_g, v_w_ukv, v_mla_out_g, v_w_out, v_ln_ffn_g, v_w_ffn_up, v_ffn_conv_w, v_ffn_conv_b, v_w_ffn_down, v_ln_final_g):
    arrs = dict(locals())
    for pre in ("", "m_", "v_"):
        arrs[pre + "w_in"] = jnp.swapaxes(arrs[pre + "w_in"], 1, 2)
    b, t, d = x.shape
    m = b * t
    x2 = x.reshape(m, d)
    tgt = loss_target.reshape(m, d)
    vec = lambda n: arrs[n].reshape(1, -1)

    core = lax.axis_index("c")
    chip = 2 * lax.axis_index("x") + lax.axis_index("y")
    def unpack(gathered, shard, entries):
        return _unpack_gathered(gathered.reshape(N_CHIPS, -1, 128), shard, chip, arrs, entries)

    shard_a, shard_b = _pack_shards_bf16(arrs, _BIG_A), _pack_shards_bf16(arrs, _BIG_B)
    fw = unpack(_allgather_weights(shard_a), shard_a, _BIG_A)
    win = fw["w_in"]
    zc = jnp.zeros((64, d), BF16)
    w_kr = win[2944:2976]
    rot_kr = jnp.swapaxes(_rot_cols(jnp.swapaxes(w_kr, 0, 1)), 0, 1)
    win_m = jnp.concatenate([win[1920:2944], zc, w_kr, zc[:32], zc, rot_kr, zc[:32]], axis=0)
    win_r = win[:RWKV_COLS]
    head = jnp.arange(512) // HEAD_DIM
    rw = dict(
        w0=jnp.concatenate([vec("decay_w0_fwd"), vec("decay_w0_bwd")], axis=1),
        w2=_block_diag(fw["decay_w2_fwd"], fw["decay_w2_bwd"]).astype(F32),
        a0=jnp.concatenate([vec("iclr_a0_fwd"), vec("iclr_a0_bwd")], axis=1),
        a2=_block_diag(fw["iclr_a2_fwd"], fw["iclr_a2_bwd"]).astype(F32),
        g2=fw["gate_g2"].astype(F32), k_k=vec("k_k"), k_a=vec("k_a"), r_k=vec("r_k"), ln_x_g=vec("ln_x_g"), ln_x_b=vec("ln_x_b"),
        ones_bd=(head[:, None] == head[None, :]).astype(F32),
    )
    cs, sn = _rope_tables(t)

    n1 = _rms_fwd(x2, vec("ln_mix_g"), "rms_mix")
    zm = _mm(n1, win_m, "nt", "proj_in_mla")
    zr = _mm(n1, win_r, "nt", "proj_in_rwkv")
    zs = _shift_fwd(zr.reshape(b, t, RWKV_COLS), vec("shift_mu_prev"), vec("shift_mu_next"))
    zs2 = zs.reshape(m, RWKV_COLS)
    wf, wb, kf, kb, kk, kaf, kab, gate = _prep_fwd(zs2, rw)
    r4 = lambda a: a.reshape(b, t, 512)
    f2 = lambda a: a.reshape(m, 512)
    kk4 = r4(kk)
    ops_f = (r4(wf), r4(kf), r4(kaf))
    ops_b = (r4(wb), r4(kb), r4(kab))
    y_f, hist_f, y_b, hist_b, s_last, gathered_b = _scan_fwd(zs, kk4, ops_f, ops_b, shard_b)
    fw.update(unpack(gathered_b, shard_b, _BIG_B))
    uq = fw["w_uq"].astype(F32).reshape(Q_RANK, HEADS, 96)
    z32 = jnp.zeros((Q_RANK, HEADS, 32), F32)
    wq = jnp.concatenate([uq[..., :64], uq[..., 64:], z32], axis=-1).reshape(Q_RANK, 1024)
    wqr = jnp.concatenate([z32, z32, _rot_cols(uq[..., 64:]), z32], axis=-1).reshape(Q_RANK, 1024)
    ukv = fw["w_ukv"].astype(F32).reshape(KV_RANK, HEADS, 128)
    wk = jnp.concatenate([ukv[..., :64], jnp.zeros_like(ukv[..., :64])], axis=-1).reshape(KV_RANK, 1024)
    wv = ukv[..., 64:].reshape(KV_RANK, 512)
    mp = dict(q_norm_g=vec("q_norm_g"), kv_norm_g=vec("kv_norm_g"), wq=wq, wqr=wqr, wk=wk, wv=wv)
    w_up_g, w_up_v = fw["w_ffn_up"][:, :D_FF], fw["w_ffn_up"][:, D_FF:]
    cw, cb = fw["ffn_conv_w"], vec("ffn_conv_b")
    y_f, y_b = f2(y_f), f2(y_b)
    y_rwkv = _post_fwd(y_f, y_b, zs2, kf, kb, gate, rw)
    q, kfull, v = _mla_fwd(zm, cs, sn, mp, t)
    o, lse = _attn_fwd(q, kfull, v, b, t)
    y_mla = _rms_fwd(o, vec("mla_out_g"), "rms_mla_out")
    ymix = jnp.concatenate([y_rwkv, y_mla], axis=1)
    h1 = _mm(ymix, fw["w_out"], "nn", "proj_out", add=x2)
    n2 = _rms_fwd(h1, vec("ln_ffn_g"), "rms_ffn")
    ug = _mm(n2, w_up_g, "nn", "ffn_up_gate")
    uv = _mm(n2, w_up_v, "nn", "ffn_up_val")
    r3f = lambda a: a.reshape(b, t, D_FF)
    act = _ffn_mid_fwd(r3f(ug), r3f(uv), cw, cb).reshape(m, D_FF)
    h2 = _mm(act, fw["w_ffn_down"], "nn", "ffn_down", add=h1)
    loss_tab, dh2, g_ln_final = _final(h2, vec("ln_final_g"), tgt)

    gfull = {}
    dact = _mm(dh2, fw["w_ffn_down"], "nt", "d_ffn_act")
    gfull["w_ffn_down"] = _mm(act, dh2, "tn", "g_ffn_down")
    dug, duv, tab_g, tab_v = _ffn_mid_bwd(r3f(ug), r3f(uv), cw, cb, r3f(dact))
    dug, duv = dug.reshape(m, D_FF), duv.reshape(m, D_FF)
    gfull["ffn_conv_w"] = jnp.concatenate([tab_g[0:3], tab_v[0:3]], axis=1)
    g_conv_b = jnp.concatenate([tab_g[3:4], tab_v[3:4]], axis=1)
    dn2 = _mm(duv, w_up_v, "nt", "d_ffn_in_val", add=_mm(dug, w_up_g, "nt", "d_ffn_in_gate"))
    shard_cols = arrs["w_ffn_up"].shape[2]
    gfull["w_ffn_up"] = jnp.concatenate([_mm(n2, dug, "tn", "g_ffn_up_gate", column_blocks=shard_cols),
                                         _mm(n2, duv, "tn", "g_ffn_up_val", column_blocks=shard_cols)], axis=0)
    dh1, g_ln_ffn = _rms_bwd(h1, vec("ln_ffn_g"), dn2, "rms_ffn_bwd", dres=dh2)
    dymix = _mm(dh1, fw["w_out"], "nt", "d_mix")
    gfull["w_out"] = _mm(ymix, dh1, "tn", "g_w_out")
    do, g_mla_out = _rms_bwd(o, vec("mla_out_g"), dymix, "rms_mla_out_bwd", dy_block=1)
    dq, dk, dv = _attn_bwd(q, kfull, v, o, lse, do, b, t)
    dzm, g_qn, g_kvn, g_wq, g_wqr, g_wk, g_wv = _mla_bwd(zm, cs, sn, mp, t, dq, dk, dv)
    gq3, gqr3 = g_wq.reshape(Q_RANK, HEADS, 128), g_wqr.reshape(Q_RANK, HEADS, 128)
    gfull["w_uq"] = jnp.concatenate(
        [gq3[..., :64], gq3[..., 64:96] + _rot_cols_t(gqr3[..., 64:96])], axis=-1
    ).reshape(Q_RANK, HEADS * 96)
    gfull["w_ukv"] = jnp.concatenate(
        [g_wk.reshape(KV_RANK, HEADS, 128)[..., :64], g_wv.reshape(KV_RANK, HEADS, 64)], axis=-1
    ).reshape(KV_RANK, 1024)
    def cores_first(entries, tag):
        packed = _pack_grads(gfull, arrs, entries)
        rh = packed.shape[1] // 2
        own = lax.dynamic_slice_in_dim(packed, core * rh, rh, axis=1)
        sib = _send_to_sibling(packed, True)
        return _add_rows([own.reshape(4 * rh, 128), sib.reshape(4 * rh, 128)], "sum_cores_" + tag, BF16).reshape(4, rh, 128)

    def join_halves(half, entries):
        other = _send_to_sibling(half, False)
        lower = jnp.where(core == 0, half, other)
        upper = jnp.where(core == 0, other, half)
        return _unpack_grads(jnp.concatenate([lower, upper], axis=0), arrs, entries)

    part_b = _pack_grads(gfull, arrs, _BIG_B, BF16)
    dys, dr_p, dk_p, dv_p, dgate, g_rk, g_lnx_g, g_lnx_b = _post_bwd(y_f, y_b, zs2, kf, kb, gate, rw, dymix)
    (dr_f, dwf, dkf, dkk_f, dkaf, dv_f, dr_b, dwb, dkb, dkk_b, dkab, dv_b, recv_b) = _scan_bwd(
        zs, kk4, r4(dys), ops_f, hist_f, ops_b, hist_b, s_last, part_b)
    rh_b = part_b.shape[1] // 2
    mine_b = lax.dynamic_slice(part_b, (chip, core * rh_b, 0), (1, rh_b, 128))[0]
    g_big = join_halves(_add_rows([mine_b] + [recv_b[k] for k in range(7)], "sum_devices_b"), _BIG_B)
    cts = dict(dwf=f2(dwf), dwb=f2(dwb), dkf=f2(dkf), dkb=f2(dkb), dkk_f=f2(dkk_f), dkk_b=f2(dkk_b), dkaf=f2(dkaf), dkab=f2(dkab),
               dr_f=f2(dr_f), dr_b=f2(dr_b), dr_p=dr_p, dk_p=dk_p, dv_p=dv_p, dg=dgate, dv_f=f2(dv_f), dv_b=f2(dv_b))
    dzs, g_w0, g_w2, g_a0, g_a2, g_g2, g_kk, g_ka = _prep_bwd(zs2, rw, cts)
    dzr, g_mu_p, g_mu_n = _shift_bwd(dzs.reshape(b, t, RWKV_COLS), zr.reshape(b, t, RWKV_COLS), vec("shift_mu_prev"), vec("shift_mu_next"))
    dzr = dzr.reshape(m, RWKV_COLS)
    gfull["decay_w2_fwd"], gfull["decay_w2_bwd"] = g_w2[:64, :512], g_w2[64:, 512:]
    gfull["iclr_a2_fwd"], gfull["iclr_a2_bwd"] = g_a2[:64, :512], g_a2[64:, 512:]
    gfull["gate_g2"] = g_g2
    dn1 = _mm(dzr, win_r, "nn", "d_proj_in_rwkv", add=_mm(dzm, win_m, "nn", "d_proj_in_mla"))
    g_m = _mm(dzm, n1, "tn", "g_w_in_mla")
    g_r = _mm(dzr, n1, "tn", "g_w_in_rwkv")
    g_kr = g_m[1088:1120] + jnp.swapaxes(_rot_cols_t(jnp.swapaxes(g_m[1216:1248], 0, 1)), 0, 1)
    gfull["w_in"] = jnp.concatenate([g_r, g_m[:1024], g_kr], axis=0)
    dx, g_ln_mix = _rms_bwd(x2, vec("ln_mix_g"), dn1, "rms_mix_bwd", dres=dh1)

    part_a = cores_first(_BIG_A, "a")
    recv_a = _scatter_to_chips(part_a)
    mine_a = lax.dynamic_index_in_dim(part_a, chip, axis=0, keepdims=False)
    g_big.update(join_halves(_add_rows([mine_a, recv_a[0], recv_a[1], recv_a[2]], "sum_chips_a"), _BIG_A))
    small = {
        "ln_mix_g": g_ln_mix, "shift_mu_prev": g_mu_p, "shift_mu_next": g_mu_n, "decay_w0_fwd": g_w0[:, :512],
        "decay_w0_bwd": g_w0[:, 512:], "iclr_a0_fwd": g_a0[:, :512], "iclr_a0_bwd": g_a0[:, 512:], "k_k": g_kk, "k_a": g_ka,
        "r_k": g_rk, "ln_x_g": g_lnx_g, "ln_x_b": g_lnx_b, "q_norm_g": g_qn, "kv_norm_g": g_kvn, "mla_out_g": g_mla_out,
        "ln_ffn_g": g_ln_ffn, "ffn_conv_b": g_conv_b, "ln_final_g": g_ln_final,
        "_loss": jnp.pad(loss_tab[0, 0:1], (0, 127)),
    }
    g_small_buf = _allreduce_small(_pack_small(small))
    g_small = _unpack_small(g_small_buf, arrs)

    grads, deltas, new_m, new_v = {}, {}, {}, {}
    for name, _, _ in _BIG:
        grads[name] = g_big[name]
        deltas[name], new_m[name], new_v[name] = _adamw(
            arrs[name], g_big[name], arrs["m_" + name], arrs["v_" + name], "adamw_" + name)
    pk = lambda pre: _pack_small({**{n: arrs[pre + n] for n in _SMALL}, "_loss": jnp.zeros((128,), F32)})
    sd, sm, sv = _adamw(pk(""), g_small_buf, pk("m_"), pk("v_"), "adamw_small")
    sd, sm, sv = _unpack_small(sd, arrs), _unpack_small(sm, arrs), _unpack_small(sv, arrs)
    for n in _SMALL:
        grads[n], deltas[n], new_m[n], new_v[n] = g_small[n], sd[n], sm[n], sv[n]
    for group in (grads, deltas, new_m, new_v):
        group["w_in"] = jnp.swapaxes(group["w_in"], 1, 2)

    return (g_small["_loss"], dx.reshape(b, t, d), *[grads[n] for n in _WEIGHTS], *[deltas[n] for n in _WEIGHTS],
            *[new_m[n] for n in _WEIGHTS], *[new_v[n] for n in _WEIGHTS])
```

```python
import functools
import math

import jax
import jax.numpy as jnp
from jax import lax
from jax.experimental import pallas as pl
from jax.experimental.pallas import tpu as pltpu

F32, BF16 = jnp.float32, jnp.bfloat16
MESH = pl.DeviceIdType.MESH
ANY = pl.BlockSpec(memory_space=pl.ANY)
VMEM = pl.BlockSpec(memory_space=pltpu.VMEM)
BS = pl.BlockSpec
SDS = jax.ShapeDtypeStruct

NORM_EPS = 1e-6
GN_EPS = 64e-5
L2_EPS = 1e-12
HEADS = 8
HEAD_DIM = 64
ROPE_DIM = 32
ROPE_THETA = 10000.0
MLA_SCALE = (64 + ROPE_DIM) ** -0.5
Q_RANK, KV_RANK = 768, 256
RWKV_COLS = 1920
MLA_PAD_COLS = Q_RANK + KV_RANK + 256
D_FF = 2816
ADAM_LR, ADAM_B1, ADAM_B2, ADAM_EPS, ADAM_WD, ADAM_STEP = 0.001, 0.9, 0.999, 1e-08, 0.01, 10

V7X_VMEM_LIMIT = 56 * 1024 * 1024
SCAN_CHUNK = 32
N_CHIPS = 4


def _cp(*sem):
    return pltpu.CompilerParams(dimension_semantics=sem, vmem_limit_bytes=V7X_VMEM_LIMIT)


def _tile(n, cands=(512, 640, 384, 256, 128)):
    for c in cands:
        if n % c == 0:
            return c
    return n


def _row_tile(n, cap=256):
    best = n
    for t in range(8, cap + 1, 8):
        if n % t == 0:
            best = t
    return best if best <= cap or n <= cap else n


def _rms(x, g):
    ms = jnp.mean(x * x, axis=-1, keepdims=True)
    return x * lax.rsqrt(ms + NORM_EPS) * g


@jax.custom_vjp
def _bdot(x, w):
    return jnp.dot(x.astype(BF16), w.astype(BF16), preferred_element_type=F32)


def _bdot_fwd(x, w):
    return _bdot(x, w), (x, w)


def _bdot_bwd(res, ct):
    x, w = res
    c = ct.astype(BF16)
    dx = lax.dot_general(c, w.astype(BF16), (((1,), (1,)), ((), ())), preferred_element_type=F32)
    dw = lax.dot_general(x.astype(BF16), c, (((0,), (0,)), ((), ())), preferred_element_type=F32)
    return dx.astype(x.dtype), dw.astype(w.dtype)


_bdot.defvjp(_bdot_fwd, _bdot_bwd)


@jax.custom_vjp
def _headsum(x, ones_bd):
    hi = x.astype(BF16)
    mid = (x - hi.astype(F32)).astype(BF16)
    ob = ones_bd.astype(BF16)
    return jnp.dot(hi, ob, preferred_element_type=F32) + jnp.dot(mid, ob, preferred_element_type=F32)


def _headsum_fwd(x, ones_bd):
    return _headsum(x, ones_bd), ones_bd


def _headsum_bwd(ones_bd, ct):
    return _headsum(ct, ones_bd), jnp.zeros_like(ones_bd)


_headsum.defvjp(_headsum_fwd, _headsum_bwd)


def _prep_fn(zs, w0, w2, a0, a2, g2, k_k, k_a, ones_bd):
    k = zs[:, 512:1024]
    wd = zs[:, 1536:1664]
    ad = zs[:, 1664:1792]
    gd = zs[:, 1792:1920]
    logit = w0 + _bdot(jnp.tanh(wd), w2)
    w = jnp.exp(-math.exp(-0.5) * jax.nn.sigmoid(logit))
    a = jax.nn.sigmoid(a0 + _bdot(ad, a2))
    g = _bdot(jax.nn.sigmoid(gd), g2)
    kkr = k * k_k
    nrm = jnp.sqrt(_headsum(kkr * kkr, ones_bd))
    kk = kkr / jnp.maximum(nrm, L2_EPS)
    a_f, a_b = a[:, :512], a[:, 512:]
    kf = k * (1.0 + (a_f - 1.0) * k_a)
    kb = k * (1.0 + (a_b - 1.0) * k_a)
    return w[:, :512], w[:, 512:], kf, kb, kk, kk * a_f, kk * a_b, g


def _post_fn(y, r, kf, kb, v, g, r_k, ln_g, ln_b, ones_bd):
    mu =_headsum(y, ones_bd) * (1.0 / HEAD_DIM)
    yc = y - mu
    var = _headsum(yc * yc, ones_bd) * (1.0 / HEAD_DIM)
    yn = yc * lax.rsqrt(var + GN_EPS) * ln_g + ln_b
    bonus = _headsum(r * (kf + kb) * r_k, ones_bd) * v
    return (yn + bonus) * g


def _cat8(x):
    return jnp.concatenate([x] * HEADS, axis=1)


def _mla_fn(zm, cs, sn, gq, gkv, wq, wqr, wk, wv):
    cq = zm[:, :Q_RANK]
    ckv = zm[:, Q_RANK:Q_RANK + KV_RANK]
    kr = zm[:, Q_RANK + KV_RANK:Q_RANK + KV_RANK + 128]
    krr = zm[:, Q_RANK + KV_RANK + 128:]
    cqn = _rms(cq, gq)
    ckvn = _rms(ckv, gkv)
    q = (_bdot(cqn, wq) * _cat8(cs) + _bdot(cqn, wqr) * _cat8(sn)) * MLA_SCALE
    kro = kr * cs + krr * sn
    kfull = _bdot(ckvn, wk) + _cat8(kro)
    v = _bdot(ckvn, wv)
    return q, kfull, v


def _adamw_math(w, g, m, v):
    m2 = ADAM_B1 * m + (1.0 - ADAM_B1) * g
    v2 = ADAM_B2 * v + (1.0 - ADAM_B2) * (g * g)
    m_hat = m2 / (1.0 - ADAM_B1 ** ADAM_STEP)
    v_hat = v2 / (1.0 - ADAM_B2 ** ADAM_STEP)
    delta = -ADAM_LR * (m_hat / (jnp.sqrt(v_hat) + ADAM_EPS) + ADAM_WD * w)
    return delta, m2, v2


_DIMS = {"nn": (((1,), (0,)), ((), ())), "nt": (((1,), (1,)), ((), ())), "tn": (((0,), (0,)), ((), ()))}


def _mm(a, b, mode, name, out_dtype=F32, add=None, column_blocks=None):
    if mode == "nn":
        (m, k), (_, n) = a.shape, b.shape
    elif mode == "nt":
        (m, k), (n, _) = a.shape, b.shape
    else:
        (k, m), (_, n) = a.shape, b.shape
    big = (1024, 1408, 768, 640, 512, 384, 256, 128)
    tm, tn, tk = _tile(m, big), column_blocks or _tile(n, big), _tile(k, (512, 1408, 640, 384, 256, 128))
    nk = k // tk

    def body(a_ref, b_ref, *rest):
        if add is None:
            o_ref, acc_ref = rest
        else:
            add_ref, o_ref, acc_ref = rest
        kk = pl.program_id(2)

        @pl.when(kk == 0)
        def _():
            acc_ref[...] = jnp.zeros_like(acc_ref)

        acc_ref[...] += lax.dot_general(
            a_ref[...].astype(BF16), b_ref[...].astype(BF16), _DIMS[mode], preferred_element_type=F32
        )

        @pl.when(kk == nk - 1)
        def _():
            r = acc_ref[...]
            if add is not None:
                r = r + add_ref[...]
            o_ref[...] = r.astype(out_dtype).reshape(o_ref.shape)

    a_spec = BS((tk, tm), lambda i, j, kk: (kk, i)) if mode == "tn" else BS((tm, tk), lambda i, j, kk: (i, kk))
    b_spec = BS((tn, tk), lambda i, j, kk: (j, kk)) if mode == "nt" else BS((tk, tn), lambda i, j, kk: (kk, j))
    o_spec = BS((tm, tn), lambda i, j, kk: (i, j))
    ins, specs = [a, b], [a_spec, b_spec]
    if add is not None:
        ins.append(add)
        specs.append(o_spec)
    out_shape = SDS((m, n), out_dtype)
    if column_blocks:
        assert add is None
        o_spec, out_shape = BS((1, tm, tn), lambda i, j, kk: (j, i, 0)), SDS((n // tn, m, tn), out_dtype)
    return pl.pallas_call(
        body, grid=(m // tm, n // tn, nk), in_specs=specs, out_specs=o_spec, out_shape=out_shape,
        scratch_shapes=[pltpu.VMEM((tm, tn), F32)], compiler_params=_cp("parallel", "parallel", "arbitrary"), name=name,
    )(*ins)


def _rms_fwd(x, g, name):
    m, d = x.shape
    tr = _tile(m)

    def body(x_ref, g_ref, o_ref):
        o_ref[...] = _rms(x_ref[...], g_ref[...]).astype(BF16)

    return pl.pallas_call(
        body, grid=(m // tr,), in_specs=[BS((tr, d), lambda i: (i, 0)), BS((1, d), lambda i: (0, 0))],
        out_specs=BS((tr, d), lambda i: (i, 0)), out_shape=SDS((m, d), BF16), compiler_params=_cp("parallel"), name=name,
    )(x, g)


def _rms_bwd(x, g, dy, name, dres=None, dy_block=0):
    m, d = x.shape
    tr = _row_tile(m, 512)

    def body(x_ref, g_ref, dy_ref, *rest):
        if dres is None:
            dx_ref, dg_ref = rest
        else:
            dres_ref, dx_ref, dg_ref = rest
        _, vjp = jax.vjp(_rms, x_ref[...], g_ref[...])
        dx, dg = vjp(dy_ref[...])
        if dres is not None:
            dx = dx + dres_ref[...]
        dx_ref[...] = dx

        @pl.when(pl.program_id(0) == 0)
        def _():
            dg_ref[...] = jnp.zeros_like(dg_ref)

        dg_ref[...] += dg

    row = BS((tr, d), lambda i: (i, 0))
    vec = BS((1, d), lambda i: (0, 0))
    ins, specs = [x, g, dy], [row, vec, BS((tr, d), lambda i: (i, dy_block))]
    if dres is not None:
        ins.append(dres)
        specs.append(row)
    return pl.pallas_call(
        body, grid=(m // tr,), in_specs=specs, out_specs=[row, vec], out_shape=[SDS((m, d), F32), SDS((1, d), F32)],
        compiler_params=_cp("arbitrary"), name=name,
    )(*ins)


def _final(h, g, tgt):
    m, d = h.shape
    tr = _row_tile(m, 512)

    def loss_fn(hh, gg, tt):
        e = _rms(hh, gg) - tt
        return 0.5 * jnp.sum(e * e) * (1.0 / d)

    def body(h_ref, g_ref, t_ref, l_ref, dh_ref, dg_ref):
        val, (dh, dg) = jax.value_and_grad(loss_fn, argnums=(0, 1))(h_ref[...], g_ref[...], t_ref[...])
        dh_ref[...] = dh

        @pl.when(pl.program_id(0) == 0)
        def _():
            dg_ref[...] = jnp.zeros_like(dg_ref)
            l_ref[...] = jnp.zeros_like(l_ref)

        dg_ref[...] += dg
        l_ref[...] += jnp.full(l_ref.shape, val, F32)

    row = BS((tr, d), lambda i: (i, 0))
    vec = BS((1, d), lambda i: (0, 0))
    return pl.pallas_call(
        body, grid=(m // tr,), in_specs=[row, vec, row], out_specs=[BS((8, 128), lambda i: (0, 0)), row, vec],
        out_shape=[SDS((8, 128), F32), SDS((m, d), F32), SDS((1, d), F32)], compiler_params=_cp("arbitrary"), name="final_loss",
    )(h, g, tgt)


def _prev_next(z, t):
    row = lax.broadcasted_iota(jnp.int32, z.shape, 0)
    zp = jnp.where(row == 0, 0.0, pltpu.roll(z, 1, axis=0))
    zn = jnp.where(row == t - 1, 0.0, pltpu.roll(z, t - 1, axis=0))
    return zp, zn


def _shift_fwd(z3, mu_p, mu_n):
    b, t, c = z3.shape
    nc = c // 128

    def body(z_ref, mp_ref, mn_ref, o_ref):
        z = z_ref[0]
        zp, zn = _prev_next(z, t)
        o_ref[0] = z + mp_ref[...] * (zp - z) + mn_ref[...] * (zn - z)

    blk = BS((1, t, 128), lambda i, j: (i, 0, j))
    vec = BS((1, 128), lambda i, j: (0, j))
    return pl.pallas_call(
        body, grid=(b, nc), in_specs=[blk, vec, vec], out_specs=blk, out_shape=SDS((b, t, c), F32),
        compiler_params=_cp("parallel", "parallel"), name="shift_fwd",
    )(z3, mu_p, mu_n)


def _shift_bwd(dzs3, z3, mu_p, mu_n):
    b, t, c = z3.shape
    nc = c // 128

    def body(d_ref, z_ref, mp_ref, mn_ref, dz_ref, dmp_ref, dmn_ref):
        d, z = d_ref[0], z_ref[0]
        mp, mn = mp_ref[...], mn_ref[...]
        zp, zn = _prev_next(z, t)
        _, dp_next = _prev_next(d * mp, t)
        dn_prev, _ = _prev_next(d * mn, t)
        dz_ref[0] = (d * (1.0 - mp - mn) + dp_next + dn_prev).astype(BF16)

        @pl.when(pl.program_id(1) == 0)
        def _():
            dmp_ref[...] = jnp.zeros_like(dmp_ref)
            dmn_ref[...] = jnp.zeros_like(dmn_ref)

        dmp_ref[...] += jnp.sum(d * (zp - z), axis=0, keepdims=True)
        dmn_ref[...] += jnp.sum(d * (zn - z), axis=0, keepdims=True)

    blk = BS((1, t, 128), lambda j, i: (i, 0, j))
    vec = BS((1, 128), lambda j, i: (0, j))
    return pl.pallas_call(
        body, grid=(nc, b), in_specs=[blk, blk, vec, vec], out_specs=[blk, vec, vec],
        out_shape=[SDS((b, t, c), BF16), SDS((1, c), F32), SDS((1, c), F32)],
        compiler_params=_cp("parallel", "arbitrary"), name="shift_bwd",
    )(dzs3, z3, mu_p, mu_n)


def _const(shape):
    nd = len(shape)
    return BS(shape, lambda i: (0,) * nd)


def _prep_fwd(zs, p):
    m = zs.shape[0]
    tr = 512
    params = [p["w0"], p["w2"], p["a0"], p["a2"], p["g2"], p["k_k"], p["k_a"], p["ones_bd"]]

    def body(zs_ref, w0, w2, a0, a2, g2, kk_, ka_, bd, wf, wb, kf, kb, kk, kaf, kab, g):
        outs = _prep_fn(zs_ref[...], w0[...], w2[...], a0[...], a2[...], g2[...], kk_[...], ka_[...], bd[...])
        for ref, val in zip((wf, wb, kf, kb, kk, kaf, kab, g), outs):
            ref[...] = val

    row = BS((tr, 512), lambda i: (i, 0))
    return pl.pallas_call(
        body, grid=(m // tr,), in_specs=[BS((tr, RWKV_COLS), lambda i: (i, 0))] + [_const(q.shape) for q in params],
        out_specs=[row] * 8, out_shape=[SDS((m, 512), F32)] * 8, compiler_params=_cp("parallel"), name="rwkv_prep_fwd",
    )(zs, *params)


def _prep_bwd(zs, p, ct_rows):
    m = zs.shape[0]
    tr = 256
    params = [p["w0"], p["w2"], p["a0"], p["a2"], p["g2"], p["k_k"], p["k_a"]]
    names = ["dwf", "dwb", "dkf", "dkb", "dkk_f", "dkk_b", "dkaf", "dkab", "dr_f", "dr_b", "dr_p", "dk_p", "dv_p", "dg",
             "dv_f", "dv_b"]
    rows = [ct_rows[n] for n in names]

    def body(zs_ref, w0, w2, a0, a2, g2, kk_, ka_, bd, *rest):
        c = {n: r[...] for n, r in zip(names, rest[:len(names)])}
        outs = rest[len(names):]
        dzs_ref, grads = outs[0], outs[1:]
        ones_bd = bd[...]
        _, vjp = jax.vjp(
            lambda *q: _prep_fn(*q, ones_bd), zs_ref[...], w0[...], w2[...], a0[...], a2[...], g2[...], kk_[...], ka_[...]
        )
        cts = (c["dwf"], c["dwb"], c["dkf"] + c["dk_p"], c["dkb"] + c["dk_p"], c["dkk_f"] + c["dkk_b"], c["dkaf"], c["dkab"], c["dg"])
        dzs, *dparams = vjp(cts)
        dr = c["dr_f"] + c["dr_b"] + c["dr_p"]
        dv = c["dv_f"] + c["dv_b"] + c["dv_p"]
        dzs_ref[:, 0:512] = dzs[:, 0:512] + dr
        dzs_ref[:, 512:1024] = dzs[:, 512:1024]
        dzs_ref[:, 1024:1536] = dzs[:, 1024:1536] + dv
        dzs_ref[:, 1536:1920] = dzs[:, 1536:1920]

        @pl.when(pl.program_id(0) == 0)
        def _():
            for gr in grads:
                gr[...] = jnp.zeros_like(gr)

        for gr, val in zip(grads, dparams):
            gr[...] += val

    row = BS((tr, 512), lambda i: (i, 0))
    return pl.pallas_call(
        body, grid=(m // tr,),
        in_specs=[BS((tr, RWKV_COLS), lambda i: (i, 0))] + [_const(q.shape) for q in params] + [_const(p["ones_bd"].shape)]
        + [row] * len(names),
        out_specs=[BS((tr, RWKV_COLS), lambda i: (i, 0))] + [_const(q.shape) for q in params],
        out_shape=[SDS((m, RWKV_COLS), F32)] + [SDS(q.shape, F32) for q in params],
        compiler_params=_cp("arbitrary"), name="rwkv_prep_bwd",
    )(zs, *params, p["ones_bd"], *rows)


def _post_specs(tr):
    r = BS((tr, 512), lambda i: (i, 0))
    v = BS((tr, 512), lambda i: (i, 2))
    row = BS((tr, 512), lambda i: (i, 0))
    return r, v, row


def _post_fwd(y_f, y_b, zs, kf, kb, g, p):
    m = zs.shape[0]
    tr = 512
    r, v, row = _post_specs(tr)
    vecs = [p["r_k"], p["ln_x_g"], p["ln_x_b"], p["ones_bd"]]

    def body(yf, yb, r_ref, v_ref, kf_ref, kb_ref, g_ref, rk, lg, lb, bd, o_ref):
        o_ref[...] = _post_fn(
            yf[...] + yb[...], r_ref[...], kf_ref[...], kb_ref[...], v_ref[...], g_ref[...], rk[...], lg[...], lb[...], bd[...]
        ).astype(BF16)

    return pl.pallas_call(
        body, grid=(m // tr,), in_specs=[row, row, r, v, row, row, row] + [_const(q.shape) for q in vecs],
        out_specs=row, out_shape=SDS((m, 512), BF16), compiler_params=_cp("parallel"), name="rwkv_post_fwd",
    )(y_f, y_b, zs, zs, kf, kb, g, *vecs)


def _post_bwd(y_f, y_b, zs, kf, kb, g, p, dymix):
    m = zs.shape[0]
    tr = 256
    r, v, row = _post_specs(tr)
    vecs = [p["r_k"], p["ln_x_g"], p["ln_x_b"]]

    def body(yf, yb, r_ref, v_ref, kf_ref, kb_ref, g_ref, rk, lg, lb, bd, dy_ref, dyo, dr, dk, dv, dg, drk, dlg, dlb):
        ones_bd = bd[...]
        _, vjp = jax.vjp(
            lambda *q: _post_fn(*q, ones_bd),
            yf[...] + yb[...], r_ref[...], kf_ref[...], kb_ref[...], v_ref[...], g_ref[...], rk[...], lg[...], lb[...],
        )
        c_y, c_r, c_kf, _, c_v, c_g, c_rk, c_lg, c_lb = vjp(dy_ref[...])
        dyo[...] = c_y
        dr[...] = c_r
        dk[...] = c_kf
        dv[...] = c_v
        dg[...] = c_g

        @pl.when(pl.program_id(0) == 0)
        def _():
            for ref in (drk, dlg, dlb):
                ref[...] = jnp.zeros_like(ref)

        drk[...] += c_rk
        dlg[...] += c_lg
        dlb[...] += c_lb

    vec = _const((1, 512))
    return pl.pallas_call(
        body, grid=(m // tr,),
        in_specs=[row, row, r, v, row, row, row] + [_const(q.shape) for q in vecs] + [_const(p["ones_bd"].shape), row],
        out_specs=[row, row, row, row, row, vec, vec, vec],
        out_shape=[SDS((m, 512), F32)] * 5 + [SDS((1, 512), F32)] * 3,
        compiler_params=_cp("arbitrary"), name="rwkv_post_bwd",
    )(y_f, y_b, zs, zs, kf, kb, g, *vecs, p["ones_bd"], dymix)


SCAN_MXU_GROUPS = 2


def _half_ones():
    ri = lax.broadcasted_iota(jnp.int32, (128, 128), 0)
    ci = lax.broadcasted_iota(jnp.int32, (128, 128), 1)
    return jnp.where((ri < 64) == (ci < 64), 1.0, 0.0).astype(BF16)


def _half_sums(xs, ones):
    out = []
    per = -(-len(xs) // SCAN_MXU_GROUPS)
    for g in range(0, len(xs), per):
        part = xs[g:g + per]
        res = jnp.dot(jnp.concatenate(part, axis=0).astype(BF16), ones, preferred_element_type=F32)
        out += [res[64 * i:64 * i + 64] for i in range(len(part))]
    return out


def _scan_specs(b, t):
    nc = t // SCAN_CHUNK
    up, down = (lambda c: c), (lambda c: nc - 1 - c)
    rows = [BS((b, SCAN_CHUNK, 512), lambda c, ci=ci: (0, ci(c), 0)) for ci in (up, down)]
    vrows = [BS((b, SCAN_CHUNK, 512), lambda c, ci=ci: (0, ci(c), 2)) for ci in (up, down)]
    hist = [BS((SCAN_CHUNK, b * 4, 64, 128), lambda c, ci=ci: (ci(c), 0, 0, 0)) for ci in (up, down)]
    return nc, rows, vrows, hist


class _Window:
    def __init__(self, g, ascending):
        self.bases = [pl.multiple_of(g * 8, 8) if asc else pl.multiple_of(SCAN_CHUNK - 8 - g * 8, 8) for asc in ascending]
        self.ascending = ascending
        self.blocks = {}
        self.row_id = lax.broadcasted_iota(jnp.int32, (8, 128), 0)

    def j(self, d, s):
        return s if self.ascending[d] else 7 - s

    def time(self, d, s):
        return self.bases[d] + self.j(d, s)

    def row(self, ref, d, bi, cols, s):
        key = (id(ref), d, bi, cols.start)
        if key not in self.blocks:
            self.blocks[key] = ref[bi, pl.ds(self.bases[d], 8), cols]
        jj = self.j(d, s)
        return self.blocks[key][jj:jj + 1, :]

    def put(self, buf, key, d, s, row):
        prev = buf.get(key)
        new = jnp.broadcast_to(row, (8, 128))
        buf[key] = new if prev is None else jnp.where(self.row_id == self.j(d, s), new, prev)

    def flush(self, buf, refs_of):
        for key, val in buf.items():
            ref, d, bi, cols = refs_of(key)
            ref[bi, pl.ds(self.bases[d], 8), cols] = val


def _pairs(b):
    return [(bi * 4 + p, bi, slice(128 * p, 128 * p + 128)) for bi in range(b) for p in range(4)]


def _colsum(x):
    return jnp.sum(x, axis=0, keepdims=True)


def _pair_matvec(row, mat):
    rid = lax.broadcasted_iota(jnp.int32, (8, 64), 0)
    lhs = jnp.where(rid == 0, row[:, :64], jnp.where(rid == 1, row[:, 64:], 0.0))
    out = jnp.dot(lhs.astype(BF16), mat.astype(BF16), preferred_element_type=F32)
    lo = lax.broadcasted_iota(jnp.int32, (1, 128), 1) < 64
    return jnp.where(lo, out[0:1], out[1:2])


def _eye_mask():
    return (lax.broadcasted_iota(jnp.int32, (64, 128), 1) & 63) == lax.broadcasted_iota(jnp.int32, (64, 128), 0)


def _scan_fwd(zs, kk, ops_f, ops_b, shard):
    b, t = zs.shape[:2]
    nc, rows, vrows, hist = _scan_specs(b, t)
    npair = b * 4

    def body(*refs):
        ins, shard_ref, outs, s_ref = refs[:12], refs[12], refs[13:17], refs[17]
        gather = (shard_ref, *refs[18:21])
        dirs = [dict(zip(("r", "kk", "v", "w", "k", "ka", "y", "h"), (*ins[6 * d:6 * d + 6], *outs[2 * d:2 * d + 2])))
                for d in (0, 1)]

        @pl.when(pl.program_id(0) == 0)
        def _():
            s_ref[...] = jnp.zeros_like(s_ref)
            _gather_halves(*gather, "start")

        @pl.when(pl.program_id(0) == nc - 1)
        def _():
            _gather_halves(*gather, "finish")

        ones, eye = _half_ones(), _eye_mask()
        chains = [(d, pr, bi, cols) for d in (0, 1) for pr, bi, cols in _pairs(b)]

        def eight_steps(g, carry):
            win = _Window(g, (True, False))
            ybuf = {}
            for s in range(8):
                s_prev, xa = [], []
                for d, pr, bi, cols in chains:
                    q = dirs[d]
                    st = s_ref[d * npair + pr]
                    q["h"][win.time(d, s), pr] = st
                    s_prev.append(st)
                    xa += [st * win.row(q["kk"], d, bi, cols, s), jnp.where(eye, win.row(q["v"], d, bi, cols, s), 0.0)]
                ra = _half_sums(xa, ones)
                xb = []
                for i, (d, pr, bi, cols) in enumerate(chains):
                    q = dirs[d]
                    s_new = s_prev[i] * win.row(q["w"], d, bi, cols, s) - ra[2 * i] * win.row(q["ka"], d, bi, cols, s) \
                        + ra[2 * i + 1] * win.row(q["k"], d, bi, cols, s)
                    s_ref[d * npair + pr] = s_new
                    xb.append(s_new * win.row(q["r"], d, bi, cols, s))
                rb = _half_sums(xb, ones)
                for i, (d, pr, bi, cols) in enumerate(chains):
                    win.put(ybuf, i, d, s, _colsum(jnp.where(eye, rb[i], 0.0)))
            win.flush(ybuf, lambda i: (dirs[chains[i][0]]["y"], chains[i][0], chains[i][2], chains[i][3]))
            return carry

        def sixteen_steps(g2, carry):
            return eight_steps(2 * g2 + 1, eight_steps(2 * g2, carry))

        lax.fori_loop(0, SCAN_CHUNK // 16, sixteen_steps, 0)

    row_shape, hist_shape = SDS((b, t, 512), F32), SDS((t, npair, 64, 128), F32)
    state = (2 * npair, 64, 128)
    return pl.pallas_call(
        body, grid=(nc,), in_specs=sum(([rows[d], rows[d], vrows[d]] + [rows[d]] * 3 for d in (0, 1)), []) + [ANY],
        out_specs=[rows[0], hist[0], rows[1], hist[1], BS(state, lambda c: (0, 0, 0)), ANY],
        out_shape=[row_shape, hist_shape, row_shape, hist_shape, SDS(state, F32), _gathered_shape(shard)],
        scratch_shapes=_gather_sems(), compiler_params=_cp("arbitrary"), name="wkv_scan",
    )(zs, kk, zs, *ops_f, zs, kk, zs, *ops_b, shard)


def _scan_bwd(zs, kk, dy, ops_f, hist_f, ops_b, hist_b, s_last, partials):
    b, t = zs.shape[:2]
    nc, rows, vrows, hist = _scan_specs(b, t)
    npair = b * 4
    names_in = ("r", "kk", "v", "dy", "w", "k", "ka", "h")
    names_out = ("dr", "dw", "dk", "dkk", "dka", "dv")

    def body(*refs):
        ins, last_ref, part_ref, outs, recv_ref = refs[:16], refs[16], refs[17], refs[18:30], refs[30]
        ds_ref, after_ref = refs[31], refs[32]
        scatter = (part_ref, recv_ref, refs[33], refs[34])
        dirs = [dict(zip(names_in + names_out, (*ins[8 * d:8 * d + 8], *outs[6 * d:6 * d + 6]))) for d in (0, 1)]

        @pl.when(pl.program_id(0) == 0)
        def _():
            ds_ref[...] = jnp.zeros_like(ds_ref)
            after_ref[...] = last_ref[...]
            _scatter_to_all(*scatter, "start")

        @pl.when(pl.program_id(0) == nc - 1)
        def _():
            _scatter_to_all(*scatter, "finish")

        ones, eye = _half_ones(), _eye_mask()
        chains = [(d, pr, bi, cols) for d in (0, 1) for pr, bi, cols in _pairs(b)]

        def eight_steps(g, carry):
            win = _Window(g, (False, True))
            obuf = {}
            s_after = [after_ref[d * npair + pr] for d, pr, _, _ in chains]
            for s in range(8):
                row = lambda name, d, bi, cols: win.row(dirs[d][name], d, bi, cols, s)
                s_prev, xa = [], []
                for d, pr, bi, cols in chains:
                    st = dirs[d]["h"][win.time(d, s), pr]
                    s_prev.append(st)
                    xa += [st * row("kk", d, bi, cols), jnp.where(eye, row("dy", d, bi, cols), 0.0)]
                ra = _half_sums(xa, ones)
                ds_now, xb = [], []
                for i, (d, pr, bi, cols) in enumerate(chains):
                    skk, dycol = ra[2 * i], ra[2 * i + 1]
                    ds = ds_ref[d * npair + pr] + dycol * row("r", d, bi, cols)
                    win.put(obuf, (i, "dr"), d, s, _pair_matvec(row("dy", d, bi, cols), s_after[i]))
                    win.put(obuf, (i, "dk"), d, s, _pair_matvec(row("v", d, bi, cols), ds))
                    win.put(obuf, (i, "dka"), d, s, -_colsum(ds * skk))
                    win.put(obuf, (i, "dw"), d, s, _colsum(ds * s_prev[i]))
                    ds_now.append(ds)
                    xb += [ds * row("k", d, bi, cols), ds * row("ka", d, bi, cols)]
                rb = _half_sums(xb, ones)
                for i, (d, pr, bi, cols) in enumerate(chains):
                    dskk_neg = rb[2 * i + 1]
                    win.put(obuf, (i, "dv"), d, s, _colsum(jnp.where(eye, rb[2 * i], 0.0)))
                    win.put(obuf, (i, "dkk"), d, s, -_colsum(s_prev[i] * dskk_neg))
                    ds_ref[d * npair + pr] = ds_now[i] * row("w", d, bi, cols) - dskk_neg * row("kk", d, bi, cols)
                s_after = s_prev
            for i, (d, pr, _, _) in enumerate(chains):
                after_ref[d * npair + pr] = s_after[i]
            win.flush(obuf, lambda key: (dirs[chains[key[0]][0]][key[1]], chains[key[0]][0], chains[key[0]][2], chains[key[0]][3]))
            return carry

        def sixteen_steps(g2, carry):
            return eight_steps(2 * g2 + 1, eight_steps(2 * g2, carry))

        lax.fori_loop(0, SCAN_CHUNK // 16, sixteen_steps, 0)

    row_shape = SDS((b, t, 512), F32)
    state = (2 * npair, 64, 128)
    return pl.pallas_call(
        body, grid=(nc,),
        in_specs=sum(([rows[d], rows[d], vrows[d]] + [rows[d]] * 4 + [hist[d]] for d in (1, 0)), [])
        + [BS(state, lambda c: (0, 0, 0)), ANY],
        out_specs=[rows[1]] * 6 + [rows[0]] * 6 + [ANY],
        out_shape=[row_shape] * 12 + [SDS((7, partials.shape[1] // 2, 128), partials.dtype)],
        scratch_shapes=[pltpu.VMEM(state, F32), pltpu.VMEM(state, F32), pltpu.SemaphoreType.DMA((7,)),
                        pltpu.SemaphoreType.DMA((7,))],
        compiler_params=_cp("arbitrary"), name="wkv_scan_bwd",
    )(zs, kk, zs, dy, *ops_f, hist_f, zs, kk, zs, dy, *ops_b, hist_b, s_last, partials)


def _mla_fwd(zm, cs, sn, p, t):
    m = zm.shape[0]
    tr = 512
    per = t // tr
    params = [p["q_norm_g"], p["kv_norm_g"], p["wq"], p["wqr"], p["wk"], p["wv"]]

    def body(z_ref, cs_ref, sn_ref, gq, gkv, wq, wqr, wk, wv, q_ref, k_ref, v_ref):
        q, kf, v = _mla_fn(z_ref[...], cs_ref[...], sn_ref[...], gq[...], gkv[...], wq[...], wqr[...], wk[...], wv[...])
        q_ref[...] = q.astype(BF16)
        k_ref[...] = kf.astype(BF16)
        v_ref[...] = v.astype(BF16)

    tab = BS((tr, 128), lambda i: (i % per, 0))
    return pl.pallas_call(
        body, grid=(m // tr,), in_specs=[BS((tr, MLA_PAD_COLS), lambda i: (i, 0)), tab, tab] + [_const(q.shape) for q in params],
        out_specs=[BS((tr, 1024), lambda i: (i, 0)), BS((tr, 1024), lambda i: (i, 0)), BS((tr, 512), lambda i: (i, 0))],
        out_shape=[SDS((m, 1024), BF16), SDS((m, 1024), BF16), SDS((m, 512), BF16)], compiler_params=_cp("parallel"), name="mla_prep_fwd",
    )(zm, cs, sn, *params)


def _mla_bwd(zm, cs, sn, p, t, dq, dk, dv):
    m = zm.shape[0]
    tr = 256
    per = t // tr
    params = [p["q_norm_g"], p["kv_norm_g"], p["wq"], p["wqr"], p["wk"], p["wv"]]

    def body(z_ref, cs_ref, sn_ref, gq, gkv, wq, wqr, wk, wv, dq_ref, dk_ref, dv_ref, dz_ref, *grads):
        cs_v, sn_v = cs_ref[...], sn_ref[...]
        _, vjp = jax.vjp(
            lambda *q: _mla_fn(q[0], cs_v, sn_v, *q[1:]), z_ref[...], gq[...], gkv[...], wq[...], wqr[...], wk[...], wv[...]
        )
        dz, *dparams = vjp((dq_ref[...], dk_ref[...], dv_ref[...]))
        dz_ref[...] = dz.astype(BF16)

        @pl.when(pl.program_id(0) == 0)
        def _():
            for gr in grads:
                gr[...] = jnp.zeros_like(gr)

        for gr, val in zip(grads, dparams):
            gr[...] += val

    tab = BS((tr, 128), lambda i: (i % per, 0))
    wide = BS((tr, 1024), lambda i: (i, 0))
    return pl.pallas_call(
        body, grid=(m // tr,),
        in_specs=[BS((tr, MLA_PAD_COLS), lambda i: (i, 0)), tab, tab] + [_const(q.shape) for q in params]
        + [wide, wide, BS((tr, 512), lambda i: (i, 0))],
        out_specs=[BS((tr, MLA_PAD_COLS), lambda i: (i, 0))] + [_const(q.shape) for q in params],
        out_shape=[SDS((m, MLA_PAD_COLS), BF16)] + [SDS(q.shape, F32) for q in params],
        compiler_params=_cp("arbitrary"), name="mla_prep_bwd",
    )(zm, cs, sn, *params, dq, dk, dv)


_NT = (((1,), (1,)), ((), ()))
_TN = (((0,), (0,)), ((), ()))


def _attn_fwd(q, kf, v, b, t):
    m = q.shape[0]
    tq = 256
    nq = t // tq

    def body(q_ref, k_ref, v_ref, o_ref, l_ref):
        lo = lax.broadcasted_iota(jnp.int32, (1, 128), 1) < 64
        v_all = v_ref[...]
        o = jnp.zeros((tq, 128), F32)
        lse = []
        for h in range(2):
            hs = slice(128 * h, 128 * h + 128)
            s = lax.dot_general(q_ref[:, hs], k_ref[:, hs], _NT, preferred_element_type=F32)
            mx = jnp.max(s, axis=1, keepdims=True)
            e = jnp.exp(s - mx)
            den = jnp.sum(e, axis=1, keepdims=True)
            vh = jnp.where(lo if h == 0 else jnp.logical_not(lo), v_all, jnp.zeros_like(v_all))
            o = o + jnp.dot(e.astype(BF16), vh, preferred_element_type=F32) / den
            lse.append(mx + jnp.log(den))
        o_ref[...] = o
        l_ref[...] = jnp.where(lo, lse[0], lse[1])

    return pl.pallas_call(
        body, grid=(b, 4, nq),
        in_specs=[BS((tq, 256), lambda bi, hp, i: (bi * nq + i, hp)), BS((t, 256), lambda bi, hp, i: (bi, hp)),
                  BS((t, 128), lambda bi, hp, i: (bi, hp))],
        out_specs=[BS((tq, 128), lambda bi, hp, i: (bi * nq + i, hp))] * 2,
        out_shape=[SDS((m, 512), F32), SDS((m, 512), F32)], compiler_params=_cp("parallel", "parallel", "arbitrary"), name="attn_fwd",
    )(q, kf, v)


def _attn_bwd(q, kf, v, o, lse, do, b, t):
    m = q.shape[0]
    tq = 256
    nq = t // tq

    def body(q_ref, k_ref, v_ref, o_ref, l_ref, do_ref, dq_ref, dk_ref, dv_ref):
        lo = lax.broadcasted_iota(jnp.int32, (1, 128), 1) < 64

        @pl.when(pl.program_id(2) == 0)
        def _():
            dk_ref[...] = jnp.zeros_like(dk_ref)
            dv_ref[...] = jnp.zeros_like(dv_ref)

        v_all, o_all, l_all, do_all = v_ref[...], o_ref[...], l_ref[...], do_ref[...]
        dv_acc = jnp.zeros((t, 128), F32)
        for h in range(2):
            hs = slice(128 * h, 128 * h + 128)
            mask = lo if h == 0 else jnp.logical_not(lo)
            qh, kh = q_ref[:, hs], k_ref[:, hs]
            s = lax.dot_general(qh, kh, _NT, preferred_element_type=F32)
            lse_h = jnp.max(jnp.where(mask, l_all, -jnp.inf), axis=1, keepdims=True)
            pr = jnp.exp(s - lse_h)
            do_h = jnp.where(mask, do_all, 0.0)
            dp = lax.dot_general(do_h.astype(BF16), v_all, _NT, preferred_element_type=F32)
            dsum = jnp.sum(do_h * o_all, axis=1, keepdims=True)
            ds = (pr * (dp - dsum)).astype(BF16)
            dq_ref[:, hs] = jnp.dot(ds, kh, preferred_element_type=F32)
            dk_ref[:, hs] += lax.dot_general(ds, qh, _TN, preferred_element_type=F32)
            dv_acc = dv_acc + lax.dot_general(pr.astype(BF16), do_h.astype(BF16), _TN, preferred_element_type=F32)
        dv_ref[...] += dv_acc

    qspec = BS((tq, 256), lambda bi, hp, i: (bi * nq + i, hp))
    kspec = BS((t, 256), lambda bi, hp, i: (bi, hp))
    vspec = BS((t, 128), lambda bi, hp, i: (bi, hp))
    ospec = BS((tq, 128), lambda bi, hp, i: (bi * nq + i, hp))
    return pl.pallas_call(
        body, grid=(b, 4, nq), in_specs=[qspec, kspec, vspec, ospec, ospec, ospec], out_specs=[qspec, kspec, vspec],
        out_shape=[SDS((m, 1024), F32), SDS((m, 1024), F32), SDS((m, 512), F32)],
        compiler_params=_cp("parallel", "parallel", "arbitrary"), name="attn_bwd",
    )(q, kf, v, o, lse, do)


def _conv3(u, w_ref, b_ref, t):
    up, un = _prev_next(u, t)
    return w_ref[0:1, :] * up + w_ref[1:2, :] * u + w_ref[2:3, :] * un + b_ref[...], up, un


def _ffn_mid_fwd(ug3, uv3, cw, cb):
    b, t, f = ug3.shape
    nc = f // 256

    def body(ug_ref, uv_ref, wg_ref, wv_ref, bg_ref, bv_ref, a_ref):
        gc, _, _ = _conv3(ug_ref[0], wg_ref, bg_ref, t)
        vc, _, _ = _conv3(uv_ref[0], wv_ref, bv_ref, t)
        a_ref[0] = (gc * jax.nn.sigmoid(gc) * vc).astype(BF16)

    blk = BS((1, t, 256), lambda i, j: (i, 0, j))
    return pl.pallas_call(
        body, grid=(b, nc),
        in_specs=[blk, blk, BS((3, 256), lambda i, j: (0, j)), BS((3, 256), lambda i, j: (0, j + nc)),
                  BS((1, 256), lambda i, j: (0, j)), BS((1, 256), lambda i, j: (0, j + nc))],
        out_specs=blk, out_shape=SDS((b, t, f), BF16), compiler_params=_cp("parallel", "parallel"), name="ffn_mid_fwd",
    )(ug3, uv3, cw, cw, cb, cb)


def _ffn_mid_bwd(ug3, uv3, cw, cb, da3):
    b, t, f = ug3.shape
    nc = f // 256

    def half(u, up, un, dc, w_ref):
        dprev, dnext = _prev_next(dc, t)
        du = w_ref[1:2, :] * dc + w_ref[0:1, :] * dnext + w_ref[2:3, :] * dprev
        sums = [jnp.sum(dc * q, axis=0, keepdims=True) for q in (up, u, un)] + [jnp.sum(dc, axis=0, keepdims=True)]
        row = lax.broadcasted_iota(jnp.int32, (8, 256), 0)
        tab = jnp.zeros((8, 256), F32)
        for i, s in enumerate(sums):
            tab = jnp.where(row == i, s, tab)
        return du, tab

    def body(ug_ref, uv_ref, wg_ref, wv_ref, bg_ref, bv_ref, da_ref, dug_ref, duv_ref, tg_ref, tv_ref):
        ug, uv, da = ug_ref[0], uv_ref[0], da_ref[0]
        gc, gp, gn = _conv3(ug, wg_ref, bg_ref, t)
        vc, vp, vn = _conv3(uv, wv_ref, bv_ref, t)
        sg = jax.nn.sigmoid(gc)
        d_gc = da * vc * (sg * (1.0 + gc * (1.0 - sg)))
        d_vc = da * (gc * sg)
        dug, tg = half(ug, gp, gn, d_gc, wg_ref)
        duv, tv = half(uv, vp, vn, d_vc, wv_ref)
        dug_ref[0] = dug.astype(BF16)
        duv_ref[0] = duv.astype(BF16)

        @pl.when(pl.program_id(1) == 0)
        def _():
            tg_ref[...] = jnp.zeros_like(tg_ref)
            tv_ref[...] = jnp.zeros_like(tv_ref)

        tg_ref[...] += tg
        tv_ref[...] += tv

    blk = BS((1, t, 256), lambda j, i: (i, 0, j))
    tab = BS((8, 256), lambda j, i: (0, j))
    return pl.pallas_call(
        body, grid=(nc, b),
        in_specs=[blk, blk, BS((3, 256), lambda j, i: (0, j)), BS((3, 256), lambda j, i: (0, j + nc)),
                  BS((1, 256), lambda j, i: (0, j)), BS((1, 256), lambda j, i: (0, j + nc)), blk],
        out_specs=[blk, blk, tab, tab],
        out_shape=[SDS((b, t, f), BF16), SDS((b, t, f), BF16), SDS((8, f), F32), SDS((8, f), F32)],
        compiler_params=_cp("parallel", "arbitrary"), name="ffn_mid_bwd",
    )(ug3, uv3, cw, cw, cb, cb, da3)


def _add_rows(parts, name, out_dtype=F32):
    r = parts[0].shape[0]
    tr = _row_tile(r, 1024)
    n = len(parts)

    def body(*refs):
        acc = refs[0][...].astype(F32)
        for q in refs[1:n]:
            acc = acc + q[...].astype(F32)
        refs[n][...] = acc.astype(out_dtype)

    row = BS((tr, 128), lambda i: (i, 0))
    return pl.pallas_call(
        body, grid=(r // tr,), in_specs=[row] * n, out_specs=row, out_shape=SDS((r, 128), out_dtype),
        compiler_params=_cp("parallel"), name=name,
    )(*parts)


def _adamw(w, g, m, v, name):
    lead = w.shape[:-2]
    r, c = w.shape[-2:]
    tr = _row_tile(r)

    def body(w_ref, g_ref, m_ref, v_ref, d_ref, m2_ref, v2_ref):
        d, m2, v2 = _adamw_math(w_ref[...], g_ref[...], m_ref[...], v_ref[...])
        d_ref[...] = d
        m2_ref[...] = m2
        v2_ref[...] = v2

    blk = BS((1,) * len(lead) + (tr, c), lambda i: (0,) * len(lead) + (i, 0))
    return pl.pallas_call(
        body, grid=(r // tr,), in_specs=[blk] * 4, out_specs=[blk] * 3, out_shape=[SDS(w.shape, F32)] * 3,
        compiler_params=_cp("parallel"), name=name,
    )(w, g, m, v)


def _place():
    return lax.axis_index("x"), lax.axis_index("y"), lax.axis_index("c")


def _flip(v, bit):
    return 1 - v if bit else v


def _allgather_weights(shard):
    def body(x_ref, out_ref, send_sems, recv_sems):
        _gather_halves(x_ref, out_ref, send_sems, recv_sems, "start")
        _gather_halves(x_ref, out_ref, send_sems, recv_sems, "finish")

    return pl.pallas_call(
        body, out_shape=_gathered_shape(shard), in_specs=[ANY], out_specs=ANY, scratch_shapes=_gather_sems(),
        name="allgather_weights",
    )(shard)


def _gathered_shape(shard):
    return SDS((8 * (shard.shape[0] // 2), 128), shard.dtype)


def _gather_sems():
    return [pltpu.SemaphoreType.DMA((6,)), pltpu.SemaphoreType.DMA((6,))]


def _gather_halves(x_ref, out_ref, send_sems, recv_sems, phase):
    rh = x_ref.shape[0] // 2
    x, y, c = _place()
    me, sibling = (x, y, c), (x, y, 1 - c)
    chips = [(1 - x, y), (x, 1 - y), (1 - x, 1 - y)]
    mine_src = x_ref.at[pl.ds(c * rh, rh), :]

    def rows(px, py, pc):
        return out_ref.at[pl.ds((4 * px + 2 * py + pc) * rh, rh), :]

    def copy(k, block, to, src=None):
        return pltpu.make_async_remote_copy(
            src_ref=rows(*block) if src is None else src, dst_ref=rows(*block), send_sem=send_sems.at[k],
            recv_sem=recv_sems.at[k], device_id=to, device_id_type=MESH,
        )

    first = [copy(j, me, (*chip, c), src=mine_src) for j, chip in enumerate(chips)]
    if phase == "start":
        for cp in first:
            cp.start()
        return
    passed = [copy(3 + j, (*chip, c), sibling) for j, chip in enumerate(chips)]
    for j, chip in enumerate(chips):
        copy(j, (*chip, c), me).wait_recv()
        passed[j].start()
    for j, chip in enumerate(chips):
        copy(3 + j, (*chip, 1 - c), me).wait_recv()
    for cp in first + passed:
        cp.wait_send()


def _scatter_partials(g_ref, recv_ref, send_sems, recv_sems, phase):
    x, y, c = _place()
    copies = []
    for j, (fx, fy) in enumerate(((1, 0), (0, 1), (1, 1))):
        px, py = _flip(x, fx), _flip(y, fy)
        copies.append(pltpu.make_async_remote_copy(
            src_ref=g_ref.at[2 * px + py], dst_ref=recv_ref.at[j], send_sem=send_sems.at[j], recv_sem=recv_sems.at[j],
            device_id=(px, py, c), device_id_type=MESH,
        ))
    if phase == "start":
        for cp in copies:
            cp.start()
        return
    for cp in copies:
        cp.wait_recv()
    for cp in copies:
        cp.wait_send()


def _scatter_to_all(g_ref, recv_ref, send_sems, recv_sems, phase):
    rh = g_ref.shape[1] // 2
    x, y, c = _place()
    copies = []
    for k in range(1, 8):
        px, py, pc = _flip(x, k >> 2 & 1), _flip(y, k >> 1 & 1), _flip(c, k & 1)
        copies.append(pltpu.make_async_remote_copy(
            src_ref=g_ref.at[2 * px + py, pl.ds(pc * rh, rh), :], dst_ref=recv_ref.at[k - 1], send_sem=send_sems.at[k - 1],
            recv_sem=recv_sems.at[k - 1], device_id=(px, py, pc), device_id_type=MESH,
        ))
    if phase == "start":
        for cp in copies:
            cp.start()
        return
    for cp in copies:
        cp.wait_recv()
    for cp in copies:
        cp.wait_send()


def _scatter_to_chips(g):
    def body(g_ref, recv_ref, send_sems, recv_sems):
        _scatter_partials(g_ref, recv_ref, send_sems, recv_sems, "start")
        _scatter_partials(g_ref, recv_ref, send_sems, recv_sems, "finish")

    return pl.pallas_call(
        body, out_shape=SDS((3,) + g.shape[1:], g.dtype), in_specs=[ANY], out_specs=ANY,
        scratch_shapes=[pltpu.SemaphoreType.DMA((3,)), pltpu.SemaphoreType.DMA((3,))], name="scatter_grads",
    )(g)


def _send_to_sibling(a, half_of_rows):
    rh = a.shape[1] // 2

    def body(a_ref, b_ref, send_sem, recv_sem):
        x, y, c = _place()
        src = a_ref.at[:, pl.ds((1 - c) * rh, rh), :] if half_of_rows else a_ref
        cp = pltpu.make_async_remote_copy(
            src_ref=src, dst_ref=b_ref, send_sem=send_sem, recv_sem=recv_sem, device_id=(x, y, 1 - c), device_id_type=MESH
        )
        cp.start()
        cp.wait()

    shape = (a.shape[0], rh, 128) if half_of_rows else a.shape
    return pl.pallas_call(
        body, out_shape=SDS(shape, a.dtype), in_specs=[ANY], out_specs=ANY,
        scratch_shapes=[pltpu.SemaphoreType.DMA, pltpu.SemaphoreType.DMA],
        name="sibling_halves" if half_of_rows else "sibling_swap",
    )(a)


def _allreduce_small(v):
    r = v.shape[0]

    def body(v_ref, out_ref, buf_ref, send_sems, recv_sems):
        x, y, c = _place()
        buf_ref[0] = v_ref[...]
        copies = []
        for k in range(1, 8):
            peer = (_flip(x, k >> 2 & 1), _flip(y, k >> 1 & 1), _flip(c, k & 1))
            cp = pltpu.make_async_remote_copy(
                src_ref=v_ref, dst_ref=buf_ref.at[k], send_sem=send_sems.at[k - 1], recv_sem=recv_sems.at[k - 1],
                device_id=peer, device_id_type=MESH,
            )
            cp.start()
            copies.append(cp)
        for cp in copies:
            cp.wait_recv()
        acc = None
        for d in range(8):
            slot = 4 * _flip(x, d >> 2 & 1) + 2 * _flip(y, d >> 1 & 1) + _flip(c, d & 1)
            term = buf_ref[slot]
            acc = term if acc is None else acc + term
        out_ref[...] = acc
        for cp in copies:
            cp.wait_send()

    return pl.pallas_call(
        body, out_shape=SDS(v.shape, F32), in_specs=[VMEM], out_specs=VMEM,
        scratch_shapes=[pltpu.VMEM((8, r, 128), F32), pltpu.SemaphoreType.DMA((7,)), pltpu.SemaphoreType.DMA((7,))],
        name="allreduce_small",
    )(v)


_BIG_A = (
    ("w_in", 0, False),
    ("decay_w2_fwd", 1, False), ("decay_w2_bwd", 1, False), ("iclr_a2_fwd", 1, False),
    ("iclr_a2_bwd", 1, False), ("gate_g2", 1, False),
)
_BIG_B = (
    ("w_uq", 0, False), ("w_ukv", 1, False), ("w_out", 0, False), ("w_ffn_up", 1, False), ("ffn_conv_w", 1, True),
    ("w_ffn_down", 0, False),
)
_BIG = _BIG_A + _BIG_B
_SMALL = (
    "ln_mix_g", "shift_mu_prev", "shift_mu_next", "decay_w0_fwd", "decay_w0_bwd", "iclr_a0_fwd", "iclr_a0_bwd", "k_k",
    "k_a", "r_k", "ln_x_g", "ln_x_b", "q_norm_g", "kv_norm_g", "mla_out_g", "ln_ffn_g", "ffn_conv_b", "ln_final_g",
)
_WEIGHTS = (
    "ln_mix_g", "w_in", "shift_mu_prev", "shift_mu_next", "decay_w0_fwd", "decay_w2_fwd", "decay_w0_bwd", "decay_w2_bwd",
    "iclr_a0_fwd", "iclr_a2_fwd", "iclr_a0_bwd", "iclr_a2_bwd", "gate_g2", "k_k", "k_a", "r_k", "ln_x_g", "ln_x_b",
    "q_norm_g", "w_uq", "kv_norm_g", "w_ukv", "mla_out_g", "w_out", "ln_ffn_g", "w_ffn_up", "ffn_conv_w", "ffn_conv_b",
    "w_ffn_down", "ln_final_g",
)


def _pad_rows(flat, rows):
    return jnp.pad(flat, (0, rows * 128 - flat.shape[0])).reshape(rows, 128)


def _rows_for(n, mult):
    rows = -(-n // 128)
    return -(-rows // mult) * mult


def _pack_shards_bf16(arrs, entries):
    parts = []
    for name, _, raw in entries:
        w = arrs[name][0]
        flat = lax.bitcast_convert_type(w, BF16).reshape(-1) if raw else w.astype(BF16).reshape(-1)
        parts.append(_pad_rows(flat, _rows_for(flat.shape[0], 32)))
    return jnp.concatenate(parts, axis=0)


def _unpack_gathered(g4, shard, chip, arrs, entries):
    out, off = {}, 0
    mine = (jnp.arange(N_CHIPS) == chip)[:, None, None]
    for name, axis, raw in entries:
        a, b = arrs[name].shape[1:]
        n = a * b * (2 if raw else 1)
        rows = _rows_for(n, 32)
        seg = jnp.where(mine, shard[None, off:off + rows], g4[:, off:off + rows]).reshape(4, rows * 128)[:, :n]
        off += rows
        if raw:
            seg = lax.bitcast_convert_type(seg.reshape(4, a * b, 2), F32)
        seg = seg.reshape(4, a, b)
        out[name] = jnp.concatenate([seg[s] for s in range(4)], axis=1) if axis == 1 else seg.reshape(4 * a, b)
    return out


def _pack_grads(full, arrs, entries, dtype=F32):
    parts = []
    for name, axis, _ in entries:
        a, b = arrs[name].shape[1:]
        g = full[name]
        if g.ndim == 3:
            sh = g
        else:
            sh = g.reshape(a, 4, b).transpose(1, 0, 2) if axis == 1 else g.reshape(4, a, b)
        rows = _rows_for(a * b, 8)
        flat = sh.reshape(4, a * b).astype(dtype)
        parts.append(jnp.pad(flat, ((0, 0), (0, rows * 128 - a * b))).reshape(4, rows, 128))
    total = sum(q.shape[1] for q in parts)
    parts.append(jnp.zeros((4, -(-total // 1024) * 1024 - total, 128), dtype))
    return jnp.concatenate(parts, axis=1)


def _unpack_grads(g, arrs, entries):
    out, off = {}, 0
    for name, _, _ in entries:
        a, b = arrs[name].shape[1:]
        rows = _rows_for(a * b, 8)
        out[name] = g[off:off + rows].reshape(-1)[:a * b].reshape(1, a, b)
        off += rows
    return out


def _pack_small(vals):
    flat = jnp.concatenate([vals[n].reshape(-1).astype(F32) for n in _SMALL] + [vals["_loss"].reshape(-1)])
    return _pad_rows(flat, _rows_for(flat.shape[0], 8))


def _unpack_small(buf, arrs):
    flat, out, off = buf.reshape(-1), {}, 0
    for n in _SMALL:
        size = arrs[n].size
        out[n] = flat[off:off + size].reshape(arrs[n].shape)
        off += size
    out["_loss"] = flat[off]
    return out


def _rot_cols(w):
    return jnp.concatenate([-w[..., 16:], w[..., :16]], axis=-1)


def _rot_cols_t(g):
    return jnp.concatenate([g[..., 16:], -g[..., :16]], axis=-1)


def _rope_tables(t):
    inv = jnp.power(ROPE_THETA, -jnp.arange(0, ROPE_DIM, 2, dtype=F32) / ROPE_DIM)
    ang = jnp.arange(t, dtype=F32)[:, None] * inv[None, :]
    one, zero = jnp.ones((t, 64), F32), jnp.zeros((t, 64), F32)
    cs = jnp.concatenate([one, jnp.cos(ang), jnp.cos(ang), zero[:, :32]], axis=1)
    sn = jnp.concatenate([zero, jnp.sin(ang), jnp.sin(ang), zero[:, :32]], axis=1)
    return cs, sn


def _block_diag(a, b):
    za = jnp.zeros_like(a)
    return jnp.concatenate([jnp.concatenate([a, za], axis=1), jnp.concatenate([za, b], axis=1)], axis=0)


def kernel(x, ln_mix_g, w_in, shift_mu_prev, shift_mu_next, decay_w0_fwd, decay_w2_fwd, decay_w0_bwd, decay_w2_bwd, iclr_a0_fwd, iclr_a2_fwd, iclr_a0_bwd, iclr_a2_bwd, gate_g2, k_k, k_a, r_k, ln_x_g, ln_x_b, q_norm_g, w_uq, kv_norm_g, w_ukv, mla_out_g, w_out, ln_ffn_g, w_ffn_up, ffn_conv_w, ffn_conv_b, w_ffn_down, ln_final_g, loss_target, m_ln_mix_g, m_w_in, m_shift_mu_prev, m_shift_mu_next, m_decay_w0_fwd, m_decay_w2_fwd, m_decay_w0_bwd, m_decay_w2_bwd, m_iclr_a0_fwd, m_iclr_a2_fwd, m_iclr_a0_bwd, m_iclr_a2_bwd, m_gate_g2, m_k_k, m_k_a, m_r_k, m_ln_x_g, m_ln_x_b, m_q_norm_g, m_w_uq, m_kv_norm_g, m_w_ukv, m_mla_out_g, m_w_out, m_ln_ffn_g, m_w_ffn_up, m_ffn_conv_w, m_ffn_conv_b, m_w_ffn_down, m_ln_final_g, v_ln_mix_g, v_w_in, v_shift_mu_prev, v_shift_mu_next, v_decay_w0_fwd, v_decay_w2_fwd, v_decay_w0_bwd, v_decay_w2_bwd, v_iclr_a0_fwd, v_iclr_a2_fwd, v_iclr_a0_bwd, v_iclr_a2_bwd, v_gate_g2, v_k_k, v_k_a, v_r_k, v_ln_x_g, v_ln_x_b, v_q_norm_g, v_w_uq, v_kv_norm_g, v_w_ukv, v_mla_out_g, v_w_out, v_ln_ffn_g, v_w_ffn_up, v_ffn_conv_w, v_ffn_conv_b, v_w_ffn_down, v_ln_final_g):
    arrs = dict(locals())
    for pre in ("", "m_", "v_"):
        arrs[pre + "w_in"] = jnp.swapaxes(arrs[pre + "w_in"], 1, 2)
    b, t, d = x.shape
    m = b * t
    x2 = x.reshape(m, d)
    tgt = loss_target.reshape(m, d)
    vec = lambda n: arrs[n].reshape(1, -1)

    core = lax.axis_index("c")
    chip = 2 * lax.axis_index("x") + lax.axis_index("y")
    def unpack(gathered, shard, entries):
        return _unpack_gathered(gathered.reshape(N_CHIPS, -1, 128), shard, chip, arrs, entries)

    shard_a, shard_b = _pack_shards_bf16(arrs, _BIG_A), _pack_shards_bf16(arrs, _BIG_B)
    fw = unpack(_allgather_weights(shard_a), shard_a, _BIG_A)
    win = fw["w_in"]
    zc = jnp.zeros((64, d), BF16)
    w_kr = win[2944:2976]
    rot_kr = jnp.swapaxes(_rot_cols(jnp.swapaxes(w_kr, 0, 1)), 0, 1)
    win_m = jnp.concatenate([win[1920:2944], zc, w_kr, zc[:32], zc, rot_kr, zc[:32]], axis=0)
    win_r = win[:RWKV_COLS]
    head = jnp.arange(512) // HEAD_DIM
    rw = dict(
        w0=jnp.concatenate([vec("decay_w0_fwd"), vec("decay_w0_bwd")], axis=1),
        w2=_block_diag(fw["decay_w2_fwd"], fw["decay_w2_bwd"]).astype(F32),
        a0=jnp.concatenate([vec("iclr_a0_fwd"), vec("iclr_a0_bwd")], axis=1),
        a2=_block_diag(fw["iclr_a2_fwd"], fw["iclr_a2_bwd"]).astype(F32),
        g2=fw["gate_g2"].astype(F32), k_k=vec("k_k"), k_a=vec("k_a"), r_k=vec("r_k"), ln_x_g=vec("ln_x_g"), ln_x_b=vec("ln_x_b"),
        ones_bd=(head[:, None] == head[None, :]).astype(F32),
    )
    cs, sn = _rope_tables(t)

    n1 = _rms_fwd(x2, vec("ln_mix_g"), "rms_mix")
    zm = _mm(n1, win_m, "nt", "proj_in_mla")
    zr = _mm(n1, win_r, "nt", "proj_in_rwkv")
    zs = _shift_fwd(zr.reshape(b, t, RWKV_COLS), vec("shift_mu_prev"), vec("shift_mu_next"))
    zs2 = zs.reshape(m, RWKV_COLS)
    wf, wb, kf, kb, kk, kaf, kab, gate = _prep_fwd(zs2, rw)
    r4 = lambda a: a.reshape(b, t, 512)
    f2 = lambda a: a.reshape(m, 512)
    kk4 = r4(kk)
    ops_f = (r4(wf), r4(kf), r4(kaf))
    ops_b = (r4(wb), r4(kb), r4(kab))
    y_f, hist_f, y_b, hist_b, s_last, gathered_b = _scan_fwd(zs, kk4, ops_f, ops_b, shard_b)
    fw.update(unpack(gathered_b, shard_b, _BIG_B))
    uq = fw["w_uq"].astype(F32).reshape(Q_RANK, HEADS, 96)
    z32 = jnp.zeros((Q_RANK, HEADS, 32), F32)
    wq = jnp.concatenate([uq[..., :64], uq[..., 64:], z32], axis=-1).reshape(Q_RANK, 1024)
    wqr = jnp.concatenate([z32, z32, _rot_cols(uq[..., 64:]), z32], axis=-1).reshape(Q_RANK, 1024)
    ukv = fw["w_ukv"].astype(F32).reshape(KV_RANK, HEADS, 128)
    wk = jnp.concatenate([ukv[..., :64], jnp.zeros_like(ukv[..., :64])], axis=-1).reshape(KV_RANK, 1024)
    wv = ukv[..., 64:].reshape(KV_RANK, 512)
    mp = dict(q_norm_g=vec("q_norm_g"), kv_norm_g=vec("kv_norm_g"), wq=wq, wqr=wqr, wk=wk, wv=wv)
    w_up_g, w_up_v = fw["w_ffn_up"][:, :D_FF], fw["w_ffn_up"][:, D_FF:]
    cw, cb = fw["ffn_conv_w"], vec("ffn_conv_b")
    y_f, y_b = f2(y_f), f2(y_b)
    y_rwkv = _post_fwd(y_f, y_b, zs2, kf, kb, gate, rw)
    q, kfull, v = _mla_fwd(zm, cs, sn, mp, t)
    o, lse = _attn_fwd(q, kfull, v, b, t)
    y_mla = _rms_fwd(o, vec("mla_out_g"), "rms_mla_out")
    ymix = jnp.concatenate([y_rwkv, y_mla], axis=1)
    h1 = _mm(ymix, fw["w_out"], "nn", "proj_out", add=x2)
    n2 = _rms_fwd(h1, vec("ln_ffn_g"), "rms_ffn")
    ug = _mm(n2, w_up_g, "nn", "ffn_up_gate")
    uv = _mm(n2, w_up_v, "nn", "ffn_up_val")
    r3f = lambda a: a.reshape(b, t, D_FF)
    act = _ffn_mid_fwd(r3f(ug), r3f(uv), cw, cb).reshape(m, D_FF)
    h2 = _mm(act, fw["w_ffn_down"], "nn", "ffn_down", add=h1)
    loss_tab, dh2, g_ln_final = _final(h2, vec("ln_final_g"), tgt)

    gfull = {}
    dact = _mm(dh2, fw["w_ffn_down"], "nt", "d_ffn_act")
    gfull["w_ffn_down"] = _mm(act, dh2, "tn", "g_ffn_down")
    dug, duv, tab_g, tab_v = _ffn_mid_bwd(r3f(ug), r3f(uv), cw, cb, r3f(dact))
    dug, duv = dug.reshape(m, D_FF), duv.reshape(m, D_FF)
    gfull["ffn_conv_w"] = jnp.concatenate([tab_g[0:3], tab_v[0:3]], axis=1)
    g_conv_b = jnp.concatenate([tab_g[3:4], tab_v[3:4]], axis=1)
    dn2 = _mm(duv, w_up_v, "nt", "d_ffn_in_val", add=_mm(dug, w_up_g, "nt", "d_ffn_in_gate"))
    shard_cols = arrs["w_ffn_up"].shape[2]
    gfull["w_ffn_up"] = jnp.concatenate([_mm(n2, dug, "tn", "g_ffn_up_gate", column_blocks=shard_cols),
                                         _mm(n2, duv, "tn", "g_ffn_up_val", column_blocks=shard_cols)], axis=0)
    dh1, g_ln_ffn = _rms_bwd(h1, vec("ln_ffn_g"), dn2, "rms_ffn_bwd", dres=dh2)
    dymix = _mm(dh1, fw["w_out"], "nt", "d_mix")
    gfull["w_out"] = _mm(ymix, dh1, "tn", "g_w_out")
    do, g_mla_out = _rms_bwd(o, vec("mla_out_g"), dymix, "rms_mla_out_bwd", dy_block=1)
    dq, dk, dv = _attn_bwd(q, kfull, v, o, lse, do, b, t)
    dzm, g_qn, g_kvn, g_wq, g_wqr, g_wk, g_wv = _mla_bwd(zm, cs, sn, mp, t, dq, dk, dv)
    gq3, gqr3 = g_wq.reshape(Q_RANK, HEADS, 128), g_wqr.reshape(Q_RANK, HEADS, 128)
    gfull["w_uq"] = jnp.concatenate(
        [gq3[..., :64], gq3[..., 64:96] + _rot_cols_t(gqr3[..., 64:96])], axis=-1
    ).reshape(Q_RANK, HEADS * 96)
    gfull["w_ukv"] = jnp.concatenate(
        [g_wk.reshape(KV_RANK, HEADS, 128)[..., :64], g_wv.reshape(KV_RANK, HEADS, 64)], axis=-1
    ).reshape(KV_RANK, 1024)
    def cores_first(entries, tag):
        packed = _pack_grads(gfull, arrs, entries)
        rh = packed.shape[1] // 2
        own = lax.dynamic_slice_in_dim(packed, core * rh, rh, axis=1)
        sib = _send_to_sibling(packed, True)
        return _add_rows([own.reshape(4 * rh, 128), sib.reshape(4 * rh, 128)], "sum_cores_" + tag, BF16).reshape(4, rh, 128)

    def join_halves(half, entries):
        other = _send_to_sibling(half, False)
        lower = jnp.where(core == 0, half, other)
        upper = jnp.where(core == 0, other, half)
        return _unpack_grads(jnp.concatenate([lower, upper], axis=0), arrs, entries)

    part_b = _pack_grads(gfull, arrs, _BIG_B, BF16)
    dys, dr_p, dk_p, dv_p, dgate, g_rk, g_lnx_g, g_lnx_b = _post_bwd(y_f, y_b, zs2, kf, kb, gate, rw, dymix)
    (dr_f, dwf, dkf, dkk_f, dkaf, dv_f, dr_b, dwb, dkb, dkk_b, dkab, dv_b, recv_b) = _scan_bwd(
        zs, kk4, r4(dys), ops_f, hist_f, ops_b, hist_b, s_last, part_b)
    rh_b = part_b.shape[1] // 2
    mine_b = lax.dynamic_slice(part_b, (chip, core * rh_b, 0), (1, rh_b, 128))[0]
    g_big = join_halves(_add_rows([mine_b] + [recv_b[k] for k in range(7)], "sum_devices_b"), _BIG_B)
    cts = dict(dwf=f2(dwf), dwb=f2(dwb), dkf=f2(dkf), dkb=f2(dkb), dkk_f=f2(dkk_f), dkk_b=f2(dkk_b), dkaf=f2(dkaf), dkab=f2(dkab),
               dr_f=f2(dr_f), dr_b=f2(dr_b), dr_p=dr_p, dk_p=dk_p, dv_p=dv_p, dg=dgate, dv_f=f2(dv_f), dv_b=f2(dv_b))
    dzs, g_w0, g_w2, g_a0, g_a2, g_g2, g_kk, g_ka = _prep_bwd(zs2, rw, cts)
    dzr, g_mu_p, g_mu_n = _shift_bwd(dzs.reshape(b, t, RWKV_COLS), zr.reshape(b, t, RWKV_COLS), vec("shift_mu_prev"), vec("shift_mu_next"))
    dzr = dzr.reshape(m, RWKV_COLS)
    gfull["decay_w2_fwd"], gfull["decay_w2_bwd"] = g_w2[:64, :512], g_w2[64:, 512:]
    gfull["iclr_a2_fwd"], gfull["iclr_a2_bwd"] = g_a2[:64, :512], g_a2[64:, 512:]
    gfull["gate_g2"] = g_g2
    dn1 = _mm(dzr, win_r, "nn", "d_proj_in_rwkv", add=_mm(dzm, win_m, "nn", "d_proj_in_mla"))
    g_m = _mm(dzm, n1, "tn", "g_w_in_mla")
    g_r = _mm(dzr, n1, "tn", "g_w_in_rwkv")
    g_kr = g_m[1088:1120] + jnp.swapaxes(_rot_cols_t(jnp.swapaxes(g_m[1216:1248], 0, 1)), 0, 1)
    gfull["w_in"] = jnp.concatenate([g_r, g_m[:1024], g_kr], axis=0)
    dx, g_ln_mix = _rms_bwd(x2, vec("ln_mix_g"), dn1, "rms_mix_bwd", dres=dh1)

    part_a = cores_first(_BIG_A, "a")
    recv_a = _scatter_to_chips(part_a)
    mine_a = lax.dynamic_index_in_dim(part_a, chip, axis=0, keepdims=False)
    g_big.update(join_halves(_add_rows([mine_a, recv_a[0], recv_a[1], recv_a[2]], "sum_chips_a"), _BIG_A))
    small = {
        "ln_mix_g": g_ln_mix, "shift_mu_prev": g_mu_p, "shift_mu_next": g_mu_n, "decay_w0_fwd": g_w0[:, :512],
        "decay_w0_bwd": g_w0[:, 512:], "iclr_a0_fwd": g_a0[:, :512], "iclr_a0_bwd": g_a0[:, 512:], "k_k": g_kk, "k_a": g_ka,
        "r_k": g_rk, "ln_x_g": g_lnx_g, "ln_x_b": g_lnx_b, "q_norm_g": g_qn, "kv_norm_g": g_kvn, "mla_out_g": g_mla_out,
        "ln_ffn_g": g_ln_ffn, "ffn_conv_b": g_conv_b, "ln_final_g": g_ln_final,
        "_loss": jnp.pad(loss_tab[0, 0:1], (0, 127)),
    }
    g_small_buf = _allreduce_small(_pack_small(small))
    g_small = _unpack_small(g_small_buf, arrs)

    grads, deltas, new_m, new_v = {}, {}, {}, {}
    for name, _, _ in _BIG:
        grads[name] = g_big[name]
        deltas[name], new_m[name], new_v[name] = _adamw(
            arrs[name], g_big[name], arrs["m_" + name], arrs["v_" + name], "adamw_" + name)
    pk = lambda pre: _pack_small({**{n: arrs[pre + n] for n in _SMALL}, "_loss": jnp.zeros((128,), F32)})
    sd, sm, sv = _adamw(pk(""), g_small_buf, pk("m_"), pk("v_"), "adamw_small")
    sd, sm, sv = _unpack_small(sd, arrs), _unpack_small(sm, arrs), _unpack_small(sv, arrs)
    for n in _SMALL:
        grads[n], deltas[n], new_m[n], new_v[n] = g_small[n], sd[n], sm[n], sv[n]
    for group in (grads, deltas, new_m, new_v):
        group["w_in"] = jnp.swapaxes(group["w_in"], 1, 2)

    return (g_small["_loss"], dx.reshape(b, t, d), *[grads[n] for n in _WEIGHTS], *[deltas[n] for n in _WEIGHTS],
            *[new_m[n] for n in _WEIGHTS], *[new_v[n] for n in _WEIGHTS])
```

```python
import functools
import math

import jax
import jax.numpy as jnp
from jax import lax
from jax.experimental import pallas as pl
from jax.experimental.pallas import tpu as pltpu

F32, BF16 = jnp.float32, jnp.bfloat16
MESH = pl.DeviceIdType.MESH
ANY = pl.BlockSpec(memory_space=pl.ANY)
VMEM = pl.BlockSpec(memory_space=pltpu.VMEM)
BS = pl.BlockSpec
SDS = jax.ShapeDtypeStruct

NORM_EPS = 1e-6
GN_EPS = 64e-5
L2_EPS = 1e-12
HEADS = 8
HEAD_DIM = 64
ROPE_DIM = 32
ROPE_THETA = 10000.0
MLA_SCALE = (64 + ROPE_DIM) ** -0.5
Q_RANK, KV_RANK = 768, 256
RWKV_COLS = 1920
MLA_PAD_COLS = Q_RANK + KV_RANK + 256
D_FF = 2816
ADAM_LR, ADAM_B1, ADAM_B2, ADAM_EPS, ADAM_WD, ADAM_STEP = 0.001, 0.9, 0.999, 1e-08, 0.01, 10

V7X_VMEM_LIMIT = 56 * 1024 * 1024
SCAN_CHUNK = 32
N_CHIPS = 4


def _cp(*sem):
    return pltpu.CompilerParams(dimension_semantics=sem, vmem_limit_bytes=V7X_VMEM_LIMIT)


def _tile(n, cands=(512, 640, 384, 256, 128)):
    for c in cands:
        if n % c == 0:
            return c
    return n


def _row_tile(n, cap=256):
    best = n
    for t in range(8, cap + 1, 8):
        if n % t == 0:
            best = t
    return best if best <= cap or n <= cap else n


def _rms(x, g):
    ms = jnp.mean(x * x, axis=-1, keepdims=True)
    return x * lax.rsqrt(ms + NORM_EPS) * g


@jax.custom_vjp
def _bdot(x, w):
    return jnp.dot(x.astype(BF16), w.astype(BF16), preferred_element_type=F32)


def _bdot_fwd(x, w):
    return _bdot(x, w), (x, w)


def _bdot_bwd(res, ct):
    x, w = res
    c = ct.astype(BF16)
    dx = lax.dot_general(c, w.astype(BF16), (((1,), (1,)), ((), ())), preferred_element_type=F32)
    dw = lax.dot_general(x.astype(BF16), c, (((0,), (0,)), ((), ())), preferred_element_type=F32)
    return dx.astype(x.dtype), dw.astype(w.dtype)


_bdot.defvjp(_bdot_fwd, _bdot_bwd)


@jax.custom_vjp
def _headsum(x, ones_bd):
    hi = x.astype(BF16)
    mid = (x - hi.astype(F32)).astype(BF16)
    ob = ones_bd.astype(BF16)
    return jnp.dot(hi, ob, preferred_element_type=F32) + jnp.dot(mid, ob, preferred_element_type=F32)


def _headsum_fwd(x, ones_bd):
    return _headsum(x, ones_bd), ones_bd


def _headsum_bwd(ones_bd, ct):
    return _headsum(ct, ones_bd), jnp.zeros_like(ones_bd)


_headsum.defvjp(_headsum_fwd, _headsum_bwd)


def _prep_fn(zs, w0, w2, a0, a2, g2, k_k, k_a, ones_bd):
    k = zs[:, 512:1024]
    wd = zs[:, 1536:1664]
    ad = zs[:, 1664:1792]
    gd = zs[:, 1792:1920]
    logit = w0 + _bdot(jnp.tanh(wd), w2)
    w = jnp.exp(-math.exp(-0.5) * jax.nn.sigmoid(logit))
    a = jax.nn.sigmoid(a0 + _bdot(ad, a2))
    g = _bdot(jax.nn.sigmoid(gd), g2)
    kkr = k * k_k
    nrm = jnp.sqrt(_headsum(kkr * kkr, ones_bd))
    kk = kkr / jnp.maximum(nrm, L2_EPS)
    a_f, a_b = a[:, :512], a[:, 512:]
    kf = k * (1.0 + (a_f - 1.0) * k_a)
    kb = k * (1.0 + (a_b - 1.0) * k_a)
    return w[:, :512], w[:, 512:], kf, kb, kk, kk * a_f, kk * a_b, g


def _post_fn(y, r, kf, kb, v, g, r_k, ln_g, ln_b, ones_bd):
    mu =_headsum(y, ones_bd) * (1.0 / HEAD_DIM)
    yc = y - mu
    var = _headsum(yc * yc, ones_bd) * (1.0 / HEAD_DIM)
    yn = yc * lax.rsqrt(var + GN_EPS) * ln_g + ln_b
    bonus = _headsum(r * (kf + kb) * r_k, ones_bd) * v
    return (yn + bonus) * g


def _cat8(x):
    return jnp.concatenate([x] * HEADS, axis=1)


def _mla_fn(zm, cs, sn, gq, gkv, wq, wqr, wk, wv):
    cq = zm[:, :Q_RANK]
    ckv = zm[:, Q_RANK:Q_RANK + KV_RANK]
    kr = zm[:, Q_RANK + KV_RANK:Q_RANK + KV_RANK + 128]
    krr = zm[:, Q_RANK + KV_RANK + 128:]
    cqn = _rms(cq, gq)
    ckvn = _rms(ckv, gkv)
    q = (_bdot(cqn, wq) * _cat8(cs) + _bdot(cqn, wqr) * _cat8(sn)) * MLA_SCALE
    kro = kr * cs + krr * sn
    kfull = _bdot(ckvn, wk) + _cat8(kro)
    v = _bdot(ckvn, wv)
    return q, kfull, v


def _adamw_math(w, g, m, v):
    m2 = ADAM_B1 * m + (1.0 - ADAM_B1) * g
    v2 = ADAM_B2 * v + (1.0 - ADAM_B2) * (g * g)
    m_hat = m2 / (1.0 - ADAM_B1 ** ADAM_STEP)
    v_hat = v2 / (1.0 - ADAM_B2 ** ADAM_STEP)
    delta = -ADAM_LR * (m_hat / (jnp.sqrt(v_hat) + ADAM_EPS) + ADAM_WD * w)
    return delta, m2, v2


_DIMS = {"nn": (((1,), (0,)), ((), ())), "nt": (((1,), (1,)), ((), ())), "tn": (((0,), (0,)), ((), ()))}


def _mm(a, b, mode, name, out_dtype=F32, add=None, column_blocks=None):
    if mode == "nn":
        (m, k), (_, n) = a.shape, b.shape
    elif mode == "nt":
        (m, k), (n, _) = a.shape, b.shape
    else:
        (k, m), (_, n) = a.shape, b.shape
    big = (1024, 1408, 768, 640, 512, 384, 256, 128)
    tm, tn, tk = _tile(m, big), column_blocks or _tile(n, big), _tile(k, (1024, 1408, 640, 512, 384, 256, 128))
    nk = k // tk

    def body(a_ref, b_ref, *rest):
        if add is None:
            o_ref, acc_ref = rest
        else:
            add_ref, o_ref, acc_ref = rest
        kk = pl.program_id(2)

        @pl.when(kk == 0)
        def _():
            acc_ref[...] = jnp.zeros_like(acc_ref)

        acc_ref[...] += lax.dot_general(
            a_ref[...].astype(BF16), b_ref[...].astype(BF16), _DIMS[mode], preferred_element_type=F32
        )

        @pl.when(kk == nk - 1)
        def _():
            r = acc_ref[...]
            if add is not None:
                r = r + add_ref[...]
            o_ref[...] = r.astype(out_dtype).reshape(o_ref.shape)

    a_spec = BS((tk, tm), lambda i, j, kk: (kk, i)) if mode == "tn" else BS((tm, tk), lambda i, j, kk: (i, kk))
    b_spec = BS((tn, tk), lambda i, j, kk: (j, kk)) if mode == "nt" else BS((tk, tn), lambda i, j, kk: (kk, j))
    o_spec = BS((tm, tn), lambda i, j, kk: (i, j))
    ins, specs = [a, b], [a_spec, b_spec]
    if add is not None:
        ins.append(add)
        specs.append(o_spec)
    out_shape = SDS((m, n), out_dtype)
    if column_blocks:
        assert add is None
        o_spec, out_shape = BS((1, tm, tn), lambda i, j, kk: (j, i, 0)), SDS((n // tn, m, tn), out_dtype)
    return pl.pallas_call(
        body, grid=(m // tm, n // tn, nk), in_specs=specs, out_specs=o_spec, out_shape=out_shape,
        scratch_shapes=[pltpu.VMEM((tm, tn), F32)], compiler_params=_cp("parallel", "parallel", "arbitrary"), name=name,
    )(*ins)


def _rms_fwd(x, g, name):
    m, d = x.shape
    tr = _tile(m)

    def body(x_ref, g_ref, o_ref):
        o_ref[...] = _rms(x_ref[...], g_ref[...]).astype(BF16)

    return pl.pallas_call(
        body, grid=(m // tr,), in_specs=[BS((tr, d), lambda i: (i, 0)), BS((1, d), lambda i: (0, 0))],
        out_specs=BS((tr, d), lambda i: (i, 0)), out_shape=SDS((m, d), BF16), compiler_params=_cp("parallel"), name=name,
    )(x, g)


def _rms_bwd(x, g, dy, name, dres=None, dy_block=0):
    m, d = x.shape
    tr = _row_tile(m, 512)

    def body(x_ref, g_ref, dy_ref, *rest):
        if dres is None:
            dx_ref, dg_ref = rest
        else:
            dres_ref, dx_ref, dg_ref = rest
        _, vjp = jax.vjp(_rms, x_ref[...], g_ref[...])
        dx, dg = vjp(dy_ref[...])
        if dres is not None:
            dx = dx + dres_ref[...]
        dx_ref[...] = dx

        @pl.when(pl.program_id(0) == 0)
        def _():
            dg_ref[...] = jnp.zeros_like(dg_ref)

        dg_ref[...] += dg

    row = BS((tr, d), lambda i: (i, 0))
    vec = BS((1, d), lambda i: (0, 0))
    ins, specs = [x, g, dy], [row, vec, BS((tr, d), lambda i: (i, dy_block))]
    if dres is not None:
        ins.append(dres)
        specs.append(row)
    return pl.pallas_call(
        body, grid=(m // tr,), in_specs=specs, out_specs=[row, vec], out_shape=[SDS((m, d), F32), SDS((1, d), F32)],
        compiler_params=_cp("arbitrary"), name=name,
    )(*ins)


def _final(h, g, tgt):
    m, d = h.shape
    tr = _row_tile(m, 512)

    def loss_fn(hh, gg, tt):
        e = _rms(hh, gg) - tt
        return 0.5 * jnp.sum(e * e) * (1.0 / d)

    def body(h_ref, g_ref, t_ref, l_ref, dh_ref, dg_ref):
        val, (dh, dg) = jax.value_and_grad(loss_fn, argnums=(0, 1))(h_ref[...], g_ref[...], t_ref[...])
        dh_ref[...] = dh

        @pl.when(pl.program_id(0) == 0)
        def _():
            dg_ref[...] = jnp.zeros_like(dg_ref)
            l_ref[...] = jnp.zeros_like(l_ref)

        dg_ref[...] += dg
        l_ref[...] += jnp.full(l_ref.shape, val, F32)

    row = BS((tr, d), lambda i: (i, 0))
    vec = BS((1, d), lambda i: (0, 0))
    return pl.pallas_call(
        body, grid=(m // tr,), in_specs=[row, vec, row], out_specs=[BS((8, 128), lambda i: (0, 0)), row, vec],
        out_shape=[SDS((8, 128), F32), SDS((m, d), F32), SDS((1, d), F32)], compiler_params=_cp("arbitrary"), name="final_loss",
    )(h, g, tgt)


def _prev_next(z, t):
    row = lax.broadcasted_iota(jnp.int32, z.shape, 0)
    zp = jnp.where(row == 0, 0.0, pltpu.roll(z, 1, axis=0))
    zn = jnp.where(row == t - 1, 0.0, pltpu.roll(z, t - 1, axis=0))
    return zp, zn


def _shift_fwd(z3, mu_p, mu_n):
    b, t, c = z3.shape
    nc = c // 128

    def body(z_ref, mp_ref, mn_ref, o_ref):
        z = z_ref[0]
        zp, zn = _prev_next(z, t)
        o_ref[0] = z + mp_ref[...] * (zp - z) + mn_ref[...] * (zn - z)

    blk = BS((1, t, 128), lambda i, j: (i, 0, j))
    vec = BS((1, 128), lambda i, j: (0, j))
    return pl.pallas_call(
        body, grid=(b, nc), in_specs=[blk, vec, vec], out_specs=blk, out_shape=SDS((b, t, c), F32),
        compiler_params=_cp("parallel", "parallel"), name="shift_fwd",
    )(z3, mu_p, mu_n)


def _shift_bwd(dzs3, z3, mu_p, mu_n):
    b, t, c = z3.shape
    nc = c // 128

    def body(d_ref, z_ref, mp_ref, mn_ref, dz_ref, dmp_ref, dmn_ref):
        d, z = d_ref[0], z_ref[0]
        mp, mn = mp_ref[...], mn_ref[...]
        zp, zn = _prev_next(z, t)
        _, dp_next = _prev_next(d * mp, t)
        dn_prev, _ = _prev_next(d * mn, t)
        dz_ref[0] = (d * (1.0 - mp - mn) + dp_next + dn_prev).astype(BF16)

        @pl.when(pl.program_id(1) == 0)
        def _():
            dmp_ref[...] = jnp.zeros_like(dmp_ref)
            dmn_ref[...] = jnp.zeros_like(dmn_ref)

        dmp_ref[...] += jnp.sum(d * (zp - z), axis=0, keepdims=True)
        dmn_ref[...] += jnp.sum(d * (zn - z), axis=0, keepdims=True)

    blk = BS((1, t, 128), lambda j, i: (i, 0, j))
    vec = BS((1, 128), lambda j, i: (0, j))
    return pl.pallas_call(
        body, grid=(nc, b), in_specs=[blk, blk, vec, vec], out_specs=[blk, vec, vec],
        out_shape=[SDS((b, t, c), BF16), SDS((1, c), F32), SDS((1, c), F32)],
        compiler_params=_cp("parallel", "arbitrary"), name="shift_bwd",
    )(dzs3, z3, mu_p, mu_n)


def _const(shape):
    nd = len(shape)
    return BS(shape, lambda i: (0,) * nd)


def _prep_fwd(zs, p):
    m = zs.shape[0]
    tr = 512
    params = [p["w0"], p["w2"], p["a0"], p["a2"], p["g2"], p["k_k"], p["k_a"], p["ones_bd"]]

    def body(zs_ref, w0, w2, a0, a2, g2, kk_, ka_, bd, wf, wb, kf, kb, kk, kaf, kab, g):
        outs = _prep_fn(zs_ref[...], w0[...], w2[...], a0[...], a2[...], g2[...], kk_[...], ka_[...], bd[...])
        for ref, val in zip((wf, wb, kf, kb, kk, kaf, kab, g), outs):
            ref[...] = val

    row = BS((tr, 512), lambda i: (i, 0))
    return pl.pallas_call(
        body, grid=(m // tr,), in_specs=[BS((tr, RWKV_COLS), lambda i: (i, 0))] + [_const(q.shape) for q in params],
        out_specs=[row] * 8, out_shape=[SDS((m, 512), F32)] * 8, compiler_params=_cp("parallel"), name="rwkv_prep_fwd",
    )(zs, *params)


def _prep_bwd(zs, p, ct_rows):
    m = zs.shape[0]
    tr = 256
    params = [p["w0"], p["w2"], p["a0"], p["a2"], p["g2"], p["k_k"], p["k_a"]]
    names = ["dwf", "dwb", "dkf", "dkb", "dkk_f", "dkk_b", "dkaf", "dkab", "dr_f", "dr_b", "dr_p", "dk_p", "dv_p", "dg",
             "dv_f", "dv_b"]
    rows = [ct_rows[n] for n in names]

    def body(zs_ref, w0, w2, a0, a2, g2, kk_, ka_, bd, *rest):
        c = {n: r[...] for n, r in zip(names, rest[:len(names)])}
        outs = rest[len(names):]
        dzs_ref, grads = outs[0], outs[1:]
        ones_bd = bd[...]
        _, vjp = jax.vjp(
            lambda *q: _prep_fn(*q, ones_bd), zs_ref[...], w0[...], w2[...], a0[...], a2[...], g2[...], kk_[...], ka_[...]
        )
        cts = (c["dwf"], c["dwb"], c["dkf"] + c["dk_p"], c["dkb"] + c["dk_p"], c["dkk_f"] + c["dkk_b"], c["dkaf"], c["dkab"], c["dg"])
        dzs, *dparams = vjp(cts)
        dr = c["dr_f"] + c["dr_b"] + c["dr_p"]
        dv = c["dv_f"] + c["dv_b"] + c["dv_p"]
        dzs_ref[:, 0:512] = dzs[:, 0:512] + dr
        dzs_ref[:, 512:1024] = dzs[:, 512:1024]
        dzs_ref[:, 1024:1536] = dzs[:, 1024:1536] + dv
        dzs_ref[:, 1536:1920] = dzs[:, 1536:1920]

        @pl.when(pl.program_id(0) == 0)
        def _():
            for gr in grads:
                gr[...] = jnp.zeros_like(gr)

        for gr, val in zip(grads, dparams):
            gr[...] += val

    row = BS((tr, 512), lambda i: (i, 0))
    return pl.pallas_call(
        body, grid=(m // tr,),
        in_specs=[BS((tr, RWKV_COLS), lambda i: (i, 0))] + [_const(q.shape) for q in params] + [_const(p["ones_bd"].shape)]
        + [row] * len(names),
        out_specs=[BS((tr, RWKV_COLS), lambda i: (i, 0))] + [_const(q.shape) for q in params],
        out_shape=[SDS((m, RWKV_COLS), F32)] + [SDS(q.shape, F32) for q in params],
        compiler_params=_cp("arbitrary"), name="rwkv_prep_bwd",
    )(zs, *params, p["ones_bd"], *rows)


def _post_specs(tr):
    r = BS((tr, 512), lambda i: (i, 0))
    v = BS((tr, 512), lambda i: (i, 2))
    row = BS((tr, 512), lambda i: (i, 0))
    return r, v, row


def _post_fwd(y_f, y_b, zs, kf, kb, g, p):
    m = zs.shape[0]
    tr = 512
    r, v, row = _post_specs(tr)
    vecs = [p["r_k"], p["ln_x_g"], p["ln_x_b"], p["ones_bd"]]

    def body(yf, yb, r_ref, v_ref, kf_ref, kb_ref, g_ref, rk, lg, lb, bd, o_ref):
        o_ref[...] = _post_fn(
            yf[...] + yb[...], r_ref[...], kf_ref[...], kb_ref[...], v_ref[...], g_ref[...], rk[...], lg[...], lb[...], bd[...]
        ).astype(BF16)

    return pl.pallas_call(
        body, grid=(m // tr,), in_specs=[row, row, r, v, row, row, row] + [_const(q.shape) for q in vecs],
        out_specs=row, out_shape=SDS((m, 512), BF16), compiler_params=_cp("parallel"), name="rwkv_post_fwd",
    )(y_f, y_b, zs, zs, kf, kb, g, *vecs)


def _post_bwd(y_f, y_b, zs, kf, kb, g, p, dymix):
    m = zs.shape[0]
    tr = 256
    r, v, row = _post_specs(tr)
    vecs = [p["r_k"], p["ln_x_g"], p["ln_x_b"]]

    def body(yf, yb, r_ref, v_ref, kf_ref, kb_ref, g_ref, rk, lg, lb, bd, dy_ref, dyo, dr, dk, dv, dg, drk, dlg, dlb):
        ones_bd = bd[...]
        _, vjp = jax.vjp(
            lambda *q: _post_fn(*q, ones_bd),
            yf[...] + yb[...], r_ref[...], kf_ref[...], kb_ref[...], v_ref[...], g_ref[...], rk[...], lg[...], lb[...],
        )
        c_y, c_r, c_kf, _, c_v, c_g, c_rk, c_lg, c_lb = vjp(dy_ref[...])
        dyo[...] = c_y
        dr[...] = c_r
        dk[...] = c_kf
        dv[...] = c_v
        dg[...] = c_g

        @pl.when(pl.program_id(0) == 0)
        def _():
            for ref in (drk, dlg, dlb):
                ref[...] = jnp.zeros_like(ref)

        drk[...] += c_rk
        dlg[...] += c_lg
        dlb[...] += c_lb

    vec = _const((1, 512))
    return pl.pallas_call(
        body, grid=(m // tr,),
        in_specs=[row, row, r, v, row, row, row] + [_const(q.shape) for q in vecs] + [_const(p["ones_bd"].shape), row],
        out_specs=[row, row, row, row, row, vec, vec, vec],
        out_shape=[SDS((m, 512), F32)] * 5 + [SDS((1, 512), F32)] * 3,
        compiler_params=_cp("arbitrary"), name="rwkv_post_bwd",
    )(y_f, y_b, zs, zs, kf, kb, g, *vecs, p["ones_bd"], dymix)


SCAN_MXU_GROUPS = 2


def _half_ones():
    ri = lax.broadcasted_iota(jnp.int32, (128, 128), 0)
    ci = lax.broadcasted_iota(jnp.int32, (128, 128), 1)
    return jnp.where((ri < 64) == (ci < 64), 1.0, 0.0).astype(BF16)


def _half_sums(xs, ones):
    out = []
    per = -(-len(xs) // SCAN_MXU_GROUPS)
    for g in range(0, len(xs), per):
        part = xs[g:g + per]
        res = jnp.dot(jnp.concatenate(part, axis=0).astype(BF16), ones, preferred_element_type=F32)
        out += [res[64 * i:64 * i + 64] for i in range(len(part))]
    return out


def _scan_specs(b, t):
    nc = t // SCAN_CHUNK
    up, down = (lambda c: c), (lambda c: nc - 1 - c)
    rows = [BS((b, SCAN_CHUNK, 512), lambda c, ci=ci: (0, ci(c), 0)) for ci in (up, down)]
    vrows = [BS((b, SCAN_CHUNK, 512), lambda c, ci=ci: (0, ci(c), 2)) for ci in (up, down)]
    hist = [BS((SCAN_CHUNK, b * 4, 64, 128), lambda c, ci=ci: (ci(c), 0, 0, 0)) for ci in (up, down)]
    return nc, rows, vrows, hist


class _Window:
    def __init__(self, g, ascending):
        self.bases = [pl.multiple_of(g * 8, 8) if asc else pl.multiple_of(SCAN_CHUNK - 8 - g * 8, 8) for asc in ascending]
        self.ascending = ascending
        self.blocks = {}
        self.row_id = lax.broadcasted_iota(jnp.int32, (8, 128), 0)

    def j(self, d, s):
        return s if self.ascending[d] else 7 - s

    def time(self, d, s):
        return self.bases[d] + self.j(d, s)

    def row(self, ref, d, bi, cols, s):
        key = (id(ref), d, bi, cols.start)
        if key not in self.blocks:
            self.blocks[key] = ref[bi, pl.ds(self.bases[d], 8), cols]
        jj = self.j(d, s)
        return self.blocks[key][jj:jj + 1, :]

    def put(self, buf, key, d, s, row):
        prev = buf.get(key)
        new = jnp.broadcast_to(row, (8, 128))
        buf[key] = new if prev is None else jnp.where(self.row_id == self.j(d, s), new, prev)

    def flush(self, buf, refs_of):
        for key, val in buf.items():
            ref, d, bi, cols = refs_of(key)
            ref[bi, pl.ds(self.bases[d], 8), cols] = val


def _pairs(b):
    return [(bi * 4 + p, bi, slice(128 * p, 128 * p + 128)) for bi in range(b) for p in range(4)]


def _colsum(x):
    return jnp.sum(x, axis=0, keepdims=True)


def _pair_matvec(row, mat):
    rid = lax.broadcasted_iota(jnp.int32, (8, 64), 0)
    lhs = jnp.where(rid == 0, row[:, :64], jnp.where(rid == 1, row[:, 64:], 0.0))
    out = jnp.dot(lhs.astype(BF16), mat.astype(BF16), preferred_element_type=F32)
    lo = lax.broadcasted_iota(jnp.int32, (1, 128), 1) < 64
    return jnp.where(lo, out[0:1], out[1:2])


def _eye_mask():
    return (lax.broadcasted_iota(jnp.int32, (64, 128), 1) & 63) == lax.broadcasted_iota(jnp.int32, (64, 128), 0)


def _scan_fwd(zs, kk, ops_f, ops_b, shard):
    b, t = zs.shape[:2]
    nc, rows, vrows, hist = _scan_specs(b, t)
    npair = b * 4

    def body(*refs):
        ins, shard_ref, outs, s_ref = refs[:12], refs[12], refs[13:17], refs[17]
        gather = (shard_ref, *refs[18:21])
        dirs = [dict(zip(("r", "kk", "v", "w", "k", "ka", "y", "h"), (*ins[6 * d:6 * d + 6], *outs[2 * d:2 * d + 2])))
                for d in (0, 1)]

        @pl.when(pl.program_id(0) == 0)
        def _():
            s_ref[...] = jnp.zeros_like(s_ref)
            _gather_halves(*gather, "start")

        @pl.when(pl.program_id(0) == nc - 1)
        def _():
            _gather_halves(*gather, "finish")

        ones, eye = _half_ones(), _eye_mask()
        chains = [(d, pr, bi, cols) for d in (0, 1) for pr, bi, cols in _pairs(b)]

        def eight_steps(g, carry):
            win = _Window(g, (True, False))
            ybuf = {}
            for s in range(8):
                s_prev, xa = [], []
                for d, pr, bi, cols in chains:
                    q = dirs[d]
                    st = s_ref[d * npair + pr]
                    q["h"][win.time(d, s), pr] = st
                    s_prev.append(st)
                    xa += [st * win.row(q["kk"], d, bi, cols, s), jnp.where(eye, win.row(q["v"], d, bi, cols, s), 0.0)]
                ra = _half_sums(xa, ones)
                xb = []
                for i, (d, pr, bi, cols) in enumerate(chains):
                    q = dirs[d]
                    s_new = s_prev[i] * win.row(q["w"], d, bi, cols, s) - ra[2 * i] * win.row(q["ka"], d, bi, cols, s) \
                        + ra[2 * i + 1] * win.row(q["k"], d, bi, cols, s)
                    s_ref[d * npair + pr] = s_new
                    xb.append(s_new * win.row(q["r"], d, bi, cols, s))
                rb = _half_sums(xb, ones)
                for i, (d, pr, bi, cols) in enumerate(chains):
                    win.put(ybuf, i, d, s, _colsum(jnp.where(eye, rb[i], 0.0)))
            win.flush(ybuf, lambda i: (dirs[chains[i][0]]["y"], chains[i][0], chains[i][2], chains[i][3]))
            return carry

        def sixteen_steps(g2, carry):
            return eight_steps(2 * g2 + 1, eight_steps(2 * g2, carry))

        lax.fori_loop(0, SCAN_CHUNK // 16, sixteen_steps, 0)

    row_shape, hist_shape = SDS((b, t, 512), F32), SDS((t, npair, 64, 128), F32)
    state = (2 * npair, 64, 128)
    return pl.pallas_call(
        body, grid=(nc,), in_specs=sum(([rows[d], rows[d], vrows[d]] + [rows[d]] * 3 for d in (0, 1)), []) + [ANY],
        out_specs=[rows[0], hist[0], rows[1], hist[1], BS(state, lambda c: (0, 0, 0)), ANY],
        out_shape=[row_shape, hist_shape, row_shape, hist_shape, SDS(state, F32), _gathered_shape(shard)],
        scratch_shapes=_gather_sems(), compiler_params=_cp("arbitrary"), name="wkv_scan",
    )(zs, kk, zs, *ops_f, zs, kk, zs, *ops_b, shard)


def _scan_bwd(zs, kk, dy, ops_f, hist_f, ops_b, hist_b, s_last, partials):
    b, t = zs.shape[:2]
    nc, rows, vrows, hist = _scan_specs(b, t)
    npair = b * 4
    names_in = ("r", "kk", "v", "dy", "w", "k", "ka", "h")
    names_out = ("dr", "dw", "dk", "dkk", "dka", "dv")

    def body(*refs):
        ins, last_ref, part_ref, outs, recv_ref = refs[:16], refs[16], refs[17], refs[18:30], refs[30]
        ds_ref, after_ref = refs[31], refs[32]
        scatter = (part_ref, recv_ref, refs[33], refs[34])
        dirs = [dict(zip(names_in + names_out, (*ins[8 * d:8 * d + 8], *outs[6 * d:6 * d + 6]))) for d in (0, 1)]

        @pl.when(pl.program_id(0) == 0)
        def _():
            ds_ref[...] = jnp.zeros_like(ds_ref)
            after_ref[...] = last_ref[...]
            _scatter_to_all(*scatter, "start")

        @pl.when(pl.program_id(0) == nc - 1)
        def _():
            _scatter_to_all(*scatter, "finish")

        ones, eye = _half_ones(), _eye_mask()
        chains = [(d, pr, bi, cols) for d in (0, 1) for pr, bi, cols in _pairs(b)]

        def eight_steps(g, carry):
            win = _Window(g, (False, True))
            obuf = {}
            s_after = [after_ref[d * npair + pr] for d, pr, _, _ in chains]
            for s in range(8):
                row = lambda name, d, bi, cols: win.row(dirs[d][name], d, bi, cols, s)
                s_prev, xa = [], []
                for d, pr, bi, cols in chains:
                    st = dirs[d]["h"][win.time(d, s), pr]
                    s_prev.append(st)
                    xa += [st * row("kk", d, bi, cols), jnp.where(eye, row("dy", d, bi, cols), 0.0)]
                ra = _half_sums(xa, ones)
                ds_now, xb = [], []
                for i, (d, pr, bi, cols) in enumerate(chains):
                    skk, dycol = ra[2 * i], ra[2 * i + 1]
                    ds = ds_ref[d * npair + pr] + dycol * row("r", d, bi, cols)
                    win.put(obuf, (i, "dr"), d, s, _pair_matvec(row("dy", d, bi, cols), s_after[i]))
                    win.put(obuf, (i, "dk"), d, s, _pair_matvec(row("v", d, bi, cols), ds))
                    win.put(obuf, (i, "dka"), d, s, -_colsum(ds * skk))
                    win.put(obuf, (i, "dw"), d, s, _colsum(ds * s_prev[i]))
                    ds_now.append(ds)
                    xb += [ds * row("k", d, bi, cols), ds * row("ka", d, bi, cols)]
                rb = _half_sums(xb, ones)
                for i, (d, pr, bi, cols) in enumerate(chains):
                    dskk_neg = rb[2 * i + 1]
                    win.put(obuf, (i, "dv"), d, s, _colsum(jnp.where(eye, rb[2 * i], 0.0)))
                    win.put(obuf, (i, "dkk"), d, s, -_colsum(s_prev[i] * dskk_neg))
                    ds_ref[d * npair + pr] = ds_now[i] * row("w", d, bi, cols) - dskk_neg * row("kk", d, bi, cols)
                s_after = s_prev
            for i, (d, pr, _, _) in enumerate(chains):
                after_ref[d * npair + pr] = s_after[i]
            win.flush(obuf, lambda key: (dirs[chains[key[0]][0]][key[1]], chains[key[0]][0], chains[key[0]][2], chains[key[0]][3]))
            return carry

        def sixteen_steps(g2, carry):
            return eight_steps(2 * g2 + 1, eight_steps(2 * g2, carry))

        lax.fori_loop(0, SCAN_CHUNK // 16, sixteen_steps, 0)

    row_shape = SDS((b, t, 512), F32)
    state = (2 * npair, 64, 128)
    return pl.pallas_call(
        body, grid=(nc,),
        in_specs=sum(([rows[d], rows[d], vrows[d]] + [rows[d]] * 4 + [hist[d]] for d in (1, 0)), [])
        + [BS(state, lambda c: (0, 0, 0)), ANY],
        out_specs=[rows[1]] * 6 + [rows[0]] * 6 + [ANY],
        out_shape=[row_shape] * 12 + [SDS((7, partials.shape[1] // 2, 128), partials.dtype)],
        scratch_shapes=[pltpu.VMEM(state, F32), pltpu.VMEM(state, F32), pltpu.SemaphoreType.DMA((7,)),
                        pltpu.SemaphoreType.DMA((7,))],
        compiler_params=_cp("arbitrary"), name="wkv_scan_bwd",
    )(zs, kk, zs, dy, *ops_f, hist_f, zs, kk, zs, dy, *ops_b, hist_b, s_last, partials)


def _mla_fwd(zm, cs, sn, p, t):
    m = zm.shape[0]
    tr = 512
    per = t // tr
    params = [p["q_norm_g"], p["kv_norm_g"], p["wq"], p["wqr"], p["wk"], p["wv"]]

    def body(z_ref, cs_ref, sn_ref, gq, gkv, wq, wqr, wk, wv, q_ref, k_ref, v_ref):
        q, kf, v = _mla_fn(z_ref[...], cs_ref[...], sn_ref[...], gq[...], gkv[...], wq[...], wqr[...], wk[...], wv[...])
        q_ref[...] = q.astype(BF16)
        k_ref[...] = kf.astype(BF16)
        v_ref[...] = v.astype(BF16)

    tab = BS((tr, 128), lambda i: (i % per, 0))
    return pl.pallas_call(
        body, grid=(m // tr,), in_specs=[BS((tr, MLA_PAD_COLS), lambda i: (i, 0)), tab, tab] + [_const(q.shape) for q in params],
        out_specs=[BS((tr, 1024), lambda i: (i, 0)), BS((tr, 1024), lambda i: (i, 0)), BS((tr, 512), lambda i: (i, 0))],
        out_shape=[SDS((m, 1024), BF16), SDS((m, 1024), BF16), SDS((m, 512), BF16)], compiler_params=_cp("parallel"), name="mla_prep_fwd",
    )(zm, cs, sn, *params)


def _mla_bwd(zm, cs, sn, p, t, dq, dk, dv):
    m = zm.shape[0]
    tr = 256
    per = t // tr
    params = [p["q_norm_g"], p["kv_norm_g"], p["wq"], p["wqr"], p["wk"], p["wv"]]

    def body(z_ref, cs_ref, sn_ref, gq, gkv, wq, wqr, wk, wv, dq_ref, dk_ref, dv_ref, dz_ref, *grads):
        cs_v, sn_v = cs_ref[...], sn_ref[...]
        _, vjp = jax.vjp(
            lambda *q: _mla_fn(q[0], cs_v, sn_v, *q[1:]), z_ref[...], gq[...], gkv[...], wq[...], wqr[...], wk[...], wv[...]
        )
        dz, *dparams = vjp((dq_ref[...], dk_ref[...], dv_ref[...]))
        dz_ref[...] = dz.astype(BF16)

        @pl.when(pl.program_id(0) == 0)
        def _():
            for gr in grads:
                gr[...] = jnp.zeros_like(gr)

        for gr, val in zip(grads, dparams):
            gr[...] += val

    tab = BS((tr, 128), lambda i: (i % per, 0))
    wide = BS((tr, 1024), lambda i: (i, 0))
    return pl.pallas_call(
        body, grid=(m // tr,),
        in_specs=[BS((tr, MLA_PAD_COLS), lambda i: (i, 0)), tab, tab] + [_const(q.shape) for q in params]
        + [wide, wide, BS((tr, 512), lambda i: (i, 0))],
        out_specs=[BS((tr, MLA_PAD_COLS), lambda i: (i, 0))] + [_const(q.shape) for q in params],
        out_shape=[SDS((m, MLA_PAD_COLS), BF16)] + [SDS(q.shape, F32) for q in params],
        compiler_params=_cp("arbitrary"), name="mla_prep_bwd",
    )(zm, cs, sn, *params, dq, dk, dv)


_NT = (((1,), (1,)), ((), ()))
_TN = (((0,), (0,)), ((), ()))


def _attn_fwd(q, kf, v, b, t):
    m = q.shape[0]
    tq = 256
    nq = t // tq

    def body(q_ref, k_ref, v_ref, o_ref, l_ref):
        lo = lax.broadcasted_iota(jnp.int32, (1, 128), 1) < 64
        v_all = v_ref[...]
        o = jnp.zeros((tq, 128), F32)
        lse = []
        for h in range(2):
            hs = slice(128 * h, 128 * h + 128)
            s = lax.dot_general(q_ref[:, hs], k_ref[:, hs], _NT, preferred_element_type=F32)
            mx = jnp.max(s, axis=1, keepdims=True)
            e = jnp.exp(s - mx)
            den = jnp.sum(e, axis=1, keepdims=True)
            vh = jnp.where(lo if h == 0 else jnp.logical_not(lo), v_all, jnp.zeros_like(v_all))
            o = o + jnp.dot(e.astype(BF16), vh, preferred_element_type=F32) / den
            lse.append(mx + jnp.log(den))
        o_ref[...] = o
        l_ref[...] = jnp.where(lo, lse[0], lse[1])

    return pl.pallas_call(
        body, grid=(b, 4, nq),
        in_specs=[BS((tq, 256), lambda bi, hp, i: (bi * nq + i, hp)), BS((t, 256), lambda bi, hp, i: (bi, hp)),
                  BS((t, 128), lambda bi, hp, i: (bi, hp))],
        out_specs=[BS((tq, 128), lambda bi, hp, i: (bi * nq + i, hp))] * 2,
        out_shape=[SDS((m, 512), F32), SDS((m, 512), F32)], compiler_params=_cp("parallel", "parallel", "arbitrary"), name="attn_fwd",
    )(q, kf, v)


def _attn_bwd(q, kf, v, o, lse, do, b, t):
    m = q.shape[0]
    tq = 256
    nq = t // tq

    def body(q_ref, k_ref, v_ref, o_ref, l_ref, do_ref, dq_ref, dk_ref, dv_ref):
        lo = lax.broadcasted_iota(jnp.int32, (1, 128), 1) < 64

        @pl.when(pl.program_id(2) == 0)
        def _():
            dk_ref[...] = jnp.zeros_like(dk_ref)
            dv_ref[...] = jnp.zeros_like(dv_ref)

        v_all, o_all, l_all, do_all = v_ref[...], o_ref[...], l_ref[...], do_ref[...]
        dv_acc = jnp.zeros((t, 128), F32)
        for h in range(2):
            hs = slice(128 * h, 128 * h + 128)
            mask = lo if h == 0 else jnp.logical_not(lo)
            qh, kh = q_ref[:, hs], k_ref[:, hs]
            s = lax.dot_general(qh, kh, _NT, preferred_element_type=F32)
            lse_h = jnp.max(jnp.where(mask, l_all, -jnp.inf), axis=1, keepdims=True)
            pr = jnp.exp(s - lse_h)
            do_h = jnp.where(mask, do_all, 0.0)
            dp = lax.dot_general(do_h.astype(BF16), v_all, _NT, preferred_element_type=F32)
            dsum = jnp.sum(do_h * o_all, axis=1, keepdims=True)
            ds = (pr * (dp - dsum)).astype(BF16)
            dq_ref[:, hs] = jnp.dot(ds, kh, preferred_element_type=F32)
            dk_ref[:, hs] += lax.dot_general(ds, qh, _TN, preferred_element_type=F32)
            dv_acc = dv_acc + lax.dot_general(pr.astype(BF16), do_h.astype(BF16), _TN, preferred_element_type=F32)
        dv_ref[...] += dv_acc

    qspec = BS((tq, 256), lambda bi, hp, i: (bi * nq + i, hp))
    kspec = BS((t, 256), lambda bi, hp, i: (bi, hp))
    vspec = BS((t, 128), lambda bi, hp, i: (bi, hp))
    ospec = BS((tq, 128), lambda bi, hp, i: (bi * nq + i, hp))
    return pl.pallas_call(
        body, grid=(b, 4, nq), in_specs=[qspec, kspec, vspec, ospec, ospec, ospec], out_specs=[qspec, kspec, vspec],
        out_shape=[SDS((m, 1024), F32), SDS((m, 1024), F32), SDS((m, 512), F32)],
        compiler_params=_cp("parallel", "parallel", "arbitrary"), name="attn_bwd",
    )(q, kf, v, o, lse, do)


def _conv3(u, w_ref, b_ref, t):
    up, un = _prev_next(u, t)
    return w_ref[0:1, :] * up + w_ref[1:2, :] * u + w_ref[2:3, :] * un + b_ref[...], up, un


def _ffn_mid_fwd(ug3, uv3, cw, cb):
    b, t, f = ug3.shape
    nc = f // 256

    def body(ug_ref, uv_ref, wg_ref, wv_ref, bg_ref, bv_ref, a_ref):
        gc, _, _ = _conv3(ug_ref[0], wg_ref, bg_ref, t)
        vc, _, _ = _conv3(uv_ref[0], wv_ref, bv_ref, t)
        a_ref[0] = (gc * jax.nn.sigmoid(gc) * vc).astype(BF16)

    blk = BS((1, t, 256), lambda i, j: (i, 0, j))
    return pl.pallas_call(
        body, grid=(b, nc),
        in_specs=[blk, blk, BS((3, 256), lambda i, j: (0, j)), BS((3, 256), lambda i, j: (0, j + nc)),
                  BS((1, 256), lambda i, j: (0, j)), BS((1, 256), lambda i, j: (0, j + nc))],
        out_specs=blk, out_shape=SDS((b, t, f), BF16), compiler_params=_cp("parallel", "parallel"), name="ffn_mid_fwd",
    )(ug3, uv3, cw, cw, cb, cb)


def _ffn_mid_bwd(ug3, uv3, cw, cb, da3):
    b, t, f = ug3.shape
    nc = f // 256

    def half(u, up, un, dc, w_ref):
        dprev, dnext = _prev_next(dc, t)
        du = w_ref[1:2, :] * dc + w_ref[0:1, :] * dnext + w_ref[2:3, :] * dprev
        sums = [jnp.sum(dc * q, axis=0, keepdims=True) for q in (up, u, un)] + [jnp.sum(dc, axis=0, keepdims=True)]
        row = lax.broadcasted_iota(jnp.int32, (8, 256), 0)
        tab = jnp.zeros((8, 256), F32)
        for i, s in enumerate(sums):
            tab = jnp.where(row == i, s, tab)
        return du, tab

    def body(ug_ref, uv_ref, wg_ref, wv_ref, bg_ref, bv_ref, da_ref, dug_ref, duv_ref, tg_ref, tv_ref):
        ug, uv, da = ug_ref[0], uv_ref[0], da_ref[0]
        gc, gp, gn = _conv3(ug, wg_ref, bg_ref, t)
        vc, vp, vn = _conv3(uv, wv_ref, bv_ref, t)
        sg = jax.nn.sigmoid(gc)
        d_gc = da * vc * (sg * (1.0 + gc * (1.0 - sg)))
        d_vc = da * (gc * sg)
        dug, tg = half(ug, gp, gn, d_gc, wg_ref)
        duv, tv = half(uv, vp, vn, d_vc, wv_ref)
        dug_ref[0] = dug.astype(BF16)
        duv_ref[0] = duv.astype(BF16)

        @pl.when(pl.program_id(1) == 0)
        def _():
            tg_ref[...] = jnp.zeros_like(tg_ref)
            tv_ref[...] = jnp.zeros_like(tv_ref)

        tg_ref[...] += tg
        tv_ref[...] += tv

    blk = BS((1, t, 256), lambda j, i: (i, 0, j))
    tab = BS((8, 256), lambda j, i: (0, j))
    return pl.pallas_call(
        body, grid=(nc, b),
        in_specs=[blk, blk, BS((3, 256), lambda j, i: (0, j)), BS((3, 256), lambda j, i: (0, j + nc)),
                  BS((1, 256), lambda j, i: (0, j)), BS((1, 256), lambda j, i: (0, j + nc)), blk],
        out_specs=[blk, blk, tab, tab],
        out_shape=[SDS((b, t, f), BF16), SDS((b, t, f), BF16), SDS((8, f), F32), SDS((8, f), F32)],
        compiler_params=_cp("parallel", "arbitrary"), name="ffn_mid_bwd",
    )(ug3, uv3, cw, cw, cb, cb, da3)


def _add_rows(parts, name, out_dtype=F32):
    r = parts[0].shape[0]
    tr = _row_tile(r, 1024)
    n = len(parts)

    def body(*refs):
        acc = refs[0][...].astype(F32)
        for q in refs[1:n]:
            acc = acc + q[...].astype(F32)
        refs[n][...] = acc.astype(out_dtype)

    row = BS((tr, 128), lambda i: (i, 0))
    return pl.pallas_call(
        body, grid=(r // tr,), in_specs=[row] * n, out_specs=row, out_shape=SDS((r, 128), out_dtype),
        compiler_params=_cp("parallel"), name=name,
    )(*parts)


def _adamw(w, g, m, v, name):
    lead = w.shape[:-2]
    r, c = w.shape[-2:]
    tr = _row_tile(r)

    def body(w_ref, g_ref, m_ref, v_ref, d_ref, m2_ref, v2_ref):
        d, m2, v2 = _adamw_math(w_ref[...], g_ref[...], m_ref[...], v_ref[...])
        d_ref[...] = d
        m2_ref[...] = m2
        v2_ref[...] = v2

    blk = BS((1,) * len(lead) + (tr, c), lambda i: (0,) * len(lead) + (i, 0))
    return pl.pallas_call(
        body, grid=(r // tr,), in_specs=[blk] * 4, out_specs=[blk] * 3, out_shape=[SDS(w.shape, F32)] * 3,
        compiler_params=_cp("parallel"), name=name,
    )(w, g, m, v)


def _place():
    return lax.axis_index("x"), lax.axis_index("y"), lax.axis_index("c")


def _flip(v, bit):
    return 1 - v if bit else v


def _allgather_weights(shard):
    def body(x_ref, out_ref, send_sems, recv_sems):
        _gather_halves(x_ref, out_ref, send_sems, recv_sems, "start")
        _gather_halves(x_ref, out_ref, send_sems, recv_sems, "finish")

    return pl.pallas_call(
        body, out_shape=_gathered_shape(shard), in_specs=[ANY], out_specs=ANY, scratch_shapes=_gather_sems(),
        name="allgather_weights",
    )(shard)


def _gathered_shape(shard):
    return SDS((8 * (shard.shape[0] // 2), 128), shard.dtype)


def _gather_sems():
    return [pltpu.SemaphoreType.DMA((6,)), pltpu.SemaphoreType.DMA((6,))]


def _gather_halves(x_ref, out_ref, send_sems, recv_sems, phase):
    rh = x_ref.shape[0] // 2
    x, y, c = _place()
    me, sibling = (x, y, c), (x, y, 1 - c)
    chips = [(1 - x, y), (x, 1 - y), (1 - x, 1 - y)]
    mine_src = x_ref.at[pl.ds(c * rh, rh), :]

    def rows(px, py, pc):
        return out_ref.at[pl.ds((4 * px + 2 * py + pc) * rh, rh), :]

    def copy(k, block, to, src=None):
        return pltpu.make_async_remote_copy(
            src_ref=rows(*block) if src is None else src, dst_ref=rows(*block), send_sem=send_sems.at[k],
            recv_sem=recv_sems.at[k], device_id=to, device_id_type=MESH,
        )

    first = [copy(j, me, (*chip, c), src=mine_src) for j, chip in enumerate(chips)]
    if phase == "start":
        for cp in first:
            cp.start()
        return
    passed = [copy(3 + j, (*chip, c), sibling) for j, chip in enumerate(chips)]
    for j, chip in enumerate(chips):
        copy(j, (*chip, c), me).wait_recv()
        passed[j].start()
    for j, chip in enumerate(chips):
        copy(3 + j, (*chip, 1 - c), me).wait_recv()
    for cp in first + passed:
        cp.wait_send()


def _scatter_partials(g_ref, recv_ref, send_sems, recv_sems, phase):
    x, y, c = _place()
    copies = []
    for j, (fx, fy) in enumerate(((1, 0), (0, 1), (1, 1))):
        px, py = _flip(x, fx), _flip(y, fy)
        copies.append(pltpu.make_async_remote_copy(
            src_ref=g_ref.at[2 * px + py], dst_ref=recv_ref.at[j], send_sem=send_sems.at[j], recv_sem=recv_sems.at[j],
            device_id=(px, py, c), device_id_type=MESH,
        ))
    if phase == "start":
        for cp in copies:
            cp.start()
        return
    for cp in copies:
        cp.wait_recv()
    for cp in copies:
        cp.wait_send()


def _scatter_to_all(g_ref, recv_ref, send_sems, recv_sems, phase):
    rh = g_ref.shape[1] // 2
    x, y, c = _place()
    copies = []
    for k in range(1, 8):
        px, py, pc = _flip(x, k >> 2 & 1), _flip(y, k >> 1 & 1), _flip(c, k & 1)
        copies.append(pltpu.make_async_remote_copy(
            src_ref=g_ref.at[2 * px + py, pl.ds(pc * rh, rh), :], dst_ref=recv_ref.at[k - 1], send_sem=send_sems.at[k - 1],
            recv_sem=recv_sems.at[k - 1], device_id=(px, py, pc), device_id_type=MESH,
        ))
    if phase == "start":
        for cp in copies:
            cp.start()
        return
    for cp in copies:
        cp.wait_recv()
    for cp in copies:
        cp.wait_send()


def _scatter_to_chips(g):
    def body(g_ref, recv_ref, send_sems, recv_sems):
        _scatter_partials(g_ref, recv_ref, send_sems, recv_sems, "start")
        _scatter_partials(g_ref, recv_ref, send_sems, recv_sems, "finish")

    return pl.pallas_call(
        body, out_shape=SDS((3,) + g.shape[1:], g.dtype), in_specs=[ANY], out_specs=ANY,
        scratch_shapes=[pltpu.SemaphoreType.DMA((3,)), pltpu.SemaphoreType.DMA((3,))], name="scatter_grads",
    )(g)


def _send_to_sibling(a, half_of_rows):
    rh = a.shape[1] // 2

    def body(a_ref, b_ref, send_sem, recv_sem):
        x, y, c = _place()
        src = a_ref.at[:, pl.ds((1 - c) * rh, rh), :] if half_of_rows else a_ref
        cp = pltpu.make_async_remote_copy(
            src_ref=src, dst_ref=b_ref, send_sem=send_sem, recv_sem=recv_sem, device_id=(x, y, 1 - c), device_id_type=MESH
        )
        cp.start()
        cp.wait()

    shape = (a.shape[0], rh, 128) if half_of_rows else a.shape
    return pl.pallas_call(
        body, out_shape=SDS(shape, a.dtype), in_specs=[ANY], out_specs=ANY,
        scratch_shapes=[pltpu.SemaphoreType.DMA, pltpu.SemaphoreType.DMA],
        name="sibling_halves" if half_of_rows else "sibling_swap",
    )(a)


def _allreduce_small(v):
    r = v.shape[0]

    def body(v_ref, out_ref, buf_ref, send_sems, recv_sems):
        x, y, c = _place()
        buf_ref[0] = v_ref[...]
        copies = []
        for k in range(1, 8):
            peer = (_flip(x, k >> 2 & 1), _flip(y, k >> 1 & 1), _flip(c, k & 1))
            cp = pltpu.make_async_remote_copy(
                src_ref=v_ref, dst_ref=buf_ref.at[k], send_sem=send_sems.at[k - 1], recv_sem=recv_sems.at[k - 1],
                device_id=peer, device_id_type=MESH,
            )
            cp.start()
            copies.append(cp)
        for cp in copies:
            cp.wait_recv()
        acc = None
        for d in range(8):
            slot = 4 * _flip(x, d >> 2 & 1) + 2 * _flip(y, d >> 1 & 1) + _flip(c, d & 1)
            term = buf_ref[slot]
            acc = term if acc is None else acc + term
        out_ref[...] = acc
        for cp in copies:
            cp.wait_send()

    return pl.pallas_call(
        body, out_shape=SDS(v.shape, F32), in_specs=[VMEM], out_specs=VMEM,
        scratch_shapes=[pltpu.VMEM((8, r, 128), F32), pltpu.SemaphoreType.DMA((7,)), pltpu.SemaphoreType.DMA((7,))],
        name="allreduce_small",
    )(v)


_BIG_A = (
    ("w_in", 0, False),
    ("decay_w2_fwd", 1, False), ("decay_w2_bwd", 1, False), ("iclr_a2_fwd", 1, False),
    ("iclr_a2_bwd", 1, False), ("gate_g2", 1, False),
)
_BIG_B = (
    ("w_uq", 0, False), ("w_ukv", 1, False), ("w_out", 0, False), ("w_ffn_up", 1, False), ("ffn_conv_w", 1, True),
    ("w_ffn_down", 0, False),
)
_BIG = _BIG_A + _BIG_B
_SMALL = (
    "ln_mix_g", "shift_mu_prev", "shift_mu_next", "decay_w0_fwd", "decay_w0_bwd", "iclr_a0_fwd", "iclr_a0_bwd", "k_k",
    "k_a", "r_k", "ln_x_g", "ln_x_b", "q_norm_g", "kv_norm_g", "mla_out_g", "ln_ffn_g", "ffn_conv_b", "ln_final_g",
)
_WEIGHTS = (
    "ln_mix_g", "w_in", "shift_mu_prev", "shift_mu_next", "decay_w0_fwd", "decay_w2_fwd", "decay_w0_bwd", "decay_w2_bwd",
    "iclr_a0_fwd", "iclr_a2_fwd", "iclr_a0_bwd", "iclr_a2_bwd", "gate_g2", "k_k", "k_a", "r_k", "ln_x_g", "ln_x_b",
    "q_norm_g", "w_uq", "kv_norm_g", "w_ukv", "mla_out_g", "w_out", "ln_ffn_g", "w_ffn_up", "ffn_conv_w", "ffn_conv_b",
    "w_ffn_down", "ln_final_g",
)


def _pad_rows(flat, rows):
    return jnp.pad(flat, (0, rows * 128 - flat.shape[0])).reshape(rows, 128)


def _rows_for(n, mult):
    rows = -(-n // 128)
    return -(-rows // mult) * mult


def _pack_shards_bf16(arrs, entries):
    parts = []
    for name, _, raw in entries:
        w = arrs[name][0]
        flat = lax.bitcast_convert_type(w, BF16).reshape(-1) if raw else w.astype(BF16).reshape(-1)
        parts.append(_pad_rows(flat, _rows_for(flat.shape[0], 32)))
    return jnp.concatenate(parts, axis=0)


def _unpack_gathered(g4, shard, chip, arrs, entries):
    out, off = {}, 0
    mine = (jnp.arange(N_CHIPS) == chip)[:, None, None]
    for name, axis, raw in entries:
        a, b = arrs[name].shape[1:]
        n = a * b * (2 if raw else 1)
        rows = _rows_for(n, 32)
        seg = jnp.where(mine, shard[None, off:off + rows], g4[:, off:off + rows]).reshape(4, rows * 128)[:, :n]
        off += rows
        if raw:
            seg = lax.bitcast_convert_type(seg.reshape(4, a * b, 2), F32)
        seg = seg.reshape(4, a, b)
        out[name] = jnp.concatenate([seg[s] for s in range(4)], axis=1) if axis == 1 else seg.reshape(4 * a, b)
    return out


def _pack_grads(full, arrs, entries, dtype=F32):
    parts = []
    for name, axis, _ in entries:
        a, b = arrs[name].shape[1:]
        g = full[name]
        if g.ndim == 3:
            sh = g
        else:
            sh = g.reshape(a, 4, b).transpose(1, 0, 2) if axis == 1 else g.reshape(4, a, b)
        rows = _rows_for(a * b, 8)
        flat = sh.reshape(4, a * b).astype(dtype)
        parts.append(jnp.pad(flat, ((0, 0), (0, rows * 128 - a * b))).reshape(4, rows, 128))
    total = sum(q.shape[1] for q in parts)
    parts.append(jnp.zeros((4, -(-total // 1024) * 1024 - total, 128), dtype))
    return jnp.concatenate(parts, axis=1)


def _unpack_grads(g, arrs, entries):
    out, off = {}, 0
    for name, _, _ in entries:
        a, b = arrs[name].shape[1:]
        rows = _rows_for(a * b, 8)
        out[name] = g[off:off + rows].reshape(-1)[:a * b].reshape(1, a, b)
        off += rows
    return out


def _pack_small(vals):
    flat = jnp.concatenate([vals[n].reshape(-1).astype(F32) for n in _SMALL] + [vals["_loss"].reshape(-1)])
    return _pad_rows(flat, _rows_for(flat.shape[0], 8))


def _unpack_small(buf, arrs):
    flat, out, off = buf.reshape(-1), {}, 0
    for n in _SMALL:
        size = arrs[n].size
        out[n] = flat[off:off + size].reshape(arrs[n].shape)
        off += size
    out["_loss"] = flat[off]
    return out


def _rot_cols(w):
    return jnp.concatenate([-w[..., 16:], w[..., :16]], axis=-1)


def _rot_cols_t(g):
    return jnp.concatenate([g[..., 16:], -g[..., :16]], axis=-1)


def _rope_tables(t):
    inv = jnp.power(ROPE_THETA, -jnp.arange(0, ROPE_DIM, 2, dtype=F32) / ROPE_DIM)
    ang = jnp.arange(t, dtype=F32)[:, None] * inv[None, :]
    one, zero = jnp.ones((t, 64), F32), jnp.zeros((t, 64), F32)
    cs = jnp.concatenate([one, jnp.cos(ang), jnp.cos(ang), zero[:, :32]], axis=1)
    sn = jnp.concatenate([zero, jnp.sin(ang), jnp.sin(ang), zero[:, :32]], axis=1)
    return cs, sn


def _block_diag(a, b):
    za = jnp.zeros_like(a)
    return jnp.concatenate([jnp.concatenate([a, za], axis=1), jnp.concatenate([za, b], axis=1)], axis=0)


def kernel(x, ln_mix_g, w_in, shift_mu_prev, shift_mu_next, decay_w0_fwd, decay_w2_fwd, decay_w0_bwd, decay_w2_bwd, iclr_a0_fwd, iclr_a2_fwd, iclr_a0_bwd, iclr_a2_bwd, gate_g2, k_k, k_a, r_k, ln_x_g, ln_x_b, q_norm_g, w_uq, kv_norm_g, w_ukv, mla_out_g, w_out, ln_ffn_g, w_ffn_up, ffn_conv_w, ffn_conv_b, w_ffn_down, ln_final_g, loss_target, m_ln_mix_g, m_w_in, m_shift_mu_prev, m_shift_mu_next, m_decay_w0_fwd, m_decay_w2_fwd, m_decay_w0_bwd, m_decay_w2_bwd, m_iclr_a0_fwd, m_iclr_a2_fwd, m_iclr_a0_bwd, m_iclr_a2_bwd, m_gate_g2, m_k_k, m_k_a, m_r_k, m_ln_x_g, m_ln_x_b, m_q_norm_g, m_w_uq, m_kv_norm_g, m_w_ukv, m_mla_out_g, m_w_out, m_ln_ffn_g, m_w_ffn_up, m_ffn_conv_w, m_ffn_conv_b, m_w_ffn_down, m_ln_final_g, v_ln_mix_g, v_w_in, v_shift_mu_prev, v_shift_mu_next, v_decay_w0_fwd, v_decay_w2_fwd, v_decay_w0_bwd, v_decay_w2_bwd, v_iclr_a0_fwd, v_iclr_a2_fwd, v_iclr_a0_bwd, v_iclr_a2_bwd, v_gate_g2, v_k_k, v_k_a, v_r_k, v_ln_x_g, v_ln_x_b, v_q_norm_g, v_w_uq, v_kv_norm_g, v_w_ukv, v_mla_out_g, v_w_out, v_ln_ffn_g, v_w_ffn_up, v_ffn_conv_w, v_ffn_conv_b, v_w_ffn_down, v_ln_final_g):
    arrs = dict(locals())
    for pre in ("", "m_", "v_"):
        arrs[pre + "w_in"] = jnp.swapaxes(arrs[pre + "w_in"], 1, 2)
    b, t, d = x.shape
    m = b * t
    x2 = x.reshape(m, d)
    tgt = loss_target.reshape(m, d)
    vec = lambda n: arrs[n].reshape(1, -1)

    core = lax.axis_index("c")
    chip = 2 * lax.axis_index("x") + lax.axis_index("y")
    def unpack(gathered, shard, entries):
        return _unpack_gathered(gathered.reshape(N_CHIPS, -1, 128), shard, chip, arrs, entries)

    shard_a, shard_b = _pack_shards_bf16(arrs, _BIG_A), _pack_shards_bf16(arrs, _BIG_B)
    fw = unpack(_allgather_weights(shard_a), shard_a, _BIG_A)
    win = fw["w_in"]
    zc = jnp.zeros((64, d), BF16)
    w_kr = win[2944:2976]
    rot_kr = jnp.swapaxes(_rot_cols(jnp.swapaxes(w_kr, 0, 1)), 0, 1)
    win_m = jnp.concatenate([win[1920:2944], zc, w_kr, zc[:32], zc, rot_kr, zc[:32]], axis=0)
    win_r = win[:RWKV_COLS]
    head = jnp.arange(512) // HEAD_DIM
    rw = dict(
        w0=jnp.concatenate([vec("decay_w0_fwd"), vec("decay_w0_bwd")], axis=1),
        w2=_block_diag(fw["decay_w2_fwd"], fw["decay_w2_bwd"]).astype(F32),
        a0=jnp.concatenate([vec("iclr_a0_fwd"), vec("iclr_a0_bwd")], axis=1),
        a2=_block_diag(fw["iclr_a2_fwd"], fw["iclr_a2_bwd"]).astype(F32),
        g2=fw["gate_g2"].astype(F32), k_k=vec("k_k"), k_a=vec("k_a"), r_k=vec("r_k"), ln_x_g=vec("ln_x_g"), ln_x_b=vec("ln_x_b"),
        ones_bd=(head[:, None] == head[None, :]).astype(F32),
    )
    cs, sn = _rope_tables(t)

    n1 = _rms_fwd(x2, vec("ln_mix_g"), "rms_mix")
    zm = _mm(n1, win_m, "nt", "proj_in_mla")
    zr = _mm(n1, win_r, "nt", "proj_in_rwkv")
    zs = _shift_fwd(zr.reshape(b, t, RWKV_COLS), vec("shift_mu_prev"), vec("shift_mu_next"))
    zs2 = zs.reshape(m, RWKV_COLS)
    wf, wb, kf, kb, kk, kaf, kab, gate = _prep_fwd(zs2, rw)
    r4 = lambda a: a.reshape(b, t, 512)
    f2 = lambda a: a.reshape(m, 512)
    kk4 = r4(kk)
    ops_f = (r4(wf), r4(kf), r4(kaf))
    ops_b = (r4(wb), r4(kb), r4(kab))
    y_f, hist_f, y_b, hist_b, s_last, gathered_b = _scan_fwd(zs, kk4, ops_f, ops_b, shard_b)
    fw.update(unpack(gathered_b, shard_b, _BIG_B))
    uq = fw["w_uq"].astype(F32).reshape(Q_RANK, HEADS, 96)
    z32 = jnp.zeros((Q_RANK, HEADS, 32), F32)
    wq = jnp.concatenate([uq[..., :64], uq[..., 64:], z32], axis=-1).reshape(Q_RANK, 1024)
    wqr = jnp.concatenate([z32, z32, _rot_cols(uq[..., 64:]), z32], axis=-1).reshape(Q_RANK, 1024)
    ukv = fw["w_ukv"].astype(F32).reshape(KV_RANK, HEADS, 128)
    wk = jnp.concatenate([ukv[..., :64], jnp.zeros_like(ukv[..., :64])], axis=-1).reshape(KV_RANK, 1024)
    wv = ukv[..., 64:].reshape(KV_RANK, 512)
    mp = dict(q_norm_g=vec("q_norm_g"), kv_norm_g=vec("kv_norm_g"), wq=wq, wqr=wqr, wk=wk, wv=wv)
    w_up_g, w_up_v = fw["w_ffn_up"][:, :D_FF], fw["w_ffn_up"][:, D_FF:]
    cw, cb = fw["ffn_conv_w"], vec("ffn_conv_b")
    y_f, y_b = f2(y_f), f2(y_b)
    y_rwkv = _post_fwd(y_f, y_b, zs2, kf, kb, gate, rw)
    q, kfull, v = _mla_fwd(zm, cs, sn, mp, t)
    o, lse = _attn_fwd(q, kfull, v, b, t)
    y_mla = _rms_fwd(o, vec("mla_out_g"), "rms_mla_out")
    ymix = jnp.concatenate([y_rwkv, y_mla], axis=1)
    h1 = _mm(ymix, fw["w_out"], "nn", "proj_out", add=x2)
    n2 = _rms_fwd(h1, vec("ln_ffn_g"), "rms_ffn")
    ug = _mm(n2, w_up_g, "nn", "ffn_up_gate")
    uv = _mm(n2, w_up_v, "nn", "ffn_up_val")
    r3f = lambda a: a.reshape(b, t, D_FF)
    act = _ffn_mid_fwd(r3f(ug), r3f(uv), cw, cb).reshape(m, D_FF)
    h2 = _mm(act, fw["w_ffn_down"], "nn", "ffn_down", add=h1)
    loss_tab, dh2, g_ln_final = _final(h2, vec("ln_final_g"), tgt)

    gfull = {}
    dact = _mm(dh2, fw["w_ffn_down"], "nt", "d_ffn_act")
    gfull["w_ffn_down"] = _mm(act, dh2, "tn", "g_ffn_down")
    dug, duv, tab_g, tab_v = _ffn_mid_bwd(r3f(ug), r3f(uv), cw, cb, r3f(dact))
    dug, duv = dug.reshape(m, D_FF), duv.reshape(m, D_FF)
    gfull["ffn_conv_w"] = jnp.concatenate([tab_g[0:3], tab_v[0:3]], axis=1)
    g_conv_b = jnp.concatenate([tab_g[3:4], tab_v[3:4]], axis=1)
    dn2 = _mm(duv, w_up_v, "nt", "d_ffn_in_val", add=_mm(dug, w_up_g, "nt", "d_ffn_in_gate"))
    shard_cols = arrs["w_ffn_up"].shape[2]
    gfull["w_ffn_up"] = jnp.concatenate([_mm(n2, dug, "tn", "g_ffn_up_gate", column_blocks=shard_cols),
                                         _mm(n2, duv, "tn", "g_ffn_up_val", column_blocks=shard_cols)], axis=0)
    dh1, g_ln_ffn = _rms_bwd(h1, vec("ln_ffn_g"), dn2, "rms_ffn_bwd", dres=dh2)
    dymix = _mm(dh1, fw["w_out"], "nt", "d_mix")
    gfull["w_out"] = _mm(ymix, dh1, "tn", "g_w_out")
    do, g_mla_out = _rms_bwd(o, vec("mla_out_g"), dymix, "rms_mla_out_bwd", dy_block=1)
    dq, dk, dv = _attn_bwd(q, kfull, v, o, lse, do, b, t)
    dzm, g_qn, g_kvn, g_wq, g_wqr, g_wk, g_wv = _mla_bwd(zm, cs, sn, mp, t, dq, dk, dv)
    gq3, gqr3 = g_wq.reshape(Q_RANK, HEADS, 128), g_wqr.reshape(Q_RANK, HEADS, 128)
    gfull["w_uq"] = jnp.concatenate(
        [gq3[..., :64], gq3[..., 64:96] + _rot_cols_t(gqr3[..., 64:96])], axis=-1
    ).reshape(Q_RANK, HEADS * 96)
    gfull["w_ukv"] = jnp.concatenate(
        [g_wk.reshape(KV_RANK, HEADS, 128)[..., :64], g_wv.reshape(KV_RANK, HEADS, 64)], axis=-1
    ).reshape(KV_RANK, 1024)
    def cores_first(entries, tag):
        packed = _pack_grads(gfull, arrs, entries)
        rh = packed.shape[1] // 2
        own = lax.dynamic_slice_in_dim(packed, core * rh, rh, axis=1)
        sib = _send_to_sibling(packed, True)
        return _add_rows([own.reshape(4 * rh, 128), sib.reshape(4 * rh, 128)], "sum_cores_" + tag, BF16).reshape(4, rh, 128)

    def join_halves(half, entries):
        other = _send_to_sibling(half, False)
        lower = jnp.where(core == 0, half, other)
        upper = jnp.where(core == 0, other, half)
        return _unpack_grads(jnp.concatenate([lower, upper], axis=0), arrs, entries)

    part_b = _pack_grads(gfull, arrs, _BIG_B, BF16)
    dys, dr_p, dk_p, dv_p, dgate, g_rk, g_lnx_g, g_lnx_b = _post_bwd(y_f, y_b, zs2, kf, kb, gate, rw, dymix)
    (dr_f, dwf, dkf, dkk_f, dkaf, dv_f, dr_b, dwb, dkb, dkk_b, dkab, dv_b, recv_b) = _scan_bwd(
        zs, kk4, r4(dys), ops_f, hist_f, ops_b, hist_b, s_last, part_b)
    rh_b = part_b.shape[1] // 2
    mine_b = lax.dynamic_slice(part_b, (chip, core * rh_b, 0), (1, rh_b, 128))[0]
    g_big = join_halves(_add_rows([mine_b] + [recv_b[k] for k in range(7)], "sum_devices_b"), _BIG_B)
    cts = dict(dwf=f2(dwf), dwb=f2(dwb), dkf=f2(dkf), dkb=f2(dkb), dkk_f=f2(dkk_f), dkk_b=f2(dkk_b), dkaf=f2(dkaf), dkab=f2(dkab),
               dr_f=f2(dr_f), dr_b=f2(dr_b), dr_p=dr_p, dk_p=dk_p, dv_p=dv_p, dg=dgate, dv_f=f2(dv_f), dv_b=f2(dv_b))
    dzs, g_w0, g_w2, g_a0, g_a2, g_g2, g_kk, g_ka = _prep_bwd(zs2, rw, cts)
    dzr, g_mu_p, g_mu_n = _shift_bwd(dzs.reshape(b, t, RWKV_COLS), zr.reshape(b, t, RWKV_COLS), vec("shift_mu_prev"), vec("shift_mu_next"))
    dzr = dzr.reshape(m, RWKV_COLS)
    gfull["decay_w2_fwd"], gfull["decay_w2_bwd"] = g_w2[:64, :512], g_w2[64:, 512:]
    gfull["iclr_a2_fwd"], gfull["iclr_a2_bwd"] = g_a2[:64, :512], g_a2[64:, 512:]
    gfull["gate_g2"] = g_g2
    dn1 = _mm(dzr, win_r, "nn", "d_proj_in_rwkv", add=_mm(dzm, win_m, "nn", "d_proj_in_mla"))
    g_m = _mm(dzm, n1, "tn", "g_w_in_mla")
    g_r = _mm(dzr, n1, "tn", "g_w_in_rwkv")
    g_kr = g_m[1088:1120] + jnp.swapaxes(_rot_cols_t(jnp.swapaxes(g_m[1216:1248], 0, 1)), 0, 1)
    gfull["w_in"] = jnp.concatenate([g_r, g_m[:1024], g_kr], axis=0)
    dx, g_ln_mix = _rms_bwd(x2, vec("ln_mix_g"), dn1, "rms_mix_bwd", dres=dh1)

    part_a = cores_first(_BIG_A, "a")
    recv_a = _scatter_to_chips(part_a)
    mine_a = lax.dynamic_index_in_dim(part_a, chip, axis=0, keepdims=False)
    g_big.update(join_halves(_add_rows([mine_a, recv_a[0], recv_a[1], recv_a[2]], "sum_chips_a"), _BIG_A))
    small = {
        "ln_mix_g": g_ln_mix, "shift_mu_prev": g_mu_p, "shift_mu_next": g_mu_n, "decay_w0_fwd": g_w0[:, :512],
        "decay_w0_bwd": g_w0[:, 512:], "iclr_a0_fwd": g_a0[:, :512], "iclr_a0_bwd": g_a0[:, 512:], "k_k": g_kk, "k_a": g_ka,
        "r_k": g_rk, "ln_x_g": g_lnx_g, "ln_x_b": g_lnx_b, "q_norm_g": g_qn, "kv_norm_g": g_kvn, "mla_out_g": g_mla_out,
        "ln_ffn_g": g_ln_ffn, "ffn_conv_b": g_conv_b, "ln_final_g": g_ln_final,
        "_loss": jnp.pad(loss_tab[0, 0:1], (0, 127)),
    }
    g_small_buf = _allreduce_small(_pack_small(small))
    g_small = _unpack_small(g_small_buf, arrs)

    grads, deltas, new_m, new_v = {}, {}, {}, {}
    for name, _, _ in _BIG:
        grads[name] = g_big[name]
        deltas[name], new_m[name], new_v[name] = _adamw(
            arrs[name], g_big[name], arrs["m_" + name], arrs["v_" + name], "adamw_" + name)
    pk = lambda pre: _pack_small({**{n: arrs[pre + n] for n in _SMALL}, "_loss": jnp.zeros((128,), F32)})
    sd, sm, sv = _adamw(pk(""), g_small_buf, pk("m_"), pk("v_"), "adamw_small")
    sd, sm, sv = _unpack_small(sd, arrs), _unpack_small(sm, arrs), _unpack_small(sv, arrs)
    for n in _SMALL:
        grads[n], deltas[n], new_m[n], new_v[n] = g_small[n], sd[n], sm[n], sv[n]
    for group in (grads, deltas, new_m, new_v):
        group["w_in"] = jnp.swapaxes(group["w_in"], 1, 2)

    return (g_small["_loss"], dx.reshape(b, t, d), *[grads[n] for n in _WEIGHTS], *[deltas[n] for n in _WEIGHTS],
            *[new_m[n] for n in _WEIGHTS], *[new_v[n] for n in _WEIGHTS])
```

```python
import functools
import math

import jax
import jax.numpy as jnp
from jax import lax
from jax.experimental import pallas as pl
from jax.experimental.pallas import tpu as pltpu

F32, BF16 = jnp.float32, jnp.bfloat16
MESH = pl.DeviceIdType.MESH
ANY = pl.BlockSpec(memory_space=pl.ANY)
VMEM = pl.BlockSpec(memory_space=pltpu.VMEM)
BS = pl.BlockSpec
SDS = jax.ShapeDtypeStruct

NORM_EPS = 1e-6
GN_EPS = 64e-5
L2_EPS = 1e-12
HEADS = 8
HEAD_DIM = 64
ROPE_DIM = 32
ROPE_THETA = 10000.0
MLA_SCALE = (64 + ROPE_DIM) ** -0.5
Q_RANK, KV_RANK = 768, 256
RWKV_COLS = 1920
MLA_PAD_COLS = Q_RANK + KV_RANK + 256
D_FF = 2816
ADAM_LR, ADAM_B1, ADAM_B2, ADAM_EPS, ADAM_WD, ADAM_STEP = 0.001, 0.9, 0.999, 1e-08, 0.01, 10

V7X_VMEM_LIMIT = 56 * 1024 * 1024
SCAN_CHUNK = 32
N_CHIPS = 4


def _cp(*sem):
    return pltpu.CompilerParams(dimension_semantics=sem, vmem_limit_bytes=V7X_VMEM_LIMIT)


def _tile(n, cands=(512, 640, 384, 256, 128)):
    for c in cands:
        if n % c == 0:
            return c
    return n


def _row_tile(n, cap=256):
    best = n
    for t in range(8, cap + 1, 8):
        if n % t == 0:
            best = t
    return best if best <= cap or n <= cap else n


def _rms(x, g):
    ms = jnp.mean(x * x, axis=-1, keepdims=True)
    return x * lax.rsqrt(ms + NORM_EPS) * g


@jax.custom_vjp
def _bdot(x, w):
    return jnp.dot(x.astype(BF16), w.astype(BF16), preferred_element_type=F32)


def _bdot_fwd(x, w):
    return _bdot(x, w), (x, w)


def _bdot_bwd(res, ct):
    x, w = res
    c = ct.astype(BF16)
    dx = lax.dot_general(c, w.astype(BF16), (((1,), (1,)), ((), ())), preferred_element_type=F32)
    dw = lax.dot_general(x.astype(BF16), c, (((0,), (0,)), ((), ())), preferred_element_type=F32)
    return dx.astype(x.dtype), dw.astype(w.dtype)


_bdot.defvjp(_bdot_fwd, _bdot_bwd)


@jax.custom_vjp
def _headsum(x, ones_bd):
    hi = x.astype(BF16)
    mid = (x - hi.astype(F32)).astype(BF16)
    ob = ones_bd.astype(BF16)
    return jnp.dot(hi, ob, preferred_element_type=F32) + jnp.dot(mid, ob, preferred_element_type=F32)


def _headsum_fwd(x, ones_bd):
    return _headsum(x, ones_bd), ones_bd


def _headsum_bwd(ones_bd, ct):
    return _headsum(ct, ones_bd), jnp.zeros_like(ones_bd)


_headsum.defvjp(_headsum_fwd, _headsum_bwd)


def _prep_fn(zs, w0, w2, a0, a2, g2, k_k, k_a, ones_bd):
    k = zs[:, 512:1024]
    wd = zs[:, 1536:1664]
    ad = zs[:, 1664:1792]
    gd = zs[:, 1792:1920]
    logit = w0 + _bdot(jnp.tanh(wd), w2)
    w = jnp.exp(-math.exp(-0.5) * jax.nn.sigmoid(logit))
    a = jax.nn.sigmoid(a0 + _bdot(ad, a2))
    g = _bdot(jax.nn.sigmoid(gd), g2)
    kkr = k * k_k
    nrm = jnp.sqrt(_headsum(kkr * kkr, ones_bd))
    kk = kkr / jnp.maximum(nrm, L2_EPS)
    a_f, a_b = a[:, :512], a[:, 512:]
    kf = k * (1.0 + (a_f - 1.0) * k_a)
    kb = k * (1.0 + (a_b - 1.0) * k_a)
    return w[:, :512], w[:, 512:], kf, kb, kk, kk * a_f, kk * a_b, g


def _post_fn(y, r, kf, kb, v, g, r_k, ln_g, ln_b, ones_bd):
    mu =_headsum(y, ones_bd) * (1.0 / HEAD_DIM)
    yc = y - mu
    var = _headsum(yc * yc, ones_bd) * (1.0 / HEAD_DIM)
    yn = yc * lax.rsqrt(var + GN_EPS) * ln_g + ln_b
    bonus = _headsum(r * (kf + kb) * r_k, ones_bd) * v
    return (yn + bonus) * g


def _cat8(x):
    return jnp.concatenate([x] * HEADS, axis=1)


def _mla_fn(zm, cs, sn, gq, gkv, wq, wqr, wk, wv):
    cq = zm[:, :Q_RANK]
    ckv = zm[:, Q_RANK:Q_RANK + KV_RANK]
    kr = zm[:, Q_RANK + KV_RANK:Q_RANK + KV_RANK + 128]
    krr = zm[:, Q_RANK + KV_RANK + 128:]
    cqn = _rms(cq, gq)
    ckvn = _rms(ckv, gkv)
    q = (_bdot(cqn, wq) * _cat8(cs) + _bdot(cqn, wqr) * _cat8(sn)) * MLA_SCALE
    kro = kr * cs + krr * sn
    kfull = _bdot(ckvn, wk) + _cat8(kro)
    v = _bdot(ckvn, wv)
    return q, kfull, v


def _adamw_math(w, g, m, v):
    m2 = ADAM_B1 * m + (1.0 - ADAM_B1) * g
    v2 = ADAM_B2 * v + (1.0 - ADAM_B2) * (g * g)
    m_hat = m2 / (1.0 - ADAM_B1 ** ADAM_STEP)
    v_hat = v2 / (1.0 - ADAM_B2 ** ADAM_STEP)
    delta = -ADAM_LR * (m_hat / (jnp.sqrt(v_hat) + ADAM_EPS) + ADAM_WD * w)
    return delta, m2, v2


_DIMS = {"nn": (((1,), (0,)), ((), ())), "nt": (((1,), (1,)), ((), ())), "tn": (((0,), (0,)), ((), ()))}


def _mm(a, b, mode, name, out_dtype=F32, add=None, column_blocks=None):
    if mode == "nn":
        (m, k), (_, n) = a.shape, b.shape
    elif mode == "nt":
        (m, k), (n, _) = a.shape, b.shape
    else:
        (k, m), (_, n) = a.shape, b.shape
    big = (1024, 1408, 768, 640, 512, 384, 256, 128)
    tm, tn, tk = _tile(m, big), column_blocks or _tile(n, big), _tile(k, (2816, 2048, 1024, 1408, 640, 512, 384, 256, 128))
    nk = k // tk

    def body(a_ref, b_ref, *rest):
        if add is None:
            o_ref, acc_ref = rest
        else:
            add_ref, o_ref, acc_ref = rest
        kk = pl.program_id(2)

        @pl.when(kk == 0)
        def _():
            acc_ref[...] = jnp.zeros_like(acc_ref)

        acc_ref[...] += lax.dot_general(
            a_ref[...].astype(BF16), b_ref[...].astype(BF16), _DIMS[mode], preferred_element_type=F32
        )

        @pl.when(kk == nk - 1)
        def _():
            r = acc_ref[...]
            if add is not None:
                r = r + add_ref[...]
            o_ref[...] = r.astype(out_dtype).reshape(o_ref.shape)

    a_spec = BS((tk, tm), lambda i, j, kk: (kk, i)) if mode == "tn" else BS((tm, tk), lambda i, j, kk: (i, kk))
    b_spec = BS((tn, tk), lambda i, j, kk: (j, kk)) if mode == "nt" else BS((tk, tn), lambda i, j, kk: (kk, j))
    o_spec = BS((tm, tn), lambda i, j, kk: (i, j))
    ins, specs = [a, b], [a_spec, b_spec]
    if add is not None:
        ins.append(add)
        specs.append(o_spec)
    out_shape = SDS((m, n), out_dtype)
    if column_blocks:
        assert add is None
        o_spec, out_shape = BS((1, tm, tn), lambda i, j, kk: (j, i, 0)), SDS((n // tn, m, tn), out_dtype)
    return pl.pallas_call(
        body, grid=(m // tm, n // tn, nk), in_specs=specs, out_specs=o_spec, out_shape=out_shape,
        scratch_shapes=[pltpu.VMEM((tm, tn), F32)], compiler_params=_cp("parallel", "parallel", "arbitrary"), name=name,
    )(*ins)


def _rms_fwd(x, g, name):
    m, d = x.shape
    tr = _tile(m)

    def body(x_ref, g_ref, o_ref):
        o_ref[...] = _rms(x_ref[...], g_ref[...]).astype(BF16)

    return pl.pallas_call(
        body, grid=(m // tr,), in_specs=[BS((tr, d), lambda i: (i, 0)), BS((1, d), lambda i: (0, 0))],
        out_specs=BS((tr, d), lambda i: (i, 0)), out_shape=SDS((m, d), BF16), compiler_params=_cp("parallel"), name=name,
    )(x, g)


def _rms_bwd(x, g, dy, name, dres=None, dy_block=0):
    m, d = x.shape
    tr = _row_tile(m, 512)

    def body(x_ref, g_ref, dy_ref, *rest):
        if dres is None:
            dx_ref, dg_ref = rest
        else:
            dres_ref, dx_ref, dg_ref = rest
        _, vjp = jax.vjp(_rms, x_ref[...], g_ref[...])
        dx, dg = vjp(dy_ref[...])
        if dres is not None:
            dx = dx + dres_ref[...]
        dx_ref[...] = dx

        @pl.when(pl.program_id(0) == 0)
        def _():
            dg_ref[...] = jnp.zeros_like(dg_ref)

        dg_ref[...] += dg

    row = BS((tr, d), lambda i: (i, 0))
    vec = BS((1, d), lambda i: (0, 0))
    ins, specs = [x, g, dy], [row, vec, BS((tr, d), lambda i: (i, dy_block))]
    if dres is not None:
        ins.append(dres)
        specs.append(row)
    return pl.pallas_call(
        body, grid=(m // tr,), in_specs=specs, out_specs=[row, vec], out_shape=[SDS((m, d), F32), SDS((1, d), F32)],
        compiler_params=_cp("arbitrary"), name=name,
    )(*ins)


def _final(h, g, tgt):
    m, d = h.shape
    tr = _row_tile(m, 512)

    def loss_fn(hh, gg, tt):
        e = _rms(hh, gg) - tt
        return 0.5 * jnp.sum(e * e) * (1.0 / d)

    def body(h_ref, g_ref, t_ref, l_ref, dh_ref, dg_ref):
        val, (dh, dg) = jax.value_and_grad(loss_fn, argnums=(0, 1))(h_ref[...], g_ref[...], t_ref[...])
        dh_ref[...] = dh

        @pl.when(pl.program_id(0) == 0)
        def _():
            dg_ref[...] = jnp.zeros_like(dg_ref)
            l_ref[...] = jnp.zeros_like(l_ref)

        dg_ref[...] += dg
        l_ref[...] += jnp.full(l_ref.shape, val, F32)

    row = BS((tr, d), lambda i: (i, 0))
    vec = BS((1, d), lambda i: (0, 0))
    return pl.pallas_call(
        body, grid=(m // tr,), in_specs=[row, vec, row], out_specs=[BS((8, 128), lambda i: (0, 0)), row, vec],
        out_shape=[SDS((8, 128), F32), SDS((m, d), F32), SDS((1, d), F32)], compiler_params=_cp("arbitrary"), name="final_loss",
    )(h, g, tgt)


def _prev_next(z, t):
    row = lax.broadcasted_iota(jnp.int32, z.shape, 0)
    zp = jnp.where(row == 0, 0.0, pltpu.roll(z, 1, axis=0))
    zn = jnp.where(row == t - 1, 0.0, pltpu.roll(z, t - 1, axis=0))
    return zp, zn


def _shift_fwd(z3, mu_p, mu_n):
    b, t, c = z3.shape
    nc = c // 128

    def body(z_ref, mp_ref, mn_ref, o_ref):
        z = z_ref[0]
        zp, zn = _prev_next(z, t)
        o_ref[0] = z + mp_ref[...] * (zp - z) + mn_ref[...] * (zn - z)

    blk = BS((1, t, 128), lambda i, j: (i, 0, j))
    vec = BS((1, 128), lambda i, j: (0, j))
    return pl.pallas_call(
        body, grid=(b, nc), in_specs=[blk, vec, vec], out_specs=blk, out_shape=SDS((b, t, c), F32),
        compiler_params=_cp("parallel", "parallel"), name="shift_fwd",
    )(z3, mu_p, mu_n)


def _shift_bwd(dzs3, z3, mu_p, mu_n):
    b, t, c = z3.shape
    nc = c // 128

    def body(d_ref, z_ref, mp_ref, mn_ref, dz_ref, dmp_ref, dmn_ref):
        d, z = d_ref[0], z_ref[0]
        mp, mn = mp_ref[...], mn_ref[...]
        zp, zn = _prev_next(z, t)
        _, dp_next = _prev_next(d * mp, t)
        dn_prev, _ = _prev_next(d * mn, t)
        dz_ref[0] = (d * (1.0 - mp - mn) + dp_next + dn_prev).astype(BF16)

        @pl.when(pl.program_id(1) == 0)
        def _():
            dmp_ref[...] = jnp.zeros_like(dmp_ref)
            dmn_ref[...] = jnp.zeros_like(dmn_ref)

        dmp_ref[...] += jnp.sum(d * (zp - z), axis=0, keepdims=True)
        dmn_ref[...] += jnp.sum(d * (zn - z), axis=0, keepdims=True)

    blk = BS((1, t, 128), lambda j, i: (i, 0, j))
    vec = BS((1, 128), lambda j, i: (0, j))
    return pl.pallas_call(
        body, grid=(nc, b), in_specs=[blk, blk, vec, vec], out_specs=[blk, vec, vec],
        out_shape=[SDS((b, t, c), BF16), SDS((1, c), F32), SDS((1, c), F32)],
        compiler_params=_cp("parallel", "arbitrary"), name="shift_bwd",
    )(dzs3, z3, mu_p, mu_n)


def _const(shape):
    nd = len(shape)
    return BS(shape, lambda i: (0,) * nd)


def _prep_fwd(zs, p):
    m = zs.shape[0]
    tr = 512
    params = [p["w0"], p["w2"], p["a0"], p["a2"], p["g2"], p["k_k"], p["k_a"], p["ones_bd"]]

    def body(zs_ref, w0, w2, a0, a2, g2, kk_, ka_, bd, wf, wb, kf, kb, kk, kaf, kab, g):
        outs = _prep_fn(zs_ref[...], w0[...], w2[...], a0[...], a2[...], g2[...], kk_[...], ka_[...], bd[...])
        for ref, val in zip((wf, wb, kf, kb, kk, kaf, kab, g), outs):
            ref[...] = val

    row = BS((tr, 512), lambda i: (i, 0))
    return pl.pallas_call(
        body, grid=(m // tr,), in_specs=[BS((tr, RWKV_COLS), lambda i: (i, 0))] + [_const(q.shape) for q in params],
        out_specs=[row] * 8, out_shape=[SDS((m, 512), F32)] * 8, compiler_params=_cp("parallel"), name="rwkv_prep_fwd",
    )(zs, *params)


def _prep_bwd(zs, p, ct_rows):
    m = zs.shape[0]
    tr = 256
    params = [p["w0"], p["w2"], p["a0"], p["a2"], p["g2"], p["k_k"], p["k_a"]]
    names = ["dwf", "dwb", "dkf", "dkb", "dkk_f", "dkk_b", "dkaf", "dkab", "dr_f", "dr_b", "dr_p", "dk_p", "dv_p", "dg",
             "dv_f", "dv_b"]
    rows = [ct_rows[n] for n in names]

    def body(zs_ref, w0, w2, a0, a2, g2, kk_, ka_, bd, *rest):
        c = {n: r[...] for n, r in zip(names, rest[:len(names)])}
        outs = rest[len(names):]
        dzs_ref, grads = outs[0], outs[1:]
        ones_bd = bd[...]
        _, vjp = jax.vjp(
            lambda *q: _prep_fn(*q, ones_bd), zs_ref[...], w0[...], w2[...], a0[...], a2[...], g2[...], kk_[...], ka_[...]
        )
        cts = (c["dwf"], c["dwb"], c["dkf"] + c["dk_p"], c["dkb"] + c["dk_p"], c["dkk_f"] + c["dkk_b"], c["dkaf"], c["dkab"], c["dg"])
        dzs, *dparams = vjp(cts)
        dr = c["dr_f"] + c["dr_b"] + c["dr_p"]
        dv = c["dv_f"] + c["dv_b"] + c["dv_p"]
        dzs_ref[:, 0:512] = dzs[:, 0:512] + dr
        dzs_ref[:, 512:1024] = dzs[:, 512:1024]
        dzs_ref[:, 1024:1536] = dzs[:, 1024:1536] + dv
        dzs_ref[:, 1536:1920] = dzs[:, 1536:1920]

        @pl.when(pl.program_id(0) == 0)
        def _():
            for gr in grads:
                gr[...] = jnp.zeros_like(gr)

        for gr, val in zip(grads, dparams):
            gr[...] += val

    row = BS((tr, 512), lambda i: (i, 0))
    return pl.pallas_call(
        body, grid=(m // tr,),
        in_specs=[BS((tr, RWKV_COLS), lambda i: (i, 0))] + [_const(q.shape) for q in params] + [_const(p["ones_bd"].shape)]
        + [row] * len(names),
        out_specs=[BS((tr, RWKV_COLS), lambda i: (i, 0))] + [_const(q.shape) for q in params],
        out_shape=[SDS((m, RWKV_COLS), F32)] + [SDS(q.shape, F32) for q in params],
        compiler_params=_cp("arbitrary"), name="rwkv_prep_bwd",
    )(zs, *params, p["ones_bd"], *rows)


def _post_specs(tr):
    r = BS((tr, 512), lambda i: (i, 0))
    v = BS((tr, 512), lambda i: (i, 2))
    row = BS((tr, 512), lambda i: (i, 0))
    return r, v, row


def _post_fwd(y_f, y_b, zs, kf, kb, g, p):
    m = zs.shape[0]
    tr = 512
    r, v, row = _post_specs(tr)
    vecs = [p["r_k"], p["ln_x_g"], p["ln_x_b"], p["ones_bd"]]

    def body(yf, yb, r_ref, v_ref, kf_ref, kb_ref, g_ref, rk, lg, lb, bd, o_ref):
        o_ref[...] = _post_fn(
            yf[...] + yb[...], r_ref[...], kf_ref[...], kb_ref[...], v_ref[...], g_ref[...], rk[...], lg[...], lb[...], bd[...]
        ).astype(BF16)

    return pl.pallas_call(
        body, grid=(m // tr,), in_specs=[row, row, r, v, row, row, row] + [_const(q.shape) for q in vecs],
        out_specs=row, out_shape=SDS((m, 512), BF16), compiler_params=_cp("parallel"), name="rwkv_post_fwd",
    )(y_f, y_b, zs, zs, kf, kb, g, *vecs)


def _post_bwd(y_f, y_b, zs, kf, kb, g, p, dymix):
    m = zs.shape[0]
    tr = 256
    r, v, row = _post_specs(tr)
    vecs = [p["r_k"], p["ln_x_g"], p["ln_x_b"]]

    def body(yf, yb, r_ref, v_ref, kf_ref, kb_ref, g_ref, rk, lg, lb, bd, dy_ref, dyo, dr, dk, dv, dg, drk, dlg, dlb):
        ones_bd = bd[...]
        _, vjp = jax.vjp(
            lambda *q: _post_fn(*q, ones_bd),
            yf[...] + yb[...], r_ref[...], kf_ref[...], kb_ref[...], v_ref[...], g_ref[...], rk[...], lg[...], lb[...],
        )
        c_y, c_r, c_kf, _, c_v, c_g, c_rk, c_lg, c_lb = vjp(dy_ref[...])
        dyo[...] = c_y
        dr[...] = c_r
        dk[...] = c_kf
        dv[...] = c_v
        dg[...] = c_g

        @pl.when(pl.program_id(0) == 0)
        def _():
            for ref in (drk, dlg, dlb):
                ref[...] = jnp.zeros_like(ref)

        drk[...] += c_rk
        dlg[...] += c_lg
        dlb[...] += c_lb

    vec = _const((1, 512))
    return pl.pallas_call(
        body, grid=(m // tr,),
        in_specs=[row, row, r, v, row, row, row] + [_const(q.shape) for q in vecs] + [_const(p["ones_bd"].shape), row],
        out_specs=[row, row, row, row, row, vec, vec, vec],
        out_shape=[SDS((m, 512), F32)] * 5 + [SDS((1, 512), F32)] * 3,
        compiler_params=_cp("arbitrary"), name="rwkv_post_bwd",
    )(y_f, y_b, zs, zs, kf, kb, g, *vecs, p["ones_bd"], dymix)


SCAN_MXU_GROUPS = 2


def _half_ones():
    ri = lax.broadcasted_iota(jnp.int32, (128, 128), 0)
    ci = lax.broadcasted_iota(jnp.int32, (128, 128), 1)
    return jnp.where((ri < 64) == (ci < 64), 1.0, 0.0).astype(BF16)


def _half_sums(xs, ones):
    out = []
    per = -(-len(xs) // SCAN_MXU_GROUPS)
    for g in range(0, len(xs), per):
        part = xs[g:g + per]
        res = jnp.dot(jnp.concatenate(part, axis=0).astype(BF16), ones, preferred_element_type=F32)
        out += [res[64 * i:64 * i + 64] for i in range(len(part))]
    return out


def _scan_specs(b, t):
    nc = t // SCAN_CHUNK
    up, down = (lambda c: c), (lambda c: nc - 1 - c)
    rows = [BS((b, SCAN_CHUNK, 512), lambda c, ci=ci: (0, ci(c), 0)) for ci in (up, down)]
    vrows = [BS((b, SCAN_CHUNK, 512), lambda c, ci=ci: (0, ci(c), 2)) for ci in (up, down)]
    hist = [BS((SCAN_CHUNK, b * 4, 64, 128), lambda c, ci=ci: (ci(c), 0, 0, 0)) for ci in (up, down)]
    return nc, rows, vrows, hist


class _Window:
    def __init__(self, g, ascending):
        self.bases = [pl.multiple_of(g * 8, 8) if asc else pl.multiple_of(SCAN_CHUNK - 8 - g * 8, 8) for asc in ascending]
        self.ascending = ascending
        self.blocks = {}
        self.row_id = lax.broadcasted_iota(jnp.int32, (8, 128), 0)

    def j(self, d, s):
        return s if self.ascending[d] else 7 - s

    def time(self, d, s):
        return self.bases[d] + self.j(d, s)

    def row(self, ref, d, bi, cols, s):
        key = (id(ref), d, bi, cols.start)
        if key not in self.blocks:
            self.blocks[key] = ref[bi, pl.ds(self.bases[d], 8), cols]
        jj = self.j(d, s)
        return self.blocks[key][jj:jj + 1, :]

    def put(self, buf, key, d, s, row):
        prev = buf.get(key)
        new = jnp.broadcast_to(row, (8, 128))
        buf[key] = new if prev is None else jnp.where(self.row_id == self.j(d, s), new, prev)

    def flush(self, buf, refs_of):
        for key, val in buf.items():
            ref, d, bi, cols = refs_of(key)
            ref[bi, pl.ds(self.bases[d], 8), cols] = val


def _pairs(b):
    return [(bi * 4 + p, bi, slice(128 * p, 128 * p + 128)) for bi in range(b) for p in range(4)]


def _colsum(x):
    return jnp.sum(x, axis=0, keepdims=True)


def _pair_matvec(row, mat):
    rid = lax.broadcasted_iota(jnp.int32, (8, 64), 0)
    lhs = jnp.where(rid == 0, row[:, :64], jnp.where(rid == 1, row[:, 64:], 0.0))
    out = jnp.dot(lhs.astype(BF16), mat.astype(BF16), preferred_element_type=F32)
    lo = lax.broadcasted_iota(jnp.int32, (1, 128), 1) < 64
    return jnp.where(lo, out[0:1], out[1:2])


def _eye_mask():
    return (lax.broadcasted_iota(jnp.int32, (64, 128), 1) & 63) == lax.broadcasted_iota(jnp.int32, (64, 128), 0)


def _scan_fwd(zs, kk, ops_f, ops_b, shard):
    b, t = zs.shape[:2]
    nc, rows, vrows, hist = _scan_specs(b, t)
    npair = b * 4

    def body(*refs):
        ins, shard_ref, outs, s_ref = refs[:12], refs[12], refs[13:17], refs[17]
        gather = (shard_ref, *refs[18:21])
        dirs = [dict(zip(("r", "kk", "v", "w", "k", "ka", "y", "h"), (*ins[6 * d:6 * d + 6], *outs[2 * d:2 * d + 2])))
                for d in (0, 1)]

        @pl.when(pl.program_id(0) == 0)
        def _():
            s_ref[...] = jnp.zeros_like(s_ref)
            _gather_halves(*gather, "start")

        @pl.when(pl.program_id(0) == nc - 1)
        def _():
            _gather_halves(*gather, "finish")

        ones, eye = _half_ones(), _eye_mask()
        chains = [(d, pr, bi, cols) for d in (0, 1) for pr, bi, cols in _pairs(b)]

        def eight_steps(g, carry):
            win = _Window(g, (True, False))
            ybuf = {}
            for s in range(8):
                s_prev, xa = [], []
                for d, pr, bi, cols in chains:
                    q = dirs[d]
                    st = s_ref[d * npair + pr]
                    q["h"][win.time(d, s), pr] = st
                    s_prev.append(st)
                    xa += [st * win.row(q["kk"], d, bi, cols, s), jnp.where(eye, win.row(q["v"], d, bi, cols, s), 0.0)]
                ra = _half_sums(xa, ones)
                xb = []
                for i, (d, pr, bi, cols) in enumerate(chains):
                    q = dirs[d]
                    s_new = s_prev[i] * win.row(q["w"], d, bi, cols, s) - ra[2 * i] * win.row(q["ka"], d, bi, cols, s) \
                        + ra[2 * i + 1] * win.row(q["k"], d, bi, cols, s)
                    s_ref[d * npair + pr] = s_new
                    xb.append(s_new * win.row(q["r"], d, bi, cols, s))
                rb = _half_sums(xb, ones)
                for i, (d, pr, bi, cols) in enumerate(chains):
                    win.put(ybuf, i, d, s, _colsum(jnp.where(eye, rb[i], 0.0)))
            win.flush(ybuf, lambda i: (dirs[chains[i][0]]["y"], chains[i][0], chains[i][2], chains[i][3]))
            return carry

        def sixteen_steps(g2, carry):
            return eight_steps(2 * g2 + 1, eight_steps(2 * g2, carry))

        lax.fori_loop(0, SCAN_CHUNK // 16, sixteen_steps, 0)

    row_shape, hist_shape = SDS((b, t, 512), F32), SDS((t, npair, 64, 128), F32)
    state = (2 * npair, 64, 128)
    return pl.pallas_call(
        body, grid=(nc,), in_specs=sum(([rows[d], rows[d], vrows[d]] + [rows[d]] * 3 for d in (0, 1)), []) + [ANY],
        out_specs=[rows[0], hist[0], rows[1], hist[1], BS(state, lambda c: (0, 0, 0)), ANY],
        out_shape=[row_shape, hist_shape, row_shape, hist_shape, SDS(state, F32), _gathered_shape(shard)],
        scratch_shapes=_gather_sems(), compiler_params=_cp("arbitrary"), name="wkv_scan",
    )(zs, kk, zs, *ops_f, zs, kk, zs, *ops_b, shard)


def _scan_bwd(zs, kk, dy, ops_f, hist_f, ops_b, hist_b, s_last, partials):
    b, t = zs.shape[:2]
    nc, rows, vrows, hist = _scan_specs(b, t)
    npair = b * 4
    names_in = ("r", "kk", "v", "dy", "w", "k", "ka", "h")
    names_out = ("dr", "dw", "dk", "dkk", "dka", "dv")

    def body(*refs):
        ins, last_ref, part_ref, outs, recv_ref = refs[:16], refs[16], refs[17], refs[18:30], refs[30]
        ds_ref, after_ref = refs[31], refs[32]
        scatter = (part_ref, recv_ref, refs[33], refs[34])
        dirs = [dict(zip(names_in + names_out, (*ins[8 * d:8 * d + 8], *outs[6 * d:6 * d + 6]))) for d in (0, 1)]

        @pl.when(pl.program_id(0) == 0)
        def _():
            ds_ref[...] = jnp.zeros_like(ds_ref)
            after_ref[...] = last_ref[...]
            _scatter_to_all(*scatter, "start")

        @pl.when(pl.program_id(0) == nc - 1)
        def _():
            _scatter_to_all(*scatter, "finish")

        ones, eye = _half_ones(), _eye_mask()
        chains = [(d, pr, bi, cols) for d in (0, 1) for pr, bi, cols in _pairs(b)]

        def eight_steps(g, carry):
            win = _Window(g, (False, True))
            obuf = {}
            s_after = [after_ref[d * npair + pr] for d, pr, _, _ in chains]
            for s in range(8):
                row = lambda name, d, bi, cols: win.row(dirs[d][name], d, bi, cols, s)
                s_prev, xa = [], []
                for d, pr, bi, cols in chains:
                    st = dirs[d]["h"][win.time(d, s), pr]
                    s_prev.append(st)
                    xa += [st * row("kk", d, bi, cols), jnp.where(eye, row("dy", d, bi, cols), 0.0)]
                ra = _half_sums(xa, ones)
                ds_now, xb = [], []
                for i, (d, pr, bi, cols) in enumerate(chains):
                    skk, dycol = ra[2 * i], ra[2 * i + 1]
                    ds = ds_ref[d * npair + pr] + dycol * row("r", d, bi, cols)
                    win.put(obuf, (i, "dr"), d, s, _pair_matvec(row("dy", d, bi, cols), s_after[i]))
                    win.put(obuf, (i, "dk"), d, s, _pair_matvec(row("v", d, bi, cols), ds))
                    win.put(obuf, (i, "dka"), d, s, -_colsum(ds * skk))
                    win.put(obuf, (i, "dw"), d, s, _colsum(ds * s_prev[i]))
                    ds_now.append(ds)
                    xb += [ds * row("k", d, bi, cols), ds * row("ka", d, bi, cols)]
                rb = _half_sums(xb, ones)
                for i, (d, pr, bi, cols) in enumerate(chains):
                    dskk_neg = rb[2 * i + 1]
                    win.put(obuf, (i, "dv"), d, s, _colsum(jnp.where(eye, rb[2 * i], 0.0)))
                    win.put(obuf, (i, "dkk"), d, s, -_colsum(s_prev[i] * dskk_neg))
                    ds_ref[d * npair + pr] = ds_now[i] * row("w", d, bi, cols) - dskk_neg * row("kk", d, bi, cols)
                s_after = s_prev
            for i, (d, pr, _, _) in enumerate(chains):
                after_ref[d * npair + pr] = s_after[i]
            win.flush(obuf, lambda key: (dirs[chains[key[0]][0]][key[1]], chains[key[0]][0], chains[key[0]][2], chains[key[0]][3]))
            return carry

        def sixteen_steps(g2, carry):
            return eight_steps(2 * g2 + 1, eight_steps(2 * g2, carry))

        lax.fori_loop(0, SCAN_CHUNK // 16, sixteen_steps, 0)

    row_shape = SDS((b, t, 512), F32)
    state = (2 * npair, 64, 128)
    return pl.pallas_call(
        body, grid=(nc,),
        in_specs=sum(([rows[d], rows[d], vrows[d]] + [rows[d]] * 4 + [hist[d]] for d in (1, 0)), [])
        + [BS(state, lambda c: (0, 0, 0)), ANY],
        out_specs=[rows[1]] * 6 + [rows[0]] * 6 + [ANY],
        out_shape=[row_shape] * 12 + [SDS((7, partials.shape[1] // 2, 128), partials.dtype)],
        scratch_shapes=[pltpu.VMEM(state, F32), pltpu.VMEM(state, F32), pltpu.SemaphoreType.DMA((7,)),
                        pltpu.SemaphoreType.DMA((7,))],
        compiler_params=_cp("arbitrary"), name="wkv_scan_bwd",
    )(zs, kk, zs, dy, *ops_f, hist_f, zs, kk, zs, dy, *ops_b, hist_b, s_last, partials)


def _mla_fwd(zm, cs, sn, p, t):
    m = zm.shape[0]
    tr = 512
    per = t // tr
    params = [p["q_norm_g"], p["kv_norm_g"], p["wq"], p["wqr"], p["wk"], p["wv"]]

    def body(z_ref, cs_ref, sn_ref, gq, gkv, wq, wqr, wk, wv, q_ref, k_ref, v_ref):
        q, kf, v = _mla_fn(z_ref[...], cs_ref[...], sn_ref[...], gq[...], gkv[...], wq[...], wqr[...], wk[...], wv[...])
        q_ref[...] = q.astype(BF16)
        k_ref[...] = kf.astype(BF16)
        v_ref[...] = v.astype(BF16)

    tab = BS((tr, 128), lambda i: (i % per, 0))
    return pl.pallas_call(
        body, grid=(m // tr,), in_specs=[BS((tr, MLA_PAD_COLS), lambda i: (i, 0)), tab, tab] + [_const(q.shape) for q in params],
        out_specs=[BS((tr, 1024), lambda i: (i, 0)), BS((tr, 1024), lambda i: (i, 0)), BS((tr, 512), lambda i: (i, 0))],
        out_shape=[SDS((m, 1024), BF16), SDS((m, 1024), BF16), SDS((m, 512), BF16)], compiler_params=_cp("parallel"), name="mla_prep_fwd",
    )(zm, cs, sn, *params)


def _mla_bwd(zm, cs, sn, p, t, dq, dk, dv):
    m = zm.shape[0]
    tr = 256
    per = t // tr
    params = [p["q_norm_g"], p["kv_norm_g"], p["wq"], p["wqr"], p["wk"], p["wv"]]

    def body(z_ref, cs_ref, sn_ref, gq, gkv, wq, wqr, wk, wv, dq_ref, dk_ref, dv_ref, dz_ref, *grads):
        cs_v, sn_v = cs_ref[...], sn_ref[...]
        _, vjp = jax.vjp(
            lambda *q: _mla_fn(q[0], cs_v, sn_v, *q[1:]), z_ref[...], gq[...], gkv[...], wq[...], wqr[...], wk[...], wv[...]
        )
        dz, *dparams = vjp((dq_ref[...], dk_ref[...], dv_ref[...]))
        dz_ref[...] = dz.astype(BF16)

        @pl.when(pl.program_id(0) == 0)
        def _():
            for gr in grads:
                gr[...] = jnp.zeros_like(gr)

        for gr, val in zip(grads, dparams):
            gr[...] += val

    tab = BS((tr, 128), lambda i: (i % per, 0))
    wide = BS((tr, 1024), lambda i: (i, 0))
    return pl.pallas_call(
        body, grid=(m // tr,),
        in_specs=[BS((tr, MLA_PAD_COLS), lambda i: (i, 0)), tab, tab] + [_const(q.shape) for q in params]
        + [wide, wide, BS((tr, 512), lambda i: (i, 0))],
        out_specs=[BS((tr, MLA_PAD_COLS), lambda i: (i, 0))] + [_const(q.shape) for q in params],
        out_shape=[SDS((m, MLA_PAD_COLS), BF16)] + [SDS(q.shape, F32) for q in params],
        compiler_params=_cp("arbitrary"), name="mla_prep_bwd",
    )(zm, cs, sn, *params, dq, dk, dv)


_NT = (((1,), (1,)), ((), ()))
_TN = (((0,), (0,)), ((), ()))


def _attn_fwd(q, kf, v, b, t):
    m = q.shape[0]
    tq = 256
    nq = t // tq

    def body(q_ref, k_ref, v_ref, o_ref, l_ref):
        lo = lax.broadcasted_iota(jnp.int32, (1, 128), 1) < 64
        v_all = v_ref[...]
        o = jnp.zeros((tq, 128), F32)
        lse = []
        for h in range(2):
            hs = slice(128 * h, 128 * h + 128)
            s = lax.dot_general(q_ref[:, hs], k_ref[:, hs], _NT, preferred_element_type=F32)
            mx = jnp.max(s, axis=1, keepdims=True)
            e = jnp.exp(s - mx)
            den = jnp.sum(e, axis=1, keepdims=True)
            vh = jnp.where(lo if h == 0 else jnp.logical_not(lo), v_all, jnp.zeros_like(v_all))
            o = o + jnp.dot(e.astype(BF16), vh, preferred_element_type=F32) / den
            lse.append(mx + jnp.log(den))
        o_ref[...] = o
        l_ref[...] = jnp.where(lo, lse[0], lse[1])

    return pl.pallas_call(
        body, grid=(b, 4, nq),
        in_specs=[BS((tq, 256), lambda bi, hp, i: (bi * nq + i, hp)), BS((t, 256), lambda bi, hp, i: (bi, hp)),
                  BS((t, 128), lambda bi, hp, i: (bi, hp))],
        out_specs=[BS((tq, 128), lambda bi, hp, i: (bi * nq + i, hp))] * 2,
        out_shape=[SDS((m, 512), F32), SDS((m, 512), F32)], compiler_params=_cp("parallel", "parallel", "arbitrary"), name="attn_fwd",
    )(q, kf, v)


def _attn_bwd(q, kf, v, o, lse, do, b, t):
    m = q.shape[0]
    tq = 256
    nq = t // tq

    def body(q_ref, k_ref, v_ref, o_ref, l_ref, do_ref, dq_ref, dk_ref, dv_ref):
        lo = lax.broadcasted_iota(jnp.int32, (1, 128), 1) < 64

        @pl.when(pl.program_id(2) == 0)
        def _():
            dk_ref[...] = jnp.zeros_like(dk_ref)
            dv_ref[...] = jnp.zeros_like(dv_ref)

        v_all, o_all, l_all, do_all = v_ref[...], o_ref[...], l_ref[...], do_ref[...]
        dv_acc = jnp.zeros((t, 128), F32)
        for h in range(2):
            hs = slice(128 * h, 128 * h + 128)
            mask = lo if h == 0 else jnp.logical_not(lo)
            qh, kh = q_ref[:, hs], k_ref[:, hs]
            s = lax.dot_general(qh, kh, _NT, preferred_element_type=F32)
            lse_h = jnp.max(jnp.where(mask, l_all, -jnp.inf), axis=1, keepdims=True)
            pr = jnp.exp(s - lse_h)
            do_h = jnp.where(mask, do_all, 0.0)
            dp = lax.dot_general(do_h.astype(BF16), v_all, _NT, preferred_element_type=F32)
            dsum = jnp.sum(do_h * o_all, axis=1, keepdims=True)
            ds = (pr * (dp - dsum)).astype(BF16)
            dq_ref[:, hs] = jnp.dot(ds, kh, preferred_element_type=F32)
            dk_ref[:, hs] += lax.dot_general(ds, qh, _TN, preferred_element_type=F32)
            dv_acc = dv_acc + lax.dot_general(pr.astype(BF16), do_h.astype(BF16), _TN, preferred_element_type=F32)
        dv_ref[...] += dv_acc

    qspec = BS((tq, 256), lambda bi, hp, i: (bi * nq + i, hp))
    kspec = BS((t, 256), lambda bi, hp, i: (bi, hp))
    vspec = BS((t, 128), lambda bi, hp, i: (bi, hp))
    ospec = BS((tq, 128), lambda bi, hp, i: (bi * nq + i, hp))
    return pl.pallas_call(
        body, grid=(b, 4, nq), in_specs=[qspec, kspec, vspec, ospec, ospec, ospec], out_specs=[qspec, kspec, vspec],
        out_shape=[SDS((m, 1024), F32), SDS((m, 1024), F32), SDS((m, 512), F32)],
        compiler_params=_cp("parallel", "parallel", "arbitrary"), name="attn_bwd",
    )(q, kf, v, o, lse, do)


def _conv3(u, w_ref, b_ref, t):
    up, un = _prev_next(u, t)
    return w_ref[0:1, :] * up + w_ref[1:2, :] * u + w_ref[2:3, :] * un + b_ref[...], up, un


def _ffn_mid_fwd(ug3, uv3, cw, cb):
    b, t, f = ug3.shape
    nc = f // 256

    def body(ug_ref, uv_ref, wg_ref, wv_ref, bg_ref, bv_ref, a_ref):
        gc, _, _ = _conv3(ug_ref[0], wg_ref, bg_ref, t)
        vc, _, _ = _conv3(uv_ref[0], wv_ref, bv_ref, t)
        a_ref[0] = (gc * jax.nn.sigmoid(gc) * vc).astype(BF16)

    blk = BS((1, t, 256), lambda i, j: (i, 0, j))
    return pl.pallas_call(
        body, grid=(b, nc),
        in_specs=[blk, blk, BS((3, 256), lambda i, j: (0, j)), BS((3, 256), lambda i, j: (0, j + nc)),
                  BS((1, 256), lambda i, j: (0, j)), BS((1, 256), lambda i, j: (0, j + nc))],
        out_specs=blk, out_shape=SDS((b, t, f), BF16), compiler_params=_cp("parallel", "parallel"), name="ffn_mid_fwd",
    )(ug3, uv3, cw, cw, cb, cb)


def _ffn_mid_bwd(ug3, uv3, cw, cb, da3):
    b, t, f = ug3.shape
    nc = f // 256

    def half(u, up, un, dc, w_ref):
        dprev, dnext = _prev_next(dc, t)
        du = w_ref[1:2, :] * dc + w_ref[0:1, :] * dnext + w_ref[2:3, :] * dprev
        sums = [jnp.sum(dc * q, axis=0, keepdims=True) for q in (up, u, un)] + [jnp.sum(dc, axis=0, keepdims=True)]
        row = lax.broadcasted_iota(jnp.int32, (8, 256), 0)
        tab = jnp.zeros((8, 256), F32)
        for i, s in enumerate(sums):
            tab = jnp.where(row == i, s, tab)
        return du, tab

    def body(ug_ref, uv_ref, wg_ref, wv_ref, bg_ref, bv_ref, da_ref, dug_ref, duv_ref, tg_ref, tv_ref):
        ug, uv, da = ug_ref[0], uv_ref[0], da_ref[0]
        gc, gp, gn = _conv3(ug, wg_ref, bg_ref, t)
        vc, vp, vn = _conv3(uv, wv_ref, bv_ref, t)
        sg = jax.nn.sigmoid(gc)
        d_gc = da * vc * (sg * (1.0 + gc * (1.0 - sg)))
        d_vc = da * (gc * sg)
        dug, tg = half(ug, gp, gn, d_gc, wg_ref)
        duv, tv = half(uv, vp, vn, d_vc, wv_ref)
        dug_ref[0] = dug.astype(BF16)
        duv_ref[0] = duv.astype(BF16)

        @pl.when(pl.program_id(1) == 0)
        def _():
            tg_ref[...] = jnp.zeros_like(tg_ref)
            tv_ref[...] = jnp.zeros_like(tv_ref)

        tg_ref[...] += tg
        tv_ref[...] += tv

    blk = BS((1, t, 256), lambda j, i: (i, 0, j))
    tab = BS((8, 256), lambda j, i: (0, j))
    return pl.pallas_call(
        body, grid=(nc, b),
        in_specs=[blk, blk, BS((3, 256), lambda j, i: (0, j)), BS((3, 256), lambda j, i: (0, j + nc)),
                  BS((1, 256), lambda j, i: (0, j)), BS((1, 256), lambda j, i: (0, j + nc)), blk],
        out_specs=[blk, blk, tab, tab],
        out_shape=[SDS((b, t, f), BF16), SDS((b, t, f), BF16), SDS((8, f), F32), SDS((8, f), F32)],
        compiler_params=_cp("parallel", "arbitrary"), name="ffn_mid_bwd",
    )(ug3, uv3, cw, cw, cb, cb, da3)


def _add_rows(parts, name, out_dtype=F32):
    r = parts[0].shape[0]
    tr = _row_tile(r, 1024)
    n = len(parts)

    def body(*refs):
        acc = refs[0][...].astype(F32)
        for q in refs[1:n]:
            acc = acc + q[...].astype(F32)
        refs[n][...] = acc.astype(out_dtype)

    row = BS((tr, 128), lambda i: (i, 0))
    return pl.pallas_call(
        body, grid=(r // tr,), in_specs=[row] * n, out_specs=row, out_shape=SDS((r, 128), out_dtype),
        compiler_params=_cp("parallel"), name=name,
    )(*parts)


def _adamw(w, g, m, v, name):
    lead = w.shape[:-2]
    r, c = w.shape[-2:]
    tr = _row_tile(r)

    def body(w_ref, g_ref, m_ref, v_ref, d_ref, m2_ref, v2_ref):
        d, m2, v2 = _adamw_math(w_ref[...], g_ref[...], m_ref[...], v_ref[...])
        d_ref[...] = d
        m2_ref[...] = m2
        v2_ref[...] = v2

    blk = BS((1,) * len(lead) + (tr, c), lambda i: (0,) * len(lead) + (i, 0))
    return pl.pallas_call(
        body, grid=(r // tr,), in_specs=[blk] * 4, out_specs=[blk] * 3, out_shape=[SDS(w.shape, F32)] * 3,
        compiler_params=_cp("parallel"), name=name,
    )(w, g, m, v)


def _place():
    return lax.axis_index("x"), lax.axis_index("y"), lax.axis_index("c")


def _flip(v, bit):
    return 1 - v if bit else v


def _allgather_weights(shard):
    def body(x_ref, out_ref, send_sems, recv_sems):
        _gather_halves(x_ref, out_ref, send_sems, recv_sems, "start")
        _gather_halves(x_ref, out_ref, send_sems, recv_sems, "finish")

    return pl.pallas_call(
        body, out_shape=_gathered_shape(shard), in_specs=[ANY], out_specs=ANY, scratch_shapes=_gather_sems(),
        name="allgather_weights",
    )(shard)


def _gathered_shape(shard):
    return SDS((8 * (shard.shape[0] // 2), 128), shard.dtype)


def _gather_sems():
    return [pltpu.SemaphoreType.DMA((6,)), pltpu.SemaphoreType.DMA((6,))]


def _gather_halves(x_ref, out_ref, send_sems, recv_sems, phase):
    rh = x_ref.shape[0] // 2
    x, y, c = _place()
    me, sibling = (x, y, c), (x, y, 1 - c)
    chips = [(1 - x, y), (x, 1 - y), (1 - x, 1 - y)]
    mine_src = x_ref.at[pl.ds(c * rh, rh), :]

    def rows(px, py, pc):
        return out_ref.at[pl.ds((4 * px + 2 * py + pc) * rh, rh), :]

    def copy(k, block, to, src=None):
        return pltpu.make_async_remote_copy(
            src_ref=rows(*block) if src is None else src, dst_ref=rows(*block), send_sem=send_sems.at[k],
            recv_sem=recv_sems.at[k], device_id=to, device_id_type=MESH,
        )

    first = [copy(j, me, (*chip, c), src=mine_src) for j, chip in enumerate(chips)]
    if phase == "start":
        for cp in first:
            cp.start()
        return
    passed = [copy(3 + j, (*chip, c), sibling) for j, chip in enumerate(chips)]
    for j, chip in enumerate(chips):
        copy(j, (*chip, c), me).wait_recv()
        passed[j].start()
    for j, chip in enumerate(chips):
        copy(3 + j, (*chip, 1 - c), me).wait_recv()
    for cp in first + passed:
        cp.wait_send()


def _scatter_partials(g_ref, recv_ref, send_sems, recv_sems, phase):
    x, y, c = _place()
    copies = []
    for j, (fx, fy) in enumerate(((1, 0), (0, 1), (1, 1))):
        px, py = _flip(x, fx), _flip(y, fy)
        copies.append(pltpu.make_async_remote_copy(
            src_ref=g_ref.at[2 * px + py], dst_ref=recv_ref.at[j], send_sem=send_sems.at[j], recv_sem=recv_sems.at[j],
            device_id=(px, py, c), device_id_type=MESH,
        ))
    if phase == "start":
        for cp in copies:
            cp.start()
        return
    for cp in copies:
        cp.wait_recv()
    for cp in copies:
        cp.wait_send()


def _scatter_to_all(g_ref, recv_ref, send_sems, recv_sems, phase):
    rh = g_ref.shape[1] // 2
    x, y, c = _place()
    copies = []
    for k in range(1, 8):
        px, py, pc = _flip(x, k >> 2 & 1), _flip(y, k >> 1 & 1), _flip(c, k & 1)
        copies.append(pltpu.make_async_remote_copy(
            src_ref=g_ref.at[2 * px + py, pl.ds(pc * rh, rh), :], dst_ref=recv_ref.at[k - 1], send_sem=send_sems.at[k - 1],
            recv_sem=recv_sems.at[k - 1], device_id=(px, py, pc), device_id_type=MESH,
        ))
    if phase == "start":
        for cp in copies:
            cp.start()
        return
    for cp in copies:
        cp.wait_recv()
    for cp in copies:
        cp.wait_send()


def _scatter_to_chips(g):
    def body(g_ref, recv_ref, send_sems, recv_sems):
        _scatter_partials(g_ref, recv_ref, send_sems, recv_sems, "start")
        _scatter_partials(g_ref, recv_ref, send_sems, recv_sems, "finish")

    return pl.pallas_call(
        body, out_shape=SDS((3,) + g.shape[1:], g.dtype), in_specs=[ANY], out_specs=ANY,
        scratch_shapes=[pltpu.SemaphoreType.DMA((3,)), pltpu.SemaphoreType.DMA((3,))], name="scatter_grads",
    )(g)


def _send_to_sibling(a, half_of_rows):
    rh = a.shape[1] // 2

    def body(a_ref, b_ref, send_sem, recv_sem):
        x, y, c = _place()
        src = a_ref.at[:, pl.ds((1 - c) * rh, rh), :] if half_of_rows else a_ref
        cp = pltpu.make_async_remote_copy(
            src_ref=src, dst_ref=b_ref, send_sem=send_sem, recv_sem=recv_sem, device_id=(x, y, 1 - c), device_id_type=MESH
        )
        cp.start()
        cp.wait()

    shape = (a.shape[0], rh, 128) if half_of_rows else a.shape
    return pl.pallas_call(
        body, out_shape=SDS(shape, a.dtype), in_specs=[ANY], out_specs=ANY,
        scratch_shapes=[pltpu.SemaphoreType.DMA, pltpu.SemaphoreType.DMA],
        name="sibling_halves" if half_of_rows else "sibling_swap",
    )(a)


def _allreduce_small(v):
    r = v.shape[0]

    def body(v_ref, out_ref, buf_ref, send_sems, recv_sems):
        x, y, c = _place()
        buf_ref[0] = v_ref[...]
        copies = []
        for k in range(1, 8):
            peer = (_flip(x, k >> 2 & 1), _flip(y, k >> 1 & 1), _flip(c, k & 1))
            cp = pltpu.make_async_remote_copy(
                src_ref=v_ref, dst_ref=buf_ref.at[k], send_sem=send_sems.at[k - 1], recv_sem=recv_sems.at[k - 1],
                device_id=peer, device_id_type=MESH,
            )
            cp.start()
            copies.append(cp)
        for cp in copies:
            cp.wait_recv()
        acc = None
        for d in range(8):
            slot = 4 * _flip(x, d >> 2 & 1) + 2 * _flip(y, d >> 1 & 1) + _flip(c, d & 1)
            term = buf_ref[slot]
            acc = term if acc is None else acc + term
        out_ref[...] = acc
        for cp in copies:
            cp.wait_send()

    return pl.pallas_call(
        body, out_shape=SDS(v.shape, F32), in_specs=[VMEM], out_specs=VMEM,
        scratch_shapes=[pltpu.VMEM((8, r, 128), F32), pltpu.SemaphoreType.DMA((7,)), pltpu.SemaphoreType.DMA((7,))],
        name="allreduce_small",
    )(v)


_BIG_A = (
    ("w_in", 0, False),
    ("decay_w2_fwd", 1, False), ("decay_w2_bwd", 1, False), ("iclr_a2_fwd", 1, False),
    ("iclr_a2_bwd", 1, False), ("gate_g2", 1, False),
)
_BIG_B = (
    ("w_uq", 0, False), ("w_ukv", 1, False), ("w_out", 0, False), ("w_ffn_up", 1, False), ("ffn_conv_w", 1, True),
    ("w_ffn_down", 0, False),
)
_BIG = _BIG_A + _BIG_B
_SMALL = (
    "ln_mix_g", "shift_mu_prev", "shift_mu_next", "decay_w0_fwd", "decay_w0_bwd", "iclr_a0_fwd", "iclr_a0_bwd", "k_k",
    "k_a", "r_k", "ln_x_g", "ln_x_b", "q_norm_g", "kv_norm_g", "mla_out_g", "ln_ffn_g", "ffn_conv_b", "ln_final_g",
)
_WEIGHTS = (
    "ln_mix_g", "w_in", "shift_mu_prev", "shift_mu_next", "decay_w0_fwd", "decay_w2_fwd", "decay_w0_bwd", "decay_w2_bwd",
    "iclr_a0_fwd", "iclr_a2_fwd", "iclr_a0_bwd", "iclr_a2_bwd", "gate_g2", "k_k", "k_a", "r_k", "ln_x_g", "ln_x_b",
    "q_norm_g", "w_uq", "kv_norm_g", "w_ukv", "mla_out_g", "w_out", "ln_ffn_g", "w_ffn_up", "ffn_conv_w", "ffn_conv_b",
    "w_ffn_down", "ln_final_g",
)


def _pad_rows(flat, rows):
    return jnp.pad(flat, (0, rows * 128 - flat.shape[0])).reshape(rows, 128)


def _rows_for(n, mult):
    rows = -(-n // 128)
    return -(-rows // mult) * mult


def _pack_shards_bf16(arrs, entries):
    parts = []
    for name, _, raw in entries:
        w = arrs[name][0]
        flat = lax.bitcast_convert_type(w, BF16).reshape(-1) if raw else w.astype(BF16).reshape(-1)
        parts.append(_pad_rows(flat, _rows_for(flat.shape[0], 32)))
    return jnp.concatenate(parts, axis=0)


def _unpack_gathered(g4, shard, chip, arrs, entries):
    out, off = {}, 0
    mine = (jnp.arange(N_CHIPS) == chip)[:, None, None]
    for name, axis, raw in entries:
        a, b = arrs[name].shape[1:]
        n = a * b * (2 if raw else 1)
        rows = _rows_for(n, 32)
        seg = jnp.where(mine, shard[None, off:off + rows], g4[:, off:off + rows]).reshape(4, rows * 128)[:, :n]
        off += rows
        if raw:
            seg = lax.bitcast_convert_type(seg.reshape(4, a * b, 2), F32)
        seg = seg.reshape(4, a, b)
        out[name] = jnp.concatenate([seg[s] for s in range(4)], axis=1) if axis == 1 else seg.reshape(4 * a, b)
    return out


def _pack_grads(full, arrs, entries, dtype=F32):
    parts = []
    for name, axis, _ in entries:
        a, b = arrs[name].shape[1:]
        g = full[name]
        if g.ndim == 3:
            sh = g
        else:
            sh = g.reshape(a, 4, b).transpose(1, 0, 2) if axis == 1 else g.reshape(4, a, b)
        rows = _rows_for(a * b, 8)
        flat = sh.reshape(4, a * b).astype(dtype)
        parts.append(jnp.pad(flat, ((0, 0), (0, rows * 128 - a * b))).reshape(4, rows, 128))
    total = sum(q.shape[1] for q in parts)
    parts.append(jnp.zeros((4, -(-total // 1024) * 1024 - total, 128), dtype))
    return jnp.concatenate(parts, axis=1)


def _unpack_grads(g, arrs, entries):
    out, off = {}, 0
    for name, _, _ in entries:
        a, b = arrs[name].shape[1:]
        rows = _rows_for(a * b, 8)
        out[name] = g[off:off + rows].reshape(-1)[:a * b].reshape(1, a, b)
        off += rows
    return out


def _pack_small(vals):
    flat = jnp.concatenate([vals[n].reshape(-1).astype(F32) for n in _SMALL] + [vals["_loss"].reshape(-1)])
    return _pad_rows(flat, _rows_for(flat.shape[0], 8))


def _unpack_small(buf, arrs):
    flat, out, off = buf.reshape(-1), {}, 0
    for n in _SMALL:
        size = arrs[n].size
        out[n] = flat[off:off + size].reshape(arrs[n].shape)
        off += size
    out["_loss"] = flat[off]
    return out


def _rot_cols(w):
    return jnp.concatenate([-w[..., 16:], w[..., :16]], axis=-1)


def _rot_cols_t(g):
    return jnp.concatenate([g[..., 16:], -g[..., :16]], axis=-1)


def _rope_tables(t):
    inv = jnp.power(ROPE_THETA, -jnp.arange(0, ROPE_DIM, 2, dtype=F32) / ROPE_DIM)
    ang = jnp.arange(t, dtype=F32)[:, None] * inv[None, :]
    one, zero = jnp.ones((t, 64), F32), jnp.zeros((t, 64), F32)
    cs = jnp.concatenate([one, jnp.cos(ang), jnp.cos(ang), zero[:, :32]], axis=1)
    sn = jnp.concatenate([zero, jnp.sin(ang), jnp.sin(ang), zero[:, :32]], axis=1)
    return cs, sn


def _block_diag(a, b):
    za = jnp.zeros_like(a)
    return jnp.concatenate([jnp.concatenate([a, za], axis=1), jnp.concatenate([za, b], axis=1)], axis=0)


def kernel(x, ln_mix_g, w_in, shift_mu_prev, shift_mu_next, decay_w0_fwd, decay_w2_fwd, decay_w0_bwd, decay_w2_bwd, iclr_a0_fwd, iclr_a2_fwd, iclr_a0_bwd, iclr_a2_bwd, gate_g2, k_k, k_a, r_k, ln_x_g, ln_x_b, q_norm_g, w_uq, kv_norm_g, w_ukv, mla_out_g, w_out, ln_ffn_g, w_ffn_up, ffn_conv_w, ffn_conv_b, w_ffn_down, ln_final_g, loss_target, m_ln_mix_g, m_w_in, m_shift_mu_prev, m_shift_mu_next, m_decay_w0_fwd, m_decay_w2_fwd, m_decay_w0_bwd, m_decay_w2_bwd, m_iclr_a0_fwd, m_iclr_a2_fwd, m_iclr_a0_bwd, m_iclr_a2_bwd, m_gate_g2, m_k_k, m_k_a, m_r_k, m_ln_x_g, m_ln_x_b, m_q_norm_g, m_w_uq, m_kv_norm_g, m_w_ukv, m_mla_out_g, m_w_out, m_ln_ffn_g, m_w_ffn_up, m_ffn_conv_w, m_ffn_conv_b, m_w_ffn_down, m_ln_final_g, v_ln_mix_g, v_w_in, v_shift_mu_prev, v_shift_mu_next, v_decay_w0_fwd, v_decay_w2_fwd, v_decay_w0_bwd, v_decay_w2_bwd, v_iclr_a0_fwd, v_iclr_a2_fwd, v_iclr_a0_bwd, v_iclr_a2_bwd, v_gate_g2, v_k_k, v_k_a, v_r_k, v_ln_x_g, v_ln_x_b, v_q_norm_g, v_w_uq, v_kv_norm_g, v_w_ukv, v_mla_out_g, v_w_out, v_ln_ffn_g, v_w_ffn_up, v_ffn_conv_w, v_ffn_conv_b, v_w_ffn_down, v_ln_final_g):
    arrs = dict(locals())
    for pre in ("", "m_", "v_"):
        arrs[pre + "w_in"] = jnp.swapaxes(arrs[pre + "w_in"], 1, 2)
    b, t, d = x.shape
    m = b * t
    x2 = x.reshape(m, d)
    tgt = loss_target.reshape(m, d)
    vec = lambda n: arrs[n].reshape(1, -1)

    core = lax.axis_index("c")
    chip = 2 * lax.axis_index("x") + lax.axis_index("y")
    def unpack(gathered, shard, entries):
        return _unpack_gathered(gathered.reshape(N_CHIPS, -1, 128), shard, chip, arrs, entries)

    shard_a, shard_b = _pack_shards_bf16(arrs, _BIG_A), _pack_shards_bf16(arrs, _BIG_B)
    fw = unpack(_allgather_weights(shard_a), shard_a, _BIG_A)
    win = fw["w_in"]
    zc = jnp.zeros((64, d), BF16)
    w_kr = win[2944:2976]
    rot_kr = jnp.swapaxes(_rot_cols(jnp.swapaxes(w_kr, 0, 1)), 0, 1)
    win_m = jnp.concatenate([win[1920:2944], zc, w_kr, zc[:32], zc, rot_kr, zc[:32]], axis=0)
    win_r = win[:RWKV_COLS]
    head = jnp.arange(512) // HEAD_DIM
    rw = dict(
        w0=jnp.concatenate([vec("decay_w0_fwd"), vec("decay_w0_bwd")], axis=1),
        w2=_block_diag(fw["decay_w2_fwd"], fw["decay_w2_bwd"]).astype(F32),
        a0=jnp.concatenate([vec("iclr_a0_fwd"), vec("iclr_a0_bwd")], axis=1),
        a2=_block_diag(fw["iclr_a2_fwd"], fw["iclr_a2_bwd"]).astype(F32),
        g2=fw["gate_g2"].astype(F32), k_k=vec("k_k"), k_a=vec("k_a"), r_k=vec("r_k"), ln_x_g=vec("ln_x_g"), ln_x_b=vec("ln_x_b"),
        ones_bd=(head[:, None] == head[None, :]).astype(F32),
    )
    cs, sn = _rope_tables(t)

    n1 = _rms_fwd(x2, vec("ln_mix_g"), "rms_mix")
    zm = _mm(n1, win_m, "nt", "proj_in_mla")
    zr = _mm(n1, win_r, "nt", "proj_in_rwkv")
    zs = _shift_fwd(zr.reshape(b, t, RWKV_COLS), vec("shift_mu_prev"), vec("shift_mu_next"))
    zs2 = zs.reshape(m, RWKV_COLS)
    wf, wb, kf, kb, kk, kaf, kab, gate = _prep_fwd(zs2, rw)
    r4 = lambda a: a.reshape(b, t, 512)
    f2 = lambda a: a.reshape(m, 512)
    kk4 = r4(kk)
    ops_f = (r4(wf), r4(kf), r4(kaf))
    ops_b = (r4(wb), r4(kb), r4(kab))
    y_f, hist_f, y_b, hist_b, s_last, gathered_b = _scan_fwd(zs, kk4, ops_f, ops_b, shard_b)
    fw.update(unpack(gathered_b, shard_b, _BIG_B))
    uq = fw["w_uq"].astype(F32).reshape(Q_RANK, HEADS, 96)
    z32 = jnp.zeros((Q_RANK, HEADS, 32), F32)
    wq = jnp.concatenate([uq[..., :64], uq[..., 64:], z32], axis=-1).reshape(Q_RANK, 1024)
    wqr = jnp.concatenate([z32, z32, _rot_cols(uq[..., 64:]), z32], axis=-1).reshape(Q_RANK, 1024)
    ukv = fw["w_ukv"].astype(F32).reshape(KV_RANK, HEADS, 128)
    wk = jnp.concatenate([ukv[..., :64], jnp.zeros_like(ukv[..., :64])], axis=-1).reshape(KV_RANK, 1024)
    wv = ukv[..., 64:].reshape(KV_RANK, 512)
    mp = dict(q_norm_g=vec("q_norm_g"), kv_norm_g=vec("kv_norm_g"), wq=wq, wqr=wqr, wk=wk, wv=wv)
    w_up_g, w_up_v = fw["w_ffn_up"][:, :D_FF], fw["w_ffn_up"][:, D_FF:]
    cw, cb = fw["ffn_conv_w"], vec("ffn_conv_b")
    y_f, y_b = f2(y_f), f2(y_b)
    y_rwkv = _post_fwd(y_f, y_b, zs2, kf, kb, gate, rw)
    q, kfull, v = _mla_fwd(zm, cs, sn, mp, t)
    o, lse = _attn_fwd(q, kfull, v, b, t)
    y_mla = _rms_fwd(o, vec("mla_out_g"), "rms_mla_out")
    ymix = jnp.concatenate([y_rwkv, y_mla], axis=1)
    h1 = _mm(ymix, fw["w_out"], "nn", "proj_out", add=x2)
    n2 = _rms_fwd(h1, vec("ln_ffn_g"), "rms_ffn")
    ug = _mm(n2, w_up_g, "nn", "ffn_up_gate")
    uv = _mm(n2, w_up_v, "nn", "ffn_up_val")
    r3f = lambda a: a.reshape(b, t, D_FF)
    act = _ffn_mid_fwd(r3f(ug), r3f(uv), cw, cb).reshape(m, D_FF)
    h2 = _mm(act, fw["w_ffn_down"], "nn", "ffn_down", add=h1)
    loss_tab, dh2, g_ln_final = _final(h2, vec("ln_final_g"), tgt)

    gfull = {}
    dact = _mm(dh2, fw["w_ffn_down"], "nt", "d_ffn_act")
    gfull["w_ffn_down"] = _mm(act, dh2, "tn", "g_ffn_down")
    dug, duv, tab_g, tab_v = _ffn_mid_bwd(r3f(ug), r3f(uv), cw, cb, r3f(dact))
    dug, duv = dug.reshape(m, D_FF), duv.reshape(m, D_FF)
    gfull["ffn_conv_w"] = jnp.concatenate([tab_g[0:3], tab_v[0:3]], axis=1)
    g_conv_b = jnp.concatenate([tab_g[3:4], tab_v[3:4]], axis=1)
    dn2 = _mm(duv, w_up_v, "nt", "d_ffn_in_val", add=_mm(dug, w_up_g, "nt", "d_ffn_in_gate"))
    shard_cols = arrs["w_ffn_up"].shape[2]
    gfull["w_ffn_up"] = jnp.concatenate([_mm(n2, dug, "tn", "g_ffn_up_gate", column_blocks=shard_cols),
                                         _mm(n2, duv, "tn", "g_ffn_up_val", column_blocks=shard_cols)], axis=0)
    dh1, g_ln_ffn = _rms_bwd(h1, vec("ln_ffn_g"), dn2, "rms_ffn_bwd", dres=dh2)
    dymix = _mm(dh1, fw["w_out"], "nt", "d_mix")
    gfull["w_out"] = _mm(ymix, dh1, "tn", "g_w_out")
    do, g_mla_out = _rms_bwd(o, vec("mla_out_g"), dymix, "rms_mla_out_bwd", dy_block=1)
    dq, dk, dv = _attn_bwd(q, kfull, v, o, lse, do, b, t)
    dzm, g_qn, g_kvn, g_wq, g_wqr, g_wk, g_wv = _mla_bwd(zm, cs, sn, mp, t, dq, dk, dv)
    gq3, gqr3 = g_wq.reshape(Q_RANK, HEADS, 128), g_wqr.reshape(Q_RANK, HEADS, 128)
    gfull["w_uq"] = jnp.concatenate(
        [gq3[..., :64], gq3[..., 64:96] + _rot_cols_t(gqr3[..., 64:96])], axis=-1
    ).reshape(Q_RANK, HEADS * 96)
    gfull["w_ukv"] = jnp.concatenate(
        [g_wk.reshape(KV_RANK, HEADS, 128)[..., :64], g_wv.reshape(KV_RANK, HEADS, 64)], axis=-1
    ).reshape(KV_RANK, 1024)
    def cores_first(entries, tag):
        packed = _pack_grads(gfull, arrs, entries)
        rh = packed.shape[1] // 2
        own = lax.dynamic_slice_in_dim(packed, core * rh, rh, axis=1)
        sib = _send_to_sibling(packed, True)
        return _add_rows([own.reshape(4 * rh, 128), sib.reshape(4 * rh, 128)], "sum_cores_" + tag, BF16).reshape(4, rh, 128)

    def join_halves(half, entries):
        other = _send_to_sibling(half, False)
        lower = jnp.where(core == 0, half, other)
        upper = jnp.where(core == 0, other, half)
        return _unpack_grads(jnp.concatenate([lower, upper], axis=0), arrs, entries)

    part_b = _pack_grads(gfull, arrs, _BIG_B, BF16)
    dys, dr_p, dk_p, dv_p, dgate, g_rk, g_lnx_g, g_lnx_b = _post_bwd(y_f, y_b, zs2, kf, kb, gate, rw, dymix)
    (dr_f, dwf, dkf, dkk_f, dkaf, dv_f, dr_b, dwb, dkb, dkk_b, dkab, dv_b, recv_b) = _scan_bwd(
        zs, kk4, r4(dys), ops_f, hist_f, ops_b, hist_b, s_last, part_b)
    rh_b = part_b.shape[1] // 2
    mine_b = lax.dynamic_slice(part_b, (chip, core * rh_b, 0), (1, rh_b, 128))[0]
    g_big = join_halves(_add_rows([mine_b] + [recv_b[k] for k in range(7)], "sum_devices_b"), _BIG_B)
    cts = dict(dwf=f2(dwf), dwb=f2(dwb), dkf=f2(dkf), dkb=f2(dkb), dkk_f=f2(dkk_f), dkk_b=f2(dkk_b), dkaf=f2(dkaf), dkab=f2(dkab),
               dr_f=f2(dr_f), dr_b=f2(dr_b), dr_p=dr_p, dk_p=dk_p, dv_p=dv_p, dg=dgate, dv_f=f2(dv_f), dv_b=f2(dv_b))
    dzs, g_w0, g_w2, g_a0, g_a2, g_g2, g_kk, g_ka = _prep_bwd(zs2, rw, cts)
    dzr, g_mu_p, g_mu_n = _shift_bwd(dzs.reshape(b, t, RWKV_COLS), zr.reshape(b, t, RWKV_COLS), vec("shift_mu_prev"), vec("shift_mu_next"))
    dzr = dzr.reshape(m, RWKV_COLS)
    gfull["decay_w2_fwd"], gfull["decay_w2_bwd"] = g_w2[:64, :512], g_w2[64:, 512:]
    gfull["iclr_a2_fwd"], gfull["iclr_a2_bwd"] = g_a2[:64, :512], g_a2[64:, 512:]
    gfull["gate_g2"] = g_g2
    dn1 = _mm(dzr, win_r, "nn", "d_proj_in_rwkv", add=_mm(dzm, win_m, "nn", "d_proj_in_mla"))
    g_m = _mm(dzm, n1, "tn", "g_w_in_mla")
    g_r = _mm(dzr, n1, "tn", "g_w_in_rwkv")
    g_kr = g_m[1088:1120] + jnp.swapaxes(_rot_cols_t(jnp.swapaxes(g_m[1216:1248], 0, 1)), 0, 1)
    gfull["w_in"] = jnp.concatenate([g_r, g_m[:1024], g_kr], axis=0)
    dx, g_ln_mix = _rms_bwd(x2, vec("ln_mix_g"), dn1, "rms_mix_bwd", dres=dh1)

    part_a = cores_first(_BIG_A, "a")
    recv_a = _scatter_to_chips(part_a)
    mine_a = lax.dynamic_index_in_dim(part_a, chip, axis=0, keepdims=False)
    g_big.update(join_halves(_add_rows([mine_a, recv_a[0], recv_a[1], recv_a[2]], "sum_chips_a"), _BIG_A))
    small = {
        "ln_mix_g": g_ln_mix, "shift_mu_prev": g_mu_p, "shift_mu_next": g_mu_n, "decay_w0_fwd": g_w0[:, :512],
        "decay_w0_bwd": g_w0[:, 512:], "iclr_a0_fwd": g_a0[:, :512], "iclr_a0_bwd": g_a0[:, 512:], "k_k": g_kk, "k_a": g_ka,
        "r_k": g_rk, "ln_x_g": g_lnx_g, "ln_x_b": g_lnx_b, "q_norm_g": g_qn, "kv_norm_g": g_kvn, "mla_out_g": g_mla_out,
        "ln_ffn_g": g_ln_ffn, "ffn_conv_b": g_conv_b, "ln_final_g": g_ln_final,
        "_loss": jnp.pad(loss_tab[0, 0:1], (0, 127)),
    }
    g_small_buf = _allreduce_small(_pack_small(small))
    g_small = _unpack_small(g_small_buf, arrs)

    grads, deltas, new_m, new_v = {}, {}, {}, {}
    for name, _, _ in _BIG:
        grads[name] = g_big[name]
        deltas[name], new_m[name], new_v[name] = _adamw(
            arrs[name], g_big[name], arrs["m_" + name], arrs["v_" + name], "adamw_" + name)
    pk = lambda pre: _pack_small({**{n: arrs[pre + n] for n in _SMALL}, "_loss": jnp.zeros((128,), F32)})
    sd, sm, sv = _adamw(pk(""), g_small_buf, pk("m_"), pk("v_"), "adamw_small")
    sd, sm, sv = _unpack_small(sd, arrs), _unpack_small(sm, arrs), _unpack_small(sv, arrs)
    for n in _SMALL:
        grads[n], deltas[n], new_m[n], new_v[n] = g_small[n], sd[n], sm[n], sv[n]
    for group in (grads, deltas, new_m, new_v):
        group["w_in"] = jnp.swapaxes(group["w_in"], 1, 2)

    return (g_small["_loss"], dx.reshape(b, t, d), *[grads[n] for n in _WEIGHTS], *[deltas[n] for n in _WEIGHTS],
            *[new_m[n] for n in _WEIGHTS], *[new_v[n] for n in _WEIGHTS])
```

```python
import functools
import math

import jax
import jax.numpy as jnp
from jax import lax
from jax.experimental import pallas as pl
from jax.experimental.pallas import tpu as pltpu

F32, BF16 = jnp.float32, jnp.bfloat16
MESH = pl.DeviceIdType.MESH
ANY = pl.BlockSpec(memory_space=pl.ANY)
VMEM = pl.BlockSpec(memory_space=pltpu.VMEM)
BS = pl.BlockSpec
SDS = jax.ShapeDtypeStruct

NORM_EPS = 1e-6
GN_EPS = 64e-5
L2_EPS = 1e-12
HEADS = 8
HEAD_DIM = 64
ROPE_DIM = 32
ROPE_THETA = 10000.0
MLA_SCALE = (64 + ROPE_DIM) ** -0.5
Q_RANK, KV_RANK = 768, 256
RWKV_COLS = 1920
MLA_PAD_COLS = Q_RANK + KV_RANK + 256
D_FF = 2816
ADAM_LR, ADAM_B1, ADAM_B2, ADAM_EPS, ADAM_WD, ADAM_STEP = 0.001, 0.9, 0.999, 1e-08, 0.01, 10

V7X_VMEM_LIMIT = 56 * 1024 * 1024
SCAN_CHUNK = 32
N_CHIPS = 4


def _cp(*sem):
    return pltpu.CompilerParams(dimension_semantics=sem, vmem_limit_bytes=V7X_VMEM_LIMIT)


def _tile(n, cands=(512, 640, 384, 256, 128)):
    for c in cands:
        if n % c == 0:
            return c
    return n


def _row_tile(n, cap=256):
    best = n
    for t in range(8, cap + 1, 8):
        if n % t == 0:
            best = t
    return best if best <= cap or n <= cap else n


def _rms(x, g):
    ms = jnp.mean(x * x, axis=-1, keepdims=True)
    return x * lax.rsqrt(ms + NORM_EPS) * g


@jax.custom_vjp
def _bdot(x, w):
    return jnp.dot(x.astype(BF16), w.astype(BF16), preferred_element_type=F32)


def _bdot_fwd(x, w):
    return _bdot(x, w), (x, w)


def _bdot_bwd(res, ct):
    x, w = res
    c = ct.astype(BF16)
    dx = lax.dot_general(c, w.astype(BF16), (((1,), (1,)), ((), ())), preferred_element_type=F32)
    dw = lax.dot_general(x.astype(BF16), c, (((0,), (0,)), ((), ())), preferred_element_type=F32)
    return dx.astype(x.dtype), dw.astype(w.dtype)


_bdot.defvjp(_bdot_fwd, _bdot_bwd)


@jax.custom_vjp
def _headsum(x, ones_bd):
    hi = x.astype(BF16)
    mid = (x - hi.astype(F32)).astype(BF16)
    ob = ones_bd.astype(BF16)
    return jnp.dot(hi, ob, preferred_element_type=F32) + jnp.dot(mid, ob, preferred_element_type=F32)


def _headsum_fwd(x, ones_bd):
    return _headsum(x, ones_bd), ones_bd


def _headsum_bwd(ones_bd, ct):
    return _headsum(ct, ones_bd), jnp.zeros_like(ones_bd)


_headsum.defvjp(_headsum_fwd, _headsum_bwd)


def _prep_fn(zs, w0, w2, a0, a2, g2, k_k, k_a, ones_bd):
    k = zs[:, 512:1024]
    wd = zs[:, 1536:1664]
    ad = zs[:, 1664:1792]
    gd = zs[:, 1792:1920]
    logit = w0 + _bdot(jnp.tanh(wd), w2)
    w = jnp.exp(-math.exp(-0.5) * jax.nn.sigmoid(logit))
    a = jax.nn.sigmoid(a0 + _bdot(ad, a2))
    g = _bdot(jax.nn.sigmoid(gd), g2)
    kkr = k * k_k
    nrm = jnp.sqrt(_headsum(kkr * kkr, ones_bd))
    kk = kkr / jnp.maximum(nrm, L2_EPS)
    a_f, a_b = a[:, :512], a[:, 512:]
    kf = k * (1.0 + (a_f - 1.0) * k_a)
    kb = k * (1.0 + (a_b - 1.0) * k_a)
    return w[:, :512], w[:, 512:], kf, kb, kk, kk * a_f, kk * a_b, g


def _post_fn(y, r, kf, kb, v, g, r_k, ln_g, ln_b, ones_bd):
    mu =_headsum(y, ones_bd) * (1.0 / HEAD_DIM)
    yc = y - mu
    var = _headsum(yc * yc, ones_bd) * (1.0 / HEAD_DIM)
    yn = yc * lax.rsqrt(var + GN_EPS) * ln_g + ln_b
    bonus = _headsum(r * (kf + kb) * r_k, ones_bd) * v
    return (yn + bonus) * g


def _cat8(x):
    return jnp.concatenate([x] * HEADS, axis=1)


def _mla_fn(zm, cs, sn, gq, gkv, wq, wqr, wk, wv):
    cq = zm[:, :Q_RANK]
    ckv = zm[:, Q_RANK:Q_RANK + KV_RANK]
    kr = zm[:, Q_RANK + KV_RANK:Q_RANK + KV_RANK + 128]
    krr = zm[:, Q_RANK + KV_RANK + 128:]
    cqn = _rms(cq, gq)
    ckvn = _rms(ckv, gkv)
    q = (_bdot(cqn, wq) * _cat8(cs) + _bdot(cqn, wqr) * _cat8(sn)) * MLA_SCALE
    kro = kr * cs + krr * sn
    kfull = _bdot(ckvn, wk) + _cat8(kro)
    v = _bdot(ckvn, wv)
    return q, kfull, v


def _adamw_math(w, g, m, v):
    m2 = ADAM_B1 * m + (1.0 - ADAM_B1) * g
    v2 = ADAM_B2 * v + (1.0 - ADAM_B2) * (g * g)
    m_hat = m2 / (1.0 - ADAM_B1 ** ADAM_STEP)
    v_hat = v2 / (1.0 - ADAM_B2 ** ADAM_STEP)
    delta = -ADAM_LR * (m_hat / (jnp.sqrt(v_hat) + ADAM_EPS) + ADAM_WD * w)
    return delta, m2, v2


_DIMS = {"nn": (((1,), (0,)), ((), ())), "nt": (((1,), (1,)), ((), ())), "tn": (((0,), (0,)), ((), ()))}


def _mm(a, b, mode, name, out_dtype=F32, add=None, column_blocks=None):
    if mode == "nn":
        (m, k), (_, n) = a.shape, b.shape
    elif mode == "nt":
        (m, k), (n, _) = a.shape, b.shape
    else:
        (k, m), (_, n) = a.shape, b.shape
    big = (1024, 1408, 768, 640, 512, 384, 256, 128)
    tm, tn, tk = _tile(m, big), column_blocks or _tile(n, big), _tile(k, (2816, 2048, 1024, 1408, 640, 512, 384, 256, 128))
    nk = k // tk

    def body(a_ref, b_ref, *rest):
        if add is None:
            o_ref, acc_ref = rest
        else:
            add_ref, o_ref, acc_ref = rest
        kk = pl.program_id(2)
        part = lax.dot_general(a_ref[...].astype(BF16), b_ref[...].astype(BF16), _DIMS[mode], preferred_element_type=F32)

        def finish(r):
            if add is not None:
                r = r + add_ref[...]
            o_ref[...] = r.astype(out_dtype).reshape(o_ref.shape)

        if nk == 1:
            finish(part)
            return

        @pl.when(kk == 0)
        def _():
            acc_ref[...] = part

        @pl.when(kk > 0)
        def _():
            acc_ref[...] += part

        @pl.when(kk == nk - 1)
        def _():
            finish(acc_ref[...])

    a_spec = BS((tk, tm), lambda i, j, kk: (kk, i)) if mode == "tn" else BS((tm, tk), lambda i, j, kk: (i, kk))
    b_spec = BS((tn, tk), lambda i, j, kk: (j, kk)) if mode == "nt" else BS((tk, tn), lambda i, j, kk: (kk, j))
    o_spec = BS((tm, tn), lambda i, j, kk: (i, j))
    ins, specs = [a, b], [a_spec, b_spec]
    if add is not None:
        ins.append(add)
        specs.append(o_spec)
    out_shape = SDS((m, n), out_dtype)
    if column_blocks:
        assert add is None
        o_spec, out_shape = BS((1, tm, tn), lambda i, j, kk: (j, i, 0)), SDS((n // tn, m, tn), out_dtype)
    return pl.pallas_call(
        body, grid=(m // tm, n // tn, nk), in_specs=specs, out_specs=o_spec, out_shape=out_shape,
        scratch_shapes=[pltpu.VMEM((tm, tn), F32)], compiler_params=_cp("parallel", "parallel", "arbitrary"), name=name,
    )(*ins)


def _rms_fwd(x, g, name):
    m, d = x.shape
    tr = _tile(m)

    def body(x_ref, g_ref, o_ref):
        o_ref[...] = _rms(x_ref[...], g_ref[...]).astype(BF16)

    return pl.pallas_call(
        body, grid=(m // tr,), in_specs=[BS((tr, d), lambda i: (i, 0)), BS((1, d), lambda i: (0, 0))],
        out_specs=BS((tr, d), lambda i: (i, 0)), out_shape=SDS((m, d), BF16), compiler_params=_cp("parallel"), name=name,
    )(x, g)


def _rms_bwd(x, g, dy, name, dres=None, dy_block=0):
    m, d = x.shape
    tr = _row_tile(m, 512)

    def body(x_ref, g_ref, dy_ref, *rest):
        if dres is None:
            dx_ref, dg_ref = rest
        else:
            dres_ref, dx_ref, dg_ref = rest
        _, vjp = jax.vjp(_rms, x_ref[...], g_ref[...])
        dx, dg = vjp(dy_ref[...])
        if dres is not None:
            dx = dx + dres_ref[...]
        dx_ref[...] = dx

        @pl.when(pl.program_id(0) == 0)
        def _():
            dg_ref[...] = jnp.zeros_like(dg_ref)

        dg_ref[...] += dg

    row = BS((tr, d), lambda i: (i, 0))
    vec = BS((1, d), lambda i: (0, 0))
    ins, specs = [x, g, dy], [row, vec, BS((tr, d), lambda i: (i, dy_block))]
    if dres is not None:
        ins.append(dres)
        specs.append(row)
    return pl.pallas_call(
        body, grid=(m // tr,), in_specs=specs, out_specs=[row, vec], out_shape=[SDS((m, d), F32), SDS((1, d), F32)],
        compiler_params=_cp("arbitrary"), name=name,
    )(*ins)


def _final(h, g, tgt):
    m, d = h.shape
    tr = _row_tile(m, 512)

    def loss_fn(hh, gg, tt):
        e = _rms(hh, gg) - tt
        return 0.5 * jnp.sum(e * e) * (1.0 / d)

    def body(h_ref, g_ref, t_ref, l_ref, dh_ref, dg_ref):
        val, (dh, dg) = jax.value_and_grad(loss_fn, argnums=(0, 1))(h_ref[...], g_ref[...], t_ref[...])
        dh_ref[...] = dh

        @pl.when(pl.program_id(0) == 0)
        def _():
            dg_ref[...] = jnp.zeros_like(dg_ref)
            l_ref[...] = jnp.zeros_like(l_ref)

        dg_ref[...] += dg
        l_ref[...] += jnp.full(l_ref.shape, val, F32)

    row = BS((tr, d), lambda i: (i, 0))
    vec = BS((1, d), lambda i: (0, 0))
    return pl.pallas_call(
        body, grid=(m // tr,), in_specs=[row, vec, row], out_specs=[BS((8, 128), lambda i: (0, 0)), row, vec],
        out_shape=[SDS((8, 128), F32), SDS((m, d), F32), SDS((1, d), F32)], compiler_params=_cp("arbitrary"), name="final_loss",
    )(h, g, tgt)


def _prev_next(z, t):
    row = lax.broadcasted_iota(jnp.int32, z.shape, 0)
    zp = jnp.where(row == 0, 0.0, pltpu.roll(z, 1, axis=0))
    zn = jnp.where(row == t - 1, 0.0, pltpu.roll(z, t - 1, axis=0))
    return zp, zn


def _shift_fwd(z3, mu_p, mu_n):
    b, t, c = z3.shape
    nc = c // 128

    def body(z_ref, mp_ref, mn_ref, o_ref):
        z = z_ref[0]
        zp, zn = _prev_next(z, t)
        o_ref[0] = z + mp_ref[...] * (zp - z) + mn_ref[...] * (zn - z)

    blk = BS((1, t, 128), lambda i, j: (i, 0, j))
    vec = BS((1, 128), lambda i, j: (0, j))
    return pl.pallas_call(
        body, grid=(b, nc), in_specs=[blk, vec, vec], out_specs=blk, out_shape=SDS((b, t, c), F32),
        compiler_params=_cp("parallel", "parallel"), name="shift_fwd",
    )(z3, mu_p, mu_n)


def _shift_bwd(dzs3, z3, mu_p, mu_n):
    b, t, c = z3.shape
    nc = c // 128

    def body(d_ref, z_ref, mp_ref, mn_ref, dz_ref, dmp_ref, dmn_ref):
        d, z = d_ref[0], z_ref[0]
        mp, mn = mp_ref[...], mn_ref[...]
        zp, zn = _prev_next(z, t)
        _, dp_next = _prev_next(d * mp, t)
        dn_prev, _ = _prev_next(d * mn, t)
        dz_ref[0] = (d * (1.0 - mp - mn) + dp_next + dn_prev).astype(BF16)

        @pl.when(pl.program_id(1) == 0)
        def _():
            dmp_ref[...] = jnp.zeros_like(dmp_ref)
            dmn_ref[...] = jnp.zeros_like(dmn_ref)

        dmp_ref[...] += jnp.sum(d * (zp - z), axis=0, keepdims=True)
        dmn_ref[...] += jnp.sum(d * (zn - z), axis=0, keepdims=True)

    blk = BS((1, t, 128), lambda j, i: (i, 0, j))
    vec = BS((1, 128), lambda j, i: (0, j))
    return pl.pallas_call(
        body, grid=(nc, b), in_specs=[blk, blk, vec, vec], out_specs=[blk, vec, vec],
        out_shape=[SDS((b, t, c), BF16), SDS((1, c), F32), SDS((1, c), F32)],
        compiler_params=_cp("parallel", "arbitrary"), name="shift_bwd",
    )(dzs3, z3, mu_p, mu_n)


def _const(shape):
    nd = len(shape)
    return BS(shape, lambda i: (0,) * nd)


def _prep_fwd(zs, p):
    m = zs.shape[0]
    tr = 512
    params = [p["w0"], p["w2"], p["a0"], p["a2"], p["g2"], p["k_k"], p["k_a"], p["ones_bd"]]

    def body(zs_ref, w0, w2, a0, a2, g2, kk_, ka_, bd, wf, wb, kf, kb, kk, kaf, kab, g):
        outs = _prep_fn(zs_ref[...], w0[...], w2[...], a0[...], a2[...], g2[...], kk_[...], ka_[...], bd[...])
        for ref, val in zip((wf, wb, kf, kb, kk, kaf, kab, g), outs):
            ref[...] = val

    row = BS((tr, 512), lambda i: (i, 0))
    return pl.pallas_call(
        body, grid=(m // tr,), in_specs=[BS((tr, RWKV_COLS), lambda i: (i, 0))] + [_const(q.shape) for q in params],
        out_specs=[row] * 8, out_shape=[SDS((m, 512), F32)] * 8, compiler_params=_cp("parallel"), name="rwkv_prep_fwd",
    )(zs, *params)


def _prep_bwd(zs, p, ct_rows):
    m = zs.shape[0]
    tr = 256
    params = [p["w0"], p["w2"], p["a0"], p["a2"], p["g2"], p["k_k"], p["k_a"]]
    names = ["dwf", "dwb", "dkf", "dkb", "dkk_f", "dkk_b", "dkaf", "dkab", "dr_f", "dr_b", "dr_p", "dk_p", "dv_p", "dg",
             "dv_f", "dv_b"]
    rows = [ct_rows[n] for n in names]

    def body(zs_ref, w0, w2, a0, a2, g2, kk_, ka_, bd, *rest):
        c = {n: r[...] for n, r in zip(names, rest[:len(names)])}
        outs = rest[len(names):]
        dzs_ref, grads = outs[0], outs[1:]
        ones_bd = bd[...]
        _, vjp = jax.vjp(
            lambda *q: _prep_fn(*q, ones_bd), zs_ref[...], w0[...], w2[...], a0[...], a2[...], g2[...], kk_[...], ka_[...]
        )
        cts = (c["dwf"], c["dwb"], c["dkf"] + c["dk_p"], c["dkb"] + c["dk_p"], c["dkk_f"] + c["dkk_b"], c["dkaf"], c["dkab"], c["dg"])
        dzs, *dparams = vjp(cts)
        dr = c["dr_f"] + c["dr_b"] + c["dr_p"]
        dv = c["dv_f"] + c["dv_b"] + c["dv_p"]
        dzs_ref[:, 0:512] = dzs[:, 0:512] + dr
        dzs_ref[:, 512:1024] = dzs[:, 512:1024]
        dzs_ref[:, 1024:1536] = dzs[:, 1024:1536] + dv
        dzs_ref[:, 1536:1920] = dzs[:, 1536:1920]

        @pl.when(pl.program_id(0) == 0)
        def _():
            for gr in grads:
                gr[...] = jnp.zeros_like(gr)

        for gr, val in zip(grads, dparams):
            gr[...] += val

    row = BS((tr, 512), lambda i: (i, 0))
    return pl.pallas_call(
        body, grid=(m // tr,),
        in_specs=[BS((tr, RWKV_COLS), lambda i: (i, 0))] + [_const(q.shape) for q in params] + [_const(p["ones_bd"].shape)]
        + [row] * len(names),
        out_specs=[BS((tr, RWKV_COLS), lambda i: (i, 0))] + [_const(q.shape) for q in params],
        out_shape=[SDS((m, RWKV_COLS), F32)] + [SDS(q.shape, F32) for q in params],
        compiler_params=_cp("arbitrary"), name="rwkv_prep_bwd",
    )(zs, *params, p["ones_bd"], *rows)


def _post_specs(tr):
    r = BS((tr, 512), lambda i: (i, 0))
    v = BS((tr, 512), lambda i: (i, 2))
    row = BS((tr, 512), lambda i: (i, 0))
    return r, v, row


def _post_fwd(y_f, y_b, zs, kf, kb, g, p):
    m = zs.shape[0]
    tr = 512
    r, v, row = _post_specs(tr)
    vecs = [p["r_k"], p["ln_x_g"], p["ln_x_b"], p["ones_bd"]]

    def body(yf, yb, r_ref, v_ref, kf_ref, kb_ref, g_ref, rk, lg, lb, bd, o_ref):
        o_ref[...] = _post_fn(
            yf[...] + yb[...], r_ref[...], kf_ref[...], kb_ref[...], v_ref[...], g_ref[...], rk[...], lg[...], lb[...], bd[...]
        ).astype(BF16)

    return pl.pallas_call(
        body, grid=(m // tr,), in_specs=[row, row, r, v, row, row, row] + [_const(q.shape) for q in vecs],
        out_specs=row, out_shape=SDS((m, 512), BF16), compiler_params=_cp("parallel"), name="rwkv_post_fwd",
    )(y_f, y_b, zs, zs, kf, kb, g, *vecs)


def _post_bwd(y_f, y_b, zs, kf, kb, g, p, dymix):
    m = zs.shape[0]
    tr = 256
    r, v, row = _post_specs(tr)
    vecs = [p["r_k"], p["ln_x_g"], p["ln_x_b"]]

    def body(yf, yb, r_ref, v_ref, kf_ref, kb_ref, g_ref, rk, lg, lb, bd, dy_ref, dyo, dr, dk, dv, dg, drk, dlg, dlb):
        ones_bd = bd[...]
        _, vjp = jax.vjp(
            lambda *q: _post_fn(*q, ones_bd),
            yf[...] + yb[...], r_ref[...], kf_ref[...], kb_ref[...], v_ref[...], g_ref[...], rk[...], lg[...], lb[...],
        )
        c_y, c_r, c_kf, _, c_v, c_g, c_rk, c_lg, c_lb = vjp(dy_ref[...])
        dyo[...] = c_y
        dr[...] = c_r
        dk[...] = c_kf
        dv[...] = c_v
        dg[...] = c_g

        @pl.when(pl.program_id(0) == 0)
        def _():
            for ref in (drk, dlg, dlb):
                ref[...] = jnp.zeros_like(ref)

        drk[...] += c_rk
        dlg[...] += c_lg
        dlb[...] += c_lb

    vec = _const((1, 512))
    return pl.pallas_call(
        body, grid=(m // tr,),
        in_specs=[row, row, r, v, row, row, row] + [_const(q.shape) for q in vecs] + [_const(p["ones_bd"].shape), row],
        out_specs=[row, row, row, row, row, vec, vec, vec],
        out_shape=[SDS((m, 512), F32)] * 5 + [SDS((1, 512), F32)] * 3,
        compiler_params=_cp("arbitrary"), name="rwkv_post_bwd",
    )(y_f, y_b, zs, zs, kf, kb, g, *vecs, p["ones_bd"], dymix)


SCAN_MXU_GROUPS = 2


def _half_ones():
    ri = lax.broadcasted_iota(jnp.int32, (128, 128), 0)
    ci = lax.broadcasted_iota(jnp.int32, (128, 128), 1)
    return jnp.where((ri < 64) == (ci < 64), 1.0, 0.0).astype(BF16)


def _half_sums(xs, ones):
    out = []
    per = -(-len(xs) // SCAN_MXU_GROUPS)
    for g in range(0, len(xs), per):
        part = xs[g:g + per]
        res = jnp.dot(jnp.concatenate(part, axis=0).astype(BF16), ones, preferred_element_type=F32)
        out += [res[64 * i:64 * i + 64] for i in range(len(part))]
    return out


def _scan_specs(b, t):
    nc = t // SCAN_CHUNK
    up, down = (lambda c: c), (lambda c: nc - 1 - c)
    rows = [BS((b, SCAN_CHUNK, 512), lambda c, ci=ci: (0, ci(c), 0)) for ci in (up, down)]
    vrows = [BS((b, SCAN_CHUNK, 512), lambda c, ci=ci: (0, ci(c), 2)) for ci in (up, down)]
    hist = [BS((SCAN_CHUNK, b * 4, 64, 128), lambda c, ci=ci: (ci(c), 0, 0, 0)) for ci in (up, down)]
    return nc, rows, vrows, hist


class _Window:
    def __init__(self, g, ascending):
        self.bases = [pl.multiple_of(g * 8, 8) if asc else pl.multiple_of(SCAN_CHUNK - 8 - g * 8, 8) for asc in ascending]
        self.ascending = ascending
        self.blocks = {}
        self.row_id = lax.broadcasted_iota(jnp.int32, (8, 128), 0)

    def j(self, d, s):
        return s if self.ascending[d] else 7 - s

    def time(self, d, s):
        return self.bases[d] + self.j(d, s)

    def row(self, ref, d, bi, cols, s):
        key = (id(ref), d, bi, cols.start)
        if key not in self.blocks:
            self.blocks[key] = ref[bi, pl.ds(self.bases[d], 8), cols]
        jj = self.j(d, s)
        return self.blocks[key][jj:jj + 1, :]

    def put(self, buf, key, d, s, row):
        prev = buf.get(key)
        new = jnp.broadcast_to(row, (8, 128))
        buf[key] = new if prev is None else jnp.where(self.row_id == self.j(d, s), new, prev)

    def flush(self, buf, refs_of):
        for key, val in buf.items():
            ref, d, bi, cols = refs_of(key)
            ref[bi, pl.ds(self.bases[d], 8), cols] = val


def _pairs(b):
    return [(bi * 4 + p, bi, slice(128 * p, 128 * p + 128)) for bi in range(b) for p in range(4)]


def _colsum(x):
    return jnp.sum(x, axis=0, keepdims=True)


def _pair_matvec(row, mat):
    rid = lax.broadcasted_iota(jnp.int32, (8, 64), 0)
    lhs = jnp.where(rid == 0, row[:, :64], jnp.where(rid == 1, row[:, 64:], 0.0))
    out = jnp.dot(lhs.astype(BF16), mat.astype(BF16), preferred_element_type=F32)
    lo = lax.broadcasted_iota(jnp.int32, (1, 128), 1) < 64
    return jnp.where(lo, out[0:1], out[1:2])


def _eye_mask():
    return (lax.broadcasted_iota(jnp.int32, (64, 128), 1) & 63) == lax.broadcasted_iota(jnp.int32, (64, 128), 0)


def _scan_fwd(zs, kk, ops_f, ops_b, shard):
    b, t = zs.shape[:2]
    nc, rows, vrows, hist = _scan_specs(b, t)
    npair = b * 4

    def body(*refs):
        ins, shard_ref, outs, s_ref = refs[:12], refs[12], refs[13:17], refs[17]
        gather = (shard_ref, *refs[18:21])
        dirs = [dict(zip(("r", "kk", "v", "w", "k", "ka", "y", "h"), (*ins[6 * d:6 * d + 6], *outs[2 * d:2 * d + 2])))
                for d in (0, 1)]

        @pl.when(pl.program_id(0) == 0)
        def _():
            s_ref[...] = jnp.zeros_like(s_ref)
            _gather_halves(*gather, "start")

        @pl.when(pl.program_id(0) == nc - 1)
        def _():
            _gather_halves(*gather, "finish")

        ones, eye = _half_ones(), _eye_mask()
        chains = [(d, pr, bi, cols) for d in (0, 1) for pr, bi, cols in _pairs(b)]

        def eight_steps(g, carry):
            win = _Window(g, (True, False))
            ybuf = {}
            for s in range(8):
                s_prev, xa = [], []
                for d, pr, bi, cols in chains:
                    q = dirs[d]
                    st = s_ref[d * npair + pr]
                    q["h"][win.time(d, s), pr] = st
                    s_prev.append(st)
                    xa += [st * win.row(q["kk"], d, bi, cols, s), jnp.where(eye, win.row(q["v"], d, bi, cols, s), 0.0)]
                ra = _half_sums(xa, ones)
                xb = []
                for i, (d, pr, bi, cols) in enumerate(chains):
                    q = dirs[d]
                    s_new = s_prev[i] * win.row(q["w"], d, bi, cols, s) - ra[2 * i] * win.row(q["ka"], d, bi, cols, s) \
                        + ra[2 * i + 1] * win.row(q["k"], d, bi, cols, s)
                    s_ref[d * npair + pr] = s_new
                    xb.append(s_new * win.row(q["r"], d, bi, cols, s))
                rb = _half_sums(xb, ones)
                for i, (d, pr, bi, cols) in enumerate(chains):
                    win.put(ybuf, i, d, s, _colsum(jnp.where(eye, rb[i], 0.0)))
            win.flush(ybuf, lambda i: (dirs[chains[i][0]]["y"], chains[i][0], chains[i][2], chains[i][3]))
            return carry

        def sixteen_steps(g2, carry):
            return eight_steps(2 * g2 + 1, eight_steps(2 * g2, carry))

        lax.fori_loop(0, SCAN_CHUNK // 16, sixteen_steps, 0)

    row_shape, hist_shape = SDS((b, t, 512), F32), SDS((t, npair, 64, 128), F32)
    state = (2 * npair, 64, 128)
    return pl.pallas_call(
        body, grid=(nc,), in_specs=sum(([rows[d], rows[d], vrows[d]] + [rows[d]] * 3 for d in (0, 1)), []) + [ANY],
        out_specs=[rows[0], hist[0], rows[1], hist[1], BS(state, lambda c: (0, 0, 0)), ANY],
        out_shape=[row_shape, hist_shape, row_shape, hist_shape, SDS(state, F32), _gathered_shape(shard)],
        scratch_shapes=_gather_sems(), compiler_params=_cp("arbitrary"), name="wkv_scan",
    )(zs, kk, zs, *ops_f, zs, kk, zs, *ops_b, shard)


def _scan_bwd(zs, kk, dy, ops_f, hist_f, ops_b, hist_b, s_last, partials):
    b, t = zs.shape[:2]
    nc, rows, vrows, hist = _scan_specs(b, t)
    npair = b * 4
    names_in = ("r", "kk", "v", "dy", "w", "k", "ka", "h")
    names_out = ("dr", "dw", "dk", "dkk", "dka", "dv")

    def body(*refs):
        ins, last_ref, part_ref, outs, recv_ref = refs[:16], refs[16], refs[17], refs[18:30], refs[30]
        ds_ref, after_ref = refs[31], refs[32]
        scatter = (part_ref, recv_ref, refs[33], refs[34])
        dirs = [dict(zip(names_in + names_out, (*ins[8 * d:8 * d + 8], *outs[6 * d:6 * d + 6]))) for d in (0, 1)]

        @pl.when(pl.program_id(0) == 0)
        def _():
            ds_ref[...] = jnp.zeros_like(ds_ref)
            after_ref[...] = last_ref[...]
            _scatter_to_all(*scatter, "start")

        @pl.when(pl.program_id(0) == nc - 1)
        def _():
            _scatter_to_all(*scatter, "finish")

        ones, eye = _half_ones(), _eye_mask()
        chains = [(d, pr, bi, cols) for d in (0, 1) for pr, bi, cols in _pairs(b)]

        def eight_steps(g, carry):
            win = _Window(g, (False, True))
            obuf = {}
            s_after = [after_ref[d * npair + pr] for d, pr, _, _ in chains]
            for s in range(8):
                row = lambda name, d, bi, cols: win.row(dirs[d][name], d, bi, cols, s)
                s_prev, xa = [], []
                for d, pr, bi, cols in chains:
                    st = dirs[d]["h"][win.time(d, s), pr]
                    s_prev.append(st)
                    xa += [st * row("kk", d, bi, cols), jnp.where(eye, row("dy", d, bi, cols), 0.0)]
                ra = _half_sums(xa, ones)
                ds_now, xb = [], []
                for i, (d, pr, bi, cols) in enumerate(chains):
                    skk, dycol = ra[2 * i], ra[2 * i + 1]
                    ds = ds_ref[d * npair + pr] + dycol * row("r", d, bi, cols)
                    win.put(obuf, (i, "dr"), d, s, _pair_matvec(row("dy", d, bi, cols), s_after[i]))
                    win.put(obuf, (i, "dk"), d, s, _pair_matvec(row("v", d, bi, cols), ds))
                    win.put(obuf, (i, "dka"), d, s, -_colsum(ds * skk))
                    win.put(obuf, (i, "dw"), d, s, _colsum(ds * s_prev[i]))
                    ds_now.append(ds)
                    xb += [ds * row("k", d, bi, cols), ds * row("ka", d, bi, cols)]
                rb = _half_sums(xb, ones)
                for i, (d, pr, bi, cols) in enumerate(chains):
                    dskk_neg = rb[2 * i + 1]
                    win.put(obuf, (i, "dv"), d, s, _colsum(jnp.where(eye, rb[2 * i], 0.0)))
                    win.put(obuf, (i, "dkk"), d, s, -_colsum(s_prev[i] * dskk_neg))
                    ds_ref[d * npair + pr] = ds_now[i] * row("w", d, bi, cols) - dskk_neg * row("kk", d, bi, cols)
                s_after = s_prev
            for i, (d, pr, _, _) in enumerate(chains):
                after_ref[d * npair + pr] = s_after[i]
            win.flush(obuf, lambda key: (dirs[chains[key[0]][0]][key[1]], chains[key[0]][0], chains[key[0]][2], chains[key[0]][3]))
            return carry

        def sixteen_steps(g2, carry):
            return eight_steps(2 * g2 + 1, eight_steps(2 * g2, carry))

        lax.fori_loop(0, SCAN_CHUNK // 16, sixteen_steps, 0)

    row_shape = SDS((b, t, 512), F32)
    state = (2 * npair, 64, 128)
    return pl.pallas_call(
        body, grid=(nc,),
        in_specs=sum(([rows[d], rows[d], vrows[d]] + [rows[d]] * 4 + [hist[d]] for d in (1, 0)), [])
        + [BS(state, lambda c: (0, 0, 0)), ANY],
        out_specs=[rows[1]] * 6 + [rows[0]] * 6 + [ANY],
        out_shape=[row_shape] * 12 + [SDS((7, partials.shape[1] // 2, 128), partials.dtype)],
        scratch_shapes=[pltpu.VMEM(state, F32), pltpu.VMEM(state, F32), pltpu.SemaphoreType.DMA((7,)),
                        pltpu.SemaphoreType.DMA((7,))],
        compiler_params=_cp("arbitrary"), name="wkv_scan_bwd",
    )(zs, kk, zs, dy, *ops_f, hist_f, zs, kk, zs, dy, *ops_b, hist_b, s_last, partials)


def _mla_fwd(zm, cs, sn, p, t):
    m = zm.shape[0]
    tr = 512
    per = t // tr
    params = [p["q_norm_g"], p["kv_norm_g"], p["wq"], p["wqr"], p["wk"], p["wv"]]

    def body(z_ref, cs_ref, sn_ref, gq, gkv, wq, wqr, wk, wv, q_ref, k_ref, v_ref):
        q, kf, v = _mla_fn(z_ref[...], cs_ref[...], sn_ref[...], gq[...], gkv[...], wq[...], wqr[...], wk[...], wv[...])
        q_ref[...] = q.astype(BF16)
        k_ref[...] = kf.astype(BF16)
        v_ref[...] = v.astype(BF16)

    tab = BS((tr, 128), lambda i: (i % per, 0))
    return pl.pallas_call(
        body, grid=(m // tr,), in_specs=[BS((tr, MLA_PAD_COLS), lambda i: (i, 0)), tab, tab] + [_const(q.shape) for q in params],
        out_specs=[BS((tr, 1024), lambda i: (i, 0)), BS((tr, 1024), lambda i: (i, 0)), BS((tr, 512), lambda i: (i, 0))],
        out_shape=[SDS((m, 1024), BF16), SDS((m, 1024), BF16), SDS((m, 512), BF16)], compiler_params=_cp("parallel"), name="mla_prep_fwd",
    )(zm, cs, sn, *params)


def _mla_bwd(zm, cs, sn, p, t, dq, dk, dv):
    m = zm.shape[0]
    tr = 256
    per = t // tr
    params = [p["q_norm_g"], p["kv_norm_g"], p["wq"], p["wqr"], p["wk"], p["wv"]]

    def body(z_ref, cs_ref, sn_ref, gq, gkv, wq, wqr, wk, wv, dq_ref, dk_ref, dv_ref, dz_ref, *grads):
        cs_v, sn_v = cs_ref[...], sn_ref[...]
        _, vjp = jax.vjp(
            lambda *q: _mla_fn(q[0], cs_v, sn_v, *q[1:]), z_ref[...], gq[...], gkv[...], wq[...], wqr[...], wk[...], wv[...]
        )
        dz, *dparams = vjp((dq_ref[...], dk_ref[...], dv_ref[...]))
        dz_ref[...] = dz.astype(BF16)

        @pl.when(pl.program_id(0) == 0)
        def _():
            for gr in grads:
                gr[...] = jnp.zeros_like(gr)

        for gr, val in zip(grads, dparams):
            gr[...] += val

    tab = BS((tr, 128), lambda i: (i % per, 0))
    wide = BS((tr, 1024), lambda i: (i, 0))
    return pl.pallas_call(
        body, grid=(m // tr,),
        in_specs=[BS((tr, MLA_PAD_COLS), lambda i: (i, 0)), tab, tab] + [_const(q.shape) for q in params]
        + [wide, wide, BS((tr, 512), lambda i: (i, 0))],
        out_specs=[BS((tr, MLA_PAD_COLS), lambda i: (i, 0))] + [_const(q.shape) for q in params],
        out_shape=[SDS((m, MLA_PAD_COLS), BF16)] + [SDS(q.shape, F32) for q in params],
        compiler_params=_cp("arbitrary"), name="mla_prep_bwd",
    )(zm, cs, sn, *params, dq, dk, dv)


_NT = (((1,), (1,)), ((), ()))
_TN = (((0,), (0,)), ((), ()))


def _attn_fwd(q, kf, v, b, t):
    m = q.shape[0]
    tq = 256
    nq = t // tq

    def body(q_ref, k_ref, v_ref, o_ref, l_ref):
        lo = lax.broadcasted_iota(jnp.int32, (1, 128), 1) < 64
        v_all = v_ref[...]
        o = jnp.zeros((tq, 128), F32)
        lse = []
        for h in range(2):
            hs = slice(128 * h, 128 * h + 128)
            s = lax.dot_general(q_ref[:, hs], k_ref[:, hs], _NT, preferred_element_type=F32)
            mx = jnp.max(s, axis=1, keepdims=True)
            e = jnp.exp(s - mx)
            den = jnp.sum(e, axis=1, keepdims=True)
            vh = jnp.where(lo if h == 0 else jnp.logical_not(lo), v_all, jnp.zeros_like(v_all))
            o = o + jnp.dot(e.astype(BF16), vh, preferred_element_type=F32) / den
            lse.append(mx + jnp.log(den))
        o_ref[...] = o
        l_ref[...] = jnp.where(lo, lse[0], lse[1])

    return pl.pallas_call(
        body, grid=(b, 4, nq),
        in_specs=[BS((tq, 256), lambda bi, hp, i: (bi * nq + i, hp)), BS((t, 256), lambda bi, hp, i: (bi, hp)),
                  BS((t, 128), lambda bi, hp, i: (bi, hp))],
        out_specs=[BS((tq, 128), lambda bi, hp, i: (bi * nq + i, hp))] * 2,
        out_shape=[SDS((m, 512), F32), SDS((m, 512), F32)], compiler_params=_cp("parallel", "parallel", "arbitrary"), name="attn_fwd",
    )(q, kf, v)


def _attn_bwd(q, kf, v, o, lse, do, b, t):
    m = q.shape[0]
    tq = 256
    nq = t // tq

    def body(q_ref, k_ref, v_ref, o_ref, l_ref, do_ref, dq_ref, dk_ref, dv_ref):
        lo = lax.broadcasted_iota(jnp.int32, (1, 128), 1) < 64

        @pl.when(pl.program_id(2) == 0)
        def _():
            dk_ref[...] = jnp.zeros_like(dk_ref)
            dv_ref[...] = jnp.zeros_like(dv_ref)

        v_all, o_all, l_all, do_all = v_ref[...], o_ref[...], l_ref[...], do_ref[...]
        dv_acc = jnp.zeros((t, 128), F32)
        for h in range(2):
            hs = slice(128 * h, 128 * h + 128)
            mask = lo if h == 0 else jnp.logical_not(lo)
            qh, kh = q_ref[:, hs], k_ref[:, hs]
            s = lax.dot_general(qh, kh, _NT, preferred_element_type=F32)
            lse_h = jnp.max(jnp.where(mask, l_all, -jnp.inf), axis=1, keepdims=True)
            pr = jnp.exp(s - lse_h)
            do_h = jnp.where(mask, do_all, 0.0)
            dp = lax.dot_general(do_h.astype(BF16), v_all, _NT, preferred_element_type=F32)
            dsum = jnp.sum(do_h * o_all, axis=1, keepdims=True)
            ds = (pr * (dp - dsum)).astype(BF16)
            dq_ref[:, hs] = jnp.dot(ds, kh, preferred_element_type=F32)
            dk_ref[:, hs] += lax.dot_general(ds, qh, _TN, preferred_element_type=F32)
            dv_acc = dv_acc + lax.dot_general(pr.astype(BF16), do_h.astype(BF16), _TN, preferred_element_type=F32)
        dv_ref[...] += dv_acc

    qspec = BS((tq, 256), lambda bi, hp, i: (bi * nq + i, hp))
    kspec = BS((t, 256), lambda bi, hp, i: (bi, hp))
    vspec = BS((t, 128), lambda bi, hp, i: (bi, hp))
    ospec = BS((tq, 128), lambda bi, hp, i: (bi * nq + i, hp))
    return pl.pallas_call(
        body, grid=(b, 4, nq), in_specs=[qspec, kspec, vspec, ospec, ospec, ospec], out_specs=[qspec, kspec, vspec],
        out_shape=[SDS((m, 1024), F32), SDS((m, 1024), F32), SDS((m, 512), F32)],
        compiler_params=_cp("parallel", "parallel", "arbitrary"), name="attn_bwd",
    )(q, kf, v, o, lse, do)


def _conv3(u, w_ref, b_ref, t):
    up, un = _prev_next(u, t)
    return w_ref[0:1, :] * up + w_ref[1:2, :] * u + w_ref[2:3, :] * un + b_ref[...], up, un


def _ffn_mid_fwd(ug3, uv3, cw, cb):
    b, t, f = ug3.shape
    nc = f // 256

    def body(ug_ref, uv_ref, wg_ref, wv_ref, bg_ref, bv_ref, a_ref):
        gc, _, _ = _conv3(ug_ref[0], wg_ref, bg_ref, t)
        vc, _, _ = _conv3(uv_ref[0], wv_ref, bv_ref, t)
        a_ref[0] = (gc * jax.nn.sigmoid(gc) * vc).astype(BF16)

    blk = BS((1, t, 256), lambda i, j: (i, 0, j))
    return pl.pallas_call(
        body, grid=(b, nc),
        in_specs=[blk, blk, BS((3, 256), lambda i, j: (0, j)), BS((3, 256), lambda i, j: (0, j + nc)),
                  BS((1, 256), lambda i, j: (0, j)), BS((1, 256), lambda i, j: (0, j + nc))],
        out_specs=blk, out_shape=SDS((b, t, f), BF16), compiler_params=_cp("parallel", "parallel"), name="ffn_mid_fwd",
    )(ug3, uv3, cw, cw, cb, cb)


def _ffn_mid_bwd(ug3, uv3, cw, cb, da3):
    b, t, f = ug3.shape
    nc = f // 256

    def half(u, up, un, dc, w_ref):
        dprev, dnext = _prev_next(dc, t)
        du = w_ref[1:2, :] * dc + w_ref[0:1, :] * dnext + w_ref[2:3, :] * dprev
        sums = [jnp.sum(dc * q, axis=0, keepdims=True) for q in (up, u, un)] + [jnp.sum(dc, axis=0, keepdims=True)]
        row = lax.broadcasted_iota(jnp.int32, (8, 256), 0)
        tab = jnp.zeros((8, 256), F32)
        for i, s in enumerate(sums):
            tab = jnp.where(row == i, s, tab)
        return du, tab

    def body(ug_ref, uv_ref, wg_ref, wv_ref, bg_ref, bv_ref, da_ref, dug_ref, duv_ref, tg_ref, tv_ref):
        ug, uv, da = ug_ref[0], uv_ref[0], da_ref[0]
        gc, gp, gn = _conv3(ug, wg_ref, bg_ref, t)
        vc, vp, vn = _conv3(uv, wv_ref, bv_ref, t)
        sg = jax.nn.sigmoid(gc)
        d_gc = da * vc * (sg * (1.0 + gc * (1.0 - sg)))
        d_vc = da * (gc * sg)
        dug, tg = half(ug, gp, gn, d_gc, wg_ref)
        duv, tv = half(uv, vp, vn, d_vc, wv_ref)
        dug_ref[0] = dug.astype(BF16)
        duv_ref[0] = duv.astype(BF16)

        @pl.when(pl.program_id(1) == 0)
        def _():
            tg_ref[...] = jnp.zeros_like(tg_ref)
            tv_ref[...] = jnp.zeros_like(tv_ref)

        tg_ref[...] += tg
        tv_ref[...] += tv

    blk = BS((1, t, 256), lambda j, i: (i, 0, j))
    tab = BS((8, 256), lambda j, i: (0, j))
    return pl.pallas_call(
        body, grid=(nc, b),
        in_specs=[blk, blk, BS((3, 256), lambda j, i: (0, j)), BS((3, 256), lambda j, i: (0, j + nc)),
                  BS((1, 256), lambda j, i: (0, j)), BS((1, 256), lambda j, i: (0, j + nc)), blk],
        out_specs=[blk, blk, tab, tab],
        out_shape=[SDS((b, t, f), BF16), SDS((b, t, f), BF16), SDS((8, f), F32), SDS((8, f), F32)],
        compiler_params=_cp("parallel", "arbitrary"), name="ffn_mid_bwd",
    )(ug3, uv3, cw, cw, cb, cb, da3)


def _add_rows(parts, name, out_dtype=F32):
    r = parts[0].shape[0]
    tr = _row_tile(r, 1024)
    n = len(parts)

    def body(*refs):
        acc = refs[0][...].astype(F32)
        for q in refs[1:n]:
            acc = acc + q[...].astype(F32)
        refs[n][...] = acc.astype(out_dtype)

    row = BS((tr, 128), lambda i: (i, 0))
    return pl.pallas_call(
        body, grid=(r // tr,), in_specs=[row] * n, out_specs=row, out_shape=SDS((r, 128), out_dtype),
        compiler_params=_cp("parallel"), name=name,
    )(*parts)


def _adamw(w, g, m, v, name):
    lead = w.shape[:-2]
    r, c = w.shape[-2:]
    tr = _row_tile(r)

    def body(w_ref, g_ref, m_ref, v_ref, d_ref, m2_ref, v2_ref):
        d, m2, v2 = _adamw_math(w_ref[...], g_ref[...], m_ref[...], v_ref[...])
        d_ref[...] = d
        m2_ref[...] = m2
        v2_ref[...] = v2

    blk = BS((1,) * len(lead) + (tr, c), lambda i: (0,) * len(lead) + (i, 0))
    return pl.pallas_call(
        body, grid=(r // tr,), in_specs=[blk] * 4, out_specs=[blk] * 3, out_shape=[SDS(w.shape, F32)] * 3,
        compiler_params=_cp("parallel"), name=name,
    )(w, g, m, v)


def _place():
    return lax.axis_index("x"), lax.axis_index("y"), lax.axis_index("c")


def _flip(v, bit):
    return 1 - v if bit else v


def _allgather_weights(shard):
    def body(x_ref, out_ref, send_sems, recv_sems):
        _gather_halves(x_ref, out_ref, send_sems, recv_sems, "start")
        _gather_halves(x_ref, out_ref, send_sems, recv_sems, "finish")

    return pl.pallas_call(
        body, out_shape=_gathered_shape(shard), in_specs=[ANY], out_specs=ANY, scratch_shapes=_gather_sems(),
        name="allgather_weights",
    )(shard)


def _gathered_shape(shard):
    return SDS((8 * (shard.shape[0] // 2), 128), shard.dtype)


def _gather_sems():
    return [pltpu.SemaphoreType.DMA((6,)), pltpu.SemaphoreType.DMA((6,))]


def _gather_halves(x_ref, out_ref, send_sems, recv_sems, phase):
    rh = x_ref.shape[0] // 2
    x, y, c = _place()
    me, sibling = (x, y, c), (x, y, 1 - c)
    chips = [(1 - x, y), (x, 1 - y), (1 - x, 1 - y)]
    mine_src = x_ref.at[pl.ds(c * rh, rh), :]

    def rows(px, py, pc):
        return out_ref.at[pl.ds((4 * px + 2 * py + pc) * rh, rh), :]

    def copy(k, block, to, src=None):
        return pltpu.make_async_remote_copy(
            src_ref=rows(*block) if src is None else src, dst_ref=rows(*block), send_sem=send_sems.at[k],
            recv_sem=recv_sems.at[k], device_id=to, device_id_type=MESH,
        )

    first = [copy(j, me, (*chip, c), src=mine_src) for j, chip in enumerate(chips)]
    if phase == "start":
        for cp in first:
            cp.start()
        return
    passed = [copy(3 + j, (*chip, c), sibling) for j, chip in enumerate(chips)]
    for j, chip in enumerate(chips):
        copy(j, (*chip, c), me).wait_recv()
        passed[j].start()
    for j, chip in enumerate(chips):
        copy(3 + j, (*chip, 1 - c), me).wait_recv()
    for cp in first + passed:
        cp.wait_send()


def _scatter_partials(g_ref, recv_ref, send_sems, recv_sems, phase):
    x, y, c = _place()
    copies = []
    for j, (fx, fy) in enumerate(((1, 0), (0, 1), (1, 1))):
        px, py = _flip(x, fx), _flip(y, fy)
        copies.append(pltpu.make_async_remote_copy(
            src_ref=g_ref.at[2 * px + py], dst_ref=recv_ref.at[j], send_sem=send_sems.at[j], recv_sem=recv_sems.at[j],
            device_id=(px, py, c), device_id_type=MESH,
        ))
    if phase == "start":
        for cp in copies:
            cp.start()
        return
    for cp in copies:
        cp.wait_recv()
    for cp in copies:
        cp.wait_send()


def _scatter_to_all(g_ref, recv_ref, send_sems, recv_sems, phase):
    rh = g_ref.shape[1] // 2
    x, y, c = _place()
    copies = []
    for k in range(1, 8):
        px, py, pc = _flip(x, k >> 2 & 1), _flip(y, k >> 1 & 1), _flip(c, k & 1)
        copies.append(pltpu.make_async_remote_copy(
            src_ref=g_ref.at[2 * px + py, pl.ds(pc * rh, rh), :], dst_ref=recv_ref.at[k - 1], send_sem=send_sems.at[k - 1],
            recv_sem=recv_sems.at[k - 1], device_id=(px, py, pc), device_id_type=MESH,
        ))
    if phase == "start":
        for cp in copies:
            cp.start()
        return
    for cp in copies:
        cp.wait_recv()
    for cp in copies:
        cp.wait_send()


def _scatter_to_chips(g):
    def body(g_ref, recv_ref, send_sems, recv_sems):
        _scatter_partials(g_ref, recv_ref, send_sems, recv_sems, "start")
        _scatter_partials(g_ref, recv_ref, send_sems, recv_sems, "finish")

    return pl.pallas_call(
        body, out_shape=SDS((3,) + g.shape[1:], g.dtype), in_specs=[ANY], out_specs=ANY,
        scratch_shapes=[pltpu.SemaphoreType.DMA((3,)), pltpu.SemaphoreType.DMA((3,))], name="scatter_grads",
    )(g)


def _send_to_sibling(a, half_of_rows):
    rh = a.shape[1] // 2

    def body(a_ref, b_ref, send_sem, recv_sem):
        x, y, c = _place()
        src = a_ref.at[:, pl.ds((1 - c) * rh, rh), :] if half_of_rows else a_ref
        cp = pltpu.make_async_remote_copy(
            src_ref=src, dst_ref=b_ref, send_sem=send_sem, recv_sem=recv_sem, device_id=(x, y, 1 - c), device_id_type=MESH
        )
        cp.start()
        cp.wait()

    shape = (a.shape[0], rh, 128) if half_of_rows else a.shape
    return pl.pallas_call(
        body, out_shape=SDS(shape, a.dtype), in_specs=[ANY], out_specs=ANY,
        scratch_shapes=[pltpu.SemaphoreType.DMA, pltpu.SemaphoreType.DMA],
        name="sibling_halves" if half_of_rows else "sibling_swap",
    )(a)


def _allreduce_small(v):
    r = v.shape[0]

    def body(v_ref, out_ref, buf_ref, send_sems, recv_sems):
        x, y, c = _place()
        buf_ref[0] = v_ref[...]
        copies = []
        for k in range(1, 8):
            peer = (_flip(x, k >> 2 & 1), _flip(y, k >> 1 & 1), _flip(c, k & 1))
            cp = pltpu.make_async_remote_copy(
                src_ref=v_ref, dst_ref=buf_ref.at[k], send_sem=send_sems.at[k - 1], recv_sem=recv_sems.at[k - 1],
                device_id=peer, device_id_type=MESH,
            )
            cp.start()
            copies.append(cp)
        for cp in copies:
            cp.wait_recv()
        acc = None
        for d in range(8):
            slot = 4 * _flip(x, d >> 2 & 1) + 2 * _flip(y, d >> 1 & 1) + _flip(c, d & 1)
            term = buf_ref[slot]
            acc = term if acc is None else acc + term
        out_ref[...] = acc
        for cp in copies:
            cp.wait_send()

    return pl.pallas_call(
        body, out_shape=SDS(v.shape, F32), in_specs=[VMEM], out_specs=VMEM,
        scratch_shapes=[pltpu.VMEM((8, r, 128), F32), pltpu.SemaphoreType.DMA((7,)), pltpu.SemaphoreType.DMA((7,))],
        name="allreduce_small",
    )(v)


_BIG_A = (
    ("w_in", 0, False),
    ("decay_w2_fwd", 1, False), ("decay_w2_bwd", 1, False), ("iclr_a2_fwd", 1, False),
    ("iclr_a2_bwd", 1, False), ("gate_g2", 1, False),
)
_BIG_B = (
    ("w_uq", 0, False), ("w_ukv", 1, False), ("w_out", 0, False), ("w_ffn_up", 1, False), ("ffn_conv_w", 1, True),
    ("w_ffn_down", 0, False),
)
_BIG = _BIG_A + _BIG_B
_SMALL = (
    "ln_mix_g", "shift_mu_prev", "shift_mu_next", "decay_w0_fwd", "decay_w0_bwd", "iclr_a0_fwd", "iclr_a0_bwd", "k_k",
    "k_a", "r_k", "ln_x_g", "ln_x_b", "q_norm_g", "kv_norm_g", "mla_out_g", "ln_ffn_g", "ffn_conv_b", "ln_final_g",
)
_WEIGHTS = (
    "ln_mix_g", "w_in", "shift_mu_prev", "shift_mu_next", "decay_w0_fwd", "decay_w2_fwd", "decay_w0_bwd", "decay_w2_bwd",
    "iclr_a0_fwd", "iclr_a2_fwd", "iclr_a0_bwd", "iclr_a2_bwd", "gate_g2", "k_k", "k_a", "r_k", "ln_x_g", "ln_x_b",
    "q_norm_g", "w_uq", "kv_norm_g", "w_ukv", "mla_out_g", "w_out", "ln_ffn_g", "w_ffn_up", "ffn_conv_w", "ffn_conv_b",
    "w_ffn_down", "ln_final_g",
)


def _pad_rows(flat, rows):
    return jnp.pad(flat, (0, rows * 128 - flat.shape[0])).reshape(rows, 128)


def _rows_for(n, mult):
    rows = -(-n // 128)
    return -(-rows // mult) * mult


def _pack_shards_bf16(arrs, entries):
    parts = []
    for name, _, raw in entries:
        w = arrs[name][0]
        flat = lax.bitcast_convert_type(w, BF16).reshape(-1) if raw else w.astype(BF16).reshape(-1)
        parts.append(_pad_rows(flat, _rows_for(flat.shape[0], 32)))
    return jnp.concatenate(parts, axis=0)


def _unpack_gathered(g4, shard, chip, arrs, entries):
    out, off = {}, 0
    mine = (jnp.arange(N_CHIPS) == chip)[:, None, None]
    for name, axis, raw in entries:
        a, b = arrs[name].shape[1:]
        n = a * b * (2 if raw else 1)
        rows = _rows_for(n, 32)
        seg = jnp.where(mine, shard[None, off:off + rows], g4[:, off:off + rows]).reshape(4, rows * 128)[:, :n]
        off += rows
        if raw:
            seg = lax.bitcast_convert_type(seg.reshape(4, a * b, 2), F32)
        seg = seg.reshape(4, a, b)
        out[name] = jnp.concatenate([seg[s] for s in range(4)], axis=1) if axis == 1 else seg.reshape(4 * a, b)
    return out


def _pack_grads(full, arrs, entries, dtype=F32):
    parts = []
    for name, axis, _ in entries:
        a, b = arrs[name].shape[1:]
        g = full[name]
        if g.ndim == 3:
            sh = g
        else:
            sh = g.reshape(a, 4, b).transpose(1, 0, 2) if axis == 1 else g.reshape(4, a, b)
        rows = _rows_for(a * b, 8)
        flat = sh.reshape(4, a * b).astype(dtype)
        parts.append(jnp.pad(flat, ((0, 0), (0, rows * 128 - a * b))).reshape(4, rows, 128))
    total = sum(q.shape[1] for q in parts)
    parts.append(jnp.zeros((4, -(-total // 1024) * 1024 - total, 128), dtype))
    return jnp.concatenate(parts, axis=1)


def _unpack_grads(g, arrs, entries):
    out, off = {}, 0
    for name, _, _ in entries:
        a, b = arrs[name].shape[1:]
        rows = _rows_for(a * b, 8)
        out[name] = g[off:off + rows].reshape(-1)[:a * b].reshape(1, a, b)
        off += rows
    return out


def _pack_small(vals):
    flat = jnp.concatenate([vals[n].reshape(-1).astype(F32) for n in _SMALL] + [vals["_loss"].reshape(-1)])
    return _pad_rows(flat, _rows_for(flat.shape[0], 8))


def _unpack_small(buf, arrs):
    flat, out, off = buf.reshape(-1), {}, 0
    for n in _SMALL:
        size = arrs[n].size
        out[n] = flat[off:off + size].reshape(arrs[n].shape)
        off += size
    out["_loss"] = flat[off]
    return out


def _rot_cols(w):
    return jnp.concatenate([-w[..., 16:], w[..., :16]], axis=-1)


def _rot_cols_t(g):
    return jnp.concatenate([g[..., 16:], -g[..., :16]], axis=-1)


def _rope_tables(t):
    inv = jnp.power(ROPE_THETA, -jnp.arange(0, ROPE_DIM, 2, dtype=F32) / ROPE_DIM)
    ang = jnp.arange(t, dtype=F32)[:, None] * inv[None, :]
    one, zero = jnp.ones((t, 64), F32), jnp.zeros((t, 64), F32)
    cs = jnp.concatenate([one, jnp.cos(ang), jnp.cos(ang), zero[:, :32]], axis=1)
    sn = jnp.concatenate([zero, jnp.sin(ang), jnp.sin(ang), zero[:, :32]], axis=1)
    return cs, sn


def _block_diag(a, b):
    za = jnp.zeros_like(a)
    return jnp.concatenate([jnp.concatenate([a, za], axis=1), jnp.concatenate([za, b], axis=1)], axis=0)


def kernel(x, ln_mix_g, w_in, shift_mu_prev, shift_mu_next, decay_w0_fwd, decay_w2_fwd, decay_w0_bwd, decay_w2_bwd, iclr_a0_fwd, iclr_a2_fwd, iclr_a0_bwd, iclr_a2_bwd, gate_g2, k_k, k_a, r_k, ln_x_g, ln_x_b, q_norm_g, w_uq, kv_norm_g, w_ukv, mla_out_g, w_out, ln_ffn_g, w_ffn_up, ffn_conv_w, ffn_conv_b, w_ffn_down, ln_final_g, loss_target, m_ln_mix_g, m_w_in, m_shift_mu_prev, m_shift_mu_next, m_decay_w0_fwd, m_decay_w2_fwd, m_decay_w0_bwd, m_decay_w2_bwd, m_iclr_a0_fwd, m_iclr_a2_fwd, m_iclr_a0_bwd, m_iclr_a2_bwd, m_gate_g2, m_k_k, m_k_a, m_r_k, m_ln_x_g, m_ln_x_b, m_q_norm_g, m_w_uq, m_kv_norm_g, m_w_ukv, m_mla_out_g, m_w_out, m_ln_ffn_g, m_w_ffn_up, m_ffn_conv_w, m_ffn_conv_b, m_w_ffn_down, m_ln_final_g, v_ln_mix_g, v_w_in, v_shift_mu_prev, v_shift_mu_next, v_decay_w0_fwd, v_decay_w2_fwd, v_decay_w0_bwd, v_decay_w2_bwd, v_iclr_a0_fwd, v_iclr_a2_fwd, v_iclr_a0_bwd, v_iclr_a2_bwd, v_gate_g2, v_k_k, v_k_a, v_r_k, v_ln_x_g, v_ln_x_b, v_q_norm_g, v_w_uq, v_kv_norm_g, v_w_ukv, v_mla_out_g, v_w_out, v_ln_ffn_g, v_w_ffn_up, v_ffn_conv_w, v_ffn_conv_b, v_w_ffn_down, v_ln_final_g):
    arrs = dict(locals())
    for pre in ("", "m_", "v_"):
        arrs[pre + "w_in"] = jnp.swapaxes(arrs[pre + "w_in"], 1, 2)
    b, t, d = x.shape
    m = b * t
    x2 = x.reshape(m, d)
    tgt = loss_target.reshape(m, d)
    vec = lambda n: arrs[n].reshape(1, -1)

    core = lax.axis_index("c")
    chip = 2 * lax.axis_index("x") + lax.axis_index("y")
    def unpack(gathered, shard, entries):
        return _unpack_gathered(gathered.reshape(N_CHIPS, -1, 128), shard, chip, arrs, entries)

    shard_a, shard_b = _pack_shards_bf16(arrs, _BIG_A), _pack_shards_bf16(arrs, _BIG_B)
    fw = unpack(_allgather_weights(shard_a), shard_a, _BIG_A)
    win = fw["w_in"]
    zc = jnp.zeros((64, d), BF16)
    w_kr = win[2944:2976]
    rot_kr = jnp.swapaxes(_rot_cols(jnp.swapaxes(w_kr, 0, 1)), 0, 1)
    win_m = jnp.concatenate([win[1920:2944], zc, w_kr, zc[:32], zc, rot_kr, zc[:32]], axis=0)
    win_r = win[:RWKV_COLS]
    head = jnp.arange(512) // HEAD_DIM
    rw = dict(
        w0=jnp.concatenate([vec("decay_w0_fwd"), vec("decay_w0_bwd")], axis=1),
        w2=_block_diag(fw["decay_w2_fwd"], fw["decay_w2_bwd"]).astype(F32),
        a0=jnp.concatenate([vec("iclr_a0_fwd"), vec("iclr_a0_bwd")], axis=1),
        a2=_block_diag(fw["iclr_a2_fwd"], fw["iclr_a2_bwd"]).astype(F32),
        g2=fw["gate_g2"].astype(F32), k_k=vec("k_k"), k_a=vec("k_a"), r_k=vec("r_k"), ln_x_g=vec("ln_x_g"), ln_x_b=vec("ln_x_b"),
        ones_bd=(head[:, None] == head[None, :]).astype(F32),
    )
    cs, sn = _rope_tables(t)

    n1 = _rms_fwd(x2, vec("ln_mix_g"), "rms_mix")
    zm = _mm(n1, win_m, "nt", "proj_in_mla")
    zr = _mm(n1, win_r, "nt", "proj_in_rwkv")
    zs = _shift_fwd(zr.reshape(b, t, RWKV_COLS), vec("shift_mu_prev"), vec("shift_mu_next"))
    zs2 = zs.reshape(m, RWKV_COLS)
    wf, wb, kf, kb, kk, kaf, kab, gate = _prep_fwd(zs2, rw)
    r4 = lambda a: a.reshape(b, t, 512)
    f2 = lambda a: a.reshape(m, 512)
    kk4 = r4(kk)
    ops_f = (r4(wf), r4(kf), r4(kaf))
    ops_b = (r4(wb), r4(kb), r4(kab))
    y_f, hist_f, y_b, hist_b, s_last, gathered_b = _scan_fwd(zs, kk4, ops_f, ops_b, shard_b)
    fw.update(unpack(gathered_b, shard_b, _BIG_B))
    uq = fw["w_uq"].astype(F32).reshape(Q_RANK, HEADS, 96)
    z32 = jnp.zeros((Q_RANK, HEADS, 32), F32)
    wq = jnp.concatenate([uq[..., :64], uq[..., 64:], z32], axis=-1).reshape(Q_RANK, 1024)
    wqr = jnp.concatenate([z32, z32, _rot_cols(uq[..., 64:]), z32], axis=-1).reshape(Q_RANK, 1024)
    ukv = fw["w_ukv"].astype(F32).reshape(KV_RANK, HEADS, 128)
    wk = jnp.concatenate([ukv[..., :64], jnp.zeros_like(ukv[..., :64])], axis=-1).reshape(KV_RANK, 1024)
    wv = ukv[..., 64:].reshape(KV_RANK, 512)
    mp = dict(q_norm_g=vec("q_norm_g"), kv_norm_g=vec("kv_norm_g"), wq=wq, wqr=wqr, wk=wk, wv=wv)
    w_up_g, w_up_v = fw["w_ffn_up"][:, :D_FF], fw["w_ffn_up"][:, D_FF:]
    cw, cb = fw["ffn_conv_w"], vec("ffn_conv_b")
    y_f, y_b = f2(y_f), f2(y_b)
    y_rwkv = _post_fwd(y_f, y_b, zs2, kf, kb, gate, rw)
    q, kfull, v = _mla_fwd(zm, cs, sn, mp, t)
    o, lse = _attn_fwd(q, kfull, v, b, t)
    y_mla = _rms_fwd(o, vec("mla_out_g"), "rms_mla_out")
    ymix = jnp.concatenate([y_rwkv, y_mla], axis=1)
    h1 = _mm(ymix, fw["w_out"], "nn", "proj_out", add=x2)
    n2 = _rms_fwd(h1, vec("ln_ffn_g"), "rms_ffn")
    ug = _mm(n2, w_up_g, "nn", "ffn_up_gate")
    uv = _mm(n2, w_up_v, "nn", "ffn_up_val")
    r3f = lambda a: a.reshape(b, t, D_FF)
    act = _ffn_mid_fwd(r3f(ug), r3f(uv), cw, cb).reshape(m, D_FF)
    h2 = _mm(act, fw["w_ffn_down"], "nn", "ffn_down", add=h1)
    loss_tab, dh2, g_ln_final = _final(h2, vec("ln_final_g"), tgt)

    gfull = {}
    dact = _mm(dh2, fw["w_ffn_down"], "nt", "d_ffn_act")
    gfull["w_ffn_down"] = _mm(act, dh2, "tn", "g_ffn_down")
    dug, duv, tab_g, tab_v = _ffn_mid_bwd(r3f(ug), r3f(uv), cw, cb, r3f(dact))
    dug, duv = dug.reshape(m, D_FF), duv.reshape(m, D_FF)
    gfull["ffn_conv_w"] = jnp.concatenate([tab_g[0:3], tab_v[0:3]], axis=1)
    g_conv_b = jnp.concatenate([tab_g[3:4], tab_v[3:4]], axis=1)
    dn2 = _mm(duv, w_up_v, "nt", "d_ffn_in_val", add=_mm(dug, w_up_g, "nt", "d_ffn_in_gate"))
    shard_cols = arrs["w_ffn_up"].shape[2]
    gfull["w_ffn_up"] = jnp.concatenate([_mm(n2, dug, "tn", "g_ffn_up_gate", column_blocks=shard_cols),
                                         _mm(n2, duv, "tn", "g_ffn_up_val", column_blocks=shard_cols)], axis=0)
    dh1, g_ln_ffn = _rms_bwd(h1, vec("ln_ffn_g"), dn2, "rms_ffn_bwd", dres=dh2)
    dymix = _mm(dh1, fw["w_out"], "nt", "d_mix")
    gfull["w_out"] = _mm(ymix, dh1, "tn", "g_w_out")
    do, g_mla_out = _rms_bwd(o, vec("mla_out_g"), dymix, "rms_mla_out_bwd", dy_block=1)
    dq, dk, dv = _attn_bwd(q, kfull, v, o, lse, do, b, t)
    dzm, g_qn, g_kvn, g_wq, g_wqr, g_wk, g_wv = _mla_bwd(zm, cs, sn, mp, t, dq, dk, dv)
    gq3, gqr3 = g_wq.reshape(Q_RANK, HEADS, 128), g_wqr.reshape(Q_RANK, HEADS, 128)
    gfull["w_uq"] = jnp.concatenate(
        [gq3[..., :64], gq3[..., 64:96] + _rot_cols_t(gqr3[..., 64:96])], axis=-1
    ).reshape(Q_RANK, HEADS * 96)
    gfull["w_ukv"] = jnp.concatenate(
        [g_wk.reshape(KV_RANK, HEADS, 128)[..., :64], g_wv.reshape(KV_RANK, HEADS, 64)], axis=-1
    ).reshape(KV_RANK, 1024)
    def cores_first(entries, tag):
        packed = _pack_grads(gfull, arrs, entries)
        rh = packed.shape[1] // 2
        own = lax.dynamic_slice_in_dim(packed, core * rh, rh, axis=1)
        sib = _send_to_sibling(packed, True)
        return _add_rows([own.reshape(4 * rh, 128), sib.reshape(4 * rh, 128)], "sum_cores_" + tag, BF16).reshape(4, rh, 128)

    def join_halves(half, entries):
        other = _send_to_sibling(half, False)
        lower = jnp.where(core == 0, half, other)
        upper = jnp.where(core == 0, other, half)
        return _unpack_grads(jnp.concatenate([lower, upper], axis=0), arrs, entries)

    part_b = _pack_grads(gfull, arrs, _BIG_B, BF16)
    dys, dr_p, dk_p, dv_p, dgate, g_rk, g_lnx_g, g_lnx_b = _post_bwd(y_f, y_b, zs2, kf, kb, gate, rw, dymix)
    (dr_f, dwf, dkf, dkk_f, dkaf, dv_f, dr_b, dwb, dkb, dkk_b, dkab, dv_b, recv_b) = _scan_bwd(
        zs, kk4, r4(dys), ops_f, hist_f, ops_b, hist_b, s_last, part_b)
    rh_b = part_b.shape[1] // 2
    mine_b = lax.dynamic_slice(part_b, (chip, core * rh_b, 0), (1, rh_b, 128))[0]
    g_big = join_halves(_add_rows([mine_b] + [recv_b[k] for k in range(7)], "sum_devices_b"), _BIG_B)
    cts = dict(dwf=f2(dwf), dwb=f2(dwb), dkf=f2(dkf), dkb=f2(dkb), dkk_f=f2(dkk_f), dkk_b=f2(dkk_b), dkaf=f2(dkaf), dkab=f2(dkab),
               dr_f=f2(dr_f), dr_b=f2(dr_b), dr_p=dr_p, dk_p=dk_p, dv_p=dv_p, dg=dgate, dv_f=f2(dv_f), dv_b=f2(dv_b))
    dzs, g_w0, g_w2, g_a0, g_a2, g_g2, g_kk, g_ka = _prep_bwd(zs2, rw, cts)
    dzr, g_mu_p, g_mu_n = _shift_bwd(dzs.reshape(b, t, RWKV_COLS), zr.reshape(b, t, RWKV_COLS), vec("shift_mu_prev"), vec("shift_mu_next"))
    dzr = dzr.reshape(m, RWKV_COLS)
    gfull["decay_w2_fwd"], gfull["decay_w2_bwd"] = g_w2[:64, :512], g_w2[64:, 512:]
    gfull["iclr_a2_fwd"], gfull["iclr_a2_bwd"] = g_a2[:64, :512], g_a2[64:, 512:]
    gfull["gate_g2"] = g_g2
    dn1 = _mm(dzr, win_r, "nn", "d_proj_in_rwkv", add=_mm(dzm, win_m, "nn", "d_proj_in_mla"))
    g_m = _mm(dzm, n1, "tn", "g_w_in_mla")
    g_r = _mm(dzr, n1, "tn", "g_w_in_rwkv")
    g_kr = g_m[1088:1120] + jnp.swapaxes(_rot_cols_t(jnp.swapaxes(g_m[1216:1248], 0, 1)), 0, 1)
    gfull["w_in"] = jnp.concatenate([g_r, g_m[:1024], g_kr], axis=0)
    dx, g_ln_mix = _rms_bwd(x2, vec("ln_mix_g"), dn1, "rms_mix_bwd", dres=dh1)

    part_a = cores_first(_BIG_A, "a")
    recv_a = _scatter_to_chips(part_a)
    mine_a = lax.dynamic_index_in_dim(part_a, chip, axis=0, keepdims=False)
    g_big.update(join_halves(_add_rows([mine_a, recv_a[0], recv_a[1], recv_a[2]], "sum_chips_a"), _BIG_A))
    small = {
        "ln_mix_g": g_ln_mix, "shift_mu_prev": g_mu_p, "shift_mu_next": g_mu_n, "decay_w0_fwd": g_w0[:, :512],
        "decay_w0_bwd": g_w0[:, 512:], "iclr_a0_fwd": g_a0[:, :512], "iclr_a0_bwd": g_a0[:, 512:], "k_k": g_kk, "k_a": g_ka,
        "r_k": g_rk, "ln_x_g": g_lnx_g, "ln_x_b": g_lnx_b, "q_norm_g": g_qn, "kv_norm_g": g_kvn, "mla_out_g": g_mla_out,
        "ln_ffn_g": g_ln_ffn, "ffn_conv_b": g_conv_b, "ln_final_g": g_ln_final,
        "_loss": jnp.pad(loss_tab[0, 0:1], (0, 127)),
    }
    g_small_buf = _allreduce_small(_pack_small(small))
    g_small = _unpack_small(g_small_buf, arrs)

    grads, deltas, new_m, new_v = {}, {}, {}, {}
    for name, _, _ in _BIG:
        grads[name] = g_big[name]
        deltas[name], new_m[name], new_v[name] = _adamw(
            arrs[name], g_big[name], arrs["m_" + name], arrs["v_" + name], "adamw_" + name)
    pk = lambda pre: _pack_small({**{n: arrs[pre + n] for n in _SMALL}, "_loss": jnp.zeros((128,), F32)})
    sd, sm, sv = _adamw(pk(""), g_small_buf, pk("m_"), pk("v_"), "adamw_small")
    sd, sm, sv = _unpack_small(sd, arrs), _unpack_small(sm, arrs), _unpack_small(sv, arrs)
    for n in _SMALL:
        grads[n], deltas[n], new_m[n], new_v[n] = g_small[n], sd[n], sm[n], sv[n]
    for group in (grads, deltas, new_m, new_v):
        group["w_in"] = jnp.swapaxes(group["w_in"], 1, 2)

    return (g_small["_loss"], dx.reshape(b, t, d), *[grads[n] for n in _WEIGHTS], *[deltas[n] for n in _WEIGHTS],
            *[new_m[n] for n in _WEIGHTS], *[new_v[n] for n in _WEIGHTS])
```

```python
import functools
import math

import jax
import jax.numpy as jnp
from jax import lax
from jax.experimental import pallas as pl
from jax.experimental.pallas import tpu as pltpu

F32, BF16 = jnp.float32, jnp.bfloat16
MESH = pl.DeviceIdType.MESH
ANY = pl.BlockSpec(memory_space=pl.ANY)
VMEM = pl.BlockSpec(memory_space=pltpu.VMEM)
BS = pl.BlockSpec
SDS = jax.ShapeDtypeStruct

NORM_EPS = 1e-6
GN_EPS = 64e-5
L2_EPS = 1e-12
HEADS = 8
HEAD_DIM = 64
ROPE_DIM = 32
ROPE_THETA = 10000.0
MLA_SCALE = (64 + ROPE_DIM) ** -0.5
Q_RANK, KV_RANK = 768, 256
RWKV_COLS = 1920
MLA_PAD_COLS = Q_RANK + KV_RANK + 256
D_FF = 2816
ADAM_LR, ADAM_B1, ADAM_B2, ADAM_EPS, ADAM_WD, ADAM_STEP = 0.001, 0.9, 0.999, 1e-08, 0.01, 10

V7X_VMEM_LIMIT = 56 * 1024 * 1024
SCAN_CHUNK = 32
N_CHIPS = 4


def _cp(*sem):
    return pltpu.CompilerParams(dimension_semantics=sem, vmem_limit_bytes=V7X_VMEM_LIMIT)


def _tile(n, cands=(512, 640, 384, 256, 128)):
    for c in cands:
        if n % c == 0:
            return c
    return n


def _row_tile(n, cap=256):
    best = n
    for t in range(8, cap + 1, 8):
        if n % t == 0:
            best = t
    return best if best <= cap or n <= cap else n


def _rms(x, g):
    ms = jnp.mean(x * x, axis=-1, keepdims=True)
    return x * lax.rsqrt(ms + NORM_EPS) * g


@jax.custom_vjp
def _bdot(x, w):
    return jnp.dot(x.astype(BF16), w.astype(BF16), preferred_element_type=F32)


def _bdot_fwd(x, w):
    return _bdot(x, w), (x, w)


def _bdot_bwd(res, ct):
    x, w = res
    c = ct.astype(BF16)
    dx = lax.dot_general(c, w.astype(BF16), (((1,), (1,)), ((), ())), preferred_element_type=F32)
    dw = lax.dot_general(x.astype(BF16), c, (((0,), (0,)), ((), ())), preferred_element_type=F32)
    return dx.astype(x.dtype), dw.astype(w.dtype)


_bdot.defvjp(_bdot_fwd, _bdot_bwd)


@jax.custom_vjp
def _headsum(x, ones_bd):
    hi = x.astype(BF16)
    mid = (x - hi.astype(F32)).astype(BF16)
    ob = ones_bd.astype(BF16)
    return jnp.dot(hi, ob, preferred_element_type=F32) + jnp.dot(mid, ob, preferred_element_type=F32)


def _headsum_fwd(x, ones_bd):
    return _headsum(x, ones_bd), ones_bd


def _headsum_bwd(ones_bd, ct):
    return _headsum(ct, ones_bd), jnp.zeros_like(ones_bd)


_headsum.defvjp(_headsum_fwd, _headsum_bwd)


def _prep_fn(zs, w0, w2, a0, a2, g2, k_k, k_a, ones_bd):
    k = zs[:, 512:1024]
    wd = zs[:, 1536:1664]
    ad = zs[:, 1664:1792]
    gd = zs[:, 1792:1920]
    logit = w0 + _bdot(jnp.tanh(wd), w2)
    w = jnp.exp(-math.exp(-0.5) * jax.nn.sigmoid(logit))
    a = jax.nn.sigmoid(a0 + _bdot(ad, a2))
    g = _bdot(jax.nn.sigmoid(gd), g2)
    kkr = k * k_k
    nrm = jnp.sqrt(_headsum(kkr * kkr, ones_bd))
    kk = kkr / jnp.maximum(nrm, L2_EPS)
    a_f, a_b = a[:, :512], a[:, 512:]
    kf = k * (1.0 + (a_f - 1.0) * k_a)
    kb = k * (1.0 + (a_b - 1.0) * k_a)
    return w[:, :512], w[:, 512:], kf, kb, kk, kk * a_f, kk * a_b, g


def _post_fn(y, r, kf, kb, v, g, r_k, ln_g, ln_b, ones_bd):
    mu =_headsum(y, ones_bd) * (1.0 / HEAD_DIM)
    yc = y - mu
    var = _headsum(yc * yc, ones_bd) * (1.0 / HEAD_DIM)
    yn = yc * lax.rsqrt(var + GN_EPS) * ln_g + ln_b
    bonus = _headsum(r * (kf + kb) * r_k, ones_bd) * v
    return (yn + bonus) * g


def _cat8(x):
    return jnp.concatenate([x] * HEADS, axis=1)


def _mla_fn(zm, cs, sn, gq, gkv, wq, wqr, wk, wv):
    cq = zm[:, :Q_RANK]
    ckv = zm[:, Q_RANK:Q_RANK + KV_RANK]
    kr = zm[:, Q_RANK + KV_RANK:Q_RANK + KV_RANK + 128]
    krr = zm[:, Q_RANK + KV_RANK + 128:]
    cqn = _rms(cq, gq)
    ckvn = _rms(ckv, gkv)
    q = (_bdot(cqn, wq) * _cat8(cs) + _bdot(cqn, wqr) * _cat8(sn)) * MLA_SCALE
    kro = kr * cs + krr * sn
    kfull = _bdot(ckvn, wk) + _cat8(kro)
    v = _bdot(ckvn, wv)
    return q, kfull, v


def _adamw_math(w, g, m, v):
    m2 = ADAM_B1 * m + (1.0 - ADAM_B1) * g
    v2 = ADAM_B2 * v + (1.0 - ADAM_B2) * (g * g)
    m_hat = m2 / (1.0 - ADAM_B1 ** ADAM_STEP)
    v_hat = v2 / (1.0 - ADAM_B2 ** ADAM_STEP)
    delta = -ADAM_LR * (m_hat / (jnp.sqrt(v_hat) + ADAM_EPS) + ADAM_WD * w)
    return delta, m2, v2


_DIMS = {"nn": (((1,), (0,)), ((), ())), "nt": (((1,), (1,)), ((), ())), "tn": (((0,), (0,)), ((), ()))}


def _mm(a, b, mode, name, out_dtype=F32, add=None, column_blocks=None):
    if mode == "nn":
        (m, k), (_, n) = a.shape, b.shape
    elif mode == "nt":
        (m, k), (n, _) = a.shape, b.shape
    else:
        (k, m), (_, n) = a.shape, b.shape
    big = (1024, 1408, 768, 640, 512, 384, 256, 128)
    tm, tn, tk = _tile(m, big), column_blocks or _tile(n, big), _tile(k, (2816, 2048, 1024, 1408, 640, 512, 384, 256, 128))
    nk = k // tk

    def body(a_ref, b_ref, *rest):
        if add is None:
            o_ref, acc_ref = rest
        else:
            add_ref, o_ref, acc_ref = rest
        kk = pl.program_id(2)

        @pl.when(kk == 0)
        def _():
            acc_ref[...] = jnp.zeros_like(acc_ref)

        acc_ref[...] += lax.dot_general(
            a_ref[...].astype(BF16), b_ref[...].astype(BF16), _DIMS[mode], preferred_element_type=F32
        )

        @pl.when(kk == nk - 1)
        def _():
            r = acc_ref[...]
            if add is not None:
                r = r + add_ref[...]
            o_ref[...] = r.astype(out_dtype).reshape(o_ref.shape)

    a_spec = BS((tk, tm), lambda i, j, kk: (kk, i)) if mode == "tn" else BS((tm, tk), lambda i, j, kk: (i, kk))
    b_spec = BS((tn, tk), lambda i, j, kk: (j, kk)) if mode == "nt" else BS((tk, tn), lambda i, j, kk: (kk, j))
    o_spec = BS((tm, tn), lambda i, j, kk: (i, j))
    ins, specs = [a, b], [a_spec, b_spec]
    if add is not None:
        ins.append(add)
        specs.append(o_spec)
    out_shape = SDS((m, n), out_dtype)
    if column_blocks:
        assert add is None
        o_spec, out_shape = BS((1, tm, tn), lambda i, j, kk: (j, i, 0)), SDS((n // tn, m, tn), out_dtype)
    return pl.pallas_call(
        body, grid=(m // tm, n // tn, nk), in_specs=specs, out_specs=o_spec, out_shape=out_shape,
        scratch_shapes=[pltpu.VMEM((tm, tn), F32)], compiler_params=_cp("parallel", "parallel", "arbitrary"), name=name,
    )(*ins)


def _rms_fwd(x, g, name):
    m, d = x.shape
    tr = _tile(m)

    def body(x_ref, g_ref, o_ref):
        o_ref[...] = _rms(x_ref[...], g_ref[...]).astype(BF16)

    return pl.pallas_call(
        body, grid=(m // tr,), in_specs=[BS((tr, d), lambda i: (i, 0)), BS((1, d), lambda i: (0, 0))],
        out_specs=BS((tr, d), lambda i: (i, 0)), out_shape=SDS((m, d), BF16), compiler_params=_cp("parallel"), name=name,
    )(x, g)


def _rms_bwd(x, g, dy, name, dres=None, dy_block=0):
    m, d = x.shape
    tr = _row_tile(m, 512)

    def body(x_ref, g_ref, dy_ref, *rest):
        if dres is None:
            dx_ref, dg_ref = rest
        else:
            dres_ref, dx_ref, dg_ref = rest
        _, vjp = jax.vjp(_rms, x_ref[...], g_ref[...])
        dx, dg = vjp(dy_ref[...])
        if dres is not None:
            dx = dx + dres_ref[...]
        dx_ref[...] = dx

        @pl.when(pl.program_id(0) == 0)
        def _():
            dg_ref[...] = jnp.zeros_like(dg_ref)

        dg_ref[...] += dg

    row = BS((tr, d), lambda i: (i, 0))
    vec = BS((1, d), lambda i: (0, 0))
    ins, specs = [x, g, dy], [row, vec, BS((tr, d), lambda i: (i, dy_block))]
    if dres is not None:
        ins.append(dres)
        specs.append(row)
    return pl.pallas_call(
        body, grid=(m // tr,), in_specs=specs, out_specs=[row, vec], out_shape=[SDS((m, d), F32), SDS((1, d), F32)],
        compiler_params=_cp("arbitrary"), name=name,
    )(*ins)


def _final(h, g, tgt):
    m, d = h.shape
    tr = _row_tile(m, 512)

    def loss_fn(hh, gg, tt):
        e = _rms(hh, gg) - tt
        return 0.5 * jnp.sum(e * e) * (1.0 / d)

    def body(h_ref, g_ref, t_ref, l_ref, dh_ref, dg_ref):
        val, (dh, dg) = jax.value_and_grad(loss_fn, argnums=(0, 1))(h_ref[...], g_ref[...], t_ref[...])
        dh_ref[...] = dh

        @pl.when(pl.program_id(0) == 0)
        def _():
            dg_ref[...] = jnp.zeros_like(dg_ref)
            l_ref[...] = jnp.zeros_like(l_ref)

        dg_ref[...] += dg
        l_ref[...] += jnp.full(l_ref.shape, val, F32)

    row = BS((tr, d), lambda i: (i, 0))
    vec = BS((1, d), lambda i: (0, 0))
    return pl.pallas_call(
        body, grid=(m // tr,), in_specs=[row, vec, row], out_specs=[BS((8, 128), lambda i: (0, 0)), row, vec],
        out_shape=[SDS((8, 128), F32), SDS((m, d), F32), SDS((1, d), F32)], compiler_params=_cp("arbitrary"), name="final_loss",
    )(h, g, tgt)


def _prev_next(z, t):
    row = lax.broadcasted_iota(jnp.int32, z.shape, 0)
    zp = jnp.where(row == 0, 0.0, pltpu.roll(z, 1, axis=0))
    zn = jnp.where(row == t - 1, 0.0, pltpu.roll(z, t - 1, axis=0))
    return zp, zn


def _shift_fwd(z3, mu_p, mu_n):
    b, t, c = z3.shape
    nc = c // 128

    def body(z_ref, mp_ref, mn_ref, o_ref):
        z = z_ref[0]
        zp, zn = _prev_next(z, t)
        o_ref[0] = z + mp_ref[...] * (zp - z) + mn_ref[...] * (zn - z)

    blk = BS((1, t, 128), lambda i, j: (i, 0, j))
    vec = BS((1, 128), lambda i, j: (0, j))
    return pl.pallas_call(
        body, grid=(b, nc), in_specs=[blk, vec, vec], out_specs=blk, out_shape=SDS((b, t, c), F32),
        compiler_params=_cp("parallel", "parallel"), name="shift_fwd",
    )(z3, mu_p, mu_n)


def _shift_bwd(dzs3, z3, mu_p, mu_n):
    b, t, c = z3.shape
    nc = c // 128

    def body(d_ref, z_ref, mp_ref, mn_ref, dz_ref, dmp_ref, dmn_ref):
        d, z = d_ref[0], z_ref[0]
        mp, mn = mp_ref[...], mn_ref[...]
        zp, zn = _prev_next(z, t)
        _, dp_next = _prev_next(d * mp, t)
        dn_prev, _ = _prev_next(d * mn, t)
        dz_ref[0] = (d * (1.0 - mp - mn) + dp_next + dn_prev).astype(BF16)

        @pl.when(pl.program_id(1) == 0)
        def _():
            dmp_ref[...] = jnp.zeros_like(dmp_ref)
            dmn_ref[...] = jnp.zeros_like(dmn_ref)

        dmp_ref[...] += jnp.sum(d * (zp - z), axis=0, keepdims=True)
        dmn_ref[...] += jnp.sum(d * (zn - z), axis=0, keepdims=True)

    blk = BS((1, t, 128), lambda j, i: (i, 0, j))
    vec = BS((1, 128), lambda j, i: (0, j))
    return pl.pallas_call(
        body, grid=(nc, b), in_specs=[blk, blk, vec, vec], out_specs=[blk, vec, vec],
        out_shape=[SDS((b, t, c), BF16), SDS((1, c), F32), SDS((1, c), F32)],
        compiler_params=_cp("parallel", "arbitrary"), name="shift_bwd",
    )(dzs3, z3, mu_p, mu_n)


def _const(shape):
    nd = len(shape)
    return BS(shape, lambda i: (0,) * nd)


def _prep_fwd(zs, p):
    m = zs.shape[0]
    tr = 512
    params = [p["w0"], p["w2"], p["a0"], p["a2"], p["g2"], p["k_k"], p["k_a"], p["ones_bd"]]

    def body(zs_ref, w0, w2, a0, a2, g2, kk_, ka_, bd, wf, wb, kf, kb, kk, kaf, kab, g):
        outs = _prep_fn(zs_ref[...], w0[...], w2[...], a0[...], a2[...], g2[...], kk_[...], ka_[...], bd[...])
        for ref, val in zip((wf, wb, kf, kb, kk, kaf, kab, g), outs):
            ref[...] = val

    row = BS((tr, 512), lambda i: (i, 0))
    return pl.pallas_call(
        body, grid=(m // tr,), in_specs=[BS((tr, RWKV_COLS), lambda i: (i, 0))] + [_const(q.shape) for q in params],
        out_specs=[row] * 8, out_shape=[SDS((m, 512), F32)] * 8, compiler_params=_cp("parallel"), name="rwkv_prep_fwd",
    )(zs, *params)


def _prep_bwd(zs, p, ct_rows):
    m = zs.shape[0]
    tr = 256
    params = [p["w0"], p["w2"], p["a0"], p["a2"], p["g2"], p["k_k"], p["k_a"]]
    names = ["dwf", "dwb", "dkf", "dkb", "dkk_f", "dkk_b", "dkaf", "dkab", "dr_f", "dr_b", "dr_p", "dk_p", "dv_p", "dg",
             "dv_f", "dv_b"]
    rows = [ct_rows[n] for n in names]

    def body(zs_ref, w0, w2, a0, a2, g2, kk_, ka_, bd, *rest):
        c = {n: r[...] for n, r in zip(names, rest[:len(names)])}
        outs = rest[len(names):]
        dzs_ref, grads = outs[0], outs[1:]
        ones_bd = bd[...]
        _, vjp = jax.vjp(
            lambda *q: _prep_fn(*q, ones_bd), zs_ref[...], w0[...], w2[...], a0[...], a2[...], g2[...], kk_[...], ka_[...]
        )
        cts = (c["dwf"], c["dwb"], c["dkf"] + c["dk_p"], c["dkb"] + c["dk_p"], c["dkk_f"] + c["dkk_b"], c["dkaf"], c["dkab"], c["dg"])
        dzs, *dparams = vjp(cts)
        dr = c["dr_f"] + c["dr_b"] + c["dr_p"]
        dv = c["dv_f"] + c["dv_b"] + c["dv_p"]
        dzs_ref[:, 0:512] = dzs[:, 0:512] + dr
        dzs_ref[:, 512:1024] = dzs[:, 512:1024]
        dzs_ref[:, 1024:1536] = dzs[:, 1024:1536] + dv
        dzs_ref[:, 1536:1920] = dzs[:, 1536:1920]

        @pl.when(pl.program_id(0) == 0)
        def _():
            for gr in grads:
                gr[...] = jnp.zeros_like(gr)

        for gr, val in zip(grads, dparams):
            gr[...] += val

    row = BS((tr, 512), lambda i: (i, 0))
    return pl.pallas_call(
        body, grid=(m // tr,),
        in_specs=[BS((tr, RWKV_COLS), lambda i: (i, 0))] + [_const(q.shape) for q in params] + [_const(p["ones_bd"].shape)]
        + [row] * len(names),
        out_specs=[BS((tr, RWKV_COLS), lambda i: (i, 0))] + [_const(q.shape) for q in params],
        out_shape=[SDS((m, RWKV_COLS), F32)] + [SDS(q.shape, F32) for q in params],
        compiler_params=_cp("arbitrary"), name="rwkv_prep_bwd",
    )(zs, *params, p["ones_bd"], *rows)


def _post_specs(tr):
    r = BS((tr, 512), lambda i: (i, 0))
    v = BS((tr, 512), lambda i: (i, 2))
    row = BS((tr, 512), lambda i: (i, 0))
    return r, v, row


def _post_fwd(y_f, y_b, zs, kf, kb, g, p):
    m = zs.shape[0]
    tr = 512
    r, v, row = _post_specs(tr)
    vecs = [p["r_k"], p["ln_x_g"], p["ln_x_b"], p["ones_bd"]]

    def body(yf, yb, r_ref, v_ref, kf_ref, kb_ref, g_ref, rk, lg, lb, bd, o_ref):
        o_ref[...] = _post_fn(
            yf[...] + yb[...], r_ref[...], kf_ref[...], kb_ref[...], v_ref[...], g_ref[...], rk[...], lg[...], lb[...], bd[...]
        ).astype(BF16)

    return pl.pallas_call(
        body, grid=(m // tr,), in_specs=[row, row, r, v, row, row, row] + [_const(q.shape) for q in vecs],
        out_specs=row, out_shape=SDS((m, 512), BF16), compiler_params=_cp("parallel"), name="rwkv_post_fwd",
    )(y_f, y_b, zs, zs, kf, kb, g, *vecs)


def _post_bwd(y_f, y_b, zs, kf, kb, g, p, dymix):
    m = zs.shape[0]
    tr = 256
    r, v, row = _post_specs(tr)
    vecs = [p["r_k"], p["ln_x_g"], p["ln_x_b"]]

    def body(yf, yb, r_ref, v_ref, kf_ref, kb_ref, g_ref, rk, lg, lb, bd, dy_ref, dyo, dr, dk, dv, dg, drk, dlg, dlb):
        ones_bd = bd[...]
        _, vjp = jax.vjp(
            lambda *q: _post_fn(*q, ones_bd),
            yf[...] + yb[...], r_ref[...], kf_ref[...], kb_ref[...], v_ref[...], g_ref[...], rk[...], lg[...], lb[...],
        )
        c_y, c_r, c_kf, _, c_v, c_g, c_rk, c_lg, c_lb = vjp(dy_ref[...])
        dyo[...] = c_y
        dr[...] = c_r
        dk[...] = c_kf
        dv[...] = c_v
        dg[...] = c_g

        @pl.when(pl.program_id(0) == 0)
        def _():
            for ref in (drk, dlg, dlb):
                ref[...] = jnp.zeros_like(ref)

        drk[...] += c_rk
        dlg[...] += c_lg
        dlb[...] += c_lb

    vec = _const((1, 512))
    return pl.pallas_call(
        body, grid=(m // tr,),
        in_specs=[row, row, r, v, row, row, row] + [_const(q.shape) for q in vecs] + [_const(p["ones_bd"].shape), row],
        out_specs=[row, row, row, row, row, vec, vec, vec],
        out_shape=[SDS((m, 512), F32)] * 5 + [SDS((1, 512), F32)] * 3,
        compiler_params=_cp("arbitrary"), name="rwkv_post_bwd",
    )(y_f, y_b, zs, zs, kf, kb, g, *vecs, p["ones_bd"], dymix)


SCAN_MXU_GROUPS = 2


def _half_ones():
    ri = lax.broadcasted_iota(jnp.int32, (128, 128), 0)
    ci = lax.broadcasted_iota(jnp.int32, (128, 128), 1)
    return jnp.where((ri < 64) == (ci < 64), 1.0, 0.0).astype(BF16)


def _half_sums(xs, ones):
    out = []
    per = -(-len(xs) // SCAN_MXU_GROUPS)
    for g in range(0, len(xs), per):
        part = xs[g:g + per]
        res = jnp.dot(jnp.concatenate(part, axis=0).astype(BF16), ones, preferred_element_type=F32)
        out += [res[64 * i:64 * i + 64] for i in range(len(part))]
    return out


def _scan_specs(b, t):
    nc = t // SCAN_CHUNK
    up, down = (lambda c: c), (lambda c: nc - 1 - c)
    rows = [BS((b, SCAN_CHUNK, 512), lambda c, ci=ci: (0, ci(c), 0)) for ci in (up, down)]
    vrows = [BS((b, SCAN_CHUNK, 512), lambda c, ci=ci: (0, ci(c), 2)) for ci in (up, down)]
    hist = [BS((SCAN_CHUNK, b * 4, 64, 128), lambda c, ci=ci: (ci(c), 0, 0, 0)) for ci in (up, down)]
    return nc, rows, vrows, hist


class _Window:
    def __init__(self, g, ascending):
        self.bases = [pl.multiple_of(g * 8, 8) if asc else pl.multiple_of(SCAN_CHUNK - 8 - g * 8, 8) for asc in ascending]
        self.ascending = ascending
        self.blocks = {}
        self.row_id = lax.broadcasted_iota(jnp.int32, (8, 128), 0)

    def j(self, d, s):
        return s if self.ascending[d] else 7 - s

    def time(self, d, s):
        return self.bases[d] + self.j(d, s)

    def row(self, ref, d, bi, cols, s):
        key = (id(ref), d, bi, cols.start)
        if key not in self.blocks:
            self.blocks[key] = ref[bi, pl.ds(self.bases[d], 8), cols]
        jj = self.j(d, s)
        return self.blocks[key][jj:jj + 1, :]

    def put(self, buf, key, d, s, row):
        prev = buf.get(key)
        new = jnp.broadcast_to(row, (8, 128))
        buf[key] = new if prev is None else jnp.where(self.row_id == self.j(d, s), new, prev)

    def flush(self, buf, refs_of):
        for key, val in buf.items():
            ref, d, bi, cols = refs_of(key)
            ref[bi, pl.ds(self.bases[d], 8), cols] = val


def _pairs(b):
    return [(bi * 4 + p, bi, slice(128 * p, 128 * p + 128)) for bi in range(b) for p in range(4)]


def _colsum(x):
    return jnp.sum(x, axis=0, keepdims=True)


def _pair_matvec(row, mat):
    rid = lax.broadcasted_iota(jnp.int32, (8, 64), 0)
    lhs = jnp.where(rid == 0, row[:, :64], jnp.where(rid == 1, row[:, 64:], 0.0))
    out = jnp.dot(lhs.astype(BF16), mat.astype(BF16), preferred_element_type=F32)
    lo = lax.broadcasted_iota(jnp.int32, (1, 128), 1) < 64
    return jnp.where(lo, out[0:1], out[1:2])


def _eye_mask():
    return (lax.broadcasted_iota(jnp.int32, (64, 128), 1) & 63) == lax.broadcasted_iota(jnp.int32, (64, 128), 0)


def _scan_fwd(zs, kk, ops_f, ops_b, shard):
    b, t = zs.shape[:2]
    nc, rows, vrows, hist = _scan_specs(b, t)
    npair = b * 4

    def body(*refs):
        ins, shard_ref, outs, s_ref = refs[:12], refs[12], refs[13:17], refs[17]
        gather = (shard_ref, *refs[18:21])
        dirs = [dict(zip(("r", "kk", "v", "w", "k", "ka", "y", "h"), (*ins[6 * d:6 * d + 6], *outs[2 * d:2 * d + 2])))
                for d in (0, 1)]

        @pl.when(pl.program_id(0) == 0)
        def _():
            s_ref[...] = jnp.zeros_like(s_ref)
            _gather_halves(*gather, "start")

        @pl.when(pl.program_id(0) == nc - 1)
        def _():
            _gather_halves(*gather, "finish")

        ones, eye = _half_ones(), _eye_mask()
        chains = [(d, pr, bi, cols) for d in (0, 1) for pr, bi, cols in _pairs(b)]

        def eight_steps(g, carry):
            win = _Window(g, (True, False))
            ybuf = {}
            for s in range(8):
                s_prev, xa = [], []
                for d, pr, bi, cols in chains:
                    q = dirs[d]
                    st = s_ref[d * npair + pr]
                    q["h"][win.time(d, s), pr] = st
                    s_prev.append(st)
                    xa += [st * win.row(q["kk"], d, bi, cols, s), jnp.where(eye, win.row(q["v"], d, bi, cols, s), 0.0)]
                ra = _half_sums(xa, ones)
                xb = []
                for i, (d, pr, bi, cols) in enumerate(chains):
                    q = dirs[d]
                    s_new = s_prev[i] * win.row(q["w"], d, bi, cols, s) - ra[2 * i] * win.row(q["ka"], d, bi, cols, s) \
                        + ra[2 * i + 1] * win.row(q["k"], d, bi, cols, s)
                    s_ref[d * npair + pr] = s_new
                    xb.append(s_new * win.row(q["r"], d, bi, cols, s))
                rb = _half_sums(xb, ones)
                for i, (d, pr, bi, cols) in enumerate(chains):
                    win.put(ybuf, i, d, s, _colsum(jnp.where(eye, rb[i], 0.0)))
            win.flush(ybuf, lambda i: (dirs[chains[i][0]]["y"], chains[i][0], chains[i][2], chains[i][3]))
            return carry

        def sixteen_steps(g2, carry):
            return eight_steps(2 * g2 + 1, eight_steps(2 * g2, carry))

        lax.fori_loop(0, SCAN_CHUNK // 16, sixteen_steps, 0)

    row_shape, hist_shape = SDS((b, t, 512), F32), SDS((t, npair, 64, 128), F32)
    state = (2 * npair, 64, 128)
    return pl.pallas_call(
        body, grid=(nc,), in_specs=sum(([rows[d], rows[d], vrows[d]] + [rows[d]] * 3 for d in (0, 1)), []) + [ANY],
        out_specs=[rows[0], hist[0], rows[1], hist[1], BS(state, lambda c: (0, 0, 0)), ANY],
        out_shape=[row_shape, hist_shape, row_shape, hist_shape, SDS(state, F32), _gathered_shape(shard)],
        scratch_shapes=_gather_sems(), compiler_params=_cp("arbitrary"), name="wkv_scan",
    )(zs, kk, zs, *ops_f, zs, kk, zs, *ops_b, shard)


def _scan_bwd(zs, kk, dy, ops_f, hist_f, ops_b, hist_b, s_last, partials):
    b, t = zs.shape[:2]
    nc, rows, vrows, hist = _scan_specs(b, t)
    npair = b * 4
    names_in = ("r", "kk", "v", "dy", "w", "k", "ka", "h")
    names_out = ("dr", "dw", "dk", "dkk", "dka", "dv")

    def body(*refs):
        ins, last_ref, part_ref, outs, recv_ref = refs[:16], refs[16], refs[17], refs[18:30], refs[30]
        ds_ref, after_ref = refs[31], refs[32]
        scatter = (part_ref, recv_ref, refs[33], refs[34])
        dirs = [dict(zip(names_in + names_out, (*ins[8 * d:8 * d + 8], *outs[6 * d:6 * d + 6]))) for d in (0, 1)]

        @pl.when(pl.program_id(0) == 0)
        def _():
            ds_ref[...] = jnp.zeros_like(ds_ref)
            after_ref[...] = last_ref[...]
            _scatter_to_all(*scatter, "start")

        @pl.when(pl.program_id(0) == nc - 1)
        def _():
            _scatter_to_all(*scatter, "finish")

        ones, eye = _half_ones(), _eye_mask()
        chains = [(d, pr, bi, cols) for d in (0, 1) for pr, bi, cols in _pairs(b)]

        def eight_steps(g, carry):
            win = _Window(g, (False, True))
            obuf = {}
            s_after = [after_ref[d * npair + pr] for d, pr, _, _ in chains]
            for s in range(8):
                row = lambda name, d, bi, cols: win.row(dirs[d][name], d, bi, cols, s)
                s_prev, xa = [], []
                for d, pr, bi, cols in chains:
                    st = dirs[d]["h"][win.time(d, s), pr]
                    s_prev.append(st)
                    xa += [st * row("kk", d, bi, cols), jnp.where(eye, row("dy", d, bi, cols), 0.0)]
                ra = _half_sums(xa, ones)
                ds_now, xb = [], []
                for i, (d, pr, bi, cols) in enumerate(chains):
                    skk, dycol = ra[2 * i], ra[2 * i + 1]
                    ds = ds_ref[d * npair + pr] + dycol * row("r", d, bi, cols)
                    win.put(obuf, (i, "dr"), d, s, _pair_matvec(row("dy", d, bi, cols), s_after[i]))
                    win.put(obuf, (i, "dk"), d, s, _pair_matvec(row("v", d, bi, cols), ds))
                    win.put(obuf, (i, "dka"), d, s, -_colsum(ds * skk))
                    win.put(obuf, (i, "dw"), d, s, _colsum(ds * s_prev[i]))
                    ds_now.append(ds)
                    xb += [ds * row("k", d, bi, cols), ds * row("ka", d, bi, cols)]
                rb = _half_sums(xb, ones)
                for i, (d, pr, bi, cols) in enumerate(chains):
                    dskk_neg = rb[2 * i + 1]
                    win.put(obuf, (i, "dv"), d, s, _colsum(jnp.where(eye, rb[2 * i], 0.0)))
                    win.put(obuf, (i, "dkk"), d, s, -_colsum(s_prev[i] * dskk_neg))
                    ds_ref[d * npair + pr] = ds_now[i] * row("w", d, bi, cols) - dskk_neg * row("kk", d, bi, cols)
                s_after = s_prev
            for i, (d, pr, _, _) in enumerate(chains):
                after_ref[d * npair + pr] = s_after[i]
            win.flush(obuf, lambda key: (dirs[chains[key[0]][0]][key[1]], chains[key[0]][0], chains[key[0]][2], chains[key[0]][3]))
            return carry

        def sixteen_steps(g2, carry):
            return eight_steps(2 * g2 + 1, eight_steps(2 * g2, carry))

        lax.fori_loop(0, SCAN_CHUNK // 16, sixteen_steps, 0)

    row_shape = SDS((b, t, 512), F32)
    state = (2 * npair, 64, 128)
    return pl.pallas_call(
        body, grid=(nc,),
        in_specs=sum(([rows[d], rows[d], vrows[d]] + [rows[d]] * 4 + [hist[d]] for d in (1, 0)), [])
        + [BS(state, lambda c: (0, 0, 0)), ANY],
        out_specs=[rows[1]] * 6 + [rows[0]] * 6 + [ANY],
        out_shape=[row_shape] * 12 + [SDS((7, partials.shape[1] // 2, 128), partials.dtype)],
        scratch_shapes=[pltpu.VMEM(state, F32), pltpu.VMEM(state, F32), pltpu.SemaphoreType.DMA((7,)),
                        pltpu.SemaphoreType.DMA((7,))],
        compiler_params=_cp("arbitrary"), name="wkv_scan_bwd",
    )(zs, kk, zs, dy, *ops_f, hist_f, zs, kk, zs, dy, *ops_b, hist_b, s_last, partials)


def _mla_fwd(zm, cs, sn, p, t):
    m = zm.shape[0]
    tr = 512
    per = t // tr
    params = [p["q_norm_g"], p["kv_norm_g"], p["wq"], p["wqr"], p["wk"], p["wv"]]

    def body(z_ref, cs_ref, sn_ref, gq, gkv, wq, wqr, wk, wv, q_ref, k_ref, v_ref):
        q, kf, v = _mla_fn(z_ref[...], cs_ref[...], sn_ref[...], gq[...], gkv[...], wq[...], wqr[...], wk[...], wv[...])
        q_ref[...] = q.astype(BF16)
        k_ref[...] = kf.astype(BF16)
        v_ref[...] = v.astype(BF16)

    tab = BS((tr, 128), lambda i: (i % per, 0))
    return pl.pallas_call(
        body, grid=(m // tr,), in_specs=[BS((tr, MLA_PAD_COLS), lambda i: (i, 0)), tab, tab] + [_const(q.shape) for q in params],
        out_specs=[BS((tr, 1024), lambda i: (i, 0)), BS((tr, 1024), lambda i: (i, 0)), BS((tr, 512), lambda i: (i, 0))],
        out_shape=[SDS((m, 1024), BF16), SDS((m, 1024), BF16), SDS((m, 512), BF16)], compiler_params=_cp("parallel"), name="mla_prep_fwd",
    )(zm, cs, sn, *params)


def _mla_bwd(zm, cs, sn, p, t, dq, dk, dv):
    m = zm.shape[0]
    tr = 256
    per = t // tr
    params = [p["q_norm_g"], p["kv_norm_g"], p["wq"], p["wqr"], p["wk"], p["wv"]]

    def body(z_ref, cs_ref, sn_ref, gq, gkv, wq, wqr, wk, wv, dq_ref, dk_ref, dv_ref, dz_ref, *grads):
        cs_v, sn_v = cs_ref[...], sn_ref[...]
        _, vjp = jax.vjp(
            lambda *q: _mla_fn(q[0], cs_v, sn_v, *q[1:]), z_ref[...], gq[...], gkv[...], wq[...], wqr[...], wk[...], wv[...]
        )
        dz, *dparams = vjp((dq_ref[...], dk_ref[...], dv_ref[...]))
        dz_ref[...] = dz.astype(BF16)

        @pl.when(pl.program_id(0) == 0)
        def _():
            for gr in grads:
                gr[...] = jnp.zeros_like(gr)

        for gr, val in zip(grads, dparams):
            gr[...] += val

    tab = BS((tr, 128), lambda i: (i % per, 0))
    wide = BS((tr, 1024), lambda i: (i, 0))
    return pl.pallas_call(
        body, grid=(m // tr,),
        in_specs=[BS((tr, MLA_PAD_COLS), lambda i: (i, 0)), tab, tab] + [_const(q.shape) for q in params]
        + [wide, wide, BS((tr, 512), lambda i: (i, 0))],
        out_specs=[BS((tr, MLA_PAD_COLS), lambda i: (i, 0))] + [_const(q.shape) for q in params],
        out_shape=[SDS((m, MLA_PAD_COLS), BF16)] + [SDS(q.shape, F32) for q in params],
        compiler_params=_cp("arbitrary"), name="mla_prep_bwd",
    )(zm, cs, sn, *params, dq, dk, dv)


_NT = (((1,), (1,)), ((), ()))
_TN = (((0,), (0,)), ((), ()))


def _attn_fwd(q, kf, v, b, t):
    m = q.shape[0]
    tq = 256
    nq = t // tq

    def body(q_ref, k_ref, v_ref, o_ref, l_ref):
        lo = lax.broadcasted_iota(jnp.int32, (1, 128), 1) < 64
        v_all = v_ref[...]
        o = jnp.zeros((tq, 128), F32)
        lse = []
        for h in range(2):
            hs = slice(128 * h, 128 * h + 128)
            s = lax.dot_general(q_ref[:, hs], k_ref[:, hs], _NT, preferred_element_type=F32)
            mx = jnp.max(s, axis=1, keepdims=True)
            e = jnp.exp(s - mx)
            den = jnp.sum(e, axis=1, keepdims=True)
            vh = jnp.where(lo if h == 0 else jnp.logical_not(lo), v_all, jnp.zeros_like(v_all))
            o = o + jnp.dot(e.astype(BF16), vh, preferred_element_type=F32) / den
            lse.append(mx + jnp.log(den))
        o_ref[...] = o
        l_ref[...] = jnp.where(lo, lse[0], lse[1])

    return pl.pallas_call(
        body, grid=(b, 4, nq),
        in_specs=[BS((tq, 256), lambda bi, hp, i: (bi * nq + i, hp)), BS((t, 256), lambda bi, hp, i: (bi, hp)),
                  BS((t, 128), lambda bi, hp, i: (bi, hp))],
        out_specs=[BS((tq, 128), lambda bi, hp, i: (bi * nq + i, hp))] * 2,
        out_shape=[SDS((m, 512), F32), SDS((m, 512), F32)], compiler_params=_cp("parallel", "parallel", "arbitrary"), name="attn_fwd",
    )(q, kf, v)


def _attn_bwd(q, kf, v, o, lse, do, b, t):
    m = q.shape[0]
    tq = 256
    nq = t // tq

    def body(q_ref, k_ref, v_ref, o_ref, l_ref, do_ref, dq_ref, dk_ref, dv_ref):
        lo = lax.broadcasted_iota(jnp.int32, (1, 128), 1) < 64

        @pl.when(pl.program_id(2) == 0)
        def _():
            dk_ref[...] = jnp.zeros_like(dk_ref)
            dv_ref[...] = jnp.zeros_like(dv_ref)

        v_all, o_all, l_all, do_all = v_ref[...], o_ref[...], l_ref[...], do_ref[...]
        dv_acc = jnp.zeros((t, 128), F32)
        for h in range(2):
            hs = slice(128 * h, 128 * h + 128)
            mask = lo if h == 0 else jnp.logical_not(lo)
            qh, kh = q_ref[:, hs], k_ref[:, hs]
            s = lax.dot_general(qh, kh, _NT, preferred_element_type=F32)
            lse_h = jnp.max(jnp.where(mask, l_all, -jnp.inf), axis=1, keepdims=True)
            pr = jnp.exp(s - lse_h)
            do_h = jnp.where(mask, do_all, 0.0)
            dp = lax.dot_general(do_h.astype(BF16), v_all, _NT, preferred_element_type=F32)
            dsum = jnp.sum(do_h * o_all, axis=1, keepdims=True)
            ds = (pr * (dp - dsum)).astype(BF16)
            dq_ref[:, hs] = jnp.dot(ds, kh, preferred_element_type=F32)
            dk_ref[:, hs] += lax.dot_general(ds, qh, _TN, preferred_element_type=F32)
            dv_acc = dv_acc + lax.dot_general(pr.astype(BF16), do_h.astype(BF16), _TN, preferred_element_type=F32)
        dv_ref[...] += dv_acc

    qspec = BS((tq, 256), lambda bi, hp, i: (bi * nq + i, hp))
    kspec = BS((t, 256), lambda bi, hp, i: (bi, hp))
    vspec = BS((t, 128), lambda bi, hp, i: (bi, hp))
    ospec = BS((tq, 128), lambda bi, hp, i: (bi * nq + i, hp))
    return pl.pallas_call(
        body, grid=(b, 4, nq), in_specs=[qspec, kspec, vspec, ospec, ospec, ospec], out_specs=[qspec, kspec, vspec],
        out_shape=[SDS((m, 1024), F32), SDS((m, 1024), F32), SDS((m, 512), F32)],
        compiler_params=_cp("parallel", "parallel", "arbitrary"), name="attn_bwd",
    )(q, kf, v, o, lse, do)


def _conv3(u, w_ref, b_ref, t):
    up, un = _prev_next(u, t)
    return w_ref[0:1, :] * up + w_ref[1:2, :] * u + w_ref[2:3, :] * un + b_ref[...], up, un


def _ffn_mid_fwd(ug3, uv3, cw, cb):
    b, t, f = ug3.shape
    nc = f // 256

    def body(ug_ref, uv_ref, wg_ref, wv_ref, bg_ref, bv_ref, a_ref):
        gc, _, _ = _conv3(ug_ref[0], wg_ref, bg_ref, t)
        vc, _, _ = _conv3(uv_ref[0], wv_ref, bv_ref, t)
        a_ref[0] = (gc * jax.nn.sigmoid(gc) * vc).astype(BF16)

    blk = BS((1, t, 256), lambda i, j: (i, 0, j))
    return pl.pallas_call(
        body, grid=(b, nc),
        in_specs=[blk, blk, BS((3, 256), lambda i, j: (0, j)), BS((3, 256), lambda i, j: (0, j + nc)),
                  BS((1, 256), lambda i, j: (0, j)), BS((1, 256), lambda i, j: (0, j + nc))],
        out_specs=blk, out_shape=SDS((b, t, f), BF16), compiler_params=_cp("parallel", "parallel"), name="ffn_mid_fwd",
    )(ug3, uv3, cw, cw, cb, cb)


def _ffn_mid_bwd(ug3, uv3, cw, cb, da3):
    b, t, f = ug3.shape
    nc = f // 256

    def half(u, up, un, dc, w_ref):
        dprev, dnext = _prev_next(dc, t)
        du = w_ref[1:2, :] * dc + w_ref[0:1, :] * dnext + w_ref[2:3, :] * dprev
        sums = [jnp.sum(dc * q, axis=0, keepdims=True) for q in (up, u, un)] + [jnp.sum(dc, axis=0, keepdims=True)]
        row = lax.broadcasted_iota(jnp.int32, (8, 256), 0)
        tab = jnp.zeros((8, 256), F32)
        for i, s in enumerate(sums):
            tab = jnp.where(row == i, s, tab)
        return du, tab

    def body(ug_ref, uv_ref, wg_ref, wv_ref, bg_ref, bv_ref, da_ref, dug_ref, duv_ref, tg_ref, tv_ref):
        ug, uv, da = ug_ref[0], uv_ref[0], da_ref[0]
        gc, gp, gn = _conv3(ug, wg_ref, bg_ref, t)
        vc, vp, vn = _conv3(uv, wv_ref, bv_ref, t)
        sg = jax.nn.sigmoid(gc)
        d_gc = da * vc * (sg * (1.0 + gc * (1.0 - sg)))
        d_vc = da * (gc * sg)
        dug, tg = half(ug, gp, gn, d_gc, wg_ref)
        duv, tv = half(uv, vp, vn, d_vc, wv_ref)
        dug_ref[0] = dug.astype(BF16)
        duv_ref[0] = duv.astype(BF16)

        @pl.when(pl.program_id(1) == 0)
        def _():
            tg_ref[...] = jnp.zeros_like(tg_ref)
            tv_ref[...] = jnp.zeros_like(tv_ref)

        tg_ref[...] += tg
        tv_ref[...] += tv

    blk = BS((1, t, 256), lambda j, i: (i, 0, j))
    tab = BS((8, 256), lambda j, i: (0, j))
    return pl.pallas_call(
        body, grid=(nc, b),
        in_specs=[blk, blk, BS((3, 256), lambda j, i: (0, j)), BS((3, 256), lambda j, i: (0, j + nc)),
                  BS((1, 256), lambda j, i: (0, j)), BS((1, 256), lambda j, i: (0, j + nc)), blk],
        out_specs=[blk, blk, tab, tab],
        out_shape=[SDS((b, t, f), BF16), SDS((b, t, f), BF16), SDS((8, f), F32), SDS((8, f), F32)],
        compiler_params=_cp("parallel", "arbitrary"), name="ffn_mid_bwd",
    )(ug3, uv3, cw, cw, cb, cb, da3)


def _add_rows(parts, name, out_dtype=F32):
    r = parts[0].shape[0]
    tr = _row_tile(r, 1024)
    n = len(parts)

    def body(*refs):
        acc = refs[0][...].astype(F32)
        for q in refs[1:n]:
            acc = acc + q[...].astype(F32)
        refs[n][...] = acc.astype(out_dtype)

    row = BS((tr, 128), lambda i: (i, 0))
    return pl.pallas_call(
        body, grid=(r // tr,), in_specs=[row] * n, out_specs=row, out_shape=SDS((r, 128), out_dtype),
        compiler_params=_cp("parallel"), name=name,
    )(*parts)


def _adamw(w, g, m, v, name):
    lead = w.shape[:-2]
    r, c = w.shape[-2:]
    tr = _row_tile(r)

    def body(w_ref, g_ref, m_ref, v_ref, d_ref, m2_ref, v2_ref):
        d, m2, v2 = _adamw_math(w_ref[...], g_ref[...], m_ref[...], v_ref[...])
        d_ref[...] = d
        m2_ref[...] = m2
        v2_ref[...] = v2

    blk = BS((1,) * len(lead) + (tr, c), lambda i: (0,) * len(lead) + (i, 0))
    return pl.pallas_call(
        body, grid=(r // tr,), in_specs=[blk] * 4, out_specs=[blk] * 3, out_shape=[SDS(w.shape, F32)] * 3,
        compiler_params=_cp("parallel"), name=name,
    )(w, g, m, v)


def _place():
    return lax.axis_index("x"), lax.axis_index("y"), lax.axis_index("c")


def _flip(v, bit):
    return 1 - v if bit else v


def _allgather_weights(shard):
    def body(x_ref, out_ref, send_sems, recv_sems):
        _gather_halves(x_ref, out_ref, send_sems, recv_sems, "start")
        _gather_halves(x_ref, out_ref, send_sems, recv_sems, "finish")

    return pl.pallas_call(
        body, out_shape=_gathered_shape(shard), in_specs=[ANY], out_specs=ANY, scratch_shapes=_gather_sems(),
        name="allgather_weights",
    )(shard)


def _gathered_shape(shard):
    return SDS((8 * (shard.shape[0] // 2), 128), shard.dtype)


def _gather_sems():
    return [pltpu.SemaphoreType.DMA((6,)), pltpu.SemaphoreType.DMA((6,))]


def _gather_halves(x_ref, out_ref, send_sems, recv_sems, phase):
    rh = x_ref.shape[0] // 2
    x, y, c = _place()
    me, sibling = (x, y, c), (x, y, 1 - c)
    chips = [(1 - x, y), (x, 1 - y), (1 - x, 1 - y)]
    mine_src = x_ref.at[pl.ds(c * rh, rh), :]

    def rows(px, py, pc):
        return out_ref.at[pl.ds((4 * px + 2 * py + pc) * rh, rh), :]

    def copy(k, block, to, src=None):
        return pltpu.make_async_remote_copy(
            src_ref=rows(*block) if src is None else src, dst_ref=rows(*block), send_sem=send_sems.at[k],
            recv_sem=recv_sems.at[k], device_id=to, device_id_type=MESH,
        )

    first = [copy(j, me, (*chip, c), src=mine_src) for j, chip in enumerate(chips)]
    if phase == "start":
        for cp in first:
            cp.start()
        return
    passed = [copy(3 + j, (*chip, c), sibling) for j, chip in enumerate(chips)]
    for j, chip in enumerate(chips):
        copy(j, (*chip, c), me).wait_recv()
        passed[j].start()
    for j, chip in enumerate(chips):
        copy(3 + j, (*chip, 1 - c), me).wait_recv()
    for cp in first + passed:
        cp.wait_send()


def _scatter_partials(g_ref, recv_ref, send_sems, recv_sems, phase):
    x, y, c = _place()
    copies = []
    for j, (fx, fy) in enumerate(((1, 0), (0, 1), (1, 1))):
        px, py = _flip(x, fx), _flip(y, fy)
        copies.append(pltpu.make_async_remote_copy(
            src_ref=g_ref.at[2 * px + py], dst_ref=recv_ref.at[j], send_sem=send_sems.at[j], recv_sem=recv_sems.at[j],
            device_id=(px, py, c), device_id_type=MESH,
        ))
    if phase == "start":
        for cp in copies:
            cp.start()
        return
    for cp in copies:
        cp.wait_recv()
    for cp in copies:
        cp.wait_send()


def _scatter_to_all(g_ref, recv_ref, send_sems, recv_sems, phase):
    rh = g_ref.shape[1] // 2
    x, y, c = _place()
    copies = []
    for k in range(1, 8):
        px, py, pc = _flip(x, k >> 2 & 1), _flip(y, k >> 1 & 1), _flip(c, k & 1)
        copies.append(pltpu.make_async_remote_copy(
            src_ref=g_ref.at[2 * px + py, pl.ds(pc * rh, rh), :], dst_ref=recv_ref.at[k - 1], send_sem=send_sems.at[k - 1],
            recv_sem=recv_sems.at[k - 1], device_id=(px, py, pc), device_id_type=MESH,
        ))
    if phase == "start":
        for cp in copies:
            cp.start()
        return
    for cp in copies:
        cp.wait_recv()
    for cp in copies:
        cp.wait_send()


def _scatter_to_chips(g):
    def body(g_ref, recv_ref, send_sems, recv_sems):
        _scatter_partials(g_ref, recv_ref, send_sems, recv_sems, "start")
        _scatter_partials(g_ref, recv_ref, send_sems, recv_sems, "finish")

    return pl.pallas_call(
        body, out_shape=SDS((3,) + g.shape[1:], g.dtype), in_specs=[ANY], out_specs=ANY,
        scratch_shapes=[pltpu.SemaphoreType.DMA((3,)), pltpu.SemaphoreType.DMA((3,))], name="scatter_grads",
    )(g)


def _send_to_sibling(a, half_of_rows):
    rh = a.shape[1] // 2

    def body(a_ref, b_ref, send_sem, recv_sem):
        x, y, c = _place()
        src = a_ref.at[:, pl.ds((1 - c) * rh, rh), :] if half_of_rows else a_ref
        cp = pltpu.make_async_remote_copy(
            src_ref=src, dst_ref=b_ref, send_sem=send_sem, recv_sem=recv_sem, device_id=(x, y, 1 - c), device_id_type=MESH
        )
        cp.start()
        cp.wait()

    shape = (a.shape[0], rh, 128) if half_of_rows else a.shape
    return pl.pallas_call(
        body, out_shape=SDS(shape, a.dtype), in_specs=[ANY], out_specs=ANY,
        scratch_shapes=[pltpu.SemaphoreType.DMA, pltpu.SemaphoreType.DMA],
        name="sibling_halves" if half_of_rows else "sibling_swap",
    )(a)


def _allreduce_small(v):
    r = v.shape[0]

    def body(v_ref, out_ref, buf_ref, send_sems, recv_sems):
        x, y, c = _place()
        buf_ref[0] = v_ref[...]
        copies = []
        for k in range(1, 8):
            peer = (_flip(x, k >> 2 & 1), _flip(y, k >> 1 & 1), _flip(c, k & 1))
            cp = pltpu.make_async_remote_copy(
                src_ref=v_ref, dst_ref=buf_ref.at[k], send_sem=send_sems.at[k - 1], recv_sem=recv_sems.at[k - 1],
                device_id=peer, device_id_type=MESH,
            )
            cp.start()
            copies.append(cp)
        for cp in copies:
            cp.wait_recv()
        acc = None
        for d in range(8):
            slot = 4 * _flip(x, d >> 2 & 1) + 2 * _flip(y, d >> 1 & 1) + _flip(c, d & 1)
            term = buf_ref[slot]
            acc = term if acc is None else acc + term
        out_ref[...] = acc
        for cp in copies:
            cp.wait_send()

    return pl.pallas_call(
        body, out_shape=SDS(v.shape, F32), in_specs=[VMEM], out_specs=VMEM,
        scratch_shapes=[pltpu.VMEM((8, r, 128), F32), pltpu.SemaphoreType.DMA((7,)), pltpu.SemaphoreType.DMA((7,))],
        name="allreduce_small",
    )(v)


_BIG_A = (
    ("w_in", 0, False),
    ("decay_w2_fwd", 1, False), ("decay_w2_bwd", 1, False), ("iclr_a2_fwd", 1, False),
    ("iclr_a2_bwd", 1, False), ("gate_g2", 1, False),
)
_BIG_B = (
    ("w_uq", 0, False), ("w_ukv", 1, False), ("w_out", 0, False), ("w_ffn_up", 1, False), ("ffn_conv_w", 1, True),
    ("w_ffn_down", 0, False),
)
_BIG = _BIG_A + _BIG_B
_SMALL = (
    "ln_mix_g", "shift_mu_prev", "shift_mu_next", "decay_w0_fwd", "decay_w0_bwd", "iclr_a0_fwd", "iclr_a0_bwd", "k_k",
    "k_a", "r_k", "ln_x_g", "ln_x_b", "q_norm_g", "kv_norm_g", "mla_out_g", "ln_ffn_g", "ffn_conv_b", "ln_final_g",
)
_WEIGHTS = (
    "ln_mix_g", "w_in", "shift_mu_prev", "shift_mu_next", "decay_w0_fwd", "decay_w2_fwd", "decay_w0_bwd", "decay_w2_bwd",
    "iclr_a0_fwd", "iclr_a2_fwd", "iclr_a0_bwd", "iclr_a2_bwd", "gate_g2", "k_k", "k_a", "r_k", "ln_x_g", "ln_x_b",
    "q_norm_g", "w_uq", "kv_norm_g", "w_ukv", "mla_out_g", "w_out", "ln_ffn_g", "w_ffn_up", "ffn_conv_w", "ffn_conv_b",
    "w_ffn_down", "ln_final_g",
)


def _pad_rows(flat, rows):
    return jnp.pad(flat, (0, rows * 128 - flat.shape[0])).reshape(rows, 128)


def _rows_for(n, mult):
    rows = -(-n // 128)
    return -(-rows // mult) * mult


def _pack_shards_bf16(arrs, entries):
    parts = []
    for name, _, raw in entries:
        w = arrs[name][0]
        flat = lax.bitcast_convert_type(w, BF16).reshape(-1) if raw else w.astype(BF16).reshape(-1)
        parts.append(_pad_rows(flat, _rows_for(flat.shape[0], 32)))
    return jnp.concatenate(parts, axis=0)


def _unpack_gathered(g4, shard, chip, arrs, entries):
    out, off = {}, 0
    mine = (jnp.arange(N_CHIPS) == chip)[:, None, None]
    for name, axis, raw in entries:
        a, b = arrs[name].shape[1:]
        n = a * b * (2 if raw else 1)
        rows = _rows_for(n, 32)
        seg = jnp.where(mine, shard[None, off:off + rows], g4[:, off:off + rows]).reshape(4, rows * 128)[:, :n]
        off += rows
        if raw:
            seg = lax.bitcast_convert_type(seg.reshape(4, a * b, 2), F32)
        seg = seg.reshape(4, a, b)
        out[name] = jnp.concatenate([seg[s] for s in range(4)], axis=1) if axis == 1 else seg.reshape(4 * a, b)
    return out


def _pack_grads(full, arrs, entries, dtype=F32):
    parts = []
    for name, axis, _ in entries:
        a, b = arrs[name].shape[1:]
        g = full[name]
        if g.ndim == 3:
            sh = g
        else:
            sh = g.reshape(a, 4, b).transpose(1, 0, 2) if axis == 1 else g.reshape(4, a, b)
        rows = _rows_for(a * b, 8)
        flat = sh.reshape(4, a * b).astype(dtype)
        parts.append(jnp.pad(flat, ((0, 0), (0, rows * 128 - a * b))).reshape(4, rows, 128))
    total = sum(q.shape[1] for q in parts)
    parts.append(jnp.zeros((4, -(-total // 1024) * 1024 - total, 128), dtype))
    return jnp.concatenate(parts, axis=1)


def _unpack_grads(g, arrs, entries):
    out, off = {}, 0
    for name, _, _ in entries:
        a, b = arrs[name].shape[1:]
        rows = _rows_for(a * b, 8)
        out[name] = g[off:off + rows].reshape(-1)[:a * b].reshape(1, a, b)
        off += rows
    return out


def _pack_small(vals):
    flat = jnp.concatenate([vals[n].reshape(-1).astype(F32) for n in _SMALL] + [vals["_loss"].reshape(-1)])
    return _pad_rows(flat, _rows_for(flat.shape[0], 8))


def _unpack_small(buf, arrs):
    flat, out, off = buf.reshape(-1), {}, 0
    for n in _SMALL:
        size = arrs[n].size
        out[n] = flat[off:off + size].reshape(arrs[n].shape)
        off += size
    out["_loss"] = flat[off]
    return out


def _rot_cols(w):
    return jnp.concatenate([-w[..., 16:], w[..., :16]], axis=-1)


def _rot_cols_t(g):
    return jnp.concatenate([g[..., 16:], -g[..., :16]], axis=-1)


def _rope_tables(t):
    inv = jnp.power(ROPE_THETA, -jnp.arange(0, ROPE_DIM, 2, dtype=F32) / ROPE_DIM)
    ang = jnp.arange(t, dtype=F32)[:, None] * inv[None, :]
    one, zero = jnp.ones((t, 64), F32), jnp.zeros((t, 64), F32)
    cs = jnp.concatenate([one, jnp.cos(ang), jnp.cos(ang), zero[:, :32]], axis=1)
    sn = jnp.concatenate([zero, jnp.sin(ang), jnp.sin(ang), zero[:, :32]], axis=1)
    return cs, sn


def _block_diag(a, b):
    za = jnp.zeros_like(a)
    return jnp.concatenate([jnp.concatenate([a, za], axis=1), jnp.concatenate([za, b], axis=1)], axis=0)


def kernel(x, ln_mix_g, w_in, shift_mu_prev, shift_mu_next, decay_w0_fwd, decay_w2_fwd, decay_w0_bwd, decay_w2_bwd, iclr_a0_fwd, iclr_a2_fwd, iclr_a0_bwd, iclr_a2_bwd, gate_g2, k_k, k_a, r_k, ln_x_g, ln_x_b, q_norm_g, w_uq, kv_norm_g, w_ukv, mla_out_g, w_out, ln_ffn_g, w_ffn_up, ffn_conv_w, ffn_conv_b, w_ffn_down, ln_final_g, loss_target, m_ln_mix_g, m_w_in, m_shift_mu_prev, m_shift_mu_next, m_decay_w0_fwd, m_decay_w2_fwd, m_decay_w0_bwd, m_decay_w2_bwd, m_iclr_a0_fwd, m_iclr_a2_fwd, m_iclr_a0_bwd, m_iclr_a2_bwd, m_gate_g2, m_k_k, m_k_a, m_r_k, m_ln_x_g, m_ln_x_b, m_q_norm_g, m_w_uq, m_kv_norm_g, m_w_ukv, m_mla_out_g, m_w_out, m_ln_ffn_g, m_w_ffn_up, m_ffn_conv_w, m_ffn_conv_b, m_w_ffn_down, m_ln_final_g, v_ln_mix_g, v_w_in, v_shift_mu_prev, v_shift_mu_next, v_decay_w0_fwd, v_decay_w2_fwd, v_decay_w0_bwd, v_decay_w2_bwd, v_iclr_a0_fwd, v_iclr_a2_fwd, v_iclr_a0_bwd, v_iclr_a2_bwd, v_gate_g2, v_k_k, v_k_a, v_r_k, v_ln_x_g, v_ln_x_b, v_q_norm_g, v_w_uq, v_kv_norm_g, v_w_ukv, v_mla_out_g, v_w_out, v_ln_ffn_g, v_w_ffn_up, v_ffn_conv_w, v_ffn_conv_b, v_w_ffn_down, v_ln_final_g):
    arrs = dict(locals())
    for pre in ("", "m_", "v_"):
        arrs[pre + "w_in"] = jnp.swapaxes(arrs[pre + "w_in"], 1, 2)
    b, t, d = x.shape
    m = b * t
    x2 = x.reshape(m, d)
    tgt = loss_target.reshape(m, d)
    vec = lambda n: arrs[n].reshape(1, -1)

    core = lax.axis_index("c")
    chip = 2 * lax.axis_index("x") + lax.axis_index("y")
    def unpack(gathered, shard, entries):
        return _unpack_gathered(gathered.reshape(N_CHIPS, -1, 128), shard, chip, arrs, entries)

    shard_a, shard_b = _pack_shards_bf16(arrs, _BIG_A), _pack_shards_bf16(arrs, _BIG_B)
    fw = unpack(_allgather_weights(shard_a), shard_a, _BIG_A)
    win = fw["w_in"]
    zc = jnp.zeros((64, d), BF16)
    w_kr = win[2944:2976]
    rot_kr = jnp.swapaxes(_rot_cols(jnp.swapaxes(w_kr, 0, 1)), 0, 1)
    win_m = jnp.concatenate([win[1920:2944], zc, w_kr, zc[:32], zc, rot_kr, zc[:32]], axis=0)
    win_r = win[:RWKV_COLS]
    head = jnp.arange(512) // HEAD_DIM
    rw = dict(
        w0=jnp.concatenate([vec("decay_w0_fwd"), vec("decay_w0_bwd")], axis=1),
        w2=_block_diag(fw["decay_w2_fwd"], fw["decay_w2_bwd"]).astype(F32),
        a0=jnp.concatenate([vec("iclr_a0_fwd"), vec("iclr_a0_bwd")], axis=1),
        a2=_block_diag(fw["iclr_a2_fwd"], fw["iclr_a2_bwd"]).astype(F32),
        g2=fw["gate_g2"].astype(F32), k_k=vec("k_k"), k_a=vec("k_a"), r_k=vec("r_k"), ln_x_g=vec("ln_x_g"), ln_x_b=vec("ln_x_b"),
        ones_bd=(head[:, None] == head[None, :]).astype(F32),
    )
    cs, sn = _rope_tables(t)

    n1 = _rms_fwd(x2, vec("ln_mix_g"), "rms_mix")
    zm = _mm(n1, win_m, "nt", "proj_in_mla")
    zr = _mm(n1, win_r, "nt", "proj_in_rwkv")
    zs = _shift_fwd(zr.reshape(b, t, RWKV_COLS), vec("shift_mu_prev"), vec("shift_mu_next"))
    zs2 = zs.reshape(m, RWKV_COLS)
    wf, wb, kf, kb, kk, kaf, kab, gate = _prep_fwd(zs2, rw)
    r4 = lambda a: a.reshape(b, t, 512)
    f2 = lambda a: a.reshape(m, 512)
    kk4 = r4(kk)
    ops_f = (r4(wf), r4(kf), r4(kaf))
    ops_b = (r4(wb), r4(kb), r4(kab))
    y_f, hist_f, y_b, hist_b, s_last, gathered_b = _scan_fwd(zs, kk4, ops_f, ops_b, shard_b)
    fw.update(unpack(gathered_b, shard_b, _BIG_B))
    uq = fw["w_uq"].astype(F32).reshape(Q_RANK, HEADS, 96)
    z32 = jnp.zeros((Q_RANK, HEADS, 32), F32)
    wq = jnp.concatenate([uq[..., :64], uq[..., 64:], z32], axis=-1).reshape(Q_RANK, 1024)
    wqr = jnp.concatenate([z32, z32, _rot_cols(uq[..., 64:]), z32], axis=-1).reshape(Q_RANK, 1024)
    ukv = fw["w_ukv"].astype(F32).reshape(KV_RANK, HEADS, 128)
    wk = jnp.concatenate([ukv[..., :64], jnp.zeros_like(ukv[..., :64])], axis=-1).reshape(KV_RANK, 1024)
    wv = ukv[..., 64:].reshape(KV_RANK, 512)
    mp = dict(q_norm_g=vec("q_norm_g"), kv_norm_g=vec("kv_norm_g"), wq=wq, wqr=wqr, wk=wk, wv=wv)
    w_up_g, w_up_v = fw["w_ffn_up"][:, :D_FF], fw["w_ffn_up"][:, D_FF:]
    cw, cb = fw["ffn_conv_w"], vec("ffn_conv_b")
    y_f, y_b = f2(y_f), f2(y_b)
    y_rwkv = _post_fwd(y_f, y_b, zs2, kf, kb, gate, rw)
    q, kfull, v = _mla_fwd(zm, cs, sn, mp, t)
    o, lse = _attn_fwd(q, kfull, v, b, t)
    y_mla = _rms_fwd(o, vec("mla_out_g"), "rms_mla_out")
    ymix = jnp.concatenate([y_rwkv, y_mla], axis=1)
    h1 = _mm(ymix, fw["w_out"], "nn", "proj_out", add=x2)
    n2 = _rms_fwd(h1, vec("ln_ffn_g"), "rms_ffn")
    u2 = _mm(n2, fw["w_ffn_up"], "nn", "ffn_up", column_blocks=D_FF)
    ug, uv = u2[0], u2[1]
    r3f = lambda a: a.reshape(b, t, D_FF)
    act = _ffn_mid_fwd(r3f(ug), r3f(uv), cw, cb).reshape(m, D_FF)
    h2 = _mm(act, fw["w_ffn_down"], "nn", "ffn_down", add=h1)
    loss_tab, dh2, g_ln_final = _final(h2, vec("ln_final_g"), tgt)

    gfull = {}
    dact = _mm(dh2, fw["w_ffn_down"], "nt", "d_ffn_act")
    gfull["w_ffn_down"] = _mm(act, dh2, "tn", "g_ffn_down")
    dug, duv, tab_g, tab_v = _ffn_mid_bwd(r3f(ug), r3f(uv), cw, cb, r3f(dact))
    dug, duv = dug.reshape(m, D_FF), duv.reshape(m, D_FF)
    gfull["ffn_conv_w"] = jnp.concatenate([tab_g[0:3], tab_v[0:3]], axis=1)
    g_conv_b = jnp.concatenate([tab_g[3:4], tab_v[3:4]], axis=1)
    dn2 = _mm(duv, w_up_v, "nt", "d_ffn_in_val", add=_mm(dug, w_up_g, "nt", "d_ffn_in_gate"))
    shard_cols = arrs["w_ffn_up"].shape[2]
    gfull["w_ffn_up"] = jnp.concatenate([_mm(n2, dug, "tn", "g_ffn_up_gate", column_blocks=shard_cols),
                                         _mm(n2, duv, "tn", "g_ffn_up_val", column_blocks=shard_cols)], axis=0)
    dh1, g_ln_ffn = _rms_bwd(h1, vec("ln_ffn_g"), dn2, "rms_ffn_bwd", dres=dh2)
    dymix = _mm(dh1, fw["w_out"], "nt", "d_mix")
    gfull["w_out"] = _mm(ymix, dh1, "tn", "g_w_out")
    do, g_mla_out = _rms_bwd(o, vec("mla_out_g"), dymix, "rms_mla_out_bwd", dy_block=1)
    dq, dk, dv = _attn_bwd(q, kfull, v, o, lse, do, b, t)
    dzm, g_qn, g_kvn, g_wq, g_wqr, g_wk, g_wv = _mla_bwd(zm, cs, sn, mp, t, dq, dk, dv)
    gq3, gqr3 = g_wq.reshape(Q_RANK, HEADS, 128), g_wqr.reshape(Q_RANK, HEADS, 128)
    gfull["w_uq"] = jnp.concatenate(
        [gq3[..., :64], gq3[..., 64:96] + _rot_cols_t(gqr3[..., 64:96])], axis=-1
    ).reshape(Q_RANK, HEADS * 96)
    gfull["w_ukv"] = jnp.concatenate(
        [g_wk.reshape(KV_RANK, HEADS, 128)[..., :64], g_wv.reshape(KV_RANK, HEADS, 64)], axis=-1
    ).reshape(KV_RANK, 1024)
    def cores_first(entries, tag):
        packed = _pack_grads(gfull, arrs, entries)
        rh = packed.shape[1] // 2
        own = lax.dynamic_slice_in_dim(packed, core * rh, rh, axis=1)
        sib = _send_to_sibling(packed, True)
        return _add_rows([own.reshape(4 * rh, 128), sib.reshape(4 * rh, 128)], "sum_cores_" + tag, BF16).reshape(4, rh, 128)

    def join_halves(half, entries):
        other = _send_to_sibling(half, False)
        lower = jnp.where(core == 0, half, other)
        upper = jnp.where(core == 0, other, half)
        return _unpack_grads(jnp.concatenate([lower, upper], axis=0), arrs, entries)

    part_b = _pack_grads(gfull, arrs, _BIG_B, BF16)
    dys, dr_p, dk_p, dv_p, dgate, g_rk, g_lnx_g, g_lnx_b = _post_bwd(y_f, y_b, zs2, kf, kb, gate, rw, dymix)
    (dr_f, dwf, dkf, dkk_f, dkaf, dv_f, dr_b, dwb, dkb, dkk_b, dkab, dv_b, recv_b) = _scan_bwd(
        zs, kk4, r4(dys), ops_f, hist_f, ops_b, hist_b, s_last, part_b)
    rh_b = part_b.shape[1] // 2
    mine_b = lax.dynamic_slice(part_b, (chip, core * rh_b, 0), (1, rh_b, 128))[0]
    g_big = join_halves(_add_rows([mine_b] + [recv_b[k] for k in range(7)], "sum_devices_b"), _BIG_B)
    cts = dict(dwf=f2(dwf), dwb=f2(dwb), dkf=f2(dkf), dkb=f2(dkb), dkk_f=f2(dkk_f), dkk_b=f2(dkk_b), dkaf=f2(dkaf), dkab=f2(dkab),
               dr_f=f2(dr_f), dr_b=f2(dr_b), dr_p=dr_p, dk_p=dk_p, dv_p=dv_p, dg=dgate, dv_f=f2(dv_f), dv_b=f2(dv_b))
    dzs, g_w0, g_w2, g_a0, g_a2, g_g2, g_kk, g_ka = _prep_bwd(zs2, rw, cts)
    dzr, g_mu_p, g_mu_n = _shift_bwd(dzs.reshape(b, t, RWKV_COLS), zr.reshape(b, t, RWKV_COLS), vec("shift_mu_prev"), vec("shift_mu_next"))
    dzr = dzr.reshape(m, RWKV_COLS)
    gfull["decay_w2_fwd"], gfull["decay_w2_bwd"] = g_w2[:64, :512], g_w2[64:, 512:]
    gfull["iclr_a2_fwd"], gfull["iclr_a2_bwd"] = g_a2[:64, :512], g_a2[64:, 512:]
    gfull["gate_g2"] = g_g2
    dn1 = _mm(dzr, win_r, "nn", "d_proj_in_rwkv", add=_mm(dzm, win_m, "nn", "d_proj_in_mla"))
    g_m = _mm(dzm, n1, "tn", "g_w_in_mla")
    g_r = _mm(dzr, n1, "tn", "g_w_in_rwkv")
    g_kr = g_m[1088:1120] + jnp.swapaxes(_rot_cols_t(jnp.swapaxes(g_m[1216:1248], 0, 1)), 0, 1)
    gfull["w_in"] = jnp.concatenate([g_r, g_m[:1024], g_kr], axis=0)
    dx, g_ln_mix = _rms_bwd(x2, vec("ln_mix_g"), dn1, "rms_mix_bwd", dres=dh1)

    part_a = cores_first(_BIG_A, "a")
    recv_a = _scatter_to_chips(part_a)
    mine_a = lax.dynamic_index_in_dim(part_a, chip, axis=0, keepdims=False)
    g_big.update(join_halves(_add_rows([mine_a, recv_a[0], recv_a[1], recv_a[2]], "sum_chips_a"), _BIG_A))
    small = {
        "ln_mix_g": g_ln_mix, "shift_mu_prev": g_mu_p, "shift_mu_next": g_mu_n, "decay_w0_fwd": g_w0[:, :512],
        "decay_w0_bwd": g_w0[:, 512:], "iclr_a0_fwd": g_a0[:, :512], "iclr_a0_bwd": g_a0[:, 512:], "k_k": g_kk, "k_a": g_ka,
        "r_k": g_rk, "ln_x_g": g_lnx_g, "ln_x_b": g_lnx_b, "q_norm_g": g_qn, "kv_norm_g": g_kvn, "mla_out_g": g_mla_out,
        "ln_ffn_g": g_ln_ffn, "ffn_conv_b": g_conv_b, "ln_final_g": g_ln_final,
        "_loss": jnp.pad(loss_tab[0, 0:1], (0, 127)),
    }
    g_small_buf = _allreduce_small(_pack_small(small))
    g_small = _unpack_small(g_small_buf, arrs)

    grads, deltas, new_m, new_v = {}, {}, {}, {}
    for name, _, _ in _BIG:
        grads[name] = g_big[name]
        deltas[name], new_m[name], new_v[name] = _adamw(
            arrs[name], g_big[name], arrs["m_" + name], arrs["v_" + name], "adamw_" + name)
    pk = lambda pre: _pack_small({**{n: arrs[pre + n] for n in _SMALL}, "_loss": jnp.zeros((128,), F32)})
    sd, sm, sv = _adamw(pk(""), g_small_buf, pk("m_"), pk("v_"), "adamw_small")
    sd, sm, sv = _unpack_small(sd, arrs), _unpack_small(sm, arrs), _unpack_small(sv, arrs)
    for n in _SMALL:
        grads[n], deltas[n], new_m[n], new_v[n] = g_small[n], sd[n], sm[n], sv[n]
    for group in (grads, deltas, new_m, new_v):
        group["w_in"] = jnp.swapaxes(group["w_in"], 1, 2)

    return (g_small["_loss"], dx.reshape(b, t, d), *[grads[n] for n in _WEIGHTS], *[deltas[n] for n in _WEIGHTS],
            *[new_m[n] for n in _WEIGHTS], *[new_v[n] for n in _WEIGHTS])
```
